```python
import math
import jax
import jax.numpy as jnp
from jax import lax
import numpy as np

D_MODEL = 1024
BATCH = 32
SEQ = 2048
DEPTH = 1

D_FF = 2816
DN_HEADS = 8
DN_HEAD_DIM = 64
DN_WIDTH = DN_HEADS * DN_HEAD_DIM
CONV_WIDTH = 4
CHUNK = 64
S5_GROUP_CH = 16
S5_GROUPS = 32
S5_WIDTH = S5_GROUPS * S5_GROUP_CH
S5_STATE = 64
N_MOD = 9
EPS = 1e-6
IN_WIDTH = 4 * DN_WIDTH + 2 * DN_HEADS + S5_WIDTH + 2 * D_MODEL

kernel_name = 'hybrid_deltanet_s5_macaron'


def rmsnorm(x, gain):
    x32 = x.astype(jnp.float32)
    y = x32 * lax.rsqrt(jnp.mean(x32 * x32, axis=-1, keepdims=True) + EPS)
    return (y * gain.astype(jnp.float32)).astype(x.dtype)


def modulate(x, shift, scale):
    return x * (1 + scale) + shift


def swiglu(x, w1, w3, w2):
    return (jax.nn.silu(x @ w1) * (x @ w3)) @ w2


def l2norm(t):
    return t * lax.rsqrt(jnp.sum(t * t, axis=-1, keepdims=True) + EPS)


def causal_depthwise_conv(x, w):
    return lax.conv_general_dilated(
        x, w[:, None, :].astype(x.dtype), window_strides=(1,),
        padding=[(CONV_WIDTH - 1, 0)], dimension_numbers=('NWC', 'WIO', 'NWC'),
        feature_group_count=x.shape[-1])


def split_combined(p):
    sizes = (DN_WIDTH, DN_WIDTH, DN_WIDTH, DN_WIDTH, DN_HEADS, DN_HEADS, S5_WIDTH, D_MODEL, D_MODEL)
    parts = []
    start = 0
    for size in sizes:
        parts.append(p[..., start:start + size])
        start += size
    return parts


def gated_deltanet(q, k, v, z, beta_logit, decay_logit, conv_w, a_log, dt_bias, g_onorm):
    f32 = jnp.float32
    dtype = q.dtype
    bsz, seq, _ = q.shape
    n_chunks = seq // CHUNK
    qkv = jax.nn.silu(causal_depthwise_conv(jnp.concatenate([q, k, v], axis=-1), conv_w)).astype(f32)
    q, k, v = jnp.split(qkv, 3, axis=-1)

    def to_chunks(t):
        return t.reshape(bsz, n_chunks, CHUNK, DN_HEADS, DN_HEAD_DIM).transpose(0, 3, 1, 2, 4)

    def to_chunks_h(t):
        return t.reshape(bsz, n_chunks, CHUNK, DN_HEADS).transpose(0, 3, 1, 2)

    q = l2norm(to_chunks(q)) * (DN_HEAD_DIM ** -0.5)
    k = l2norm(to_chunks(k))
    v = to_chunks(v)
    beta = to_chunks_h(jax.nn.sigmoid(beta_logit.astype(f32)))
    log_alpha = -jnp.exp(a_log.astype(f32)) * jax.nn.softplus(decay_logit.astype(f32) + dt_bias.astype(f32))
    g_cum = jnp.cumsum(to_chunks_h(log_alpha), axis=-1)
    causal = jnp.tril(jnp.ones((CHUNK, CHUNK), dtype=bool))
    strict = jnp.tril(jnp.ones((CHUNK, CHUNK), dtype=bool), k=-1)
    decay = jnp.exp(jnp.where(causal, g_cum[..., :, None] - g_cum[..., None, :], -jnp.inf))
    k_beta = k * beta[..., None]
    kk = jnp.where(strict, jnp.einsum('bhnik,bhnjk->bhnij', k_beta, k) * decay, 0.0)
    lhs = kk + jnp.eye(CHUNK, dtype=f32)
    rhs = jnp.concatenate([v * beta[..., None], k_beta * jnp.exp(g_cum)[..., None]], axis=-1)
    sol = lax.linalg.triangular_solve(lhs, rhs, left_side=True, lower=True, unit_diagonal=True)
    u_c, w_c = sol[..., :DN_HEAD_DIM], sol[..., DN_HEAD_DIM:]
    attn = jnp.einsum('bhnik,bhnjk->bhnij', q, k) * decay
    q_dec = q * jnp.exp(g_cum)[..., None]
    g_last = g_cum[..., -1]
    k_dec = k * jnp.exp(g_last[..., None] - g_cum)[..., None]

    def step(state, xs):
        q_i, k_i, u_i, w_i, a_i, gl_i = xs
        v_new = u_i - jnp.einsum('bhck,bhkv->bhcv', w_i, state)
        o_i = jnp.einsum('bhck,bhkv->bhcv', q_i, state) + jnp.einsum('bhij,bhjv->bhiv', a_i, v_new)
        state = state * jnp.exp(gl_i)[..., None, None] + jnp.einsum('bhck,bhcv->bhkv', k_i, v_new)
        return state, o_i

    xs = (q_dec, k_dec, u_c, w_c, attn, g_last)
    xs = tuple(jnp.moveaxis(t, 2, 0) for t in xs)
    state0 = jnp.zeros((bsz, DN_HEADS, DN_HEAD_DIM, DN_HEAD_DIM), f32)
    _, o = lax.scan(step, state0, xs)
    o = o.transpose(1, 0, 3, 2, 4).reshape(bsz, seq, DN_HEADS, DN_HEAD_DIM)
    gate = jax.nn.silu(z.astype(f32)).reshape(bsz, seq, DN_HEADS, DN_HEAD_DIM)
    o = o * lax.rsqrt(jnp.mean(o * o, axis=-1, keepdims=True) + EPS) * g_onorm.astype(f32) * gate
    return o.reshape(bsz, seq, DN_WIDTH).astype(dtype)


def s5_ssm(u_in, lam_re, lam_im, log_step, b_re, b_im, c_re, c_im, d_skip, w_glu, b_glu):
    f32 = jnp.float32
    dtype = u_in.dtype
    bsz, seq, _ = u_in.shape
    u = u_in.astype(f32).reshape(bsz, seq, S5_GROUPS, S5_GROUP_CH)
    lam_re = jnp.minimum(lam_re.astype(f32), -1e-4)
    lam_im = lam_im.astype(f32)
    step = jnp.exp(log_step.astype(f32))[:, None]
    mag = jnp.exp(lam_re * step)
    ang = lam_im * step
    lb_re = mag * jnp.cos(ang)
    lb_im = mag * jnp.sin(ang)
    den = lam_re * lam_re + lam_im * lam_im
    coef_re = ((lb_re - 1.0) * lam_re + lb_im * lam_im) / den
    coef_im = (lb_im * lam_re - (lb_re - 1.0) * lam_im) / den
    b_re = b_re.astype(f32)
    b_im = b_im.astype(f32)
    bb_re = coef_re[..., None] * b_re - coef_im[..., None] * b_im
    bb_im = coef_re[..., None] * b_im + coef_im[..., None] * b_re
    bu_re = jnp.einsum('bsgc,gpc->bsgp', u, bb_re)
    bu_im = jnp.einsum('bsgc,gpc->bsgp', u, bb_im)
    a_re = jnp.broadcast_to(lb_re, (1, seq, S5_GROUPS, S5_STATE))
    a_im = jnp.broadcast_to(lb_im, (1, seq, S5_GROUPS, S5_STATE))

    def combine(e1, e2):
        a1r, a1i, b1r, b1i = e1
        a2r, a2i, b2r, b2i = e2
        return (a2r * a1r - a2i * a1i,
                a2r * a1i + a2i * a1r,
                a2r * b1r - a2i * b1i + b2r,
                a2r * b1i + a2i * b1r + b2i)

    _, _, x_re, x_im = lax.associative_scan(combine, (a_re, a_im, bu_re, bu_im), axis=1)
    y = (jnp.einsum('bsgp,gcp->bsgc', x_re, c_re.astype(f32))
         - jnp.einsum('bsgp,gcp->bsgc', x_im, c_im.astype(f32))
         + d_skip.astype(f32).reshape(S5_GROUPS, S5_GROUP_CH) * u)
    y = jax.nn.gelu(y.reshape(bsz, seq, S5_WIDTH))
    y = y * jax.nn.sigmoid(y @ w_glu.astype(f32) + b_glu.astype(f32))
    return y.astype(dtype)


def hybrid_layer(h, c, w_ada, b_ada, g_ffn1, w1_ffn1, w3_ffn1, w2_ffn1, g_mix, w_in, conv_qkv,
                 a_log, dt_bias, g_onorm, lam_re, lam_im, log_step, b_re, b_im, c_re, c_im,
                 d_skip, w_glu, b_glu, w_proj_a, w_proj_b, w_out, g_ffn2, w1_ffn2, w3_ffn2, w2_ffn2):
    mod = jax.nn.silu(c) @ w_ada + b_ada
    sh1, sc1, gt1, sh2, sc2, gt2, sh3, sc3, gt3 = [m[:, None, :] for m in jnp.split(mod, N_MOD, axis=-1)]
    h = h + 0.5 * gt1 * swiglu(modulate(rmsnorm(h, g_ffn1), sh1, sc1), w1_ffn1, w3_ffn1, w2_ffn1)
    u = modulate(rmsnorm(h, g_mix), sh2, sc2)
    q, k, v, z, beta_logit, decay_logit, s5_in, gate_a, gate_b = split_combined(u @ w_in)
    y_a = gated_deltanet(q, k, v, z, beta_logit, decay_logit, conv_qkv, a_log, dt_bias, g_onorm) @ w_proj_a
    y_b = s5_ssm(s5_in, lam_re, lam_im, log_step, b_re, b_im, c_re, c_im, d_skip, w_glu, b_glu) @ w_proj_b
    merged = jax.nn.sigmoid(gate_a) * y_a + jax.nn.sigmoid(gate_b) * y_b
    h = h + gt2 * (merged @ w_out)
    h = h + 0.5 * gt3 * swiglu(modulate(rmsnorm(h, g_ffn2), sh3, sc3), w1_ffn2, w3_ffn2, w2_ffn2)
    return h


def _fwd_setup_inputs(seed: int = 0) -> dict:
    key = jax.random.key(seed)
    ks = jax.random.split(key, 40)
    f32 = jnp.float32
    L = DEPTH

    def nrm(k, shape, scale):
        return jax.random.normal(k, shape, f32) * scale

    def log_uniform(k, shape, lo, hi):
        return jax.random.uniform(k, shape, f32, math.log(lo), math.log(hi))

    dt = jnp.exp(log_uniform(ks[12], (L, DN_HEADS), 1e-3, 1e-1))
    n_idx = jnp.arange(S5_STATE, dtype=f32)
    return {
        'x': nrm(ks[0], (BATCH, SEQ, D_MODEL), 1.0),
        'c': nrm(ks[1], (BATCH, D_MODEL), 1.0),
        'w_ada': nrm(ks[2], (L, D_MODEL, N_MOD * D_MODEL), 0.5 * D_MODEL ** -0.5),
        'b_ada': nrm(ks[3], (L, N_MOD * D_MODEL), 0.02),
        'g_ffn1': 1.0 + nrm(ks[4], (L, D_MODEL), 0.02),
        'w1_ffn1': nrm(ks[5], (L, D_MODEL, D_FF), D_MODEL ** -0.5),
        'w3_ffn1': nrm(ks[6], (L, D_MODEL, D_FF), D_MODEL ** -0.5),
        'w2_ffn1': nrm(ks[7], (L, D_FF, D_MODEL), D_FF ** -0.5),
        'g_mix': 1.0 + nrm(ks[8], (L, D_MODEL), 0.02),
        'w_in': nrm(ks[9], (L, D_MODEL, IN_WIDTH), D_MODEL ** -0.5),
        'conv_qkv': nrm(ks[10], (L, CONV_WIDTH, 3 * DN_WIDTH), CONV_WIDTH ** -0.5),
        'a_log': jnp.log(jax.random.uniform(ks[11], (L, DN_HEADS), f32, 1.0, 16.0)),
        'dt_bias': dt + jnp.log(-jnp.expm1(-dt)),
        'g_onorm': 1.0 + nrm(ks[13], (L, DN_HEAD_DIM), 0.02),
        'lam_re': -0.5 + nrm(ks[14], (L, S5_GROUPS, S5_STATE), 0.01),
        'lam_im': math.pi * n_idx + nrm(ks[15], (L, S5_GROUPS, S5_STATE), 0.01),
        'log_step': log_uniform(ks[16], (L, S5_GROUPS), 1e-3, 1e-1),
        'b_re': nrm(ks[17], (L, S5_GROUPS, S5_STATE, S5_GROUP_CH), (2 * S5_GROUP_CH) ** -0.5),
        'b_im': nrm(ks[18], (L, S5_GROUPS, S5_STATE, S5_GROUP_CH), (2 * S5_GROUP_CH) ** -0.5),
        'c_re': nrm(ks[19], (L, S5_GROUPS, S5_GROUP_CH, S5_STATE), S5_STATE ** -0.5),
        'c_im': nrm(ks[20], (L, S5_GROUPS, S5_GROUP_CH, S5_STATE), S5_STATE ** -0.5),
        'd_skip': nrm(ks[21], (L, S5_WIDTH), 1.0),
        'w_glu': nrm(ks[22], (L, S5_WIDTH, S5_WIDTH), S5_WIDTH ** -0.5),
        'b_glu': nrm(ks[23], (L, S5_WIDTH), 0.02),
        'w_proj_a': nrm(ks[24], (L, DN_WIDTH, D_MODEL), DN_WIDTH ** -0.5),
        'w_proj_b': nrm(ks[25], (L, S5_WIDTH, D_MODEL), S5_WIDTH ** -0.5),
        'w_out': nrm(ks[26], (L, D_MODEL, D_MODEL), D_MODEL ** -0.5),
        'g_ffn2': 1.0 + nrm(ks[27], (L, D_MODEL), 0.02),
        'w1_ffn2': nrm(ks[28], (L, D_MODEL, D_FF), D_MODEL ** -0.5),
        'w3_ffn2': nrm(ks[29], (L, D_MODEL, D_FF), D_MODEL ** -0.5),
        'w2_ffn2': nrm(ks[30], (L, D_FF, D_MODEL), D_FF ** -0.5),
        'g_final': 1.0 + nrm(ks[31], (D_MODEL,), 0.02),
    }


def _fwd_reference(x, c, w_ada, b_ada, g_ffn1, w1_ffn1, w3_ffn1, w2_ffn1, g_mix, w_in, conv_qkv,
              a_log, dt_bias, g_onorm, lam_re, lam_im, log_step, b_re, b_im, c_re, c_im,
              d_skip, w_glu, b_glu, w_proj_a, w_proj_b, w_out, g_ffn2, w1_ffn2, w3_ffn2, w2_ffn2,
              g_final):
    h = x
    for layer in range(DEPTH):
        h = hybrid_layer(
            h, c, w_ada[layer], b_ada[layer], g_ffn1[layer], w1_ffn1[layer], w3_ffn1[layer],
            w2_ffn1[layer], g_mix[layer], w_in[layer], conv_qkv[layer], a_log[layer],
            dt_bias[layer], g_onorm[layer], lam_re[layer], lam_im[layer], log_step[layer],
            b_re[layer], b_im[layer], c_re[layer], c_im[layer], d_skip[layer], w_glu[layer],
            b_glu[layer], w_proj_a[layer], w_proj_b[layer], w_out[layer], g_ffn2[layer],
            w1_ffn2[layer], w3_ffn2[layer], w2_ffn2[layer])
    return rmsnorm(h, g_final)


import jax as _jax
import jax.numpy as _jnp

TWIN_FORMAT = 'train_step'
FWD_PARAMS = ['x', 'c', 'w_ada', 'b_ada', 'g_ffn1', 'w1_ffn1', 'w3_ffn1', 'w2_ffn1', 'g_mix', 'w_in', 'conv_qkv', 'a_log', 'dt_bias', 'g_onorm', 'lam_re', 'lam_im', 'log_step', 'b_re', 'b_im', 'c_re', 'c_im', 'd_skip', 'w_glu', 'b_glu', 'w_proj_a', 'w_proj_b', 'w_out', 'g_ffn2', 'w1_ffn2', 'w3_ffn2', 'w2_ffn2', 'g_final']
TWIN_WEIGHTS = ['w_ada', 'b_ada', 'g_ffn1', 'w1_ffn1', 'w3_ffn1', 'w2_ffn1', 'g_mix', 'w_in', 'conv_qkv', 'a_log', 'dt_bias', 'g_onorm', 'lam_re', 'lam_im', 'log_step', 'b_re', 'b_im', 'c_re', 'c_im', 'd_skip', 'w_glu', 'b_glu', 'w_proj_a', 'w_proj_b', 'w_out', 'g_ffn2', 'w1_ffn2', 'w3_ffn2', 'w2_ffn2', 'g_final']
TWIN_DIFF_INPUT = 'x'
TWIN_INPUTS = ['x', 'c', 'w_ada', 'b_ada', 'g_ffn1', 'w1_ffn1', 'w3_ffn1', 'w2_ffn1', 'g_mix', 'w_in', 'conv_qkv', 'a_log', 'dt_bias', 'g_onorm', 'lam_re', 'lam_im', 'log_step', 'b_re', 'b_im', 'c_re', 'c_im', 'd_skip', 'w_glu', 'b_glu', 'w_proj_a', 'w_proj_b', 'w_out', 'g_ffn2', 'w1_ffn2', 'w3_ffn2', 'w2_ffn2', 'g_final', 'loss_target', 'm_w_ada', 'm_b_ada', 'm_g_ffn1', 'm_w1_ffn1', 'm_w3_ffn1', 'm_w2_ffn1', 'm_g_mix', 'm_w_in', 'm_conv_qkv', 'm_a_log', 'm_dt_bias', 'm_g_onorm', 'm_lam_re', 'm_lam_im', 'm_log_step', 'm_b_re', 'm_b_im', 'm_c_re', 'm_c_im', 'm_d_skip', 'm_w_glu', 'm_b_glu', 'm_w_proj_a', 'm_w_proj_b', 'm_w_out', 'm_g_ffn2', 'm_w1_ffn2', 'm_w3_ffn2', 'm_w2_ffn2', 'm_g_final', 'v_w_ada', 'v_b_ada', 'v_g_ffn1', 'v_w1_ffn1', 'v_w3_ffn1', 'v_w2_ffn1', 'v_g_mix', 'v_w_in', 'v_conv_qkv', 'v_a_log', 'v_dt_bias', 'v_g_onorm', 'v_lam_re', 'v_lam_im', 'v_log_step', 'v_b_re', 'v_b_im', 'v_c_re', 'v_c_im', 'v_d_skip', 'v_w_glu', 'v_b_glu', 'v_w_proj_a', 'v_w_proj_b', 'v_w_out', 'v_g_ffn2', 'v_w1_ffn2', 'v_w3_ffn2', 'v_w2_ffn2', 'v_g_final']
TWIN_OUTPUTS = ['loss', 'grad_x', 'grad_w_ada', 'grad_b_ada', 'grad_g_ffn1', 'grad_w1_ffn1', 'grad_w3_ffn1', 'grad_w2_ffn1', 'grad_g_mix', 'grad_w_in', 'grad_conv_qkv', 'grad_a_log', 'grad_dt_bias', 'grad_g_onorm', 'grad_lam_re', 'grad_lam_im', 'grad_log_step', 'grad_b_re', 'grad_b_im', 'grad_c_re', 'grad_c_im', 'grad_d_skip', 'grad_w_glu', 'grad_b_glu', 'grad_w_proj_a', 'grad_w_proj_b', 'grad_w_out', 'grad_g_ffn2', 'grad_w1_ffn2', 'grad_w3_ffn2', 'grad_w2_ffn2', 'grad_g_final', 'delta_w_ada', 'delta_b_ada', 'delta_g_ffn1', 'delta_w1_ffn1', 'delta_w3_ffn1', 'delta_w2_ffn1', 'delta_g_mix', 'delta_w_in', 'delta_conv_qkv', 'delta_a_log', 'delta_dt_bias', 'delta_g_onorm', 'delta_lam_re', 'delta_lam_im', 'delta_log_step', 'delta_b_re', 'delta_b_im', 'delta_c_re', 'delta_c_im', 'delta_d_skip', 'delta_w_glu', 'delta_b_glu', 'delta_w_proj_a', 'delta_w_proj_b', 'delta_w_out', 'delta_g_ffn2', 'delta_w1_ffn2', 'delta_w3_ffn2', 'delta_w2_ffn2', 'delta_g_final', 'new_m_w_ada', 'new_m_b_ada', 'new_m_g_ffn1', 'new_m_w1_ffn1', 'new_m_w3_ffn1', 'new_m_w2_ffn1', 'new_m_g_mix', 'new_m_w_in', 'new_m_conv_qkv', 'new_m_a_log', 'new_m_dt_bias', 'new_m_g_onorm', 'new_m_lam_re', 'new_m_lam_im', 'new_m_log_step', 'new_m_b_re', 'new_m_b_im', 'new_m_c_re', 'new_m_c_im', 'new_m_d_skip', 'new_m_w_glu', 'new_m_b_glu', 'new_m_w_proj_a', 'new_m_w_proj_b', 'new_m_w_out', 'new_m_g_ffn2', 'new_m_w1_ffn2', 'new_m_w3_ffn2', 'new_m_w2_ffn2', 'new_m_g_final', 'new_v_w_ada', 'new_v_b_ada', 'new_v_g_ffn1', 'new_v_w1_ffn1', 'new_v_w3_ffn1', 'new_v_w2_ffn1', 'new_v_g_mix', 'new_v_w_in', 'new_v_conv_qkv', 'new_v_a_log', 'new_v_dt_bias', 'new_v_g_onorm', 'new_v_lam_re', 'new_v_lam_im', 'new_v_log_step', 'new_v_b_re', 'new_v_b_im', 'new_v_c_re', 'new_v_c_im', 'new_v_d_skip', 'new_v_w_glu', 'new_v_b_glu', 'new_v_w_proj_a', 'new_v_w_proj_b', 'new_v_w_out', 'new_v_g_ffn2', 'new_v_w1_ffn2', 'new_v_w3_ffn2', 'new_v_w2_ffn2', 'new_v_g_final']
TWIN_LEAF_KINDS = {'loss': 'loss', 'grad_x': 'grad_x', 'grad_w_ada': 'grad_w', 'grad_b_ada': 'grad_w', 'grad_g_ffn1': 'grad_w', 'grad_w1_ffn1': 'grad_w', 'grad_w3_ffn1': 'grad_w', 'grad_w2_ffn1': 'grad_w', 'grad_g_mix': 'grad_w', 'grad_w_in': 'grad_w', 'grad_conv_qkv': 'grad_w', 'grad_a_log': 'grad_w', 'grad_dt_bias': 'grad_w', 'grad_g_onorm': 'grad_w', 'grad_lam_re': 'grad_w', 'grad_lam_im': 'grad_w', 'grad_log_step': 'grad_w', 'grad_b_re': 'grad_w', 'grad_b_im': 'grad_w', 'grad_c_re': 'grad_w', 'grad_c_im': 'grad_w', 'grad_d_skip': 'grad_w', 'grad_w_glu': 'grad_w', 'grad_b_glu': 'grad_w', 'grad_w_proj_a': 'grad_w', 'grad_w_proj_b': 'grad_w', 'grad_w_out': 'grad_w', 'grad_g_ffn2': 'grad_w', 'grad_w1_ffn2': 'grad_w', 'grad_w3_ffn2': 'grad_w', 'grad_w2_ffn2': 'grad_w', 'grad_g_final': 'grad_w', 'delta_w_ada': 'delta_w', 'delta_b_ada': 'delta_w', 'delta_g_ffn1': 'delta_w', 'delta_w1_ffn1': 'delta_w', 'delta_w3_ffn1': 'delta_w', 'delta_w2_ffn1': 'delta_w', 'delta_g_mix': 'delta_w', 'delta_w_in': 'delta_w', 'delta_conv_qkv': 'delta_w', 'delta_a_log': 'delta_w', 'delta_dt_bias': 'delta_w', 'delta_g_onorm': 'delta_w', 'delta_lam_re': 'delta_w', 'delta_lam_im': 'delta_w', 'delta_log_step': 'delta_w', 'delta_b_re': 'delta_w', 'delta_b_im': 'delta_w', 'delta_c_re': 'delta_w', 'delta_c_im': 'delta_w', 'delta_d_skip': 'delta_w', 'delta_w_glu': 'delta_w', 'delta_b_glu': 'delta_w', 'delta_w_proj_a': 'delta_w', 'delta_w_proj_b': 'delta_w', 'delta_w_out': 'delta_w', 'delta_g_ffn2': 'delta_w', 'delta_w1_ffn2': 'delta_w', 'delta_w3_ffn2': 'delta_w', 'delta_w2_ffn2': 'delta_w', 'delta_g_final': 'delta_w', 'new_m_w_ada': 'new_m', 'new_m_b_ada': 'new_m', 'new_m_g_ffn1': 'new_m', 'new_m_w1_ffn1': 'new_m', 'new_m_w3_ffn1': 'new_m', 'new_m_w2_ffn1': 'new_m', 'new_m_g_mix': 'new_m', 'new_m_w_in': 'new_m', 'new_m_conv_qkv': 'new_m', 'new_m_a_log': 'new_m', 'new_m_dt_bias': 'new_m', 'new_m_g_onorm': 'new_m', 'new_m_lam_re': 'new_m', 'new_m_lam_im': 'new_m', 'new_m_log_step': 'new_m', 'new_m_b_re': 'new_m', 'new_m_b_im': 'new_m', 'new_m_c_re': 'new_m', 'new_m_c_im': 'new_m', 'new_m_d_skip': 'new_m', 'new_m_w_glu': 'new_m', 'new_m_b_glu': 'new_m', 'new_m_w_proj_a': 'new_m', 'new_m_w_proj_b': 'new_m', 'new_m_w_out': 'new_m', 'new_m_g_ffn2': 'new_m', 'new_m_w1_ffn2': 'new_m', 'new_m_w3_ffn2': 'new_m', 'new_m_w2_ffn2': 'new_m', 'new_m_g_final': 'new_m', 'new_v_w_ada': 'new_v', 'new_v_b_ada': 'new_v', 'new_v_g_ffn1': 'new_v', 'new_v_w1_ffn1': 'new_v', 'new_v_w3_ffn1': 'new_v', 'new_v_w2_ffn1': 'new_v', 'new_v_g_mix': 'new_v', 'new_v_w_in': 'new_v', 'new_v_conv_qkv': 'new_v', 'new_v_a_log': 'new_v', 'new_v_dt_bias': 'new_v', 'new_v_g_onorm': 'new_v', 'new_v_lam_re': 'new_v', 'new_v_lam_im': 'new_v', 'new_v_log_step': 'new_v', 'new_v_b_re': 'new_v', 'new_v_b_im': 'new_v', 'new_v_c_re': 'new_v', 'new_v_c_im': 'new_v', 'new_v_d_skip': 'new_v', 'new_v_w_glu': 'new_v', 'new_v_b_glu': 'new_v', 'new_v_w_proj_a': 'new_v', 'new_v_w_proj_b': 'new_v', 'new_v_w_out': 'new_v', 'new_v_g_ffn2': 'new_v', 'new_v_w1_ffn2': 'new_v', 'new_v_w3_ffn2': 'new_v', 'new_v_w2_ffn2': 'new_v', 'new_v_g_final': 'new_v'}


def _forward(args):
    return _fwd_reference(*[args[k] for k in FWD_PARAMS])


def _output_shape():
    out = _jax.eval_shape(lambda: _forward(_fwd_setup_inputs(0)))
    return out.shape, out.dtype

N_MICROBATCH = 1
ADAM_LR = 0.001
ADAM_B1 = 0.9
ADAM_B2 = 0.999
ADAM_EPS = 1e-08
ADAM_WD = 0.01
ADAM_STEP = 10
PER_EXAMPLE_BATCH_AXIS = {'x': 0, 'c': 0, 'loss_target': 0}
SHARED_INPUTS = []
_WEIGHT_DTYPES = {'w_ada': _jnp.float32, 'b_ada': _jnp.float32, 'g_ffn1': _jnp.float32, 'w1_ffn1': _jnp.float32, 'w3_ffn1': _jnp.float32, 'w2_ffn1': _jnp.float32, 'g_mix': _jnp.float32, 'w_in': _jnp.float32, 'conv_qkv': _jnp.float32, 'a_log': _jnp.float32, 'dt_bias': _jnp.float32, 'g_onorm': _jnp.float32, 'lam_re': _jnp.float32, 'lam_im': _jnp.float32, 'log_step': _jnp.float32, 'b_re': _jnp.float32, 'b_im': _jnp.float32, 'c_re': _jnp.float32, 'c_im': _jnp.float32, 'd_skip': _jnp.float32, 'w_glu': _jnp.float32, 'b_glu': _jnp.float32, 'w_proj_a': _jnp.float32, 'w_proj_b': _jnp.float32, 'w_out': _jnp.float32, 'g_ffn2': _jnp.float32, 'w1_ffn2': _jnp.float32, 'w3_ffn2': _jnp.float32, 'w2_ffn2': _jnp.float32, 'g_final': _jnp.float32}
MOMENT_SCALE = {'w_ada': 4.014851e-02, 'b_ada': 6.541862e-02, 'g_ffn1': 4.027564e-02, 'w1_ffn1': 1.729995e-02, 'w3_ffn1': 1.673267e-02, 'w2_ffn1': 2.776548e-02, 'g_mix': 5.640719e-02, 'w_in': 2.731324e-02, 'conv_qkv': 3.722900e-02, 'a_log': 2.107022e-01, 'dt_bias': 2.017889e-01, 'g_onorm': 1.718658e-01, 'lam_re': 1.950056e-03, 'lam_im': 3.204773e-03, 'log_step': 8.389733e-01, 'b_re': 1.483608e-03, 'b_im': 1.660033e-03, 'c_re': 1.995701e-03, 'c_im': 2.073142e-03, 'd_skip': 2.654376e-02, 'w_glu': 7.797633e-03, 'b_glu': 1.174565e-02, 'w_proj_a': 2.803724e-02, 'w_proj_b': 1.595062e-02, 'w_out': 3.301424e-02, 'g_ffn2': 3.837919e-02, 'w1_ffn2': 1.689890e-02, 'w3_ffn2': 1.639690e-02, 'w2_ffn2': 2.724278e-02, 'g_final': 6.389527e+01}


def _to_microbatches(a, axis):
    t = _jnp.moveaxis(a, axis, 0)
    t = t.reshape((N_MICROBATCH, t.shape[0] // N_MICROBATCH) + t.shape[1:])
    return _jnp.moveaxis(t, 1, axis + 1)


def setup_inputs(seed: int = 0) -> dict:
    inp = _fwd_setup_inputs(seed)
    key = _jax.random.fold_in(_jax.random.key(seed), 7919)
    shape, _ = _output_shape()
    out = dict(inp)
    out["loss_target"] = _jax.random.normal(_jax.random.fold_in(key, 0), shape, _jnp.float32)
    for i, name in enumerate(TWIN_WEIGHTS):
        w = inp[name].astype(_jnp.float32)
        if MOMENT_SCALE is None:
            s = _jnp.sqrt(_jnp.mean(_jnp.square(w)) + 1e-30)
        else:
            s = MOMENT_SCALE[name]
        km, kv = _jax.random.split(_jax.random.fold_in(key, i + 1))
        out[name] = w
        out["m_" + name] = s * _jax.random.normal(km, w.shape, _jnp.float32)
        out["v_" + name] = (s * s) * _jax.random.uniform(kv, w.shape, _jnp.float32, 0.5, 1.5)
    if N_MICROBATCH > 1:
        for name, axis in PER_EXAMPLE_BATCH_AXIS.items():
            out[name] = _to_microbatches(out[name], axis)
    return {'x': out['x'], 'c': out['c'], 'w_ada': out['w_ada'], 'b_ada': out['b_ada'], 'g_ffn1': out['g_ffn1'], 'w1_ffn1': out['w1_ffn1'], 'w3_ffn1': out['w3_ffn1'], 'w2_ffn1': out['w2_ffn1'], 'g_mix': out['g_mix'], 'w_in': out['w_in'], 'conv_qkv': out['conv_qkv'], 'a_log': out['a_log'], 'dt_bias': out['dt_bias'], 'g_onorm': out['g_onorm'], 'lam_re': out['lam_re'], 'lam_im': out['lam_im'], 'log_step': out['log_step'], 'b_re': out['b_re'], 'b_im': out['b_im'], 'c_re': out['c_re'], 'c_im': out['c_im'], 'd_skip': out['d_skip'], 'w_glu': out['w_glu'], 'b_glu': out['b_glu'], 'w_proj_a': out['w_proj_a'], 'w_proj_b': out['w_proj_b'], 'w_out': out['w_out'], 'g_ffn2': out['g_ffn2'], 'w1_ffn2': out['w1_ffn2'], 'w3_ffn2': out['w3_ffn2'], 'w2_ffn2': out['w2_ffn2'], 'g_final': out['g_final'], 'loss_target': out['loss_target'], 'm_w_ada': out['m_w_ada'], 'm_b_ada': out['m_b_ada'], 'm_g_ffn1': out['m_g_ffn1'], 'm_w1_ffn1': out['m_w1_ffn1'], 'm_w3_ffn1': out['m_w3_ffn1'], 'm_w2_ffn1': out['m_w2_ffn1'], 'm_g_mix': out['m_g_mix'], 'm_w_in': out['m_w_in'], 'm_conv_qkv': out['m_conv_qkv'], 'm_a_log': out['m_a_log'], 'm_dt_bias': out['m_dt_bias'], 'm_g_onorm': out['m_g_onorm'], 'm_lam_re': out['m_lam_re'], 'm_lam_im': out['m_lam_im'], 'm_log_step': out['m_log_step'], 'm_b_re': out['m_b_re'], 'm_b_im': out['m_b_im'], 'm_c_re': out['m_c_re'], 'm_c_im': out['m_c_im'], 'm_d_skip': out['m_d_skip'], 'm_w_glu': out['m_w_glu'], 'm_b_glu': out['m_b_glu'], 'm_w_proj_a': out['m_w_proj_a'], 'm_w_proj_b': out['m_w_proj_b'], 'm_w_out': out['m_w_out'], 'm_g_ffn2': out['m_g_ffn2'], 'm_w1_ffn2': out['m_w1_ffn2'], 'm_w3_ffn2': out['m_w3_ffn2'], 'm_w2_ffn2': out['m_w2_ffn2'], 'm_g_final': out['m_g_final'], 'v_w_ada': out['v_w_ada'], 'v_b_ada': out['v_b_ada'], 'v_g_ffn1': out['v_g_ffn1'], 'v_w1_ffn1': out['v_w1_ffn1'], 'v_w3_ffn1': out['v_w3_ffn1'], 'v_w2_ffn1': out['v_w2_ffn1'], 'v_g_mix': out['v_g_mix'], 'v_w_in': out['v_w_in'], 'v_conv_qkv': out['v_conv_qkv'], 'v_a_log': out['v_a_log'], 'v_dt_bias': out['v_dt_bias'], 'v_g_onorm': out['v_g_onorm'], 'v_lam_re': out['v_lam_re'], 'v_lam_im': out['v_lam_im'], 'v_log_step': out['v_log_step'], 'v_b_re': out['v_b_re'], 'v_b_im': out['v_b_im'], 'v_c_re': out['v_c_re'], 'v_c_im': out['v_c_im'], 'v_d_skip': out['v_d_skip'], 'v_w_glu': out['v_w_glu'], 'v_b_glu': out['v_b_glu'], 'v_w_proj_a': out['v_w_proj_a'], 'v_w_proj_b': out['v_w_proj_b'], 'v_w_out': out['v_w_out'], 'v_g_ffn2': out['v_g_ffn2'], 'v_w1_ffn2': out['v_w1_ffn2'], 'v_w3_ffn2': out['v_w3_ffn2'], 'v_w2_ffn2': out['v_w2_ffn2'], 'v_g_final': out['v_g_final']}


def _loss(weights, diff, rest, loss_target):
    with _jax.named_scope("forward"):
        args = {**rest, TWIN_DIFF_INPUT: diff, **{k: w.astype(_WEIGHT_DTYPES[k]) for k, w in weights.items()}}
        y = _forward(args)
    with _jax.named_scope("loss_head"):
        err = _jnp.square(y.astype(_jnp.float32) - loss_target)
        return 0.5 * _jnp.sum(_jnp.mean(err, axis=-1)) if err.ndim else 0.5 * err


def _adamw(w, g, m, v):
    m = ADAM_B1 * m + (1.0 - ADAM_B1) * g
    v = ADAM_B2 * v + (1.0 - ADAM_B2) * _jnp.square(g)
    m_hat = m / (1.0 - ADAM_B1 ** ADAM_STEP)
    v_hat = v / (1.0 - ADAM_B2 ** ADAM_STEP)
    delta = -ADAM_LR * (m_hat / (_jnp.sqrt(v_hat) + ADAM_EPS) + ADAM_WD * w)
    return delta, m, v


def reference(x, c, w_ada, b_ada, g_ffn1, w1_ffn1, w3_ffn1, w2_ffn1, g_mix, w_in, conv_qkv, a_log, dt_bias, g_onorm, lam_re, lam_im, log_step, b_re, b_im, c_re, c_im, d_skip, w_glu, b_glu, w_proj_a, w_proj_b, w_out, g_ffn2, w1_ffn2, w3_ffn2, w2_ffn2, g_final, loss_target, m_w_ada, m_b_ada, m_g_ffn1, m_w1_ffn1, m_w3_ffn1, m_w2_ffn1, m_g_mix, m_w_in, m_conv_qkv, m_a_log, m_dt_bias, m_g_onorm, m_lam_re, m_lam_im, m_log_step, m_b_re, m_b_im, m_c_re, m_c_im, m_d_skip, m_w_glu, m_b_glu, m_w_proj_a, m_w_proj_b, m_w_out, m_g_ffn2, m_w1_ffn2, m_w3_ffn2, m_w2_ffn2, m_g_final, v_w_ada, v_b_ada, v_g_ffn1, v_w1_ffn1, v_w3_ffn1, v_w2_ffn1, v_g_mix, v_w_in, v_conv_qkv, v_a_log, v_dt_bias, v_g_onorm, v_lam_re, v_lam_im, v_log_step, v_b_re, v_b_im, v_c_re, v_c_im, v_d_skip, v_w_glu, v_b_glu, v_w_proj_a, v_w_proj_b, v_w_out, v_g_ffn2, v_w1_ffn2, v_w3_ffn2, v_w2_ffn2, v_g_final):
    given = dict(x=x, c=c, w_ada=w_ada, b_ada=b_ada, g_ffn1=g_ffn1, w1_ffn1=w1_ffn1, w3_ffn1=w3_ffn1, w2_ffn1=w2_ffn1, g_mix=g_mix, w_in=w_in, conv_qkv=conv_qkv, a_log=a_log, dt_bias=dt_bias, g_onorm=g_onorm, lam_re=lam_re, lam_im=lam_im, log_step=log_step, b_re=b_re, b_im=b_im, c_re=c_re, c_im=c_im, d_skip=d_skip, w_glu=w_glu, b_glu=b_glu, w_proj_a=w_proj_a, w_proj_b=w_proj_b, w_out=w_out, g_ffn2=g_ffn2, w1_ffn2=w1_ffn2, w3_ffn2=w3_ffn2, w2_ffn2=w2_ffn2, g_final=g_final, loss_target=loss_target, m_w_ada=m_w_ada, m_b_ada=m_b_ada, m_g_ffn1=m_g_ffn1, m_w1_ffn1=m_w1_ffn1, m_w3_ffn1=m_w3_ffn1, m_w2_ffn1=m_w2_ffn1, m_g_mix=m_g_mix, m_w_in=m_w_in, m_conv_qkv=m_conv_qkv, m_a_log=m_a_log, m_dt_bias=m_dt_bias, m_g_onorm=m_g_onorm, m_lam_re=m_lam_re, m_lam_im=m_lam_im, m_log_step=m_log_step, m_b_re=m_b_re, m_b_im=m_b_im, m_c_re=m_c_re, m_c_im=m_c_im, m_d_skip=m_d_skip, m_w_glu=m_w_glu, m_b_glu=m_b_glu, m_w_proj_a=m_w_proj_a, m_w_proj_b=m_w_proj_b, m_w_out=m_w_out, m_g_ffn2=m_g_ffn2, m_w1_ffn2=m_w1_ffn2, m_w3_ffn2=m_w3_ffn2, m_w2_ffn2=m_w2_ffn2, m_g_final=m_g_final, v_w_ada=v_w_ada, v_b_ada=v_b_ada, v_g_ffn1=v_g_ffn1, v_w1_ffn1=v_w1_ffn1, v_w3_ffn1=v_w3_ffn1, v_w2_ffn1=v_w2_ffn1, v_g_mix=v_g_mix, v_w_in=v_w_in, v_conv_qkv=v_conv_qkv, v_a_log=v_a_log, v_dt_bias=v_dt_bias, v_g_onorm=v_g_onorm, v_lam_re=v_lam_re, v_lam_im=v_lam_im, v_log_step=v_log_step, v_b_re=v_b_re, v_b_im=v_b_im, v_c_re=v_c_re, v_c_im=v_c_im, v_d_skip=v_d_skip, v_w_glu=v_w_glu, v_b_glu=v_b_glu, v_w_proj_a=v_w_proj_a, v_w_proj_b=v_w_proj_b, v_w_out=v_w_out, v_g_ffn2=v_g_ffn2, v_w1_ffn2=v_w1_ffn2, v_w3_ffn2=v_w3_ffn2, v_w2_ffn2=v_w2_ffn2, v_g_final=v_g_final)
    weights = {n: given[n] for n in TWIN_WEIGHTS}
    shared = {n: given[n] for n in SHARED_INPUTS}
    per_example = {n: given[n] for n in ['x', 'c']}
    grad_fn = _jax.value_and_grad(_loss, argnums=(0, 1))

    def one_microbatch(ex, loss_target):
        ex = dict(ex)
        diff = ex.pop(TWIN_DIFF_INPUT)
        return grad_fn(weights, diff, {**shared, **ex}, loss_target)

    if N_MICROBATCH == 1:
        loss, (grad_w, grad_x) = one_microbatch(per_example, given["loss_target"])
    else:
        def body(carry, xs):
            loss_sum, grad_sum = carry
            l_k, (gw_k, gx_k) = one_microbatch(xs[0], xs[1])
            with _jax.named_scope("update"):
                return (loss_sum + l_k, _jax.tree.map(_jnp.add, grad_sum, gw_k)), gx_k

        init = (_jnp.zeros((), _jnp.float32), _jax.tree.map(_jnp.zeros_like, weights))
        (loss, grad_w), grad_x = _jax.lax.scan(body, init, (per_example, given["loss_target"]))
    with _jax.named_scope("update"):
        delta_w, new_m, new_v = {}, {}, {}
        for n in TWIN_WEIGHTS:
            delta_w[n], new_m[n], new_v[n] = _adamw(weights[n], grad_w[n], given["m_" + n], given["v_" + n])
    return (loss, grad_x, *[grad_w[n] for n in TWIN_WEIGHTS], *[delta_w[n] for n in TWIN_WEIGHTS],
            *[new_m[n] for n in TWIN_WEIGHTS], *[new_v[n] for n in TWIN_WEIGHTS])
```

```python
import functools
import math

import jax
import jax.numpy as jnp
from jax import lax
from jax.experimental import pallas as pl
from jax.experimental.pallas import tpu as pltpu

F32 = jnp.float32
BF16 = jnp.bfloat16
MXU_DTYPE = BF16

D_MODEL = 1024
D_FF = 2816
DN_HEADS = 8
DN_HEAD_DIM = 64
DN_WIDTH = DN_HEADS * DN_HEAD_DIM
CONV_WIDTH = 4
CHUNK = 64
S5_GROUP_CH = 16
S5_GROUPS = 32
S5_WIDTH = S5_GROUPS * S5_GROUP_CH
S5_STATE = 64
S5_LANES = S5_GROUPS * S5_STATE
N_MOD = 9
EPS = 1e-6
N_SHARD = 4
FF_SHARD = D_FF // N_SHARD
BA_PAD = 128

ADAM_LR = 0.001
ADAM_B1 = 0.9
ADAM_B2 = 0.999
ADAM_EPS = 1e-08
ADAM_WD = 0.01
ADAM_STEP = 10

VMEM_BYTES_V7X = 64 * 1024 * 1024
SUBLANES = 8
LANES = 128


def _params(block_bytes, extra_bytes=0):
    need = 2 * block_bytes + extra_bytes + (4 << 20)
    return pltpu.CompilerParams(vmem_limit_bytes=int(min(max(need, 16 << 20), VMEM_BYTES_V7X - (8 << 20))))


def _nbytes(shape, dtype):
    return math.prod(shape) * jnp.dtype(dtype).itemsize


_NN = (((1,), (0,)), ((), ()))
_NT = (((1,), (1,)), ((), ()))
_TN = (((0,), (0,)), ((), ()))


def _mm_act(pairs, mode, *, name, out_sharded=False, reduce_shards=False, out_dtype=F32, add=None, tm=512):
    n_tok = pairs[0][0].shape[1]
    n_out = pairs[0][1].shape[2] if mode == "nn" else pairs[0][1].shape[1]
    tm = min(tm, n_tok)
    tn = n_out if n_out <= 1536 else 1024
    assert n_tok % tm == 0 and n_out % tn == 0
    n_so = N_SHARD if out_sharded else 1
    n_red = N_SHARD if reduce_shards else 1
    grid = (n_so, n_tok // tm, n_out // tn, n_red)
    dims = _NN if mode == "nn" else _NT

    def shard_of(n_sh):
        if n_sh == 1:
            return lambda s, r: 0
        return (lambda s, r: s) if out_sharded else (lambda s, r: r)

    in_specs, args, blk = [], [], 0
    for a, b in pairs:
        k_dim = a.shape[2]
        sa, sb = shard_of(a.shape[0]), shard_of(b.shape[0])
        in_specs.append(pl.BlockSpec((1, tm, k_dim), lambda s, i, j, r, sa=sa: (sa(s, r), i, 0)))
        if mode == "nn":
            assert b.shape[1] == k_dim
            in_specs.append(pl.BlockSpec((1, k_dim, tn), lambda s, i, j, r, sb=sb: (sb(s, r), 0, j)))
        else:
            assert b.shape[2] == k_dim
            in_specs.append(pl.BlockSpec((1, tn, k_dim), lambda s, i, j, r, sb=sb: (sb(s, r), j, 0)))
        args += [a, b]
        blk += _nbytes((tm, k_dim), a.dtype) + _nbytes((k_dim, tn), b.dtype)
    if add is not None:
        in_specs.append(pl.BlockSpec((tm, tn), lambda s, i, j, r: (i, j)))
        args.append(add)
        blk += _nbytes((tm, tn), F32)
    blk += _nbytes((tm, tn), out_dtype)
    n_pairs = len(pairs)

    def body(*refs):
        out_ref = refs[2 * n_pairs + (add is not None)]
        acc = None
        for k in range(n_pairs):
            a = refs[2 * k][0].astype(MXU_DTYPE)
            b = refs[2 * k + 1][0].astype(MXU_DTYPE)
            d = lax.dot_general(a, b, dims, preferred_element_type=F32)
            acc = d if acc is None else acc + d

        def finish(total):
            if add is not None:
                total = total + refs[2 * n_pairs][...]
            out_ref[0] = total.astype(out_dtype)

        if n_red == 1:
            finish(acc)
        else:
            acc_ref = refs[-1]
            r = pl.program_id(3)

            @pl.when(r == 0)
            def _():
                acc_ref[...] = acc

            @pl.when(r > 0)
            def _():
                acc_ref[...] += acc

            @pl.when(r == n_red - 1)
            def _():
                finish(acc_ref[...])

    return pl.pallas_call(
        body,
        name=name,
        grid=grid,
        in_specs=in_specs,
        out_specs=pl.BlockSpec((1, tm, tn), lambda s, i, j, r: (s, i, j)),
        out_shape=jax.ShapeDtypeStruct((n_so, n_tok, n_out), out_dtype),
        scratch_shapes=[pltpu.VMEM((tm, tn), F32)] if n_red > 1 else [],
        compiler_params=_params(blk, 3 * _nbytes((tm, tn), F32)),
    )(*args)


def _mm_tn(a, b, *, name, tt=512):
    n_tok, k_dim = a.shape[1], a.shape[2]
    n_out = b.shape[2]
    tt = min(tt, n_tok)
    tk = k_dim if k_dim <= 1536 else 1024
    tn = n_out if n_out <= 1536 else 1024
    assert n_tok % tt == 0 and k_dim % tk == 0 and n_out % tn == 0
    n_so = max(a.shape[0], b.shape[0])
    sa = (lambda s: s) if a.shape[0] > 1 else (lambda s: 0)
    sb = (lambda s: s) if b.shape[0] > 1 else (lambda s: 0)
    grid = (n_so, k_dim // tk, n_out // tn, n_tok // tt)

    def body(a_ref, b_ref, out_ref):
        d = lax.dot_general(a_ref[0].astype(MXU_DTYPE), b_ref[0].astype(MXU_DTYPE), _TN, preferred_element_type=F32)
        t = pl.program_id(3)

        @pl.when(t == 0)
        def _():
            out_ref[0] = d

        @pl.when(t > 0)
        def _():
            out_ref[0] += d

    blk = _nbytes((tt, tk), a.dtype) + _nbytes((tt, tn), b.dtype) + _nbytes((tk, tn), F32)
    return pl.pallas_call(
        body,
        name=name,
        grid=grid,
        in_specs=[
            pl.BlockSpec((1, tt, tk), lambda s, ki, nj, t: (sa(s), t, ki)),
            pl.BlockSpec((1, tt, tn), lambda s, ki, nj, t: (sb(s), t, nj)),
        ],
        out_specs=pl.BlockSpec((1, tk, tn), lambda s, ki, nj, t: (s, ki, nj)),
        out_shape=jax.ShapeDtypeStruct((n_so, k_dim, n_out), F32),
        compiler_params=_params(blk, 2 * _nbytes((tk, tn), F32) + _nbytes((tt, tk), F32)),
    )(a, b)


@functools.partial(jax.custom_vjp, nondiff_argnums=(2,))
def _mdot(a, b, dims):
    return lax.dot_general(a.astype(MXU_DTYPE), b.astype(MXU_DTYPE), dims, preferred_element_type=F32)


def _mdot_fwd(a, b, dims):
    return _mdot(a, b, dims), (a, b)


def _mdot_bwd(dims, res, g):
    a, b = res
    (ca, cb), (ba, bb) = dims
    nb = len(ba)
    assert tuple(ba) == tuple(range(nb)) and tuple(bb) == tuple(range(nb)) and len(ca) == 1 and a.ndim == nb + 2
    batch = (tuple(range(nb)), tuple(range(nb)))
    ra, rb = nb, nb + 1
    a_free = (set(range(nb, nb + 2)) - set(ca)).pop()
    b_free = (set(range(nb, nb + 2)) - set(cb)).pop()
    if a_free < ca[0]:
        da = _mdot(g, b, (((rb,), (b_free,)), batch))
    else:
        da = _mdot(b, g, (((b_free,), (rb,)), batch))
    if b_free > cb[0]:
        db = _mdot(a, g, (((a_free,), (ra,)), batch))
    else:
        db = _mdot(g, a, (((ra,), (a_free,)), batch))
    return da.astype(a.dtype), db.astype(b.dtype)


_mdot.defvjp(_mdot_fwd, _mdot_bwd)


def _rms(x, gain):
    return x * lax.rsqrt(jnp.mean(x * x, axis=-1, keepdims=True) + EPS) * gain


def _pre_fn(coef, x_in, f, gate, gain, shift, scale):
    x_new = x_in if f is None else x_in + coef * gate * f
    return x_new, _rms(x_new, gain) * (1.0 + scale) + shift


def _row_spec(ts):
    return pl.BlockSpec((1, ts, D_MODEL), lambda b, j: (b, j, 0))


_BATCH_VEC = pl.BlockSpec((1, 1, D_MODEL), lambda b, j: (b, 0, 0))
_ONE_VEC = pl.BlockSpec((1, D_MODEL), lambda b, j: (0, 0))


def _pre(x_in, f, gate, gain, shift, scale, coef, *, name, ts=512):
    n_b, n_s, _ = x_in.shape
    ts = min(ts, n_s)
    has_res = f is not None

    def body(*refs):
        if has_res:
            x_ref, f_ref, gate_ref, gain_ref, sh_ref, sc_ref, xn_ref, a_ref = refs
            x_new, a = _pre_fn(coef, x_ref[0], f_ref[0], gate_ref[0], gain_ref[...], sh_ref[0], sc_ref[0])
            xn_ref[0] = x_new
        else:
            x_ref, gain_ref, sh_ref, sc_ref, a_ref = refs
            _, a = _pre_fn(coef, x_ref[0], None, None, gain_ref[...], sh_ref[0], sc_ref[0])
        a_ref[0] = a.astype(a_ref.dtype)

    row = _row_spec(ts)
    if has_res:
        args = (x_in, f, gate, gain, shift, scale)
        in_specs = [row, row, _BATCH_VEC, _ONE_VEC, _BATCH_VEC, _BATCH_VEC]
        out_specs = (row, row)
        out_shape = (jax.ShapeDtypeStruct(x_in.shape, F32), jax.ShapeDtypeStruct(x_in.shape, MXU_DTYPE))
    else:
        args = (x_in, gain, shift, scale)
        in_specs = [row, _ONE_VEC, _BATCH_VEC, _BATCH_VEC]
        out_specs = row
        out_shape = jax.ShapeDtypeStruct(x_in.shape, MXU_DTYPE)
    return pl.pallas_call(
        body, name=name, grid=(n_b, n_s // ts), in_specs=in_specs, out_specs=out_specs, out_shape=out_shape,
        compiler_params=_params(5 * _nbytes((ts, D_MODEL), F32), 4 * _nbytes((ts, D_MODEL), F32)),
    )(*args)


def _accumulate(ref, value, first):
    @pl.when(first)
    def _():
        ref[...] = value

    @pl.when(jnp.logical_not(first))
    def _():
        ref[...] += value


def _pre_bwd(x_in, f, gate, gain, shift, scale, coef, da, dx_up, *, name, ts=512):
    n_b, n_s, _ = x_in.shape
    ts = min(ts, n_s)
    has_res = f is not None
    has_up = dx_up is not None

    def body(*refs):
        refs = list(refs)
        x_ref = refs.pop(0)
        f_ref, gate_ref = (refs.pop(0), refs.pop(0)) if has_res else (None, None)
        gain_ref, sh_ref, sc_ref, da_ref = refs.pop(0), refs.pop(0), refs.pop(0), refs.pop(0)
        up_ref = refs.pop(0) if has_up else None
        dx_ref = refs.pop(0)
        df_ref, dgate_ref = (refs.pop(0), refs.pop(0)) if has_res else (None, None)
        dgain_ref, dsh_ref, dsc_ref = refs
        b, j = pl.program_id(0), pl.program_id(1)
        da_v = da_ref[0].astype(F32)
        up_v = up_ref[0] if has_up else jnp.zeros((ts, D_MODEL), F32)
        if has_res:
            fn = functools.partial(_pre_fn, coef)
            _, pull = jax.vjp(fn, x_ref[0], f_ref[0], gate_ref[0], gain_ref[...], sh_ref[0], sc_ref[0])
            dx, df, dgate, dgain, dsh, dsc = pull((up_v, da_v))
            df_ref[0] = df.astype(df_ref.dtype)
            _accumulate(dgate_ref, dgate[None], j == 0)
        else:
            fn = lambda x, g, sh, sc: _pre_fn(coef, x, None, None, g, sh, sc)
            _, pull = jax.vjp(fn, x_ref[0], gain_ref[...], sh_ref[0], sc_ref[0])
            dx, dgain, dsh, dsc = pull((up_v, da_v))
        dx_ref[0] = dx
        _accumulate(dgain_ref, dgain, jnp.logical_and(b == 0, j == 0))
        _accumulate(dsh_ref, dsh[None], j == 0)
        _accumulate(dsc_ref, dsc[None], j == 0)

    row = _row_spec(ts)
    args, in_specs = [x_in], [row]
    if has_res:
        args += [f, gate]
        in_specs += [row, _BATCH_VEC]
    args += [gain, shift, scale, da]
    in_specs += [_ONE_VEC, _BATCH_VEC, _BATCH_VEC, row]
    if has_up:
        args.append(dx_up)
        in_specs.append(row)
    vec = jax.ShapeDtypeStruct((n_b, 1, D_MODEL), F32)
    out_shape, out_specs = [jax.ShapeDtypeStruct(x_in.shape, F32)], [row]
    if has_res:
        out_shape += [jax.ShapeDtypeStruct(x_in.shape, MXU_DTYPE), vec]
        out_specs += [row, _BATCH_VEC]
    out_shape += [jax.ShapeDtypeStruct((1, D_MODEL), F32), vec, vec]
    out_specs += [_ONE_VEC, _BATCH_VEC, _BATCH_VEC]
    return pl.pallas_call(
        body, name=name, grid=(n_b, n_s // ts), in_specs=in_specs, out_specs=tuple(out_specs), out_shape=tuple(out_shape),
        compiler_params=_params(6 * _nbytes((ts, D_MODEL), F32), 8 * _nbytes((ts, D_MODEL), F32)),
    )(*args)


def _final_fn(x_in, f, gate, gain, target):
    x_new = x_in + 0.5 * gate * f
    err = jnp.square(_rms(x_new, gain) - target)
    return 0.5 * jnp.sum(jnp.mean(err, axis=-1))


def _final(x_in, f, gate, gain, target, *, name, ts=512):
    n_b, n_s, _ = x_in.shape
    ts = min(ts, n_s)

    def body(x_ref, f_ref, gate_ref, gain_ref, t_ref, loss_ref, dx_ref, df_ref, dgate_ref, dgain_ref):
        b, j = pl.program_id(0), pl.program_id(1)
        loss, (dx, df, dgate, dgain) = jax.value_and_grad(_final_fn, argnums=(0, 1, 2, 3))(
            x_ref[0], f_ref[0], gate_ref[0], gain_ref[...], t_ref[0])
        first = jnp.logical_and(b == 0, j == 0)
        _accumulate(loss_ref, jnp.reshape(loss, (1, 1)), first)
        dx_ref[0] = dx
        df_ref[0] = df.astype(df_ref.dtype)
        _accumulate(dgate_ref, dgate[None], j == 0)
        _accumulate(dgain_ref, dgain, first)

    row = _row_spec(ts)
    return pl.pallas_call(
        body, name=name, grid=(n_b, n_s // ts),
        in_specs=[row, row, _BATCH_VEC, _ONE_VEC, row],
        out_specs=(pl.BlockSpec((1, 1), lambda b, j: (0, 0)), row, row, _BATCH_VEC, _ONE_VEC),
        out_shape=(jax.ShapeDtypeStruct((1, 1), F32), jax.ShapeDtypeStruct(x_in.shape, F32),
                   jax.ShapeDtypeStruct(x_in.shape, MXU_DTYPE), jax.ShapeDtypeStruct((n_b, 1, D_MODEL), F32),
                   jax.ShapeDtypeStruct((1, D_MODEL), F32)),
        compiler_params=_params(5 * _nbytes((ts, D_MODEL), F32), 8 * _nbytes((ts, D_MODEL), F32)),
    )(x_in, f, gate, gain, target)


def _ffn_up(a, w1s, w3s, *, name, tm=512):
    n_tok = a.shape[0]
    tm = min(tm, n_tok)

    def body(a_ref, w1_ref, w3_ref, h1_ref, h3_ref, g_ref):
        av = a_ref[...].astype(MXU_DTYPE)
        h1 = lax.dot_general(av, w1_ref[0].astype(MXU_DTYPE), _NN, preferred_element_type=F32)
        h3 = lax.dot_general(av, w3_ref[0].astype(MXU_DTYPE), _NN, preferred_element_type=F32)
        h1_ref[0] = h1.astype(h1_ref.dtype)
        h3_ref[0] = h3.astype(h3_ref.dtype)
        g_ref[0] = (jax.nn.silu(h1) * h3).astype(g_ref.dtype)

    w_spec = pl.BlockSpec((1, D_MODEL, FF_SHARD), lambda s, i: (s, 0, 0))
    h_spec = pl.BlockSpec((1, tm, FF_SHARD), lambda s, i: (s, i, 0))
    h_shape = jax.ShapeDtypeStruct((N_SHARD, n_tok, FF_SHARD), MXU_DTYPE)
    blk = _nbytes((tm, D_MODEL), a.dtype) + 2 * _nbytes((D_MODEL, FF_SHARD), w1s.dtype) + 3 * _nbytes((tm, FF_SHARD), MXU_DTYPE)
    return pl.pallas_call(
        body, name=name, grid=(N_SHARD, n_tok // tm),
        in_specs=[pl.BlockSpec((tm, D_MODEL), lambda s, i: (i, 0)), w_spec, w_spec],
        out_specs=(h_spec, h_spec, h_spec), out_shape=(h_shape, h_shape, h_shape),
        compiler_params=_params(blk, 6 * _nbytes((tm, FF_SHARD), F32)),
    )(a, w1s, w3s)


def _ffn_down_bwd(df, w2s, h1, h3, *, name, tm=512):
    n_tok = df.shape[0]
    tm = min(tm, n_tok)

    def body(df_ref, w2_ref, h1_ref, h3_ref, dh1_ref, dh3_ref):
        dg = lax.dot_general(df_ref[...].astype(MXU_DTYPE), w2_ref[0].astype(MXU_DTYPE), _NT, preferred_element_type=F32)
        h1v = h1_ref[0].astype(F32)
        h3v = h3_ref[0].astype(F32)
        sig = jax.nn.sigmoid(h1v)
        dh3_ref[0] = (dg * (h1v * sig)).astype(dh3_ref.dtype)
        dh1_ref[0] = (dg * h3v * (sig * (1.0 + h1v * (1.0 - sig)))).astype(dh1_ref.dtype)

    h_spec = pl.BlockSpec((1, tm, FF_SHARD), lambda s, i: (s, i, 0))
    h_shape = jax.ShapeDtypeStruct((N_SHARD, n_tok, FF_SHARD), MXU_DTYPE)
    blk = _nbytes((tm, D_MODEL), df.dtype) + _nbytes((FF_SHARD, D_MODEL), w2s.dtype) + 4 * _nbytes((tm, FF_SHARD), MXU_DTYPE)
    return pl.pallas_call(
        body, name=name, grid=(N_SHARD, n_tok // tm),
        in_specs=[pl.BlockSpec((tm, D_MODEL), lambda s, i: (i, 0)),
                  pl.BlockSpec((1, FF_SHARD, D_MODEL), lambda s, i: (s, 0, 0)), h_spec, h_spec],
        out_specs=(h_spec, h_spec), out_shape=(h_shape, h_shape),
        compiler_params=_params(blk, 8 * _nbytes((tm, FF_SHARD), F32)),
    )(df, w2s, h1, h3)


def _ffn_fwd(a, w1s, w3s, w2s, tag):
    h1, h3, g = _ffn_up(a, w1s, w3s, name=f"{tag}_up")
    f = _mm_act([(g, w2s)], "nn", reduce_shards=True, name=f"{tag}_down")[0]
    return f, (h1, h3, g)


def _ffn_bwd(a, w1s, w3s, w2s, saved, df, tag):
    h1, h3, g = saved
    dh1, dh3 = _ffn_down_bwd(df, w2s, h1, h3, name=f"{tag}_down_bwd")
    da = _mm_act([(dh1, w1s), (dh3, w3s)], "nt", reduce_shards=True, name=f"{tag}_up_bwd")[0]
    a3 = a[None]
    dw1 = _mm_tn(a3, dh1, name=f"{tag}_dw1")
    dw3 = _mm_tn(a3, dh3, name=f"{tag}_dw3")
    dw2 = _mm_tn(g, df[None], name=f"{tag}_dw2")
    return da, dw1, dw3, dw2


CONV_LANES = 256


def _shift_down(x, d):
    if d == 0:
        return x
    row = lax.broadcasted_iota(jnp.int32, x.shape, 0)
    return jnp.where(row >= d, pltpu.roll(x, d, 0), 0.0)


def _shift_up(x, d):
    if d == 0:
        return x
    n = x.shape[0]
    row = lax.broadcasted_iota(jnp.int32, x.shape, 0)
    return jnp.where(row < n - d, pltpu.roll(x, n - d, 0), 0.0)


def _conv_pre(x, w):
    acc = None
    for j in range(CONV_WIDTH):
        term = w[j:j + 1, :] * _shift_down(x, CONV_WIDTH - 1 - j)
        acc = term if acc is None else acc + term
    return acc


def _conv_fwd(x, w, *, name):
    n_b, n_s, n_c = x.shape
    spec = pl.BlockSpec((1, n_s, CONV_LANES), lambda b, cj: (b, 0, cj))

    def body(x_ref, w_ref, o_ref):
        o_ref[0] = jax.nn.silu(_conv_pre(x_ref[0], w_ref[...]))

    return pl.pallas_call(
        body, name=name, grid=(n_b, n_c // CONV_LANES),
        in_specs=[spec, pl.BlockSpec((CONV_WIDTH, CONV_LANES), lambda b, cj: (0, cj))],
        out_specs=spec, out_shape=jax.ShapeDtypeStruct(x.shape, F32),
        compiler_params=_params(2 * _nbytes((n_s, CONV_LANES), F32), 6 * _nbytes((n_s, CONV_LANES), F32)),
    )(x, w)


def _conv_bwd(x, w, dout, *, name):
    n_b, n_s, n_c = x.shape
    spec = pl.BlockSpec((1, n_s, CONV_LANES), lambda cj, b: (b, 0, cj))
    w_spec = pl.BlockSpec((CONV_WIDTH, CONV_LANES), lambda cj, b: (0, cj))

    def body(x_ref, w_ref, do_ref, dx_ref, dw_ref):
        xv, wv = x_ref[0], w_ref[...]
        pre = _conv_pre(xv, wv)
        sig = jax.nn.sigmoid(pre)
        dpre = do_ref[0] * (sig * (1.0 + pre * (1.0 - sig)))
        dx = None
        first = pl.program_id(1) == 0
        for j in range(CONV_WIDTH):
            d = CONV_WIDTH - 1 - j
            term = wv[j:j + 1, :] * _shift_up(dpre, d)
            dx = term if dx is None else dx + term
            dwj = jnp.sum(dpre * _shift_down(xv, d), axis=0, keepdims=True)
            _accumulate(dw_ref.at[j:j + 1, :], dwj, first)
        dx_ref[0] = dx.astype(dx_ref.dtype)

    return pl.pallas_call(
        body, name=name, grid=(n_c // CONV_LANES, n_b),
        in_specs=[spec, w_spec, spec], out_specs=(spec, w_spec),
        out_shape=(jax.ShapeDtypeStruct(x.shape, MXU_DTYPE), jax.ShapeDtypeStruct((CONV_WIDTH, n_c), F32)),
        compiler_params=_params(3 * _nbytes((n_s, CONV_LANES), F32), 8 * _nbytes((n_s, CONV_LANES), F32)),
    )(x, w, dout)


_BNT = (((2,), (2,)), ((0,), (0,)))
_BNN = (((2,), (1,)), ((0,), (0,)))
_BTN = (((1,), (1,)), ((0,), (0,)))
DN_PREP_CHUNKS = 8
DN_SCAN_HEADS = 4
N_DOUBLINGS = 5


def _fdot(a, b, dims):
    return lax.dot_general(a, b, dims, precision=lax.Precision.HIGHEST, preferred_element_type=F32)


def _dn_prep_fn(qc, kc, vc, bl, lac, lar, a_log, dt_bias):
    q = qc * lax.rsqrt(jnp.sum(qc * qc, axis=-1, keepdims=True) + EPS) * (DN_HEAD_DIM ** -0.5)
    k = kc * lax.rsqrt(jnp.sum(kc * kc, axis=-1, keepdims=True) + EPS)
    beta = jax.nn.sigmoid(bl)
    neg_a = -jnp.exp(a_log)
    lgc = neg_a * jax.nn.softplus(lac + dt_bias)
    lgr = neg_a * jax.nn.softplus(lar + dt_bias)
    row = lax.broadcasted_iota(jnp.int32, (CHUNK, CHUNK), 0)
    col = lax.broadcasted_iota(jnp.int32, (CHUNK, CHUNK), 1)
    causal, strict = row >= col, row > col
    g_c = jnp.sum(jnp.where(causal, lgr, 0.0), axis=-1, keepdims=True)
    g_r = jnp.sum(jnp.where(row <= col, lgc, 0.0), axis=-2, keepdims=True)
    decay = jnp.exp(jnp.where(causal, g_c - g_r, -jnp.inf))
    kb = k * beta
    a = jnp.where(strict, _mdot(kb, k, _BNT) * decay, 0.0)
    eye = jnp.where(row == col, 1.0, 0.0)
    inv = eye - a
    power = a
    for _ in range(N_DOUBLINGS):
        power = _fdot(power, power, _BNN)
        inv = inv + _fdot(inv, power, _BNN)
    u = _fdot(inv, vc * beta, _BNN)
    w = _fdot(inv, kb * jnp.exp(g_c), _BNN)
    attn = _mdot(q, k, _BNT) * decay
    g_last = jnp.sum(lgc, axis=-2, keepdims=True)
    return q * jnp.exp(g_c), k * jnp.exp(g_last - g_c), u, w, attn, g_last


def _dn_prep_specs(n_cb):
    tok = n_cb * CHUNK
    wide = pl.BlockSpec((1, 1, tok, DN_HEAD_DIM), lambda h, b, j: (b, h, j, 0))
    col = pl.BlockSpec((1, 1, tok, 1), lambda h, b, j: (b, h, j, 0))
    rowv = pl.BlockSpec((1, 1, n_cb, 1, CHUNK), lambda h, b, j: (b, h, j, 0, 0))
    one = pl.BlockSpec((1, 1, n_cb, 1, 1), lambda h, b, j: (b, h, j, 0, 0))
    head = pl.BlockSpec((1, 1, 1), lambda h, b, j: (h, 0, 0))
    return wide, col, rowv, one, head


def _dn_prep_load(n_cb, q_ref, k_ref, v_ref, bl_ref, lac_ref, lar_ref, al_ref, dt_ref):
    wide = lambda r: r[0, 0].reshape(n_cb, CHUNK, DN_HEAD_DIM)
    colv = lambda r: r[0, 0].reshape(n_cb, CHUNK, 1)
    return (wide(q_ref), wide(k_ref), wide(v_ref), colv(bl_ref), colv(lac_ref), lar_ref[0, 0], al_ref[...], dt_ref[...])


def _dn_prep(qh, kh, vh, bl, lac, lar, a_log, dt_bias, *, name):
    n_b, n_h, n_s, _ = qh.shape
    n_cb = min(DN_PREP_CHUNKS, n_s // CHUNK)
    tok = n_cb * CHUNK
    wide, col, rowv, one, head = _dn_prep_specs(n_cb)

    def body(*refs):
        outs = _dn_prep_fn(*_dn_prep_load(n_cb, *refs[:8]))
        for ref, val in zip(refs[8:13], outs[:5]):
            ref[0, 0] = val.reshape(tok, DN_HEAD_DIM)
        refs[13][0, 0] = outs[5]

    big = jax.ShapeDtypeStruct(qh.shape, F32)
    return pl.pallas_call(
        body, name=name, grid=(n_h, n_b, n_s // tok),
        in_specs=[wide, wide, wide, col, col, rowv, head, head],
        out_specs=(wide, wide, wide, wide, wide, one),
        out_shape=(big, big, big, big, big, jax.ShapeDtypeStruct((n_b, n_h, n_s // CHUNK, 1, 1), F32)),
        compiler_params=_params(10 * _nbytes((tok, LANES), F32), 48 * _nbytes((tok, LANES), F32)),
    )(qh, kh, vh, bl, lac, lar, a_log, dt_bias)


def _dn_prep_bwd(qh, kh, vh, bl, lac, lar, a_log, dt_bias, cts, *, name):
    n_b, n_h, n_s, _ = qh.shape
    n_cb = min(DN_PREP_CHUNKS, n_s // CHUNK)
    tok = n_cb * CHUNK
    wide, col, rowv, one, head = _dn_prep_specs(n_cb)

    def body(*refs):
        prim = _dn_prep_load(n_cb, *refs[:8])
        ct = tuple(r[0, 0].reshape(n_cb, CHUNK, DN_HEAD_DIM) for r in refs[8:13]) + (refs[13][0, 0],)
        _, pull = jax.vjp(_dn_prep_fn, *prim)
        dq, dk, dv, dbl, dlac, dlar, dal, ddt = pull(ct)
        for ref, val in zip(refs[14:17], (dq, dk, dv)):
            ref[0, 0] = val.reshape(tok, DN_HEAD_DIM)
        refs[17][0, 0] = dbl.reshape(tok, 1)
        refs[18][0, 0] = dlac.reshape(tok, 1)
        refs[19][0, 0] = dlar
        first = jnp.logical_and(pl.program_id(1) == 0, pl.program_id(2) == 0)
        _accumulate(refs[20], dal, first)
        _accumulate(refs[21], ddt, first)

    big = jax.ShapeDtypeStruct(qh.shape, F32)
    return pl.pallas_call(
        body, name=name, grid=(n_h, n_b, n_s // tok),
        in_specs=[wide, wide, wide, col, col, rowv, head, head, wide, wide, wide, wide, wide, one],
        out_specs=(wide, wide, wide, col, col, rowv, head, head),
        out_shape=(big, big, big, jax.ShapeDtypeStruct(bl.shape, F32), jax.ShapeDtypeStruct(lac.shape, F32),
                   jax.ShapeDtypeStruct(lar.shape, F32), jax.ShapeDtypeStruct(a_log.shape, F32),
                   jax.ShapeDtypeStruct(dt_bias.shape, F32)),
        compiler_params=_params(18 * _nbytes((tok, LANES), F32), 96 * _nbytes((tok, LANES), F32)),
    )(qh, kh, vh, bl, lac, lar, a_log, dt_bias, *cts)


def _dn_step(state, q, k, u, w, a, gl):
    v_new = u - _mdot(w, state, _BNN)
    o = _mdot(q, state, _BNN) + _mdot(a, v_new, _BNN)
    return state * jnp.exp(gl) + _mdot(k, v_new, _BTN), o


def _dn_scan_specs(n_cb, n_blocks, reverse):
    tok = n_cb * CHUNK
    jj = (lambda j: n_blocks - 1 - j) if reverse else (lambda j: j)
    wide = pl.BlockSpec((1, DN_SCAN_HEADS, tok, DN_HEAD_DIM), lambda b, h, j: (b, h, jj(j), 0))
    one = pl.BlockSpec((1, DN_SCAN_HEADS, n_cb, 1, 1), lambda b, h, j: (b, h, jj(j), 0, 0))
    st = pl.BlockSpec((1, DN_SCAN_HEADS, n_cb, DN_HEAD_DIM, DN_HEAD_DIM), lambda b, h, j: (b, h, jj(j), 0, 0))
    return wide, one, st


def _dn_scan(qd, kd, u, w, attn, g_last, *, name):
    n_b, n_h, n_s, _ = qd.shape
    n_cb = min(DN_PREP_CHUNKS, n_s // CHUNK)
    n_blocks = n_s // (n_cb * CHUNK)
    wide, one, st = _dn_scan_specs(n_cb, n_blocks, False)

    def body(qd_ref, kd_ref, u_ref, w_ref, a_ref, gl_ref, o_ref, st_ref, state_ref):
        @pl.when(pl.program_id(2) == 0)
        def _():
            state_ref[...] = jnp.zeros(state_ref.shape, F32)

        def step(n, state):
            rows = pl.ds(pl.multiple_of(n * CHUNK, CHUNK), CHUNK)
            st_ref[0, :, n] = state
            state, o = _dn_step(state, qd_ref[0, :, rows, :], kd_ref[0, :, rows, :], u_ref[0, :, rows, :],
                                w_ref[0, :, rows, :], a_ref[0, :, rows, :], gl_ref[0, :, n])
            o_ref[0, :, rows, :] = o
            return state

        state_ref[...] = lax.fori_loop(0, n_cb, step, state_ref[...])

    return pl.pallas_call(
        body, name=name, grid=(n_b, n_h // DN_SCAN_HEADS, n_blocks),
        in_specs=[wide, wide, wide, wide, wide, one], out_specs=(wide, st),
        out_shape=(jax.ShapeDtypeStruct(qd.shape, F32),
                   jax.ShapeDtypeStruct((n_b, n_h, n_s // CHUNK, DN_HEAD_DIM, DN_HEAD_DIM), F32)),
        scratch_shapes=[pltpu.VMEM((DN_SCAN_HEADS, DN_HEAD_DIM, DN_HEAD_DIM), F32)],
        compiler_params=_params(8 * _nbytes((DN_SCAN_HEADS, n_cb * CHUNK, LANES), F32), 8 << 20),
    )(qd, kd, u, w, attn, g_last)


def _dn_scan_bwd(qd, kd, u, w, attn, g_last, states, do, *, name):
    n_b, n_h, n_s, _ = qd.shape
    n_cb = min(DN_PREP_CHUNKS, n_s // CHUNK)
    n_blocks = n_s // (n_cb * CHUNK)
    wide, one, st = _dn_scan_specs(n_cb, n_blocks, True)

    def body(qd_ref, kd_ref, u_ref, w_ref, a_ref, gl_ref, st_ref, do_ref,
             dq_ref, dk_ref, du_ref, dw_ref, da_ref, dgl_ref, dstate_ref):
        @pl.when(pl.program_id(2) == 0)
        def _():
            dstate_ref[...] = jnp.zeros(dstate_ref.shape, F32)

        def step(i, dstate):
            n = n_cb - 1 - i
            rows = pl.ds(pl.multiple_of(n * CHUNK, CHUNK), CHUNK)
            _, pull = jax.vjp(_dn_step, st_ref[0, :, n], qd_ref[0, :, rows, :], kd_ref[0, :, rows, :],
                              u_ref[0, :, rows, :], w_ref[0, :, rows, :], a_ref[0, :, rows, :], gl_ref[0, :, n])
            dstate, dq, dk, du, dw, da, dgl = pull((dstate, do_ref[0, :, rows, :]))
            dq_ref[0, :, rows, :] = dq
            dk_ref[0, :, rows, :] = dk
            du_ref[0, :, rows, :] = du
            dw_ref[0, :, rows, :] = dw
            da_ref[0, :, rows, :] = da
            dgl_ref[0, :, n] = dgl
            return dstate

        dstate_ref[...] = lax.fori_loop(0, n_cb, step, dstate_ref[...])

    big = jax.ShapeDtypeStruct(qd.shape, F32)
    return pl.pallas_call(
        body, name=name, grid=(n_b, n_h // DN_SCAN_HEADS, n_blocks),
        in_specs=[wide, wide, wide, wide, wide, one, st, wide],
        out_specs=(wide, wide, wide, wide, wide, one),
        out_shape=(big, big, big, big, big, jax.ShapeDtypeStruct(g_last.shape, F32)),
        scratch_shapes=[pltpu.VMEM((DN_SCAN_HEADS, DN_HEAD_DIM, DN_HEAD_DIM), F32)],
        compiler_params=_params(13 * _nbytes((DN_SCAN_HEADS, n_cb * CHUNK, LANES), F32), 8 << 20),
    )(qd, kd, u, w, attn, g_last, states, do)


def _dn_post_fn(o, z, gain):
    return o * lax.rsqrt(jnp.mean(o * o, axis=-1, keepdims=True) + EPS) * gain * jax.nn.silu(z)


_HEAD_ROWS = lambda n_s: pl.BlockSpec((1, 1, n_s, DN_HEAD_DIM), lambda b, h: (b, h, 0, 0))
_HEAD_GAIN = pl.BlockSpec((1, DN_HEAD_DIM), lambda b, h: (0, 0))


def _dn_post(o, z, gain, *, name):
    n_b, n_h, n_s, _ = o.shape

    def body(o_ref, z_ref, g_ref, out_ref):
        out_ref[0, 0] = _dn_post_fn(o_ref[0, 0], z_ref[0, 0], g_ref[...]).astype(out_ref.dtype)

    rows = _HEAD_ROWS(n_s)
    return pl.pallas_call(
        body, name=name, grid=(n_b, n_h), in_specs=[rows, rows, _HEAD_GAIN], out_specs=rows,
        out_shape=jax.ShapeDtypeStruct(o.shape, MXU_DTYPE),
        compiler_params=_params(3 * _nbytes((n_s, LANES), F32), 6 * _nbytes((n_s, LANES), F32)),
    )(o, z, gain)


def _dn_post_bwd(o, z, gain, dout, *, name):
    n_b, n_h, n_s, _ = o.shape

    def body(o_ref, z_ref, g_ref, dout_ref, do_ref, dz_ref, dg_ref):
        _, pull = jax.vjp(_dn_post_fn, o_ref[0, 0], z_ref[0, 0], g_ref[...])
        do, dz, dg = pull(dout_ref[0, 0].astype(F32))
        do_ref[0, 0] = do
        dz_ref[0, 0] = dz.astype(dz_ref.dtype)
        _accumulate(dg_ref, dg, jnp.logical_and(pl.program_id(0) == 0, pl.program_id(1) == 0))

    rows = _HEAD_ROWS(n_s)
    return pl.pallas_call(
        body, name=name, grid=(n_b, n_h), in_specs=[rows, rows, _HEAD_GAIN, rows],
        out_specs=(rows, rows, _HEAD_GAIN),
        out_shape=(jax.ShapeDtypeStruct(o.shape, F32), jax.ShapeDtypeStruct(o.shape, MXU_DTYPE),
                   jax.ShapeDtypeStruct((1, DN_HEAD_DIM), F32)),
        compiler_params=_params(5 * _nbytes((n_s, LANES), F32), 10 * _nbytes((n_s, LANES), F32)),
    )(o, z, gain, dout)


S5_SCAN_LANES = 256
TILE_ROWS = SUBLANES


def _s5_prep_fn(lam_re, lam_im, log_step, bt_re, bt_im, c_im):
    lr = jnp.minimum(lam_re, -1e-4)
    step = jnp.exp(log_step)
    mag = jnp.exp(lr * step)
    ang = lam_im * step
    lb_re = mag * jnp.cos(ang)
    lb_im = mag * jnp.sin(ang)
    den = lr * lr + lam_im * lam_im
    coef_re = ((lb_re - 1.0) * lr + lb_im * lam_im) / den
    coef_im = (lb_im * lr - (lb_re - 1.0) * lam_im) / den
    return (lb_re, lb_im, coef_re * bt_re - coef_im * bt_im, coef_re * bt_im + coef_im * bt_re, -c_im)


def _s5_prep(lam_re, lam_im, log_step, bt_re, bt_im, c_im, *, name):
    def body(*refs):
        outs = _s5_prep_fn(*(r[...] for r in refs[:6]))
        for ref, val in zip(refs[6:], outs):
            ref[...] = val

    vec = jax.ShapeDtypeStruct(lam_re.shape, F32)
    mat = jax.ShapeDtypeStruct(bt_re.shape, F32)
    return pl.pallas_call(body, name=name, out_shape=(vec, vec, mat, mat, mat))(lam_re, lam_im, log_step, bt_re, bt_im, c_im)


def _s5_prep_bwd(lam_re, lam_im, log_step, bt_re, bt_im, c_im, cts, *, name):
    def body(*refs):
        _, pull = jax.vjp(_s5_prep_fn, *(r[...] for r in refs[:6]))
        grads = pull(tuple(r[...] for r in refs[6:11]))
        for ref, val in zip(refs[11:], grads):
            ref[...] = val

    shapes = tuple(jax.ShapeDtypeStruct(a.shape, F32) for a in (lam_re, lam_im, log_step, bt_re, bt_im, c_im))
    return pl.pallas_call(body, name=name, out_shape=shapes)(lam_re, lam_im, log_step, bt_re, bt_im, c_im, *cts)


def _cmul(ar, ai, br, bi):
    return ar * br - ai * bi, ar * bi + ai * br


def _s5_powers(lr, li):
    pows = [(lr, li)]
    for _ in range(TILE_ROWS - 1):
        pows.append(_cmul(pows[-1][0], pows[-1][1], lr, li))
    return pows


def _s5_carry_table(pows, n_lanes, reverse):
    row = lax.broadcasted_iota(jnp.int32, (TILE_ROWS, n_lanes), 0)
    t_re = jnp.zeros((TILE_ROWS, n_lanes), F32)
    t_im = jnp.zeros((TILE_ROWS, n_lanes), F32)
    for r in range(TILE_ROWS):
        p_re, p_im = pows[TILE_ROWS - 1 - r] if reverse else pows[r]
        t_re = jnp.where(row == r, p_re, t_re)
        t_im = jnp.where(row == r, p_im, t_im)
    return t_re, t_im


def _s5_tile(y_re, y_im, pows, reverse):
    d = 1
    while d < TILE_ROWS:
        p_re, p_im = pows[d - 1]
        if reverse:
            s_re, s_im = _shift_up(y_re, d), _shift_up(y_im, d)
        else:
            s_re, s_im = _shift_down(y_re, d), _shift_down(y_im, d)
        m_re, m_im = _cmul(p_re, p_im, s_re, s_im)
        y_re, y_im = y_re + m_re, y_im + m_im
        d *= 2
    return y_re, y_im


def _s5_scan(bu, lb_re, lb_im, *, name):
    n_b, n_s, _ = bu.shape
    n_lb = S5_LANES // S5_SCAN_LANES
    n_tiles = n_s // TILE_ROWS
    L = S5_SCAN_LANES

    def body(re_ref, im_ref, lr_ref, li_ref, xr_ref, xi_ref):
        pows = _s5_powers(lr_ref[...], li_ref[...])
        t_re, t_im = _s5_carry_table(pows, L, False)

        def step(i, carry):
            rows = pl.ds(pl.multiple_of(i * TILE_ROWS, TILE_ROWS), TILE_ROWS)
            y_re, y_im = _s5_tile(re_ref[0, rows, :], im_ref[0, rows, :], pows, False)
            c_re, c_im = _cmul(t_re, t_im, carry[0], carry[1])
            y_re, y_im = y_re + c_re, y_im + c_im
            xr_ref[0, rows, :] = y_re
            xi_ref[0, rows, :] = y_im
            return y_re[TILE_ROWS - 1:, :], y_im[TILE_ROWS - 1:, :]

        zero = jnp.zeros((1, L), F32)
        lax.fori_loop(0, n_tiles, step, (zero, zero))

    blk_spec = lambda off: pl.BlockSpec((1, n_s, L), lambda b, j: (b, 0, j + off))
    lam_spec = pl.BlockSpec((1, L), lambda b, j: (0, j))
    x_shape = jax.ShapeDtypeStruct((n_b, n_s, S5_LANES), F32)
    return pl.pallas_call(
        body, name=name, grid=(n_b, n_lb),
        in_specs=[blk_spec(0), blk_spec(n_lb), lam_spec, lam_spec],
        out_specs=(blk_spec(0), blk_spec(0)), out_shape=(x_shape, x_shape),
        compiler_params=_params(4 * _nbytes((n_s, L), F32), 4 << 20),
    )(bu, bu, lb_re, lb_im)


def _s5_scan_bwd(dx, x_re, x_im, lb_re, lb_im, *, name):
    n_b, n_s, _ = dx.shape
    n_lb = S5_LANES // S5_SCAN_LANES
    n_tiles = n_s // TILE_ROWS
    L = S5_SCAN_LANES

    def body(dr_ref, di_ref, xr_ref, xi_ref, lr_ref, li_ref, ar_ref, ai_ref, dlr_ref, dli_ref):
        pows = _s5_powers(lr_ref[...], -li_ref[...])
        t_re, t_im = _s5_carry_table(pows, L, True)
        row = lax.broadcasted_iota(jnp.int32, (TILE_ROWS, L), 0)

        def step(k, carry):
            c_re, c_im, s_re, s_im = carry
            i = n_tiles - 1 - k
            rows = pl.ds(pl.multiple_of(i * TILE_ROWS, TILE_ROWS), TILE_ROWS)
            a_re, a_im = _s5_tile(dr_ref[0, rows, :], di_ref[0, rows, :], pows, True)
            m_re, m_im = _cmul(t_re, t_im, c_re, c_im)
            a_re, a_im = a_re + m_re, a_im + m_im
            ar_ref[0, rows, :] = a_re.astype(ar_ref.dtype)
            ai_ref[0, rows, :] = a_im.astype(ai_ref.dtype)
            prev = pl.ds(pl.multiple_of(jnp.maximum(i - 1, 0) * TILE_ROWS, TILE_ROWS), TILE_ROWS)
            keep = jnp.where(i > 0, 1.0, 0.0)
            last_re = xr_ref[0, prev, :][TILE_ROWS - 1:, :] * keep
            last_im = xi_ref[0, prev, :][TILE_ROWS - 1:, :] * keep
            xp_re = jnp.where(row == 0, last_re, _shift_down(xr_ref[0, rows, :], 1))
            xp_im = jnp.where(row == 0, last_im, _shift_down(xi_ref[0, rows, :], 1))
            s_re = s_re + a_re * xp_re + a_im * xp_im
            s_im = s_im + a_im * xp_re - a_re * xp_im
            return a_re[:1, :], a_im[:1, :], s_re, s_im

        zero = jnp.zeros((1, L), F32)
        zt = jnp.zeros((TILE_ROWS, L), F32)
        _, _, s_re, s_im = lax.fori_loop(0, n_tiles, step, (zero, zero, zt, zt))
        first = pl.program_id(1) == 0
        _accumulate(dlr_ref, jnp.sum(s_re, axis=0, keepdims=True), first)
        _accumulate(dli_ref, jnp.sum(s_im, axis=0, keepdims=True), first)

    blk_spec = lambda off: pl.BlockSpec((1, n_s, L), lambda j, b: (b, 0, j + off))
    lam_spec = pl.BlockSpec((1, L), lambda j, b: (0, j))
    a_shape = jax.ShapeDtypeStruct((n_b, n_s, S5_LANES), MXU_DTYPE)
    lam_shape = jax.ShapeDtypeStruct((1, S5_LANES), F32)
    return pl.pallas_call(
        body, name=name, grid=(n_lb, n_b),
        in_specs=[blk_spec(0), blk_spec(n_lb), blk_spec(0), blk_spec(0), lam_spec, lam_spec],
        out_specs=(blk_spec(0), blk_spec(0), lam_spec, lam_spec),
        out_shape=(a_shape, a_shape, lam_shape, lam_shape),
        compiler_params=_params(5 * _nbytes((n_s, L), F32), 4 << 20),
    )(dx, dx, x_re, x_im, lb_re, lb_im)


def _s5_out_fn(ymm, u, d_skip, w_glu, b_glu):
    y = jax.nn.gelu(ymm + d_skip * u)
    return y * jax.nn.sigmoid(_mdot(y, w_glu, _NN) + b_glu)


def _s5_out_specs(tm):
    rows = pl.BlockSpec((tm, S5_WIDTH), lambda i: (i, 0))
    vec = pl.BlockSpec((1, S5_WIDTH), lambda i: (0, 0))
    mat = pl.BlockSpec((S5_WIDTH, S5_WIDTH), lambda i: (0, 0))
    return rows, vec, mat


def _s5_out(ymm, u, d_skip, w_glu, b_glu, *, name, tm=512):
    n_tok = ymm.shape[0]
    tm = min(tm, n_tok)
    rows, vec, mat = _s5_out_specs(tm)

    def body(y_ref, u_ref, d_ref, w_ref, b_ref, o_ref):
        o_ref[...] = _s5_out_fn(y_ref[...], u_ref[...], d_ref[...], w_ref[...], b_ref[...]).astype(o_ref.dtype)

    return pl.pallas_call(
        body, name=name, grid=(n_tok // tm,), in_specs=[rows, rows, vec, mat, vec], out_specs=rows,
        out_shape=jax.ShapeDtypeStruct(ymm.shape, MXU_DTYPE),
        compiler_params=_params(4 * _nbytes((tm, S5_WIDTH), F32), 8 * _nbytes((tm, S5_WIDTH), F32)),
    )(ymm, u, d_skip, w_glu, b_glu)


def _s5_out_bwd(ymm, u, d_skip, w_glu, b_glu, dout, *, name, tm=512):
    n_tok = ymm.shape[0]
    tm = min(tm, n_tok)
    rows, vec, mat = _s5_out_specs(tm)

    def body(y_ref, u_ref, d_ref, w_ref, b_ref, do_ref, dy_ref, du_ref, dd_ref, dw_ref, db_ref):
        _, pull = jax.vjp(_s5_out_fn, y_ref[...], u_ref[...], d_ref[...], w_ref[...].astype(F32), b_ref[...])
        dy, du, dd, dw, db = pull(do_ref[...])
        dy_ref[...] = dy.astype(dy_ref.dtype)
        du_ref[...] = du
        first = pl.program_id(0) == 0
        _accumulate(dd_ref, dd, first)
        _accumulate(dw_ref, dw, first)
        _accumulate(db_ref, db, first)

    return pl.pallas_call(
        body, name=name, grid=(n_tok // tm,), in_specs=[rows, rows, vec, mat, vec, rows],
        out_specs=(rows, rows, vec, mat, vec),
        out_shape=(jax.ShapeDtypeStruct(ymm.shape, MXU_DTYPE), jax.ShapeDtypeStruct(ymm.shape, F32),
                   jax.ShapeDtypeStruct((1, S5_WIDTH), F32), jax.ShapeDtypeStruct((S5_WIDTH, S5_WIDTH), F32),
                   jax.ShapeDtypeStruct((1, S5_WIDTH), F32)),
        compiler_params=_params(6 * _nbytes((tm, S5_WIDTH), F32), 12 * _nbytes((tm, S5_WIDTH), F32)),
    )(ymm, u, d_skip, w_glu, b_glu, dout)


def _merge_fn(ga, gb, ya, yb):
    return jax.nn.sigmoid(ga) * ya + jax.nn.sigmoid(gb) * yb


def _merge(gab, ya, yb, *, name, tm=512):
    n_tok = ya.shape[0]
    tm = min(tm, n_tok)
    rows = pl.BlockSpec((tm, D_MODEL), lambda i: (i, 0))

    def body(ga_ref, gb_ref, ya_ref, yb_ref, o_ref):
        o_ref[...] = _merge_fn(ga_ref[...], gb_ref[...], ya_ref[...], yb_ref[...]).astype(o_ref.dtype)

    return pl.pallas_call(
        body, name=name, grid=(n_tok // tm,),
        in_specs=[rows, pl.BlockSpec((tm, D_MODEL), lambda i: (i, 1)), rows, rows], out_specs=rows,
        out_shape=jax.ShapeDtypeStruct(ya.shape, MXU_DTYPE),
        compiler_params=_params(5 * _nbytes((tm, D_MODEL), F32), 4 * _nbytes((tm, D_MODEL), F32)),
    )(gab, gab, ya, yb)


def _merge_bwd(gab, ya, yb, dout, *, name, tm=512):
    n_tok = ya.shape[0]
    tm = min(tm, n_tok)
    rows = pl.BlockSpec((tm, D_MODEL), lambda i: (i, 0))

    def body(ga_ref, gb_ref, ya_ref, yb_ref, do_ref, *out_refs):
        _, pull = jax.vjp(_merge_fn, ga_ref[...], gb_ref[...], ya_ref[...], yb_ref[...])
        for ref, val in zip(out_refs, pull(do_ref[...])):
            ref[...] = val.astype(ref.dtype)

    shape = jax.ShapeDtypeStruct(ya.shape, MXU_DTYPE)
    return pl.pallas_call(
        body, name=name, grid=(n_tok // tm,),
        in_specs=[rows, pl.BlockSpec((tm, D_MODEL), lambda i: (i, 1)), rows, rows, rows],
        out_specs=(rows, rows, rows, rows), out_shape=(shape, shape, shape, shape),
        compiler_params=_params(7 * _nbytes((tm, D_MODEL), F32), 6 * _nbytes((tm, D_MODEL), F32)),
    )(gab, gab, ya, yb, dout)


ADA_SHARD = N_MOD * D_MODEL // N_SHARD


def _ada_fwd(c_pad, w_s, b_s, *, name):
    n_r = c_pad.shape[0]

    def body(c_ref, w_ref, b_ref, o_ref):
        sc = jax.nn.silu(c_ref[...]).astype(MXU_DTYPE)
        o_ref[0] = lax.dot_general(sc, w_ref[0].astype(MXU_DTYPE), _NN, preferred_element_type=F32) + b_ref[0]

    return pl.pallas_call(
        body, name=name, grid=(N_SHARD,),
        in_specs=[pl.BlockSpec((n_r, D_MODEL), lambda s: (0, 0)),
                  pl.BlockSpec((1, D_MODEL, ADA_SHARD), lambda s: (s, 0, 0)),
                  pl.BlockSpec((1, 1, ADA_SHARD), lambda s: (s, 0, 0))],
        out_specs=pl.BlockSpec((1, n_r, ADA_SHARD), lambda s: (s, 0, 0)),
        out_shape=jax.ShapeDtypeStruct((N_SHARD, n_r, ADA_SHARD), F32),
        compiler_params=_params(_nbytes((D_MODEL, ADA_SHARD), w_s.dtype), 1 << 20),
    )(c_pad, w_s, b_s)


def _ada_bwd(c_pad, dmod_s, *, name):
    n_r = c_pad.shape[0]

    def body(c_ref, d_ref, dw_ref, db_ref):
        sc = jax.nn.silu(c_ref[...])
        dm = d_ref[0]
        dw_ref[0] = _fdot(sc, dm, _TN)
        db_ref[0] = jnp.sum(dm, axis=0, keepdims=True)

    return pl.pallas_call(
        body, name=name, grid=(N_SHARD,),
        in_specs=[pl.BlockSpec((n_r, D_MODEL), lambda s: (0, 0)), pl.BlockSpec((1, n_r, ADA_SHARD), lambda s: (s, 0, 0))],
        out_specs=(pl.BlockSpec((1, D_MODEL, ADA_SHARD), lambda s: (s, 0, 0)),
                   pl.BlockSpec((1, 1, ADA_SHARD), lambda s: (s, 0, 0))),
        out_shape=(jax.ShapeDtypeStruct((N_SHARD, D_MODEL, ADA_SHARD), F32),
                   jax.ShapeDtypeStruct((N_SHARD, 1, ADA_SHARD), F32)),
        compiler_params=_params(_nbytes((D_MODEL, ADA_SHARD), F32), 2 * _nbytes((D_MODEL, ADA_SHARD), F32)),
    )(c_pad, dmod_s)


def _heads(t, n_b, n_s):
    return t.reshape(n_b, n_s, DN_HEADS, DN_HEAD_DIM).transpose(0, 2, 1, 3)


def _unheads(t):
    n_b, _, n_s, _ = t.shape
    return t.transpose(0, 2, 1, 3).reshape(n_b, n_s, DN_WIDTH)


def _block_diag(blocks):
    n_g, n_r, n_c = blocks.shape
    eye = jnp.eye(n_g, dtype=blocks.dtype)
    return (blocks[:, :, None, :] * eye[:, None, :, None]).reshape(n_g * n_r, n_g * n_c)


def _diag_blocks(mat, n_r, n_c):
    n_g = mat.shape[0] // n_r
    m4 = mat.reshape(n_g, n_r, n_g, n_c)
    idx = jnp.arange(n_g)
    return m4[idx, :, idx, :]


def _local_step(x, c, target, wts):
    n_b, n_s, _ = x.shape
    n_tok = n_b * n_s
    flat = lambda t: t.reshape(n_tok, t.shape[-1])
    unflat = lambda t: t.reshape(n_b, n_s, t.shape[-1])
    n_chunks = n_s // CHUNK

    c_pad = jnp.zeros((SUBLANES, D_MODEL), F32).at[:n_b].set(c)
    mod_s = _ada_fwd(c_pad, wts["w_ada"], wts["b_ada"], name="ada_fwd")
    mod = mod_s.transpose(1, 0, 2).reshape(SUBLANES, N_MOD * D_MODEL)[:n_b]
    sh1, sc1, gt1, sh2, sc2, gt2, sh3, sc3, gt3 = [m[:, None, :] for m in jnp.split(mod, N_MOD, axis=-1)]

    a1 = _pre(x, None, None, wts["g_ffn1"], sh1, sc1, 0.0, name="pre1")
    f1, ffn1_saved = _ffn_fwd(flat(a1), wts["w1_ffn1"], wts["w3_ffn1"], wts["w2_ffn1"], "ffn1")
    x1, a2 = _pre(x, unflat(f1), gt1, wts["g_mix"], sh2, sc2, 0.5, name="pre2")
    u = flat(a2)[None]
    p_qkv = _mm_act([(u, wts["w_qkv"])], "nn", name="in_qkv")[0]
    p_z = _mm_act([(u, wts["w_z"])], "nn", name="in_z")[0]
    p_gab = _mm_act([(u, wts["w_gab"])], "nn", name="in_gab")[0]
    p_s5 = _mm_act([(u, wts["w_s5"])], "nn", name="in_s5")[0]
    p_ba = _mm_act([(u, wts["w_ba"])], "nn", name="in_ba")[0]

    qkv_c = _conv_fwd(unflat(p_qkv), wts["conv_qkv"], name="conv_fwd")
    qh, kh, vh = [_heads(t, n_b, n_s) for t in jnp.split(qkv_c, 3, axis=-1)]
    zh = _heads(p_z, n_b, n_s)
    ba = p_ba.reshape(n_b, n_s, BA_PAD)
    bl = ba[:, :, :DN_HEADS].transpose(0, 2, 1)[..., None]
    lac = ba[:, :, DN_HEADS:2 * DN_HEADS].transpose(0, 2, 1)[..., None]
    lar = lac.reshape(n_b, DN_HEADS, n_chunks, 1, CHUNK)
    a_log, dt_bias = wts["a_log"], wts["dt_bias"]
    dn_in = (qh, kh, vh, bl, lac, lar, a_log, dt_bias)
    qd, kd, uc, wc, attn, g_last = _dn_prep(*dn_in, name="dn_prep")
    o, states = _dn_scan(qd, kd, uc, wc, attn, g_last, name="dn_scan")
    og = _dn_post(o, zh, wts["g_onorm"], name="dn_post")
    og_t = _unheads(og).reshape(1, n_tok, DN_WIDTH)
    ya = _mm_act([(og_t, wts["w_proj_a"])], "nn", name="proj_a")[0]

    s5p_in = (wts["lam_re"], wts["lam_im"], wts["log_step"], wts["bt_re"], wts["bt_im"], wts["c_im"])
    lb_re, lb_im, bb_re, bb_im, c_neg = _s5_prep(*s5p_in, name="s5_prep")
    wb_re, wb_im = _block_diag(bb_re), _block_diag(bb_im)
    wb = jnp.concatenate([wb_re, wb_im], axis=1)[None]
    wc_re = _block_diag(wts["c_re"].transpose(0, 2, 1))
    wc_im = _block_diag(c_neg.transpose(0, 2, 1))
    lbr, lbi = lb_re.reshape(1, S5_LANES), lb_im.reshape(1, S5_LANES)
    bu = _mm_act([(p_s5[None], wb)], "nn", name="s5_bu")[0]
    x_re, x_im = _s5_scan(unflat(bu), lbr, lbi, name="s5_scan")
    xr_t, xi_t = x_re.reshape(1, n_tok, S5_LANES), x_im.reshape(1, n_tok, S5_LANES)
    ymm = _mm_act([(xr_t, wc_re[None]), (xi_t, wc_im[None])], "nn", name="s5_y")[0]
    y2 = _s5_out(ymm, p_s5, wts["d_skip"], wts["w_glu"], wts["b_glu"], name="s5_out")
    yb = _mm_act([(y2[None], wts["w_proj_b"])], "nn", name="proj_b")[0]

    merged = _merge(p_gab, ya, yb, name="merge")
    m_out = _mm_act([(merged[None], wts["w_out"])], "nn", name="mix_out")[0]
    x2, a3 = _pre(x1, unflat(m_out), gt2, wts["g_ffn2"], sh3, sc3, 1.0, name="pre3")
    f3, ffn2_saved = _ffn_fwd(flat(a3), wts["w1_ffn2"], wts["w3_ffn2"], wts["w2_ffn2"], "ffn2")

    g = {}
    loss, dx2_res, df3, dgt3, g["g_final"] = _final(x2, unflat(f3), gt3, wts["g_final"], target, name="final")
    da3, g["w1_ffn2"], g["w3_ffn2"], g["w2_ffn2"] = _ffn_bwd(
        flat(a3), wts["w1_ffn2"], wts["w3_ffn2"], wts["w2_ffn2"], ffn2_saved, flat(df3), "ffn2")
    dx1_res, dm_out, dgt2, g["g_ffn2"], dsh3, dsc3 = _pre_bwd(
        x1, unflat(m_out), gt2, wts["g_ffn2"], sh3, sc3, 1.0, unflat(da3), dx2_res, name="pre3_bwd")
    dm_out = flat(dm_out)[None]
    dmerged = _mm_act([(dm_out, wts["w_out"])], "nt", name="mix_out_bwd")[0]
    g["w_out"] = _mm_tn(merged[None], dm_out, name="dw_out")[0]
    dga, dgb, dya, dyb = _merge_bwd(p_gab, ya, yb, dmerged, name="merge_bwd")

    dy2 = _mm_act([(dyb[None], wts["w_proj_b"])], "nt", name="proj_b_bwd")[0]
    g["w_proj_b"] = _mm_tn(y2[None], dyb[None], name="dw_proj_b")[0]
    dymm, du_skip, g["d_skip"], g["w_glu"], g["b_glu"] = _s5_out_bwd(
        ymm, p_s5, wts["d_skip"], wts["w_glu"], wts["b_glu"], dy2, name="s5_out_bwd")
    wc_cat = jnp.concatenate([wc_re, wc_im], axis=0)[None]
    dxs = _mm_act([(dymm[None], wc_cat)], "nt", name="s5_y_bwd")[0]
    dwc_re = _mm_tn(xr_t, dymm[None], name="dwc_re")[0]
    dwc_im = _mm_tn(xi_t, dymm[None], name="dwc_im")[0]
    a_re, a_im, dlb_re, dlb_im = _s5_scan_bwd(unflat(dxs), x_re, x_im, lbr, lbi, name="s5_scan_bwd")
    ar_t, ai_t = a_re.reshape(1, n_tok, S5_LANES), a_im.reshape(1, n_tok, S5_LANES)
    dp_s5 = _mm_act([(ar_t, wb_re[None]), (ai_t, wb_im[None])], "nt", add=du_skip, out_dtype=MXU_DTYPE, name="s5_bu_bwd")[0]
    dwb_re = _mm_tn(p_s5[None], ar_t, name="dwb_re")[0]
    dwb_im = _mm_tn(p_s5[None], ai_t, name="dwb_im")[0]
    g["c_re"] = _diag_blocks(dwc_re, S5_STATE, S5_GROUP_CH).transpose(0, 2, 1)
    s5_cts = (dlb_re.reshape(lb_re.shape), dlb_im.reshape(lb_im.shape),
              _diag_blocks(dwb_re, S5_GROUP_CH, S5_STATE), _diag_blocks(dwb_im, S5_GROUP_CH, S5_STATE),
              _diag_blocks(dwc_im, S5_STATE, S5_GROUP_CH).transpose(0, 2, 1))
    g["lam_re"], g["lam_im"], g["log_step"], g["bt_re"], g["bt_im"], g["c_im"] = _s5_prep_bwd(
        *s5p_in, s5_cts, name="s5_prep_bwd")

    dog = _mm_act([(dya[None], wts["w_proj_a"])], "nt", name="proj_a_bwd")[0]
    g["w_proj_a"] = _mm_tn(og_t, dya[None], name="dw_proj_a")[0]
    do, dzh, g["g_onorm"] = _dn_post_bwd(o, zh, wts["g_onorm"], _heads(dog, n_b, n_s), name="dn_post_bwd")
    scan_cts = _dn_scan_bwd(qd, kd, uc, wc, attn, g_last, states, do, name="dn_scan_bwd")
    dqh, dkh, dvh, dbl, dlac, dlar, g["a_log"], g["dt_bias"] = _dn_prep_bwd(*dn_in, scan_cts, name="dn_prep_bwd")
    dqkv_c = jnp.concatenate([_unheads(t) for t in (dqh, dkh, dvh)], axis=-1)
    dqkv, g["conv_qkv"] = _conv_bwd(unflat(p_qkv), wts["conv_qkv"], dqkv_c, name="conv_bwd")
    dla = dlac[..., 0] + dlar.reshape(n_b, DN_HEADS, n_s)
    dba = jnp.concatenate([dbl[..., 0].transpose(0, 2, 1), dla.transpose(0, 2, 1),
                           jnp.zeros((n_b, n_s, BA_PAD - 2 * DN_HEADS), F32)], axis=-1).astype(MXU_DTYPE)
    dz = _unheads(dzh)

    dps = {"w_qkv": flat(dqkv)[None], "w_z": flat(dz)[None], "w_ga": dga[None], "w_gb": dgb[None],
           "w_s5": dp_s5[None], "w_ba": flat(dba)[None]}
    w_ga, w_gb = wts["w_gab"][:, :, :D_MODEL], wts["w_gab"][:, :, D_MODEL:]
    w_of = dict(wts, w_ga=w_ga, w_gb=w_gb)
    du = _mm_act([(dps[k], w_of[k]) for k in dps], "nt", name="in_bwd")[0]
    for k in dps:
        g[k] = _mm_tn(u, dps[k], name=f"d{k}")[0]
    dx0_res, df1, dgt1, g["g_mix"], dsh2, dsc2 = _pre_bwd(
        x, unflat(f1), gt1, wts["g_mix"], sh2, sc2, 0.5, unflat(du), dx1_res, name="pre2_bwd")
    da1, g["w1_ffn1"], g["w3_ffn1"], g["w2_ffn1"] = _ffn_bwd(
        flat(a1), wts["w1_ffn1"], wts["w3_ffn1"], wts["w2_ffn1"], ffn1_saved, flat(df1), "ffn1")
    grad_x, g["g_ffn1"], dsh1, dsc1 = _pre_bwd(
        x, None, None, wts["g_ffn1"], sh1, sc1, 0.0, unflat(da1), dx0_res, name="pre1_bwd")

    dmod = jnp.concatenate([t[:, 0, :] for t in (dsh1, dsc1, dgt1, dsh2, dsc2, dgt2, dsh3, dsc3, dgt3)], axis=-1)
    dmod_pad = jnp.zeros((SUBLANES, N_MOD * D_MODEL), F32).at[:n_b].set(dmod)
    dmod_s = dmod_pad.reshape(SUBLANES, N_SHARD, ADA_SHARD).transpose(1, 0, 2)
    g["w_ada"], g["b_ada"] = _ada_bwd(c_pad, dmod_s, name="ada_bwd")
    return loss, grad_x, g


IN_SPLITS = (("w_qkv", 3 * DN_WIDTH), ("w_z", DN_WIDTH), ("w_ba", 2 * DN_HEADS), ("w_s5", S5_WIDTH),
             ("w_ga", D_MODEL), ("w_gb", D_MODEL))
SHARDED = ("w_ada", "w1_ffn1", "w3_ffn1", "w2_ffn1", "w_in", "conv_qkv", "w_glu", "w_proj_a", "w_proj_b", "w_out",
           "w1_ffn2", "w3_ffn2", "w2_ffn2")
COLUMN_SHARDED = ("w_ada", "w1_ffn1", "w3_ffn1", "w_in", "conv_qkv", "w_proj_a", "w_proj_b", "w1_ffn2", "w3_ffn2")


def _cat_columns(stack):
    return stack.transpose(1, 0, 2).reshape(stack.shape[1], N_SHARD * stack.shape[2])


def _split_columns(full):
    n_r, n_c = full.shape
    return full.reshape(n_r, N_SHARD, n_c // N_SHARD).transpose(1, 0, 2)


def _gathered_weights(st, rep):
    w = {k: st[k] for k in ("w_ada", "w1_ffn1", "w3_ffn1", "w2_ffn1", "w1_ffn2", "w3_ffn2", "w2_ffn2")}
    w["b_ada"] = rep["b_ada"].reshape(N_SHARD, 1, ADA_SHARD)
    for k in ("g_ffn1", "g_mix", "g_ffn2", "g_final"):
        w[k] = rep[k].reshape(1, D_MODEL)
    w_in = _cat_columns(st["w_in"])
    start = 0
    for k, size in IN_SPLITS:
        w[k] = w_in[None, :, start:start + size]
        start += size
    w["w_gab"] = jnp.concatenate([w.pop("w_ga"), w.pop("w_gb")], axis=-1)
    w["w_ba"] = jnp.pad(w["w_ba"], ((0, 0), (0, 0), (0, BA_PAD - 2 * DN_HEADS)))
    w["conv_qkv"] = _cat_columns(st["conv_qkv"])
    w["a_log"] = rep["a_log"].reshape(DN_HEADS, 1, 1)
    w["dt_bias"] = rep["dt_bias"].reshape(DN_HEADS, 1, 1)
    w["g_onorm"] = rep["g_onorm"].reshape(1, DN_HEAD_DIM)
    w["lam_re"] = rep["lam_re"].reshape(S5_GROUPS, 1, S5_STATE)
    w["lam_im"] = rep["lam_im"].reshape(S5_GROUPS, 1, S5_STATE)
    w["log_step"] = rep["log_step"].reshape(S5_GROUPS, 1, 1)
    w["bt_re"] = rep["b_re"][0].transpose(0, 2, 1)
    w["bt_im"] = rep["b_im"][0].transpose(0, 2, 1)
    w["c_re"] = rep["c_re"][0]
    w["c_im"] = rep["c_im"][0]
    w["d_skip"] = rep["d_skip"].reshape(1, S5_WIDTH)
    w["b_glu"] = rep["b_glu"].reshape(1, S5_WIDTH)
    w["w_glu"] = st["w_glu"].reshape(S5_WIDTH, S5_WIDTH)
    w["w_proj_a"] = _cat_columns(st["w_proj_a"])[None]
    w["w_proj_b"] = _cat_columns(st["w_proj_b"])[None]
    w["w_out"] = st["w_out"].reshape(1, D_MODEL, D_MODEL)
    return w


def _grads_to_problem_layout(g):
    st = {k: g[k] for k in ("w_ada", "w1_ffn1", "w3_ffn1", "w2_ffn1", "w1_ffn2", "w3_ffn2", "w2_ffn2")}
    w_in = jnp.concatenate([g[k][:, :size] for k, size in IN_SPLITS], axis=1)
    st["w_in"] = _split_columns(w_in)
    st["w_glu"] = g["w_glu"].reshape(N_SHARD, S5_WIDTH // N_SHARD, S5_WIDTH)
    st["w_proj_a"] = _split_columns(g["w_proj_a"])
    st["w_proj_b"] = _split_columns(g["w_proj_b"])
    st["w_out"] = g["w_out"].reshape(N_SHARD, D_MODEL // N_SHARD, D_MODEL)
    small = {
        "b_ada": g["b_ada"].reshape(1, N_MOD * D_MODEL),
        "g_ffn1": g["g_ffn1"], "g_mix": g["g_mix"], "g_ffn2": g["g_ffn2"], "g_final": g["g_final"].reshape(D_MODEL),
        "conv_qkv": g["conv_qkv"][None],
        "a_log": g["a_log"].reshape(1, DN_HEADS), "dt_bias": g["dt_bias"].reshape(1, DN_HEADS),
        "g_onorm": g["g_onorm"],
        "lam_re": g["lam_re"].reshape(1, S5_GROUPS, S5_STATE), "lam_im": g["lam_im"].reshape(1, S5_GROUPS, S5_STATE),
        "log_step": g["log_step"].reshape(1, S5_GROUPS),
        "b_re": g["bt_re"].transpose(0, 2, 1)[None], "b_im": g["bt_im"].transpose(0, 2, 1)[None],
        "c_re": g["c_re"][None], "c_im": g["c_im"][None],
        "d_skip": g["d_skip"], "b_glu": g["b_glu"],
    }
    return st, small


ELEMENTWISE_BLOCK_BYTES = 1 << 20


def _row_tile(n_rows, n_cols, n_lead=1):
    best = None
    for t in range(SUBLANES, n_rows + 1, SUBLANES):
        if n_rows % t == 0 and n_lead * t * n_cols * 4 <= ELEMENTWISE_BLOCK_BYTES:
            best = t
    return best if best is not None else n_rows


def _add_sibling_half(g4, recv, my_c, *, name):
    n_sh, _, n_h, n_c = g4.shape
    th = _row_tile(n_h, n_c)

    def body(c_ref, g_ref, r_ref, o_ref):
        o_ref[0] = (g_ref[0, 0] + r_ref[0]).astype(o_ref.dtype)

    grid_spec = pltpu.PrefetchScalarGridSpec(
        num_scalar_prefetch=1, grid=(n_sh, n_h // th),
        in_specs=[pl.BlockSpec((1, 1, th, n_c), lambda s, i, c_ref: (s, c_ref[0], i, 0)),
                  pl.BlockSpec((1, th, n_c), lambda s, i, c_ref: (s, i, 0))],
        out_specs=pl.BlockSpec((1, th, n_c), lambda s, i, c_ref: (s, i, 0)))
    return pl.pallas_call(
        body, name=name, grid_spec=grid_spec, out_shape=jax.ShapeDtypeStruct((n_sh, n_h, n_c), MXU_DTYPE),
        compiler_params=_params(3 * _nbytes((th, n_c), F32)),
    )(my_c, g4, recv)


def _sum_slots(parts, *, name):
    n_p, n_r, n_c = parts.shape
    th = _row_tile(n_r, n_c, n_p)

    def body(p_ref, o_ref):
        total = p_ref[0].astype(F32)
        for k in range(1, n_p):
            total = total + p_ref[k].astype(F32)
        o_ref[...] = total

    return pl.pallas_call(
        body, name=name, grid=(n_r // th,),
        in_specs=[pl.BlockSpec((n_p, th, n_c), lambda i: (0, i, 0))],
        out_specs=pl.BlockSpec((th, n_c), lambda i: (i, 0)),
        out_shape=jax.ShapeDtypeStruct((n_r, n_c), F32),
        compiler_params=_params((n_p + 1) * _nbytes((th, n_c), F32)),
    )(parts)


def _adamw(w, g, m, v, *, name):
    n_r, n_c = w.shape
    th = _row_tile(n_r, n_c)
    bias1 = 1.0 - ADAM_B1 ** ADAM_STEP
    bias2 = 1.0 - ADAM_B2 ** ADAM_STEP

    def body(w_ref, g_ref, m_ref, v_ref, d_ref, mo_ref, vo_ref):
        gv = g_ref[...]
        m_new = ADAM_B1 * m_ref[...] + (1.0 - ADAM_B1) * gv
        v_new = ADAM_B2 * v_ref[...] + (1.0 - ADAM_B2) * jnp.square(gv)
        d_ref[...] = -ADAM_LR * ((m_new / bias1) / (jnp.sqrt(v_new / bias2) + ADAM_EPS) + ADAM_WD * w_ref[...])
        mo_ref[...] = m_new
        vo_ref[...] = v_new

    spec = pl.BlockSpec((th, n_c), lambda i: (i, 0))
    shape = jax.ShapeDtypeStruct((n_r, n_c), F32)
    return pl.pallas_call(
        body, name=name, grid=(n_r // th,), in_specs=[spec] * 4, out_specs=(spec,) * 3, out_shape=(shape,) * 3,
        compiler_params=_params(7 * _nbytes((th, n_c), F32)),
    )(w, g, m, v)


CHIP_FLIPS = ((1, 0), (0, 1), (1, 1))
DEVICE_FLIPS = tuple((fx, fy, fc) for fx in (0, 1) for fy in (0, 1) for fc in (0, 1))[1:]


def _exchange(ins, out_shapes, plan, n_local, n_remote, *, name):
    n_in, n_out = len(ins), len(out_shapes)

    def body(*refs):
        in_refs, out_refs = refs[:n_in], refs[n_in:n_in + n_out]
        send_sems, recv_sems, local_sems = refs[n_in + n_out:]
        me = (lax.axis_index("x"), lax.axis_index("y"), lax.axis_index("c"))
        local, remote = plan(in_refs, out_refs, me)
        assert len(local) == n_local and len(remote) == n_remote
        here = [pltpu.make_async_copy(src, dst, local_sems.at[i]) for i, (src, dst) in enumerate(local)]
        for cp in here:
            cp.start()
        sends = [pltpu.make_async_remote_copy(src_ref=src, dst_ref=dst, send_sem=send_sems.at[i], recv_sem=recv_sems.at[i],
                                              device_id=peer, device_id_type=pl.DeviceIdType.MESH)
                 for i, (src, dst, _, peer) in enumerate(remote)]
        for cp in sends:
            cp.start()
        for i, (src, _, landing, peer) in enumerate(remote):
            pltpu.make_async_remote_copy(src_ref=src, dst_ref=landing, send_sem=send_sems.at[i], recv_sem=recv_sems.at[i],
                                         device_id=peer, device_id_type=pl.DeviceIdType.MESH).wait_recv()
        for cp in sends:
            cp.wait_send()
        for cp in here:
            cp.wait()

    any_spec = pl.BlockSpec(memory_space=pl.ANY)
    return pl.pallas_call(
        body, name=name, in_specs=[any_spec] * n_in, out_specs=tuple([any_spec] * n_out), out_shape=tuple(out_shapes),
        scratch_shapes=[pltpu.SemaphoreType.DMA((n_remote,)), pltpu.SemaphoreType.DMA((n_remote,)),
                        pltpu.SemaphoreType.DMA((max(n_local, 1),))],
    )(*ins)


def _gather_shards(shards, *, name):
    n = len(shards)

    def plan(in_refs, out_refs, me):
        x, y, c = me
        mine = 2 * x + y
        local = [(in_refs[k], out_refs[k].at[mine]) for k in range(n)]
        remote = []
        for fx, fy in CHIP_FLIPS:
            px, py = x ^ fx, y ^ fy
            for k in range(n):
                remote.append((in_refs[k], out_refs[k].at[mine], out_refs[k].at[2 * px + py], (px, py, c)))
        return local, remote

    shapes = [jax.ShapeDtypeStruct((N_SHARD,) + s.shape, s.dtype) for s in shards]
    return _exchange(shards, shapes, plan, n, len(CHIP_FLIPS) * n, name=name)


def _swap_sibling_halves(g4s, *, name):
    n = len(g4s)

    def plan(in_refs, out_refs, me):
        x, y, c = me
        remote = [(in_refs[k].at[:, 1 - c], out_refs[k], out_refs[k], (x, y, 1 - c)) for k in range(n)]
        return [], remote

    shapes = [jax.ShapeDtypeStruct((a.shape[0],) + a.shape[2:], a.dtype) for a in g4s]
    return _exchange(g4s, shapes, plan, 0, n, name=name)


def _scatter_to_chips(hs, *, name):
    n = len(hs)

    def plan(in_refs, out_refs, me):
        x, y, c = me
        mine = 2 * x + y
        local = [(in_refs[k].at[mine], out_refs[k].at[mine]) for k in range(n)]
        remote = []
        for fx, fy in CHIP_FLIPS:
            px, py = x ^ fx, y ^ fy
            peer = 2 * px + py
            for k in range(n):
                remote.append((in_refs[k].at[peer], out_refs[k].at[mine], out_refs[k].at[peer], (px, py, c)))
        return local, remote

    shapes = [jax.ShapeDtypeStruct(a.shape, a.dtype) for a in hs]
    return _exchange(hs, shapes, plan, n, len(CHIP_FLIPS) * n, name=name)


def _join_sibling_halves(rs, *, name):
    n = len(rs)

    def plan(in_refs, out_refs, me):
        x, y, c = me
        local = [(in_refs[k], out_refs[k].at[c]) for k in range(n)]
        remote = [(in_refs[k], out_refs[k].at[c], out_refs[k].at[1 - c], (x, y, 1 - c)) for k in range(n)]
        return local, remote

    shapes = [jax.ShapeDtypeStruct((2,) + a.shape, a.dtype) for a in rs]
    return _exchange(rs, shapes, plan, n, n, name=name)


def _gather_all_devices(packed, *, name):
    def plan(in_refs, out_refs, me):
        x, y, c = me
        mine = 4 * x + 2 * y + c
        remote = []
        for fx, fy, fc in DEVICE_FLIPS:
            px, py, pc = x ^ fx, y ^ fy, c ^ fc
            remote.append((in_refs[0], out_refs[0].at[mine], out_refs[0].at[4 * px + 2 * py + pc], (px, py, pc)))
        return [(in_refs[0], out_refs[0].at[mine])], remote

    shape = jax.ShapeDtypeStruct((2 * N_SHARD,) + packed.shape, packed.dtype)
    return _exchange([packed], [shape], plan, 1, len(DEVICE_FLIPS), name=name)[0]


WEIGHT_NAMES = ("w_ada", "b_ada", "g_ffn1", "w1_ffn1", "w3_ffn1", "w2_ffn1", "g_mix", "w_in", "conv_qkv", "a_log",
                "dt_bias", "g_onorm", "lam_re", "lam_im", "log_step", "b_re", "b_im", "c_re", "c_im", "d_skip", "w_glu",
                "b_glu", "w_proj_a", "w_proj_b", "w_out", "g_ffn2", "w1_ffn2", "w3_ffn2", "w2_ffn2", "g_final")
LARGE = tuple(n for n in SHARDED if n != "conv_qkv")
SMALL = tuple(n for n in WEIGHT_NAMES if n not in LARGE)
PACK_ROW = SUBLANES * LANES


def _pack(arrays):
    flat = jnp.concatenate([a.reshape(-1) for a in arrays])
    n_pad = -flat.shape[0] % PACK_ROW
    return jnp.pad(flat, (0, n_pad)).reshape(-1, LANES)


def _unpack(packed, shapes):
    flat = packed.reshape(-1)
    out, start = [], 0
    for s in shapes:
        size = math.prod(s)
        out.append(flat[start:start + size].reshape(s))
        start += size
    return out


def _step(x, c, target, weights, m_in, v_in):
    xi, yi, ci = lax.axis_index("x"), lax.axis_index("y"), lax.axis_index("c")
    my_chip = 2 * xi + yi

    shards = [weights[n][0].astype(F32 if n == "conv_qkv" else MXU_DTYPE) for n in SHARDED]
    stacks = dict(zip(SHARDED, _gather_shards(shards, name="gather_weights")))
    rep = {n: weights[n] for n in WEIGHT_NAMES if n not in SHARDED}
    loss, grad_x, g = _local_step(x, c, target, _gathered_weights(stacks, rep))
    g_stacks, g_small = _grads_to_problem_layout(g)

    g4s = [g_stacks[n].reshape(N_SHARD, 2, g_stacks[n].shape[1] // 2, g_stacks[n].shape[2]) for n in LARGE]
    from_sibling = _swap_sibling_halves(g4s, name="swap_sibling_halves")
    c_idx = jnp.reshape(ci, (1,)).astype(jnp.int32)
    chip_sums = [_add_sibling_half(a, r, c_idx, name=f"chip_sum_{n}") for n, a, r in zip(LARGE, g4s, from_sibling)]
    from_chips = _scatter_to_chips(chip_sums, name="scatter_to_chips")
    reduced = [_sum_slots(p, name=f"sum_chips_{n}") for n, p in zip(LARGE, from_chips)]
    joined = _join_sibling_halves(reduced, name="join_sibling_halves")
    grads = {n: j.reshape(1, 2 * j.shape[1], j.shape[2]) for n, j in zip(LARGE, joined)}

    small_shapes = [g_small[n].shape for n in SMALL] + [(1, 1)]
    packed = _pack([g_small[n] for n in SMALL] + [loss])
    total = _sum_slots(_gather_all_devices(packed, name="gather_small"), name="sum_small")
    *small_grads, loss_sum = _unpack(total, small_shapes)
    grads.update(zip(SMALL, small_grads))
    n_conv = weights["conv_qkv"].shape[-1]
    grads["conv_qkv"] = lax.dynamic_slice_in_dim(grads["conv_qkv"], my_chip * n_conv, n_conv, axis=2)

    delta, new_m, new_v = {}, {}, {}
    for n in LARGE + ("conv_qkv",):
        two_d = lambda a: a.reshape(-1, a.shape[-1])
        outs = _adamw(two_d(weights[n]), two_d(grads[n]), two_d(m_in[n]), two_d(v_in[n]), name=f"adamw_{n}")
        delta[n], new_m[n], new_v[n] = [o.reshape(weights[n].shape) for o in outs]
    packed_names = tuple(n for n in SMALL if n != "conv_qkv")
    shapes = [weights[n].shape for n in packed_names]
    outs = _adamw(*[_pack([d[n] for n in packed_names]) for d in (weights, grads, m_in, v_in)], name="adamw_small")
    for d, o in zip((delta, new_m, new_v), outs):
        d.update(zip(packed_names, _unpack(o, shapes)))
    return (loss_sum.reshape(()), grad_x, *[grads[n] for n in WEIGHT_NAMES], *[delta[n] for n in WEIGHT_NAMES],
            *[new_m[n] for n in WEIGHT_NAMES], *[new_v[n] for n in WEIGHT_NAMES])


def kernel(x, c, w_ada, b_ada, g_ffn1, w1_ffn1, w3_ffn1, w2_ffn1, g_mix, w_in, conv_qkv, a_log, dt_bias, g_onorm, lam_re, lam_im, log_step, b_re, b_im, c_re, c_im, d_skip, w_glu, b_glu, w_proj_a, w_proj_b, w_out, g_ffn2, w1_ffn2, w3_ffn2, w2_ffn2, g_final, loss_target, m_w_ada, m_b_ada, m_g_ffn1, m_w1_ffn1, m_w3_ffn1, m_w2_ffn1, m_g_mix, m_w_in, m_conv_qkv, m_a_log, m_dt_bias, m_g_onorm, m_lam_re, m_lam_im, m_log_step, m_b_re, m_b_im, m_c_re, m_c_im, m_d_skip, m_w_glu, m_b_glu, m_w_proj_a, m_w_proj_b, m_w_out, m_g_ffn2, m_w1_ffn2, m_w3_ffn2, m_w2_ffn2, m_g_final, v_w_ada, v_b_ada, v_g_ffn1, v_w1_ffn1, v_w3_ffn1, v_w2_ffn1, v_g_mix, v_w_in, v_conv_qkv, v_a_log, v_dt_bias, v_g_onorm, v_lam_re, v_lam_im, v_log_step, v_b_re, v_b_im, v_c_re, v_c_im, v_d_skip, v_w_glu, v_b_glu, v_w_proj_a, v_w_proj_b, v_w_out, v_g_ffn2, v_w1_ffn2, v_w3_ffn2, v_w2_ffn2, v_g_final):
    w_vals = (w_ada, b_ada, g_ffn1, w1_ffn1, w3_ffn1, w2_ffn1, g_mix, w_in, conv_qkv, a_log, dt_bias, g_onorm, lam_re, lam_im, log_step, b_re, b_im, c_re, c_im, d_skip, w_glu, b_glu, w_proj_a, w_proj_b, w_out, g_ffn2, w1_ffn2, w3_ffn2, w2_ffn2, g_final)
    m_vals = (m_w_ada, m_b_ada, m_g_ffn1, m_w1_ffn1, m_w3_ffn1, m_w2_ffn1, m_g_mix, m_w_in, m_conv_qkv, m_a_log, m_dt_bias, m_g_onorm, m_lam_re, m_lam_im, m_log_step, m_b_re, m_b_im, m_c_re, m_c_im, m_d_skip, m_w_glu, m_b_glu, m_w_proj_a, m_w_proj_b, m_w_out, m_g_ffn2, m_w1_ffn2, m_w3_ffn2, m_w2_ffn2, m_g_final)
    v_vals = (v_w_ada, v_b_ada, v_g_ffn1, v_w1_ffn1, v_w3_ffn1, v_w2_ffn1, v_g_mix, v_w_in, v_conv_qkv, v_a_log, v_dt_bias, v_g_onorm, v_lam_re, v_lam_im, v_log_step, v_b_re, v_b_im, v_c_re, v_c_im, v_d_skip, v_w_glu, v_b_glu, v_w_proj_a, v_w_proj_b, v_w_out, v_g_ffn2, v_w1_ffn2, v_w3_ffn2, v_w2_ffn2, v_g_final)
    return _step(x, c, loss_target, dict(zip(WEIGHT_NAMES, w_vals)), dict(zip(WEIGHT_NAMES, m_vals)),
                 dict(zip(WEIGHT_NAMES, v_vals)))
```

```python
import functools
import math

import jax
import jax.numpy as jnp
from jax import lax
from jax.experimental import pallas as pl
from jax.experimental.pallas import tpu as pltpu

F32 = jnp.float32
BF16 = jnp.bfloat16
MXU_DTYPE = BF16

D_MODEL = 1024
D_FF = 2816
DN_HEADS = 8
DN_HEAD_DIM = 64
DN_WIDTH = DN_HEADS * DN_HEAD_DIM
CONV_WIDTH = 4
CHUNK = 64
S5_GROUP_CH = 16
S5_GROUPS = 32
S5_WIDTH = S5_GROUPS * S5_GROUP_CH
S5_STATE = 64
S5_LANES = S5_GROUPS * S5_STATE
N_MOD = 9
EPS = 1e-6
N_SHARD = 4
FF_SHARD = D_FF // N_SHARD
BA_PAD = 128

ADAM_LR = 0.001
ADAM_B1 = 0.9
ADAM_B2 = 0.999
ADAM_EPS = 1e-08
ADAM_WD = 0.01
ADAM_STEP = 10

VMEM_BYTES_V7X = 64 * 1024 * 1024
SUBLANES = 8
LANES = 128


def _params(block_bytes, extra_bytes=0):
    need = 2 * block_bytes + extra_bytes + (4 << 20)
    return pltpu.CompilerParams(vmem_limit_bytes=int(min(max(need, 16 << 20), VMEM_BYTES_V7X - (8 << 20))))


def _nbytes(shape, dtype):
    return math.prod(shape) * jnp.dtype(dtype).itemsize


_NN = (((1,), (0,)), ((), ()))
_NT = (((1,), (1,)), ((), ()))
_TN = (((0,), (0,)), ((), ()))


def _mm_act(pairs, mode, *, name, out_sharded=False, reduce_shards=False, out_dtype=F32, add=None, tm=512):
    n_tok = pairs[0][0].shape[1]
    n_out = pairs[0][1].shape[2] if mode == "nn" else pairs[0][1].shape[1]
    tm = min(tm, n_tok)
    tn = n_out if n_out <= 1536 else 1024
    assert n_tok % tm == 0 and n_out % tn == 0
    n_so = N_SHARD if out_sharded else 1
    n_red = N_SHARD if reduce_shards else 1
    grid = (n_so, n_tok // tm, n_out // tn, n_red)
    dims = _NN if mode == "nn" else _NT

    def shard_of(n_sh):
        if n_sh == 1:
            return lambda s, r: 0
        return (lambda s, r: s) if out_sharded else (lambda s, r: r)

    in_specs, args, blk = [], [], 0
    for a, b in pairs:
        k_dim = a.shape[2]
        sa, sb = shard_of(a.shape[0]), shard_of(b.shape[0])
        in_specs.append(pl.BlockSpec((1, tm, k_dim), lambda s, i, j, r, sa=sa: (sa(s, r), i, 0)))
        if mode == "nn":
            assert b.shape[1] == k_dim
            in_specs.append(pl.BlockSpec((1, k_dim, tn), lambda s, i, j, r, sb=sb: (sb(s, r), 0, j)))
        else:
            assert b.shape[2] == k_dim
            in_specs.append(pl.BlockSpec((1, tn, k_dim), lambda s, i, j, r, sb=sb: (sb(s, r), j, 0)))
        args += [a, b]
        blk += _nbytes((tm, k_dim), a.dtype) + _nbytes((k_dim, tn), b.dtype)
    if add is not None:
        in_specs.append(pl.BlockSpec((tm, tn), lambda s, i, j, r: (i, j)))
        args.append(add)
        blk += _nbytes((tm, tn), F32)
    blk += _nbytes((tm, tn), out_dtype)
    n_pairs = len(pairs)

    def body(*refs):
        out_ref = refs[2 * n_pairs + (add is not None)]
        acc = None
        for k in range(n_pairs):
            a = refs[2 * k][0].astype(MXU_DTYPE)
            b = refs[2 * k + 1][0].astype(MXU_DTYPE)
            d = lax.dot_general(a, b, dims, preferred_element_type=F32)
            acc = d if acc is None else acc + d

        def finish(total):
            if add is not None:
                total = total + refs[2 * n_pairs][...]
            out_ref[0] = total.astype(out_dtype)

        if n_red == 1:
            finish(acc)
        else:
            acc_ref = refs[-1]
            r = pl.program_id(3)

            @pl.when(r == 0)
            def _():
                acc_ref[...] = acc

            @pl.when(r > 0)
            def _():
                acc_ref[...] += acc

            @pl.when(r == n_red - 1)
            def _():
                finish(acc_ref[...])

    return pl.pallas_call(
        body,
        name=name,
        grid=grid,
        in_specs=in_specs,
        out_specs=pl.BlockSpec((1, tm, tn), lambda s, i, j, r: (s, i, j)),
        out_shape=jax.ShapeDtypeStruct((n_so, n_tok, n_out), out_dtype),
        scratch_shapes=[pltpu.VMEM((tm, tn), F32)] if n_red > 1 else [],
        compiler_params=_params(blk, 3 * _nbytes((tm, tn), F32)),
    )(*args)


def _mm_tn(a, b, *, name, tt=512):
    n_tok, k_dim = a.shape[1], a.shape[2]
    n_out = b.shape[2]
    tt = min(tt, n_tok)
    tk = k_dim if k_dim <= 1536 else 1024
    tn = n_out if n_out <= 1536 else 1024
    assert n_tok % tt == 0 and k_dim % tk == 0 and n_out % tn == 0
    n_so = max(a.shape[0], b.shape[0])
    sa = (lambda s: s) if a.shape[0] > 1 else (lambda s: 0)
    sb = (lambda s: s) if b.shape[0] > 1 else (lambda s: 0)
    grid = (n_so, k_dim // tk, n_out // tn, n_tok // tt)

    def body(a_ref, b_ref, out_ref):
        d = lax.dot_general(a_ref[0].astype(MXU_DTYPE), b_ref[0].astype(MXU_DTYPE), _TN, preferred_element_type=F32)
        t = pl.program_id(3)

        @pl.when(t == 0)
        def _():
            out_ref[0] = d

        @pl.when(t > 0)
        def _():
            out_ref[0] += d

    blk = _nbytes((tt, tk), a.dtype) + _nbytes((tt, tn), b.dtype) + _nbytes((tk, tn), F32)
    return pl.pallas_call(
        body,
        name=name,
        grid=grid,
        in_specs=[
            pl.BlockSpec((1, tt, tk), lambda s, ki, nj, t: (sa(s), t, ki)),
            pl.BlockSpec((1, tt, tn), lambda s, ki, nj, t: (sb(s), t, nj)),
        ],
        out_specs=pl.BlockSpec((1, tk, tn), lambda s, ki, nj, t: (s, ki, nj)),
        out_shape=jax.ShapeDtypeStruct((n_so, k_dim, n_out), F32),
        compiler_params=_params(blk, 2 * _nbytes((tk, tn), F32) + _nbytes((tt, tk), F32)),
    )(a, b)


@functools.partial(jax.custom_vjp, nondiff_argnums=(2,))
def _mdot(a, b, dims):
    return lax.dot_general(a.astype(MXU_DTYPE), b.astype(MXU_DTYPE), dims, preferred_element_type=F32)


def _mdot_fwd(a, b, dims):
    return _mdot(a, b, dims), (a, b)


def _mdot_bwd(dims, res, g):
    a, b = res
    (ca, cb), (ba, bb) = dims
    nb = len(ba)
    assert tuple(ba) == tuple(range(nb)) and tuple(bb) == tuple(range(nb)) and len(ca) == 1 and a.ndim == nb + 2
    batch = (tuple(range(nb)), tuple(range(nb)))
    ra, rb = nb, nb + 1
    a_free = (set(range(nb, nb + 2)) - set(ca)).pop()
    b_free = (set(range(nb, nb + 2)) - set(cb)).pop()
    if a_free < ca[0]:
        da = _mdot(g, b, (((rb,), (b_free,)), batch))
    else:
        da = _mdot(b, g, (((b_free,), (rb,)), batch))
    if b_free > cb[0]:
        db = _mdot(a, g, (((a_free,), (ra,)), batch))
    else:
        db = _mdot(g, a, (((ra,), (a_free,)), batch))
    return da.astype(a.dtype), db.astype(b.dtype)


_mdot.defvjp(_mdot_fwd, _mdot_bwd)


def _rms(x, gain):
    return x * lax.rsqrt(jnp.mean(x * x, axis=-1, keepdims=True) + EPS) * gain


def _pre_fn(coef, x_in, f, gate, gain, shift, scale):
    x_new = x_in if f is None else x_in + coef * gate * f
    return x_new, _rms(x_new, gain) * (1.0 + scale) + shift


def _row_spec(ts):
    return pl.BlockSpec((1, ts, D_MODEL), lambda b, j: (b, j, 0))


_BATCH_VEC = pl.BlockSpec((1, 1, D_MODEL), lambda b, j: (b, 0, 0))
_ONE_VEC = pl.BlockSpec((1, D_MODEL), lambda b, j: (0, 0))


def _pre(x_in, f, gate, gain, shift, scale, coef, *, name, ts=512):
    n_b, n_s, _ = x_in.shape
    ts = min(ts, n_s)
    has_res = f is not None

    def body(*refs):
        if has_res:
            x_ref, f_ref, gate_ref, gain_ref, sh_ref, sc_ref, xn_ref, a_ref = refs
            x_new, a = _pre_fn(coef, x_ref[0], f_ref[0], gate_ref[0], gain_ref[...], sh_ref[0], sc_ref[0])
            xn_ref[0] = x_new
        else:
            x_ref, gain_ref, sh_ref, sc_ref, a_ref = refs
            _, a = _pre_fn(coef, x_ref[0], None, None, gain_ref[...], sh_ref[0], sc_ref[0])
        a_ref[0] = a.astype(a_ref.dtype)

    row = _row_spec(ts)
    if has_res:
        args = (x_in, f, gate, gain, shift, scale)
        in_specs = [row, row, _BATCH_VEC, _ONE_VEC, _BATCH_VEC, _BATCH_VEC]
        out_specs = (row, row)
        out_shape = (jax.ShapeDtypeStruct(x_in.shape, F32), jax.ShapeDtypeStruct(x_in.shape, MXU_DTYPE))
    else:
        args = (x_in, gain, shift, scale)
        in_specs = [row, _ONE_VEC, _BATCH_VEC, _BATCH_VEC]
        out_specs = row
        out_shape = jax.ShapeDtypeStruct(x_in.shape, MXU_DTYPE)
    return pl.pallas_call(
        body, name=name, grid=(n_b, n_s // ts), in_specs=in_specs, out_specs=out_specs, out_shape=out_shape,
        compiler_params=_params(5 * _nbytes((ts, D_MODEL), F32), 4 * _nbytes((ts, D_MODEL), F32)),
    )(*args)


def _accumulate(ref, value, first):
    @pl.when(first)
    def _():
        ref[...] = value

    @pl.when(jnp.logical_not(first))
    def _():
        ref[...] += value


def _pre_bwd(x_in, f, gate, gain, shift, scale, coef, da, dx_up, *, name, ts=512):
    n_b, n_s, _ = x_in.shape
    ts = min(ts, n_s)
    has_res = f is not None
    has_up = dx_up is not None

    def body(*refs):
        refs = list(refs)
        x_ref = refs.pop(0)
        f_ref, gate_ref = (refs.pop(0), refs.pop(0)) if has_res else (None, None)
        gain_ref, sh_ref, sc_ref, da_ref = refs.pop(0), refs.pop(0), refs.pop(0), refs.pop(0)
        up_ref = refs.pop(0) if has_up else None
        dx_ref = refs.pop(0)
        df_ref, dgate_ref = (refs.pop(0), refs.pop(0)) if has_res else (None, None)
        dgain_ref, dsh_ref, dsc_ref = refs
        b, j = pl.program_id(0), pl.program_id(1)
        da_v = da_ref[0].astype(F32)
        up_v = up_ref[0] if has_up else jnp.zeros((ts, D_MODEL), F32)
        if has_res:
            fn = functools.partial(_pre_fn, coef)
            _, pull = jax.vjp(fn, x_ref[0], f_ref[0], gate_ref[0], gain_ref[...], sh_ref[0], sc_ref[0])
            dx, df, dgate, dgain, dsh, dsc = pull((up_v, da_v))
            df_ref[0] = df.astype(df_ref.dtype)
            _accumulate(dgate_ref, dgate[None], j == 0)
        else:
            fn = lambda x, g, sh, sc: _pre_fn(coef, x, None, None, g, sh, sc)
            _, pull = jax.vjp(fn, x_ref[0], gain_ref[...], sh_ref[0], sc_ref[0])
            dx, dgain, dsh, dsc = pull((up_v, da_v))
        dx_ref[0] = dx
        _accumulate(dgain_ref, dgain, jnp.logical_and(b == 0, j == 0))
        _accumulate(dsh_ref, dsh[None], j == 0)
        _accumulate(dsc_ref, dsc[None], j == 0)

    row = _row_spec(ts)
    args, in_specs = [x_in], [row]
    if has_res:
        args += [f, gate]
        in_specs += [row, _BATCH_VEC]
    args += [gain, shift, scale, da]
    in_specs += [_ONE_VEC, _BATCH_VEC, _BATCH_VEC, row]
    if has_up:
        args.append(dx_up)
        in_specs.append(row)
    vec = jax.ShapeDtypeStruct((n_b, 1, D_MODEL), F32)
    out_shape, out_specs = [jax.ShapeDtypeStruct(x_in.shape, F32)], [row]
    if has_res:
        out_shape += [jax.ShapeDtypeStruct(x_in.shape, MXU_DTYPE), vec]
        out_specs += [row, _BATCH_VEC]
    out_shape += [jax.ShapeDtypeStruct((1, D_MODEL), F32), vec, vec]
    out_specs += [_ONE_VEC, _BATCH_VEC, _BATCH_VEC]
    return pl.pallas_call(
        body, name=name, grid=(n_b, n_s // ts), in_specs=in_specs, out_specs=tuple(out_specs), out_shape=tuple(out_shape),
        compiler_params=_params(6 * _nbytes((ts, D_MODEL), F32), 8 * _nbytes((ts, D_MODEL), F32)),
    )(*args)


def _final_fn(x_in, f, gate, gain, target):
    x_new = x_in + 0.5 * gate * f
    err = jnp.square(_rms(x_new, gain) - target)
    return 0.5 * jnp.sum(jnp.mean(err, axis=-1))


def _final(x_in, f, gate, gain, target, *, name, ts=512):
    n_b, n_s, _ = x_in.shape
    ts = min(ts, n_s)

    def body(x_ref, f_ref, gate_ref, gain_ref, t_ref, loss_ref, dx_ref, df_ref, dgate_ref, dgain_ref):
        b, j = pl.program_id(0), pl.program_id(1)
        loss, (dx, df, dgate, dgain) = jax.value_and_grad(_final_fn, argnums=(0, 1, 2, 3))(
            x_ref[0], f_ref[0], gate_ref[0], gain_ref[...], t_ref[0])
        first = jnp.logical_and(b == 0, j == 0)
        _accumulate(loss_ref, jnp.reshape(loss, (1, 1)), first)
        dx_ref[0] = dx
        df_ref[0] = df.astype(df_ref.dtype)
        _accumulate(dgate_ref, dgate[None], j == 0)
        _accumulate(dgain_ref, dgain, first)

    row = _row_spec(ts)
    return pl.pallas_call(
        body, name=name, grid=(n_b, n_s // ts),
        in_specs=[row, row, _BATCH_VEC, _ONE_VEC, row],
        out_specs=(pl.BlockSpec((1, 1), lambda b, j: (0, 0)), row, row, _BATCH_VEC, _ONE_VEC),
        out_shape=(jax.ShapeDtypeStruct((1, 1), F32), jax.ShapeDtypeStruct(x_in.shape, F32),
                   jax.ShapeDtypeStruct(x_in.shape, MXU_DTYPE), jax.ShapeDtypeStruct((n_b, 1, D_MODEL), F32),
                   jax.ShapeDtypeStruct((1, D_MODEL), F32)),
        compiler_params=_params(5 * _nbytes((ts, D_MODEL), F32), 8 * _nbytes((ts, D_MODEL), F32)),
    )(x_in, f, gate, gain, target)


def _ffn_up(a, w1s, w3s, *, name, tm=512):
    n_tok = a.shape[0]
    tm = min(tm, n_tok)

    def body(a_ref, w1_ref, w3_ref, h1_ref, h3_ref, g_ref):
        av = a_ref[...].astype(MXU_DTYPE)
        h1 = lax.dot_general(av, w1_ref[0].astype(MXU_DTYPE), _NN, preferred_element_type=F32)
        h3 = lax.dot_general(av, w3_ref[0].astype(MXU_DTYPE), _NN, preferred_element_type=F32)
        h1_ref[0] = h1.astype(h1_ref.dtype)
        h3_ref[0] = h3.astype(h3_ref.dtype)
        g_ref[0] = (jax.nn.silu(h1) * h3).astype(g_ref.dtype)

    w_spec = pl.BlockSpec((1, D_MODEL, FF_SHARD), lambda s, i: (s, 0, 0))
    h_spec = pl.BlockSpec((1, tm, FF_SHARD), lambda s, i: (s, i, 0))
    h_shape = jax.ShapeDtypeStruct((N_SHARD, n_tok, FF_SHARD), MXU_DTYPE)
    blk = _nbytes((tm, D_MODEL), a.dtype) + 2 * _nbytes((D_MODEL, FF_SHARD), w1s.dtype) + 3 * _nbytes((tm, FF_SHARD), MXU_DTYPE)
    return pl.pallas_call(
        body, name=name, grid=(N_SHARD, n_tok // tm),
        in_specs=[pl.BlockSpec((tm, D_MODEL), lambda s, i: (i, 0)), w_spec, w_spec],
        out_specs=(h_spec, h_spec, h_spec), out_shape=(h_shape, h_shape, h_shape),
        compiler_params=_params(blk, 6 * _nbytes((tm, FF_SHARD), F32)),
    )(a, w1s, w3s)


def _ffn_down_bwd(df, w2s, h1, h3, *, name, tm=512):
    n_tok = df.shape[0]
    tm = min(tm, n_tok)

    def body(df_ref, w2_ref, h1_ref, h3_ref, dh1_ref, dh3_ref):
        dg = lax.dot_general(df_ref[...].astype(MXU_DTYPE), w2_ref[0].astype(MXU_DTYPE), _NT, preferred_element_type=F32)
        h1v = h1_ref[0].astype(F32)
        h3v = h3_ref[0].astype(F32)
        sig = jax.nn.sigmoid(h1v)
        dh3_ref[0] = (dg * (h1v * sig)).astype(dh3_ref.dtype)
        dh1_ref[0] = (dg * h3v * (sig * (1.0 + h1v * (1.0 - sig)))).astype(dh1_ref.dtype)

    h_spec = pl.BlockSpec((1, tm, FF_SHARD), lambda s, i: (s, i, 0))
    h_shape = jax.ShapeDtypeStruct((N_SHARD, n_tok, FF_SHARD), MXU_DTYPE)
    blk = _nbytes((tm, D_MODEL), df.dtype) + _nbytes((FF_SHARD, D_MODEL), w2s.dtype) + 4 * _nbytes((tm, FF_SHARD), MXU_DTYPE)
    return pl.pallas_call(
        body, name=name, grid=(N_SHARD, n_tok // tm),
        in_specs=[pl.BlockSpec((tm, D_MODEL), lambda s, i: (i, 0)),
                  pl.BlockSpec((1, FF_SHARD, D_MODEL), lambda s, i: (s, 0, 0)), h_spec, h_spec],
        out_specs=(h_spec, h_spec), out_shape=(h_shape, h_shape),
        compiler_params=_params(blk, 8 * _nbytes((tm, FF_SHARD), F32)),
    )(df, w2s, h1, h3)


def _ffn_fwd(a, w1s, w3s, w2s, tag):
    h1, h3, g = _ffn_up(a, w1s, w3s, name=f"{tag}_up")
    f = _mm_act([(g, w2s)], "nn", reduce_shards=True, name=f"{tag}_down")[0]
    return f, (h1, h3, g)


def _ffn_bwd(a, w1s, w3s, w2s, saved, df, tag):
    h1, h3, g = saved
    dh1, dh3 = _ffn_down_bwd(df, w2s, h1, h3, name=f"{tag}_down_bwd")
    da = _mm_act([(dh1, w1s), (dh3, w3s)], "nt", reduce_shards=True, name=f"{tag}_up_bwd")[0]
    a3 = a[None]
    dw1 = _mm_tn(a3, dh1, name=f"{tag}_dw1")
    dw3 = _mm_tn(a3, dh3, name=f"{tag}_dw3")
    dw2 = _mm_tn(g, df[None], name=f"{tag}_dw2")
    return da, dw1, dw3, dw2


CONV_LANES = 256


def _shift_down(x, d):
    if d == 0:
        return x
    row = lax.broadcasted_iota(jnp.int32, x.shape, 0)
    return jnp.where(row >= d, pltpu.roll(x, d, 0), 0.0)


def _shift_up(x, d):
    if d == 0:
        return x
    n = x.shape[0]
    row = lax.broadcasted_iota(jnp.int32, x.shape, 0)
    return jnp.where(row < n - d, pltpu.roll(x, n - d, 0), 0.0)


def _conv_pre(x, w):
    acc = None
    for j in range(CONV_WIDTH):
        term = w[j:j + 1, :] * _shift_down(x, CONV_WIDTH - 1 - j)
        acc = term if acc is None else acc + term
    return acc


def _conv_fwd(x, w, *, name):
    n_b, n_s, n_c = x.shape
    spec = pl.BlockSpec((1, n_s, CONV_LANES), lambda b, cj: (b, 0, cj))

    def body(x_ref, w_ref, o_ref):
        o_ref[0] = jax.nn.silu(_conv_pre(x_ref[0], w_ref[...]))

    return pl.pallas_call(
        body, name=name, grid=(n_b, n_c // CONV_LANES),
        in_specs=[spec, pl.BlockSpec((CONV_WIDTH, CONV_LANES), lambda b, cj: (0, cj))],
        out_specs=spec, out_shape=jax.ShapeDtypeStruct(x.shape, F32),
        compiler_params=_params(2 * _nbytes((n_s, CONV_LANES), F32), 6 * _nbytes((n_s, CONV_LANES), F32)),
    )(x, w)


def _conv_bwd(x, w, dout, *, name):
    n_b, n_s, n_c = x.shape
    spec = pl.BlockSpec((1, n_s, CONV_LANES), lambda cj, b: (b, 0, cj))
    w_spec = pl.BlockSpec((CONV_WIDTH, CONV_LANES), lambda cj, b: (0, cj))

    def body(x_ref, w_ref, do_ref, dx_ref, dw_ref):
        xv, wv = x_ref[0], w_ref[...]
        pre = _conv_pre(xv, wv)
        sig = jax.nn.sigmoid(pre)
        dpre = do_ref[0] * (sig * (1.0 + pre * (1.0 - sig)))
        dx = None
        first = pl.program_id(1) == 0
        for j in range(CONV_WIDTH):
            d = CONV_WIDTH - 1 - j
            term = wv[j:j + 1, :] * _shift_up(dpre, d)
            dx = term if dx is None else dx + term
            dwj = jnp.sum(dpre * _shift_down(xv, d), axis=0, keepdims=True)
            _accumulate(dw_ref.at[j:j + 1, :], dwj, first)
        dx_ref[0] = dx.astype(dx_ref.dtype)

    return pl.pallas_call(
        body, name=name, grid=(n_c // CONV_LANES, n_b),
        in_specs=[spec, w_spec, spec], out_specs=(spec, w_spec),
        out_shape=(jax.ShapeDtypeStruct(x.shape, MXU_DTYPE), jax.ShapeDtypeStruct((CONV_WIDTH, n_c), F32)),
        compiler_params=_params(3 * _nbytes((n_s, CONV_LANES), F32), 8 * _nbytes((n_s, CONV_LANES), F32)),
    )(x, w, dout)


_BNT = (((2,), (2,)), ((0,), (0,)))
_BNN = (((2,), (1,)), ((0,), (0,)))
_BTN = (((1,), (1,)), ((0,), (0,)))
DN_PREP_CHUNKS = 8
DN_SCAN_HEADS = 4
N_DOUBLINGS = 5


def _fdot(a, b, dims):
    return lax.dot_general(a, b, dims, precision=lax.Precision.HIGHEST, preferred_element_type=F32)


def _hdot(a, b, dims):
    return lax.dot_general(a, b, dims, precision=lax.Precision.HIGH, preferred_element_type=F32)


def _solve_by_doubling(a, rhs_u, rhs_w):
    row = lax.broadcasted_iota(jnp.int32, (CHUNK, CHUNK), 0)
    col = lax.broadcasted_iota(jnp.int32, (CHUNK, CHUNK), 1)
    inv = jnp.where(row == col, 1.0, 0.0) - a
    power = a
    for _ in range(N_DOUBLINGS):
        power = _hdot(power, power, _BNN)
        inv = inv + _hdot(inv, power, _BNN)
    return _hdot(inv, rhs_u, _BNN), _hdot(inv, rhs_w, _BNN), inv


@jax.custom_vjp
def _solve_saved(a, rhs_u, rhs_w, inv, u, w):
    return u, w


def _solve_saved_fwd(a, rhs_u, rhs_w, inv, u, w):
    return (u, w), (inv, u, w)


def _solve_saved_bwd(res, cts):
    inv, u, w = res
    gu = _hdot(inv, cts[0], _BTN)
    gw = _hdot(inv, cts[1], _BTN)
    da = -(_hdot(gu, u, _BNT) + _hdot(gw, w, _BNT))
    return da, gu, gw, jnp.zeros_like(inv), jnp.zeros_like(u), jnp.zeros_like(w)


_solve_saved.defvjp(_solve_saved_fwd, _solve_saved_bwd)


def _dn_prep_fn(solve, qc, kc, vc, bl, lac, lar, a_log, dt_bias):
    q = qc * lax.rsqrt(jnp.sum(qc * qc, axis=-1, keepdims=True) + EPS) * (DN_HEAD_DIM ** -0.5)
    k = kc * lax.rsqrt(jnp.sum(kc * kc, axis=-1, keepdims=True) + EPS)
    beta = jax.nn.sigmoid(bl)
    neg_a = -jnp.exp(a_log)
    lgc = neg_a * jax.nn.softplus(lac + dt_bias)
    lgr = neg_a * jax.nn.softplus(lar + dt_bias)
    row = lax.broadcasted_iota(jnp.int32, (CHUNK, CHUNK), 0)
    col = lax.broadcasted_iota(jnp.int32, (CHUNK, CHUNK), 1)
    causal, strict = row >= col, row > col
    g_c = jnp.sum(jnp.where(causal, lgr, 0.0), axis=-1, keepdims=True)
    g_r = jnp.sum(jnp.where(row <= col, lgc, 0.0), axis=-2, keepdims=True)
    decay = jnp.exp(jnp.where(causal, g_c - g_r, -jnp.inf))
    kb = k * beta
    a = jnp.where(strict, _mdot(kb, k, _BNT) * decay, 0.0)
    u, w, extra = solve(a, vc * beta, kb * jnp.exp(g_c))
    attn = _mdot(q, k, _BNT) * decay
    g_last = jnp.sum(lgc, axis=-2, keepdims=True)
    return q * jnp.exp(g_c), k * jnp.exp(g_last - g_c), u, w, attn, g_last, extra


def _dn_prep_specs(n_cb):
    tok = n_cb * CHUNK
    wide = pl.BlockSpec((1, 1, tok, DN_HEAD_DIM), lambda h, b, j: (b, h, j, 0))
    col = pl.BlockSpec((1, 1, tok, 1), lambda h, b, j: (b, h, j, 0))
    rowv = pl.BlockSpec((1, 1, n_cb, 1, CHUNK), lambda h, b, j: (b, h, j, 0, 0))
    one = pl.BlockSpec((1, 1, n_cb, 1, 1), lambda h, b, j: (b, h, j, 0, 0))
    head = pl.BlockSpec((1, 1, 1), lambda h, b, j: (h, 0, 0))
    return wide, col, rowv, one, head


def _dn_prep_load(n_cb, q_ref, k_ref, v_ref, bl_ref, lac_ref, lar_ref, al_ref, dt_ref):
    wide = lambda r: r[0, 0].reshape(n_cb, CHUNK, DN_HEAD_DIM)
    colv = lambda r: r[0, 0].reshape(n_cb, CHUNK, 1)
    return (wide(q_ref), wide(k_ref), wide(v_ref), colv(bl_ref), colv(lac_ref), lar_ref[0, 0], al_ref[...], dt_ref[...])


def _dn_prep(qh, kh, vh, bl, lac, lar, a_log, dt_bias, *, name):
    n_b, n_h, n_s, _ = qh.shape
    n_cb = min(DN_PREP_CHUNKS, n_s // CHUNK)
    tok = n_cb * CHUNK
    wide, col, rowv, one, head = _dn_prep_specs(n_cb)

    def body(*refs):
        outs = _dn_prep_fn(_solve_by_doubling, *_dn_prep_load(n_cb, *refs[:8]))
        for ref, val in zip(refs[8:13], outs[:5]):
            ref[0, 0] = val.reshape(tok, DN_HEAD_DIM)
        refs[13][0, 0] = outs[5]
        refs[14][0, 0] = outs[6].reshape(tok, DN_HEAD_DIM)

    big = jax.ShapeDtypeStruct(qh.shape, F32)
    return pl.pallas_call(
        body, name=name, grid=(n_h, n_b, n_s // tok),
        in_specs=[wide, wide, wide, col, col, rowv, head, head],
        out_specs=(wide, wide, wide, wide, wide, one, wide),
        out_shape=(big, big, big, big, big, jax.ShapeDtypeStruct((n_b, n_h, n_s // CHUNK, 1, 1), F32), big),
        compiler_params=_params(11 * _nbytes((tok, LANES), F32), 48 * _nbytes((tok, LANES), F32)),
    )(qh, kh, vh, bl, lac, lar, a_log, dt_bias)


def _dn_prep_bwd(qh, kh, vh, bl, lac, lar, a_log, dt_bias, inv, u, w, cts, *, name):
    n_b, n_h, n_s, _ = qh.shape
    n_cb = min(DN_PREP_CHUNKS, n_s // CHUNK)
    tok = n_cb * CHUNK
    wide, col, rowv, one, head = _dn_prep_specs(n_cb)

    def body(*refs):
        prim = _dn_prep_load(n_cb, *refs[:8])
        chunks = lambda r: r[0, 0].reshape(n_cb, CHUNK, DN_HEAD_DIM)
        inv_v, u_v, w_v = chunks(refs[8]), chunks(refs[9]), chunks(refs[10])
        ct = tuple(chunks(r) for r in refs[11:16]) + (refs[16][0, 0],)

        def fn(*args):
            solve = lambda a, ru, rw: _solve_saved(a, ru, rw, inv_v, u_v, w_v) + (None,)
            return _dn_prep_fn(solve, *args)[:6]

        _, pull = jax.vjp(fn, *prim)
        dq, dk, dv, dbl, dlac, dlar, dal, ddt = pull(ct)
        outs = refs[17:]
        for ref, val in zip(outs[:3], (dq, dk, dv)):
            ref[0, 0] = val.reshape(tok, DN_HEAD_DIM)
        outs[3][0, 0] = dbl.reshape(tok, 1)
        outs[4][0, 0] = dlac.reshape(tok, 1)
        outs[5][0, 0] = dlar
        first = jnp.logical_and(pl.program_id(1) == 0, pl.program_id(2) == 0)
        _accumulate(outs[6], dal, first)
        _accumulate(outs[7], ddt, first)

    big = jax.ShapeDtypeStruct(qh.shape, F32)
    return pl.pallas_call(
        body, name=name, grid=(n_h, n_b, n_s // tok),
        in_specs=[wide, wide, wide, col, col, rowv, head, head, wide, wide, wide, wide, wide, wide, wide, wide, one],
        out_specs=(wide, wide, wide, col, col, rowv, head, head),
        out_shape=(big, big, big, jax.ShapeDtypeStruct(bl.shape, F32), jax.ShapeDtypeStruct(lac.shape, F32),
                   jax.ShapeDtypeStruct(lar.shape, F32), jax.ShapeDtypeStruct(a_log.shape, F32),
                   jax.ShapeDtypeStruct(dt_bias.shape, F32)),
        compiler_params=_params(21 * _nbytes((tok, LANES), F32), 64 * _nbytes((tok, LANES), F32)),
    )(qh, kh, vh, bl, lac, lar, a_log, dt_bias, inv, u, w, *cts)


def _dn_step(state, q, k, u, w, a, gl):
    v_new = u - _mdot(w, state, _BNN)
    o = _mdot(q, state, _BNN) + _mdot(a, v_new, _BNN)
    return state * jnp.exp(gl) + _mdot(k, v_new, _BTN), o


def _dn_scan_specs(n_cb, n_blocks, reverse):
    tok = n_cb * CHUNK
    jj = (lambda j: n_blocks - 1 - j) if reverse else (lambda j: j)
    wide = pl.BlockSpec((1, DN_SCAN_HEADS, tok, DN_HEAD_DIM), lambda b, h, j: (b, h, jj(j), 0))
    one = pl.BlockSpec((1, DN_SCAN_HEADS, n_cb, 1, 1), lambda b, h, j: (b, h, jj(j), 0, 0))
    st = pl.BlockSpec((1, DN_SCAN_HEADS, n_cb, DN_HEAD_DIM, DN_HEAD_DIM), lambda b, h, j: (b, h, jj(j), 0, 0))
    return wide, one, st


def _dn_scan(qd, kd, u, w, attn, g_last, *, name):
    n_b, n_h, n_s, _ = qd.shape
    n_cb = min(DN_PREP_CHUNKS, n_s // CHUNK)
    n_blocks = n_s // (n_cb * CHUNK)
    wide, one, st = _dn_scan_specs(n_cb, n_blocks, False)

    def body(qd_ref, kd_ref, u_ref, w_ref, a_ref, gl_ref, o_ref, st_ref, state_ref):
        @pl.when(pl.program_id(2) == 0)
        def _():
            state_ref[...] = jnp.zeros(state_ref.shape, F32)

        def step(n, state):
            rows = pl.ds(pl.multiple_of(n * CHUNK, CHUNK), CHUNK)
            st_ref[0, :, n] = state
            state, o = _dn_step(state, qd_ref[0, :, rows, :], kd_ref[0, :, rows, :], u_ref[0, :, rows, :],
                                w_ref[0, :, rows, :], a_ref[0, :, rows, :], gl_ref[0, :, n])
            o_ref[0, :, rows, :] = o
            return state

        state_ref[...] = lax.fori_loop(0, n_cb, step, state_ref[...])

    return pl.pallas_call(
        body, name=name, grid=(n_b, n_h // DN_SCAN_HEADS, n_blocks),
        in_specs=[wide, wide, wide, wide, wide, one], out_specs=(wide, st),
        out_shape=(jax.ShapeDtypeStruct(qd.shape, F32),
                   jax.ShapeDtypeStruct((n_b, n_h, n_s // CHUNK, DN_HEAD_DIM, DN_HEAD_DIM), F32)),
        scratch_shapes=[pltpu.VMEM((DN_SCAN_HEADS, DN_HEAD_DIM, DN_HEAD_DIM), F32)],
        compiler_params=_params(8 * _nbytes((DN_SCAN_HEADS, n_cb * CHUNK, LANES), F32), 8 << 20),
    )(qd, kd, u, w, attn, g_last)


def _dn_scan_bwd(qd, kd, u, w, attn, g_last, states, do, *, name):
    n_b, n_h, n_s, _ = qd.shape
    n_cb = min(DN_PREP_CHUNKS, n_s // CHUNK)
    n_blocks = n_s // (n_cb * CHUNK)
    wide, one, st = _dn_scan_specs(n_cb, n_blocks, True)

    def body(qd_ref, kd_ref, u_ref, w_ref, a_ref, gl_ref, st_ref, do_ref,
             dq_ref, dk_ref, du_ref, dw_ref, da_ref, dgl_ref, dstate_ref):
        @pl.when(pl.program_id(2) == 0)
        def _():
            dstate_ref[...] = jnp.zeros(dstate_ref.shape, F32)

        def step(i, dstate):
            n = n_cb - 1 - i
            rows = pl.ds(pl.multiple_of(n * CHUNK, CHUNK), CHUNK)
            _, pull = jax.vjp(_dn_step, st_ref[0, :, n], qd_ref[0, :, rows, :], kd_ref[0, :, rows, :],
                              u_ref[0, :, rows, :], w_ref[0, :, rows, :], a_ref[0, :, rows, :], gl_ref[0, :, n])
            dstate, dq, dk, du, dw, da, dgl = pull((dstate, do_ref[0, :, rows, :]))
            dq_ref[0, :, rows, :] = dq
            dk_ref[0, :, rows, :] = dk
            du_ref[0, :, rows, :] = du
            dw_ref[0, :, rows, :] = dw
            da_ref[0, :, rows, :] = da
            dgl_ref[0, :, n] = dgl
            return dstate

        dstate_ref[...] = lax.fori_loop(0, n_cb, step, dstate_ref[...])

    big = jax.ShapeDtypeStruct(qd.shape, F32)
    return pl.pallas_call(
        body, name=name, grid=(n_b, n_h // DN_SCAN_HEADS, n_blocks),
        in_specs=[wide, wide, wide, wide, wide, one, st, wide],
        out_specs=(wide, wide, wide, wide, wide, one),
        out_shape=(big, big, big, big, big, jax.ShapeDtypeStruct(g_last.shape, F32)),
        scratch_shapes=[pltpu.VMEM((DN_SCAN_HEADS, DN_HEAD_DIM, DN_HEAD_DIM), F32)],
        compiler_params=_params(13 * _nbytes((DN_SCAN_HEADS, n_cb * CHUNK, LANES), F32), 8 << 20),
    )(qd, kd, u, w, attn, g_last, states, do)


def _dn_post_fn(o, z, gain):
    return o * lax.rsqrt(jnp.mean(o * o, axis=-1, keepdims=True) + EPS) * gain * jax.nn.silu(z)


_HEAD_ROWS = lambda n_s: pl.BlockSpec((1, 1, n_s, DN_HEAD_DIM), lambda b, h: (b, h, 0, 0))
_HEAD_GAIN = pl.BlockSpec((1, DN_HEAD_DIM), lambda b, h: (0, 0))


def _dn_post(o, z, gain, *, name):
    n_b, n_h, n_s, _ = o.shape

    def body(o_ref, z_ref, g_ref, out_ref):
        out_ref[0, 0] = _dn_post_fn(o_ref[0, 0], z_ref[0, 0], g_ref[...]).astype(out_ref.dtype)

    rows = _HEAD_ROWS(n_s)
    return pl.pallas_call(
        body, name=name, grid=(n_b, n_h), in_specs=[rows, rows, _HEAD_GAIN], out_specs=rows,
        out_shape=jax.ShapeDtypeStruct(o.shape, MXU_DTYPE),
        compiler_params=_params(3 * _nbytes((n_s, LANES), F32), 6 * _nbytes((n_s, LANES), F32)),
    )(o, z, gain)


def _dn_post_bwd(o, z, gain, dout, *, name):
    n_b, n_h, n_s, _ = o.shape

    def body(o_ref, z_ref, g_ref, dout_ref, do_ref, dz_ref, dg_ref):
        _, pull = jax.vjp(_dn_post_fn, o_ref[0, 0], z_ref[0, 0], g_ref[...])
        do, dz, dg = pull(dout_ref[0, 0].astype(F32))
        do_ref[0, 0] = do
        dz_ref[0, 0] = dz.astype(dz_ref.dtype)
        _accumulate(dg_ref, dg, jnp.logical_and(pl.program_id(0) == 0, pl.program_id(1) == 0))

    rows = _HEAD_ROWS(n_s)
    return pl.pallas_call(
        body, name=name, grid=(n_b, n_h), in_specs=[rows, rows, _HEAD_GAIN, rows],
        out_specs=(rows, rows, _HEAD_GAIN),
        out_shape=(jax.ShapeDtypeStruct(o.shape, F32), jax.ShapeDtypeStruct(o.shape, MXU_DTYPE),
                   jax.ShapeDtypeStruct((1, DN_HEAD_DIM), F32)),
        compiler_params=_params(5 * _nbytes((n_s, LANES), F32), 10 * _nbytes((n_s, LANES), F32)),
    )(o, z, gain, dout)


S5_SCAN_LANES = 256
TILE_ROWS = SUBLANES


def _s5_prep_fn(lam_re, lam_im, log_step, bt_re, bt_im, c_im):
    lr = jnp.minimum(lam_re, -1e-4)
    step = jnp.exp(log_step)
    mag = jnp.exp(lr * step)
    ang = lam_im * step
    lb_re = mag * jnp.cos(ang)
    lb_im = mag * jnp.sin(ang)
    den = lr * lr + lam_im * lam_im
    coef_re = ((lb_re - 1.0) * lr + lb_im * lam_im) / den
    coef_im = (lb_im * lr - (lb_re - 1.0) * lam_im) / den
    return (lb_re, lb_im, coef_re * bt_re - coef_im * bt_im, coef_re * bt_im + coef_im * bt_re, -c_im)


def _s5_prep(lam_re, lam_im, log_step, bt_re, bt_im, c_im, *, name):
    def body(*refs):
        outs = _s5_prep_fn(*(r[...] for r in refs[:6]))
        for ref, val in zip(refs[6:], outs):
            ref[...] = val

    vec = jax.ShapeDtypeStruct(lam_re.shape, F32)
    mat = jax.ShapeDtypeStruct(bt_re.shape, F32)
    return pl.pallas_call(body, name=name, out_shape=(vec, vec, mat, mat, mat))(lam_re, lam_im, log_step, bt_re, bt_im, c_im)


def _s5_prep_bwd(lam_re, lam_im, log_step, bt_re, bt_im, c_im, cts, *, name):
    def body(*refs):
        _, pull = jax.vjp(_s5_prep_fn, *(r[...] for r in refs[:6]))
        grads = pull(tuple(r[...] for r in refs[6:11]))
        for ref, val in zip(refs[11:], grads):
            ref[...] = val

    shapes = tuple(jax.ShapeDtypeStruct(a.shape, F32) for a in (lam_re, lam_im, log_step, bt_re, bt_im, c_im))
    return pl.pallas_call(body, name=name, out_shape=shapes)(lam_re, lam_im, log_step, bt_re, bt_im, c_im, *cts)


def _cmul(ar, ai, br, bi):
    return ar * br - ai * bi, ar * bi + ai * br


def _s5_powers(lr, li):
    pows = [(lr, li)]
    for _ in range(TILE_ROWS - 1):
        pows.append(_cmul(pows[-1][0], pows[-1][1], lr, li))
    return pows


def _s5_carry_table(pows, n_lanes, reverse):
    row = lax.broadcasted_iota(jnp.int32, (TILE_ROWS, n_lanes), 0)
    t_re = jnp.zeros((TILE_ROWS, n_lanes), F32)
    t_im = jnp.zeros((TILE_ROWS, n_lanes), F32)
    for r in range(TILE_ROWS):
        p_re, p_im = pows[TILE_ROWS - 1 - r] if reverse else pows[r]
        t_re = jnp.where(row == r, p_re, t_re)
        t_im = jnp.where(row == r, p_im, t_im)
    return t_re, t_im


def _s5_tile(y_re, y_im, pows, reverse):
    d = 1
    while d < TILE_ROWS:
        p_re, p_im = pows[d - 1]
        if reverse:
            s_re, s_im = _shift_up(y_re, d), _shift_up(y_im, d)
        else:
            s_re, s_im = _shift_down(y_re, d), _shift_down(y_im, d)
        m_re, m_im = _cmul(p_re, p_im, s_re, s_im)
        y_re, y_im = y_re + m_re, y_im + m_im
        d *= 2
    return y_re, y_im


def _s5_scan(bu, lb_re, lb_im, *, name):
    n_b, n_s, _ = bu.shape
    n_lb = S5_LANES // S5_SCAN_LANES
    n_tiles = n_s // TILE_ROWS
    L = S5_SCAN_LANES

    def body(re_ref, im_ref, lr_ref, li_ref, xr_ref, xi_ref):
        pows = _s5_powers(lr_ref[...], li_ref[...])
        t_re, t_im = _s5_carry_table(pows, L, False)

        def step(i, carry):
            rows = pl.ds(pl.multiple_of(i * TILE_ROWS, TILE_ROWS), TILE_ROWS)
            y_re, y_im = _s5_tile(re_ref[0, rows, :], im_ref[0, rows, :], pows, False)
            c_re, c_im = _cmul(t_re, t_im, carry[0], carry[1])
            y_re, y_im = y_re + c_re, y_im + c_im
            xr_ref[0, rows, :] = y_re
            xi_ref[0, rows, :] = y_im
            return y_re[TILE_ROWS - 1:, :], y_im[TILE_ROWS - 1:, :]

        zero = jnp.zeros((1, L), F32)
        lax.fori_loop(0, n_tiles, step, (zero, zero))

    blk_spec = lambda off: pl.BlockSpec((1, n_s, L), lambda b, j: (b, 0, j + off))
    lam_spec = pl.BlockSpec((1, L), lambda b, j: (0, j))
    x_shape = jax.ShapeDtypeStruct((n_b, n_s, S5_LANES), F32)
    return pl.pallas_call(
        body, name=name, grid=(n_b, n_lb),
        in_specs=[blk_spec(0), blk_spec(n_lb), lam_spec, lam_spec],
        out_specs=(blk_spec(0), blk_spec(0)), out_shape=(x_shape, x_shape),
        compiler_params=_params(4 * _nbytes((n_s, L), F32), 4 << 20),
    )(bu, bu, lb_re, lb_im)


def _s5_scan_bwd(dx, x_re, x_im, lb_re, lb_im, *, name):
    n_b, n_s, _ = dx.shape
    n_lb = S5_LANES // S5_SCAN_LANES
    n_tiles = n_s // TILE_ROWS
    L = S5_SCAN_LANES

    def body(dr_ref, di_ref, xr_ref, xi_ref, lr_ref, li_ref, ar_ref, ai_ref, dlr_ref, dli_ref):
        pows = _s5_powers(lr_ref[...], -li_ref[...])
        t_re, t_im = _s5_carry_table(pows, L, True)
        row = lax.broadcasted_iota(jnp.int32, (TILE_ROWS, L), 0)

        def step(k, carry):
            c_re, c_im, s_re, s_im = carry
            i = n_tiles - 1 - k
            rows = pl.ds(pl.multiple_of(i * TILE_ROWS, TILE_ROWS), TILE_ROWS)
            a_re, a_im = _s5_tile(dr_ref[0, rows, :], di_ref[0, rows, :], pows, True)
            m_re, m_im = _cmul(t_re, t_im, c_re, c_im)
            a_re, a_im = a_re + m_re, a_im + m_im
            ar_ref[0, rows, :] = a_re.astype(ar_ref.dtype)
            ai_ref[0, rows, :] = a_im.astype(ai_ref.dtype)
            prev = pl.ds(pl.multiple_of(jnp.maximum(i - 1, 0) * TILE_ROWS, TILE_ROWS), TILE_ROWS)
            keep = jnp.where(i > 0, 1.0, 0.0)
            last_re = xr_ref[0, prev, :][TILE_ROWS - 1:, :] * keep
            last_im = xi_ref[0, prev, :][TILE_ROWS - 1:, :] * keep
            xp_re = jnp.where(row == 0, last_re, _shift_down(xr_ref[0, rows, :], 1))
            xp_im = jnp.where(row == 0, last_im, _shift_down(xi_ref[0, rows, :], 1))
            s_re = s_re + a_re * xp_re + a_im * xp_im
            s_im = s_im + a_im * xp_re - a_re * xp_im
            return a_re[:1, :], a_im[:1, :], s_re, s_im

        zero = jnp.zeros((1, L), F32)
        zt = jnp.zeros((TILE_ROWS, L), F32)
        _, _, s_re, s_im = lax.fori_loop(0, n_tiles, step, (zero, zero, zt, zt))
        first = pl.program_id(1) == 0
        _accumulate(dlr_ref, jnp.sum(s_re, axis=0, keepdims=True), first)
        _accumulate(dli_ref, jnp.sum(s_im, axis=0, keepdims=True), first)

    blk_spec = lambda off: pl.BlockSpec((1, n_s, L), lambda j, b: (b, 0, j + off))
    lam_spec = pl.BlockSpec((1, L), lambda j, b: (0, j))
    a_shape = jax.ShapeDtypeStruct((n_b, n_s, S5_LANES), MXU_DTYPE)
    lam_shape = jax.ShapeDtypeStruct((1, S5_LANES), F32)
    return pl.pallas_call(
        body, name=name, grid=(n_lb, n_b),
        in_specs=[blk_spec(0), blk_spec(n_lb), blk_spec(0), blk_spec(0), lam_spec, lam_spec],
        out_specs=(blk_spec(0), blk_spec(0), lam_spec, lam_spec),
        out_shape=(a_shape, a_shape, lam_shape, lam_shape),
        compiler_params=_params(5 * _nbytes((n_s, L), F32), 4 << 20),
    )(dx, dx, x_re, x_im, lb_re, lb_im)


def _s5_out_fn(ymm, u, d_skip, w_glu, b_glu):
    y = jax.nn.gelu(ymm + d_skip * u)
    return y * jax.nn.sigmoid(_mdot(y, w_glu, _NN) + b_glu)


def _s5_out_specs(tm):
    rows = pl.BlockSpec((tm, S5_WIDTH), lambda i: (i, 0))
    vec = pl.BlockSpec((1, S5_WIDTH), lambda i: (0, 0))
    mat = pl.BlockSpec((S5_WIDTH, S5_WIDTH), lambda i: (0, 0))
    return rows, vec, mat


def _s5_out(ymm, u, d_skip, w_glu, b_glu, *, name, tm=512):
    n_tok = ymm.shape[0]
    tm = min(tm, n_tok)
    rows, vec, mat = _s5_out_specs(tm)

    def body(y_ref, u_ref, d_ref, w_ref, b_ref, o_ref):
        o_ref[...] = _s5_out_fn(y_ref[...], u_ref[...], d_ref[...], w_ref[...], b_ref[...]).astype(o_ref.dtype)

    return pl.pallas_call(
        body, name=name, grid=(n_tok // tm,), in_specs=[rows, rows, vec, mat, vec], out_specs=rows,
        out_shape=jax.ShapeDtypeStruct(ymm.shape, MXU_DTYPE),
        compiler_params=_params(4 * _nbytes((tm, S5_WIDTH), F32), 8 * _nbytes((tm, S5_WIDTH), F32)),
    )(ymm, u, d_skip, w_glu, b_glu)


def _s5_out_bwd(ymm, u, d_skip, w_glu, b_glu, dout, *, name, tm=512):
    n_tok = ymm.shape[0]
    tm = min(tm, n_tok)
    rows, vec, mat = _s5_out_specs(tm)

    def body(y_ref, u_ref, d_ref, w_ref, b_ref, do_ref, dy_ref, du_ref, dd_ref, dw_ref, db_ref):
        _, pull = jax.vjp(_s5_out_fn, y_ref[...], u_ref[...], d_ref[...], w_ref[...].astype(F32), b_ref[...])
        dy, du, dd, dw, db = pull(do_ref[...])
        dy_ref[...] = dy.astype(dy_ref.dtype)
        du_ref[...] = du
        first = pl.program_id(0) == 0
        _accumulate(dd_ref, dd, first)
        _accumulate(dw_ref, dw, first)
        _accumulate(db_ref, db, first)

    return pl.pallas_call(
        body, name=name, grid=(n_tok // tm,), in_specs=[rows, rows, vec, mat, vec, rows],
        out_specs=(rows, rows, vec, mat, vec),
        out_shape=(jax.ShapeDtypeStruct(ymm.shape, MXU_DTYPE), jax.ShapeDtypeStruct(ymm.shape, F32),
                   jax.ShapeDtypeStruct((1, S5_WIDTH), F32), jax.ShapeDtypeStruct((S5_WIDTH, S5_WIDTH), F32),
                   jax.ShapeDtypeStruct((1, S5_WIDTH), F32)),
        compiler_params=_params(6 * _nbytes((tm, S5_WIDTH), F32), 12 * _nbytes((tm, S5_WIDTH), F32)),
    )(ymm, u, d_skip, w_glu, b_glu, dout)


def _merge_fn(ga, gb, ya, yb):
    return jax.nn.sigmoid(ga) * ya + jax.nn.sigmoid(gb) * yb


def _merge(gab, ya, yb, *, name, tm=512):
    n_tok = ya.shape[0]
    tm = min(tm, n_tok)
    rows = pl.BlockSpec((tm, D_MODEL), lambda i: (i, 0))

    def body(ga_ref, gb_ref, ya_ref, yb_ref, o_ref):
        o_ref[...] = _merge_fn(ga_ref[...], gb_ref[...], ya_ref[...], yb_ref[...]).astype(o_ref.dtype)

    return pl.pallas_call(
        body, name=name, grid=(n_tok // tm,),
        in_specs=[rows, pl.BlockSpec((tm, D_MODEL), lambda i: (i, 1)), rows, rows], out_specs=rows,
        out_shape=jax.ShapeDtypeStruct(ya.shape, MXU_DTYPE),
        compiler_params=_params(5 * _nbytes((tm, D_MODEL), F32), 4 * _nbytes((tm, D_MODEL), F32)),
    )(gab, gab, ya, yb)


def _merge_bwd(gab, ya, yb, dout, *, name, tm=512):
    n_tok = ya.shape[0]
    tm = min(tm, n_tok)
    rows = pl.BlockSpec((tm, D_MODEL), lambda i: (i, 0))

    def body(ga_ref, gb_ref, ya_ref, yb_ref, do_ref, *out_refs):
        _, pull = jax.vjp(_merge_fn, ga_ref[...], gb_ref[...], ya_ref[...], yb_ref[...])
        for ref, val in zip(out_refs, pull(do_ref[...])):
            ref[...] = val.astype(ref.dtype)

    shape = jax.ShapeDtypeStruct(ya.shape, MXU_DTYPE)
    return pl.pallas_call(
        body, name=name, grid=(n_tok // tm,),
        in_specs=[rows, pl.BlockSpec((tm, D_MODEL), lambda i: (i, 1)), rows, rows, rows],
        out_specs=(rows, rows, rows, rows), out_shape=(shape, shape, shape, shape),
        compiler_params=_params(7 * _nbytes((tm, D_MODEL), F32), 6 * _nbytes((tm, D_MODEL), F32)),
    )(gab, gab, ya, yb, dout)


ADA_SHARD = N_MOD * D_MODEL // N_SHARD


def _ada_fwd(c_pad, w_s, b_s, *, name):
    n_r = c_pad.shape[0]

    def body(c_ref, w_ref, b_ref, o_ref):
        sc = jax.nn.silu(c_ref[...]).astype(MXU_DTYPE)
        o_ref[0] = lax.dot_general(sc, w_ref[0].astype(MXU_DTYPE), _NN, preferred_element_type=F32) + b_ref[0]

    return pl.pallas_call(
        body, name=name, grid=(N_SHARD,),
        in_specs=[pl.BlockSpec((n_r, D_MODEL), lambda s: (0, 0)),
                  pl.BlockSpec((1, D_MODEL, ADA_SHARD), lambda s: (s, 0, 0)),
                  pl.BlockSpec((1, 1, ADA_SHARD), lambda s: (s, 0, 0))],
        out_specs=pl.BlockSpec((1, n_r, ADA_SHARD), lambda s: (s, 0, 0)),
        out_shape=jax.ShapeDtypeStruct((N_SHARD, n_r, ADA_SHARD), F32),
        compiler_params=_params(_nbytes((D_MODEL, ADA_SHARD), w_s.dtype), 1 << 20),
    )(c_pad, w_s, b_s)


def _ada_bwd(c_pad, dmod_s, *, name):
    n_r = c_pad.shape[0]

    def body(c_ref, d_ref, dw_ref, db_ref):
        sc = jax.nn.silu(c_ref[...])
        dm = d_ref[0]
        dw_ref[0] = _fdot(sc, dm, _TN)
        db_ref[0] = jnp.sum(dm, axis=0, keepdims=True)

    return pl.pallas_call(
        body, name=name, grid=(N_SHARD,),
        in_specs=[pl.BlockSpec((n_r, D_MODEL), lambda s: (0, 0)), pl.BlockSpec((1, n_r, ADA_SHARD), lambda s: (s, 0, 0))],
        out_specs=(pl.BlockSpec((1, D_MODEL, ADA_SHARD), lambda s: (s, 0, 0)),
                   pl.BlockSpec((1, 1, ADA_SHARD), lambda s: (s, 0, 0))),
        out_shape=(jax.ShapeDtypeStruct((N_SHARD, D_MODEL, ADA_SHARD), F32),
                   jax.ShapeDtypeStruct((N_SHARD, 1, ADA_SHARD), F32)),
        compiler_params=_params(_nbytes((D_MODEL, ADA_SHARD), F32), 2 * _nbytes((D_MODEL, ADA_SHARD), F32)),
    )(c_pad, dmod_s)


def _heads(t, n_b, n_s):
    return t.reshape(n_b, n_s, DN_HEADS, DN_HEAD_DIM).transpose(0, 2, 1, 3)


def _unheads(t):
    n_b, _, n_s, _ = t.shape
    return t.transpose(0, 2, 1, 3).reshape(n_b, n_s, DN_WIDTH)


def _block_diag(blocks):
    n_g, n_r, n_c = blocks.shape
    eye = jnp.eye(n_g, dtype=blocks.dtype)
    return (blocks[:, :, None, :] * eye[:, None, :, None]).reshape(n_g * n_r, n_g * n_c)


def _diag_blocks(mat, n_r, n_c):
    n_g = mat.shape[0] // n_r
    return jnp.stack([mat[g * n_r:(g + 1) * n_r, g * n_c:(g + 1) * n_c] for g in range(n_g)])


def _local_step(x, c, target, wts):
    n_b, n_s, _ = x.shape
    n_tok = n_b * n_s
    flat = lambda t: t.reshape(n_tok, t.shape[-1])
    unflat = lambda t: t.reshape(n_b, n_s, t.shape[-1])
    n_chunks = n_s // CHUNK

    c_pad = jnp.zeros((SUBLANES, D_MODEL), F32).at[:n_b].set(c)
    mod_s = _ada_fwd(c_pad, wts["w_ada"], wts["b_ada"], name="ada_fwd")
    mod = mod_s.transpose(1, 0, 2).reshape(SUBLANES, N_MOD * D_MODEL)[:n_b]
    sh1, sc1, gt1, sh2, sc2, gt2, sh3, sc3, gt3 = [m[:, None, :] for m in jnp.split(mod, N_MOD, axis=-1)]

    a1 = _pre(x, None, None, wts["g_ffn1"], sh1, sc1, 0.0, name="pre1")
    f1, ffn1_saved = _ffn_fwd(flat(a1), wts["w1_ffn1"], wts["w3_ffn1"], wts["w2_ffn1"], "ffn1")
    x1, a2 = _pre(x, unflat(f1), gt1, wts["g_mix"], sh2, sc2, 0.5, name="pre2")
    u = flat(a2)[None]
    p_qkv = _mm_act([(u, wts["w_qkv"])], "nn", name="in_qkv")[0]
    p_z = _mm_act([(u, wts["w_z"])], "nn", name="in_z")[0]
    p_gab = _mm_act([(u, wts["w_gab"])], "nn", name="in_gab")[0]
    p_s5 = _mm_act([(u, wts["w_s5"])], "nn", name="in_s5")[0]
    p_ba = _mm_act([(u, wts["w_ba"])], "nn", name="in_ba")[0]

    qkv_c = _conv_fwd(unflat(p_qkv), wts["conv_qkv"], name="conv_fwd")
    qh, kh, vh = [_heads(t, n_b, n_s) for t in jnp.split(qkv_c, 3, axis=-1)]
    zh = _heads(p_z, n_b, n_s)
    ba = p_ba.reshape(n_b, n_s, BA_PAD)
    bl = ba[:, :, :DN_HEADS].transpose(0, 2, 1)[..., None]
    lac = ba[:, :, DN_HEADS:2 * DN_HEADS].transpose(0, 2, 1)[..., None]
    lar = lac.reshape(n_b, DN_HEADS, n_chunks, 1, CHUNK)
    a_log, dt_bias = wts["a_log"], wts["dt_bias"]
    dn_in = (qh, kh, vh, bl, lac, lar, a_log, dt_bias)
    qd, kd, uc, wc, attn, g_last, dn_inv = _dn_prep(*dn_in, name="dn_prep")
    o, states = _dn_scan(qd, kd, uc, wc, attn, g_last, name="dn_scan")
    og = _dn_post(o, zh, wts["g_onorm"], name="dn_post")
    og_t = _unheads(og).reshape(1, n_tok, DN_WIDTH)
    ya = _mm_act([(og_t, wts["w_proj_a"])], "nn", name="proj_a")[0]

    s5p_in = (wts["lam_re"], wts["lam_im"], wts["log_step"], wts["bt_re"], wts["bt_im"], wts["c_im"])
    lb_re, lb_im, bb_re, bb_im, c_neg = _s5_prep(*s5p_in, name="s5_prep")
    wb_re, wb_im = _block_diag(bb_re), _block_diag(bb_im)
    wb = jnp.concatenate([wb_re, wb_im], axis=1)[None]
    wc_re = _block_diag(wts["c_re"].transpose(0, 2, 1))
    wc_im = _block_diag(c_neg.transpose(0, 2, 1))
    lbr, lbi = lb_re.reshape(1, S5_LANES), lb_im.reshape(1, S5_LANES)
    bu = _mm_act([(p_s5[None], wb)], "nn", name="s5_bu")[0]
    x_re, x_im = _s5_scan(unflat(bu), lbr, lbi, name="s5_scan")
    xr_t, xi_t = x_re.reshape(1, n_tok, S5_LANES), x_im.reshape(1, n_tok, S5_LANES)
    ymm = _mm_act([(xr_t, wc_re[None]), (xi_t, wc_im[None])], "nn", name="s5_y")[0]
    y2 = _s5_out(ymm, p_s5, wts["d_skip"], wts["w_glu"], wts["b_glu"], name="s5_out")
    yb = _mm_act([(y2[None], wts["w_proj_b"])], "nn", name="proj_b")[0]

    merged = _merge(p_gab, ya, yb, name="merge")
    m_out = _mm_act([(merged[None], wts["w_out"])], "nn", name="mix_out")[0]
    x2, a3 = _pre(x1, unflat(m_out), gt2, wts["g_ffn2"], sh3, sc3, 1.0, name="pre3")
    f3, ffn2_saved = _ffn_fwd(flat(a3), wts["w1_ffn2"], wts["w3_ffn2"], wts["w2_ffn2"], "ffn2")

    g = {}
    loss, dx2_res, df3, dgt3, g["g_final"] = _final(x2, unflat(f3), gt3, wts["g_final"], target, name="final")
    da3, g["w1_ffn2"], g["w3_ffn2"], g["w2_ffn2"] = _ffn_bwd(
        flat(a3), wts["w1_ffn2"], wts["w3_ffn2"], wts["w2_ffn2"], ffn2_saved, flat(df3), "ffn2")
    dx1_res, dm_out, dgt2, g["g_ffn2"], dsh3, dsc3 = _pre_bwd(
        x1, unflat(m_out), gt2, wts["g_ffn2"], sh3, sc3, 1.0, unflat(da3), dx2_res, name="pre3_bwd")
    dm_out = flat(dm_out)[None]
    dmerged = _mm_act([(dm_out, wts["w_out"])], "nt", name="mix_out_bwd")[0]
    g["w_out"] = _mm_tn(merged[None], dm_out, name="dw_out")[0]
    dga, dgb, dya, dyb = _merge_bwd(p_gab, ya, yb, dmerged, name="merge_bwd")

    dy2 = _mm_act([(dyb[None], wts["w_proj_b"])], "nt", name="proj_b_bwd")[0]
    g["w_proj_b"] = _mm_tn(y2[None], dyb[None], name="dw_proj_b")[0]
    dymm, du_skip, g["d_skip"], g["w_glu"], g["b_glu"] = _s5_out_bwd(
        ymm, p_s5, wts["d_skip"], wts["w_glu"], wts["b_glu"], dy2, name="s5_out_bwd")
    wc_cat = jnp.concatenate([wc_re, wc_im], axis=0)[None]
    dxs = _mm_act([(dymm[None], wc_cat)], "nt", name="s5_y_bwd")[0]
    dwc_re = _mm_tn(xr_t, dymm[None], name="dwc_re")[0]
    dwc_im = _mm_tn(xi_t, dymm[None], name="dwc_im")[0]
    a_re, a_im, dlb_re, dlb_im = _s5_scan_bwd(unflat(dxs), x_re, x_im, lbr, lbi, name="s5_scan_bwd")
    ar_t, ai_t = a_re.reshape(1, n_tok, S5_LANES), a_im.reshape(1, n_tok, S5_LANES)
    dp_s5 = _mm_act([(ar_t, wb_re[None]), (ai_t, wb_im[None])], "nt", add=du_skip, out_dtype=MXU_DTYPE, name="s5_bu_bwd")[0]
    dwb_re = _mm_tn(p_s5[None], ar_t, name="dwb_re")[0]
    dwb_im = _mm_tn(p_s5[None], ai_t, name="dwb_im")[0]
    g["c_re"] = _diag_blocks(dwc_re, S5_STATE, S5_GROUP_CH).transpose(0, 2, 1)
    s5_cts = (dlb_re.reshape(lb_re.shape), dlb_im.reshape(lb_im.shape),
              _diag_blocks(dwb_re, S5_GROUP_CH, S5_STATE), _diag_blocks(dwb_im, S5_GROUP_CH, S5_STATE),
              _diag_blocks(dwc_im, S5_STATE, S5_GROUP_CH).transpose(0, 2, 1))
    g["lam_re"], g["lam_im"], g["log_step"], g["bt_re"], g["bt_im"], g["c_im"] = _s5_prep_bwd(
        *s5p_in, s5_cts, name="s5_prep_bwd")

    dog = _mm_act([(dya[None], wts["w_proj_a"])], "nt", name="proj_a_bwd")[0]
    g["w_proj_a"] = _mm_tn(og_t, dya[None], name="dw_proj_a")[0]
    do, dzh, g["g_onorm"] = _dn_post_bwd(o, zh, wts["g_onorm"], _heads(dog, n_b, n_s), name="dn_post_bwd")
    scan_cts = _dn_scan_bwd(qd, kd, uc, wc, attn, g_last, states, do, name="dn_scan_bwd")
    dqh, dkh, dvh, dbl, dlac, dlar, g["a_log"], g["dt_bias"] = _dn_prep_bwd(*dn_in, dn_inv, uc, wc, scan_cts, name="dn_prep_bwd")
    dqkv_c = jnp.concatenate([_unheads(t) for t in (dqh, dkh, dvh)], axis=-1)
    dqkv, g["conv_qkv"] = _conv_bwd(unflat(p_qkv), wts["conv_qkv"], dqkv_c, name="conv_bwd")
    dla = dlac[..., 0] + dlar.reshape(n_b, DN_HEADS, n_s)
    dba = jnp.concatenate([dbl[..., 0].transpose(0, 2, 1), dla.transpose(0, 2, 1),
                           jnp.zeros((n_b, n_s, BA_PAD - 2 * DN_HEADS), F32)], axis=-1).astype(MXU_DTYPE)
    dz = _unheads(dzh)

    dps = {"w_qkv": flat(dqkv)[None], "w_z": flat(dz)[None], "w_ga": dga[None], "w_gb": dgb[None],
           "w_s5": dp_s5[None], "w_ba": flat(dba)[None]}
    w_ga, w_gb = wts["w_gab"][:, :, :D_MODEL], wts["w_gab"][:, :, D_MODEL:]
    w_of = dict(wts, w_ga=w_ga, w_gb=w_gb)
    du = _mm_act([(dps[k], w_of[k]) for k in dps], "nt", name="in_bwd")[0]
    for k in dps:
        g[k] = _mm_tn(u, dps[k], name=f"d{k}")[0]
    dx0_res, df1, dgt1, g["g_mix"], dsh2, dsc2 = _pre_bwd(
        x, unflat(f1), gt1, wts["g_mix"], sh2, sc2, 0.5, unflat(du), dx1_res, name="pre2_bwd")
    da1, g["w1_ffn1"], g["w3_ffn1"], g["w2_ffn1"] = _ffn_bwd(
        flat(a1), wts["w1_ffn1"], wts["w3_ffn1"], wts["w2_ffn1"], ffn1_saved, flat(df1), "ffn1")
    grad_x, g["g_ffn1"], dsh1, dsc1 = _pre_bwd(
        x, None, None, wts["g_ffn1"], sh1, sc1, 0.0, unflat(da1), dx0_res, name="pre1_bwd")

    dmod = jnp.concatenate([t[:, 0, :] for t in (dsh1, dsc1, dgt1, dsh2, dsc2, dgt2, dsh3, dsc3, dgt3)], axis=-1)
    dmod_pad = jnp.zeros((SUBLANES, N_MOD * D_MODEL), F32).at[:n_b].set(dmod)
    dmod_s = dmod_pad.reshape(SUBLANES, N_SHARD, ADA_SHARD).transpose(1, 0, 2)
    g["w_ada"], g["b_ada"] = _ada_bwd(c_pad, dmod_s, name="ada_bwd")
    return loss, grad_x, g


IN_SPLITS = (("w_qkv", 3 * DN_WIDTH), ("w_z", DN_WIDTH), ("w_ba", 2 * DN_HEADS), ("w_s5", S5_WIDTH),
             ("w_ga", D_MODEL), ("w_gb", D_MODEL))
SHARDED = ("w_ada", "w1_ffn1", "w3_ffn1", "w2_ffn1", "w_in", "conv_qkv", "w_glu", "w_proj_a", "w_proj_b", "w_out",
           "w1_ffn2", "w3_ffn2", "w2_ffn2")
COLUMN_SHARDED = ("w_ada", "w1_ffn1", "w3_ffn1", "w_in", "conv_qkv", "w_proj_a", "w_proj_b", "w1_ffn2", "w3_ffn2")


def _cat_columns(stack):
    return stack.transpose(1, 0, 2).reshape(stack.shape[1], N_SHARD * stack.shape[2])


def _split_columns(full):
    n_r, n_c = full.shape
    return full.reshape(n_r, N_SHARD, n_c // N_SHARD).transpose(1, 0, 2)


def _gathered_weights(st, rep):
    w = {k: st[k] for k in ("w_ada", "w1_ffn1", "w3_ffn1", "w2_ffn1", "w1_ffn2", "w3_ffn2", "w2_ffn2")}
    w["b_ada"] = rep["b_ada"].reshape(N_SHARD, 1, ADA_SHARD)
    for k in ("g_ffn1", "g_mix", "g_ffn2", "g_final"):
        w[k] = rep[k].reshape(1, D_MODEL)
    w_in = _cat_columns(st["w_in"])
    start = 0
    for k, size in IN_SPLITS:
        w[k] = w_in[None, :, start:start + size]
        start += size
    w["w_gab"] = jnp.concatenate([w.pop("w_ga"), w.pop("w_gb")], axis=-1)
    w["w_ba"] = jnp.pad(w["w_ba"], ((0, 0), (0, 0), (0, BA_PAD - 2 * DN_HEADS)))
    w["conv_qkv"] = _cat_columns(st["conv_qkv"])
    w["a_log"] = rep["a_log"].reshape(DN_HEADS, 1, 1)
    w["dt_bias"] = rep["dt_bias"].reshape(DN_HEADS, 1, 1)
    w["g_onorm"] = rep["g_onorm"].reshape(1, DN_HEAD_DIM)
    w["lam_re"] = rep["lam_re"].reshape(S5_GROUPS, 1, S5_STATE)
    w["lam_im"] = rep["lam_im"].reshape(S5_GROUPS, 1, S5_STATE)
    w["log_step"] = rep["log_step"].reshape(S5_GROUPS, 1, 1)
    w["bt_re"] = rep["b_re"][0].transpose(0, 2, 1)
    w["bt_im"] = rep["b_im"][0].transpose(0, 2, 1)
    w["c_re"] = rep["c_re"][0]
    w["c_im"] = rep["c_im"][0]
    w["d_skip"] = rep["d_skip"].reshape(1, S5_WIDTH)
    w["b_glu"] = rep["b_glu"].reshape(1, S5_WIDTH)
    w["w_glu"] = st["w_glu"].reshape(S5_WIDTH, S5_WIDTH)
    w["w_proj_a"] = _cat_columns(st["w_proj_a"])[None]
    w["w_proj_b"] = _cat_columns(st["w_proj_b"])[None]
    w["w_out"] = st["w_out"].reshape(1, D_MODEL, D_MODEL)
    return w


def _grads_to_problem_layout(g):
    st = {k: g[k] for k in ("w_ada", "w1_ffn1", "w3_ffn1", "w2_ffn1", "w1_ffn2", "w3_ffn2", "w2_ffn2")}
    w_in = jnp.concatenate([g[k][:, :size] for k, size in IN_SPLITS], axis=1)
    st["w_in"] = _split_columns(w_in)
    st["w_glu"] = g["w_glu"].reshape(N_SHARD, S5_WIDTH // N_SHARD, S5_WIDTH)
    st["w_proj_a"] = _split_columns(g["w_proj_a"])
    st["w_proj_b"] = _split_columns(g["w_proj_b"])
    st["w_out"] = g["w_out"].reshape(N_SHARD, D_MODEL // N_SHARD, D_MODEL)
    small = {
        "b_ada": g["b_ada"].reshape(1, N_MOD * D_MODEL),
        "g_ffn1": g["g_ffn1"], "g_mix": g["g_mix"], "g_ffn2": g["g_ffn2"], "g_final": g["g_final"].reshape(D_MODEL),
        "conv_qkv": g["conv_qkv"][None],
        "a_log": g["a_log"].reshape(1, DN_HEADS), "dt_bias": g["dt_bias"].reshape(1, DN_HEADS),
        "g_onorm": g["g_onorm"],
        "lam_re": g["lam_re"].reshape(1, S5_GROUPS, S5_STATE), "lam_im": g["lam_im"].reshape(1, S5_GROUPS, S5_STATE),
        "log_step": g["log_step"].reshape(1, S5_GROUPS),
        "b_re": g["bt_re"].transpose(0, 2, 1)[None], "b_im": g["bt_im"].transpose(0, 2, 1)[None],
        "c_re": g["c_re"][None], "c_im": g["c_im"][None],
        "d_skip": g["d_skip"], "b_glu": g["b_glu"],
    }
    return st, small


ELEMENTWISE_BLOCK_BYTES = 1 << 20


def _row_tile(n_rows, n_cols, n_lead=1, multiple=SUBLANES):
    best = None
    for t in range(multiple, n_rows + 1, multiple):
        if n_rows % t == 0 and n_lead * t * n_cols * 4 <= ELEMENTWISE_BLOCK_BYTES:
            best = t
    return best if best is not None else n_rows


def _add_sibling_half(g4, recv, my_c, *, name):
    n_sh, _, n_h, n_c = g4.shape
    th = _row_tile(n_h, n_c, multiple=2 * SUBLANES)

    def body(c_ref, g_ref, r_ref, o_ref):
        o_ref[0] = (g_ref[0, 0] + r_ref[0]).astype(o_ref.dtype)

    grid_spec = pltpu.PrefetchScalarGridSpec(
        num_scalar_prefetch=1, grid=(n_sh, n_h // th),
        in_specs=[pl.BlockSpec((1, 1, th, n_c), lambda s, i, c_ref: (s, c_ref[0], i, 0)),
                  pl.BlockSpec((1, th, n_c), lambda s, i, c_ref: (s, i, 0))],
        out_specs=pl.BlockSpec((1, th, n_c), lambda s, i, c_ref: (s, i, 0)))
    return pl.pallas_call(
        body, name=name, grid_spec=grid_spec, out_shape=jax.ShapeDtypeStruct((n_sh, n_h, n_c), MXU_DTYPE),
        compiler_params=_params(3 * _nbytes((th, n_c), F32)),
    )(my_c, g4, recv)


def _sum_slots(parts, *, name):
    n_p, n_r, n_c = parts.shape
    th = _row_tile(n_r, n_c, n_p)

    def body(p_ref, o_ref):
        total = p_ref[0].astype(F32)
        for k in range(1, n_p):
            total = total + p_ref[k].astype(F32)
        o_ref[...] = total

    return pl.pallas_call(
        body, name=name, grid=(n_r // th,),
        in_specs=[pl.BlockSpec((n_p, th, n_c), lambda i: (0, i, 0))],
        out_specs=pl.BlockSpec((th, n_c), lambda i: (i, 0)),
        out_shape=jax.ShapeDtypeStruct((n_r, n_c), F32),
        compiler_params=_params((n_p + 1) * _nbytes((th, n_c), F32)),
    )(parts)


def _cast_into_slot(w, place, dtype, *, name):
    n_r, n_c = w.shape
    th = _row_tile(n_r, n_c, multiple=2 * SUBLANES)

    def body(p_ref, w_ref, o_ref):
        o_ref[0] = w_ref[...].astype(o_ref.dtype)

    grid_spec = pltpu.PrefetchScalarGridSpec(
        num_scalar_prefetch=1, grid=(n_r // th,),
        in_specs=[pl.BlockSpec((th, n_c), lambda i, p: (i, 0))],
        out_specs=pl.BlockSpec((1, th, n_c), lambda i, p: (p[1], i, 0)))
    return pl.pallas_call(
        body, name=name, grid_spec=grid_spec, out_shape=jax.ShapeDtypeStruct((N_SHARD, n_r, n_c), dtype),
        compiler_params=_params(2 * _nbytes((th, n_c), F32)),
    )(place, w)


def _sum_chips(own, parts, place, *, name):
    n_sh, n_h, n_c = own.shape
    th = _row_tile(n_h, n_c, n_sh, multiple=2 * SUBLANES)

    def body(p_ref, own_ref, a_ref, b_ref, c_ref, o_ref):
        o_ref[0] = ((own_ref[0].astype(F32) + a_ref[0].astype(F32)) + b_ref[0].astype(F32)) + c_ref[0].astype(F32)

    slab = lambda k: pl.BlockSpec((1, th, n_c), lambda i, p, k=k: (p[k], i, 0))
    grid_spec = pltpu.PrefetchScalarGridSpec(
        num_scalar_prefetch=1, grid=(n_h // th,),
        in_specs=[slab(1), slab(2), slab(3), slab(4)], out_specs=slab(0))
    return pl.pallas_call(
        body, name=name, grid_spec=grid_spec, out_shape=jax.ShapeDtypeStruct((2, n_h, n_c), F32),
        compiler_params=_params(5 * _nbytes((th, n_c), F32)),
    )(place, own, parts, parts, parts)


def _adamw(w, g, m, v, *, name):
    n_r, n_c = w.shape
    th = _row_tile(n_r, n_c)
    bias1 = 1.0 - ADAM_B1 ** ADAM_STEP
    bias2 = 1.0 - ADAM_B2 ** ADAM_STEP

    def body(w_ref, g_ref, m_ref, v_ref, d_ref, mo_ref, vo_ref):
        gv = g_ref[...]
        m_new = ADAM_B1 * m_ref[...] + (1.0 - ADAM_B1) * gv
        v_new = ADAM_B2 * v_ref[...] + (1.0 - ADAM_B2) * jnp.square(gv)
        d_ref[...] = -ADAM_LR * ((m_new / bias1) / (jnp.sqrt(v_new / bias2) + ADAM_EPS) + ADAM_WD * w_ref[...])
        mo_ref[...] = m_new
        vo_ref[...] = v_new

    spec = pl.BlockSpec((th, n_c), lambda i: (i, 0))
    shape = jax.ShapeDtypeStruct((n_r, n_c), F32)
    return pl.pallas_call(
        body, name=name, grid=(n_r // th,), in_specs=[spec] * 4, out_specs=(spec,) * 3, out_shape=(shape,) * 3,
        compiler_params=_params(7 * _nbytes((th, n_c), F32)),
    )(w, g, m, v)


CHIP_FLIPS = ((1, 0), (0, 1), (1, 1))
DEVICE_FLIPS = tuple((fx, fy, fc) for fx in (0, 1) for fy in (0, 1) for fc in (0, 1))[1:]


def _exchange(ins, out_shapes, plan, n_local, n_remote, *, name, aliased=False):
    n_in, n_out = len(ins), len(out_shapes)

    def body(*refs):
        in_refs, out_refs = refs[:n_in], refs[n_in:n_in + n_out]
        send_sems, recv_sems, local_sems = refs[n_in + n_out:]
        me = (lax.axis_index("x"), lax.axis_index("y"), lax.axis_index("c"))
        local, remote = plan(in_refs, out_refs, me)
        assert len(local) == n_local and len(remote) == n_remote
        here = [pltpu.make_async_copy(src, dst, local_sems.at[i]) for i, (src, dst) in enumerate(local)]
        for cp in here:
            cp.start()
        sends = [pltpu.make_async_remote_copy(src_ref=src, dst_ref=dst, send_sem=send_sems.at[i], recv_sem=recv_sems.at[i],
                                              device_id=peer, device_id_type=pl.DeviceIdType.MESH)
                 for i, (src, dst, _, peer) in enumerate(remote)]
        for cp in sends:
            cp.start()
        for i, (src, _, landing, peer) in enumerate(remote):
            pltpu.make_async_remote_copy(src_ref=src, dst_ref=landing, send_sem=send_sems.at[i], recv_sem=recv_sems.at[i],
                                         device_id=peer, device_id_type=pl.DeviceIdType.MESH).wait_recv()
        for cp in sends:
            cp.wait_send()
        for cp in here:
            cp.wait()

    any_spec = pl.BlockSpec(memory_space=pl.ANY)
    return pl.pallas_call(
        body, name=name, in_specs=[any_spec] * n_in, out_specs=tuple([any_spec] * n_out), out_shape=tuple(out_shapes),
        scratch_shapes=[pltpu.SemaphoreType.DMA((n_remote,)), pltpu.SemaphoreType.DMA((n_remote,)),
                        pltpu.SemaphoreType.DMA((max(n_local, 1),))],
        input_output_aliases={k: k for k in range(n_in)} if aliased else {},
    )(*ins)


def _gather_shards(stacks, *, name):
    n = len(stacks)
    halved = [a.shape[1] >= 32 for a in stacks]
    n_ici = len(CHIP_FLIPS) * n
    n_pass = len(CHIP_FLIPS) * sum(halved)

    def body(*refs):
        outs = refs[n:2 * n]
        send_sems, recv_sems = refs[2 * n:]
        x, y, c = lax.axis_index("x"), lax.axis_index("y"), lax.axis_index("c")
        mine = 2 * x + y

        def rows(k, slot, half):
            if not halved[k]:
                return outs[k].at[slot]
            n_h = stacks[k].shape[1] // 2
            return outs[k].at[slot, pl.ds(pl.multiple_of(half * n_h, 16), n_h)]

        def copy(i, src, dst, peer):
            return pltpu.make_async_remote_copy(src_ref=src, dst_ref=dst, send_sem=send_sems.at[i], recv_sem=recv_sems.at[i],
                                                device_id=peer, device_id_type=pl.DeviceIdType.MESH)

        started = []
        for j, (fx, fy) in enumerate(CHIP_FLIPS):
            for k in range(n):
                cp = copy(j * n + k, rows(k, mine, c), rows(k, mine, c), (x ^ fx, y ^ fy, c))
                cp.start()
                started.append(cp)
        i_pass = n_ici
        expect = []
        for j, (fx, fy) in enumerate(CHIP_FLIPS):
            peer_chip = 2 * (x ^ fx) + (y ^ fy)
            for k in range(n):
                landed = rows(k, peer_chip, c)
                copy(j * n + k, landed, landed, (x ^ fx, y ^ fy, c)).wait_recv()
                if halved[k]:
                    cp = copy(i_pass, landed, landed, (x, y, 1 - c))
                    cp.start()
                    started.append(cp)
                    expect.append((i_pass, rows(k, peer_chip, 1 - c)))
                    i_pass += 1
        for i, landing in expect:
            copy(i, landing, landing, (x, y, 1 - c)).wait_recv()
        for cp in started:
            cp.wait_send()

    any_spec = pl.BlockSpec(memory_space=pl.ANY)
    n_sem = n_ici + n_pass
    return pl.pallas_call(
        body, name=name, in_specs=[any_spec] * n, out_specs=tuple([any_spec] * n),
        out_shape=tuple(jax.ShapeDtypeStruct(a.shape, a.dtype) for a in stacks),
        scratch_shapes=[pltpu.SemaphoreType.DMA((n_sem,)), pltpu.SemaphoreType.DMA((n_sem,))],
        input_output_aliases={k: k for k in range(n)},
    )(*stacks)


def _swap_sibling_halves(g4s, *, name):
    n = len(g4s)

    def plan(in_refs, out_refs, me):
        x, y, c = me
        remote = [(in_refs[k].at[:, 1 - c], out_refs[k], out_refs[k], (x, y, 1 - c)) for k in range(n)]
        return [], remote

    shapes = [jax.ShapeDtypeStruct((a.shape[0],) + a.shape[2:], a.dtype) for a in g4s]
    return _exchange(g4s, shapes, plan, 0, n, name=name)


def _scatter_to_chips(hs, *, name):
    n = len(hs)

    def plan(in_refs, out_refs, me):
        x, y, c = me
        mine = 2 * x + y
        remote = []
        for fx, fy in CHIP_FLIPS:
            px, py = x ^ fx, y ^ fy
            peer = 2 * px + py
            for k in range(n):
                remote.append((in_refs[k].at[peer], out_refs[k].at[mine], out_refs[k].at[peer], (px, py, c)))
        return [], remote

    shapes = [jax.ShapeDtypeStruct(a.shape, a.dtype) for a in hs]
    return _exchange(hs, shapes, plan, 0, len(CHIP_FLIPS) * n, name=name)


def _join_sibling_halves(rs, *, name):
    n = len(rs)

    def plan(in_refs, out_refs, me):
        x, y, c = me
        remote = [(out_refs[k].at[c], out_refs[k].at[c], out_refs[k].at[1 - c], (x, y, 1 - c)) for k in range(n)]
        return [], remote

    shapes = [jax.ShapeDtypeStruct(a.shape, a.dtype) for a in rs]
    return _exchange(rs, shapes, plan, 0, n, name=name, aliased=True)


def _gather_all_devices(packed, *, name):
    def plan(in_refs, out_refs, me):
        x, y, c = me
        mine = 4 * x + 2 * y + c
        remote = []
        for fx, fy, fc in DEVICE_FLIPS:
            px, py, pc = x ^ fx, y ^ fy, c ^ fc
            remote.append((in_refs[0], out_refs[0].at[mine], out_refs[0].at[4 * px + 2 * py + pc], (px, py, pc)))
        return [(in_refs[0], out_refs[0].at[mine])], remote

    shape = jax.ShapeDtypeStruct((2 * N_SHARD,) + packed.shape, packed.dtype)
    return _exchange([packed], [shape], plan, 1, len(DEVICE_FLIPS), name=name)[0]


WEIGHT_NAMES = ("w_ada", "b_ada", "g_ffn1", "w1_ffn1", "w3_ffn1", "w2_ffn1", "g_mix", "w_in", "conv_qkv", "a_log",
                "dt_bias", "g_onorm", "lam_re", "lam_im", "log_step", "b_re", "b_im", "c_re", "c_im", "d_skip", "w_glu",
                "b_glu", "w_proj_a", "w_proj_b", "w_out", "g_ffn2", "w1_ffn2", "w3_ffn2", "w2_ffn2", "g_final")
LARGE = tuple(n for n in SHARDED if n != "conv_qkv")
SMALL = tuple(n for n in WEIGHT_NAMES if n not in LARGE)
PACK_ROW = SUBLANES * LANES


def _pack(arrays):
    flat = jnp.concatenate([a.reshape(-1) for a in arrays])
    n_pad = -flat.shape[0] % PACK_ROW
    return jnp.pad(flat, (0, n_pad)).reshape(-1, LANES)


def _unpack(packed, shapes):
    flat = packed.reshape(-1)
    out, start = [], 0
    for s in shapes:
        size = math.prod(s)
        out.append(flat[start:start + size].reshape(s))
        start += size
    return out


def _step(x, c, target, weights, m_in, v_in):
    xi, yi, ci = lax.axis_index("x"), lax.axis_index("y"), lax.axis_index("c")
    my_chip = 2 * xi + yi

    others = [k + (k >= my_chip).astype(jnp.int32) for k in range(N_SHARD - 1)]
    place = jnp.stack([ci, my_chip] + others).astype(jnp.int32)

    slots = [_cast_into_slot(weights[n][0], place, F32 if n == "conv_qkv" else MXU_DTYPE, name=f"cast_{n}") for n in SHARDED]
    stacks = dict(zip(SHARDED, _gather_shards(slots, name="gather_weights")))
    rep = {n: weights[n] for n in WEIGHT_NAMES if n not in SHARDED}
    loss, grad_x, g = _local_step(x, c, target, _gathered_weights(stacks, rep))
    g_stacks, g_small = _grads_to_problem_layout(g)

    g4s = [g_stacks[n].reshape(N_SHARD, 2, g_stacks[n].shape[1] // 2, g_stacks[n].shape[2]) for n in LARGE]
    from_sibling = _swap_sibling_halves(g4s, name="swap_sibling_halves")
    chip_sums = [_add_sibling_half(a, r, place, name=f"chip_sum_{n}") for n, a, r in zip(LARGE, g4s, from_sibling)]
    from_chips = _scatter_to_chips(chip_sums, name="scatter_to_chips")
    reduced = [_sum_chips(h, p, place, name=f"sum_chips_{n}") for n, h, p in zip(LARGE, chip_sums, from_chips)]
    joined = _join_sibling_halves(reduced, name="join_sibling_halves")
    grads = {n: j.reshape(1, 2 * j.shape[1], j.shape[2]) for n, j in zip(LARGE, joined)}

    small_shapes = [g_small[n].shape for n in SMALL] + [(1, 1)]
    packed = _pack([g_small[n] for n in SMALL] + [loss])
    total = _sum_slots(_gather_all_devices(packed, name="gather_small"), name="sum_small")
    *small_grads, loss_sum = _unpack(total, small_shapes)
    grads.update(zip(SMALL, small_grads))
    n_conv = weights["conv_qkv"].shape[-1]
    grads["conv_qkv"] = lax.dynamic_slice_in_dim(grads["conv_qkv"], my_chip * n_conv, n_conv, axis=2)

    delta, new_m, new_v = {}, {}, {}
    for n in LARGE + ("conv_qkv",):
        two_d = lambda a: a.reshape(-1, a.shape[-1])
        outs = _adamw(two_d(weights[n]), two_d(grads[n]), two_d(m_in[n]), two_d(v_in[n]), name=f"adamw_{n}")
        delta[n], new_m[n], new_v[n] = [o.reshape(weights[n].shape) for o in outs]
    packed_names = tuple(n for n in SMALL if n != "conv_qkv")
    shapes = [weights[n].shape for n in packed_names]
    outs = _adamw(*[_pack([d[n] for n in packed_names]) for d in (weights, grads, m_in, v_in)], name="adamw_small")
    for d, o in zip((delta, new_m, new_v), outs):
        d.update(zip(packed_names, _unpack(o, shapes)))
    return (loss_sum.reshape(()), grad_x, *[grads[n] for n in WEIGHT_NAMES], *[delta[n] for n in WEIGHT_NAMES],
            *[new_m[n] for n in WEIGHT_NAMES], *[new_v[n] for n in WEIGHT_NAMES])


def kernel(x, c, w_ada, b_ada, g_ffn1, w1_ffn1, w3_ffn1, w2_ffn1, g_mix, w_in, conv_qkv, a_log, dt_bias, g_onorm, lam_re, lam_im, log_step, b_re, b_im, c_re, c_im, d_skip, w_glu, b_glu, w_proj_a, w_proj_b, w_out, g_ffn2, w1_ffn2, w3_ffn2, w2_ffn2, g_final, loss_target, m_w_ada, m_b_ada, m_g_ffn1, m_w1_ffn1, m_w3_ffn1, m_w2_ffn1, m_g_mix, m_w_in, m_conv_qkv, m_a_log, m_dt_bias, m_g_onorm, m_lam_re, m_lam_im, m_log_step, m_b_re, m_b_im, m_c_re, m_c_im, m_d_skip, m_w_glu, m_b_glu, m_w_proj_a, m_w_proj_b, m_w_out, m_g_ffn2, m_w1_ffn2, m_w3_ffn2, m_w2_ffn2, m_g_final, v_w_ada, v_b_ada, v_g_ffn1, v_w1_ffn1, v_w3_ffn1, v_w2_ffn1, v_g_mix, v_w_in, v_conv_qkv, v_a_log, v_dt_bias, v_g_onorm, v_lam_re, v_lam_im, v_log_step, v_b_re, v_b_im, v_c_re, v_c_im, v_d_skip, v_w_glu, v_b_glu, v_w_proj_a, v_w_proj_b, v_w_out, v_g_ffn2, v_w1_ffn2, v_w3_ffn2, v_w2_ffn2, v_g_final):
    w_vals = (w_ada, b_ada, g_ffn1, w1_ffn1, w3_ffn1, w2_ffn1, g_mix, w_in, conv_qkv, a_log, dt_bias, g_onorm, lam_re, lam_im, log_step, b_re, b_im, c_re, c_im, d_skip, w_glu, b_glu, w_proj_a, w_proj_b, w_out, g_ffn2, w1_ffn2, w3_ffn2, w2_ffn2, g_final)
    m_vals = (m_w_ada, m_b_ada, m_g_ffn1, m_w1_ffn1, m_w3_ffn1, m_w2_ffn1, m_g_mix, m_w_in, m_conv_qkv, m_a_log, m_dt_bias, m_g_onorm, m_lam_re, m_lam_im, m_log_step, m_b_re, m_b_im, m_c_re, m_c_im, m_d_skip, m_w_glu, m_b_glu, m_w_proj_a, m_w_proj_b, m_w_out, m_g_ffn2, m_w1_ffn2, m_w3_ffn2, m_w2_ffn2, m_g_final)
    v_vals = (v_w_ada, v_b_ada, v_g_ffn1, v_w1_ffn1, v_w3_ffn1, v_w2_ffn1, v_g_mix, v_w_in, v_conv_qkv, v_a_log, v_dt_bias, v_g_onorm, v_lam_re, v_lam_im, v_log_step, v_b_re, v_b_im, v_c_re, v_c_im, v_d_skip, v_w_glu, v_b_glu, v_w_proj_a, v_w_proj_b, v_w_out, v_g_ffn2, v_w1_ffn2, v_w3_ffn2, v_w2_ffn2, v_g_final)
    return _step(x, c, loss_target, dict(zip(WEIGHT_NAMES, w_vals)), dict(zip(WEIGHT_NAMES, m_vals)),
                 dict(zip(WEIGHT_NAMES, v_vals)))
```

```python
import functools
import math

import jax
import jax.numpy as jnp
from jax import lax
from jax.experimental import pallas as pl
from jax.experimental.pallas import tpu as pltpu

F32 = jnp.float32
BF16 = jnp.bfloat16
MXU_DTYPE = BF16

D_MODEL = 1024
D_FF = 2816
DN_HEADS = 8
DN_HEAD_DIM = 64
DN_WIDTH = DN_HEADS * DN_HEAD_DIM
CONV_WIDTH = 4
CHUNK = 64
S5_GROUP_CH = 16
S5_GROUPS = 32
S5_WIDTH = S5_GROUPS * S5_GROUP_CH
S5_STATE = 64
S5_LANES = S5_GROUPS * S5_STATE
N_MOD = 9
EPS = 1e-6
N_SHARD = 4
FF_SHARD = D_FF // N_SHARD
BA_PAD = 128

ADAM_LR = 0.001
ADAM_B1 = 0.9
ADAM_B2 = 0.999
ADAM_EPS = 1e-08
ADAM_WD = 0.01
ADAM_STEP = 10

VMEM_BYTES_V7X = 64 * 1024 * 1024
SUBLANES = 8
LANES = 128


def _params(block_bytes, extra_bytes=0):
    need = 2 * block_bytes + extra_bytes + (4 << 20)
    return pltpu.CompilerParams(vmem_limit_bytes=int(min(max(need, 16 << 20), VMEM_BYTES_V7X - (8 << 20))))


def _nbytes(shape, dtype):
    return math.prod(shape) * jnp.dtype(dtype).itemsize


HBM_OPERAND_BYTES = 1 << 20


def _hbm(*args):
    return [pltpu.with_memory_space_constraint(a, pltpu.HBM) if _nbytes(a.shape, a.dtype) >= HBM_OPERAND_BYTES else a
            for a in args]


_NN = (((1,), (0,)), ((), ()))
_NT = (((1,), (1,)), ((), ()))
_TN = (((0,), (0,)), ((), ()))


def _mm_act(pairs, mode, *, name, out_sharded=False, reduce_shards=False, out_dtype=F32, add=None, tm=512):
    n_tok = pairs[0][0].shape[1]
    n_out = pairs[0][1].shape[2] if mode == "nn" else pairs[0][1].shape[1]
    tm = min(tm, n_tok)
    tn = n_out if n_out <= 1536 else 1024
    assert n_tok % tm == 0 and n_out % tn == 0
    n_so = N_SHARD if out_sharded else 1
    n_red = N_SHARD if reduce_shards else 1
    grid = (n_so, n_tok // tm, n_out // tn, n_red)
    dims = _NN if mode == "nn" else _NT

    def shard_of(n_sh):
        if n_sh == 1:
            return lambda s, r: 0
        return (lambda s, r: s) if out_sharded else (lambda s, r: r)

    in_specs, args, blk = [], [], 0
    for a, b in pairs:
        k_dim = a.shape[2]
        sa, sb = shard_of(a.shape[0]), shard_of(b.shape[0])
        in_specs.append(pl.BlockSpec((1, tm, k_dim), lambda s, i, j, r, sa=sa: (sa(s, r), i, 0)))
        if mode == "nn":
            assert b.shape[1] == k_dim
            in_specs.append(pl.BlockSpec((1, k_dim, tn), lambda s, i, j, r, sb=sb: (sb(s, r), 0, j)))
        else:
            assert b.shape[2] == k_dim
            in_specs.append(pl.BlockSpec((1, tn, k_dim), lambda s, i, j, r, sb=sb: (sb(s, r), j, 0)))
        args += [a, b]
        blk += _nbytes((tm, k_dim), a.dtype) + _nbytes((k_dim, tn), b.dtype)
    if add is not None:
        in_specs.append(pl.BlockSpec((1, tm, tn), lambda s, i, j, r: (s, i, j)))
        args.append(add)
        blk += _nbytes((tm, tn), F32)
    blk += _nbytes((tm, tn), out_dtype)
    n_pairs = len(pairs)

    def body(*refs):
        out_ref = refs[2 * n_pairs + (add is not None)]
        acc = None
        for k in range(n_pairs):
            a = refs[2 * k][0].astype(MXU_DTYPE)
            b = refs[2 * k + 1][0].astype(MXU_DTYPE)
            d = lax.dot_general(a, b, dims, preferred_element_type=F32)
            acc = d if acc is None else acc + d

        def finish(total):
            if add is not None:
                total = total + refs[2 * n_pairs][0]
            out_ref[0] = total.astype(out_dtype)

        if n_red == 1:
            finish(acc)
        else:
            acc_ref = refs[-1]
            r = pl.program_id(3)

            @pl.when(r == 0)
            def _():
                acc_ref[...] = acc

            @pl.when(r > 0)
            def _():
                acc_ref[...] += acc

            @pl.when(r == n_red - 1)
            def _():
                finish(acc_ref[...])

    return pl.pallas_call(
        body,
        name=name,
        grid=grid,
        in_specs=in_specs,
        out_specs=pl.BlockSpec((1, tm, tn), lambda s, i, j, r: (s, i, j)),
        out_shape=jax.ShapeDtypeStruct((n_so, n_tok, n_out), out_dtype),
        scratch_shapes=[pltpu.VMEM((tm, tn), F32)] if n_red > 1 else [],
        compiler_params=_params(blk, 3 * _nbytes((tm, tn), F32)),
    )(*_hbm(*args))


def _mm_tn(a, b, *, name, tt=512):
    n_tok, k_dim = a.shape[1], a.shape[2]
    n_out = b.shape[2]
    tt = min(tt, n_tok)
    tk = k_dim if k_dim <= 1536 else 1024
    tn = n_out if n_out <= 1536 else 1024
    assert n_tok % tt == 0 and k_dim % tk == 0 and n_out % tn == 0
    n_so = max(a.shape[0], b.shape[0])
    sa = (lambda s: s) if a.shape[0] > 1 else (lambda s: 0)
    sb = (lambda s: s) if b.shape[0] > 1 else (lambda s: 0)
    grid = (n_so, k_dim // tk, n_out // tn, n_tok // tt)

    def body(a_ref, b_ref, out_ref):
        d = lax.dot_general(a_ref[0].astype(MXU_DTYPE), b_ref[0].astype(MXU_DTYPE), _TN, preferred_element_type=F32)
        t = pl.program_id(3)

        @pl.when(t == 0)
        def _():
            out_ref[0] = d

        @pl.when(t > 0)
        def _():
            out_ref[0] += d

    blk = _nbytes((tt, tk), a.dtype) + _nbytes((tt, tn), b.dtype) + _nbytes((tk, tn), F32)
    return pl.pallas_call(
        body,
        name=name,
        grid=grid,
        in_specs=[
            pl.BlockSpec((1, tt, tk), lambda s, ki, nj, t: (sa(s), t, ki)),
            pl.BlockSpec((1, tt, tn), lambda s, ki, nj, t: (sb(s), t, nj)),
        ],
        out_specs=pl.BlockSpec((1, tk, tn), lambda s, ki, nj, t: (s, ki, nj)),
        out_shape=jax.ShapeDtypeStruct((n_so, k_dim, n_out), F32),
        compiler_params=_params(blk, 2 * _nbytes((tk, tn), F32) + _nbytes((tt, tk), F32)),
    )(*_hbm(a, b))


@functools.partial(jax.custom_vjp, nondiff_argnums=(2,))
def _mdot(a, b, dims):
    return lax.dot_general(a.astype(MXU_DTYPE), b.astype(MXU_DTYPE), dims, preferred_element_type=F32)


def _mdot_fwd(a, b, dims):
    return _mdot(a, b, dims), (a, b)


def _mdot_bwd(dims, res, g):
    a, b = res
    (ca, cb), (ba, bb) = dims
    nb = len(ba)
    assert tuple(ba) == tuple(range(nb)) and tuple(bb) == tuple(range(nb)) and len(ca) == 1 and a.ndim == nb + 2
    batch = (tuple(range(nb)), tuple(range(nb)))
    ra, rb = nb, nb + 1
    a_free = (set(range(nb, nb + 2)) - set(ca)).pop()
    b_free = (set(range(nb, nb + 2)) - set(cb)).pop()
    if a_free < ca[0]:
        da = _mdot(g, b, (((rb,), (b_free,)), batch))
    else:
        da = _mdot(b, g, (((b_free,), (rb,)), batch))
    if b_free > cb[0]:
        db = _mdot(a, g, (((a_free,), (ra,)), batch))
    else:
        db = _mdot(g, a, (((ra,), (a_free,)), batch))
    return da.astype(a.dtype), db.astype(b.dtype)


_mdot.defvjp(_mdot_fwd, _mdot_bwd)


def _rms(x, gain):
    return x * lax.rsqrt(jnp.mean(x * x, axis=-1, keepdims=True) + EPS) * gain


def _pre_fn(coef, x_in, f, gate, gain, shift, scale):
    x_new = x_in if f is None else x_in + coef * gate * f
    return x_new, _rms(x_new, gain) * (1.0 + scale) + shift


def _row_spec(ts):
    return pl.BlockSpec((1, ts, D_MODEL), lambda b, j: (b, j, 0))


_BATCH_VEC = pl.BlockSpec((1, 1, D_MODEL), lambda b, j: (b, 0, 0))
_ONE_VEC = pl.BlockSpec((1, D_MODEL), lambda b, j: (0, 0))


def _pre(x_in, f, gate, gain, shift, scale, coef, *, name, ts=512):
    n_b, n_s, _ = x_in.shape
    ts = min(ts, n_s)
    has_res = f is not None

    def body(*refs):
        if has_res:
            x_ref, f_ref, gate_ref, gain_ref, sh_ref, sc_ref, xn_ref, a_ref = refs
            x_new, a = _pre_fn(coef, x_ref[0], f_ref[0], gate_ref[0], gain_ref[...], sh_ref[0], sc_ref[0])
            xn_ref[0] = x_new
        else:
            x_ref, gain_ref, sh_ref, sc_ref, a_ref = refs
            _, a = _pre_fn(coef, x_ref[0], None, None, gain_ref[...], sh_ref[0], sc_ref[0])
        a_ref[0] = a.astype(a_ref.dtype)

    row = _row_spec(ts)
    if has_res:
        args = (x_in, f, gate, gain, shift, scale)
        in_specs = [row, row, _BATCH_VEC, _ONE_VEC, _BATCH_VEC, _BATCH_VEC]
        out_specs = (row, row)
        out_shape = (jax.ShapeDtypeStruct(x_in.shape, F32), jax.ShapeDtypeStruct(x_in.shape, MXU_DTYPE))
    else:
        args = (x_in, gain, shift, scale)
        in_specs = [row, _ONE_VEC, _BATCH_VEC, _BATCH_VEC]
        out_specs = row
        out_shape = jax.ShapeDtypeStruct(x_in.shape, MXU_DTYPE)
    return pl.pallas_call(
        body, name=name, grid=(n_b, n_s // ts), in_specs=in_specs, out_specs=out_specs, out_shape=out_shape,
        compiler_params=_params(5 * _nbytes((ts, D_MODEL), F32), 4 * _nbytes((ts, D_MODEL), F32)),
    )(*_hbm(*args))


def _accumulate(ref, value, first):
    @pl.when(first)
    def _():
        ref[...] = value

    @pl.when(jnp.logical_not(first))
    def _():
        ref[...] += value


def _pre_bwd(x_in, f, gate, gain, shift, scale, coef, da, dx_up, *, name, ts=512):
    n_b, n_s, _ = x_in.shape
    ts = min(ts, n_s)
    has_res = f is not None
    has_up = dx_up is not None

    def body(*refs):
        refs = list(refs)
        x_ref = refs.pop(0)
        f_ref, gate_ref = (refs.pop(0), refs.pop(0)) if has_res else (None, None)
        gain_ref, sh_ref, sc_ref, da_ref = refs.pop(0), refs.pop(0), refs.pop(0), refs.pop(0)
        up_ref = refs.pop(0) if has_up else None
        dx_ref = refs.pop(0)
        df_ref, dgate_ref = (refs.pop(0), refs.pop(0)) if has_res else (None, None)
        dgain_ref, dsh_ref, dsc_ref = refs
        b, j = pl.program_id(0), pl.program_id(1)
        da_v = da_ref[0].astype(F32)
        up_v = up_ref[0] if has_up else jnp.zeros((ts, D_MODEL), F32)
        if has_res:
            fn = functools.partial(_pre_fn, coef)
            _, pull = jax.vjp(fn, x_ref[0], f_ref[0], gate_ref[0], gain_ref[...], sh_ref[0], sc_ref[0])
            dx, df, dgate, dgain, dsh, dsc = pull((up_v, da_v))
            df_ref[0] = df.astype(df_ref.dtype)
            _accumulate(dgate_ref, dgate[None], j == 0)
        else:
            fn = lambda x, g, sh, sc: _pre_fn(coef, x, None, None, g, sh, sc)
            _, pull = jax.vjp(fn, x_ref[0], gain_ref[...], sh_ref[0], sc_ref[0])
            dx, dgain, dsh, dsc = pull((up_v, da_v))
        dx_ref[0] = dx
        _accumulate(dgain_ref, dgain, jnp.logical_and(b == 0, j == 0))
        _accumulate(dsh_ref, dsh[None], j == 0)
        _accumulate(dsc_ref, dsc[None], j == 0)

    row = _row_spec(ts)
    args, in_specs = [x_in], [row]
    if has_res:
        args += [f, gate]
        in_specs += [row, _BATCH_VEC]
    args += [gain, shift, scale, da]
    in_specs += [_ONE_VEC, _BATCH_VEC, _BATCH_VEC, row]
    if has_up:
        args.append(dx_up)
        in_specs.append(row)
    vec = jax.ShapeDtypeStruct((n_b, 1, D_MODEL), F32)
    out_shape, out_specs = [jax.ShapeDtypeStruct(x_in.shape, F32)], [row]
    if has_res:
        out_shape += [jax.ShapeDtypeStruct(x_in.shape, MXU_DTYPE), vec]
        out_specs += [row, _BATCH_VEC]
    out_shape += [jax.ShapeDtypeStruct((1, D_MODEL), F32), vec, vec]
    out_specs += [_ONE_VEC, _BATCH_VEC, _BATCH_VEC]
    return pl.pallas_call(
        body, name=name, grid=(n_b, n_s // ts), in_specs=in_specs, out_specs=tuple(out_specs), out_shape=tuple(out_shape),
        compiler_params=_params(6 * _nbytes((ts, D_MODEL), F32), 8 * _nbytes((ts, D_MODEL), F32)),
    )(*_hbm(*args))


def _final_fn(x_in, f, gate, gain, target):
    x_new = x_in + 0.5 * gate * f
    err = jnp.square(_rms(x_new, gain) - target)
    return 0.5 * jnp.sum(jnp.mean(err, axis=-1))


def _final(x_in, f, gate, gain, target, *, name, ts=512):
    n_b, n_s, _ = x_in.shape
    ts = min(ts, n_s)

    def body(x_ref, f_ref, gate_ref, gain_ref, t_ref, loss_ref, dx_ref, df_ref, dgate_ref, dgain_ref):
        b, j = pl.program_id(0), pl.program_id(1)
        loss, (dx, df, dgate, dgain) = jax.value_and_grad(_final_fn, argnums=(0, 1, 2, 3))(
            x_ref[0], f_ref[0], gate_ref[0], gain_ref[...], t_ref[0])
        first = jnp.logical_and(b == 0, j == 0)
        _accumulate(loss_ref, jnp.reshape(loss, (1, 1)), first)
        dx_ref[0] = dx
        df_ref[0] = df.astype(df_ref.dtype)
        _accumulate(dgate_ref, dgate[None], j == 0)
        _accumulate(dgain_ref, dgain, first)

    row = _row_spec(ts)
    return pl.pallas_call(
        body, name=name, grid=(n_b, n_s // ts),
        in_specs=[row, row, _BATCH_VEC, _ONE_VEC, row],
        out_specs=(pl.BlockSpec((1, 1), lambda b, j: (0, 0)), row, row, _BATCH_VEC, _ONE_VEC),
        out_shape=(jax.ShapeDtypeStruct((1, 1), F32), jax.ShapeDtypeStruct(x_in.shape, F32),
                   jax.ShapeDtypeStruct(x_in.shape, MXU_DTYPE), jax.ShapeDtypeStruct((n_b, 1, D_MODEL), F32),
                   jax.ShapeDtypeStruct((1, D_MODEL), F32)),
        compiler_params=_params(5 * _nbytes((ts, D_MODEL), F32), 8 * _nbytes((ts, D_MODEL), F32)),
    )(*_hbm(x_in, f, gate, gain, target))


def _ffn_up(a, w1s, w3s, *, name, tm=512):
    n_tok = a.shape[0]
    tm = min(tm, n_tok)

    def body(a_ref, w1_ref, w3_ref, h1_ref, h3_ref, g_ref):
        av = a_ref[...].astype(MXU_DTYPE)
        h1 = lax.dot_general(av, w1_ref[0].astype(MXU_DTYPE), _NN, preferred_element_type=F32)
        h3 = lax.dot_general(av, w3_ref[0].astype(MXU_DTYPE), _NN, preferred_element_type=F32)
        h1_ref[0] = h1.astype(h1_ref.dtype)
        h3_ref[0] = h3.astype(h3_ref.dtype)
        g_ref[0] = (jax.nn.silu(h1) * h3).astype(g_ref.dtype)

    w_spec = pl.BlockSpec((1, D_MODEL, FF_SHARD), lambda s, i: (s, 0, 0))
    h_spec = pl.BlockSpec((1, tm, FF_SHARD), lambda s, i: (s, i, 0))
    h_shape = jax.ShapeDtypeStruct((N_SHARD, n_tok, FF_SHARD), MXU_DTYPE)
    blk = _nbytes((tm, D_MODEL), a.dtype) + 2 * _nbytes((D_MODEL, FF_SHARD), w1s.dtype) + 3 * _nbytes((tm, FF_SHARD), MXU_DTYPE)
    return pl.pallas_call(
        body, name=name, grid=(N_SHARD, n_tok // tm),
        in_specs=[pl.BlockSpec((tm, D_MODEL), lambda s, i: (i, 0)), w_spec, w_spec],
        out_specs=(h_spec, h_spec, h_spec), out_shape=(h_shape, h_shape, h_shape),
        compiler_params=_params(blk, 6 * _nbytes((tm, FF_SHARD), F32)),
    )(*_hbm(a, w1s, w3s))


def _ffn_down_bwd(df, w2s, h1, h3, *, name, tm=512):
    n_tok = df.shape[0]
    tm = min(tm, n_tok)

    def body(df_ref, w2_ref, h1_ref, h3_ref, dh1_ref, dh3_ref):
        dg = lax.dot_general(df_ref[...].astype(MXU_DTYPE), w2_ref[0].astype(MXU_DTYPE), _NT, preferred_element_type=F32)
        h1v = h1_ref[0].astype(F32)
        h3v = h3_ref[0].astype(F32)
        sig = jax.nn.sigmoid(h1v)
        dh3_ref[0] = (dg * (h1v * sig)).astype(dh3_ref.dtype)
        dh1_ref[0] = (dg * h3v * (sig * (1.0 + h1v * (1.0 - sig)))).astype(dh1_ref.dtype)

    h_spec = pl.BlockSpec((1, tm, FF_SHARD), lambda s, i: (s, i, 0))
    h_shape = jax.ShapeDtypeStruct((N_SHARD, n_tok, FF_SHARD), MXU_DTYPE)
    blk = _nbytes((tm, D_MODEL), df.dtype) + _nbytes((FF_SHARD, D_MODEL), w2s.dtype) + 4 * _nbytes((tm, FF_SHARD), MXU_DTYPE)
    return pl.pallas_call(
        body, name=name, grid=(N_SHARD, n_tok // tm),
        in_specs=[pl.BlockSpec((tm, D_MODEL), lambda s, i: (i, 0)),
                  pl.BlockSpec((1, FF_SHARD, D_MODEL), lambda s, i: (s, 0, 0)), h_spec, h_spec],
        out_specs=(h_spec, h_spec), out_shape=(h_shape, h_shape),
        compiler_params=_params(blk, 8 * _nbytes((tm, FF_SHARD), F32)),
    )(*_hbm(df, w2s, h1, h3))


def _ffn_fwd(a, w1s, w3s, w2s, tag):
    h1, h3, g = _ffn_up(a, w1s, w3s, name=f"{tag}_up")
    f = _mm_act([(g, w2s)], "nn", reduce_shards=True, name=f"{tag}_down")[0]
    return f, (h1, h3, g)


def _ffn_bwd(a, w1s, w3s, w2s, saved, df, tag):
    h1, h3, g = saved
    dh1, dh3 = _ffn_down_bwd(df, w2s, h1, h3, name=f"{tag}_down_bwd")
    da = _mm_act([(dh1, w1s), (dh3, w3s)], "nt", reduce_shards=True, name=f"{tag}_up_bwd")[0]
    a3 = a[None]
    dw1 = _mm_tn(a3, dh1, name=f"{tag}_dw1")
    dw3 = _mm_tn(a3, dh3, name=f"{tag}_dw3")
    dw2 = _mm_tn(g, df[None], name=f"{tag}_dw2")
    return da, dw1, dw3, dw2


CONV_LANES = 256


def _shift_down(x, d):
    if d == 0:
        return x
    row = lax.broadcasted_iota(jnp.int32, x.shape, 0)
    return jnp.where(row >= d, pltpu.roll(x, d, 0), 0.0)


def _shift_up(x, d):
    if d == 0:
        return x
    n = x.shape[0]
    row = lax.broadcasted_iota(jnp.int32, x.shape, 0)
    return jnp.where(row < n - d, pltpu.roll(x, n - d, 0), 0.0)


def _conv_pre(x, w):
    acc = None
    for j in range(CONV_WIDTH):
        term = w[j:j + 1, :] * _shift_down(x, CONV_WIDTH - 1 - j)
        acc = term if acc is None else acc + term
    return acc


def _conv_fwd(x, w, *, name):
    n_b, n_s, n_c = x.shape
    spec = pl.BlockSpec((1, n_s, CONV_LANES), lambda b, cj: (b, 0, cj))

    def body(x_ref, w_ref, o_ref):
        o_ref[0] = jax.nn.silu(_conv_pre(x_ref[0], w_ref[...]))

    return pl.pallas_call(
        body, name=name, grid=(n_b, n_c // CONV_LANES),
        in_specs=[spec, pl.BlockSpec((CONV_WIDTH, CONV_LANES), lambda b, cj: (0, cj))],
        out_specs=spec, out_shape=jax.ShapeDtypeStruct(x.shape, F32),
        compiler_params=_params(2 * _nbytes((n_s, CONV_LANES), F32), 6 * _nbytes((n_s, CONV_LANES), F32)),
    )(*_hbm(x, w))


def _conv_bwd(x, w, dout, *, name):
    n_b, n_s, n_c = x.shape
    spec = pl.BlockSpec((1, n_s, CONV_LANES), lambda cj, b: (b, 0, cj))
    w_spec = pl.BlockSpec((CONV_WIDTH, CONV_LANES), lambda cj, b: (0, cj))

    def body(x_ref, w_ref, do_ref, dx_ref, dw_ref):
        xv, wv = x_ref[0], w_ref[...]
        pre = _conv_pre(xv, wv)
        sig = jax.nn.sigmoid(pre)
        dpre = do_ref[0] * (sig * (1.0 + pre * (1.0 - sig)))
        dx = None
        first = pl.program_id(1) == 0
        for j in range(CONV_WIDTH):
            d = CONV_WIDTH - 1 - j
            term = wv[j:j + 1, :] * _shift_up(dpre, d)
            dx = term if dx is None else dx + term
            dwj = jnp.sum(dpre * _shift_down(xv, d), axis=0, keepdims=True)
            _accumulate(dw_ref.at[j:j + 1, :], dwj, first)
        dx_ref[0] = dx.astype(dx_ref.dtype)

    return pl.pallas_call(
        body, name=name, grid=(n_c // CONV_LANES, n_b),
        in_specs=[spec, w_spec, spec], out_specs=(spec, w_spec),
        out_shape=(jax.ShapeDtypeStruct(x.shape, MXU_DTYPE), jax.ShapeDtypeStruct((CONV_WIDTH, n_c), F32)),
        compiler_params=_params(3 * _nbytes((n_s, CONV_LANES), F32), 8 * _nbytes((n_s, CONV_LANES), F32)),
    )(*_hbm(x, w, dout))


_BNT = (((2,), (2,)), ((0,), (0,)))
_BNN = (((2,), (1,)), ((0,), (0,)))
_BTN = (((1,), (1,)), ((0,), (0,)))
DN_PREP_CHUNKS = 8
DN_SCAN_HEADS = 4
N_DOUBLINGS = 5


def _fdot(a, b, dims):
    return lax.dot_general(a, b, dims, precision=lax.Precision.HIGHEST, preferred_element_type=F32)


def _hdot(a, b, dims):
    return lax.dot_general(a, b, dims, precision=lax.Precision.HIGH, preferred_element_type=F32)


def _solve_by_doubling(a, rhs_u, rhs_w):
    row = lax.broadcasted_iota(jnp.int32, (CHUNK, CHUNK), 0)
    col = lax.broadcasted_iota(jnp.int32, (CHUNK, CHUNK), 1)
    inv = jnp.where(row == col, 1.0, 0.0) - a
    power = a
    for _ in range(N_DOUBLINGS):
        power = _hdot(power, power, _BNN)
        inv = inv + _hdot(inv, power, _BNN)
    return _hdot(inv, rhs_u, _BNN), _hdot(inv, rhs_w, _BNN), inv


@jax.custom_vjp
def _solve_saved(a, rhs_u, rhs_w, inv, u, w):
    return u, w


def _solve_saved_fwd(a, rhs_u, rhs_w, inv, u, w):
    return (u, w), (inv, u, w)


def _solve_saved_bwd(res, cts):
    inv, u, w = res
    gu = _hdot(inv, cts[0], _BTN)
    gw = _hdot(inv, cts[1], _BTN)
    da = -(_hdot(gu, u, _BNT) + _hdot(gw, w, _BNT))
    return da, gu, gw, jnp.zeros_like(inv), jnp.zeros_like(u), jnp.zeros_like(w)


_solve_saved.defvjp(_solve_saved_fwd, _solve_saved_bwd)


def _dn_prep_fn(solve, qc, kc, vc, bl, lac, lar, a_log, dt_bias):
    q = qc * lax.rsqrt(jnp.sum(qc * qc, axis=-1, keepdims=True) + EPS) * (DN_HEAD_DIM ** -0.5)
    k = kc * lax.rsqrt(jnp.sum(kc * kc, axis=-1, keepdims=True) + EPS)
    beta = jax.nn.sigmoid(bl)
    neg_a = -jnp.exp(a_log)
    lgc = neg_a * jax.nn.softplus(lac + dt_bias)
    lgr = neg_a * jax.nn.softplus(lar + dt_bias)
    row = lax.broadcasted_iota(jnp.int32, (CHUNK, CHUNK), 0)
    col = lax.broadcasted_iota(jnp.int32, (CHUNK, CHUNK), 1)
    causal, strict = row >= col, row > col
    g_c = jnp.sum(jnp.where(causal, lgr, 0.0), axis=-1, keepdims=True)
    g_r = jnp.sum(jnp.where(row <= col, lgc, 0.0), axis=-2, keepdims=True)
    decay = jnp.exp(jnp.where(causal, g_c - g_r, -jnp.inf))
    kb = k * beta
    a = jnp.where(strict, _mdot(kb, k, _BNT) * decay, 0.0)
    u, w, extra = solve(a, vc * beta, kb * jnp.exp(g_c))
    attn = _mdot(q, k, _BNT) * decay
    g_last = jnp.sum(lgc, axis=-2, keepdims=True)
    return q * jnp.exp(g_c), k * jnp.exp(g_last - g_c), u, w, attn, g_last, extra


def _dn_prep_specs(n_cb):
    tok = n_cb * CHUNK
    wide = pl.BlockSpec((1, 1, tok, DN_HEAD_DIM), lambda h, b, j: (b, h, j, 0))
    col = pl.BlockSpec((1, 1, tok, 1), lambda h, b, j: (b, h, j, 0))
    rowv = pl.BlockSpec((1, 1, n_cb, 1, CHUNK), lambda h, b, j: (b, h, j, 0, 0))
    one = pl.BlockSpec((1, 1, n_cb, 1, 1), lambda h, b, j: (b, h, j, 0, 0))
    head = pl.BlockSpec((1, 1, 1), lambda h, b, j: (h, 0, 0))
    return wide, col, rowv, one, head


def _dn_prep_load(n_cb, q_ref, k_ref, v_ref, bl_ref, lac_ref, lar_ref, al_ref, dt_ref):
    wide = lambda r: r[0, 0].reshape(n_cb, CHUNK, DN_HEAD_DIM)
    colv = lambda r: r[0, 0].reshape(n_cb, CHUNK, 1)
    return (wide(q_ref), wide(k_ref), wide(v_ref), colv(bl_ref), colv(lac_ref), lar_ref[0, 0], al_ref[...], dt_ref[...])


def _dn_prep(qh, kh, vh, bl, lac, lar, a_log, dt_bias, *, name):
    n_b, n_h, n_s, _ = qh.shape
    n_cb = min(DN_PREP_CHUNKS, n_s // CHUNK)
    tok = n_cb * CHUNK
    wide, col, rowv, one, head = _dn_prep_specs(n_cb)

    def body(*refs):
        outs = _dn_prep_fn(_solve_by_doubling, *_dn_prep_load(n_cb, *refs[:8]))
        for ref, val in zip(refs[8:13], outs[:5]):
            ref[0, 0] = val.reshape(tok, DN_HEAD_DIM)
        refs[13][0, 0] = outs[5]
        refs[14][0, 0] = outs[6].reshape(tok, DN_HEAD_DIM)

    big = jax.ShapeDtypeStruct(qh.shape, F32)
    return pl.pallas_call(
        body, name=name, grid=(n_h, n_b, n_s // tok),
        in_specs=[wide, wide, wide, col, col, rowv, head, head],
        out_specs=(wide, wide, wide, wide, wide, one, wide),
        out_shape=(big, big, big, big, big, jax.ShapeDtypeStruct((n_b, n_h, n_s // CHUNK, 1, 1), F32), big),
        compiler_params=_params(11 * _nbytes((tok, LANES), F32), 48 * _nbytes((tok, LANES), F32)),
    )(*_hbm(qh, kh, vh, bl, lac, lar, a_log, dt_bias))


def _dn_prep_bwd(qh, kh, vh, bl, lac, lar, a_log, dt_bias, inv, u, w, cts, *, name):
    n_b, n_h, n_s, _ = qh.shape
    n_cb = min(DN_PREP_CHUNKS, n_s // CHUNK)
    tok = n_cb * CHUNK
    wide, col, rowv, one, head = _dn_prep_specs(n_cb)

    def body(*refs):
        prim = _dn_prep_load(n_cb, *refs[:8])
        chunks = lambda r: r[0, 0].reshape(n_cb, CHUNK, DN_HEAD_DIM)
        inv_v, u_v, w_v = chunks(refs[8]), chunks(refs[9]), chunks(refs[10])
        ct = tuple(chunks(r) for r in refs[11:16]) + (refs[16][0, 0],)

        def fn(*args):
            solve = lambda a, ru, rw: _solve_saved(a, ru, rw, inv_v, u_v, w_v) + (None,)
            return _dn_prep_fn(solve, *args)[:6]

        _, pull = jax.vjp(fn, *prim)
        dq, dk, dv, dbl, dlac, dlar, dal, ddt = pull(ct)
        outs = refs[17:]
        for ref, val in zip(outs[:3], (dq, dk, dv)):
            ref[0, 0] = val.reshape(tok, DN_HEAD_DIM)
        outs[3][0, 0] = dbl.reshape(tok, 1)
        outs[4][0, 0] = dlac.reshape(tok, 1)
        outs[5][0, 0] = dlar
        first = jnp.logical_and(pl.program_id(1) == 0, pl.program_id(2) == 0)
        _accumulate(outs[6], dal, first)
        _accumulate(outs[7], ddt, first)

    big = jax.ShapeDtypeStruct(qh.shape, F32)
    return pl.pallas_call(
        body, name=name, grid=(n_h, n_b, n_s // tok),
        in_specs=[wide, wide, wide, col, col, rowv, head, head, wide, wide, wide, wide, wide, wide, wide, wide, one],
        out_specs=(wide, wide, wide, col, col, rowv, head, head),
        out_shape=(big, big, big, jax.ShapeDtypeStruct(bl.shape, F32), jax.ShapeDtypeStruct(lac.shape, F32),
                   jax.ShapeDtypeStruct(lar.shape, F32), jax.ShapeDtypeStruct(a_log.shape, F32),
                   jax.ShapeDtypeStruct(dt_bias.shape, F32)),
        compiler_params=_params(21 * _nbytes((tok, LANES), F32), 64 * _nbytes((tok, LANES), F32)),
    )(*_hbm(qh, kh, vh, bl, lac, lar, a_log, dt_bias, inv, u, w, *cts))


def _dn_step(state, q, k, u, w, a, gl):
    v_new = u - _mdot(w, state, _BNN)
    o = _mdot(q, state, _BNN) + _mdot(a, v_new, _BNN)
    return state * jnp.exp(gl) + _mdot(k, v_new, _BTN), o


def _dn_scan_specs(n_cb, n_blocks, reverse):
    tok = n_cb * CHUNK
    jj = (lambda j: n_blocks - 1 - j) if reverse else (lambda j: j)
    wide = pl.BlockSpec((1, DN_SCAN_HEADS, tok, DN_HEAD_DIM), lambda b, h, j: (b, h, jj(j), 0))
    one = pl.BlockSpec((1, DN_SCAN_HEADS, n_cb, 1, 1), lambda b, h, j: (b, h, jj(j), 0, 0))
    st = pl.BlockSpec((1, DN_SCAN_HEADS, n_cb, DN_HEAD_DIM, DN_HEAD_DIM), lambda b, h, j: (b, h, jj(j), 0, 0))
    return wide, one, st


def _dn_scan(qd, kd, u, w, attn, g_last, *, name):
    n_b, n_h, n_s, _ = qd.shape
    n_cb = min(DN_PREP_CHUNKS, n_s // CHUNK)
    n_blocks = n_s // (n_cb * CHUNK)
    wide, one, st = _dn_scan_specs(n_cb, n_blocks, False)

    def body(qd_ref, kd_ref, u_ref, w_ref, a_ref, gl_ref, o_ref, st_ref, state_ref):
        @pl.when(pl.program_id(2) == 0)
        def _():
            state_ref[...] = jnp.zeros(state_ref.shape, F32)

        def step(n, state):
            rows = pl.ds(pl.multiple_of(n * CHUNK, CHUNK), CHUNK)
            st_ref[0, :, n] = state
            state, o = _dn_step(state, qd_ref[0, :, rows, :], kd_ref[0, :, rows, :], u_ref[0, :, rows, :],
                                w_ref[0, :, rows, :], a_ref[0, :, rows, :], gl_ref[0, :, n])
            o_ref[0, :, rows, :] = o
            return state

        state_ref[...] = lax.fori_loop(0, n_cb, step, state_ref[...])

    return pl.pallas_call(
        body, name=name, grid=(n_b, n_h // DN_SCAN_HEADS, n_blocks),
        in_specs=[wide, wide, wide, wide, wide, one], out_specs=(wide, st),
        out_shape=(jax.ShapeDtypeStruct(qd.shape, F32),
                   jax.ShapeDtypeStruct((n_b, n_h, n_s // CHUNK, DN_HEAD_DIM, DN_HEAD_DIM), F32)),
        scratch_shapes=[pltpu.VMEM((DN_SCAN_HEADS, DN_HEAD_DIM, DN_HEAD_DIM), F32)],
        compiler_params=_params(8 * _nbytes((DN_SCAN_HEADS, n_cb * CHUNK, LANES), F32), 8 << 20),
    )(*_hbm(qd, kd, u, w, attn, g_last))


def _dn_scan_bwd(qd, kd, u, w, attn, g_last, states, do, *, name):
    n_b, n_h, n_s, _ = qd.shape
    n_cb = min(DN_PREP_CHUNKS, n_s // CHUNK)
    n_blocks = n_s // (n_cb * CHUNK)
    wide, one, st = _dn_scan_specs(n_cb, n_blocks, True)

    def body(qd_ref, kd_ref, u_ref, w_ref, a_ref, gl_ref, st_ref, do_ref,
             dq_ref, dk_ref, du_ref, dw_ref, da_ref, dgl_ref, dstate_ref):
        @pl.when(pl.program_id(2) == 0)
        def _():
            dstate_ref[...] = jnp.zeros(dstate_ref.shape, F32)

        def step(i, dstate):
            n = n_cb - 1 - i
            rows = pl.ds(pl.multiple_of(n * CHUNK, CHUNK), CHUNK)
            _, pull = jax.vjp(_dn_step, st_ref[0, :, n], qd_ref[0, :, rows, :], kd_ref[0, :, rows, :],
                              u_ref[0, :, rows, :], w_ref[0, :, rows, :], a_ref[0, :, rows, :], gl_ref[0, :, n])
            dstate, dq, dk, du, dw, da, dgl = pull((dstate, do_ref[0, :, rows, :]))
            dq_ref[0, :, rows, :] = dq
            dk_ref[0, :, rows, :] = dk
            du_ref[0, :, rows, :] = du
            dw_ref[0, :, rows, :] = dw
            da_ref[0, :, rows, :] = da
            dgl_ref[0, :, n] = dgl
            return dstate

        dstate_ref[...] = lax.fori_loop(0, n_cb, step, dstate_ref[...])

    big = jax.ShapeDtypeStruct(qd.shape, F32)
    return pl.pallas_call(
        body, name=name, grid=(n_b, n_h // DN_SCAN_HEADS, n_blocks),
        in_specs=[wide, wide, wide, wide, wide, one, st, wide],
        out_specs=(wide, wide, wide, wide, wide, one),
        out_shape=(big, big, big, big, big, jax.ShapeDtypeStruct(g_last.shape, F32)),
        scratch_shapes=[pltpu.VMEM((DN_SCAN_HEADS, DN_HEAD_DIM, DN_HEAD_DIM), F32)],
        compiler_params=_params(13 * _nbytes((DN_SCAN_HEADS, n_cb * CHUNK, LANES), F32), 8 << 20),
    )(*_hbm(qd, kd, u, w, attn, g_last, states, do))


def _dn_post_fn(o, z, gain):
    return o * lax.rsqrt(jnp.mean(o * o, axis=-1, keepdims=True) + EPS) * gain * jax.nn.silu(z)


_HEAD_ROWS = lambda n_s: pl.BlockSpec((1, 1, n_s, DN_HEAD_DIM), lambda b, h: (b, h, 0, 0))
_HEAD_GAIN = pl.BlockSpec((1, DN_HEAD_DIM), lambda b, h: (0, 0))


def _dn_post(o, z, gain, *, name):
    n_b, n_h, n_s, _ = o.shape

    def body(o_ref, z_ref, g_ref, out_ref):
        out_ref[0, 0] = _dn_post_fn(o_ref[0, 0], z_ref[0, 0], g_ref[...]).astype(out_ref.dtype)

    rows = _HEAD_ROWS(n_s)
    return pl.pallas_call(
        body, name=name, grid=(n_b, n_h), in_specs=[rows, rows, _HEAD_GAIN], out_specs=rows,
        out_shape=jax.ShapeDtypeStruct(o.shape, MXU_DTYPE),
        compiler_params=_params(3 * _nbytes((n_s, LANES), F32), 6 * _nbytes((n_s, LANES), F32)),
    )(*_hbm(o, z, gain))


def _dn_post_bwd(o, z, gain, dout, *, name):
    n_b, n_h, n_s, _ = o.shape

    def body(o_ref, z_ref, g_ref, dout_ref, do_ref, dz_ref, dg_ref):
        _, pull = jax.vjp(_dn_post_fn, o_ref[0, 0], z_ref[0, 0], g_ref[...])
        do, dz, dg = pull(dout_ref[0, 0].astype(F32))
        do_ref[0, 0] = do
        dz_ref[0, 0] = dz.astype(dz_ref.dtype)
        _accumulate(dg_ref, dg, jnp.logical_and(pl.program_id(0) == 0, pl.program_id(1) == 0))

    rows = _HEAD_ROWS(n_s)
    return pl.pallas_call(
        body, name=name, grid=(n_b, n_h), in_specs=[rows, rows, _HEAD_GAIN, rows],
        out_specs=(rows, rows, _HEAD_GAIN),
        out_shape=(jax.ShapeDtypeStruct(o.shape, F32), jax.ShapeDtypeStruct(o.shape, MXU_DTYPE),
                   jax.ShapeDtypeStruct((1, DN_HEAD_DIM), F32)),
        compiler_params=_params(5 * _nbytes((n_s, LANES), F32), 10 * _nbytes((n_s, LANES), F32)),
    )(*_hbm(o, z, gain, dout))


S5_SCAN_LANES = 256
TILE_ROWS = SUBLANES


def _s5_prep_fn(lam_re, lam_im, log_step, bt_re, bt_im, c_im):
    lr = jnp.minimum(lam_re, -1e-4)
    step = jnp.exp(log_step)
    mag = jnp.exp(lr * step)
    ang = lam_im * step
    lb_re = mag * jnp.cos(ang)
    lb_im = mag * jnp.sin(ang)
    den = lr * lr + lam_im * lam_im
    coef_re = ((lb_re - 1.0) * lr + lb_im * lam_im) / den
    coef_im = (lb_im * lr - (lb_re - 1.0) * lam_im) / den
    return (lb_re, lb_im, coef_re * bt_re - coef_im * bt_im, coef_re * bt_im + coef_im * bt_re, -c_im)


def _s5_prep(lam_re, lam_im, log_step, bt_re, bt_im, c_im, *, name):
    def body(*refs):
        outs = _s5_prep_fn(*(r[...] for r in refs[:6]))
        for ref, val in zip(refs[6:], outs):
            ref[...] = val

    vec = jax.ShapeDtypeStruct(lam_re.shape, F32)
    mat = jax.ShapeDtypeStruct(bt_re.shape, F32)
    return pl.pallas_call(body, name=name, out_shape=(vec, vec, mat, mat, mat))(lam_re, lam_im, log_step, bt_re, bt_im, c_im)


def _s5_prep_bwd(lam_re, lam_im, log_step, bt_re, bt_im, c_im, cts, *, name):
    def body(*refs):
        _, pull = jax.vjp(_s5_prep_fn, *(r[...] for r in refs[:6]))
        grads = pull(tuple(r[...] for r in refs[6:11]))
        for ref, val in zip(refs[11:], grads):
            ref[...] = val

    shapes = tuple(jax.ShapeDtypeStruct(a.shape, F32) for a in (lam_re, lam_im, log_step, bt_re, bt_im, c_im))
    return pl.pallas_call(body, name=name, out_shape=shapes)(lam_re, lam_im, log_step, bt_re, bt_im, c_im, *cts)


def _cmul(ar, ai, br, bi):
    return ar * br - ai * bi, ar * bi + ai * br


def _s5_powers(lr, li):
    pows = [(lr, li)]
    for _ in range(TILE_ROWS - 1):
        pows.append(_cmul(pows[-1][0], pows[-1][1], lr, li))
    return pows


def _s5_carry_table(pows, n_lanes, reverse):
    row = lax.broadcasted_iota(jnp.int32, (TILE_ROWS, n_lanes), 0)
    t_re = jnp.zeros((TILE_ROWS, n_lanes), F32)
    t_im = jnp.zeros((TILE_ROWS, n_lanes), F32)
    for r in range(TILE_ROWS):
        p_re, p_im = pows[TILE_ROWS - 1 - r] if reverse else pows[r]
        t_re = jnp.where(row == r, p_re, t_re)
        t_im = jnp.where(row == r, p_im, t_im)
    return t_re, t_im


def _s5_tile(y_re, y_im, pows, reverse):
    d = 1
    while d < TILE_ROWS:
        p_re, p_im = pows[d - 1]
        if reverse:
            s_re, s_im = _shift_up(y_re, d), _shift_up(y_im, d)
        else:
            s_re, s_im = _shift_down(y_re, d), _shift_down(y_im, d)
        m_re, m_im = _cmul(p_re, p_im, s_re, s_im)
        y_re, y_im = y_re + m_re, y_im + m_im
        d *= 2
    return y_re, y_im


S5_BLOCKS = N_SHARD
S5_BLOCK_CH = S5_WIDTH // S5_BLOCKS
S5_BLOCK_LANES = S5_LANES // S5_BLOCKS
SCAN_PER_BLOCK = S5_BLOCK_LANES // S5_SCAN_LANES


def _s5_scan_specs(n_s, order):
    L = S5_SCAN_LANES

    def cat_spec(part):
        return pl.BlockSpec((1, 1, n_s, L), lambda *g: (order(*g)[1] // SCAN_PER_BLOCK, order(*g)[0], 0,
                                                        part * SCAN_PER_BLOCK + order(*g)[1] % SCAN_PER_BLOCK))

    one = pl.BlockSpec((1, 1, n_s, L), lambda *g: (order(*g)[1] // SCAN_PER_BLOCK, order(*g)[0], 0,
                                                   order(*g)[1] % SCAN_PER_BLOCK))
    lam = pl.BlockSpec((1, L), lambda *g: (0, order(*g)[1]))
    return cat_spec, one, lam


def _s5_scan(bu, lb_re, lb_im, *, name):
    n_blk, n_b, n_s, _ = bu.shape
    n_lb = S5_LANES // S5_SCAN_LANES
    n_tiles = n_s // TILE_ROWS
    L = S5_SCAN_LANES

    def body(re_ref, im_ref, lr_ref, li_ref, xr_ref, xi_ref):
        pows = _s5_powers(lr_ref[...], li_ref[...])
        t_re, t_im = _s5_carry_table(pows, L, False)

        def step(i, carry):
            rows = pl.ds(pl.multiple_of(i * TILE_ROWS, TILE_ROWS), TILE_ROWS)
            y_re, y_im = _s5_tile(re_ref[0, 0, rows, :], im_ref[0, 0, rows, :], pows, False)
            c_re, c_im = _cmul(t_re, t_im, carry[0], carry[1])
            y_re, y_im = y_re + c_re, y_im + c_im
            xr_ref[0, 0, rows, :] = y_re
            xi_ref[0, 0, rows, :] = y_im
            return y_re[TILE_ROWS - 1:, :], y_im[TILE_ROWS - 1:, :]

        zero = jnp.zeros((1, L), F32)
        lax.fori_loop(0, n_tiles, step, (zero, zero))

    cat_spec, one, lam = _s5_scan_specs(n_s, lambda b, j: (b, j))
    x_shape = jax.ShapeDtypeStruct((n_blk, n_b, n_s, S5_BLOCK_LANES), F32)
    return pl.pallas_call(
        body, name=name, grid=(n_b, n_lb),
        in_specs=[cat_spec(0), cat_spec(1), lam, lam],
        out_specs=(one, one), out_shape=(x_shape, x_shape),
        compiler_params=_params(4 * _nbytes((n_s, L), F32), 4 << 20),
    )(*_hbm(bu, bu, lb_re, lb_im))


def _s5_scan_bwd(dx, x_re, x_im, lb_re, lb_im, *, name):
    n_blk, n_b, n_s, _ = dx.shape
    n_lb = S5_LANES // S5_SCAN_LANES
    n_tiles = n_s // TILE_ROWS
    L = S5_SCAN_LANES

    def body(dr_ref, di_ref, xr_ref, xi_ref, lr_ref, li_ref, ar_ref, ai_ref, dlr_ref, dli_ref):
        pows = _s5_powers(lr_ref[...], -li_ref[...])
        t_re, t_im = _s5_carry_table(pows, L, True)
        row = lax.broadcasted_iota(jnp.int32, (TILE_ROWS, L), 0)

        def step(k, carry):
            c_re, c_im, s_re, s_im = carry
            i = n_tiles - 1 - k
            rows = pl.ds(pl.multiple_of(i * TILE_ROWS, TILE_ROWS), TILE_ROWS)
            a_re, a_im = _s5_tile(dr_ref[0, 0, rows, :], di_ref[0, 0, rows, :], pows, True)
            m_re, m_im = _cmul(t_re, t_im, c_re, c_im)
            a_re, a_im = a_re + m_re, a_im + m_im
            ar_ref[0, 0, rows, :] = a_re.astype(ar_ref.dtype)
            ai_ref[0, 0, rows, :] = a_im.astype(ai_ref.dtype)
            prev = pl.ds(pl.multiple_of(jnp.maximum(i - 1, 0) * TILE_ROWS, TILE_ROWS), TILE_ROWS)
            keep = jnp.where(i > 0, 1.0, 0.0)
            last_re = xr_ref[0, 0, prev, :][TILE_ROWS - 1:, :] * keep
            last_im = xi_ref[0, 0, prev, :][TILE_ROWS - 1:, :] * keep
            xp_re = jnp.where(row == 0, last_re, _shift_down(xr_ref[0, 0, rows, :], 1))
            xp_im = jnp.where(row == 0, last_im, _shift_down(xi_ref[0, 0, rows, :], 1))
            s_re = s_re + a_re * xp_re + a_im * xp_im
            s_im = s_im + a_im * xp_re - a_re * xp_im
            return a_re[:1, :], a_im[:1, :], s_re, s_im

        zero = jnp.zeros((1, L), F32)
        zt = jnp.zeros((TILE_ROWS, L), F32)
        _, _, s_re, s_im = lax.fori_loop(0, n_tiles, step, (zero, zero, zt, zt))
        first = pl.program_id(1) == 0
        _accumulate(dlr_ref, jnp.sum(s_re, axis=0, keepdims=True), first)
        _accumulate(dli_ref, jnp.sum(s_im, axis=0, keepdims=True), first)

    cat_spec, one, lam = _s5_scan_specs(n_s, lambda j, b: (b, j))
    a_shape = jax.ShapeDtypeStruct((n_blk, n_b, n_s, S5_BLOCK_LANES), MXU_DTYPE)
    lam_shape = jax.ShapeDtypeStruct((1, S5_LANES), F32)
    return pl.pallas_call(
        body, name=name, grid=(n_lb, n_b),
        in_specs=[cat_spec(0), cat_spec(1), one, one, lam, lam],
        out_specs=(one, one, lam, lam),
        out_shape=(a_shape, a_shape, lam_shape, lam_shape),
        compiler_params=_params(5 * _nbytes((n_s, L), F32), 4 << 20),
    )(*_hbm(dx, dx, x_re, x_im, lb_re, lb_im))


def _s5_out_fn(ymm, u, d_skip, w_glu, b_glu):
    y = jax.nn.gelu(ymm + d_skip * u)
    return y * jax.nn.sigmoid(_mdot(y, w_glu, _NN) + b_glu)


def _s5_out_specs(tm):
    rows = pl.BlockSpec((S5_BLOCKS, tm, S5_BLOCK_CH), lambda i: (0, i, 0))
    flat_rows = pl.BlockSpec((tm, S5_WIDTH), lambda i: (i, 0))
    vec = pl.BlockSpec((1, S5_WIDTH), lambda i: (0, 0))
    mat = pl.BlockSpec((S5_WIDTH, S5_WIDTH), lambda i: (0, 0))
    return rows, flat_rows, vec, mat


def _blocks_to_lanes(ref):
    return jnp.concatenate([ref[k] for k in range(S5_BLOCKS)], axis=-1)


def _lanes_to_blocks(ref, val):
    for k in range(S5_BLOCKS):
        ref[k] = val[:, k * S5_BLOCK_CH:(k + 1) * S5_BLOCK_CH].astype(ref.dtype)


def _s5_out(ymm, u, d_skip, w_glu, b_glu, *, name, tm=512):
    n_tok = ymm.shape[1]
    tm = min(tm, n_tok)
    rows, flat_rows, vec, mat = _s5_out_specs(tm)

    def body(y_ref, u_ref, d_ref, w_ref, b_ref, o_ref):
        out = _s5_out_fn(_blocks_to_lanes(y_ref), _blocks_to_lanes(u_ref), d_ref[...], w_ref[...], b_ref[...])
        o_ref[...] = out.astype(o_ref.dtype)

    return pl.pallas_call(
        body, name=name, grid=(n_tok // tm,), in_specs=[rows, rows, vec, mat, vec], out_specs=flat_rows,
        out_shape=jax.ShapeDtypeStruct((n_tok, S5_WIDTH), MXU_DTYPE),
        compiler_params=_params(4 * _nbytes((tm, S5_WIDTH), F32), 8 * _nbytes((tm, S5_WIDTH), F32)),
    )(*_hbm(ymm, u, d_skip, w_glu, b_glu))


def _s5_out_bwd(ymm, u, d_skip, w_glu, b_glu, dout, *, name, tm=512):
    n_tok = ymm.shape[1]
    tm = min(tm, n_tok)
    rows, flat_rows, vec, mat = _s5_out_specs(tm)

    def body(y_ref, u_ref, d_ref, w_ref, b_ref, do_ref, dy_ref, du_ref, dd_ref, dw_ref, db_ref):
        _, pull = jax.vjp(_s5_out_fn, _blocks_to_lanes(y_ref), _blocks_to_lanes(u_ref), d_ref[...],
                          w_ref[...].astype(F32), b_ref[...])
        dy, du, dd, dw, db = pull(do_ref[...])
        _lanes_to_blocks(dy_ref, dy)
        _lanes_to_blocks(du_ref, du)
        first = pl.program_id(0) == 0
        _accumulate(dd_ref, dd, first)
        _accumulate(dw_ref, dw, first)
        _accumulate(db_ref, db, first)

    return pl.pallas_call(
        body, name=name, grid=(n_tok // tm,), in_specs=[rows, rows, vec, mat, vec, flat_rows],
        out_specs=(rows, rows, vec, mat, vec),
        out_shape=(jax.ShapeDtypeStruct(ymm.shape, MXU_DTYPE), jax.ShapeDtypeStruct(ymm.shape, F32),
                   jax.ShapeDtypeStruct((1, S5_WIDTH), F32), jax.ShapeDtypeStruct((S5_WIDTH, S5_WIDTH), F32),
                   jax.ShapeDtypeStruct((1, S5_WIDTH), F32)),
        compiler_params=_params(6 * _nbytes((tm, S5_WIDTH), F32), 12 * _nbytes((tm, S5_WIDTH), F32)),
    )(*_hbm(ymm, u, d_skip, w_glu, b_glu, dout))


def _merge_fn(ga, gb, ya, yb):
    return jax.nn.sigmoid(ga) * ya + jax.nn.sigmoid(gb) * yb


def _merge(gab, ya, yb, *, name, tm=512):
    n_tok = ya.shape[0]
    tm = min(tm, n_tok)
    rows = pl.BlockSpec((tm, D_MODEL), lambda i: (i, 0))

    def body(ga_ref, gb_ref, ya_ref, yb_ref, o_ref):
        o_ref[...] = _merge_fn(ga_ref[...], gb_ref[...], ya_ref[...], yb_ref[...]).astype(o_ref.dtype)

    return pl.pallas_call(
        body, name=name, grid=(n_tok // tm,),
        in_specs=[rows, pl.BlockSpec((tm, D_MODEL), lambda i: (i, 1)), rows, rows], out_specs=rows,
        out_shape=jax.ShapeDtypeStruct(ya.shape, MXU_DTYPE),
        compiler_params=_params(5 * _nbytes((tm, D_MODEL), F32), 4 * _nbytes((tm, D_MODEL), F32)),
    )(*_hbm(gab, gab, ya, yb))


def _merge_bwd(gab, ya, yb, dout, *, name, tm=512):
    n_tok = ya.shape[0]
    tm = min(tm, n_tok)
    rows = pl.BlockSpec((tm, D_MODEL), lambda i: (i, 0))

    def body(ga_ref, gb_ref, ya_ref, yb_ref, do_ref, *out_refs):
        _, pull = jax.vjp(_merge_fn, ga_ref[...], gb_ref[...], ya_ref[...], yb_ref[...])
        for ref, val in zip(out_refs, pull(do_ref[...])):
            ref[...] = val.astype(ref.dtype)

    shape = jax.ShapeDtypeStruct(ya.shape, MXU_DTYPE)
    return pl.pallas_call(
        body, name=name, grid=(n_tok // tm,),
        in_specs=[rows, pl.BlockSpec((tm, D_MODEL), lambda i: (i, 1)), rows, rows, rows],
        out_specs=(rows, rows, rows, rows), out_shape=(shape, shape, shape, shape),
        compiler_params=_params(7 * _nbytes((tm, D_MODEL), F32), 6 * _nbytes((tm, D_MODEL), F32)),
    )(*_hbm(gab, gab, ya, yb, dout))


ADA_SHARD = N_MOD * D_MODEL // N_SHARD


def _ada_fwd(c_pad, w_s, b_s, *, name):
    n_r = c_pad.shape[0]

    def body(c_ref, w_ref, b_ref, o_ref):
        sc = jax.nn.silu(c_ref[...]).astype(MXU_DTYPE)
        o_ref[0] = lax.dot_general(sc, w_ref[0].astype(MXU_DTYPE), _NN, preferred_element_type=F32) + b_ref[0]

    return pl.pallas_call(
        body, name=name, grid=(N_SHARD,),
        in_specs=[pl.BlockSpec((n_r, D_MODEL), lambda s: (0, 0)),
                  pl.BlockSpec((1, D_MODEL, ADA_SHARD), lambda s: (s, 0, 0)),
                  pl.BlockSpec((1, 1, ADA_SHARD), lambda s: (s, 0, 0))],
        out_specs=pl.BlockSpec((1, n_r, ADA_SHARD), lambda s: (s, 0, 0)),
        out_shape=jax.ShapeDtypeStruct((N_SHARD, n_r, ADA_SHARD), F32),
        compiler_params=_params(_nbytes((D_MODEL, ADA_SHARD), w_s.dtype), 1 << 20),
    )(*_hbm(c_pad, w_s, b_s))


def _ada_bwd(c_pad, dmod_s, *, name):
    n_r = c_pad.shape[0]

    def body(c_ref, d_ref, dw_ref, db_ref):
        sc = jax.nn.silu(c_ref[...])
        dm = d_ref[0]
        dw_ref[0] = _fdot(sc, dm, _TN)
        db_ref[0] = jnp.sum(dm, axis=0, keepdims=True)

    return pl.pallas_call(
        body, name=name, grid=(N_SHARD,),
        in_specs=[pl.BlockSpec((n_r, D_MODEL), lambda s: (0, 0)), pl.BlockSpec((1, n_r, ADA_SHARD), lambda s: (s, 0, 0))],
        out_specs=(pl.BlockSpec((1, D_MODEL, ADA_SHARD), lambda s: (s, 0, 0)),
                   pl.BlockSpec((1, 1, ADA_SHARD), lambda s: (s, 0, 0))),
        out_shape=(jax.ShapeDtypeStruct((N_SHARD, D_MODEL, ADA_SHARD), F32),
                   jax.ShapeDtypeStruct((N_SHARD, 1, ADA_SHARD), F32)),
        compiler_params=_params(_nbytes((D_MODEL, ADA_SHARD), F32), 2 * _nbytes((D_MODEL, ADA_SHARD), F32)),
    )(*_hbm(c_pad, dmod_s))


def _heads(t, n_b, n_s):
    return t.reshape(n_b, n_s, DN_HEADS, DN_HEAD_DIM).transpose(0, 2, 1, 3)


def _unheads(t):
    n_b, _, n_s, _ = t.shape
    return t.transpose(0, 2, 1, 3).reshape(n_b, n_s, DN_WIDTH)


def _block_diag(blocks):
    n_per = S5_GROUPS // S5_BLOCKS
    _, n_r, n_c = blocks.shape
    b4 = blocks.reshape(S5_BLOCKS, n_per, n_r, n_c)
    eye = jnp.eye(n_per, dtype=blocks.dtype)
    return (b4[:, :, :, None, :] * eye[None, :, None, :, None]).reshape(S5_BLOCKS, n_per * n_r, n_per * n_c)


def _diag_blocks(mat, n_r, n_c):
    n_per = S5_GROUPS // S5_BLOCKS
    return jnp.stack([mat[k, g * n_r:(g + 1) * n_r, g * n_c:(g + 1) * n_c]
                      for k in range(S5_BLOCKS) for g in range(n_per)])


def _local_step(x, c, target, wts):
    n_b, n_s, _ = x.shape
    n_tok = n_b * n_s
    flat = lambda t: t.reshape(n_tok, t.shape[-1])
    unflat = lambda t: t.reshape(n_b, n_s, t.shape[-1])
    n_chunks = n_s // CHUNK

    c_pad = jnp.zeros((SUBLANES, D_MODEL), F32).at[:n_b].set(c)
    mod_s = _ada_fwd(c_pad, wts["w_ada"], wts["b_ada"], name="ada_fwd")
    mod = mod_s.transpose(1, 0, 2).reshape(SUBLANES, N_MOD * D_MODEL)[:n_b]
    sh1, sc1, gt1, sh2, sc2, gt2, sh3, sc3, gt3 = [m[:, None, :] for m in jnp.split(mod, N_MOD, axis=-1)]

    a1 = _pre(x, None, None, wts["g_ffn1"], sh1, sc1, 0.0, name="pre1")
    f1, ffn1_saved = _ffn_fwd(flat(a1), wts["w1_ffn1"], wts["w3_ffn1"], wts["w2_ffn1"], "ffn1")
    x1, a2 = _pre(x, unflat(f1), gt1, wts["g_mix"], sh2, sc2, 0.5, name="pre2")
    u = flat(a2)[None]
    p_qkv = _mm_act([(u, wts["w_qkv"])], "nn", name="in_qkv")[0]
    p_z = _mm_act([(u, wts["w_z"])], "nn", name="in_z")[0]
    p_gab = _mm_act([(u, wts["w_gab"])], "nn", name="in_gab")[0]
    p_s5 = _mm_act([(u, wts["w_s5"])], "nn", out_sharded=True, name="in_s5")
    p_ba = _mm_act([(u, wts["w_ba"])], "nn", name="in_ba")[0]

    qkv_c = _conv_fwd(unflat(p_qkv), wts["conv_qkv"], name="conv_fwd")
    qh, kh, vh = [_heads(t, n_b, n_s) for t in jnp.split(qkv_c, 3, axis=-1)]
    zh = _heads(p_z, n_b, n_s)
    ba = p_ba.reshape(n_b, n_s, BA_PAD)
    bl = ba[:, :, :DN_HEADS].transpose(0, 2, 1)[..., None]
    lac = ba[:, :, DN_HEADS:2 * DN_HEADS].transpose(0, 2, 1)[..., None]
    lar = lac.reshape(n_b, DN_HEADS, n_chunks, 1, CHUNK)
    a_log, dt_bias = wts["a_log"], wts["dt_bias"]
    dn_in = (qh, kh, vh, bl, lac, lar, a_log, dt_bias)
    qd, kd, uc, wc, attn, g_last, dn_inv = _dn_prep(*dn_in, name="dn_prep")
    o, states = _dn_scan(qd, kd, uc, wc, attn, g_last, name="dn_scan")
    og = _dn_post(o, zh, wts["g_onorm"], name="dn_post")
    og_t = _unheads(og).reshape(1, n_tok, DN_WIDTH)
    ya = _mm_act([(og_t, wts["w_proj_a"])], "nn", name="proj_a")[0]

    s5p_in = (wts["lam_re"], wts["lam_im"], wts["log_step"], wts["bt_re"], wts["bt_im"], wts["c_im"])
    lb_re, lb_im, bb_re, bb_im, c_neg = _s5_prep(*s5p_in, name="s5_prep")
    wb_re, wb_im = _block_diag(bb_re), _block_diag(bb_im)
    wc_re = _block_diag(wts["c_re"].transpose(0, 2, 1))
    wc_im = _block_diag(c_neg.transpose(0, 2, 1))
    lbr, lbi = lb_re.reshape(1, S5_LANES), lb_im.reshape(1, S5_LANES)
    bu = _mm_act([(p_s5, jnp.concatenate([wb_re, wb_im], axis=2))], "nn", out_sharded=True, name="s5_bu")
    x_re, x_im = _s5_scan(bu.reshape(S5_BLOCKS, n_b, n_s, 2 * S5_BLOCK_LANES), lbr, lbi, name="s5_scan")
    xr_t = x_re.reshape(S5_BLOCKS, n_tok, S5_BLOCK_LANES)
    xi_t = x_im.reshape(S5_BLOCKS, n_tok, S5_BLOCK_LANES)
    ymm = _mm_act([(xr_t, wc_re), (xi_t, wc_im)], "nn", out_sharded=True, name="s5_y")
    y2 = _s5_out(ymm, p_s5, wts["d_skip"], wts["w_glu"], wts["b_glu"], name="s5_out")
    yb = _mm_act([(y2[None], wts["w_proj_b"])], "nn", name="proj_b")[0]

    merged = _merge(p_gab, ya, yb, name="merge")
    m_out = _mm_act([(merged[None], wts["w_out"])], "nn", name="mix_out")[0]
    x2, a3 = _pre(x1, unflat(m_out), gt2, wts["g_ffn2"], sh3, sc3, 1.0, name="pre3")
    f3, ffn2_saved = _ffn_fwd(flat(a3), wts["w1_ffn2"], wts["w3_ffn2"], wts["w2_ffn2"], "ffn2")

    g = {}
    loss, dx2_res, df3, dgt3, g["g_final"] = _final(x2, unflat(f3), gt3, wts["g_final"], target, name="final")
    da3, g["w1_ffn2"], g["w3_ffn2"], g["w2_ffn2"] = _ffn_bwd(
        flat(a3), wts["w1_ffn2"], wts["w3_ffn2"], wts["w2_ffn2"], ffn2_saved, flat(df3), "ffn2")
    dx1_res, dm_out, dgt2, g["g_ffn2"], dsh3, dsc3 = _pre_bwd(
        x1, unflat(m_out), gt2, wts["g_ffn2"], sh3, sc3, 1.0, unflat(da3), dx2_res, name="pre3_bwd")
    dm_out = flat(dm_out)[None]
    dmerged = _mm_act([(dm_out, wts["w_out"])], "nt", name="mix_out_bwd")[0]
    g["w_out"] = _mm_tn(merged[None], dm_out, name="dw_out")[0]
    dga, dgb, dya, dyb = _merge_bwd(p_gab, ya, yb, dmerged, name="merge_bwd")

    dy2 = _mm_act([(dyb[None], wts["w_proj_b"])], "nt", name="proj_b_bwd")[0]
    g["w_proj_b"] = _mm_tn(y2[None], dyb[None], name="dw_proj_b")[0]
    dymm, du_skip, g["d_skip"], g["w_glu"], g["b_glu"] = _s5_out_bwd(
        ymm, p_s5, wts["d_skip"], wts["w_glu"], wts["b_glu"], dy2, name="s5_out_bwd")
    dxs = _mm_act([(dymm, jnp.concatenate([wc_re, wc_im], axis=1))], "nt", out_sharded=True, name="s5_y_bwd")
    dwc_re = _mm_tn(xr_t, dymm, name="dwc_re")
    dwc_im = _mm_tn(xi_t, dymm, name="dwc_im")
    a_re, a_im, dlb_re, dlb_im = _s5_scan_bwd(
        dxs.reshape(S5_BLOCKS, n_b, n_s, 2 * S5_BLOCK_LANES), x_re, x_im, lbr, lbi, name="s5_scan_bwd")
    ar_t = a_re.reshape(S5_BLOCKS, n_tok, S5_BLOCK_LANES)
    ai_t = a_im.reshape(S5_BLOCKS, n_tok, S5_BLOCK_LANES)
    dp_s5 = _mm_act([(ar_t, wb_re), (ai_t, wb_im)], "nt", out_sharded=True, add=du_skip, out_dtype=MXU_DTYPE, name="s5_bu_bwd")
    dwb_re = _mm_tn(p_s5, ar_t, name="dwb_re")
    dwb_im = _mm_tn(p_s5, ai_t, name="dwb_im")
    g["c_re"] = _diag_blocks(dwc_re, S5_STATE, S5_GROUP_CH).transpose(0, 2, 1)
    s5_cts = (dlb_re.reshape(lb_re.shape), dlb_im.reshape(lb_im.shape),
              _diag_blocks(dwb_re, S5_GROUP_CH, S5_STATE), _diag_blocks(dwb_im, S5_GROUP_CH, S5_STATE),
              _diag_blocks(dwc_im, S5_STATE, S5_GROUP_CH).transpose(0, 2, 1))
    g["lam_re"], g["lam_im"], g["log_step"], g["bt_re"], g["bt_im"], g["c_im"] = _s5_prep_bwd(
        *s5p_in, s5_cts, name="s5_prep_bwd")

    dog = _mm_act([(dya[None], wts["w_proj_a"])], "nt", name="proj_a_bwd")[0]
    g["w_proj_a"] = _mm_tn(og_t, dya[None], name="dw_proj_a")[0]
    do, dzh, g["g_onorm"] = _dn_post_bwd(o, zh, wts["g_onorm"], _heads(dog, n_b, n_s), name="dn_post_bwd")
    scan_cts = _dn_scan_bwd(qd, kd, uc, wc, attn, g_last, states, do, name="dn_scan_bwd")
    dqh, dkh, dvh, dbl, dlac, dlar, g["a_log"], g["dt_bias"] = _dn_prep_bwd(*dn_in, dn_inv, uc, wc, scan_cts, name="dn_prep_bwd")
    dqkv_c = jnp.concatenate([_unheads(t) for t in (dqh, dkh, dvh)], axis=-1)
    dqkv, g["conv_qkv"] = _conv_bwd(unflat(p_qkv), wts["conv_qkv"], dqkv_c, name="conv_bwd")
    dla = dlac[..., 0] + dlar.reshape(n_b, DN_HEADS, n_s)
    dba = jnp.concatenate([dbl[..., 0].transpose(0, 2, 1), dla.transpose(0, 2, 1),
                           jnp.zeros((n_b, n_s, BA_PAD - 2 * DN_HEADS), F32)], axis=-1).astype(MXU_DTYPE)
    dz = _unheads(dzh)

    dps = {"w_qkv": flat(dqkv)[None], "w_z": flat(dz)[None], "w_ga": dga[None], "w_gb": dgb[None], "w_ba": flat(dba)[None]}
    w_ga, w_gb = wts["w_gab"][:, :, :D_MODEL], wts["w_gab"][:, :, D_MODEL:]
    w_of = dict(wts, w_ga=w_ga, w_gb=w_gb)
    du_s5 = _mm_act([(dp_s5, wts["w_s5"])], "nt", reduce_shards=True, name="in_s5_bwd")
    du = _mm_act([(dps[k], w_of[k]) for k in dps], "nt", add=du_s5, name="in_bwd")[0]
    for k in dps:
        g[k] = _mm_tn(u, dps[k], name=f"d{k}")[0]
    g["w_s5"] = _cat_columns(_mm_tn(u, dp_s5, name="dw_s5"))
    dx0_res, df1, dgt1, g["g_mix"], dsh2, dsc2 = _pre_bwd(
        x, unflat(f1), gt1, wts["g_mix"], sh2, sc2, 0.5, unflat(du), dx1_res, name="pre2_bwd")
    da1, g["w1_ffn1"], g["w3_ffn1"], g["w2_ffn1"] = _ffn_bwd(
        flat(a1), wts["w1_ffn1"], wts["w3_ffn1"], wts["w2_ffn1"], ffn1_saved, flat(df1), "ffn1")
    grad_x, g["g_ffn1"], dsh1, dsc1 = _pre_bwd(
        x, None, None, wts["g_ffn1"], sh1, sc1, 0.0, unflat(da1), dx0_res, name="pre1_bwd")

    dmod = jnp.concatenate([t[:, 0, :] for t in (dsh1, dsc1, dgt1, dsh2, dsc2, dgt2, dsh3, dsc3, dgt3)], axis=-1)
    dmod_pad = jnp.zeros((SUBLANES, N_MOD * D_MODEL), F32).at[:n_b].set(dmod)
    dmod_s = dmod_pad.reshape(SUBLANES, N_SHARD, ADA_SHARD).transpose(1, 0, 2)
    g["w_ada"], g["b_ada"] = _ada_bwd(c_pad, dmod_s, name="ada_bwd")
    return loss, grad_x, g


IN_SPLITS = (("w_qkv", 3 * DN_WIDTH), ("w_z", DN_WIDTH), ("w_ba", 2 * DN_HEADS), ("w_s5", S5_WIDTH),
             ("w_ga", D_MODEL), ("w_gb", D_MODEL))
SHARDED = ("w_ada", "w1_ffn1", "w3_ffn1", "w2_ffn1", "w_in", "conv_qkv", "w_glu", "w_proj_a", "w_proj_b", "w_out",
           "w1_ffn2", "w3_ffn2", "w2_ffn2")
COLUMN_SHARDED = ("w_ada", "w1_ffn1", "w3_ffn1", "w_in", "conv_qkv", "w_proj_a", "w_proj_b", "w1_ffn2", "w3_ffn2")


def _cat_columns(stack):
    return stack.transpose(1, 0, 2).reshape(stack.shape[1], N_SHARD * stack.shape[2])


def _split_columns(full):
    n_r, n_c = full.shape
    return full.reshape(n_r, N_SHARD, n_c // N_SHARD).transpose(1, 0, 2)


def _gathered_weights(st, rep):
    w = {k: st[k] for k in ("w_ada", "w1_ffn1", "w3_ffn1", "w2_ffn1", "w1_ffn2", "w3_ffn2", "w2_ffn2")}
    w["b_ada"] = rep["b_ada"].reshape(N_SHARD, 1, ADA_SHARD)
    for k in ("g_ffn1", "g_mix", "g_ffn2", "g_final"):
        w[k] = rep[k].reshape(1, D_MODEL)
    w_in = _cat_columns(st["w_in"])
    start = 0
    for k, size in IN_SPLITS:
        w[k] = w_in[None, :, start:start + size]
        start += size
    w["w_gab"] = jnp.concatenate([w.pop("w_ga"), w.pop("w_gb")], axis=-1)
    w["w_s5"] = _split_columns(w["w_s5"][0])
    w["w_ba"] = jnp.pad(w["w_ba"], ((0, 0), (0, 0), (0, BA_PAD - 2 * DN_HEADS)))
    w["conv_qkv"] = _cat_columns(st["conv_qkv"])
    w["a_log"] = rep["a_log"].reshape(DN_HEADS, 1, 1)
    w["dt_bias"] = rep["dt_bias"].reshape(DN_HEADS, 1, 1)
    w["g_onorm"] = rep["g_onorm"].reshape(1, DN_HEAD_DIM)
    w["lam_re"] = rep["lam_re"].reshape(S5_GROUPS, 1, S5_STATE)
    w["lam_im"] = rep["lam_im"].reshape(S5_GROUPS, 1, S5_STATE)
    w["log_step"] = rep["log_step"].reshape(S5_GROUPS, 1, 1)
    w["bt_re"] = rep["b_re"][0].transpose(0, 2, 1)
    w["bt_im"] = rep["b_im"][0].transpose(0, 2, 1)
    w["c_re"] = rep["c_re"][0]
    w["c_im"] = rep["c_im"][0]
    w["d_skip"] = rep["d_skip"].reshape(1, S5_WIDTH)
    w["b_glu"] = rep["b_glu"].reshape(1, S5_WIDTH)
    w["w_glu"] = st["w_glu"].reshape(S5_WIDTH, S5_WIDTH)
    w["w_proj_a"] = _cat_columns(st["w_proj_a"])[None]
    w["w_proj_b"] = _cat_columns(st["w_proj_b"])[None]
    w["w_out"] = st["w_out"].reshape(1, D_MODEL, D_MODEL)
    return w


def _grads_to_problem_layout(g):
    st = {k: g[k] for k in ("w_ada", "w1_ffn1", "w3_ffn1", "w2_ffn1", "w1_ffn2", "w3_ffn2", "w2_ffn2")}
    w_in = jnp.concatenate([g[k][:, :size] for k, size in IN_SPLITS], axis=1)
    st["w_in"] = _split_columns(w_in)
    st["w_glu"] = g["w_glu"].reshape(N_SHARD, S5_WIDTH // N_SHARD, S5_WIDTH)
    st["w_proj_a"] = _split_columns(g["w_proj_a"])
    st["w_proj_b"] = _split_columns(g["w_proj_b"])
    st["w_out"] = g["w_out"].reshape(N_SHARD, D_MODEL // N_SHARD, D_MODEL)
    small = {
        "b_ada": g["b_ada"].reshape(1, N_MOD * D_MODEL),
        "g_ffn1": g["g_ffn1"], "g_mix": g["g_mix"], "g_ffn2": g["g_ffn2"], "g_final": g["g_final"].reshape(D_MODEL),
        "conv_qkv": g["conv_qkv"][None],
        "a_log": g["a_log"].reshape(1, DN_HEADS), "dt_bias": g["dt_bias"].reshape(1, DN_HEADS),
        "g_onorm": g["g_onorm"],
        "lam_re": g["lam_re"].reshape(1, S5_GROUPS, S5_STATE), "lam_im": g["lam_im"].reshape(1, S5_GROUPS, S5_STATE),
        "log_step": g["log_step"].reshape(1, S5_GROUPS),
        "b_re": g["bt_re"].transpose(0, 2, 1)[None], "b_im": g["bt_im"].transpose(0, 2, 1)[None],
        "c_re": g["c_re"][None], "c_im": g["c_im"][None],
        "d_skip": g["d_skip"], "b_glu": g["b_glu"],
    }
    return st, small


ELEMENTWISE_BLOCK_BYTES = 1 << 20


def _row_tile(n_rows, n_cols, n_lead=1, multiple=SUBLANES):
    best = None
    for t in range(multiple, n_rows + 1, multiple):
        if n_rows % t == 0 and n_lead * t * n_cols * 4 <= ELEMENTWISE_BLOCK_BYTES:
            best = t
    return best if best is not None else n_rows


def _add_sibling_half(g4, recv, my_c, *, name):
    n_sh, _, n_h, n_c = g4.shape
    th = _row_tile(n_h, n_c, multiple=2 * SUBLANES)

    def body(c_ref, g_ref, r_ref, o_ref):
        o_ref[0] = (g_ref[0, 0] + r_ref[0]).astype(o_ref.dtype)

    grid_spec = pltpu.PrefetchScalarGridSpec(
        num_scalar_prefetch=1, grid=(n_sh, n_h // th),
        in_specs=[pl.BlockSpec((1, 1, th, n_c), lambda s, i, c_ref: (s, c_ref[0], i, 0)),
                  pl.BlockSpec((1, th, n_c), lambda s, i, c_ref: (s, i, 0))],
        out_specs=pl.BlockSpec((1, th, n_c), lambda s, i, c_ref: (s, i, 0)))
    return pl.pallas_call(
        body, name=name, grid_spec=grid_spec, out_shape=jax.ShapeDtypeStruct((n_sh, n_h, n_c), MXU_DTYPE),
        compiler_params=_params(3 * _nbytes((th, n_c), F32)),
    )(*_hbm(my_c, g4, recv))


def _sum_slots(parts, *, name):
    n_p, n_r, n_c = parts.shape
    th = _row_tile(n_r, n_c, n_p)

    def body(p_ref, o_ref):
        total = p_ref[0].astype(F32)
        for k in range(1, n_p):
            total = total + p_ref[k].astype(F32)
        o_ref[...] = total

    return pl.pallas_call(
        body, name=name, grid=(n_r // th,),
        in_specs=[pl.BlockSpec((n_p, th, n_c), lambda i: (0, i, 0))],
        out_specs=pl.BlockSpec((th, n_c), lambda i: (i, 0)),
        out_shape=jax.ShapeDtypeStruct((n_r, n_c), F32),
        compiler_params=_params((n_p + 1) * _nbytes((th, n_c), F32)),
    )(*_hbm(parts))


def _cast_into_slot(w, place, dtype, *, name):
    n_r, n_c = w.shape
    th = _row_tile(n_r, n_c, multiple=2 * SUBLANES)

    def body(p_ref, w_ref, o_ref):
        o_ref[0] = w_ref[...].astype(o_ref.dtype)

    grid_spec = pltpu.PrefetchScalarGridSpec(
        num_scalar_prefetch=1, grid=(n_r // th,),
        in_specs=[pl.BlockSpec((th, n_c), lambda i, p: (i, 0))],
        out_specs=pl.BlockSpec((1, th, n_c), lambda i, p: (p[1], i, 0)))
    return pl.pallas_call(
        body, name=name, grid_spec=grid_spec, out_shape=jax.ShapeDtypeStruct((N_SHARD, n_r, n_c), dtype),
        compiler_params=_params(2 * _nbytes((th, n_c), F32)),
    )(*_hbm(place, w))


def _sum_chips(own, parts, place, *, name):
    n_sh, n_h, n_c = own.shape
    th = _row_tile(n_h, n_c, n_sh, multiple=2 * SUBLANES)

    def body(p_ref, own_ref, a_ref, b_ref, c_ref, o_ref):
        o_ref[0] = ((own_ref[0].astype(F32) + a_ref[0].astype(F32)) + b_ref[0].astype(F32)) + c_ref[0].astype(F32)

    slab = lambda k: pl.BlockSpec((1, th, n_c), lambda i, p, k=k: (p[k], i, 0))
    grid_spec = pltpu.PrefetchScalarGridSpec(
        num_scalar_prefetch=1, grid=(n_h // th,),
        in_specs=[slab(1), slab(2), slab(3), slab(4)], out_specs=slab(0))
    return pl.pallas_call(
        body, name=name, grid_spec=grid_spec, out_shape=jax.ShapeDtypeStruct((2, n_h, n_c), F32),
        compiler_params=_params(5 * _nbytes((th, n_c), F32)),
    )(*_hbm(place, own, parts, parts, parts))


def _adamw(w, g, m, v, *, name):
    n_r, n_c = w.shape
    th = _row_tile(n_r, n_c)
    bias1 = 1.0 - ADAM_B1 ** ADAM_STEP
    bias2 = 1.0 - ADAM_B2 ** ADAM_STEP

    def body(w_ref, g_ref, m_ref, v_ref, d_ref, mo_ref, vo_ref):
        gv = g_ref[...]
        m_new = ADAM_B1 * m_ref[...] + (1.0 - ADAM_B1) * gv
        v_new = ADAM_B2 * v_ref[...] + (1.0 - ADAM_B2) * jnp.square(gv)
        d_ref[...] = -ADAM_LR * ((m_new / bias1) / (jnp.sqrt(v_new / bias2) + ADAM_EPS) + ADAM_WD * w_ref[...])
        mo_ref[...] = m_new
        vo_ref[...] = v_new

    spec = pl.BlockSpec((th, n_c), lambda i: (i, 0))
    shape = jax.ShapeDtypeStruct((n_r, n_c), F32)
    return pl.pallas_call(
        body, name=name, grid=(n_r // th,), in_specs=[spec] * 4, out_specs=(spec,) * 3, out_shape=(shape,) * 3,
        compiler_params=_params(7 * _nbytes((th, n_c), F32)),
    )(*_hbm(w, g, m, v))


CHIP_FLIPS = ((1, 0), (0, 1), (1, 1))
DEVICE_FLIPS = tuple((fx, fy, fc) for fx in (0, 1) for fy in (0, 1) for fc in (0, 1))[1:]


def _exchange(ins, out_shapes, plan, n_local, n_remote, *, name, aliased=False):
    n_in, n_out = len(ins), len(out_shapes)

    def body(*refs):
        in_refs, out_refs = refs[:n_in], refs[n_in:n_in + n_out]
        send_sems, recv_sems, local_sems = refs[n_in + n_out:]
        me = (lax.axis_index("x"), lax.axis_index("y"), lax.axis_index("c"))
        local, remote = plan(in_refs, out_refs, me)
        assert len(local) == n_local and len(remote) == n_remote
        here = [pltpu.make_async_copy(src, dst, local_sems.at[i]) for i, (src, dst) in enumerate(local)]
        for cp in here:
            cp.start()
        sends = [pltpu.make_async_remote_copy(src_ref=src, dst_ref=dst, send_sem=send_sems.at[i], recv_sem=recv_sems.at[i],
                                              device_id=peer, device_id_type=pl.DeviceIdType.MESH)
                 for i, (src, dst, _, peer) in enumerate(remote)]
        for cp in sends:
            cp.start()
        for i, (src, _, landing, peer) in enumerate(remote):
            pltpu.make_async_remote_copy(src_ref=src, dst_ref=landing, send_sem=send_sems.at[i], recv_sem=recv_sems.at[i],
                                         device_id=peer, device_id_type=pl.DeviceIdType.MESH).wait_recv()
        for cp in sends:
            cp.wait_send()
        for cp in here:
            cp.wait()

    any_spec = pl.BlockSpec(memory_space=pl.ANY)
    return pl.pallas_call(
        body, name=name, in_specs=[any_spec] * n_in, out_specs=tuple([any_spec] * n_out), out_shape=tuple(out_shapes),
        scratch_shapes=[pltpu.SemaphoreType.DMA((n_remote,)), pltpu.SemaphoreType.DMA((n_remote,)),
                        pltpu.SemaphoreType.DMA((max(n_local, 1),))],
        input_output_aliases={k: k for k in range(n_in)} if aliased else {},
    )(*ins)


def _gather_shards(stacks, *, name):
    n = len(stacks)
    halved = [a.shape[1] >= 32 for a in stacks]
    n_ici = len(CHIP_FLIPS) * n
    n_pass = len(CHIP_FLIPS) * sum(halved)

    def body(*refs):
        outs = refs[n:2 * n]
        send_sems, recv_sems = refs[2 * n:]
        x, y, c = lax.axis_index("x"), lax.axis_index("y"), lax.axis_index("c")
        mine = 2 * x + y

        def rows(k, slot, half):
            if not halved[k]:
                return outs[k].at[slot]
            n_h = stacks[k].shape[1] // 2
            return outs[k].at[slot, pl.ds(pl.multiple_of(half * n_h, 16), n_h)]

        def copy(i, src, dst, peer):
            return pltpu.make_async_remote_copy(src_ref=src, dst_ref=dst, send_sem=send_sems.at[i], recv_sem=recv_sems.at[i],
                                                device_id=peer, device_id_type=pl.DeviceIdType.MESH)

        started = []
        for j, (fx, fy) in enumerate(CHIP_FLIPS):
            for k in range(n):
                cp = copy(j * n + k, rows(k, mine, c), rows(k, mine, c), (x ^ fx, y ^ fy, c))
                cp.start()
                started.append(cp)
        i_pass = n_ici
        expect = []
        for j, (fx, fy) in enumerate(CHIP_FLIPS):
            peer_chip = 2 * (x ^ fx) + (y ^ fy)
            for k in range(n):
                landed = rows(k, peer_chip, c)
                copy(j * n + k, landed, landed, (x ^ fx, y ^ fy, c)).wait_recv()
                if halved[k]:
                    cp = copy(i_pass, landed, landed, (x, y, 1 - c))
                    cp.start()
                    started.append(cp)
                    expect.append((i_pass, rows(k, peer_chip, 1 - c)))
                    i_pass += 1
        for i, landing in expect:
            copy(i, landing, landing, (x, y, 1 - c)).wait_recv()
        for cp in started:
            cp.wait_send()

    any_spec = pl.BlockSpec(memory_space=pl.ANY)
    n_sem = n_ici + n_pass
    return pl.pallas_call(
        body, name=name, in_specs=[any_spec] * n, out_specs=tuple([any_spec] * n),
        out_shape=tuple(jax.ShapeDtypeStruct(a.shape, a.dtype) for a in stacks),
        scratch_shapes=[pltpu.SemaphoreType.DMA((n_sem,)), pltpu.SemaphoreType.DMA((n_sem,))],
        input_output_aliases={k: k for k in range(n)},
    )(*stacks)


def _swap_sibling_halves(g4s, *, name):
    n = len(g4s)

    def plan(in_refs, out_refs, me):
        x, y, c = me
        remote = [(in_refs[k].at[:, 1 - c], out_refs[k], out_refs[k], (x, y, 1 - c)) for k in range(n)]
        return [], remote

    shapes = [jax.ShapeDtypeStruct((a.shape[0],) + a.shape[2:], a.dtype) for a in g4s]
    return _exchange(g4s, shapes, plan, 0, n, name=name)


def _scatter_to_chips(hs, *, name):
    n = len(hs)

    def plan(in_refs, out_refs, me):
        x, y, c = me
        mine = 2 * x + y
        remote = []
        for fx, fy in CHIP_FLIPS:
            px, py = x ^ fx, y ^ fy
            peer = 2 * px + py
            for k in range(n):
                remote.append((in_refs[k].at[peer], out_refs[k].at[mine], out_refs[k].at[peer], (px, py, c)))
        return [], remote

    shapes = [jax.ShapeDtypeStruct(a.shape, a.dtype) for a in hs]
    return _exchange(hs, shapes, plan, 0, len(CHIP_FLIPS) * n, name=name)


def _join_sibling_halves(rs, *, name):
    n = len(rs)

    def plan(in_refs, out_refs, me):
        x, y, c = me
        remote = [(out_refs[k].at[c], out_refs[k].at[c], out_refs[k].at[1 - c], (x, y, 1 - c)) for k in range(n)]
        return [], remote

    shapes = [jax.ShapeDtypeStruct(a.shape, a.dtype) for a in rs]
    return _exchange(rs, shapes, plan, 0, n, name=name, aliased=True)


def _gather_all_devices(packed, *, name):
    def plan(in_refs, out_refs, me):
        x, y, c = me
        mine = 4 * x + 2 * y + c
        remote = []
        for fx, fy, fc in DEVICE_FLIPS:
            px, py, pc = x ^ fx, y ^ fy, c ^ fc
            remote.append((in_refs[0], out_refs[0].at[mine], out_refs[0].at[4 * px + 2 * py + pc], (px, py, pc)))
        return [(in_refs[0], out_refs[0].at[mine])], remote

    shape = jax.ShapeDtypeStruct((2 * N_SHARD,) + packed.shape, packed.dtype)
    return _exchange([packed], [shape], plan, 1, len(DEVICE_FLIPS), name=name)[0]


WEIGHT_NAMES = ("w_ada", "b_ada", "g_ffn1", "w1_ffn1", "w3_ffn1", "w2_ffn1", "g_mix", "w_in", "conv_qkv", "a_log",
                "dt_bias", "g_onorm", "lam_re", "lam_im", "log_step", "b_re", "b_im", "c_re", "c_im", "d_skip", "w_glu",
                "b_glu", "w_proj_a", "w_proj_b", "w_out", "g_ffn2", "w1_ffn2", "w3_ffn2", "w2_ffn2", "g_final")
LARGE = tuple(n for n in SHARDED if n != "conv_qkv")
SMALL = tuple(n for n in WEIGHT_NAMES if n not in LARGE)
PACK_ROW = SUBLANES * LANES


def _pack(arrays):
    flat = jnp.concatenate([a.reshape(-1) for a in arrays])
    n_pad = -flat.shape[0] % PACK_ROW
    return jnp.pad(flat, (0, n_pad)).reshape(-1, LANES)


def _unpack(packed, shapes):
    flat = packed.reshape(-1)
    out, start = [], 0
    for s in shapes:
        size = math.prod(s)
        out.append(flat[start:start + size].reshape(s))
        start += size
    return out


def _step(x, c, target, weights, m_in, v_in):
    xi, yi, ci = lax.axis_index("x"), lax.axis_index("y"), lax.axis_index("c")
    my_chip = 2 * xi + yi

    others = [k + (k >= my_chip).astype(jnp.int32) for k in range(N_SHARD - 1)]
    place = jnp.stack([ci, my_chip] + others).astype(jnp.int32)

    slots = [_cast_into_slot(weights[n][0], place, F32 if n == "conv_qkv" else MXU_DTYPE, name=f"cast_{n}") for n in SHARDED]
    stacks = dict(zip(SHARDED, _gather_shards(slots, name="gather_weights")))
    rep = {n: weights[n] for n in WEIGHT_NAMES if n not in SHARDED}
    loss, grad_x, g = _local_step(x, c, target, _gathered_weights(stacks, rep))
    g_stacks, g_small = _grads_to_problem_layout(g)

    g4s = [g_stacks[n].reshape(N_SHARD, 2, g_stacks[n].shape[1] // 2, g_stacks[n].shape[2]) for n in LARGE]
    from_sibling = _swap_sibling_halves(g4s, name="swap_sibling_halves")
    chip_sums = [_add_sibling_half(a, r, place, name=f"chip_sum_{n}") for n, a, r in zip(LARGE, g4s, from_sibling)]
    from_chips = _scatter_to_chips(chip_sums, name="scatter_to_chips")
    reduced = [_sum_chips(h, p, place, name=f"sum_chips_{n}") for n, h, p in zip(LARGE, chip_sums, from_chips)]
    joined = _join_sibling_halves(reduced, name="join_sibling_halves")
    grads = {n: j.reshape(1, 2 * j.shape[1], j.shape[2]) for n, j in zip(LARGE, joined)}

    small_shapes = [g_small[n].shape for n in SMALL] + [(1, 1)]
    packed = _pack([g_small[n] for n in SMALL] + [loss])
    total = _sum_slots(_gather_all_devices(packed, name="gather_small"), name="sum_small")
    *small_grads, loss_sum = _unpack(total, small_shapes)
    grads.update(zip(SMALL, small_grads))
    n_conv = weights["conv_qkv"].shape[-1]
    grads["conv_qkv"] = lax.dynamic_slice_in_dim(grads["conv_qkv"], my_chip * n_conv, n_conv, axis=2)

    delta, new_m, new_v = {}, {}, {}
    for n in LARGE + ("conv_qkv",):
        two_d = lambda a: a.reshape(-1, a.shape[-1])
        outs = _adamw(two_d(weights[n]), two_d(grads[n]), two_d(m_in[n]), two_d(v_in[n]), name=f"adamw_{n}")
        delta[n], new_m[n], new_v[n] = [o.reshape(weights[n].shape) for o in outs]
    packed_names = tuple(n for n in SMALL if n != "conv_qkv")
    shapes = [weights[n].shape for n in packed_names]
    outs = _adamw(*[_pack([d[n] for n in packed_names]) for d in (weights, grads, m_in, v_in)], name="adamw_small")
    for d, o in zip((delta, new_m, new_v), outs):
        d.update(zip(packed_names, _unpack(o, shapes)))
    return (loss_sum.reshape(()), grad_x, *[grads[n] for n in WEIGHT_NAMES], *[delta[n] for n in WEIGHT_NAMES],
            *[new_m[n] for n in WEIGHT_NAMES], *[new_v[n] for n in WEIGHT_NAMES])


def kernel(x, c, w_ada, b_ada, g_ffn1, w1_ffn1, w3_ffn1, w2_ffn1, g_mix, w_in, conv_qkv, a_log, dt_bias, g_onorm, lam_re, lam_im, log_step, b_re, b_im, c_re, c_im, d_skip, w_glu, b_glu, w_proj_a, w_proj_b, w_out, g_ffn2, w1_ffn2, w3_ffn2, w2_ffn2, g_final, loss_target, m_w_ada, m_b_ada, m_g_ffn1, m_w1_ffn1, m_w3_ffn1, m_w2_ffn1, m_g_mix, m_w_in, m_conv_qkv, m_a_log, m_dt_bias, m_g_onorm, m_lam_re, m_lam_im, m_log_step, m_b_re, m_b_im, m_c_re, m_c_im, m_d_skip, m_w_glu, m_b_glu, m_w_proj_a, m_w_proj_b, m_w_out, m_g_ffn2, m_w1_ffn2, m_w3_ffn2, m_w2_ffn2, m_g_final, v_w_ada, v_b_ada, v_g_ffn1, v_w1_ffn1, v_w3_ffn1, v_w2_ffn1, v_g_mix, v_w_in, v_conv_qkv, v_a_log, v_dt_bias, v_g_onorm, v_lam_re, v_lam_im, v_log_step, v_b_re, v_b_im, v_c_re, v_c_im, v_d_skip, v_w_glu, v_b_glu, v_w_proj_a, v_w_proj_b, v_w_out, v_g_ffn2, v_w1_ffn2, v_w3_ffn2, v_w2_ffn2, v_g_final):
    w_vals = (w_ada, b_ada, g_ffn1, w1_ffn1, w3_ffn1, w2_ffn1, g_mix, w_in, conv_qkv, a_log, dt_bias, g_onorm, lam_re, lam_im, log_step, b_re, b_im, c_re, c_im, d_skip, w_glu, b_glu, w_proj_a, w_proj_b, w_out, g_ffn2, w1_ffn2, w3_ffn2, w2_ffn2, g_final)
    m_vals = (m_w_ada, m_b_ada, m_g_ffn1, m_w1_ffn1, m_w3_ffn1, m_w2_ffn1, m_g_mix, m_w_in, m_conv_qkv, m_a_log, m_dt_bias, m_g_onorm, m_lam_re, m_lam_im, m_log_step, m_b_re, m_b_im, m_c_re, m_c_im, m_d_skip, m_w_glu, m_b_glu, m_w_proj_a, m_w_proj_b, m_w_out, m_g_ffn2, m_w1_ffn2, m_w3_ffn2, m_w2_ffn2, m_g_final)
    v_vals = (v_w_ada, v_b_ada, v_g_ffn1, v_w1_ffn1, v_w3_ffn1, v_w2_ffn1, v_g_mix, v_w_in, v_conv_qkv, v_a_log, v_dt_bias, v_g_onorm, v_lam_re, v_lam_im, v_log_step, v_b_re, v_b_im, v_c_re, v_c_im, v_d_skip, v_w_glu, v_b_glu, v_w_proj_a, v_w_proj_b, v_w_out, v_g_ffn2, v_w1_ffn2, v_w3_ffn2, v_w2_ffn2, v_g_final)
    return _step(x, c, loss_target, dict(zip(WEIGHT_NAMES, w_vals)), dict(zip(WEIGHT_NAMES, m_vals)),
                 dict(zip(WEIGHT_NAMES, v_vals)))
```

```python
import functools
import math

import jax
import jax.numpy as jnp
from jax import lax
from jax.experimental import pallas as pl
from jax.experimental.pallas import tpu as pltpu

F32 = jnp.float32
BF16 = jnp.bfloat16
MXU_DTYPE = BF16

D_MODEL = 1024
D_FF = 2816
DN_HEADS = 8
DN_HEAD_DIM = 64
DN_WIDTH = DN_HEADS * DN_HEAD_DIM
CONV_WIDTH = 4
CHUNK = 64
S5_GROUP_CH = 16
S5_GROUPS = 32
S5_WIDTH = S5_GROUPS * S5_GROUP_CH
S5_STATE = 64
S5_LANES = S5_GROUPS * S5_STATE
N_MOD = 9
EPS = 1e-6
N_SHARD = 4
FF_SHARD = D_FF // N_SHARD
BA_PAD = 128

ADAM_LR = 0.001
ADAM_B1 = 0.9
ADAM_B2 = 0.999
ADAM_EPS = 1e-08
ADAM_WD = 0.01
ADAM_STEP = 10

VMEM_BYTES_V7X = 64 * 1024 * 1024
SUBLANES = 8
LANES = 128


def _params(block_bytes, extra_bytes=0):
    need = 2 * block_bytes + extra_bytes + (4 << 20)
    return pltpu.CompilerParams(vmem_limit_bytes=int(min(max(need, 16 << 20), VMEM_BYTES_V7X - (8 << 20))))


def _nbytes(shape, dtype):
    return math.prod(shape) * jnp.dtype(dtype).itemsize


HBM_OPERAND_BYTES = 1 << 20


def _hbm(*args):
    return [pltpu.with_memory_space_constraint(a, pltpu.HBM) if _nbytes(a.shape, a.dtype) >= HBM_OPERAND_BYTES else a
            for a in args]


_NN = (((1,), (0,)), ((), ()))
_NT = (((1,), (1,)), ((), ()))
_TN = (((0,), (0,)), ((), ()))


def _mm_act(pairs, mode, *, name, out_sharded=False, reduce_shards=False, out_dtype=F32, add=None, tm=512):
    n_tok = pairs[0][0].shape[1]
    n_out = pairs[0][1].shape[2] if mode == "nn" else pairs[0][1].shape[1]
    tm = min(tm, n_tok)
    tn = n_out if n_out <= 1536 else 1024
    assert n_tok % tm == 0 and n_out % tn == 0
    n_so = N_SHARD if out_sharded else 1
    n_red = N_SHARD if reduce_shards else 1
    grid = (n_so, n_tok // tm, n_out // tn, n_red)
    dims = _NN if mode == "nn" else _NT

    def shard_of(n_sh):
        if n_sh == 1:
            return lambda s, r: 0
        return (lambda s, r: s) if out_sharded else (lambda s, r: r)

    in_specs, args, blk = [], [], 0
    for a, b in pairs:
        k_dim = a.shape[2]
        sa, sb = shard_of(a.shape[0]), shard_of(b.shape[0])
        in_specs.append(pl.BlockSpec((1, tm, k_dim), lambda s, i, j, r, sa=sa: (sa(s, r), i, 0)))
        if mode == "nn":
            assert b.shape[1] == k_dim
            in_specs.append(pl.BlockSpec((1, k_dim, tn), lambda s, i, j, r, sb=sb: (sb(s, r), 0, j)))
        else:
            assert b.shape[2] == k_dim
            in_specs.append(pl.BlockSpec((1, tn, k_dim), lambda s, i, j, r, sb=sb: (sb(s, r), j, 0)))
        args += [a, b]
        blk += _nbytes((tm, k_dim), a.dtype) + _nbytes((k_dim, tn), b.dtype)
    if add is not None:
        in_specs.append(pl.BlockSpec((1, tm, tn), lambda s, i, j, r: (s, i, j)))
        args.append(add)
        blk += _nbytes((tm, tn), F32)
    blk += _nbytes((tm, tn), out_dtype)
    n_pairs = len(pairs)

    def body(*refs):
        out_ref = refs[2 * n_pairs + (add is not None)]
        acc = None
        for k in range(n_pairs):
            a = refs[2 * k][0].astype(MXU_DTYPE)
            b = refs[2 * k + 1][0].astype(MXU_DTYPE)
            d = lax.dot_general(a, b, dims, preferred_element_type=F32)
            acc = d if acc is None else acc + d

        def finish(total):
            if add is not None:
                total = total + refs[2 * n_pairs][0]
            out_ref[0] = total.astype(out_dtype)

        if n_red == 1:
            finish(acc)
        else:
            acc_ref = refs[-1]
            r = pl.program_id(3)

            @pl.when(r == 0)
            def _():
                acc_ref[...] = acc

            @pl.when(r > 0)
            def _():
                acc_ref[...] += acc

            @pl.when(r == n_red - 1)
            def _():
                finish(acc_ref[...])

    return pl.pallas_call(
        body,
        name=name,
        grid=grid,
        in_specs=in_specs,
        out_specs=pl.BlockSpec((1, tm, tn), lambda s, i, j, r: (s, i, j)),
        out_shape=jax.ShapeDtypeStruct((n_so, n_tok, n_out), out_dtype),
        scratch_shapes=[pltpu.VMEM((tm, tn), F32)] if n_red > 1 else [],
        compiler_params=_params(blk, 3 * _nbytes((tm, tn), F32)),
    )(*_hbm(*args))


def _mm_tn(a, b, *, name, tt=512):
    n_tok, k_dim = a.shape[1], a.shape[2]
    n_out = b.shape[2]
    tt = min(tt, n_tok)
    tk = k_dim if k_dim <= 1536 else 1024
    tn = n_out if n_out <= 1536 else 1024
    assert n_tok % tt == 0 and k_dim % tk == 0 and n_out % tn == 0
    n_so = max(a.shape[0], b.shape[0])
    sa = (lambda s: s) if a.shape[0] > 1 else (lambda s: 0)
    sb = (lambda s: s) if b.shape[0] > 1 else (lambda s: 0)
    grid = (n_so, k_dim // tk, n_out // tn, n_tok // tt)

    def body(a_ref, b_ref, out_ref):
        d = lax.dot_general(a_ref[0].astype(MXU_DTYPE), b_ref[0].astype(MXU_DTYPE), _TN, preferred_element_type=F32)
        t = pl.program_id(3)

        @pl.when(t == 0)
        def _():
            out_ref[0] = d

        @pl.when(t > 0)
        def _():
            out_ref[0] += d

    blk = _nbytes((tt, tk), a.dtype) + _nbytes((tt, tn), b.dtype) + _nbytes((tk, tn), F32)
    return pl.pallas_call(
        body,
        name=name,
        grid=grid,
        in_specs=[
            pl.BlockSpec((1, tt, tk), lambda s, ki, nj, t: (sa(s), t, ki)),
            pl.BlockSpec((1, tt, tn), lambda s, ki, nj, t: (sb(s), t, nj)),
        ],
        out_specs=pl.BlockSpec((1, tk, tn), lambda s, ki, nj, t: (s, ki, nj)),
        out_shape=jax.ShapeDtypeStruct((n_so, k_dim, n_out), F32),
        compiler_params=_params(blk, 2 * _nbytes((tk, tn), F32) + _nbytes((tt, tk), F32)),
    )(*_hbm(a, b))


@functools.partial(jax.custom_vjp, nondiff_argnums=(2,))
def _mdot(a, b, dims):
    return lax.dot_general(a.astype(MXU_DTYPE), b.astype(MXU_DTYPE), dims, preferred_element_type=F32)


def _mdot_fwd(a, b, dims):
    return _mdot(a, b, dims), (a, b)


def _mdot_bwd(dims, res, g):
    a, b = res
    (ca, cb), (ba, bb) = dims
    nb = len(ba)
    assert tuple(ba) == tuple(range(nb)) and tuple(bb) == tuple(range(nb)) and len(ca) == 1 and a.ndim == nb + 2
    batch = (tuple(range(nb)), tuple(range(nb)))
    ra, rb = nb, nb + 1
    a_free = (set(range(nb, nb + 2)) - set(ca)).pop()
    b_free = (set(range(nb, nb + 2)) - set(cb)).pop()
    if a_free < ca[0]:
        da = _mdot(g, b, (((rb,), (b_free,)), batch))
    else:
        da = _mdot(b, g, (((b_free,), (rb,)), batch))
    if b_free > cb[0]:
        db = _mdot(a, g, (((a_free,), (ra,)), batch))
    else:
        db = _mdot(g, a, (((ra,), (a_free,)), batch))
    return da.astype(a.dtype), db.astype(b.dtype)


_mdot.defvjp(_mdot_fwd, _mdot_bwd)


def _rms(x, gain):
    return x * lax.rsqrt(jnp.mean(x * x, axis=-1, keepdims=True) + EPS) * gain


def _pre_fn(coef, x_in, f, gate, gain, shift, scale):
    x_new = x_in if f is None else x_in + coef * gate * f
    return x_new, _rms(x_new, gain) * (1.0 + scale) + shift


def _row_spec(ts):
    return pl.BlockSpec((1, ts, D_MODEL), lambda b, j: (b, j, 0))


_BATCH_VEC = pl.BlockSpec((1, 1, D_MODEL), lambda b, j: (b, 0, 0))
_ONE_VEC = pl.BlockSpec((1, D_MODEL), lambda b, j: (0, 0))


def _pre(x_in, f, gate, gain, shift, scale, coef, *, name, ts=512):
    n_b, n_s, _ = x_in.shape
    ts = min(ts, n_s)
    has_res = f is not None

    def body(*refs):
        if has_res:
            x_ref, f_ref, gate_ref, gain_ref, sh_ref, sc_ref, xn_ref, a_ref = refs
            x_new, a = _pre_fn(coef, x_ref[0], f_ref[0], gate_ref[0], gain_ref[...], sh_ref[0], sc_ref[0])
            xn_ref[0] = x_new
        else:
            x_ref, gain_ref, sh_ref, sc_ref, a_ref = refs
            _, a = _pre_fn(coef, x_ref[0], None, None, gain_ref[...], sh_ref[0], sc_ref[0])
        a_ref[0] = a.astype(a_ref.dtype)

    row = _row_spec(ts)
    if has_res:
        args = (x_in, f, gate, gain, shift, scale)
        in_specs = [row, row, _BATCH_VEC, _ONE_VEC, _BATCH_VEC, _BATCH_VEC]
        out_specs = (row, row)
        out_shape = (jax.ShapeDtypeStruct(x_in.shape, F32), jax.ShapeDtypeStruct(x_in.shape, MXU_DTYPE))
    else:
        args = (x_in, gain, shift, scale)
        in_specs = [row, _ONE_VEC, _BATCH_VEC, _BATCH_VEC]
        out_specs = row
        out_shape = jax.ShapeDtypeStruct(x_in.shape, MXU_DTYPE)
    return pl.pallas_call(
        body, name=name, grid=(n_b, n_s // ts), in_specs=in_specs, out_specs=out_specs, out_shape=out_shape,
        compiler_params=_params(5 * _nbytes((ts, D_MODEL), F32), 4 * _nbytes((ts, D_MODEL), F32)),
    )(*_hbm(*args))


def _accumulate(ref, value, first):
    @pl.when(first)
    def _():
        ref[...] = value

    @pl.when(jnp.logical_not(first))
    def _():
        ref[...] += value


def _pre_bwd(x_in, f, gate, gain, shift, scale, coef, da, dx_up, *, name, ts=512):
    n_b, n_s, _ = x_in.shape
    ts = min(ts, n_s)
    has_res = f is not None
    has_up = dx_up is not None

    def body(*refs):
        refs = list(refs)
        x_ref = refs.pop(0)
        f_ref, gate_ref = (refs.pop(0), refs.pop(0)) if has_res else (None, None)
        gain_ref, sh_ref, sc_ref, da_ref = refs.pop(0), refs.pop(0), refs.pop(0), refs.pop(0)
        up_ref = refs.pop(0) if has_up else None
        dx_ref = refs.pop(0)
        df_ref, dgate_ref = (refs.pop(0), refs.pop(0)) if has_res else (None, None)
        dgain_ref, dsh_ref, dsc_ref = refs
        b, j = pl.program_id(0), pl.program_id(1)
        da_v = da_ref[0].astype(F32)
        up_v = up_ref[0] if has_up else jnp.zeros((ts, D_MODEL), F32)
        if has_res:
            fn = functools.partial(_pre_fn, coef)
            _, pull = jax.vjp(fn, x_ref[0], f_ref[0], gate_ref[0], gain_ref[...], sh_ref[0], sc_ref[0])
            dx, df, dgate, dgain, dsh, dsc = pull((up_v, da_v))
            df_ref[0] = df.astype(df_ref.dtype)
            _accumulate(dgate_ref, dgate[None], j == 0)
        else:
            fn = lambda x, g, sh, sc: _pre_fn(coef, x, None, None, g, sh, sc)
            _, pull = jax.vjp(fn, x_ref[0], gain_ref[...], sh_ref[0], sc_ref[0])
            dx, dgain, dsh, dsc = pull((up_v, da_v))
        dx_ref[0] = dx
        _accumulate(dgain_ref, dgain, jnp.logical_and(b == 0, j == 0))
        _accumulate(dsh_ref, dsh[None], j == 0)
        _accumulate(dsc_ref, dsc[None], j == 0)

    row = _row_spec(ts)
    args, in_specs = [x_in], [row]
    if has_res:
        args += [f, gate]
        in_specs += [row, _BATCH_VEC]
    args += [gain, shift, scale, da]
    in_specs += [_ONE_VEC, _BATCH_VEC, _BATCH_VEC, row]
    if has_up:
        args.append(dx_up)
        in_specs.append(row)
    vec = jax.ShapeDtypeStruct((n_b, 1, D_MODEL), F32)
    out_shape, out_specs = [jax.ShapeDtypeStruct(x_in.shape, F32)], [row]
    if has_res:
        out_shape += [jax.ShapeDtypeStruct(x_in.shape, MXU_DTYPE), vec]
        out_specs += [row, _BATCH_VEC]
    out_shape += [jax.ShapeDtypeStruct((1, D_MODEL), F32), vec, vec]
    out_specs += [_ONE_VEC, _BATCH_VEC, _BATCH_VEC]
    return pl.pallas_call(
        body, name=name, grid=(n_b, n_s // ts), in_specs=in_specs, out_specs=tuple(out_specs), out_shape=tuple(out_shape),
        compiler_params=_params(6 * _nbytes((ts, D_MODEL), F32), 8 * _nbytes((ts, D_MODEL), F32)),
    )(*_hbm(*args))


def _final_fn(x_in, f, gate, gain, target):
    x_new = x_in + 0.5 * gate * f
    err = jnp.square(_rms(x_new, gain) - target)
    return 0.5 * jnp.sum(jnp.mean(err, axis=-1))


def _final(x_in, f, gate, gain, target, *, name, ts=512):
    n_b, n_s, _ = x_in.shape
    ts = min(ts, n_s)

    def body(x_ref, f_ref, gate_ref, gain_ref, t_ref, loss_ref, dx_ref, df_ref, dgate_ref, dgain_ref):
        b, j = pl.program_id(0), pl.program_id(1)
        loss, (dx, df, dgate, dgain) = jax.value_and_grad(_final_fn, argnums=(0, 1, 2, 3))(
            x_ref[0], f_ref[0], gate_ref[0], gain_ref[...], t_ref[0])
        first = jnp.logical_and(b == 0, j == 0)
        _accumulate(loss_ref, jnp.reshape(loss, (1, 1)), first)
        dx_ref[0] = dx
        df_ref[0] = df.astype(df_ref.dtype)
        _accumulate(dgate_ref, dgate[None], j == 0)
        _accumulate(dgain_ref, dgain, first)

    row = _row_spec(ts)
    return pl.pallas_call(
        body, name=name, grid=(n_b, n_s // ts),
        in_specs=[row, row, _BATCH_VEC, _ONE_VEC, row],
        out_specs=(pl.BlockSpec((1, 1), lambda b, j: (0, 0)), row, row, _BATCH_VEC, _ONE_VEC),
        out_shape=(jax.ShapeDtypeStruct((1, 1), F32), jax.ShapeDtypeStruct(x_in.shape, F32),
                   jax.ShapeDtypeStruct(x_in.shape, MXU_DTYPE), jax.ShapeDtypeStruct((n_b, 1, D_MODEL), F32),
                   jax.ShapeDtypeStruct((1, D_MODEL), F32)),
        compiler_params=_params(5 * _nbytes((ts, D_MODEL), F32), 8 * _nbytes((ts, D_MODEL), F32)),
    )(*_hbm(x_in, f, gate, gain, target))


def _ffn_up(a, w1s, w3s, *, name, tm=512):
    n_tok = a.shape[0]
    tm = min(tm, n_tok)

    def body(a_ref, w1_ref, w3_ref, h1_ref, h3_ref, g_ref):
        av = a_ref[...].astype(MXU_DTYPE)
        h1 = lax.dot_general(av, w1_ref[0].astype(MXU_DTYPE), _NT, preferred_element_type=F32)
        h3 = lax.dot_general(av, w3_ref[0].astype(MXU_DTYPE), _NT, preferred_element_type=F32)
        h1_ref[0] = h1.astype(h1_ref.dtype)
        h3_ref[0] = h3.astype(h3_ref.dtype)
        g_ref[0] = (jax.nn.silu(h1) * h3).astype(g_ref.dtype)

    w_spec = pl.BlockSpec((1, FF_SHARD, D_MODEL), lambda s, i: (s, 0, 0))
    h_spec = pl.BlockSpec((1, tm, FF_SHARD), lambda s, i: (s, i, 0))
    h_shape = jax.ShapeDtypeStruct((N_SHARD, n_tok, FF_SHARD), MXU_DTYPE)
    blk = _nbytes((tm, D_MODEL), a.dtype) + 2 * _nbytes((D_MODEL, FF_SHARD), w1s.dtype) + 3 * _nbytes((tm, FF_SHARD), MXU_DTYPE)
    return pl.pallas_call(
        body, name=name, grid=(N_SHARD, n_tok // tm),
        in_specs=[pl.BlockSpec((tm, D_MODEL), lambda s, i: (i, 0)), w_spec, w_spec],
        out_specs=(h_spec, h_spec, h_spec), out_shape=(h_shape, h_shape, h_shape),
        compiler_params=_params(blk, 6 * _nbytes((tm, FF_SHARD), F32)),
    )(*_hbm(a, w1s, w3s))


def _ffn_down_bwd(df, w2s, h1, h3, *, name, tm=512):
    n_tok = df.shape[0]
    tm = min(tm, n_tok)

    def body(df_ref, w2_ref, h1_ref, h3_ref, dh1_ref, dh3_ref):
        dg = lax.dot_general(df_ref[...].astype(MXU_DTYPE), w2_ref[0].astype(MXU_DTYPE), _NT, preferred_element_type=F32)
        h1v = h1_ref[0].astype(F32)
        h3v = h3_ref[0].astype(F32)
        sig = jax.nn.sigmoid(h1v)
        dh3_ref[0] = (dg * (h1v * sig)).astype(dh3_ref.dtype)
        dh1_ref[0] = (dg * h3v * (sig * (1.0 + h1v * (1.0 - sig)))).astype(dh1_ref.dtype)

    h_spec = pl.BlockSpec((1, tm, FF_SHARD), lambda s, i: (s, i, 0))
    h_shape = jax.ShapeDtypeStruct((N_SHARD, n_tok, FF_SHARD), MXU_DTYPE)
    blk = _nbytes((tm, D_MODEL), df.dtype) + _nbytes((FF_SHARD, D_MODEL), w2s.dtype) + 4 * _nbytes((tm, FF_SHARD), MXU_DTYPE)
    return pl.pallas_call(
        body, name=name, grid=(N_SHARD, n_tok // tm),
        in_specs=[pl.BlockSpec((tm, D_MODEL), lambda s, i: (i, 0)),
                  pl.BlockSpec((1, FF_SHARD, D_MODEL), lambda s, i: (s, 0, 0)), h_spec, h_spec],
        out_specs=(h_spec, h_spec), out_shape=(h_shape, h_shape),
        compiler_params=_params(blk, 8 * _nbytes((tm, FF_SHARD), F32)),
    )(*_hbm(df, w2s, h1, h3))


def _ffn_fwd(a, w1s, w3s, w2s, tag):
    h1, h3, g = _ffn_up(a, w1s, w3s, name=f"{tag}_up")
    f = _mm_act([(g, w2s)], "nn", reduce_shards=True, name=f"{tag}_down")[0]
    return f, (h1, h3, g)


def _ffn_bwd(a, w1s, w3s, w2s, saved, df, tag):
    h1, h3, g = saved
    dh1, dh3 = _ffn_down_bwd(df, w2s, h1, h3, name=f"{tag}_down_bwd")
    da = _mm_act([(dh1, w1s), (dh3, w3s)], "nn", reduce_shards=True, name=f"{tag}_up_bwd")[0]
    a3 = a[None]
    dw1 = _mm_tn(dh1, a3, name=f"{tag}_dw1")
    dw3 = _mm_tn(dh3, a3, name=f"{tag}_dw3")
    dw2 = _mm_tn(g, df[None], name=f"{tag}_dw2")
    return da, dw1, dw3, dw2


CONV_LANES = 256


def _shift_down(x, d):
    if d == 0:
        return x
    row = lax.broadcasted_iota(jnp.int32, x.shape, 0)
    return jnp.where(row >= d, pltpu.roll(x, d, 0), 0.0)


def _shift_up(x, d):
    if d == 0:
        return x
    n = x.shape[0]
    row = lax.broadcasted_iota(jnp.int32, x.shape, 0)
    return jnp.where(row < n - d, pltpu.roll(x, n - d, 0), 0.0)


def _conv_pre(x, w):
    acc = None
    for j in range(CONV_WIDTH):
        term = w[j:j + 1, :] * _shift_down(x, CONV_WIDTH - 1 - j)
        acc = term if acc is None else acc + term
    return acc


def _conv_fwd(x, w, *, name):
    n_b, n_s, n_c = x.shape
    spec = pl.BlockSpec((1, n_s, CONV_LANES), lambda b, cj: (b, 0, cj))

    def body(x_ref, w_ref, o_ref):
        o_ref[0] = jax.nn.silu(_conv_pre(x_ref[0], w_ref[...]))

    return pl.pallas_call(
        body, name=name, grid=(n_b, n_c // CONV_LANES),
        in_specs=[spec, pl.BlockSpec((CONV_WIDTH, CONV_LANES), lambda b, cj: (0, cj))],
        out_specs=spec, out_shape=jax.ShapeDtypeStruct(x.shape, F32),
        compiler_params=_params(2 * _nbytes((n_s, CONV_LANES), F32), 6 * _nbytes((n_s, CONV_LANES), F32)),
    )(*_hbm(x, w))


def _conv_bwd(x, w, dout, *, name):
    n_b, n_s, n_c = x.shape
    spec = pl.BlockSpec((1, n_s, CONV_LANES), lambda cj, b: (b, 0, cj))
    w_spec = pl.BlockSpec((CONV_WIDTH, CONV_LANES), lambda cj, b: (0, cj))

    def body(x_ref, w_ref, do_ref, dx_ref, dw_ref):
        xv, wv = x_ref[0], w_ref[...]
        pre = _conv_pre(xv, wv)
        sig = jax.nn.sigmoid(pre)
        dpre = do_ref[0] * (sig * (1.0 + pre * (1.0 - sig)))
        dx = None
        first = pl.program_id(1) == 0
        for j in range(CONV_WIDTH):
            d = CONV_WIDTH - 1 - j
            term = wv[j:j + 1, :] * _shift_up(dpre, d)
            dx = term if dx is None else dx + term
            dwj = jnp.sum(dpre * _shift_down(xv, d), axis=0, keepdims=True)
            _accumulate(dw_ref.at[j:j + 1, :], dwj, first)
        dx_ref[0] = dx.astype(dx_ref.dtype)

    return pl.pallas_call(
        body, name=name, grid=(n_c // CONV_LANES, n_b),
        in_specs=[spec, w_spec, spec], out_specs=(spec, w_spec),
        out_shape=(jax.ShapeDtypeStruct(x.shape, MXU_DTYPE), jax.ShapeDtypeStruct((CONV_WIDTH, n_c), F32)),
        compiler_params=_params(3 * _nbytes((n_s, CONV_LANES), F32), 8 * _nbytes((n_s, CONV_LANES), F32)),
    )(*_hbm(x, w, dout))


_BNT = (((2,), (2,)), ((0,), (0,)))
_BNN = (((2,), (1,)), ((0,), (0,)))
_BTN = (((1,), (1,)), ((0,), (0,)))
DN_PREP_CHUNKS = 8
DN_SCAN_HEADS = 4
N_DOUBLINGS = 5


def _fdot(a, b, dims):
    return lax.dot_general(a, b, dims, precision=lax.Precision.HIGHEST, preferred_element_type=F32)


def _hdot(a, b, dims):
    return lax.dot_general(a, b, dims, precision=lax.Precision.HIGH, preferred_element_type=F32)


def _solve_by_doubling(a, rhs_u, rhs_w):
    row = lax.broadcasted_iota(jnp.int32, (CHUNK, CHUNK), 0)
    col = lax.broadcasted_iota(jnp.int32, (CHUNK, CHUNK), 1)
    inv = jnp.where(row == col, 1.0, 0.0) - a
    power = a
    for _ in range(N_DOUBLINGS):
        power = _hdot(power, power, _BNN)
        inv = inv + _hdot(inv, power, _BNN)
    return _hdot(inv, rhs_u, _BNN), _hdot(inv, rhs_w, _BNN), inv


@jax.custom_vjp
def _solve_saved(a, rhs_u, rhs_w, inv, u, w):
    return u, w


def _solve_saved_fwd(a, rhs_u, rhs_w, inv, u, w):
    return (u, w), (inv, u, w)


def _solve_saved_bwd(res, cts):
    inv, u, w = res
    gu = _hdot(inv, cts[0], _BTN)
    gw = _hdot(inv, cts[1], _BTN)
    da = -(_hdot(gu, u, _BNT) + _hdot(gw, w, _BNT))
    return da, gu, gw, jnp.zeros_like(inv), jnp.zeros_like(u), jnp.zeros_like(w)


_solve_saved.defvjp(_solve_saved_fwd, _solve_saved_bwd)


def _dn_prep_fn(solve, qc, kc, vc, bl, lac, lar, a_log, dt_bias):
    q = qc * lax.rsqrt(jnp.sum(qc * qc, axis=-1, keepdims=True) + EPS) * (DN_HEAD_DIM ** -0.5)
    k = kc * lax.rsqrt(jnp.sum(kc * kc, axis=-1, keepdims=True) + EPS)
    beta = jax.nn.sigmoid(bl)
    neg_a = -jnp.exp(a_log)
    lgc = neg_a * jax.nn.softplus(lac + dt_bias)
    lgr = neg_a * jax.nn.softplus(lar + dt_bias)
    row = lax.broadcasted_iota(jnp.int32, (CHUNK, CHUNK), 0)
    col = lax.broadcasted_iota(jnp.int32, (CHUNK, CHUNK), 1)
    causal, strict = row >= col, row > col
    g_c = jnp.sum(jnp.where(causal, lgr, 0.0), axis=-1, keepdims=True)
    g_r = jnp.sum(jnp.where(row <= col, lgc, 0.0), axis=-2, keepdims=True)
    decay = jnp.exp(jnp.where(causal, g_c - g_r, -jnp.inf))
    kb = k * beta
    a = jnp.where(strict, _mdot(kb, k, _BNT) * decay, 0.0)
    u, w, extra = solve(a, vc * beta, kb * jnp.exp(g_c))
    attn = _mdot(q, k, _BNT) * decay
    g_last = jnp.sum(lgc, axis=-2, keepdims=True)
    return q * jnp.exp(g_c), k * jnp.exp(g_last - g_c), u, w, attn, g_last, extra


def _dn_prep_specs(n_cb):
    tok = n_cb * CHUNK
    wide = pl.BlockSpec((1, 1, tok, DN_HEAD_DIM), lambda h, b, j: (b, h, j, 0))
    col = pl.BlockSpec((1, 1, tok, 1), lambda h, b, j: (b, h, j, 0))
    rowv = pl.BlockSpec((1, 1, n_cb, 1, CHUNK), lambda h, b, j: (b, h, j, 0, 0))
    one = pl.BlockSpec((1, 1, n_cb, 1, 1), lambda h, b, j: (b, h, j, 0, 0))
    head = pl.BlockSpec((1, 1, 1), lambda h, b, j: (h, 0, 0))
    return wide, col, rowv, one, head


def _dn_prep_load(n_cb, q_ref, k_ref, v_ref, bl_ref, lac_ref, lar_ref, al_ref, dt_ref):
    wide = lambda r: r[0, 0].reshape(n_cb, CHUNK, DN_HEAD_DIM)
    colv = lambda r: r[0, 0].reshape(n_cb, CHUNK, 1)
    return (wide(q_ref), wide(k_ref), wide(v_ref), colv(bl_ref), colv(lac_ref), lar_ref[0, 0], al_ref[...], dt_ref[...])


def _dn_prep(qh, kh, vh, bl, lac, lar, a_log, dt_bias, *, name):
    n_b, n_h, n_s, _ = qh.shape
    n_cb = min(DN_PREP_CHUNKS, n_s // CHUNK)
    tok = n_cb * CHUNK
    wide, col, rowv, one, head = _dn_prep_specs(n_cb)

    def body(*refs):
        outs = _dn_prep_fn(_solve_by_doubling, *_dn_prep_load(n_cb, *refs[:8]))
        for ref, val in zip(refs[8:13], outs[:5]):
            ref[0, 0] = val.reshape(tok, DN_HEAD_DIM)
        refs[13][0, 0] = outs[5]
        refs[14][0, 0] = outs[6].reshape(tok, DN_HEAD_DIM)

    big = jax.ShapeDtypeStruct(qh.shape, F32)
    return pl.pallas_call(
        body, name=name, grid=(n_h, n_b, n_s // tok),
        in_specs=[wide, wide, wide, col, col, rowv, head, head],
        out_specs=(wide, wide, wide, wide, wide, one, wide),
        out_shape=(big, big, big, big, big, jax.ShapeDtypeStruct((n_b, n_h, n_s // CHUNK, 1, 1), F32), big),
        compiler_params=_params(11 * _nbytes((tok, LANES), F32), 48 * _nbytes((tok, LANES), F32)),
    )(*_hbm(qh, kh, vh, bl, lac, lar, a_log, dt_bias))


def _dn_prep_bwd(qh, kh, vh, bl, lac, lar, a_log, dt_bias, inv, u, w, cts, *, name):
    n_b, n_h, n_s, _ = qh.shape
    n_cb = min(DN_PREP_CHUNKS, n_s // CHUNK)
    tok = n_cb * CHUNK
    wide, col, rowv, one, head = _dn_prep_specs(n_cb)

    def body(*refs):
        prim = _dn_prep_load(n_cb, *refs[:8])
        chunks = lambda r: r[0, 0].reshape(n_cb, CHUNK, DN_HEAD_DIM)
        inv_v, u_v, w_v = chunks(refs[8]), chunks(refs[9]), chunks(refs[10])
        ct = tuple(chunks(r) for r in refs[11:16]) + (refs[16][0, 0],)

        def fn(*args):
            solve = lambda a, ru, rw: _solve_saved(a, ru, rw, inv_v, u_v, w_v) + (None,)
            return _dn_prep_fn(solve, *args)[:6]

        _, pull = jax.vjp(fn, *prim)
        dq, dk, dv, dbl, dlac, dlar, dal, ddt = pull(ct)
        outs = refs[17:]
        for ref, val in zip(outs[:3], (dq, dk, dv)):
            ref[0, 0] = val.reshape(tok, DN_HEAD_DIM)
        outs[3][0, 0] = dbl.reshape(tok, 1)
        outs[4][0, 0] = dlac.reshape(tok, 1)
        outs[5][0, 0] = dlar
        first = jnp.logical_and(pl.program_id(1) == 0, pl.program_id(2) == 0)
        _accumulate(outs[6], dal, first)
        _accumulate(outs[7], ddt, first)

    big = jax.ShapeDtypeStruct(qh.shape, F32)
    return pl.pallas_call(
        body, name=name, grid=(n_h, n_b, n_s // tok),
        in_specs=[wide, wide, wide, col, col, rowv, head, head, wide, wide, wide, wide, wide, wide, wide, wide, one],
        out_specs=(wide, wide, wide, col, col, rowv, head, head),
        out_shape=(big, big, big, jax.ShapeDtypeStruct(bl.shape, F32), jax.ShapeDtypeStruct(lac.shape, F32),
                   jax.ShapeDtypeStruct(lar.shape, F32), jax.ShapeDtypeStruct(a_log.shape, F32),
                   jax.ShapeDtypeStruct(dt_bias.shape, F32)),
        compiler_params=_params(21 * _nbytes((tok, LANES), F32), 64 * _nbytes((tok, LANES), F32)),
    )(*_hbm(qh, kh, vh, bl, lac, lar, a_log, dt_bias, inv, u, w, *cts))


def _dn_step(state, q, k, u, w, a, gl):
    v_new = u - _mdot(w, state, _BNN)
    o = _mdot(q, state, _BNN) + _mdot(a, v_new, _BNN)
    return state * jnp.exp(gl) + _mdot(k, v_new, _BTN), o


def _dn_scan_specs(n_cb, n_blocks, reverse):
    tok = n_cb * CHUNK
    jj = (lambda j: n_blocks - 1 - j) if reverse else (lambda j: j)
    wide = pl.BlockSpec((1, DN_SCAN_HEADS, tok, DN_HEAD_DIM), lambda b, h, j: (b, h, jj(j), 0))
    one = pl.BlockSpec((1, DN_SCAN_HEADS, n_cb, 1, 1), lambda b, h, j: (b, h, jj(j), 0, 0))
    st = pl.BlockSpec((1, DN_SCAN_HEADS, n_cb, DN_HEAD_DIM, DN_HEAD_DIM), lambda b, h, j: (b, h, jj(j), 0, 0))
    return wide, one, st


def _dn_scan(qd, kd, u, w, attn, g_last, *, name):
    n_b, n_h, n_s, _ = qd.shape
    n_cb = min(DN_PREP_CHUNKS, n_s // CHUNK)
    n_blocks = n_s // (n_cb * CHUNK)
    wide, one, st = _dn_scan_specs(n_cb, n_blocks, False)

    def body(qd_ref, kd_ref, u_ref, w_ref, a_ref, gl_ref, o_ref, st_ref, state_ref):
        @pl.when(pl.program_id(2) == 0)
        def _():
            state_ref[...] = jnp.zeros(state_ref.shape, F32)

        def step(n, state):
            rows = pl.ds(pl.multiple_of(n * CHUNK, CHUNK), CHUNK)
            st_ref[0, :, n] = state
            state, o = _dn_step(state, qd_ref[0, :, rows, :], kd_ref[0, :, rows, :], u_ref[0, :, rows, :],
                                w_ref[0, :, rows, :], a_ref[0, :, rows, :], gl_ref[0, :, n])
            o_ref[0, :, rows, :] = o
            return state

        state_ref[...] = lax.fori_loop(0, n_cb, step, state_ref[...])

    return pl.pallas_call(
        body, name=name, grid=(n_b, n_h // DN_SCAN_HEADS, n_blocks),
        in_specs=[wide, wide, wide, wide, wide, one], out_specs=(wide, st),
        out_shape=(jax.ShapeDtypeStruct(qd.shape, F32),
                   jax.ShapeDtypeStruct((n_b, n_h, n_s // CHUNK, DN_HEAD_DIM, DN_HEAD_DIM), F32)),
        scratch_shapes=[pltpu.VMEM((DN_SCAN_HEADS, DN_HEAD_DIM, DN_HEAD_DIM), F32)],
        compiler_params=_params(8 * _nbytes((DN_SCAN_HEADS, n_cb * CHUNK, LANES), F32), 8 << 20),
    )(*_hbm(qd, kd, u, w, attn, g_last))


def _dn_scan_bwd(qd, kd, u, w, attn, g_last, states, do, *, name):
    n_b, n_h, n_s, _ = qd.shape
    n_cb = min(DN_PREP_CHUNKS, n_s // CHUNK)
    n_blocks = n_s // (n_cb * CHUNK)
    wide, one, st = _dn_scan_specs(n_cb, n_blocks, True)

    def body(qd_ref, kd_ref, u_ref, w_ref, a_ref, gl_ref, st_ref, do_ref,
             dq_ref, dk_ref, du_ref, dw_ref, da_ref, dgl_ref, dstate_ref):
        @pl.when(pl.program_id(2) == 0)
        def _():
            dstate_ref[...] = jnp.zeros(dstate_ref.shape, F32)

        def step(i, dstate):
            n = n_cb - 1 - i
            rows = pl.ds(pl.multiple_of(n * CHUNK, CHUNK), CHUNK)
            _, pull = jax.vjp(_dn_step, st_ref[0, :, n], qd_ref[0, :, rows, :], kd_ref[0, :, rows, :],
                              u_ref[0, :, rows, :], w_ref[0, :, rows, :], a_ref[0, :, rows, :], gl_ref[0, :, n])
            dstate, dq, dk, du, dw, da, dgl = pull((dstate, do_ref[0, :, rows, :]))
            dq_ref[0, :, rows, :] = dq
            dk_ref[0, :, rows, :] = dk
            du_ref[0, :, rows, :] = du
            dw_ref[0, :, rows, :] = dw
            da_ref[0, :, rows, :] = da
            dgl_ref[0, :, n] = dgl
            return dstate

        dstate_ref[...] = lax.fori_loop(0, n_cb, step, dstate_ref[...])

    big = jax.ShapeDtypeStruct(qd.shape, F32)
    return pl.pallas_call(
        body, name=name, grid=(n_b, n_h // DN_SCAN_HEADS, n_blocks),
        in_specs=[wide, wide, wide, wide, wide, one, st, wide],
        out_specs=(wide, wide, wide, wide, wide, one),
        out_shape=(big, big, big, big, big, jax.ShapeDtypeStruct(g_last.shape, F32)),
        scratch_shapes=[pltpu.VMEM((DN_SCAN_HEADS, DN_HEAD_DIM, DN_HEAD_DIM), F32)],
        compiler_params=_params(13 * _nbytes((DN_SCAN_HEADS, n_cb * CHUNK, LANES), F32), 8 << 20),
    )(*_hbm(qd, kd, u, w, attn, g_last, states, do))


def _dn_post_fn(o, z, gain):
    return o * lax.rsqrt(jnp.mean(o * o, axis=-1, keepdims=True) + EPS) * gain * jax.nn.silu(z)


_HEAD_ROWS = lambda n_s: pl.BlockSpec((1, 1, n_s, DN_HEAD_DIM), lambda b, h: (b, h, 0, 0))
_HEAD_GAIN = pl.BlockSpec((1, DN_HEAD_DIM), lambda b, h: (0, 0))


def _dn_post(o, z, gain, *, name):
    n_b, n_h, n_s, _ = o.shape

    def body(o_ref, z_ref, g_ref, out_ref):
        out_ref[0, 0] = _dn_post_fn(o_ref[0, 0], z_ref[0, 0], g_ref[...]).astype(out_ref.dtype)

    rows = _HEAD_ROWS(n_s)
    return pl.pallas_call(
        body, name=name, grid=(n_b, n_h), in_specs=[rows, rows, _HEAD_GAIN], out_specs=rows,
        out_shape=jax.ShapeDtypeStruct(o.shape, MXU_DTYPE),
        compiler_params=_params(3 * _nbytes((n_s, LANES), F32), 6 * _nbytes((n_s, LANES), F32)),
    )(*_hbm(o, z, gain))


def _dn_post_bwd(o, z, gain, dout, *, name):
    n_b, n_h, n_s, _ = o.shape

    def body(o_ref, z_ref, g_ref, dout_ref, do_ref, dz_ref, dg_ref):
        _, pull = jax.vjp(_dn_post_fn, o_ref[0, 0], z_ref[0, 0], g_ref[...])
        do, dz, dg = pull(dout_ref[0, 0].astype(F32))
        do_ref[0, 0] = do
        dz_ref[0, 0] = dz.astype(dz_ref.dtype)
        _accumulate(dg_ref, dg, jnp.logical_and(pl.program_id(0) == 0, pl.program_id(1) == 0))

    rows = _HEAD_ROWS(n_s)
    return pl.pallas_call(
        body, name=name, grid=(n_b, n_h), in_specs=[rows, rows, _HEAD_GAIN, rows],
        out_specs=(rows, rows, _HEAD_GAIN),
        out_shape=(jax.ShapeDtypeStruct(o.shape, F32), jax.ShapeDtypeStruct(o.shape, MXU_DTYPE),
                   jax.ShapeDtypeStruct((1, DN_HEAD_DIM), F32)),
        compiler_params=_params(5 * _nbytes((n_s, LANES), F32), 10 * _nbytes((n_s, LANES), F32)),
    )(*_hbm(o, z, gain, dout))


S5_SCAN_LANES = 256
TILE_ROWS = SUBLANES


def _s5_prep_fn(lam_re, lam_im, log_step, bt_re, bt_im, c_im):
    lr = jnp.minimum(lam_re, -1e-4)
    step = jnp.exp(log_step)
    mag = jnp.exp(lr * step)
    ang = lam_im * step
    lb_re = mag * jnp.cos(ang)
    lb_im = mag * jnp.sin(ang)
    den = lr * lr + lam_im * lam_im
    coef_re = ((lb_re - 1.0) * lr + lb_im * lam_im) / den
    coef_im = (lb_im * lr - (lb_re - 1.0) * lam_im) / den
    return (lb_re, lb_im, coef_re * bt_re - coef_im * bt_im, coef_re * bt_im + coef_im * bt_re, -c_im)


def _s5_prep(lam_re, lam_im, log_step, bt_re, bt_im, c_im, *, name):
    def body(*refs):
        outs = _s5_prep_fn(*(r[...] for r in refs[:6]))
        for ref, val in zip(refs[6:], outs):
            ref[...] = val

    vec = jax.ShapeDtypeStruct(lam_re.shape, F32)
    mat = jax.ShapeDtypeStruct(bt_re.shape, F32)
    return pl.pallas_call(body, name=name, out_shape=(vec, vec, mat, mat, mat))(lam_re, lam_im, log_step, bt_re, bt_im, c_im)


def _s5_prep_bwd(lam_re, lam_im, log_step, bt_re, bt_im, c_im, cts, *, name):
    def body(*refs):
        _, pull = jax.vjp(_s5_prep_fn, *(r[...] for r in refs[:6]))
        grads = pull(tuple(r[...] for r in refs[6:11]))
        for ref, val in zip(refs[11:], grads):
            ref[...] = val

    shapes = tuple(jax.ShapeDtypeStruct(a.shape, F32) for a in (lam_re, lam_im, log_step, bt_re, bt_im, c_im))
    return pl.pallas_call(body, name=name, out_shape=shapes)(lam_re, lam_im, log_step, bt_re, bt_im, c_im, *cts)


def _cmul(ar, ai, br, bi):
    return ar * br - ai * bi, ar * bi + ai * br


def _s5_powers(lr, li):
    pows = [(lr, li)]
    for _ in range(TILE_ROWS - 1):
        pows.append(_cmul(pows[-1][0], pows[-1][1], lr, li))
    return pows


def _s5_carry_table(pows, n_lanes, reverse):
    row = lax.broadcasted_iota(jnp.int32, (TILE_ROWS, n_lanes), 0)
    t_re = jnp.zeros((TILE_ROWS, n_lanes), F32)
    t_im = jnp.zeros((TILE_ROWS, n_lanes), F32)
    for r in range(TILE_ROWS):
        p_re, p_im = pows[TILE_ROWS - 1 - r] if reverse else pows[r]
        t_re = jnp.where(row == r, p_re, t_re)
        t_im = jnp.where(row == r, p_im, t_im)
    return t_re, t_im


def _s5_tile(y_re, y_im, pows, reverse):
    d = 1
    while d < TILE_ROWS:
        p_re, p_im = pows[d - 1]
        if reverse:
            s_re, s_im = _shift_up(y_re, d), _shift_up(y_im, d)
        else:
            s_re, s_im = _shift_down(y_re, d), _shift_down(y_im, d)
        m_re, m_im = _cmul(p_re, p_im, s_re, s_im)
        y_re, y_im = y_re + m_re, y_im + m_im
        d *= 2
    return y_re, y_im


S5_BLOCKS = N_SHARD
S5_BLOCK_CH = S5_WIDTH // S5_BLOCKS
S5_BLOCK_LANES = S5_LANES // S5_BLOCKS
SCAN_PER_BLOCK = S5_BLOCK_LANES // S5_SCAN_LANES


def _s5_scan_specs(n_s, order):
    L = S5_SCAN_LANES

    def cat_spec(part):
        return pl.BlockSpec((1, 1, n_s, L), lambda *g: (order(*g)[1] // SCAN_PER_BLOCK, order(*g)[0], 0,
                                                        part * SCAN_PER_BLOCK + order(*g)[1] % SCAN_PER_BLOCK))

    one = pl.BlockSpec((1, 1, n_s, L), lambda *g: (order(*g)[1] // SCAN_PER_BLOCK, order(*g)[0], 0,
                                                   order(*g)[1] % SCAN_PER_BLOCK))
    lam = pl.BlockSpec((1, L), lambda *g: (0, order(*g)[1]))
    return cat_spec, one, lam


def _s5_scan(bu, lb_re, lb_im, *, name):
    n_blk, n_b, n_s, _ = bu.shape
    n_lb = S5_LANES // S5_SCAN_LANES
    n_tiles = n_s // TILE_ROWS
    L = S5_SCAN_LANES

    def body(re_ref, im_ref, lr_ref, li_ref, xr_ref, xi_ref):
        pows = _s5_powers(lr_ref[...], li_ref[...])
        t_re, t_im = _s5_carry_table(pows, L, False)

        def step(i, carry):
            rows = pl.ds(pl.multiple_of(i * TILE_ROWS, TILE_ROWS), TILE_ROWS)
            y_re, y_im = _s5_tile(re_ref[0, 0, rows, :], im_ref[0, 0, rows, :], pows, False)
            c_re, c_im = _cmul(t_re, t_im, carry[0], carry[1])
            y_re, y_im = y_re + c_re, y_im + c_im
            xr_ref[0, 0, rows, :] = y_re
            xi_ref[0, 0, rows, :] = y_im
            return y_re[TILE_ROWS - 1:, :], y_im[TILE_ROWS - 1:, :]

        zero = jnp.zeros((1, L), F32)
        lax.fori_loop(0, n_tiles, step, (zero, zero))

    cat_spec, one, lam = _s5_scan_specs(n_s, lambda b, j: (b, j))
    x_shape = jax.ShapeDtypeStruct((n_blk, n_b, n_s, S5_BLOCK_LANES), F32)
    return pl.pallas_call(
        body, name=name, grid=(n_b, n_lb),
        in_specs=[cat_spec(0), cat_spec(1), lam, lam],
        out_specs=(one, one), out_shape=(x_shape, x_shape),
        compiler_params=_params(4 * _nbytes((n_s, L), F32), 4 << 20),
    )(*_hbm(bu, bu, lb_re, lb_im))


def _s5_scan_bwd(dx, x_re, x_im, lb_re, lb_im, *, name):
    n_blk, n_b, n_s, _ = dx.shape
    n_lb = S5_LANES // S5_SCAN_LANES
    n_tiles = n_s // TILE_ROWS
    L = S5_SCAN_LANES

    def body(dr_ref, di_ref, xr_ref, xi_ref, lr_ref, li_ref, ar_ref, ai_ref, dlr_ref, dli_ref):
        pows = _s5_powers(lr_ref[...], -li_ref[...])
        t_re, t_im = _s5_carry_table(pows, L, True)
        row = lax.broadcasted_iota(jnp.int32, (TILE_ROWS, L), 0)

        def step(k, carry):
            c_re, c_im, s_re, s_im = carry
            i = n_tiles - 1 - k
            rows = pl.ds(pl.multiple_of(i * TILE_ROWS, TILE_ROWS), TILE_ROWS)
            a_re, a_im = _s5_tile(dr_ref[0, 0, rows, :], di_ref[0, 0, rows, :], pows, True)
            m_re, m_im = _cmul(t_re, t_im, c_re, c_im)
            a_re, a_im = a_re + m_re, a_im + m_im
            ar_ref[0, 0, rows, :] = a_re.astype(ar_ref.dtype)
            ai_ref[0, 0, rows, :] = a_im.astype(ai_ref.dtype)
            prev = pl.ds(pl.multiple_of(jnp.maximum(i - 1, 0) * TILE_ROWS, TILE_ROWS), TILE_ROWS)
            keep = jnp.where(i > 0, 1.0, 0.0)
            last_re = xr_ref[0, 0, prev, :][TILE_ROWS - 1:, :] * keep
            last_im = xi_ref[0, 0, prev, :][TILE_ROWS - 1:, :] * keep
            xp_re = jnp.where(row == 0, last_re, _shift_down(xr_ref[0, 0, rows, :], 1))
            xp_im = jnp.where(row == 0, last_im, _shift_down(xi_ref[0, 0, rows, :], 1))
            s_re = s_re + a_re * xp_re + a_im * xp_im
            s_im = s_im + a_im * xp_re - a_re * xp_im
            return a_re[:1, :], a_im[:1, :], s_re, s_im

        zero = jnp.zeros((1, L), F32)
        zt = jnp.zeros((TILE_ROWS, L), F32)
        _, _, s_re, s_im = lax.fori_loop(0, n_tiles, step, (zero, zero, zt, zt))
        first = pl.program_id(1) == 0
        _accumulate(dlr_ref, jnp.sum(s_re, axis=0, keepdims=True), first)
        _accumulate(dli_ref, jnp.sum(s_im, axis=0, keepdims=True), first)

    cat_spec, one, lam = _s5_scan_specs(n_s, lambda j, b: (b, j))
    a_shape = jax.ShapeDtypeStruct((n_blk, n_b, n_s, S5_BLOCK_LANES), MXU_DTYPE)
    lam_shape = jax.ShapeDtypeStruct((1, S5_LANES), F32)
    return pl.pallas_call(
        body, name=name, grid=(n_lb, n_b),
        in_specs=[cat_spec(0), cat_spec(1), one, one, lam, lam],
        out_specs=(one, one, lam, lam),
        out_shape=(a_shape, a_shape, lam_shape, lam_shape),
        compiler_params=_params(5 * _nbytes((n_s, L), F32), 4 << 20),
    )(*_hbm(dx, dx, x_re, x_im, lb_re, lb_im))


def _scan_rows(i):
    return pl.ds(pl.multiple_of(i * TILE_ROWS, TILE_ROWS), TILE_ROWS)


def _s5_mix_specs(n_s, order):
    jb = lambda *g: order(*g)[0]
    bb = lambda *g: order(*g)[1]
    act = pl.BlockSpec((1, 1, n_s, S5_BLOCK_CH), lambda *g: (jb(*g), bb(*g), 0, 0))
    state = pl.BlockSpec((1, 1, n_s, S5_BLOCK_LANES), lambda *g: (jb(*g), bb(*g), 0, 0))
    lam = pl.BlockSpec((1, S5_BLOCK_LANES), lambda *g: (0, jb(*g)))
    w_in = pl.BlockSpec((1, S5_BLOCK_CH, S5_BLOCK_LANES), lambda *g: (jb(*g), 0, 0))
    w_out = pl.BlockSpec((1, S5_BLOCK_LANES, S5_BLOCK_CH), lambda *g: (jb(*g), 0, 0))
    return act, state, lam, w_in, w_out


def _s5_mix(u, wb_re, wb_im, lb_re, lb_im, wc_re, wc_im, *, name):
    n_blk, n_b, n_s, _ = u.shape
    n_tiles = n_s // TILE_ROWS
    L = S5_BLOCK_LANES

    def body(u_ref, wbr_ref, wbi_ref, lr_ref, li_ref, wcr_ref, wci_ref, y_ref, xr_ref, xi_ref):
        uv = u_ref[0, 0].astype(MXU_DTYPE)
        xr_ref[0, 0] = lax.dot_general(uv, wbr_ref[0].astype(MXU_DTYPE), _NN, preferred_element_type=F32)
        xi_ref[0, 0] = lax.dot_general(uv, wbi_ref[0].astype(MXU_DTYPE), _NN, preferred_element_type=F32)
        pows = _s5_powers(lr_ref[...], li_ref[...])
        t_re, t_im = _s5_carry_table(pows, L, False)

        def step(i, carry):
            rows = _scan_rows(i)
            y_re, y_im = _s5_tile(xr_ref[0, 0, rows, :], xi_ref[0, 0, rows, :], pows, False)
            c_re, c_im = _cmul(t_re, t_im, carry[0], carry[1])
            y_re, y_im = y_re + c_re, y_im + c_im
            xr_ref[0, 0, rows, :] = y_re
            xi_ref[0, 0, rows, :] = y_im
            return y_re[TILE_ROWS - 1:, :], y_im[TILE_ROWS - 1:, :]

        zero = jnp.zeros((1, L), F32)
        lax.fori_loop(0, n_tiles, step, (zero, zero))
        y_ref[0, 0] = (
            lax.dot_general(xr_ref[0, 0].astype(MXU_DTYPE), wcr_ref[0].astype(MXU_DTYPE), _NN, preferred_element_type=F32)
            + lax.dot_general(xi_ref[0, 0].astype(MXU_DTYPE), wci_ref[0].astype(MXU_DTYPE), _NN, preferred_element_type=F32))

    act, state, lam, w_in, w_out = _s5_mix_specs(n_s, lambda b, j: (j, b))
    x_shape = jax.ShapeDtypeStruct((n_blk, n_b, n_s, L), F32)
    return pl.pallas_call(
        body, name=name, grid=(n_b, n_blk),
        in_specs=[act, w_in, w_in, lam, lam, w_out, w_out], out_specs=(act, state, state),
        out_shape=(jax.ShapeDtypeStruct(u.shape, F32), x_shape, x_shape),
        compiler_params=_params(2 * _nbytes((n_s, L), F32) + 2 * _nbytes((n_s, S5_BLOCK_CH), F32), 3 * _nbytes((n_s, L), F32)),
    )(*_hbm(u, wb_re, wb_im, lb_re, lb_im, wc_re, wc_im))


def _s5_mix_bwd(dy, du_skip, u, x_re, x_im, wb_re, wb_im, lb_re, lb_im, wc_re, wc_im, *, name):
    n_blk, n_b, n_s, _ = u.shape
    n_tiles = n_s // TILE_ROWS
    L = S5_BLOCK_LANES

    def body(dy_ref, ds_ref, u_ref, xr_ref, xi_ref, wbr_ref, wbi_ref, lr_ref, li_ref, wcr_ref, wci_ref,
             du_ref, dwbr_ref, dwbi_ref, dlr_ref, dli_ref, dwcr_ref, dwci_ref, ar_ref, ai_ref):
        dyv = dy_ref[0, 0].astype(MXU_DTYPE)
        ar_ref[...] = lax.dot_general(dyv, wcr_ref[0].astype(MXU_DTYPE), _NT, preferred_element_type=F32)
        ai_ref[...] = lax.dot_general(dyv, wci_ref[0].astype(MXU_DTYPE), _NT, preferred_element_type=F32)
        pows = _s5_powers(lr_ref[...], -li_ref[...])
        t_re, t_im = _s5_carry_table(pows, L, True)
        row = lax.broadcasted_iota(jnp.int32, (TILE_ROWS, L), 0)

        def step(k, carry):
            c_re, c_im, s_re, s_im = carry
            i = n_tiles - 1 - k
            rows = _scan_rows(i)
            a_re, a_im = _s5_tile(ar_ref[rows, :], ai_ref[rows, :], pows, True)
            m_re, m_im = _cmul(t_re, t_im, c_re, c_im)
            a_re, a_im = a_re + m_re, a_im + m_im
            ar_ref[rows, :] = a_re
            ai_ref[rows, :] = a_im
            prev = _scan_rows(jnp.maximum(i - 1, 0))
            keep = jnp.where(i > 0, 1.0, 0.0)
            last_re = xr_ref[0, 0, prev, :][TILE_ROWS - 1:, :] * keep
            last_im = xi_ref[0, 0, prev, :][TILE_ROWS - 1:, :] * keep
            xp_re = jnp.where(row == 0, last_re, _shift_down(xr_ref[0, 0, rows, :], 1))
            xp_im = jnp.where(row == 0, last_im, _shift_down(xi_ref[0, 0, rows, :], 1))
            s_re = s_re + a_re * xp_re + a_im * xp_im
            s_im = s_im + a_im * xp_re - a_re * xp_im
            return a_re[:1, :], a_im[:1, :], s_re, s_im

        zero = jnp.zeros((1, L), F32)
        zt = jnp.zeros((TILE_ROWS, L), F32)
        _, _, s_re, s_im = lax.fori_loop(0, n_tiles, step, (zero, zero, zt, zt))
        first = pl.program_id(1) == 0
        _accumulate(dlr_ref, jnp.sum(s_re, axis=0, keepdims=True), first)
        _accumulate(dli_ref, jnp.sum(s_im, axis=0, keepdims=True), first)
        a_re, a_im = ar_ref[...].astype(MXU_DTYPE), ai_ref[...].astype(MXU_DTYPE)
        du = (lax.dot_general(a_re, wbr_ref[0].astype(MXU_DTYPE), _NT, preferred_element_type=F32)
              + lax.dot_general(a_im, wbi_ref[0].astype(MXU_DTYPE), _NT, preferred_element_type=F32))
        du_ref[0, 0] = (du + ds_ref[0, 0]).astype(du_ref.dtype)
        uv = u_ref[0, 0].astype(MXU_DTYPE)
        _accumulate(dwbr_ref, lax.dot_general(uv, a_re, _TN, preferred_element_type=F32)[None], first)
        _accumulate(dwbi_ref, lax.dot_general(uv, a_im, _TN, preferred_element_type=F32)[None], first)
        _accumulate(dwcr_ref, lax.dot_general(xr_ref[0, 0].astype(MXU_DTYPE), dyv, _TN, preferred_element_type=F32)[None], first)
        _accumulate(dwci_ref, lax.dot_general(xi_ref[0, 0].astype(MXU_DTYPE), dyv, _TN, preferred_element_type=F32)[None], first)

    act, state, lam, w_in, w_out = _s5_mix_specs(n_s, lambda j, b: (j, b))
    lam_shape = jax.ShapeDtypeStruct((1, S5_LANES), F32)
    return pl.pallas_call(
        body, name=name, grid=(n_blk, n_b),
        in_specs=[act, act, act, state, state, w_in, w_in, lam, lam, w_out, w_out],
        out_specs=(act, w_in, w_in, lam, lam, w_out, w_out),
        out_shape=(jax.ShapeDtypeStruct(u.shape, MXU_DTYPE), jax.ShapeDtypeStruct(wb_re.shape, F32),
                   jax.ShapeDtypeStruct(wb_im.shape, F32), lam_shape, lam_shape,
                   jax.ShapeDtypeStruct(wc_re.shape, F32), jax.ShapeDtypeStruct(wc_im.shape, F32)),
        scratch_shapes=[pltpu.VMEM((n_s, L), F32), pltpu.VMEM((n_s, L), F32)],
        compiler_params=_params(2 * _nbytes((n_s, L), F32) + 4 * _nbytes((n_s, S5_BLOCK_CH), F32), 5 * _nbytes((n_s, L), F32)),
    )(*_hbm(dy, du_skip, u, x_re, x_im, wb_re, wb_im, lb_re, lb_im, wc_re, wc_im))


def _s5_out_fn(ymm, u, d_skip, w_glu, b_glu):
    y = jax.nn.gelu(ymm + d_skip * u)
    return y * jax.nn.sigmoid(_mdot(y, w_glu, _NN) + b_glu)


def _s5_out_specs(tm):
    rows = pl.BlockSpec((S5_BLOCKS, tm, S5_BLOCK_CH), lambda i: (0, i, 0))
    flat_rows = pl.BlockSpec((tm, S5_WIDTH), lambda i: (i, 0))
    vec = pl.BlockSpec((1, S5_WIDTH), lambda i: (0, 0))
    mat = pl.BlockSpec((S5_WIDTH, S5_WIDTH), lambda i: (0, 0))
    return rows, flat_rows, vec, mat


def _blocks_to_lanes(ref):
    return jnp.concatenate([ref[k] for k in range(S5_BLOCKS)], axis=-1)


def _lanes_to_blocks(ref, val):
    for k in range(S5_BLOCKS):
        ref[k] = val[:, k * S5_BLOCK_CH:(k + 1) * S5_BLOCK_CH].astype(ref.dtype)


def _s5_out(ymm, u, d_skip, w_glu, b_glu, *, name, tm=512):
    n_tok = ymm.shape[1]
    tm = min(tm, n_tok)
    rows, flat_rows, vec, mat = _s5_out_specs(tm)

    def body(y_ref, u_ref, d_ref, w_ref, b_ref, o_ref):
        out = _s5_out_fn(_blocks_to_lanes(y_ref), _blocks_to_lanes(u_ref), d_ref[...], w_ref[...], b_ref[...])
        o_ref[...] = out.astype(o_ref.dtype)

    return pl.pallas_call(
        body, name=name, grid=(n_tok // tm,), in_specs=[rows, rows, vec, mat, vec], out_specs=flat_rows,
        out_shape=jax.ShapeDtypeStruct((n_tok, S5_WIDTH), MXU_DTYPE),
        compiler_params=_params(4 * _nbytes((tm, S5_WIDTH), F32), 8 * _nbytes((tm, S5_WIDTH), F32)),
    )(*_hbm(ymm, u, d_skip, w_glu, b_glu))


def _s5_out_bwd(ymm, u, d_skip, w_glu, b_glu, dout, *, name, tm=512):
    n_tok = ymm.shape[1]
    tm = min(tm, n_tok)
    rows, flat_rows, vec, mat = _s5_out_specs(tm)

    def body(y_ref, u_ref, d_ref, w_ref, b_ref, do_ref, dy_ref, du_ref, dd_ref, dw_ref, db_ref):
        _, pull = jax.vjp(_s5_out_fn, _blocks_to_lanes(y_ref), _blocks_to_lanes(u_ref), d_ref[...],
                          w_ref[...].astype(F32), b_ref[...])
        dy, du, dd, dw, db = pull(do_ref[...])
        _lanes_to_blocks(dy_ref, dy)
        _lanes_to_blocks(du_ref, du)
        first = pl.program_id(0) == 0
        _accumulate(dd_ref, dd, first)
        _accumulate(dw_ref, dw, first)
        _accumulate(db_ref, db, first)

    return pl.pallas_call(
        body, name=name, grid=(n_tok // tm,), in_specs=[rows, rows, vec, mat, vec, flat_rows],
        out_specs=(rows, rows, vec, mat, vec),
        out_shape=(jax.ShapeDtypeStruct(ymm.shape, MXU_DTYPE), jax.ShapeDtypeStruct(ymm.shape, F32),
                   jax.ShapeDtypeStruct((1, S5_WIDTH), F32), jax.ShapeDtypeStruct((S5_WIDTH, S5_WIDTH), F32),
                   jax.ShapeDtypeStruct((1, S5_WIDTH), F32)),
        compiler_params=_params(6 * _nbytes((tm, S5_WIDTH), F32), 12 * _nbytes((tm, S5_WIDTH), F32)),
    )(*_hbm(ymm, u, d_skip, w_glu, b_glu, dout))


def _merge_fn(ga, gb, ya, yb):
    return jax.nn.sigmoid(ga) * ya + jax.nn.sigmoid(gb) * yb


def _merge(gab, ya, yb, *, name, tm=512):
    n_tok = ya.shape[0]
    tm = min(tm, n_tok)
    rows = pl.BlockSpec((tm, D_MODEL), lambda i: (i, 0))

    def body(ga_ref, gb_ref, ya_ref, yb_ref, o_ref):
        o_ref[...] = _merge_fn(ga_ref[...], gb_ref[...], ya_ref[...], yb_ref[...]).astype(o_ref.dtype)

    return pl.pallas_call(
        body, name=name, grid=(n_tok // tm,),
        in_specs=[rows, pl.BlockSpec((tm, D_MODEL), lambda i: (i, 1)), rows, rows], out_specs=rows,
        out_shape=jax.ShapeDtypeStruct(ya.shape, MXU_DTYPE),
        compiler_params=_params(5 * _nbytes((tm, D_MODEL), F32), 4 * _nbytes((tm, D_MODEL), F32)),
    )(*_hbm(gab, gab, ya, yb))


def _merge_bwd(gab, ya, yb, dout, *, name, tm=512):
    n_tok = ya.shape[0]
    tm = min(tm, n_tok)
    rows = pl.BlockSpec((tm, D_MODEL), lambda i: (i, 0))

    def body(ga_ref, gb_ref, ya_ref, yb_ref, do_ref, *out_refs):
        _, pull = jax.vjp(_merge_fn, ga_ref[...], gb_ref[...], ya_ref[...], yb_ref[...])
        for ref, val in zip(out_refs, pull(do_ref[...])):
            ref[...] = val.astype(ref.dtype)

    shape = jax.ShapeDtypeStruct(ya.shape, MXU_DTYPE)
    return pl.pallas_call(
        body, name=name, grid=(n_tok // tm,),
        in_specs=[rows, pl.BlockSpec((tm, D_MODEL), lambda i: (i, 1)), rows, rows, rows],
        out_specs=(rows, rows, rows, rows), out_shape=(shape, shape, shape, shape),
        compiler_params=_params(7 * _nbytes((tm, D_MODEL), F32), 6 * _nbytes((tm, D_MODEL), F32)),
    )(*_hbm(gab, gab, ya, yb, dout))


ADA_SHARD = N_MOD * D_MODEL // N_SHARD


def _ada_fwd(c_pad, w_s, b_s, *, name):
    n_r = c_pad.shape[0]

    def body(c_ref, w_ref, b_ref, o_ref):
        sc = jax.nn.silu(c_ref[...]).astype(MXU_DTYPE)
        o_ref[0] = lax.dot_general(sc, w_ref[0].astype(MXU_DTYPE), _NN, preferred_element_type=F32) + b_ref[0]

    return pl.pallas_call(
        body, name=name, grid=(N_SHARD,),
        in_specs=[pl.BlockSpec((n_r, D_MODEL), lambda s: (0, 0)),
                  pl.BlockSpec((1, D_MODEL, ADA_SHARD), lambda s: (s, 0, 0)),
                  pl.BlockSpec((1, 1, ADA_SHARD), lambda s: (s, 0, 0))],
        out_specs=pl.BlockSpec((1, n_r, ADA_SHARD), lambda s: (s, 0, 0)),
        out_shape=jax.ShapeDtypeStruct((N_SHARD, n_r, ADA_SHARD), F32),
        compiler_params=_params(_nbytes((D_MODEL, ADA_SHARD), w_s.dtype), 1 << 20),
    )(*_hbm(c_pad, w_s, b_s))


def _ada_bwd(c_pad, dmod_s, *, name):
    n_r = c_pad.shape[0]

    def body(c_ref, d_ref, dw_ref, db_ref):
        sc = jax.nn.silu(c_ref[...])
        dm = d_ref[0]
        dw_ref[0] = _fdot(sc, dm, _TN)
        db_ref[0] = jnp.sum(dm, axis=0, keepdims=True)

    return pl.pallas_call(
        body, name=name, grid=(N_SHARD,),
        in_specs=[pl.BlockSpec((n_r, D_MODEL), lambda s: (0, 0)), pl.BlockSpec((1, n_r, ADA_SHARD), lambda s: (s, 0, 0))],
        out_specs=(pl.BlockSpec((1, D_MODEL, ADA_SHARD), lambda s: (s, 0, 0)),
                   pl.BlockSpec((1, 1, ADA_SHARD), lambda s: (s, 0, 0))),
        out_shape=(jax.ShapeDtypeStruct((N_SHARD, D_MODEL, ADA_SHARD), F32),
                   jax.ShapeDtypeStruct((N_SHARD, 1, ADA_SHARD), F32)),
        compiler_params=_params(_nbytes((D_MODEL, ADA_SHARD), F32), 2 * _nbytes((D_MODEL, ADA_SHARD), F32)),
    )(*_hbm(c_pad, dmod_s))


def _heads(t, n_b, n_s):
    return t.reshape(n_b, n_s, DN_HEADS, DN_HEAD_DIM).transpose(0, 2, 1, 3)


def _unheads(t):
    n_b, _, n_s, _ = t.shape
    return t.transpose(0, 2, 1, 3).reshape(n_b, n_s, DN_WIDTH)


def _block_diag(blocks):
    n_per = S5_GROUPS // S5_BLOCKS
    _, n_r, n_c = blocks.shape
    b4 = blocks.reshape(S5_BLOCKS, n_per, n_r, n_c)
    eye = jnp.eye(n_per, dtype=blocks.dtype)
    return (b4[:, :, :, None, :] * eye[None, :, None, :, None]).reshape(S5_BLOCKS, n_per * n_r, n_per * n_c)


def _diag_blocks(mat, n_r, n_c):
    n_per = S5_GROUPS // S5_BLOCKS
    m5 = mat.reshape(S5_BLOCKS, n_per, n_r, n_per, n_c)
    eye = jnp.eye(n_per, dtype=mat.dtype)
    return jnp.sum(m5 * eye[None, :, None, :, None], axis=3).reshape(S5_GROUPS, n_r, n_c)


def _local_step(x, c, target, wts):
    n_b, n_s, _ = x.shape
    n_tok = n_b * n_s
    flat = lambda t: t.reshape(n_tok, t.shape[-1])
    unflat = lambda t: t.reshape(n_b, n_s, t.shape[-1])
    n_chunks = n_s // CHUNK

    c_pad = jnp.zeros((SUBLANES, D_MODEL), F32).at[:n_b].set(c)
    mod_s = _ada_fwd(c_pad, wts["w_ada"], wts["b_ada"], name="ada_fwd")
    mod = mod_s.transpose(1, 0, 2).reshape(SUBLANES, N_MOD * D_MODEL)[:n_b]
    sh1, sc1, gt1, sh2, sc2, gt2, sh3, sc3, gt3 = [m[:, None, :] for m in jnp.split(mod, N_MOD, axis=-1)]

    a1 = _pre(x, None, None, wts["g_ffn1"], sh1, sc1, 0.0, name="pre1")
    f1, ffn1_saved = _ffn_fwd(flat(a1), wts["w1_ffn1"], wts["w3_ffn1"], wts["w2_ffn1"], "ffn1")
    x1, a2 = _pre(x, unflat(f1), gt1, wts["g_mix"], sh2, sc2, 0.5, name="pre2")
    u = flat(a2)[None]
    p_qkv = _mm_act([(u, wts["w_qkv"])], "nn", name="in_qkv")[0]
    p_z = _mm_act([(u, wts["w_z"])], "nn", name="in_z")[0]
    p_gab = _mm_act([(u, wts["w_gab"])], "nn", name="in_gab")[0]
    p_s5 = _mm_act([(u, wts["w_s5"])], "nn", out_sharded=True, name="in_s5")
    p_ba = _mm_act([(u, wts["w_ba"])], "nn", name="in_ba")[0]

    qkv_c = _conv_fwd(unflat(p_qkv), wts["conv_qkv"], name="conv_fwd")
    qh, kh, vh = [_heads(t, n_b, n_s) for t in jnp.split(qkv_c, 3, axis=-1)]
    zh = _heads(p_z, n_b, n_s)
    ba = p_ba.reshape(n_b, n_s, BA_PAD)
    bl = ba[:, :, :DN_HEADS].transpose(0, 2, 1)[..., None]
    lac = ba[:, :, DN_HEADS:2 * DN_HEADS].transpose(0, 2, 1)[..., None]
    lar = lac.reshape(n_b, DN_HEADS, n_chunks, 1, CHUNK)
    a_log, dt_bias = wts["a_log"], wts["dt_bias"]
    dn_in = (qh, kh, vh, bl, lac, lar, a_log, dt_bias)
    qd, kd, uc, wc, attn, g_last, dn_inv = _dn_prep(*dn_in, name="dn_prep")
    o, states = _dn_scan(qd, kd, uc, wc, attn, g_last, name="dn_scan")
    og = _dn_post(o, zh, wts["g_onorm"], name="dn_post")
    og_t = _unheads(og).reshape(1, n_tok, DN_WIDTH)
    ya = _mm_act([(og_t, wts["w_proj_a"])], "nn", name="proj_a")[0]

    s5p_in = (wts["lam_re"], wts["lam_im"], wts["log_step"], wts["bt_re"], wts["bt_im"], wts["c_im"])
    lb_re, lb_im, bb_re, bb_im, c_neg = _s5_prep(*s5p_in, name="s5_prep")
    wb_re, wb_im = _block_diag(bb_re), _block_diag(bb_im)
    wc_re = _block_diag(wts["c_re"].transpose(0, 2, 1))
    wc_im = _block_diag(c_neg.transpose(0, 2, 1))
    lbr, lbi = lb_re.reshape(1, S5_LANES), lb_im.reshape(1, S5_LANES)
    s5_blocks = lambda t: t.reshape(S5_BLOCKS, n_b, n_s, S5_BLOCK_CH)
    s5_w = (wb_re, wb_im, lbr, lbi, wc_re, wc_im)
    ymm, x_re, x_im = _s5_mix(s5_blocks(p_s5), *s5_w, name="s5_mix")
    ymm = ymm.reshape(S5_BLOCKS, n_tok, S5_BLOCK_CH)
    y2 = _s5_out(ymm, p_s5, wts["d_skip"], wts["w_glu"], wts["b_glu"], name="s5_out")
    yb = _mm_act([(y2[None], wts["w_proj_b"])], "nn", name="proj_b")[0]

    merged = _merge(p_gab, ya, yb, name="merge")
    m_out = _mm_act([(merged[None], wts["w_out"])], "nn", name="mix_out")[0]
    x2, a3 = _pre(x1, unflat(m_out), gt2, wts["g_ffn2"], sh3, sc3, 1.0, name="pre3")
    f3, ffn2_saved = _ffn_fwd(flat(a3), wts["w1_ffn2"], wts["w3_ffn2"], wts["w2_ffn2"], "ffn2")

    g = {}
    loss, dx2_res, df3, dgt3, g["g_final"] = _final(x2, unflat(f3), gt3, wts["g_final"], target, name="final")
    da3, g["w1_ffn2"], g["w3_ffn2"], g["w2_ffn2"] = _ffn_bwd(
        flat(a3), wts["w1_ffn2"], wts["w3_ffn2"], wts["w2_ffn2"], ffn2_saved, flat(df3), "ffn2")
    dx1_res, dm_out, dgt2, g["g_ffn2"], dsh3, dsc3 = _pre_bwd(
        x1, unflat(m_out), gt2, wts["g_ffn2"], sh3, sc3, 1.0, unflat(da3), dx2_res, name="pre3_bwd")
    dm_out = flat(dm_out)[None]
    dmerged = _mm_act([(dm_out, wts["w_out"])], "nt", name="mix_out_bwd")[0]
    g["w_out"] = _mm_tn(merged[None], dm_out, name="dw_out")[0]
    dga, dgb, dya, dyb = _merge_bwd(p_gab, ya, yb, dmerged, name="merge_bwd")

    dy2 = _mm_act([(dyb[None], wts["w_proj_b"])], "nt", name="proj_b_bwd")[0]
    g["w_proj_b"] = _mm_tn(y2[None], dyb[None], name="dw_proj_b")[0]
    dymm, du_skip, g["d_skip"], g["w_glu"], g["b_glu"] = _s5_out_bwd(
        ymm, p_s5, wts["d_skip"], wts["w_glu"], wts["b_glu"], dy2, name="s5_out_bwd")
    dp_s5, dwb_re, dwb_im, dlb_re, dlb_im, dwc_re, dwc_im = _s5_mix_bwd(
        s5_blocks(dymm), s5_blocks(du_skip), s5_blocks(p_s5), x_re, x_im, *s5_w, name="s5_mix_bwd")
    dp_s5 = dp_s5.reshape(S5_BLOCKS, n_tok, S5_BLOCK_CH)
    g["c_re"] = _diag_blocks(dwc_re, S5_STATE, S5_GROUP_CH).transpose(0, 2, 1)
    s5_cts = (dlb_re.reshape(lb_re.shape), dlb_im.reshape(lb_im.shape),
              _diag_blocks(dwb_re, S5_GROUP_CH, S5_STATE), _diag_blocks(dwb_im, S5_GROUP_CH, S5_STATE),
              _diag_blocks(dwc_im, S5_STATE, S5_GROUP_CH).transpose(0, 2, 1))
    g["lam_re"], g["lam_im"], g["log_step"], g["bt_re"], g["bt_im"], g["c_im"] = _s5_prep_bwd(
        *s5p_in, s5_cts, name="s5_prep_bwd")

    dog = _mm_act([(dya[None], wts["w_proj_a"])], "nt", name="proj_a_bwd")[0]
    g["w_proj_a"] = _mm_tn(og_t, dya[None], name="dw_proj_a")[0]
    do, dzh, g["g_onorm"] = _dn_post_bwd(o, zh, wts["g_onorm"], _heads(dog, n_b, n_s), name="dn_post_bwd")
    scan_cts = _dn_scan_bwd(qd, kd, uc, wc, attn, g_last, states, do, name="dn_scan_bwd")
    dqh, dkh, dvh, dbl, dlac, dlar, g["a_log"], g["dt_bias"] = _dn_prep_bwd(*dn_in, dn_inv, uc, wc, scan_cts, name="dn_prep_bwd")
    dqkv_c = jnp.concatenate([_unheads(t) for t in (dqh, dkh, dvh)], axis=-1)
    dqkv, g["conv_qkv"] = _conv_bwd(unflat(p_qkv), wts["conv_qkv"], dqkv_c, name="conv_bwd")
    dla = dlac[..., 0] + dlar.reshape(n_b, DN_HEADS, n_s)
    dba = jnp.concatenate([dbl[..., 0].transpose(0, 2, 1), dla.transpose(0, 2, 1),
                           jnp.zeros((n_b, n_s, BA_PAD - 2 * DN_HEADS), F32)], axis=-1).astype(MXU_DTYPE)
    dz = _unheads(dzh)

    dps = {"w_qkv": flat(dqkv)[None], "w_z": flat(dz)[None], "w_ga": dga[None], "w_gb": dgb[None], "w_ba": flat(dba)[None]}
    w_ga, w_gb = wts["w_gab"][:, :, :D_MODEL], wts["w_gab"][:, :, D_MODEL:]
    w_of = dict(wts, w_ga=w_ga, w_gb=w_gb)
    du_s5 = _mm_act([(dp_s5, wts["w_s5"])], "nt", reduce_shards=True, name="in_s5_bwd")
    du = _mm_act([(dps[k], w_of[k]) for k in dps], "nt", add=du_s5, name="in_bwd")[0]
    for k in dps:
        g[k] = _mm_tn(u, dps[k], name=f"d{k}")[0]
    g["w_s5"] = _cat_columns(_mm_tn(u, dp_s5, name="dw_s5"))
    dx0_res, df1, dgt1, g["g_mix"], dsh2, dsc2 = _pre_bwd(
        x, unflat(f1), gt1, wts["g_mix"], sh2, sc2, 0.5, unflat(du), dx1_res, name="pre2_bwd")
    da1, g["w1_ffn1"], g["w3_ffn1"], g["w2_ffn1"] = _ffn_bwd(
        flat(a1), wts["w1_ffn1"], wts["w3_ffn1"], wts["w2_ffn1"], ffn1_saved, flat(df1), "ffn1")
    grad_x, g["g_ffn1"], dsh1, dsc1 = _pre_bwd(
        x, None, None, wts["g_ffn1"], sh1, sc1, 0.0, unflat(da1), dx0_res, name="pre1_bwd")

    dmod = jnp.concatenate([t[:, 0, :] for t in (dsh1, dsc1, dgt1, dsh2, dsc2, dgt2, dsh3, dsc3, dgt3)], axis=-1)
    dmod_pad = jnp.zeros((SUBLANES, N_MOD * D_MODEL), F32).at[:n_b].set(dmod)
    dmod_s = dmod_pad.reshape(SUBLANES, N_SHARD, ADA_SHARD).transpose(1, 0, 2)
    g["w_ada"], g["b_ada"] = _ada_bwd(c_pad, dmod_s, name="ada_bwd")
    return loss, grad_x, g


IN_SPLITS = (("w_qkv", 3 * DN_WIDTH), ("w_z", DN_WIDTH), ("w_ba", 2 * DN_HEADS), ("w_s5", S5_WIDTH),
             ("w_ga", D_MODEL), ("w_gb", D_MODEL))
SHARDED = ("w_ada", "w1_ffn1", "w3_ffn1", "w2_ffn1", "w_in", "conv_qkv", "w_glu", "w_proj_a", "w_proj_b", "w_out",
           "w1_ffn2", "w3_ffn2", "w2_ffn2")
COLUMN_SHARDED = ("w_ada", "w1_ffn1", "w3_ffn1", "w_in", "conv_qkv", "w_proj_a", "w_proj_b", "w1_ffn2", "w3_ffn2")


def _cat_columns(stack):
    return stack.transpose(1, 0, 2).reshape(stack.shape[1], N_SHARD * stack.shape[2])


def _split_columns(full):
    n_r, n_c = full.shape
    return full.reshape(n_r, N_SHARD, n_c // N_SHARD).transpose(1, 0, 2)


def _gathered_weights(st, rep):
    w = {k: st[k] for k in ("w_ada", "w1_ffn1", "w3_ffn1", "w2_ffn1", "w1_ffn2", "w3_ffn2", "w2_ffn2")}
    w["b_ada"] = rep["b_ada"].reshape(N_SHARD, 1, ADA_SHARD)
    for k in ("g_ffn1", "g_mix", "g_ffn2", "g_final"):
        w[k] = rep[k].reshape(1, D_MODEL)
    w_in = _cat_columns(st["w_in"])
    start = 0
    for k, size in IN_SPLITS:
        w[k] = w_in[None, :, start:start + size]
        start += size
    w["w_gab"] = jnp.concatenate([w.pop("w_ga"), w.pop("w_gb")], axis=-1)
    w["w_s5"] = _split_columns(w["w_s5"][0])
    w["w_ba"] = jnp.pad(w["w_ba"], ((0, 0), (0, 0), (0, BA_PAD - 2 * DN_HEADS)))
    w["conv_qkv"] = _cat_columns(st["conv_qkv"])
    w["a_log"] = rep["a_log"].reshape(DN_HEADS, 1, 1)
    w["dt_bias"] = rep["dt_bias"].reshape(DN_HEADS, 1, 1)
    w["g_onorm"] = rep["g_onorm"].reshape(1, DN_HEAD_DIM)
    w["lam_re"] = rep["lam_re"].reshape(S5_GROUPS, 1, S5_STATE)
    w["lam_im"] = rep["lam_im"].reshape(S5_GROUPS, 1, S5_STATE)
    w["log_step"] = rep["log_step"].reshape(S5_GROUPS, 1, 1)
    w["bt_re"] = rep["b_re"][0].transpose(0, 2, 1)
    w["bt_im"] = rep["b_im"][0].transpose(0, 2, 1)
    w["c_re"] = rep["c_re"][0]
    w["c_im"] = rep["c_im"][0]
    w["d_skip"] = rep["d_skip"].reshape(1, S5_WIDTH)
    w["b_glu"] = rep["b_glu"].reshape(1, S5_WIDTH)
    w["w_glu"] = st["w_glu"].reshape(S5_WIDTH, S5_WIDTH)
    w["w_proj_a"] = _cat_columns(st["w_proj_a"])[None]
    w["w_proj_b"] = _cat_columns(st["w_proj_b"])[None]
    w["w_out"] = st["w_out"].reshape(1, D_MODEL, D_MODEL)
    return w


def _grads_to_problem_layout(g):
    st = {k: g[k] for k in ("w_ada", "w1_ffn1", "w3_ffn1", "w2_ffn1", "w1_ffn2", "w3_ffn2", "w2_ffn2")}
    w_in = jnp.concatenate([g[k][:, :size] for k, size in IN_SPLITS], axis=1)
    st["w_in"] = _split_columns(w_in)
    st["w_glu"] = g["w_glu"].reshape(N_SHARD, S5_WIDTH // N_SHARD, S5_WIDTH)
    st["w_proj_a"] = _split_columns(g["w_proj_a"])
    st["w_proj_b"] = _split_columns(g["w_proj_b"])
    st["w_out"] = g["w_out"].reshape(N_SHARD, D_MODEL // N_SHARD, D_MODEL)
    small = {
        "b_ada": g["b_ada"].reshape(1, N_MOD * D_MODEL),
        "g_ffn1": g["g_ffn1"], "g_mix": g["g_mix"], "g_ffn2": g["g_ffn2"], "g_final": g["g_final"].reshape(D_MODEL),
        "conv_qkv": g["conv_qkv"][None],
        "a_log": g["a_log"].reshape(1, DN_HEADS), "dt_bias": g["dt_bias"].reshape(1, DN_HEADS),
        "g_onorm": g["g_onorm"],
        "lam_re": g["lam_re"].reshape(1, S5_GROUPS, S5_STATE), "lam_im": g["lam_im"].reshape(1, S5_GROUPS, S5_STATE),
        "log_step": g["log_step"].reshape(1, S5_GROUPS),
        "b_re": g["bt_re"].transpose(0, 2, 1)[None], "b_im": g["bt_im"].transpose(0, 2, 1)[None],
        "c_re": g["c_re"][None], "c_im": g["c_im"][None],
        "d_skip": g["d_skip"], "b_glu": g["b_glu"],
    }
    return st, small


ELEMENTWISE_BLOCK_BYTES = 1 << 20


def _row_tile(n_rows, n_cols, n_lead=1, multiple=SUBLANES):
    best = None
    for t in range(multiple, n_rows + 1, multiple):
        if n_rows % t == 0 and n_lead * t * n_cols * 4 <= ELEMENTWISE_BLOCK_BYTES:
            best = t
    return best if best is not None else n_rows


def _add_sibling_half(g4, recv, my_c, *, name):
    n_sh, _, n_h, n_c = g4.shape
    th = _row_tile(n_h, n_c, multiple=2 * SUBLANES)

    def body(c_ref, g_ref, r_ref, o_ref):
        o_ref[0] = (g_ref[0, 0] + r_ref[0]).astype(o_ref.dtype)

    grid_spec = pltpu.PrefetchScalarGridSpec(
        num_scalar_prefetch=1, grid=(n_sh, n_h // th),
        in_specs=[pl.BlockSpec((1, 1, th, n_c), lambda s, i, c_ref: (s, c_ref[0], i, 0)),
                  pl.BlockSpec((1, th, n_c), lambda s, i, c_ref: (s, i, 0))],
        out_specs=pl.BlockSpec((1, th, n_c), lambda s, i, c_ref: (s, i, 0)))
    return pl.pallas_call(
        body, name=name, grid_spec=grid_spec, out_shape=jax.ShapeDtypeStruct((n_sh, n_h, n_c), MXU_DTYPE),
        compiler_params=_params(3 * _nbytes((th, n_c), F32)),
    )(*_hbm(my_c, g4, recv))


def _sum_slots(parts, *, name):
    n_p, n_r, n_c = parts.shape
    th = _row_tile(n_r, n_c, n_p)

    def body(p_ref, o_ref):
        total = p_ref[0].astype(F32)
        for k in range(1, n_p):
            total = total + p_ref[k].astype(F32)
        o_ref[...] = total

    return pl.pallas_call(
        body, name=name, grid=(n_r // th,),
        in_specs=[pl.BlockSpec((n_p, th, n_c), lambda i: (0, i, 0))],
        out_specs=pl.BlockSpec((th, n_c), lambda i: (i, 0)),
        out_shape=jax.ShapeDtypeStruct((n_r, n_c), F32),
        compiler_params=_params((n_p + 1) * _nbytes((th, n_c), F32)),
    )(*_hbm(parts))


def _cast_into_slot(w, place, dtype, *, name):
    n_r, n_c = w.shape
    th = _row_tile(n_r, n_c, multiple=2 * SUBLANES)

    def body(p_ref, w_ref, o_ref):
        o_ref[0] = w_ref[...].astype(o_ref.dtype)

    grid_spec = pltpu.PrefetchScalarGridSpec(
        num_scalar_prefetch=1, grid=(n_r // th,),
        in_specs=[pl.BlockSpec((th, n_c), lambda i, p: (i, 0))],
        out_specs=pl.BlockSpec((1, th, n_c), lambda i, p: (p[1], i, 0)))
    return pl.pallas_call(
        body, name=name, grid_spec=grid_spec, out_shape=jax.ShapeDtypeStruct((N_SHARD, n_r, n_c), dtype),
        compiler_params=_params(2 * _nbytes((th, n_c), F32)),
    )(*_hbm(place, w))


def _sum_chips(own, parts, place, *, name):
    n_sh, n_h, n_c = own.shape
    th = _row_tile(n_h, n_c, n_sh, multiple=2 * SUBLANES)

    def body(p_ref, own_ref, a_ref, b_ref, c_ref, o_ref):
        o_ref[0] = ((own_ref[0].astype(F32) + a_ref[0].astype(F32)) + b_ref[0].astype(F32)) + c_ref[0].astype(F32)

    slab = lambda k: pl.BlockSpec((1, th, n_c), lambda i, p, k=k: (p[k], i, 0))
    grid_spec = pltpu.PrefetchScalarGridSpec(
        num_scalar_prefetch=1, grid=(n_h // th,),
        in_specs=[slab(1), slab(2), slab(3), slab(4)], out_specs=slab(0))
    return pl.pallas_call(
        body, name=name, grid_spec=grid_spec, out_shape=jax.ShapeDtypeStruct((2, n_h, n_c), F32),
        compiler_params=_params(5 * _nbytes((th, n_c), F32)),
    )(*_hbm(place, own, parts, parts, parts))


def _adamw(w, g, m, v, *, name):
    n_r, n_c = w.shape
    th = _row_tile(n_r, n_c)
    bias1 = 1.0 - ADAM_B1 ** ADAM_STEP
    bias2 = 1.0 - ADAM_B2 ** ADAM_STEP

    def body(w_ref, g_ref, m_ref, v_ref, d_ref, mo_ref, vo_ref):
        gv = g_ref[...]
        m_new = ADAM_B1 * m_ref[...] + (1.0 - ADAM_B1) * gv
        v_new = ADAM_B2 * v_ref[...] + (1.0 - ADAM_B2) * jnp.square(gv)
        d_ref[...] = -ADAM_LR * ((m_new / bias1) / (jnp.sqrt(v_new / bias2) + ADAM_EPS) + ADAM_WD * w_ref[...])
        mo_ref[...] = m_new
        vo_ref[...] = v_new

    spec = pl.BlockSpec((th, n_c), lambda i: (i, 0))
    shape = jax.ShapeDtypeStruct((n_r, n_c), F32)
    return pl.pallas_call(
        body, name=name, grid=(n_r // th,), in_specs=[spec] * 4, out_specs=(spec,) * 3, out_shape=(shape,) * 3,
        compiler_params=_params(7 * _nbytes((th, n_c), F32)),
    )(*_hbm(w, g, m, v))


CHIP_FLIPS = ((1, 0), (0, 1), (1, 1))
DEVICE_FLIPS = tuple((fx, fy, fc) for fx in (0, 1) for fy in (0, 1) for fc in (0, 1))[1:]


def _exchange(ins, out_shapes, plan, n_local, n_remote, *, name, aliased=False):
    n_in, n_out = len(ins), len(out_shapes)

    def body(*refs):
        in_refs, out_refs = refs[:n_in], refs[n_in:n_in + n_out]
        send_sems, recv_sems, local_sems = refs[n_in + n_out:]
        me = (lax.axis_index("x"), lax.axis_index("y"), lax.axis_index("c"))
        local, remote = plan(in_refs, out_refs, me)
        assert len(local) == n_local and len(remote) == n_remote
        here = [pltpu.make_async_copy(src, dst, local_sems.at[i]) for i, (src, dst) in enumerate(local)]
        for cp in here:
            cp.start()
        sends = [pltpu.make_async_remote_copy(src_ref=src, dst_ref=dst, send_sem=send_sems.at[i], recv_sem=recv_sems.at[i],
                                              device_id=peer, device_id_type=pl.DeviceIdType.MESH)
                 for i, (src, dst, _, peer) in enumerate(remote)]
        for cp in sends:
            cp.start()
        for i, (src, _, landing, peer) in enumerate(remote):
            pltpu.make_async_remote_copy(src_ref=src, dst_ref=landing, send_sem=send_sems.at[i], recv_sem=recv_sems.at[i],
                                         device_id=peer, device_id_type=pl.DeviceIdType.MESH).wait_recv()
        for cp in sends:
            cp.wait_send()
        for cp in here:
            cp.wait()

    any_spec = pl.BlockSpec(memory_space=pl.ANY)
    return pl.pallas_call(
        body, name=name, in_specs=[any_spec] * n_in, out_specs=tuple([any_spec] * n_out), out_shape=tuple(out_shapes),
        scratch_shapes=[pltpu.SemaphoreType.DMA((n_remote,)), pltpu.SemaphoreType.DMA((n_remote,)),
                        pltpu.SemaphoreType.DMA((max(n_local, 1),))],
        input_output_aliases={k: k for k in range(n_in)} if aliased else {},
    )(*ins)


def _gather_shards(stacks, *, name):
    n = len(stacks)
    halved = [a.shape[1] >= 32 for a in stacks]
    n_ici = len(CHIP_FLIPS) * n
    n_pass = len(CHIP_FLIPS) * sum(halved)

    def body(*refs):
        outs = refs[n:2 * n]
        send_sems, recv_sems = refs[2 * n:]
        x, y, c = lax.axis_index("x"), lax.axis_index("y"), lax.axis_index("c")
        mine = 2 * x + y

        def rows(k, slot, half):
            if not halved[k]:
                return outs[k].at[slot]
            n_h = stacks[k].shape[1] // 2
            return outs[k].at[slot, pl.ds(pl.multiple_of(half * n_h, 16), n_h)]

        def copy(i, src, dst, peer):
            return pltpu.make_async_remote_copy(src_ref=src, dst_ref=dst, send_sem=send_sems.at[i], recv_sem=recv_sems.at[i],
                                                device_id=peer, device_id_type=pl.DeviceIdType.MESH)

        started = []
        for j, (fx, fy) in enumerate(CHIP_FLIPS):
            for k in range(n):
                cp = copy(j * n + k, rows(k, mine, c), rows(k, mine, c), (x ^ fx, y ^ fy, c))
                cp.start()
                started.append(cp)
        i_pass = n_ici
        expect = []
        for j, (fx, fy) in enumerate(CHIP_FLIPS):
            peer_chip = 2 * (x ^ fx) + (y ^ fy)
            for k in range(n):
                landed = rows(k, peer_chip, c)
                copy(j * n + k, landed, landed, (x ^ fx, y ^ fy, c)).wait_recv()
                if halved[k]:
                    cp = copy(i_pass, landed, landed, (x, y, 1 - c))
                    cp.start()
                    started.append(cp)
                    expect.append((i_pass, rows(k, peer_chip, 1 - c)))
                    i_pass += 1
        for i, landing in expect:
            copy(i, landing, landing, (x, y, 1 - c)).wait_recv()
        for cp in started:
            cp.wait_send()

    any_spec = pl.BlockSpec(memory_space=pl.ANY)
    n_sem = n_ici + n_pass
    return pl.pallas_call(
        body, name=name, in_specs=[any_spec] * n, out_specs=tuple([any_spec] * n),
        out_shape=tuple(jax.ShapeDtypeStruct(a.shape, a.dtype) for a in stacks),
        scratch_shapes=[pltpu.SemaphoreType.DMA((n_sem,)), pltpu.SemaphoreType.DMA((n_sem,))],
        input_output_aliases={k: k for k in range(n)},
    )(*stacks)


def _swap_sibling_halves(g4s, *, name):
    n = len(g4s)

    def plan(in_refs, out_refs, me):
        x, y, c = me
        remote = [(in_refs[k].at[:, 1 - c], out_refs[k], out_refs[k], (x, y, 1 - c)) for k in range(n)]
        return [], remote

    shapes = [jax.ShapeDtypeStruct((a.shape[0],) + a.shape[2:], a.dtype) for a in g4s]
    return _exchange(g4s, shapes, plan, 0, n, name=name)


def _scatter_to_chips(hs, *, name):
    n = len(hs)

    def plan(in_refs, out_refs, me):
        x, y, c = me
        mine = 2 * x + y
        remote = []
        for fx, fy in CHIP_FLIPS:
            px, py = x ^ fx, y ^ fy
            peer = 2 * px + py
            for k in range(n):
                remote.append((in_refs[k].at[peer], out_refs[k].at[mine], out_refs[k].at[peer], (px, py, c)))
        return [], remote

    shapes = [jax.ShapeDtypeStruct(a.shape, a.dtype) for a in hs]
    return _exchange(hs, shapes, plan, 0, len(CHIP_FLIPS) * n, name=name)


def _join_sibling_halves(rs, *, name):
    n = len(rs)

    def plan(in_refs, out_refs, me):
        x, y, c = me
        remote = [(out_refs[k].at[c], out_refs[k].at[c], out_refs[k].at[1 - c], (x, y, 1 - c)) for k in range(n)]
        return [], remote

    shapes = [jax.ShapeDtypeStruct(a.shape, a.dtype) for a in rs]
    return _exchange(rs, shapes, plan, 0, n, name=name, aliased=True)


def _gather_all_devices(packed, *, name):
    def plan(in_refs, out_refs, me):
        x, y, c = me
        mine = 4 * x + 2 * y + c
        remote = []
        for fx, fy, fc in DEVICE_FLIPS:
            px, py, pc = x ^ fx, y ^ fy, c ^ fc
            remote.append((in_refs[0], out_refs[0].at[mine], out_refs[0].at[4 * px + 2 * py + pc], (px, py, pc)))
        return [(in_refs[0], out_refs[0].at[mine])], remote

    shape = jax.ShapeDtypeStruct((2 * N_SHARD,) + packed.shape, packed.dtype)
    return _exchange([packed], [shape], plan, 1, len(DEVICE_FLIPS), name=name)[0]


WEIGHT_NAMES = ("w_ada", "b_ada", "g_ffn1", "w1_ffn1", "w3_ffn1", "w2_ffn1", "g_mix", "w_in", "conv_qkv", "a_log",
                "dt_bias", "g_onorm", "lam_re", "lam_im", "log_step", "b_re", "b_im", "c_re", "c_im", "d_skip", "w_glu",
                "b_glu", "w_proj_a", "w_proj_b", "w_out", "g_ffn2", "w1_ffn2", "w3_ffn2", "w2_ffn2", "g_final")
LARGE = tuple(n for n in SHARDED if n != "conv_qkv")
SMALL = tuple(n for n in WEIGHT_NAMES if n not in LARGE)
PACK_ROW = SUBLANES * LANES


def _pack(arrays):
    flat = jnp.concatenate([a.reshape(-1) for a in arrays])
    n_pad = -flat.shape[0] % PACK_ROW
    return jnp.pad(flat, (0, n_pad)).reshape(-1, LANES)


def _unpack(packed, shapes):
    flat = packed.reshape(-1)
    out, start = [], 0
    for s in shapes:
        size = math.prod(s)
        out.append(flat[start:start + size].reshape(s))
        start += size
    return out


TRANSPOSED = ("w1_ffn1", "w3_ffn1", "w1_ffn2", "w3_ffn2")


def _to_internal(name, a):
    return jnp.swapaxes(a[0], 0, 1) if name in TRANSPOSED else a[0]


def _from_internal(name, a):
    return (jnp.swapaxes(a, 0, 1) if name in TRANSPOSED else a)[None]


def _step(x, c, target, weights, m_in, v_in):
    xi, yi, ci = lax.axis_index("x"), lax.axis_index("y"), lax.axis_index("c")
    my_chip = 2 * xi + yi

    others = [k + (k >= my_chip).astype(jnp.int32) for k in range(N_SHARD - 1)]
    place = jnp.stack([ci, my_chip] + others).astype(jnp.int32)

    slots = [_cast_into_slot(_to_internal(n, weights[n]), place, F32 if n == "conv_qkv" else MXU_DTYPE, name=f"cast_{n}")
             for n in SHARDED]
    stacks = dict(zip(SHARDED, _gather_shards(slots, name="gather_weights")))
    rep = {n: weights[n] for n in WEIGHT_NAMES if n not in SHARDED}
    loss, grad_x, g = _local_step(x, c, target, _gathered_weights(stacks, rep))
    g_stacks, g_small = _grads_to_problem_layout(g)

    g4s = [g_stacks[n].reshape(N_SHARD, 2, g_stacks[n].shape[1] // 2, g_stacks[n].shape[2]) for n in LARGE]
    from_sibling = _swap_sibling_halves(g4s, name="swap_sibling_halves")
    chip_sums = [_add_sibling_half(a, r, place, name=f"chip_sum_{n}") for n, a, r in zip(LARGE, g4s, from_sibling)]
    from_chips = _scatter_to_chips(chip_sums, name="scatter_to_chips")
    reduced = [_sum_chips(h, p, place, name=f"sum_chips_{n}") for n, h, p in zip(LARGE, chip_sums, from_chips)]
    joined = _join_sibling_halves(reduced, name="join_sibling_halves")
    grads_2d = {n: j.reshape(2 * j.shape[1], j.shape[2]) for n, j in zip(LARGE, joined)}
    grads = {n: _from_internal(n, a) for n, a in grads_2d.items()}

    small_shapes = [g_small[n].shape for n in SMALL] + [(1, 1)]
    packed = _pack([g_small[n] for n in SMALL] + [loss])
    total = _sum_slots(_gather_all_devices(packed, name="gather_small"), name="sum_small")
    *small_grads, loss_sum = _unpack(total, small_shapes)
    grads.update(zip(SMALL, small_grads))
    n_conv = weights["conv_qkv"].shape[-1]
    grads["conv_qkv"] = lax.dynamic_slice_in_dim(grads["conv_qkv"], my_chip * n_conv, n_conv, axis=2)

    delta, new_m, new_v = {}, {}, {}
    grads_2d["conv_qkv"] = grads["conv_qkv"][0]
    for n in LARGE + ("conv_qkv",):
        outs = _adamw(_to_internal(n, weights[n]), grads_2d[n], _to_internal(n, m_in[n]), _to_internal(n, v_in[n]),
                      name=f"adamw_{n}")
        delta[n], new_m[n], new_v[n] = [_from_internal(n, o) for o in outs]
    packed_names = tuple(n for n in SMALL if n != "conv_qkv")
    shapes = [weights[n].shape for n in packed_names]
    outs = _adamw(*[_pack([d[n] for n in packed_names]) for d in (weights, grads, m_in, v_in)], name="adamw_small")
    for d, o in zip((delta, new_m, new_v), outs):
        d.update(zip(packed_names, _unpack(o, shapes)))
    return (loss_sum.reshape(()), grad_x, *[grads[n] for n in WEIGHT_NAMES], *[delta[n] for n in WEIGHT_NAMES],
            *[new_m[n] for n in WEIGHT_NAMES], *[new_v[n] for n in WEIGHT_NAMES])


def kernel(x, c, w_ada, b_ada, g_ffn1, w1_ffn1, w3_ffn1, w2_ffn1, g_mix, w_in, conv_qkv, a_log, dt_bias, g_onorm, lam_re, lam_im, log_step, b_re, b_im, c_re, c_im, d_skip, w_glu, b_glu, w_proj_a, w_proj_b, w_out, g_ffn2, w1_ffn2, w3_ffn2, w2_ffn2, g_final, loss_target, m_w_ada, m_b_ada, m_g_ffn1, m_w1_ffn1, m_w3_ffn1, m_w2_ffn1, m_g_mix, m_w_in, m_conv_qkv, m_a_log, m_dt_bias, m_g_onorm, m_lam_re, m_lam_im, m_log_step, m_b_re, m_b_im, m_c_re, m_c_im, m_d_skip, m_w_glu, m_b_glu, m_w_proj_a, m_w_proj_b, m_w_out, m_g_ffn2, m_w1_ffn2, m_w3_ffn2, m_w2_ffn2, m_g_final, v_w_ada, v_b_ada, v_g_ffn1, v_w1_ffn1, v_w3_ffn1, v_w2_ffn1, v_g_mix, v_w_in, v_conv_qkv, v_a_log, v_dt_bias, v_g_onorm, v_lam_re, v_lam_im, v_log_step, v_b_re, v_b_im, v_c_re, v_c_im, v_d_skip, v_w_glu, v_b_glu, v_w_proj_a, v_w_proj_b, v_w_out, v_g_ffn2, v_w1_ffn2, v_w3_ffn2, v_w2_ffn2, v_g_final):
    w_vals = (w_ada, b_ada, g_ffn1, w1_ffn1, w3_ffn1, w2_ffn1, g_mix, w_in, conv_qkv, a_log, dt_bias, g_onorm, lam_re, lam_im, log_step, b_re, b_im, c_re, c_im, d_skip, w_glu, b_glu, w_proj_a, w_proj_b, w_out, g_ffn2, w1_ffn2, w3_ffn2, w2_ffn2, g_final)
    m_vals = (m_w_ada, m_b_ada, m_g_ffn1, m_w1_ffn1, m_w3_ffn1, m_w2_ffn1, m_g_mix, m_w_in, m_conv_qkv, m_a_log, m_dt_bias, m_g_onorm, m_lam_re, m_lam_im, m_log_step, m_b_re, m_b_im, m_c_re, m_c_im, m_d_skip, m_w_glu, m_b_glu, m_w_proj_a, m_w_proj_b, m_w_out, m_g_ffn2, m_w1_ffn2, m_w3_ffn2, m_w2_ffn2, m_g_final)
    v_vals = (v_w_ada, v_b_ada, v_g_ffn1, v_w1_ffn1, v_w3_ffn1, v_w2_ffn1, v_g_mix, v_w_in, v_conv_qkv, v_a_log, v_dt_bias, v_g_onorm, v_lam_re, v_lam_im, v_log_step, v_b_re, v_b_im, v_c_re, v_c_im, v_d_skip, v_w_glu, v_b_glu, v_w_proj_a, v_w_proj_b, v_w_out, v_g_ffn2, v_w1_ffn2, v_w3_ffn2, v_w2_ffn2, v_g_final)
    return _step(x, c, loss_target, dict(zip(WEIGHT_NAMES, w_vals)), dict(zip(WEIGHT_NAMES, m_vals)),
                 dict(zip(WEIGHT_NAMES, v_vals)))
```

```python
import functools
import math

import jax
import jax.numpy as jnp
from jax import lax
from jax.experimental import pallas as pl
from jax.experimental.pallas import tpu as pltpu

F32 = jnp.float32
BF16 = jnp.bfloat16
MXU_DTYPE = BF16

D_MODEL = 1024
D_FF = 2816
DN_HEADS = 8
DN_HEAD_DIM = 64
DN_WIDTH = DN_HEADS * DN_HEAD_DIM
CONV_WIDTH = 4
CHUNK = 64
S5_GROUP_CH = 16
S5_GROUPS = 32
S5_WIDTH = S5_GROUPS * S5_GROUP_CH
S5_STATE = 64
S5_LANES = S5_GROUPS * S5_STATE
N_MOD = 9
EPS = 1e-6
N_SHARD = 4
FF_SHARD = D_FF // N_SHARD
BA_PAD = 128

ADAM_LR = 0.001
ADAM_B1 = 0.9
ADAM_B2 = 0.999
ADAM_EPS = 1e-08
ADAM_WD = 0.01
ADAM_STEP = 10

VMEM_BYTES_V7X = 64 * 1024 * 1024
SUBLANES = 8
LANES = 128


def _params(block_bytes, extra_bytes=0):
    need = 2 * block_bytes + extra_bytes + (4 << 20)
    return pltpu.CompilerParams(vmem_limit_bytes=int(min(max(need, 16 << 20), VMEM_BYTES_V7X - (8 << 20))))


def _nbytes(shape, dtype):
    return math.prod(shape) * jnp.dtype(dtype).itemsize


HBM_OPERAND_BYTES = 1 << 20


def _hbm(*args):
    return [pltpu.with_memory_space_constraint(a, pltpu.HBM) if _nbytes(a.shape, a.dtype) >= HBM_OPERAND_BYTES else a
            for a in args]


_NN = (((1,), (0,)), ((), ()))
_NT = (((1,), (1,)), ((), ()))
_TN = (((0,), (0,)), ((), ()))


def _mm_act(pairs, mode, *, name, out_sharded=False, reduce_shards=False, out_dtype=F32, add=None, tm=512):
    n_tok = pairs[0][0].shape[1]
    n_out = pairs[0][1].shape[2] if mode == "nn" else pairs[0][1].shape[1]
    tm = min(tm, n_tok)
    tn = n_out if n_out <= 1536 else 1024
    assert n_tok % tm == 0 and n_out % tn == 0
    n_so = N_SHARD if out_sharded else 1
    n_red = N_SHARD if reduce_shards else 1
    grid = (n_so, n_tok // tm, n_out // tn, n_red)
    dims = _NN if mode == "nn" else _NT

    def shard_of(n_sh):
        if n_sh == 1:
            return lambda s, r: 0
        return (lambda s, r: s) if out_sharded else (lambda s, r: r)

    in_specs, args, blk = [], [], 0
    for a, b in pairs:
        k_dim = a.shape[2]
        sa, sb = shard_of(a.shape[0]), shard_of(b.shape[0])
        in_specs.append(pl.BlockSpec((1, tm, k_dim), lambda s, i, j, r, sa=sa: (sa(s, r), i, 0)))
        if mode == "nn":
            assert b.shape[1] == k_dim
            in_specs.append(pl.BlockSpec((1, k_dim, tn), lambda s, i, j, r, sb=sb: (sb(s, r), 0, j)))
        else:
            assert b.shape[2] == k_dim
            in_specs.append(pl.BlockSpec((1, tn, k_dim), lambda s, i, j, r, sb=sb: (sb(s, r), j, 0)))
        args += [a, b]
        blk += _nbytes((tm, k_dim), a.dtype) + _nbytes((k_dim, tn), b.dtype)
    if add is not None:
        in_specs.append(pl.BlockSpec((1, tm, tn), lambda s, i, j, r: (s, i, j)))
        args.append(add)
        blk += _nbytes((tm, tn), F32)
    blk += _nbytes((tm, tn), out_dtype)
    n_pairs = len(pairs)

    def body(*refs):
        out_ref = refs[2 * n_pairs + (add is not None)]
        acc = None
        for k in range(n_pairs):
            a = refs[2 * k][0].astype(MXU_DTYPE)
            b = refs[2 * k + 1][0].astype(MXU_DTYPE)
            d = lax.dot_general(a, b, dims, preferred_element_type=F32)
            acc = d if acc is None else acc + d

        def finish(total):
            if add is not None:
                total = total + refs[2 * n_pairs][0]
            out_ref[0] = total.astype(out_dtype)

        if n_red == 1:
            finish(acc)
        else:
            acc_ref = refs[-1]
            r = pl.program_id(3)

            @pl.when(r == 0)
            def _():
                acc_ref[...] = acc

            @pl.when(r > 0)
            def _():
                acc_ref[...] += acc

            @pl.when(r == n_red - 1)
            def _():
                finish(acc_ref[...])

    return pl.pallas_call(
        body,
        name=name,
        grid=grid,
        in_specs=in_specs,
        out_specs=pl.BlockSpec((1, tm, tn), lambda s, i, j, r: (s, i, j)),
        out_shape=jax.ShapeDtypeStruct((n_so, n_tok, n_out), out_dtype),
        scratch_shapes=[pltpu.VMEM((tm, tn), F32)] if n_red > 1 else [],
        compiler_params=_params(blk, 3 * _nbytes((tm, tn), F32)),
    )(*_hbm(*args))


def _mm_tn(a, b, *, name, tt=512):
    n_tok, k_dim = a.shape[1], a.shape[2]
    n_out = b.shape[2]
    tt = min(tt, n_tok)
    tk = k_dim if k_dim <= 1536 else 1024
    tn = n_out if n_out <= 1536 else 1024
    assert n_tok % tt == 0 and k_dim % tk == 0 and n_out % tn == 0
    n_so = max(a.shape[0], b.shape[0])
    sa = (lambda s: s) if a.shape[0] > 1 else (lambda s: 0)
    sb = (lambda s: s) if b.shape[0] > 1 else (lambda s: 0)
    grid = (n_so, k_dim // tk, n_out // tn, n_tok // tt)

    def body(a_ref, b_ref, out_ref):
        d = lax.dot_general(a_ref[0].astype(MXU_DTYPE), b_ref[0].astype(MXU_DTYPE), _TN, preferred_element_type=F32)
        t = pl.program_id(3)

        @pl.when(t == 0)
        def _():
            out_ref[0] = d

        @pl.when(t > 0)
        def _():
            out_ref[0] += d

    blk = _nbytes((tt, tk), a.dtype) + _nbytes((tt, tn), b.dtype) + _nbytes((tk, tn), F32)
    return pl.pallas_call(
        body,
        name=name,
        grid=grid,
        in_specs=[
            pl.BlockSpec((1, tt, tk), lambda s, ki, nj, t: (sa(s), t, ki)),
            pl.BlockSpec((1, tt, tn), lambda s, ki, nj, t: (sb(s), t, nj)),
        ],
        out_specs=pl.BlockSpec((1, tk, tn), lambda s, ki, nj, t: (s, ki, nj)),
        out_shape=jax.ShapeDtypeStruct((n_so, k_dim, n_out), F32),
        compiler_params=_params(blk, 2 * _nbytes((tk, tn), F32) + _nbytes((tt, tk), F32)),
    )(*_hbm(a, b))


@functools.partial(jax.custom_vjp, nondiff_argnums=(2,))
def _mdot(a, b, dims):
    return lax.dot_general(a.astype(MXU_DTYPE), b.astype(MXU_DTYPE), dims, preferred_element_type=F32)


def _mdot_fwd(a, b, dims):
    return _mdot(a, b, dims), (a, b)


def _mdot_bwd(dims, res, g):
    a, b = res
    (ca, cb), (ba, bb) = dims
    nb = len(ba)
    assert tuple(ba) == tuple(range(nb)) and tuple(bb) == tuple(range(nb)) and len(ca) == 1 and a.ndim == nb + 2
    batch = (tuple(range(nb)), tuple(range(nb)))
    ra, rb = nb, nb + 1
    a_free = (set(range(nb, nb + 2)) - set(ca)).pop()
    b_free = (set(range(nb, nb + 2)) - set(cb)).pop()
    if a_free < ca[0]:
        da = _mdot(g, b, (((rb,), (b_free,)), batch))
    else:
        da = _mdot(b, g, (((b_free,), (rb,)), batch))
    if b_free > cb[0]:
        db = _mdot(a, g, (((a_free,), (ra,)), batch))
    else:
        db = _mdot(g, a, (((ra,), (a_free,)), batch))
    return da.astype(a.dtype), db.astype(b.dtype)


_mdot.defvjp(_mdot_fwd, _mdot_bwd)


def _rms(x, gain):
    return x * lax.rsqrt(jnp.mean(x * x, axis=-1, keepdims=True) + EPS) * gain


def _pre_fn(coef, x_in, f, gate, gain, shift, scale):
    x_new = x_in if f is None else x_in + coef * gate * f
    return x_new, _rms(x_new, gain) * (1.0 + scale) + shift


def _row_spec(ts):
    return pl.BlockSpec((1, ts, D_MODEL), lambda b, j: (b, j, 0))


_BATCH_VEC = pl.BlockSpec((1, 1, D_MODEL), lambda b, j: (b, 0, 0))
_ONE_VEC = pl.BlockSpec((1, D_MODEL), lambda b, j: (0, 0))


def _pre(x_in, f, gate, gain, shift, scale, coef, *, name, ts=512):
    n_b, n_s, _ = x_in.shape
    ts = min(ts, n_s)
    has_res = f is not None

    def body(*refs):
        if has_res:
            x_ref, f_ref, gate_ref, gain_ref, sh_ref, sc_ref, xn_ref, a_ref = refs
            x_new, a = _pre_fn(coef, x_ref[0], f_ref[0], gate_ref[0], gain_ref[...], sh_ref[0], sc_ref[0])
            xn_ref[0] = x_new
        else:
            x_ref, gain_ref, sh_ref, sc_ref, a_ref = refs
            _, a = _pre_fn(coef, x_ref[0], None, None, gain_ref[...], sh_ref[0], sc_ref[0])
        a_ref[0] = a.astype(a_ref.dtype)

    row = _row_spec(ts)
    if has_res:
        args = (x_in, f, gate, gain, shift, scale)
        in_specs = [row, row, _BATCH_VEC, _ONE_VEC, _BATCH_VEC, _BATCH_VEC]
        out_specs = (row, row)
        out_shape = (jax.ShapeDtypeStruct(x_in.shape, F32), jax.ShapeDtypeStruct(x_in.shape, MXU_DTYPE))
    else:
        args = (x_in, gain, shift, scale)
        in_specs = [row, _ONE_VEC, _BATCH_VEC, _BATCH_VEC]
        out_specs = row
        out_shape = jax.ShapeDtypeStruct(x_in.shape, MXU_DTYPE)
    return pl.pallas_call(
        body, name=name, grid=(n_b, n_s // ts), in_specs=in_specs, out_specs=out_specs, out_shape=out_shape,
        compiler_params=_params(5 * _nbytes((ts, D_MODEL), F32), 4 * _nbytes((ts, D_MODEL), F32)),
    )(*_hbm(*args))


def _accumulate(ref, value, first):
    @pl.when(first)
    def _():
        ref[...] = value

    @pl.when(jnp.logical_not(first))
    def _():
        ref[...] += value


def _pre_bwd(x_in, f, gate, gain, shift, scale, coef, da, dx_up, *, name, ts=512):
    n_b, n_s, _ = x_in.shape
    ts = min(ts, n_s)
    has_res = f is not None
    has_up = dx_up is not None

    def body(*refs):
        refs = list(refs)
        x_ref = refs.pop(0)
        f_ref, gate_ref = (refs.pop(0), refs.pop(0)) if has_res else (None, None)
        gain_ref, sh_ref, sc_ref, da_ref = refs.pop(0), refs.pop(0), refs.pop(0), refs.pop(0)
        up_ref = refs.pop(0) if has_up else None
        dx_ref = refs.pop(0)
        df_ref, dgate_ref = (refs.pop(0), refs.pop(0)) if has_res else (None, None)
        dgain_ref, dsh_ref, dsc_ref = refs
        b, j = pl.program_id(0), pl.program_id(1)
        da_v = da_ref[0].astype(F32)
        up_v = up_ref[0] if has_up else jnp.zeros((ts, D_MODEL), F32)
        if has_res:
            fn = functools.partial(_pre_fn, coef)
            _, pull = jax.vjp(fn, x_ref[0], f_ref[0], gate_ref[0], gain_ref[...], sh_ref[0], sc_ref[0])
            dx, df, dgate, dgain, dsh, dsc = pull((up_v, da_v))
            df_ref[0] = df.astype(df_ref.dtype)
            _accumulate(dgate_ref, dgate[None], j == 0)
        else:
            fn = lambda x, g, sh, sc: _pre_fn(coef, x, None, None, g, sh, sc)
            _, pull = jax.vjp(fn, x_ref[0], gain_ref[...], sh_ref[0], sc_ref[0])
            dx, dgain, dsh, dsc = pull((up_v, da_v))
        dx_ref[0] = dx
        _accumulate(dgain_ref, dgain, jnp.logical_and(b == 0, j == 0))
        _accumulate(dsh_ref, dsh[None], j == 0)
        _accumulate(dsc_ref, dsc[None], j == 0)

    row = _row_spec(ts)
    args, in_specs = [x_in], [row]
    if has_res:
        args += [f, gate]
        in_specs += [row, _BATCH_VEC]
    args += [gain, shift, scale, da]
    in_specs += [_ONE_VEC, _BATCH_VEC, _BATCH_VEC, row]
    if has_up:
        args.append(dx_up)
        in_specs.append(row)
    vec = jax.ShapeDtypeStruct((n_b, 1, D_MODEL), F32)
    out_shape, out_specs = [jax.ShapeDtypeStruct(x_in.shape, F32)], [row]
    if has_res:
        out_shape += [jax.ShapeDtypeStruct(x_in.shape, MXU_DTYPE), vec]
        out_specs += [row, _BATCH_VEC]
    out_shape += [jax.ShapeDtypeStruct((1, D_MODEL), F32), vec, vec]
    out_specs += [_ONE_VEC, _BATCH_VEC, _BATCH_VEC]
    return pl.pallas_call(
        body, name=name, grid=(n_b, n_s // ts), in_specs=in_specs, out_specs=tuple(out_specs), out_shape=tuple(out_shape),
        compiler_params=_params(6 * _nbytes((ts, D_MODEL), F32), 8 * _nbytes((ts, D_MODEL), F32)),
    )(*_hbm(*args))


def _final_fn(x_in, f, gate, gain, target):
    x_new = x_in + 0.5 * gate * f
    err = jnp.square(_rms(x_new, gain) - target)
    return 0.5 * jnp.sum(jnp.mean(err, axis=-1))


def _final(x_in, f, gate, gain, target, *, name, ts=512):
    n_b, n_s, _ = x_in.shape
    ts = min(ts, n_s)

    def body(x_ref, f_ref, gate_ref, gain_ref, t_ref, loss_ref, dx_ref, df_ref, dgate_ref, dgain_ref):
        b, j = pl.program_id(0), pl.program_id(1)
        loss, (dx, df, dgate, dgain) = jax.value_and_grad(_final_fn, argnums=(0, 1, 2, 3))(
            x_ref[0], f_ref[0], gate_ref[0], gain_ref[...], t_ref[0])
        first = jnp.logical_and(b == 0, j == 0)
        _accumulate(loss_ref, jnp.reshape(loss, (1, 1)), first)
        dx_ref[0] = dx
        df_ref[0] = df.astype(df_ref.dtype)
        _accumulate(dgate_ref, dgate[None], j == 0)
        _accumulate(dgain_ref, dgain, first)

    row = _row_spec(ts)
    return pl.pallas_call(
        body, name=name, grid=(n_b, n_s // ts),
        in_specs=[row, row, _BATCH_VEC, _ONE_VEC, row],
        out_specs=(pl.BlockSpec((1, 1), lambda b, j: (0, 0)), row, row, _BATCH_VEC, _ONE_VEC),
        out_shape=(jax.ShapeDtypeStruct((1, 1), F32), jax.ShapeDtypeStruct(x_in.shape, F32),
                   jax.ShapeDtypeStruct(x_in.shape, MXU_DTYPE), jax.ShapeDtypeStruct((n_b, 1, D_MODEL), F32),
                   jax.ShapeDtypeStruct((1, D_MODEL), F32)),
        compiler_params=_params(5 * _nbytes((ts, D_MODEL), F32), 8 * _nbytes((ts, D_MODEL), F32)),
    )(*_hbm(x_in, f, gate, gain, target))


FFN_TOKENS = 1024


def _ffn_up(a, w1s, w3s, *, name, tm=FFN_TOKENS):
    n_tok = a.shape[0]
    tm = min(tm, n_tok)

    def body(a_ref, w1_ref, w3_ref, h1_ref, h3_ref, g_ref):
        av = a_ref[...].astype(MXU_DTYPE)
        h1 = lax.dot_general(av, w1_ref[0].astype(MXU_DTYPE), _NT, preferred_element_type=F32)
        h3 = lax.dot_general(av, w3_ref[0].astype(MXU_DTYPE), _NT, preferred_element_type=F32)
        h1_ref[0] = h1.astype(h1_ref.dtype)
        h3_ref[0] = h3.astype(h3_ref.dtype)
        g_ref[0] = (jax.nn.silu(h1) * h3).astype(g_ref.dtype)

    w_spec = pl.BlockSpec((1, FF_SHARD, D_MODEL), lambda s, i: (s, 0, 0))
    h_spec = pl.BlockSpec((1, tm, FF_SHARD), lambda s, i: (s, i, 0))
    h_shape = jax.ShapeDtypeStruct((N_SHARD, n_tok, FF_SHARD), MXU_DTYPE)
    blk = _nbytes((tm, D_MODEL), a.dtype) + 2 * _nbytes((D_MODEL, FF_SHARD), w1s.dtype) + 3 * _nbytes((tm, FF_SHARD), MXU_DTYPE)
    return pl.pallas_call(
        body, name=name, grid=(N_SHARD, n_tok // tm),
        in_specs=[pl.BlockSpec((tm, D_MODEL), lambda s, i: (i, 0)), w_spec, w_spec],
        out_specs=(h_spec, h_spec, h_spec), out_shape=(h_shape, h_shape, h_shape),
        compiler_params=_params(blk, 6 * _nbytes((tm, FF_SHARD), F32)),
    )(*_hbm(a, w1s, w3s))


def _ffn_down_bwd(df, w2s, h1, h3, *, name, tm=FFN_TOKENS):
    n_tok = df.shape[0]
    tm = min(tm, n_tok)

    def body(df_ref, w2_ref, h1_ref, h3_ref, dh1_ref, dh3_ref):
        dg = lax.dot_general(df_ref[...].astype(MXU_DTYPE), w2_ref[0].astype(MXU_DTYPE), _NT, preferred_element_type=F32)
        h1v = h1_ref[0].astype(F32)
        h3v = h3_ref[0].astype(F32)
        sig = jax.nn.sigmoid(h1v)
        dh3_ref[0] = (dg * (h1v * sig)).astype(dh3_ref.dtype)
        dh1_ref[0] = (dg * h3v * (sig * (1.0 + h1v * (1.0 - sig)))).astype(dh1_ref.dtype)

    h_spec = pl.BlockSpec((1, tm, FF_SHARD), lambda s, i: (s, i, 0))
    h_shape = jax.ShapeDtypeStruct((N_SHARD, n_tok, FF_SHARD), MXU_DTYPE)
    blk = _nbytes((tm, D_MODEL), df.dtype) + _nbytes((FF_SHARD, D_MODEL), w2s.dtype) + 4 * _nbytes((tm, FF_SHARD), MXU_DTYPE)
    return pl.pallas_call(
        body, name=name, grid=(N_SHARD, n_tok // tm),
        in_specs=[pl.BlockSpec((tm, D_MODEL), lambda s, i: (i, 0)),
                  pl.BlockSpec((1, FF_SHARD, D_MODEL), lambda s, i: (s, 0, 0)), h_spec, h_spec],
        out_specs=(h_spec, h_spec), out_shape=(h_shape, h_shape),
        compiler_params=_params(blk, 8 * _nbytes((tm, FF_SHARD), F32)),
    )(*_hbm(df, w2s, h1, h3))


def _ffn_fwd(a, w1s, w3s, w2s, tag):
    h1, h3, g = _ffn_up(a, w1s, w3s, name=f"{tag}_up")
    f = _mm_act([(g, w2s)], "nn", reduce_shards=True, tm=FFN_TOKENS, name=f"{tag}_down")[0]
    return f, (h1, h3, g)


def _ffn_bwd(a, w1s, w3s, w2s, saved, df, tag):
    h1, h3, g = saved
    dh1, dh3 = _ffn_down_bwd(df, w2s, h1, h3, name=f"{tag}_down_bwd")
    da = _mm_act([(dh1, w1s), (dh3, w3s)], "nn", reduce_shards=True, tm=FFN_TOKENS, name=f"{tag}_up_bwd")[0]
    a3 = a[None]
    dw1 = _mm_tn(dh1, a3, tt=FFN_TOKENS, name=f"{tag}_dw1")
    dw3 = _mm_tn(dh3, a3, tt=FFN_TOKENS, name=f"{tag}_dw3")
    dw2 = _mm_tn(g, df[None], tt=FFN_TOKENS, name=f"{tag}_dw2")
    return da, dw1, dw3, dw2


CONV_LANES = 256


def _shift_down(x, d):
    if d == 0:
        return x
    row = lax.broadcasted_iota(jnp.int32, x.shape, 0)
    return jnp.where(row >= d, pltpu.roll(x, d, 0), 0.0)


def _shift_up(x, d):
    if d == 0:
        return x
    n = x.shape[0]
    row = lax.broadcasted_iota(jnp.int32, x.shape, 0)
    return jnp.where(row < n - d, pltpu.roll(x, n - d, 0), 0.0)


def _conv_pre(x, w):
    acc = None
    for j in range(CONV_WIDTH):
        term = w[j:j + 1, :] * _shift_down(x, CONV_WIDTH - 1 - j)
        acc = term if acc is None else acc + term
    return acc


def _conv_fwd(x, w, *, name):
    n_b, n_s, n_c = x.shape
    spec = pl.BlockSpec((1, n_s, CONV_LANES), lambda b, cj: (b, 0, cj))

    def body(x_ref, w_ref, o_ref):
        o_ref[0] = jax.nn.silu(_conv_pre(x_ref[0], w_ref[...]))

    return pl.pallas_call(
        body, name=name, grid=(n_b, n_c // CONV_LANES),
        in_specs=[spec, pl.BlockSpec((CONV_WIDTH, CONV_LANES), lambda b, cj: (0, cj))],
        out_specs=spec, out_shape=jax.ShapeDtypeStruct(x.shape, F32),
        compiler_params=_params(2 * _nbytes((n_s, CONV_LANES), F32), 6 * _nbytes((n_s, CONV_LANES), F32)),
    )(*_hbm(x, w))


def _conv_bwd(x, w, dout, *, name):
    n_b, n_s, n_c = x.shape
    spec = pl.BlockSpec((1, n_s, CONV_LANES), lambda cj, b: (b, 0, cj))
    w_spec = pl.BlockSpec((CONV_WIDTH, CONV_LANES), lambda cj, b: (0, cj))

    def body(x_ref, w_ref, do_ref, dx_ref, dw_ref):
        xv, wv = x_ref[0], w_ref[...]
        pre = _conv_pre(xv, wv)
        sig = jax.nn.sigmoid(pre)
        dpre = do_ref[0] * (sig * (1.0 + pre * (1.0 - sig)))
        dx = None
        first = pl.program_id(1) == 0
        for j in range(CONV_WIDTH):
            d = CONV_WIDTH - 1 - j
            term = wv[j:j + 1, :] * _shift_up(dpre, d)
            dx = term if dx is None else dx + term
            dwj = jnp.sum(dpre * _shift_down(xv, d), axis=0, keepdims=True)
            _accumulate(dw_ref.at[j:j + 1, :], dwj, first)
        dx_ref[0] = dx.astype(dx_ref.dtype)

    return pl.pallas_call(
        body, name=name, grid=(n_c // CONV_LANES, n_b),
        in_specs=[spec, w_spec, spec], out_specs=(spec, w_spec),
        out_shape=(jax.ShapeDtypeStruct(x.shape, MXU_DTYPE), jax.ShapeDtypeStruct((CONV_WIDTH, n_c), F32)),
        compiler_params=_params(3 * _nbytes((n_s, CONV_LANES), F32), 8 * _nbytes((n_s, CONV_LANES), F32)),
    )(*_hbm(x, w, dout))


_BNT = (((2,), (2,)), ((0,), (0,)))
_BNN = (((2,), (1,)), ((0,), (0,)))
_BTN = (((1,), (1,)), ((0,), (0,)))
DN_PREP_CHUNKS = 8
DN_SCAN_HEADS = 4
N_DOUBLINGS = 5


def _fdot(a, b, dims):
    return lax.dot_general(a, b, dims, precision=lax.Precision.HIGHEST, preferred_element_type=F32)


def _hdot(a, b, dims):
    return lax.dot_general(a, b, dims, precision=lax.Precision.HIGH, preferred_element_type=F32)


def _solve_by_doubling(a, rhs_u, rhs_w):
    row = lax.broadcasted_iota(jnp.int32, (CHUNK, CHUNK), 0)
    col = lax.broadcasted_iota(jnp.int32, (CHUNK, CHUNK), 1)
    inv = jnp.where(row == col, 1.0, 0.0) - a
    power = a
    for _ in range(N_DOUBLINGS):
        power = _hdot(power, power, _BNN)
        inv = inv + _hdot(inv, power, _BNN)
    return _hdot(inv, rhs_u, _BNN), _hdot(inv, rhs_w, _BNN), inv


@jax.custom_vjp
def _solve_saved(a, rhs_u, rhs_w, inv, u, w):
    return u, w


def _solve_saved_fwd(a, rhs_u, rhs_w, inv, u, w):
    return (u, w), (inv, u, w)


def _solve_saved_bwd(res, cts):
    inv, u, w = res
    gu = _hdot(inv, cts[0], _BTN)
    gw = _hdot(inv, cts[1], _BTN)
    da = -(_hdot(gu, u, _BNT) + _hdot(gw, w, _BNT))
    return da, gu, gw, jnp.zeros_like(inv), jnp.zeros_like(u), jnp.zeros_like(w)


_solve_saved.defvjp(_solve_saved_fwd, _solve_saved_bwd)


def _dn_prep_fn(solve, qc, kc, vc, bl, lac, lar, a_log, dt_bias):
    q = qc * lax.rsqrt(jnp.sum(qc * qc, axis=-1, keepdims=True) + EPS) * (DN_HEAD_DIM ** -0.5)
    k = kc * lax.rsqrt(jnp.sum(kc * kc, axis=-1, keepdims=True) + EPS)
    beta = jax.nn.sigmoid(bl)
    neg_a = -jnp.exp(a_log)
    lgc = neg_a * jax.nn.softplus(lac + dt_bias)
    lgr = neg_a * jax.nn.softplus(lar + dt_bias)
    row = lax.broadcasted_iota(jnp.int32, (CHUNK, CHUNK), 0)
    col = lax.broadcasted_iota(jnp.int32, (CHUNK, CHUNK), 1)
    causal, strict = row >= col, row > col
    g_c = jnp.sum(jnp.where(causal, lgr, 0.0), axis=-1, keepdims=True)
    g_r = jnp.sum(jnp.where(row <= col, lgc, 0.0), axis=-2, keepdims=True)
    decay = jnp.exp(jnp.where(causal, g_c - g_r, -jnp.inf))
    kb = k * beta
    a = jnp.where(strict, _mdot(kb, k, _BNT) * decay, 0.0)
    u, w, extra = solve(a, vc * beta, kb * jnp.exp(g_c))
    attn = _mdot(q, k, _BNT) * decay
    g_last = jnp.sum(lgc, axis=-2, keepdims=True)
    return q * jnp.exp(g_c), k * jnp.exp(g_last - g_c), u, w, attn, g_last, extra


def _dn_prep_specs(n_cb):
    tok = n_cb * CHUNK
    wide = pl.BlockSpec((1, 1, tok, DN_HEAD_DIM), lambda h, b, j: (b, h, j, 0))
    col = pl.BlockSpec((1, 1, tok, 1), lambda h, b, j: (b, h, j, 0))
    rowv = pl.BlockSpec((1, 1, n_cb, 1, CHUNK), lambda h, b, j: (b, h, j, 0, 0))
    one = pl.BlockSpec((1, 1, n_cb, 1, 1), lambda h, b, j: (b, h, j, 0, 0))
    head = pl.BlockSpec((1, 1, 1), lambda h, b, j: (h, 0, 0))
    return wide, col, rowv, one, head


def _dn_prep_load(n_cb, q_ref, k_ref, v_ref, bl_ref, lac_ref, lar_ref, al_ref, dt_ref):
    wide = lambda r: r[0, 0].reshape(n_cb, CHUNK, DN_HEAD_DIM)
    colv = lambda r: r[0, 0].reshape(n_cb, CHUNK, 1)
    return (wide(q_ref), wide(k_ref), wide(v_ref), colv(bl_ref), colv(lac_ref), lar_ref[0, 0], al_ref[...], dt_ref[...])


def _dn_prep(qh, kh, vh, bl, lac, lar, a_log, dt_bias, *, name):
    n_b, n_h, n_s, _ = qh.shape
    n_cb = min(DN_PREP_CHUNKS, n_s // CHUNK)
    tok = n_cb * CHUNK
    wide, col, rowv, one, head = _dn_prep_specs(n_cb)

    def body(*refs):
        outs = _dn_prep_fn(_solve_by_doubling, *_dn_prep_load(n_cb, *refs[:8]))
        for ref, val in zip(refs[8:13], outs[:5]):
            ref[0, 0] = val.reshape(tok, DN_HEAD_DIM)
        refs[13][0, 0] = outs[5]
        refs[14][0, 0] = outs[6].reshape(tok, DN_HEAD_DIM)

    big = jax.ShapeDtypeStruct(qh.shape, F32)
    return pl.pallas_call(
        body, name=name, grid=(n_h, n_b, n_s // tok),
        in_specs=[wide, wide, wide, col, col, rowv, head, head],
        out_specs=(wide, wide, wide, wide, wide, one, wide),
        out_shape=(big, big, big, big, big, jax.ShapeDtypeStruct((n_b, n_h, n_s // CHUNK, 1, 1), F32), big),
        compiler_params=_params(11 * _nbytes((tok, LANES), F32), 48 * _nbytes((tok, LANES), F32)),
    )(*_hbm(qh, kh, vh, bl, lac, lar, a_log, dt_bias))


def _dn_prep_bwd(qh, kh, vh, bl, lac, lar, a_log, dt_bias, inv, u, w, cts, *, name):
    n_b, n_h, n_s, _ = qh.shape
    n_cb = min(DN_PREP_CHUNKS, n_s // CHUNK)
    tok = n_cb * CHUNK
    wide, col, rowv, one, head = _dn_prep_specs(n_cb)

    def body(*refs):
        prim = _dn_prep_load(n_cb, *refs[:8])
        chunks = lambda r: r[0, 0].reshape(n_cb, CHUNK, DN_HEAD_DIM)
        inv_v, u_v, w_v = chunks(refs[8]), chunks(refs[9]), chunks(refs[10])
        ct = tuple(chunks(r) for r in refs[11:16]) + (refs[16][0, 0],)

        def fn(*args):
            solve = lambda a, ru, rw: _solve_saved(a, ru, rw, inv_v, u_v, w_v) + (None,)
            return _dn_prep_fn(solve, *args)[:6]

        _, pull = jax.vjp(fn, *prim)
        dq, dk, dv, dbl, dlac, dlar, dal, ddt = pull(ct)
        outs = refs[17:]
        for ref, val in zip(outs[:3], (dq, dk, dv)):
            ref[0, 0] = val.reshape(tok, DN_HEAD_DIM)
        outs[3][0, 0] = dbl.reshape(tok, 1)
        outs[4][0, 0] = dlac.reshape(tok, 1)
        outs[5][0, 0] = dlar
        first = jnp.logical_and(pl.program_id(1) == 0, pl.program_id(2) == 0)
        _accumulate(outs[6], dal, first)
        _accumulate(outs[7], ddt, first)

    big = jax.ShapeDtypeStruct(qh.shape, F32)
    return pl.pallas_call(
        body, name=name, grid=(n_h, n_b, n_s // tok),
        in_specs=[wide, wide, wide, col, col, rowv, head, head, wide, wide, wide, wide, wide, wide, wide, wide, one],
        out_specs=(wide, wide, wide, col, col, rowv, head, head),
        out_shape=(big, big, big, jax.ShapeDtypeStruct(bl.shape, F32), jax.ShapeDtypeStruct(lac.shape, F32),
                   jax.ShapeDtypeStruct(lar.shape, F32), jax.ShapeDtypeStruct(a_log.shape, F32),
                   jax.ShapeDtypeStruct(dt_bias.shape, F32)),
        compiler_params=_params(21 * _nbytes((tok, LANES), F32), 64 * _nbytes((tok, LANES), F32)),
    )(*_hbm(qh, kh, vh, bl, lac, lar, a_log, dt_bias, inv, u, w, *cts))


def _dn_step(state, q, k, u, w, a, gl):
    v_new = u - _mdot(w, state, _BNN)
    o = _mdot(q, state, _BNN) + _mdot(a, v_new, _BNN)
    return state * jnp.exp(gl) + _mdot(k, v_new, _BTN), o


def _dn_scan_specs(n_cb, n_blocks, reverse):
    tok = n_cb * CHUNK
    jj = (lambda j: n_blocks - 1 - j) if reverse else (lambda j: j)
    wide = pl.BlockSpec((1, DN_SCAN_HEADS, tok, DN_HEAD_DIM), lambda b, h, j: (b, h, jj(j), 0))
    one = pl.BlockSpec((1, DN_SCAN_HEADS, n_cb, 1, 1), lambda b, h, j: (b, h, jj(j), 0, 0))
    st = pl.BlockSpec((1, DN_SCAN_HEADS, n_cb, DN_HEAD_DIM, DN_HEAD_DIM), lambda b, h, j: (b, h, jj(j), 0, 0))
    return wide, one, st


def _dn_scan(qd, kd, u, w, attn, g_last, *, name):
    n_b, n_h, n_s, _ = qd.shape
    n_cb = min(DN_PREP_CHUNKS, n_s // CHUNK)
    n_blocks = n_s // (n_cb * CHUNK)
    wide, one, st = _dn_scan_specs(n_cb, n_blocks, False)

    def body(qd_ref, kd_ref, u_ref, w_ref, a_ref, gl_ref, o_ref, st_ref, state_ref):
        @pl.when(pl.program_id(2) == 0)
        def _():
            state_ref[...] = jnp.zeros(state_ref.shape, F32)

        def step(n, state):
            rows = pl.ds(pl.multiple_of(n * CHUNK, CHUNK), CHUNK)
            st_ref[0, :, n] = state
            state, o = _dn_step(state, qd_ref[0, :, rows, :], kd_ref[0, :, rows, :], u_ref[0, :, rows, :],
                                w_ref[0, :, rows, :], a_ref[0, :, rows, :], gl_ref[0, :, n])
            o_ref[0, :, rows, :] = o
            return state

        state_ref[...] = lax.fori_loop(0, n_cb, step, state_ref[...])

    return pl.pallas_call(
        body, name=name, grid=(n_b, n_h // DN_SCAN_HEADS, n_blocks),
        in_specs=[wide, wide, wide, wide, wide, one], out_specs=(wide, st),
        out_shape=(jax.ShapeDtypeStruct(qd.shape, F32),
                   jax.ShapeDtypeStruct((n_b, n_h, n_s // CHUNK, DN_HEAD_DIM, DN_HEAD_DIM), F32)),
        scratch_shapes=[pltpu.VMEM((DN_SCAN_HEADS, DN_HEAD_DIM, DN_HEAD_DIM), F32)],
        compiler_params=_params(8 * _nbytes((DN_SCAN_HEADS, n_cb * CHUNK, LANES), F32), 8 << 20),
    )(*_hbm(qd, kd, u, w, attn, g_last))


def _dn_scan_bwd(qd, kd, u, w, attn, g_last, states, do, *, name):
    n_b, n_h, n_s, _ = qd.shape
    n_cb = min(DN_PREP_CHUNKS, n_s // CHUNK)
    n_blocks = n_s // (n_cb * CHUNK)
    wide, one, st = _dn_scan_specs(n_cb, n_blocks, True)

    def body(qd_ref, kd_ref, u_ref, w_ref, a_ref, gl_ref, st_ref, do_ref,
             dq_ref, dk_ref, du_ref, dw_ref, da_ref, dgl_ref, dstate_ref):
        @pl.when(pl.program_id(2) == 0)
        def _():
            dstate_ref[...] = jnp.zeros(dstate_ref.shape, F32)

        def step(i, dstate):
            n = n_cb - 1 - i
            rows = pl.ds(pl.multiple_of(n * CHUNK, CHUNK), CHUNK)
            _, pull = jax.vjp(_dn_step, st_ref[0, :, n], qd_ref[0, :, rows, :], kd_ref[0, :, rows, :],
                              u_ref[0, :, rows, :], w_ref[0, :, rows, :], a_ref[0, :, rows, :], gl_ref[0, :, n])
            dstate, dq, dk, du, dw, da, dgl = pull((dstate, do_ref[0, :, rows, :]))
            dq_ref[0, :, rows, :] = dq
            dk_ref[0, :, rows, :] = dk
            du_ref[0, :, rows, :] = du
            dw_ref[0, :, rows, :] = dw
            da_ref[0, :, rows, :] = da
            dgl_ref[0, :, n] = dgl
            return dstate

        dstate_ref[...] = lax.fori_loop(0, n_cb, step, dstate_ref[...])

    big = jax.ShapeDtypeStruct(qd.shape, F32)
    return pl.pallas_call(
        body, name=name, grid=(n_b, n_h // DN_SCAN_HEADS, n_blocks),
        in_specs=[wide, wide, wide, wide, wide, one, st, wide],
        out_specs=(wide, wide, wide, wide, wide, one),
        out_shape=(big, big, big, big, big, jax.ShapeDtypeStruct(g_last.shape, F32)),
        scratch_shapes=[pltpu.VMEM((DN_SCAN_HEADS, DN_HEAD_DIM, DN_HEAD_DIM), F32)],
        compiler_params=_params(13 * _nbytes((DN_SCAN_HEADS, n_cb * CHUNK, LANES), F32), 8 << 20),
    )(*_hbm(qd, kd, u, w, attn, g_last, states, do))


def _dn_post_fn(o, z, gain):
    return o * lax.rsqrt(jnp.mean(o * o, axis=-1, keepdims=True) + EPS) * gain * jax.nn.silu(z)


_HEAD_ROWS = lambda n_s: pl.BlockSpec((1, 1, n_s, DN_HEAD_DIM), lambda b, h: (b, h, 0, 0))
_HEAD_GAIN = pl.BlockSpec((1, DN_HEAD_DIM), lambda b, h: (0, 0))


def _dn_post(o, z, gain, *, name):
    n_b, n_h, n_s, _ = o.shape

    def body(o_ref, z_ref, g_ref, out_ref):
        out_ref[0, 0] = _dn_post_fn(o_ref[0, 0], z_ref[0, 0], g_ref[...]).astype(out_ref.dtype)

    rows = _HEAD_ROWS(n_s)
    return pl.pallas_call(
        body, name=name, grid=(n_b, n_h), in_specs=[rows, rows, _HEAD_GAIN], out_specs=rows,
        out_shape=jax.ShapeDtypeStruct(o.shape, MXU_DTYPE),
        compiler_params=_params(3 * _nbytes((n_s, LANES), F32), 6 * _nbytes((n_s, LANES), F32)),
    )(*_hbm(o, z, gain))


def _dn_post_bwd(o, z, gain, dout, *, name):
    n_b, n_h, n_s, _ = o.shape

    def body(o_ref, z_ref, g_ref, dout_ref, do_ref, dz_ref, dg_ref):
        _, pull = jax.vjp(_dn_post_fn, o_ref[0, 0], z_ref[0, 0], g_ref[...])
        do, dz, dg = pull(dout_ref[0, 0].astype(F32))
        do_ref[0, 0] = do
        dz_ref[0, 0] = dz.astype(dz_ref.dtype)
        _accumulate(dg_ref, dg, jnp.logical_and(pl.program_id(0) == 0, pl.program_id(1) == 0))

    rows = _HEAD_ROWS(n_s)
    return pl.pallas_call(
        body, name=name, grid=(n_b, n_h), in_specs=[rows, rows, _HEAD_GAIN, rows],
        out_specs=(rows, rows, _HEAD_GAIN),
        out_shape=(jax.ShapeDtypeStruct(o.shape, F32), jax.ShapeDtypeStruct(o.shape, MXU_DTYPE),
                   jax.ShapeDtypeStruct((1, DN_HEAD_DIM), F32)),
        compiler_params=_params(5 * _nbytes((n_s, LANES), F32), 10 * _nbytes((n_s, LANES), F32)),
    )(*_hbm(o, z, gain, dout))


S5_SCAN_LANES = 256
TILE_ROWS = SUBLANES


def _s5_prep_fn(lam_re, lam_im, log_step, bt_re, bt_im, c_im):
    lr = jnp.minimum(lam_re, -1e-4)
    step = jnp.exp(log_step)
    mag = jnp.exp(lr * step)
    ang = lam_im * step
    lb_re = mag * jnp.cos(ang)
    lb_im = mag * jnp.sin(ang)
    den = lr * lr + lam_im * lam_im
    coef_re = ((lb_re - 1.0) * lr + lb_im * lam_im) / den
    coef_im = (lb_im * lr - (lb_re - 1.0) * lam_im) / den
    return (lb_re, lb_im, coef_re * bt_re - coef_im * bt_im, coef_re * bt_im + coef_im * bt_re, -c_im)


def _s5_prep(lam_re, lam_im, log_step, bt_re, bt_im, c_im, *, name):
    def body(*refs):
        outs = _s5_prep_fn(*(r[...] for r in refs[:6]))
        for ref, val in zip(refs[6:], outs):
            ref[...] = val

    vec = jax.ShapeDtypeStruct(lam_re.shape, F32)
    mat = jax.ShapeDtypeStruct(bt_re.shape, F32)
    return pl.pallas_call(body, name=name, out_shape=(vec, vec, mat, mat, mat))(lam_re, lam_im, log_step, bt_re, bt_im, c_im)


def _s5_prep_bwd(lam_re, lam_im, log_step, bt_re, bt_im, c_im, cts, *, name):
    def body(*refs):
        _, pull = jax.vjp(_s5_prep_fn, *(r[...] for r in refs[:6]))
        grads = pull(tuple(r[...] for r in refs[6:11]))
        for ref, val in zip(refs[11:], grads):
            ref[...] = val

    shapes = tuple(jax.ShapeDtypeStruct(a.shape, F32) for a in (lam_re, lam_im, log_step, bt_re, bt_im, c_im))
    return pl.pallas_call(body, name=name, out_shape=shapes)(lam_re, lam_im, log_step, bt_re, bt_im, c_im, *cts)


def _cmul(ar, ai, br, bi):
    return ar * br - ai * bi, ar * bi + ai * br


def _s5_powers(lr, li):
    pows = [(lr, li)]
    for _ in range(TILE_ROWS - 1):
        pows.append(_cmul(pows[-1][0], pows[-1][1], lr, li))
    return pows


def _s5_carry_table(pows, n_lanes, reverse):
    row = lax.broadcasted_iota(jnp.int32, (TILE_ROWS, n_lanes), 0)
    t_re = jnp.zeros((TILE_ROWS, n_lanes), F32)
    t_im = jnp.zeros((TILE_ROWS, n_lanes), F32)
    for r in range(TILE_ROWS):
        p_re, p_im = pows[TILE_ROWS - 1 - r] if reverse else pows[r]
        t_re = jnp.where(row == r, p_re, t_re)
        t_im = jnp.where(row == r, p_im, t_im)
    return t_re, t_im


def _s5_tile(y_re, y_im, pows, reverse):
    d = 1
    while d < TILE_ROWS:
        p_re, p_im = pows[d - 1]
        if reverse:
            s_re, s_im = _shift_up(y_re, d), _shift_up(y_im, d)
        else:
            s_re, s_im = _shift_down(y_re, d), _shift_down(y_im, d)
        m_re, m_im = _cmul(p_re, p_im, s_re, s_im)
        y_re, y_im = y_re + m_re, y_im + m_im
        d *= 2
    return y_re, y_im


S5_BLOCKS = N_SHARD
S5_BLOCK_CH = S5_WIDTH // S5_BLOCKS
S5_BLOCK_LANES = S5_LANES // S5_BLOCKS
SCAN_PER_BLOCK = S5_BLOCK_LANES // S5_SCAN_LANES


def _s5_scan_specs(n_s, order):
    L = S5_SCAN_LANES

    def cat_spec(part):
        return pl.BlockSpec((1, 1, n_s, L), lambda *g: (order(*g)[1] // SCAN_PER_BLOCK, order(*g)[0], 0,
                                                        part * SCAN_PER_BLOCK + order(*g)[1] % SCAN_PER_BLOCK))

    one = pl.BlockSpec((1, 1, n_s, L), lambda *g: (order(*g)[1] // SCAN_PER_BLOCK, order(*g)[0], 0,
                                                   order(*g)[1] % SCAN_PER_BLOCK))
    lam = pl.BlockSpec((1, L), lambda *g: (0, order(*g)[1]))
    return cat_spec, one, lam


def _s5_scan(bu, lb_re, lb_im, *, name):
    n_blk, n_b, n_s, _ = bu.shape
    n_lb = S5_LANES // S5_SCAN_LANES
    n_tiles = n_s // TILE_ROWS
    L = S5_SCAN_LANES

    def body(re_ref, im_ref, lr_ref, li_ref, xr_ref, xi_ref):
        pows = _s5_powers(lr_ref[...], li_ref[...])
        t_re, t_im = _s5_carry_table(pows, L, False)

        def step(i, carry):
            rows = pl.ds(pl.multiple_of(i * TILE_ROWS, TILE_ROWS), TILE_ROWS)
            y_re, y_im = _s5_tile(re_ref[0, 0, rows, :], im_ref[0, 0, rows, :], pows, False)
            c_re, c_im = _cmul(t_re, t_im, carry[0], carry[1])
            y_re, y_im = y_re + c_re, y_im + c_im
            xr_ref[0, 0, rows, :] = y_re
            xi_ref[0, 0, rows, :] = y_im
            return y_re[TILE_ROWS - 1:, :], y_im[TILE_ROWS - 1:, :]

        zero = jnp.zeros((1, L), F32)
        lax.fori_loop(0, n_tiles, step, (zero, zero))

    cat_spec, one, lam = _s5_scan_specs(n_s, lambda b, j: (b, j))
    x_shape = jax.ShapeDtypeStruct((n_blk, n_b, n_s, S5_BLOCK_LANES), F32)
    return pl.pallas_call(
        body, name=name, grid=(n_b, n_lb),
        in_specs=[cat_spec(0), cat_spec(1), lam, lam],
        out_specs=(one, one), out_shape=(x_shape, x_shape),
        compiler_params=_params(4 * _nbytes((n_s, L), F32), 4 << 20),
    )(*_hbm(bu, bu, lb_re, lb_im))


def _s5_scan_bwd(dx, x_re, x_im, lb_re, lb_im, *, name):
    n_blk, n_b, n_s, _ = dx.shape
    n_lb = S5_LANES // S5_SCAN_LANES
    n_tiles = n_s // TILE_ROWS
    L = S5_SCAN_LANES

    def body(dr_ref, di_ref, xr_ref, xi_ref, lr_ref, li_ref, ar_ref, ai_ref, dlr_ref, dli_ref):
        pows = _s5_powers(lr_ref[...], -li_ref[...])
        t_re, t_im = _s5_carry_table(pows, L, True)
        row = lax.broadcasted_iota(jnp.int32, (TILE_ROWS, L), 0)

        def step(k, carry):
            c_re, c_im, s_re, s_im = carry
            i = n_tiles - 1 - k
            rows = pl.ds(pl.multiple_of(i * TILE_ROWS, TILE_ROWS), TILE_ROWS)
            a_re, a_im = _s5_tile(dr_ref[0, 0, rows, :], di_ref[0, 0, rows, :], pows, True)
            m_re, m_im = _cmul(t_re, t_im, c_re, c_im)
            a_re, a_im = a_re + m_re, a_im + m_im
            ar_ref[0, 0, rows, :] = a_re.astype(ar_ref.dtype)
            ai_ref[0, 0, rows, :] = a_im.astype(ai_ref.dtype)
            prev = pl.ds(pl.multiple_of(jnp.maximum(i - 1, 0) * TILE_ROWS, TILE_ROWS), TILE_ROWS)
            keep = jnp.where(i > 0, 1.0, 0.0)
            last_re = xr_ref[0, 0, prev, :][TILE_ROWS - 1:, :] * keep
            last_im = xi_ref[0, 0, prev, :][TILE_ROWS - 1:, :] * keep
            xp_re = jnp.where(row == 0, last_re, _shift_down(xr_ref[0, 0, rows, :], 1))
            xp_im = jnp.where(row == 0, last_im, _shift_down(xi_ref[0, 0, rows, :], 1))
            s_re = s_re + a_re * xp_re + a_im * xp_im
            s_im = s_im + a_im * xp_re - a_re * xp_im
            return a_re[:1, :], a_im[:1, :], s_re, s_im

        zero = jnp.zeros((1, L), F32)
        zt = jnp.zeros((TILE_ROWS, L), F32)
        _, _, s_re, s_im = lax.fori_loop(0, n_tiles, step, (zero, zero, zt, zt))
        first = pl.program_id(1) == 0
        _accumulate(dlr_ref, jnp.sum(s_re, axis=0, keepdims=True), first)
        _accumulate(dli_ref, jnp.sum(s_im, axis=0, keepdims=True), first)

    cat_spec, one, lam = _s5_scan_specs(n_s, lambda j, b: (b, j))
    a_shape = jax.ShapeDtypeStruct((n_blk, n_b, n_s, S5_BLOCK_LANES), MXU_DTYPE)
    lam_shape = jax.ShapeDtypeStruct((1, S5_LANES), F32)
    return pl.pallas_call(
        body, name=name, grid=(n_lb, n_b),
        in_specs=[cat_spec(0), cat_spec(1), one, one, lam, lam],
        out_specs=(one, one, lam, lam),
        out_shape=(a_shape, a_shape, lam_shape, lam_shape),
        compiler_params=_params(5 * _nbytes((n_s, L), F32), 4 << 20),
    )(*_hbm(dx, dx, x_re, x_im, lb_re, lb_im))


def _scan_rows(i):
    return pl.ds(pl.multiple_of(i * TILE_ROWS, TILE_ROWS), TILE_ROWS)


def _s5_mix_specs(n_s, order):
    jb = lambda *g: order(*g)[0]
    bb = lambda *g: order(*g)[1]
    act = pl.BlockSpec((1, 1, n_s, S5_BLOCK_CH), lambda *g: (bb(*g), 0, 0, jb(*g)))
    state = pl.BlockSpec((1, 1, n_s, S5_BLOCK_LANES), lambda *g: (jb(*g), bb(*g), 0, 0))
    lam = pl.BlockSpec((1, S5_BLOCK_LANES), lambda *g: (0, jb(*g)))
    w_in = pl.BlockSpec((1, S5_BLOCK_CH, S5_BLOCK_LANES), lambda *g: (jb(*g), 0, 0))
    w_out = pl.BlockSpec((1, S5_BLOCK_LANES, S5_BLOCK_CH), lambda *g: (jb(*g), 0, 0))
    return act, state, lam, w_in, w_out


def _s5_mix(u, wb_re, wb_im, lb_re, lb_im, wc_re, wc_im, *, name):
    n_b, n_s, _ = u.shape
    n_blk = S5_BLOCKS
    lanes = lambda t: t[:, None]
    n_tiles = n_s // TILE_ROWS
    L = S5_BLOCK_LANES

    def body(u_ref, wbr_ref, wbi_ref, lr_ref, li_ref, wcr_ref, wci_ref, y_ref, xr_ref, xi_ref):
        uv = u_ref[0, 0].astype(MXU_DTYPE)
        xr_ref[0, 0] = lax.dot_general(uv, wbr_ref[0].astype(MXU_DTYPE), _NN, preferred_element_type=F32)
        xi_ref[0, 0] = lax.dot_general(uv, wbi_ref[0].astype(MXU_DTYPE), _NN, preferred_element_type=F32)
        pows = _s5_powers(lr_ref[...], li_ref[...])
        t_re, t_im = _s5_carry_table(pows, L, False)

        def step(i, carry):
            rows = _scan_rows(i)
            y_re, y_im = _s5_tile(xr_ref[0, 0, rows, :], xi_ref[0, 0, rows, :], pows, False)
            c_re, c_im = _cmul(t_re, t_im, carry[0], carry[1])
            y_re, y_im = y_re + c_re, y_im + c_im
            xr_ref[0, 0, rows, :] = y_re
            xi_ref[0, 0, rows, :] = y_im
            return y_re[TILE_ROWS - 1:, :], y_im[TILE_ROWS - 1:, :]

        zero = jnp.zeros((1, L), F32)
        lax.fori_loop(0, n_tiles, step, (zero, zero))
        y_ref[0, 0] = (
            lax.dot_general(xr_ref[0, 0].astype(MXU_DTYPE), wcr_ref[0].astype(MXU_DTYPE), _NN, preferred_element_type=F32)
            + lax.dot_general(xi_ref[0, 0].astype(MXU_DTYPE), wci_ref[0].astype(MXU_DTYPE), _NN, preferred_element_type=F32))

    act, state, lam, w_in, w_out = _s5_mix_specs(n_s, lambda b, j: (j, b))
    x_shape = jax.ShapeDtypeStruct((n_blk, n_b, n_s, L), F32)
    return pl.pallas_call(
        body, name=name, grid=(n_b, n_blk),
        in_specs=[act, w_in, w_in, lam, lam, w_out, w_out], out_specs=(act, state, state),
        out_shape=(jax.ShapeDtypeStruct((n_b, 1, n_s, S5_WIDTH), F32), x_shape, x_shape),
        compiler_params=_params(2 * _nbytes((n_s, L), F32) + 2 * _nbytes((n_s, S5_BLOCK_CH), F32), 3 * _nbytes((n_s, L), F32)),
    )(*_hbm(lanes(u), wb_re, wb_im, lb_re, lb_im, wc_re, wc_im))


def _s5_mix_bwd(dy, du_skip, u, x_re, x_im, wb_re, wb_im, lb_re, lb_im, wc_re, wc_im, *, name):
    n_b, n_s, _ = u.shape
    n_blk = S5_BLOCKS
    lanes = lambda t: t[:, None]
    n_tiles = n_s // TILE_ROWS
    L = S5_BLOCK_LANES

    def body(dy_ref, ds_ref, u_ref, xr_ref, xi_ref, wbr_ref, wbi_ref, lr_ref, li_ref, wcr_ref, wci_ref,
             du_ref, dwbr_ref, dwbi_ref, dlr_ref, dli_ref, dwcr_ref, dwci_ref, ar_ref, ai_ref):
        dyv = dy_ref[0, 0].astype(MXU_DTYPE)
        ar_ref[...] = lax.dot_general(dyv, wcr_ref[0].astype(MXU_DTYPE), _NT, preferred_element_type=F32)
        ai_ref[...] = lax.dot_general(dyv, wci_ref[0].astype(MXU_DTYPE), _NT, preferred_element_type=F32)
        pows = _s5_powers(lr_ref[...], -li_ref[...])
        t_re, t_im = _s5_carry_table(pows, L, True)
        row = lax.broadcasted_iota(jnp.int32, (TILE_ROWS, L), 0)

        def step(k, carry):
            c_re, c_im, s_re, s_im = carry
            i = n_tiles - 1 - k
            rows = _scan_rows(i)
            a_re, a_im = _s5_tile(ar_ref[rows, :], ai_ref[rows, :], pows, True)
            m_re, m_im = _cmul(t_re, t_im, c_re, c_im)
            a_re, a_im = a_re + m_re, a_im + m_im
            ar_ref[rows, :] = a_re
            ai_ref[rows, :] = a_im
            prev = _scan_rows(jnp.maximum(i - 1, 0))
            keep = jnp.where(i > 0, 1.0, 0.0)
            last_re = xr_ref[0, 0, prev, :][TILE_ROWS - 1:, :] * keep
            last_im = xi_ref[0, 0, prev, :][TILE_ROWS - 1:, :] * keep
            xp_re = jnp.where(row == 0, last_re, _shift_down(xr_ref[0, 0, rows, :], 1))
            xp_im = jnp.where(row == 0, last_im, _shift_down(xi_ref[0, 0, rows, :], 1))
            s_re = s_re + a_re * xp_re + a_im * xp_im
            s_im = s_im + a_im * xp_re - a_re * xp_im
            return a_re[:1, :], a_im[:1, :], s_re, s_im

        zero = jnp.zeros((1, L), F32)
        zt = jnp.zeros((TILE_ROWS, L), F32)
        _, _, s_re, s_im = lax.fori_loop(0, n_tiles, step, (zero, zero, zt, zt))
        first = pl.program_id(1) == 0
        _accumulate(dlr_ref, jnp.sum(s_re, axis=0, keepdims=True), first)
        _accumulate(dli_ref, jnp.sum(s_im, axis=0, keepdims=True), first)
        a_re, a_im = ar_ref[...].astype(MXU_DTYPE), ai_ref[...].astype(MXU_DTYPE)
        du = (lax.dot_general(a_re, wbr_ref[0].astype(MXU_DTYPE), _NT, preferred_element_type=F32)
              + lax.dot_general(a_im, wbi_ref[0].astype(MXU_DTYPE), _NT, preferred_element_type=F32))
        du_ref[0, 0] = (du + ds_ref[0, 0]).astype(du_ref.dtype)
        uv = u_ref[0, 0].astype(MXU_DTYPE)
        _accumulate(dwbr_ref, lax.dot_general(uv, a_re, _TN, preferred_element_type=F32)[None], first)
        _accumulate(dwbi_ref, lax.dot_general(uv, a_im, _TN, preferred_element_type=F32)[None], first)
        _accumulate(dwcr_ref, lax.dot_general(xr_ref[0, 0].astype(MXU_DTYPE), dyv, _TN, preferred_element_type=F32)[None], first)
        _accumulate(dwci_ref, lax.dot_general(xi_ref[0, 0].astype(MXU_DTYPE), dyv, _TN, preferred_element_type=F32)[None], first)

    act, state, lam, w_in, w_out = _s5_mix_specs(n_s, lambda j, b: (j, b))
    lam_shape = jax.ShapeDtypeStruct((1, S5_LANES), F32)
    return pl.pallas_call(
        body, name=name, grid=(n_blk, n_b),
        in_specs=[act, act, act, state, state, w_in, w_in, lam, lam, w_out, w_out],
        out_specs=(act, w_in, w_in, lam, lam, w_out, w_out),
        out_shape=(jax.ShapeDtypeStruct((n_b, 1, n_s, S5_WIDTH), MXU_DTYPE), jax.ShapeDtypeStruct(wb_re.shape, F32),
                   jax.ShapeDtypeStruct(wb_im.shape, F32), lam_shape, lam_shape,
                   jax.ShapeDtypeStruct(wc_re.shape, F32), jax.ShapeDtypeStruct(wc_im.shape, F32)),
        scratch_shapes=[pltpu.VMEM((n_s, L), F32), pltpu.VMEM((n_s, L), F32)],
        compiler_params=_params(2 * _nbytes((n_s, L), F32) + 4 * _nbytes((n_s, S5_BLOCK_CH), F32), 5 * _nbytes((n_s, L), F32)),
    )(*_hbm(lanes(dy), lanes(du_skip), lanes(u), x_re, x_im, wb_re, wb_im, lb_re, lb_im, wc_re, wc_im))


def _s5_out_fn(ymm, u, d_skip, w_glu, b_glu):
    y = jax.nn.gelu(ymm + d_skip * u)
    return y * jax.nn.sigmoid(_mdot(y, w_glu, _NN) + b_glu)


def _s5_out_specs(tm):
    rows = pl.BlockSpec((tm, S5_WIDTH), lambda i: (i, 0))
    vec = pl.BlockSpec((1, S5_WIDTH), lambda i: (0, 0))
    mat = pl.BlockSpec((S5_WIDTH, S5_WIDTH), lambda i: (0, 0))
    return rows, vec, mat


def _s5_out(ymm, u, d_skip, w_glu, b_glu, *, name, tm=512):
    n_tok = ymm.shape[0]
    tm = min(tm, n_tok)
    rows, vec, mat = _s5_out_specs(tm)

    def body(y_ref, u_ref, d_ref, w_ref, b_ref, o_ref):
        o_ref[...] = _s5_out_fn(y_ref[...], u_ref[...], d_ref[...], w_ref[...], b_ref[...]).astype(o_ref.dtype)

    return pl.pallas_call(
        body, name=name, grid=(n_tok // tm,), in_specs=[rows, rows, vec, mat, vec], out_specs=rows,
        out_shape=jax.ShapeDtypeStruct((n_tok, S5_WIDTH), MXU_DTYPE),
        compiler_params=_params(4 * _nbytes((tm, S5_WIDTH), F32), 8 * _nbytes((tm, S5_WIDTH), F32)),
    )(*_hbm(ymm, u, d_skip, w_glu, b_glu))


def _s5_out_bwd(ymm, u, d_skip, w_glu, b_glu, dout, *, name, tm=512):
    n_tok = ymm.shape[0]
    tm = min(tm, n_tok)
    rows, vec, mat = _s5_out_specs(tm)

    def body(y_ref, u_ref, d_ref, w_ref, b_ref, do_ref, dy_ref, du_ref, dd_ref, dw_ref, db_ref):
        _, pull = jax.vjp(_s5_out_fn, y_ref[...], u_ref[...], d_ref[...], w_ref[...].astype(F32), b_ref[...])
        dy, du, dd, dw, db = pull(do_ref[...])
        dy_ref[...] = dy.astype(dy_ref.dtype)
        du_ref[...] = du
        first = pl.program_id(0) == 0
        _accumulate(dd_ref, dd, first)
        _accumulate(dw_ref, dw, first)
        _accumulate(db_ref, db, first)

    return pl.pallas_call(
        body, name=name, grid=(n_tok // tm,), in_specs=[rows, rows, vec, mat, vec, rows],
        out_specs=(rows, rows, vec, mat, vec),
        out_shape=(jax.ShapeDtypeStruct(ymm.shape, MXU_DTYPE), jax.ShapeDtypeStruct(ymm.shape, F32),
                   jax.ShapeDtypeStruct((1, S5_WIDTH), F32), jax.ShapeDtypeStruct((S5_WIDTH, S5_WIDTH), F32),
                   jax.ShapeDtypeStruct((1, S5_WIDTH), F32)),
        compiler_params=_params(6 * _nbytes((tm, S5_WIDTH), F32), 12 * _nbytes((tm, S5_WIDTH), F32)),
    )(*_hbm(ymm, u, d_skip, w_glu, b_glu, dout))


def _merge_fn(ga, gb, ya, yb):
    return jax.nn.sigmoid(ga) * ya + jax.nn.sigmoid(gb) * yb


def _merge(gab, ya, yb, *, name, tm=512):
    n_tok = ya.shape[0]
    tm = min(tm, n_tok)
    rows = pl.BlockSpec((tm, D_MODEL), lambda i: (i, 0))

    def body(ga_ref, gb_ref, ya_ref, yb_ref, o_ref):
        o_ref[...] = _merge_fn(ga_ref[...], gb_ref[...], ya_ref[...], yb_ref[...]).astype(o_ref.dtype)

    return pl.pallas_call(
        body, name=name, grid=(n_tok // tm,),
        in_specs=[rows, pl.BlockSpec((tm, D_MODEL), lambda i: (i, 1)), rows, rows], out_specs=rows,
        out_shape=jax.ShapeDtypeStruct(ya.shape, MXU_DTYPE),
        compiler_params=_params(5 * _nbytes((tm, D_MODEL), F32), 4 * _nbytes((tm, D_MODEL), F32)),
    )(*_hbm(gab, gab, ya, yb))


def _merge_bwd(gab, ya, yb, dout, *, name, tm=512):
    n_tok = ya.shape[0]
    tm = min(tm, n_tok)
    rows = pl.BlockSpec((tm, D_MODEL), lambda i: (i, 0))

    def body(ga_ref, gb_ref, ya_ref, yb_ref, do_ref, *out_refs):
        _, pull = jax.vjp(_merge_fn, ga_ref[...], gb_ref[...], ya_ref[...], yb_ref[...])
        for ref, val in zip(out_refs, pull(do_ref[...])):
            ref[...] = val.astype(ref.dtype)

    shape = jax.ShapeDtypeStruct(ya.shape, MXU_DTYPE)
    return pl.pallas_call(
        body, name=name, grid=(n_tok // tm,),
        in_specs=[rows, pl.BlockSpec((tm, D_MODEL), lambda i: (i, 1)), rows, rows, rows],
        out_specs=(rows, rows, rows, rows), out_shape=(shape, shape, shape, shape),
        compiler_params=_params(7 * _nbytes((tm, D_MODEL), F32), 6 * _nbytes((tm, D_MODEL), F32)),
    )(*_hbm(gab, gab, ya, yb, dout))


ADA_SHARD = N_MOD * D_MODEL // N_SHARD


def _ada_fwd(c_pad, w_s, b_s, *, name):
    n_r = c_pad.shape[0]

    def body(c_ref, w_ref, b_ref, o_ref):
        sc = jax.nn.silu(c_ref[...]).astype(MXU_DTYPE)
        o_ref[0] = lax.dot_general(sc, w_ref[0].astype(MXU_DTYPE), _NN, preferred_element_type=F32) + b_ref[0]

    return pl.pallas_call(
        body, name=name, grid=(N_SHARD,),
        in_specs=[pl.BlockSpec((n_r, D_MODEL), lambda s: (0, 0)),
                  pl.BlockSpec((1, D_MODEL, ADA_SHARD), lambda s: (s, 0, 0)),
                  pl.BlockSpec((1, 1, ADA_SHARD), lambda s: (s, 0, 0))],
        out_specs=pl.BlockSpec((1, n_r, ADA_SHARD), lambda s: (s, 0, 0)),
        out_shape=jax.ShapeDtypeStruct((N_SHARD, n_r, ADA_SHARD), F32),
        compiler_params=_params(_nbytes((D_MODEL, ADA_SHARD), w_s.dtype), 1 << 20),
    )(*_hbm(c_pad, w_s, b_s))


def _ada_bwd(c_pad, dmod_s, *, name):
    n_r = c_pad.shape[0]

    def body(c_ref, d_ref, dw_ref, db_ref):
        sc = jax.nn.silu(c_ref[...])
        dm = d_ref[0]
        dw_ref[0] = _fdot(sc, dm, _TN)
        db_ref[0] = jnp.sum(dm, axis=0, keepdims=True)

    return pl.pallas_call(
        body, name=name, grid=(N_SHARD,),
        in_specs=[pl.BlockSpec((n_r, D_MODEL), lambda s: (0, 0)), pl.BlockSpec((1, n_r, ADA_SHARD), lambda s: (s, 0, 0))],
        out_specs=(pl.BlockSpec((1, D_MODEL, ADA_SHARD), lambda s: (s, 0, 0)),
                   pl.BlockSpec((1, 1, ADA_SHARD), lambda s: (s, 0, 0))),
        out_shape=(jax.ShapeDtypeStruct((N_SHARD, D_MODEL, ADA_SHARD), F32),
                   jax.ShapeDtypeStruct((N_SHARD, 1, ADA_SHARD), F32)),
        compiler_params=_params(_nbytes((D_MODEL, ADA_SHARD), F32), 2 * _nbytes((D_MODEL, ADA_SHARD), F32)),
    )(*_hbm(c_pad, dmod_s))


def _heads(t, n_b, n_s):
    return t.reshape(n_b, n_s, DN_HEADS, DN_HEAD_DIM).transpose(0, 2, 1, 3)


def _unheads(t):
    n_b, _, n_s, _ = t.shape
    return t.transpose(0, 2, 1, 3).reshape(n_b, n_s, DN_WIDTH)


def _block_diag(blocks):
    n_per = S5_GROUPS // S5_BLOCKS
    _, n_r, n_c = blocks.shape
    b4 = blocks.reshape(S5_BLOCKS, n_per, n_r, n_c)
    eye = jnp.eye(n_per, dtype=blocks.dtype)
    return (b4[:, :, :, None, :] * eye[None, :, None, :, None]).reshape(S5_BLOCKS, n_per * n_r, n_per * n_c)


def _diag_blocks(mat, n_r, n_c):
    n_per = S5_GROUPS // S5_BLOCKS
    m5 = mat.reshape(S5_BLOCKS, n_per, n_r, n_per, n_c)
    eye = jnp.eye(n_per, dtype=mat.dtype)
    return jnp.sum(m5 * eye[None, :, None, :, None], axis=3).reshape(S5_GROUPS, n_r, n_c)


def _local_step(x, c, target, wts):
    n_b, n_s, _ = x.shape
    n_tok = n_b * n_s
    flat = lambda t: t.reshape(n_tok, t.shape[-1])
    unflat = lambda t: t.reshape(n_b, n_s, t.shape[-1])
    n_chunks = n_s // CHUNK

    c_pad = jnp.zeros((SUBLANES, D_MODEL), F32).at[:n_b].set(c)
    mod_s = _ada_fwd(c_pad, wts["w_ada"], wts["b_ada"], name="ada_fwd")
    mod = mod_s.transpose(1, 0, 2).reshape(SUBLANES, N_MOD * D_MODEL)[:n_b]
    sh1, sc1, gt1, sh2, sc2, gt2, sh3, sc3, gt3 = [m[:, None, :] for m in jnp.split(mod, N_MOD, axis=-1)]

    a1 = _pre(x, None, None, wts["g_ffn1"], sh1, sc1, 0.0, name="pre1")
    f1, ffn1_saved = _ffn_fwd(flat(a1), wts["w1_ffn1"], wts["w3_ffn1"], wts["w2_ffn1"], "ffn1")
    x1, a2 = _pre(x, unflat(f1), gt1, wts["g_mix"], sh2, sc2, 0.5, name="pre2")
    u = flat(a2)[None]
    p_qkv = _mm_act([(u, wts["w_qkv"])], "nn", name="in_qkv")[0]
    p_z = _mm_act([(u, wts["w_z"])], "nn", name="in_z")[0]
    p_gab = _mm_act([(u, wts["w_gab"])], "nn", name="in_gab")[0]
    p_s5 = _mm_act([(u, wts["w_s5"])], "nn", name="in_s5")[0]
    p_ba = _mm_act([(u, wts["w_ba"])], "nn", name="in_ba")[0]

    qkv_c = _conv_fwd(unflat(p_qkv), wts["conv_qkv"], name="conv_fwd")
    qh, kh, vh = [_heads(t, n_b, n_s) for t in jnp.split(qkv_c, 3, axis=-1)]
    zh = _heads(p_z, n_b, n_s)
    ba = p_ba.reshape(n_b, n_s, BA_PAD)
    bl = ba[:, :, :DN_HEADS].transpose(0, 2, 1)[..., None]
    lac = ba[:, :, DN_HEADS:2 * DN_HEADS].transpose(0, 2, 1)[..., None]
    lar = lac.reshape(n_b, DN_HEADS, n_chunks, 1, CHUNK)
    a_log, dt_bias = wts["a_log"], wts["dt_bias"]
    dn_in = (qh, kh, vh, bl, lac, lar, a_log, dt_bias)
    qd, kd, uc, wc, attn, g_last, dn_inv = _dn_prep(*dn_in, name="dn_prep")
    o, states = _dn_scan(qd, kd, uc, wc, attn, g_last, name="dn_scan")
    og = _dn_post(o, zh, wts["g_onorm"], name="dn_post")
    og_t = _unheads(og).reshape(1, n_tok, DN_WIDTH)
    ya = _mm_act([(og_t, wts["w_proj_a"])], "nn", name="proj_a")[0]

    s5p_in = (wts["lam_re"], wts["lam_im"], wts["log_step"], wts["bt_re"], wts["bt_im"], wts["c_im"])
    lb_re, lb_im, bb_re, bb_im, c_neg = _s5_prep(*s5p_in, name="s5_prep")
    wb_re, wb_im = _block_diag(bb_re), _block_diag(bb_im)
    wc_re = _block_diag(wts["c_re"].transpose(0, 2, 1))
    wc_im = _block_diag(c_neg.transpose(0, 2, 1))
    lbr, lbi = lb_re.reshape(1, S5_LANES), lb_im.reshape(1, S5_LANES)
    s5_w = (wb_re, wb_im, lbr, lbi, wc_re, wc_im)
    ymm, x_re, x_im = _s5_mix(unflat(p_s5), *s5_w, name="s5_mix")
    ymm = ymm.reshape(n_tok, S5_WIDTH)
    y2 = _s5_out(ymm, p_s5, wts["d_skip"], wts["w_glu"], wts["b_glu"], name="s5_out")
    yb = _mm_act([(y2[None], wts["w_proj_b"])], "nn", name="proj_b")[0]

    merged = _merge(p_gab, ya, yb, name="merge")
    m_out = _mm_act([(merged[None], wts["w_out"])], "nn", name="mix_out")[0]
    x2, a3 = _pre(x1, unflat(m_out), gt2, wts["g_ffn2"], sh3, sc3, 1.0, name="pre3")
    f3, ffn2_saved = _ffn_fwd(flat(a3), wts["w1_ffn2"], wts["w3_ffn2"], wts["w2_ffn2"], "ffn2")

    g = {}
    loss, dx2_res, df3, dgt3, g["g_final"] = _final(x2, unflat(f3), gt3, wts["g_final"], target, name="final")
    da3, g["w1_ffn2"], g["w3_ffn2"], g["w2_ffn2"] = _ffn_bwd(
        flat(a3), wts["w1_ffn2"], wts["w3_ffn2"], wts["w2_ffn2"], ffn2_saved, flat(df3), "ffn2")
    dx1_res, dm_out, dgt2, g["g_ffn2"], dsh3, dsc3 = _pre_bwd(
        x1, unflat(m_out), gt2, wts["g_ffn2"], sh3, sc3, 1.0, unflat(da3), dx2_res, name="pre3_bwd")
    dm_out = flat(dm_out)[None]
    dmerged = _mm_act([(dm_out, wts["w_out"])], "nt", name="mix_out_bwd")[0]
    g["w_out"] = _mm_tn(merged[None], dm_out, name="dw_out")[0]
    dga, dgb, dya, dyb = _merge_bwd(p_gab, ya, yb, dmerged, name="merge_bwd")

    dy2 = _mm_act([(dyb[None], wts["w_proj_b"])], "nt", name="proj_b_bwd")[0]
    g["w_proj_b"] = _mm_tn(y2[None], dyb[None], name="dw_proj_b")[0]
    dymm, du_skip, g["d_skip"], g["w_glu"], g["b_glu"] = _s5_out_bwd(
        ymm, p_s5, wts["d_skip"], wts["w_glu"], wts["b_glu"], dy2, name="s5_out_bwd")
    dp_s5, dwb_re, dwb_im, dlb_re, dlb_im, dwc_re, dwc_im = _s5_mix_bwd(
        unflat(dymm), unflat(du_skip), unflat(p_s5), x_re, x_im, *s5_w, name="s5_mix_bwd")
    dp_s5 = dp_s5.reshape(n_tok, S5_WIDTH)
    g["c_re"] = _diag_blocks(dwc_re, S5_STATE, S5_GROUP_CH).transpose(0, 2, 1)
    s5_cts = (dlb_re.reshape(lb_re.shape), dlb_im.reshape(lb_im.shape),
              _diag_blocks(dwb_re, S5_GROUP_CH, S5_STATE), _diag_blocks(dwb_im, S5_GROUP_CH, S5_STATE),
              _diag_blocks(dwc_im, S5_STATE, S5_GROUP_CH).transpose(0, 2, 1))
    g["lam_re"], g["lam_im"], g["log_step"], g["bt_re"], g["bt_im"], g["c_im"] = _s5_prep_bwd(
        *s5p_in, s5_cts, name="s5_prep_bwd")

    dog = _mm_act([(dya[None], wts["w_proj_a"])], "nt", name="proj_a_bwd")[0]
    g["w_proj_a"] = _mm_tn(og_t, dya[None], name="dw_proj_a")[0]
    do, dzh, g["g_onorm"] = _dn_post_bwd(o, zh, wts["g_onorm"], _heads(dog, n_b, n_s), name="dn_post_bwd")
    scan_cts = _dn_scan_bwd(qd, kd, uc, wc, attn, g_last, states, do, name="dn_scan_bwd")
    dqh, dkh, dvh, dbl, dlac, dlar, g["a_log"], g["dt_bias"] = _dn_prep_bwd(*dn_in, dn_inv, uc, wc, scan_cts, name="dn_prep_bwd")
    dqkv_c = jnp.concatenate([_unheads(t) for t in (dqh, dkh, dvh)], axis=-1)
    dqkv, g["conv_qkv"] = _conv_bwd(unflat(p_qkv), wts["conv_qkv"], dqkv_c, name="conv_bwd")
    dla = dlac[..., 0] + dlar.reshape(n_b, DN_HEADS, n_s)
    dba = jnp.concatenate([dbl[..., 0].transpose(0, 2, 1), dla.transpose(0, 2, 1),
                           jnp.zeros((n_b, n_s, BA_PAD - 2 * DN_HEADS), F32)], axis=-1).astype(MXU_DTYPE)
    dz = _unheads(dzh)

    dps = {"w_qkv": flat(dqkv)[None], "w_z": flat(dz)[None], "w_ga": dga[None], "w_gb": dgb[None],
           "w_s5": dp_s5[None], "w_ba": flat(dba)[None]}
    w_ga, w_gb = wts["w_gab"][:, :, :D_MODEL], wts["w_gab"][:, :, D_MODEL:]
    w_of = dict(wts, w_ga=w_ga, w_gb=w_gb)
    du = _mm_act([(dps[k], w_of[k]) for k in dps], "nt", name="in_bwd")[0]
    for k in dps:
        g[k] = _mm_tn(u, dps[k], name=f"d{k}")[0]
    dx0_res, df1, dgt1, g["g_mix"], dsh2, dsc2 = _pre_bwd(
        x, unflat(f1), gt1, wts["g_mix"], sh2, sc2, 0.5, unflat(du), dx1_res, name="pre2_bwd")
    da1, g["w1_ffn1"], g["w3_ffn1"], g["w2_ffn1"] = _ffn_bwd(
        flat(a1), wts["w1_ffn1"], wts["w3_ffn1"], wts["w2_ffn1"], ffn1_saved, flat(df1), "ffn1")
    grad_x, g["g_ffn1"], dsh1, dsc1 = _pre_bwd(
        x, None, None, wts["g_ffn1"], sh1, sc1, 0.0, unflat(da1), dx0_res, name="pre1_bwd")

    dmod = jnp.concatenate([t[:, 0, :] for t in (dsh1, dsc1, dgt1, dsh2, dsc2, dgt2, dsh3, dsc3, dgt3)], axis=-1)
    dmod_pad = jnp.zeros((SUBLANES, N_MOD * D_MODEL), F32).at[:n_b].set(dmod)
    dmod_s = dmod_pad.reshape(SUBLANES, N_SHARD, ADA_SHARD).transpose(1, 0, 2)
    g["w_ada"], g["b_ada"] = _ada_bwd(c_pad, dmod_s, name="ada_bwd")
    return loss, grad_x, g


IN_SPLITS = (("w_qkv", 3 * DN_WIDTH), ("w_z", DN_WIDTH), ("w_ba", 2 * DN_HEADS), ("w_s5", S5_WIDTH),
             ("w_ga", D_MODEL), ("w_gb", D_MODEL))
SHARDED = ("w_ada", "w1_ffn1", "w3_ffn1", "w2_ffn1", "w_in", "conv_qkv", "w_glu", "w_proj_a", "w_proj_b", "w_out",
           "w1_ffn2", "w3_ffn2", "w2_ffn2")
COLUMN_SHARDED = ("w_ada", "w1_ffn1", "w3_ffn1", "w_in", "conv_qkv", "w_proj_a", "w_proj_b", "w1_ffn2", "w3_ffn2")


def _cat_columns(stack):
    return stack.transpose(1, 0, 2).reshape(stack.shape[1], N_SHARD * stack.shape[2])


def _split_columns(full):
    n_r, n_c = full.shape
    return full.reshape(n_r, N_SHARD, n_c // N_SHARD).transpose(1, 0, 2)


def _gathered_weights(st, rep):
    w = {k: st[k] for k in ("w_ada", "w1_ffn1", "w3_ffn1", "w2_ffn1", "w1_ffn2", "w3_ffn2", "w2_ffn2")}
    w["b_ada"] = rep["b_ada"].reshape(N_SHARD, 1, ADA_SHARD)
    for k in ("g_ffn1", "g_mix", "g_ffn2", "g_final"):
        w[k] = rep[k].reshape(1, D_MODEL)
    w_in = _cat_columns(st["w_in"])
    start = 0
    for k, size in IN_SPLITS:
        w[k] = w_in[None, :, start:start + size]
        start += size
    w["w_gab"] = jnp.concatenate([w.pop("w_ga"), w.pop("w_gb")], axis=-1)
    w["w_ba"] = jnp.pad(w["w_ba"], ((0, 0), (0, 0), (0, BA_PAD - 2 * DN_HEADS)))
    w["conv_qkv"] = _cat_columns(st["conv_qkv"])
    w["a_log"] = rep["a_log"].reshape(DN_HEADS, 1, 1)
    w["dt_bias"] = rep["dt_bias"].reshape(DN_HEADS, 1, 1)
    w["g_onorm"] = rep["g_onorm"].reshape(1, DN_HEAD_DIM)
    w["lam_re"] = rep["lam_re"].reshape(S5_GROUPS, 1, S5_STATE)
    w["lam_im"] = rep["lam_im"].reshape(S5_GROUPS, 1, S5_STATE)
    w["log_step"] = rep["log_step"].reshape(S5_GROUPS, 1, 1)
    w["bt_re"] = rep["b_re"][0].transpose(0, 2, 1)
    w["bt_im"] = rep["b_im"][0].transpose(0, 2, 1)
    w["c_re"] = rep["c_re"][0]
    w["c_im"] = rep["c_im"][0]
    w["d_skip"] = rep["d_skip"].reshape(1, S5_WIDTH)
    w["b_glu"] = rep["b_glu"].reshape(1, S5_WIDTH)
    w["w_glu"] = st["w_glu"].reshape(S5_WIDTH, S5_WIDTH)
    w["w_proj_a"] = _cat_columns(st["w_proj_a"])[None]
    w["w_proj_b"] = _cat_columns(st["w_proj_b"])[None]
    w["w_out"] = st["w_out"].reshape(1, D_MODEL, D_MODEL)
    return w


def _grads_to_problem_layout(g):
    st = {k: g[k] for k in ("w_ada", "w1_ffn1", "w3_ffn1", "w2_ffn1", "w1_ffn2", "w3_ffn2", "w2_ffn2")}
    w_in = jnp.concatenate([g[k][:, :size] for k, size in IN_SPLITS], axis=1)
    st["w_in"] = _split_columns(w_in)
    st["w_glu"] = g["w_glu"].reshape(N_SHARD, S5_WIDTH // N_SHARD, S5_WIDTH)
    st["w_proj_a"] = _split_columns(g["w_proj_a"])
    st["w_proj_b"] = _split_columns(g["w_proj_b"])
    st["w_out"] = g["w_out"].reshape(N_SHARD, D_MODEL // N_SHARD, D_MODEL)
    small = {
        "b_ada": g["b_ada"].reshape(1, N_MOD * D_MODEL),
        "g_ffn1": g["g_ffn1"], "g_mix": g["g_mix"], "g_ffn2": g["g_ffn2"], "g_final": g["g_final"].reshape(D_MODEL),
        "conv_qkv": g["conv_qkv"][None],
        "a_log": g["a_log"].reshape(1, DN_HEADS), "dt_bias": g["dt_bias"].reshape(1, DN_HEADS),
        "g_onorm": g["g_onorm"],
        "lam_re": g["lam_re"].reshape(1, S5_GROUPS, S5_STATE), "lam_im": g["lam_im"].reshape(1, S5_GROUPS, S5_STATE),
        "log_step": g["log_step"].reshape(1, S5_GROUPS),
        "b_re": g["bt_re"].transpose(0, 2, 1)[None], "b_im": g["bt_im"].transpose(0, 2, 1)[None],
        "c_re": g["c_re"][None], "c_im": g["c_im"][None],
        "d_skip": g["d_skip"], "b_glu": g["b_glu"],
    }
    return st, small


ELEMENTWISE_BLOCK_BYTES = 1 << 20


def _row_tile(n_rows, n_cols, n_lead=1, multiple=SUBLANES):
    best = None
    for t in range(multiple, n_rows + 1, multiple):
        if n_rows % t == 0 and n_lead * t * n_cols * 4 <= ELEMENTWISE_BLOCK_BYTES:
            best = t
    return best if best is not None else n_rows


def _add_sibling_half(g4, recv, my_c, *, name):
    n_sh, _, n_h, n_c = g4.shape
    th = _row_tile(n_h, n_c, multiple=2 * SUBLANES)

    def body(c_ref, g_ref, r_ref, o_ref):
        o_ref[0] = (g_ref[0, 0] + r_ref[0]).astype(o_ref.dtype)

    grid_spec = pltpu.PrefetchScalarGridSpec(
        num_scalar_prefetch=1, grid=(n_sh, n_h // th),
        in_specs=[pl.BlockSpec((1, 1, th, n_c), lambda s, i, c_ref: (s, c_ref[0], i, 0)),
                  pl.BlockSpec((1, th, n_c), lambda s, i, c_ref: (s, i, 0))],
        out_specs=pl.BlockSpec((1, th, n_c), lambda s, i, c_ref: (s, i, 0)))
    return pl.pallas_call(
        body, name=name, grid_spec=grid_spec, out_shape=jax.ShapeDtypeStruct((n_sh, n_h, n_c), MXU_DTYPE),
        compiler_params=_params(3 * _nbytes((th, n_c), F32)),
    )(*_hbm(my_c, g4, recv))


def _sum_slots(parts, *, name):
    n_p, n_r, n_c = parts.shape
    th = _row_tile(n_r, n_c, n_p)

    def body(p_ref, o_ref):
        total = p_ref[0].astype(F32)
        for k in range(1, n_p):
            total = total + p_ref[k].astype(F32)
        o_ref[...] = total

    return pl.pallas_call(
        body, name=name, grid=(n_r // th,),
        in_specs=[pl.BlockSpec((n_p, th, n_c), lambda i: (0, i, 0))],
        out_specs=pl.BlockSpec((th, n_c), lambda i: (i, 0)),
        out_shape=jax.ShapeDtypeStruct((n_r, n_c), F32),
        compiler_params=_params((n_p + 1) * _nbytes((th, n_c), F32)),
    )(*_hbm(parts))


def _cast_into_slot(w, place, dtype, *, name):
    n_r, n_c = w.shape
    th = _row_tile(n_r, n_c, multiple=2 * SUBLANES)

    def body(p_ref, w_ref, o_ref):
        o_ref[0] = w_ref[...].astype(o_ref.dtype)

    grid_spec = pltpu.PrefetchScalarGridSpec(
        num_scalar_prefetch=1, grid=(n_r // th,),
        in_specs=[pl.BlockSpec((th, n_c), lambda i, p: (i, 0))],
        out_specs=pl.BlockSpec((1, th, n_c), lambda i, p: (p[1], i, 0)))
    return pl.pallas_call(
        body, name=name, grid_spec=grid_spec, out_shape=jax.ShapeDtypeStruct((N_SHARD, n_r, n_c), dtype),
        compiler_params=_params(2 * _nbytes((th, n_c), F32)),
    )(*_hbm(place, w))


def _sum_chips(own, parts, place, *, name):
    n_sh, n_h, n_c = own.shape
    th = _row_tile(n_h, n_c, n_sh, multiple=2 * SUBLANES)

    def body(p_ref, own_ref, a_ref, b_ref, c_ref, o_ref):
        o_ref[0] = ((own_ref[0].astype(F32) + a_ref[0].astype(F32)) + b_ref[0].astype(F32)) + c_ref[0].astype(F32)

    slab = lambda k: pl.BlockSpec((1, th, n_c), lambda i, p, k=k: (p[k], i, 0))
    grid_spec = pltpu.PrefetchScalarGridSpec(
        num_scalar_prefetch=1, grid=(n_h // th,),
        in_specs=[slab(1), slab(2), slab(3), slab(4)], out_specs=slab(0))
    return pl.pallas_call(
        body, name=name, grid_spec=grid_spec, out_shape=jax.ShapeDtypeStruct((2, n_h, n_c), F32),
        compiler_params=_params(5 * _nbytes((th, n_c), F32)),
    )(*_hbm(place, own, parts, parts, parts))


def _adamw(w, g, m, v, *, name):
    n_r, n_c = w.shape
    th = _row_tile(n_r, n_c)
    bias1 = 1.0 - ADAM_B1 ** ADAM_STEP
    bias2 = 1.0 - ADAM_B2 ** ADAM_STEP

    def body(w_ref, g_ref, m_ref, v_ref, d_ref, mo_ref, vo_ref):
        gv = g_ref[...]
        m_new = ADAM_B1 * m_ref[...] + (1.0 - ADAM_B1) * gv
        v_new = ADAM_B2 * v_ref[...] + (1.0 - ADAM_B2) * jnp.square(gv)
        d_ref[...] = -ADAM_LR * ((m_new / bias1) / (jnp.sqrt(v_new / bias2) + ADAM_EPS) + ADAM_WD * w_ref[...])
        mo_ref[...] = m_new
        vo_ref[...] = v_new

    spec = pl.BlockSpec((th, n_c), lambda i: (i, 0))
    shape = jax.ShapeDtypeStruct((n_r, n_c), F32)
    return pl.pallas_call(
        body, name=name, grid=(n_r // th,), in_specs=[spec] * 4, out_specs=(spec,) * 3, out_shape=(shape,) * 3,
        compiler_params=_params(7 * _nbytes((th, n_c), F32)),
    )(*_hbm(w, g, m, v))


CHIP_FLIPS = ((1, 0), (0, 1), (1, 1))
DEVICE_FLIPS = tuple((fx, fy, fc) for fx in (0, 1) for fy in (0, 1) for fc in (0, 1))[1:]


def _exchange(ins, out_shapes, plan, n_local, n_remote, *, name, aliased=False):
    n_in, n_out = len(ins), len(out_shapes)

    def body(*refs):
        in_refs, out_refs = refs[:n_in], refs[n_in:n_in + n_out]
        send_sems, recv_sems, local_sems = refs[n_in + n_out:]
        me = (lax.axis_index("x"), lax.axis_index("y"), lax.axis_index("c"))
        local, remote = plan(in_refs, out_refs, me)
        assert len(local) == n_local and len(remote) == n_remote
        here = [pltpu.make_async_copy(src, dst, local_sems.at[i]) for i, (src, dst) in enumerate(local)]
        for cp in here:
            cp.start()
        sends = [pltpu.make_async_remote_copy(src_ref=src, dst_ref=dst, send_sem=send_sems.at[i], recv_sem=recv_sems.at[i],
                                              device_id=peer, device_id_type=pl.DeviceIdType.MESH)
                 for i, (src, dst, _, peer) in enumerate(remote)]
        for cp in sends:
            cp.start()
        for i, (src, _, landing, peer) in enumerate(remote):
            pltpu.make_async_remote_copy(src_ref=src, dst_ref=landing, send_sem=send_sems.at[i], recv_sem=recv_sems.at[i],
                                         device_id=peer, device_id_type=pl.DeviceIdType.MESH).wait_recv()
        for cp in sends:
            cp.wait_send()
        for cp in here:
            cp.wait()

    any_spec = pl.BlockSpec(memory_space=pl.ANY)
    return pl.pallas_call(
        body, name=name, in_specs=[any_spec] * n_in, out_specs=tuple([any_spec] * n_out), out_shape=tuple(out_shapes),
        scratch_shapes=[pltpu.SemaphoreType.DMA((n_remote,)), pltpu.SemaphoreType.DMA((n_remote,)),
                        pltpu.SemaphoreType.DMA((max(n_local, 1),))],
        input_output_aliases={k: k for k in range(n_in)} if aliased else {},
    )(*ins)


def _gather_shards(stacks, *, name):
    n = len(stacks)
    halved = [a.shape[1] >= 32 for a in stacks]
    n_ici = len(CHIP_FLIPS) * n
    n_pass = len(CHIP_FLIPS) * sum(halved)

    def body(*refs):
        outs = refs[n:2 * n]
        send_sems, recv_sems = refs[2 * n:]
        x, y, c = lax.axis_index("x"), lax.axis_index("y"), lax.axis_index("c")
        mine = 2 * x + y

        def rows(k, slot, half):
            if not halved[k]:
                return outs[k].at[slot]
            n_h = stacks[k].shape[1] // 2
            return outs[k].at[slot, pl.ds(pl.multiple_of(half * n_h, 16), n_h)]

        def copy(i, src, dst, peer):
            return pltpu.make_async_remote_copy(src_ref=src, dst_ref=dst, send_sem=send_sems.at[i], recv_sem=recv_sems.at[i],
                                                device_id=peer, device_id_type=pl.DeviceIdType.MESH)

        started = []
        for j, (fx, fy) in enumerate(CHIP_FLIPS):
            for k in range(n):
                cp = copy(j * n + k, rows(k, mine, c), rows(k, mine, c), (x ^ fx, y ^ fy, c))
                cp.start()
                started.append(cp)
        i_pass = n_ici
        expect = []
        for j, (fx, fy) in enumerate(CHIP_FLIPS):
            peer_chip = 2 * (x ^ fx) + (y ^ fy)
            for k in range(n):
                landed = rows(k, peer_chip, c)
                copy(j * n + k, landed, landed, (x ^ fx, y ^ fy, c)).wait_recv()
                if halved[k]:
                    cp = copy(i_pass, landed, landed, (x, y, 1 - c))
                    cp.start()
                    started.append(cp)
                    expect.append((i_pass, rows(k, peer_chip, 1 - c)))
                    i_pass += 1
        for i, landing in expect:
            copy(i, landing, landing, (x, y, 1 - c)).wait_recv()
        for cp in started:
            cp.wait_send()

    any_spec = pl.BlockSpec(memory_space=pl.ANY)
    n_sem = n_ici + n_pass
    return pl.pallas_call(
        body, name=name, in_specs=[any_spec] * n, out_specs=tuple([any_spec] * n),
        out_shape=tuple(jax.ShapeDtypeStruct(a.shape, a.dtype) for a in stacks),
        scratch_shapes=[pltpu.SemaphoreType.DMA((n_sem,)), pltpu.SemaphoreType.DMA((n_sem,))],
        input_output_aliases={k: k for k in range(n)},
    )(*stacks)


def _swap_sibling_halves(g4s, *, name):
    n = len(g4s)

    def plan(in_refs, out_refs, me):
        x, y, c = me
        remote = [(in_refs[k].at[:, 1 - c], out_refs[k], out_refs[k], (x, y, 1 - c)) for k in range(n)]
        return [], remote

    shapes = [jax.ShapeDtypeStruct((a.shape[0],) + a.shape[2:], a.dtype) for a in g4s]
    return _exchange(g4s, shapes, plan, 0, n, name=name)


def _scatter_to_chips(hs, *, name):
    n = len(hs)

    def plan(in_refs, out_refs, me):
        x, y, c = me
        mine = 2 * x + y
        remote = []
        for fx, fy in CHIP_FLIPS:
            px, py = x ^ fx, y ^ fy
            peer = 2 * px + py
            for k in range(n):
                remote.append((in_refs[k].at[peer], out_refs[k].at[mine], out_refs[k].at[peer], (px, py, c)))
        return [], remote

    shapes = [jax.ShapeDtypeStruct(a.shape, a.dtype) for a in hs]
    return _exchange(hs, shapes, plan, 0, len(CHIP_FLIPS) * n, name=name)


def _join_sibling_halves(rs, *, name):
    n = len(rs)

    def plan(in_refs, out_refs, me):
        x, y, c = me
        remote = [(out_refs[k].at[c], out_refs[k].at[c], out_refs[k].at[1 - c], (x, y, 1 - c)) for k in range(n)]
        return [], remote

    shapes = [jax.ShapeDtypeStruct(a.shape, a.dtype) for a in rs]
    return _exchange(rs, shapes, plan, 0, n, name=name, aliased=True)


def _gather_all_devices(packed, *, name):
    def plan(in_refs, out_refs, me):
        x, y, c = me
        mine = 4 * x + 2 * y + c
        remote = []
        for fx, fy, fc in DEVICE_FLIPS:
            px, py, pc = x ^ fx, y ^ fy, c ^ fc
            remote.append((in_refs[0], out_refs[0].at[mine], out_refs[0].at[4 * px + 2 * py + pc], (px, py, pc)))
        return [(in_refs[0], out_refs[0].at[mine])], remote

    shape = jax.ShapeDtypeStruct((2 * N_SHARD,) + packed.shape, packed.dtype)
    return _exchange([packed], [shape], plan, 1, len(DEVICE_FLIPS), name=name)[0]


WEIGHT_NAMES = ("w_ada", "b_ada", "g_ffn1", "w1_ffn1", "w3_ffn1", "w2_ffn1", "g_mix", "w_in", "conv_qkv", "a_log",
                "dt_bias", "g_onorm", "lam_re", "lam_im", "log_step", "b_re", "b_im", "c_re", "c_im", "d_skip", "w_glu",
                "b_glu", "w_proj_a", "w_proj_b", "w_out", "g_ffn2", "w1_ffn2", "w3_ffn2", "w2_ffn2", "g_final")
LARGE = tuple(n for n in SHARDED if n != "conv_qkv")
SMALL = tuple(n for n in WEIGHT_NAMES if n not in LARGE)
PACK_ROW = SUBLANES * LANES


def _pack(arrays):
    flat = jnp.concatenate([a.reshape(-1) for a in arrays])
    n_pad = -flat.shape[0] % PACK_ROW
    return jnp.pad(flat, (0, n_pad)).reshape(-1, LANES)


def _unpack(packed, shapes):
    flat = packed.reshape(-1)
    out, start = [], 0
    for s in shapes:
        size = math.prod(s)
        out.append(flat[start:start + size].reshape(s))
        start += size
    return out


TRANSPOSED = ("w1_ffn1", "w3_ffn1", "w1_ffn2", "w3_ffn2")


def _to_internal(name, a):
    return jnp.swapaxes(a[0], 0, 1) if name in TRANSPOSED else a[0]


def _from_internal(name, a):
    return (jnp.swapaxes(a, 0, 1) if name in TRANSPOSED else a)[None]


def _step(x, c, target, weights, m_in, v_in):
    xi, yi, ci = lax.axis_index("x"), lax.axis_index("y"), lax.axis_index("c")
    my_chip = 2 * xi + yi

    others = [k + (k >= my_chip).astype(jnp.int32) for k in range(N_SHARD - 1)]
    place = jnp.stack([ci, my_chip] + others).astype(jnp.int32)

    slots = [_cast_into_slot(_to_internal(n, weights[n]), place, F32 if n == "conv_qkv" else MXU_DTYPE, name=f"cast_{n}")
             for n in SHARDED]
    stacks = dict(zip(SHARDED, _gather_shards(slots, name="gather_weights")))
    rep = {n: weights[n] for n in WEIGHT_NAMES if n not in SHARDED}
    loss, grad_x, g = _local_step(x, c, target, _gathered_weights(stacks, rep))
    g_stacks, g_small = _grads_to_problem_layout(g)

    g4s = [g_stacks[n].reshape(N_SHARD, 2, g_stacks[n].shape[1] // 2, g_stacks[n].shape[2]) for n in LARGE]
    from_sibling = _swap_sibling_halves(g4s, name="swap_sibling_halves")
    chip_sums = [_add_sibling_half(a, r, place, name=f"chip_sum_{n}") for n, a, r in zip(LARGE, g4s, from_sibling)]
    from_chips = _scatter_to_chips(chip_sums, name="scatter_to_chips")
    reduced = [_sum_chips(h, p, place, name=f"sum_chips_{n}") for n, h, p in zip(LARGE, chip_sums, from_chips)]
    joined = _join_sibling_halves(reduced, name="join_sibling_halves")
    grads_2d = {n: j.reshape(2 * j.shape[1], j.shape[2]) for n, j in zip(LARGE, joined)}
    grads = {n: _from_internal(n, a) for n, a in grads_2d.items()}

    small_shapes = [g_small[n].shape for n in SMALL] + [(1, 1)]
    packed = _pack([g_small[n] for n in SMALL] + [loss])
    total = _sum_slots(_gather_all_devices(packed, name="gather_small"), name="sum_small")
    *small_grads, loss_sum = _unpack(total, small_shapes)
    grads.update(zip(SMALL, small_grads))
    n_conv = weights["conv_qkv"].shape[-1]
    grads["conv_qkv"] = lax.dynamic_slice_in_dim(grads["conv_qkv"], my_chip * n_conv, n_conv, axis=2)

    delta, new_m, new_v = {}, {}, {}
    grads_2d["conv_qkv"] = grads["conv_qkv"][0]
    for n in LARGE + ("conv_qkv",):
        outs = _adamw(_to_internal(n, weights[n]), grads_2d[n], _to_internal(n, m_in[n]), _to_internal(n, v_in[n]),
                      name=f"adamw_{n}")
        delta[n], new_m[n], new_v[n] = [_from_internal(n, o) for o in outs]
    packed_names = tuple(n for n in SMALL if n != "conv_qkv")
    shapes = [weights[n].shape for n in packed_names]
    outs = _adamw(*[_pack([d[n] for n in packed_names]) for d in (weights, grads, m_in, v_in)], name="adamw_small")
    for d, o in zip((delta, new_m, new_v), outs):
        d.update(zip(packed_names, _unpack(o, shapes)))
    return (loss_sum.reshape(()), grad_x, *[grads[n] for n in WEIGHT_NAMES], *[delta[n] for n in WEIGHT_NAMES],
            *[new_m[n] for n in WEIGHT_NAMES], *[new_v[n] for n in WEIGHT_NAMES])


def kernel(x, c, w_ada, b_ada, g_ffn1, w1_ffn1, w3_ffn1, w2_ffn1, g_mix, w_in, conv_qkv, a_log, dt_bias, g_onorm, lam_re, lam_im, log_step, b_re, b_im, c_re, c_im, d_skip, w_glu, b_glu, w_proj_a, w_proj_b, w_out, g_ffn2, w1_ffn2, w3_ffn2, w2_ffn2, g_final, loss_target, m_w_ada, m_b_ada, m_g_ffn1, m_w1_ffn1, m_w3_ffn1, m_w2_ffn1, m_g_mix, m_w_in, m_conv_qkv, m_a_log, m_dt_bias, m_g_onorm, m_lam_re, m_lam_im, m_log_step, m_b_re, m_b_im, m_c_re, m_c_im, m_d_skip, m_w_glu, m_b_glu, m_w_proj_a, m_w_proj_b, m_w_out, m_g_ffn2, m_w1_ffn2, m_w3_ffn2, m_w2_ffn2, m_g_final, v_w_ada, v_b_ada, v_g_ffn1, v_w1_ffn1, v_w3_ffn1, v_w2_ffn1, v_g_mix, v_w_in, v_conv_qkv, v_a_log, v_dt_bias, v_g_onorm, v_lam_re, v_lam_im, v_log_step, v_b_re, v_b_im, v_c_re, v_c_im, v_d_skip, v_w_glu, v_b_glu, v_w_proj_a, v_w_proj_b, v_w_out, v_g_ffn2, v_w1_ffn2, v_w3_ffn2, v_w2_ffn2, v_g_final):
    w_vals = (w_ada, b_ada, g_ffn1, w1_ffn1, w3_ffn1, w2_ffn1, g_mix, w_in, conv_qkv, a_log, dt_bias, g_onorm, lam_re, lam_im, log_step, b_re, b_im, c_re, c_im, d_skip, w_glu, b_glu, w_proj_a, w_proj_b, w_out, g_ffn2, w1_ffn2, w3_ffn2, w2_ffn2, g_final)
    m_vals = (m_w_ada, m_b_ada, m_g_ffn1, m_w1_ffn1, m_w3_ffn1, m_w2_ffn1, m_g_mix, m_w_in, m_conv_qkv, m_a_log, m_dt_bias, m_g_onorm, m_lam_re, m_lam_im, m_log_step, m_b_re, m_b_im, m_c_re, m_c_im, m_d_skip, m_w_glu, m_b_glu, m_w_proj_a, m_w_proj_b, m_w_out, m_g_ffn2, m_w1_ffn2, m_w3_ffn2, m_w2_ffn2, m_g_final)
    v_vals = (v_w_ada, v_b_ada, v_g_ffn1, v_w1_ffn1, v_w3_ffn1, v_w2_ffn1, v_g_mix, v_w_in, v_conv_qkv, v_a_log, v_dt_bias, v_g_onorm, v_lam_re, v_lam_im, v_log_step, v_b_re, v_b_im, v_c_re, v_c_im, v_d_skip, v_w_glu, v_b_glu, v_w_proj_a, v_w_proj_b, v_w_out, v_g_ffn2, v_w1_ffn2, v_w3_ffn2, v_w2_ffn2, v_g_final)
    return _step(x, c, loss_target, dict(zip(WEIGHT_NAMES, w_vals)), dict(zip(WEIGHT_NAMES, m_vals)),
                 dict(zip(WEIGHT_NAMES, v_vals)))
```

```python
import functools
import math

import jax
import jax.numpy as jnp
from jax import lax
from jax.experimental import pallas as pl
from jax.experimental.pallas import tpu as pltpu

F32 = jnp.float32
BF16 = jnp.bfloat16
MXU_DTYPE = BF16

D_MODEL = 1024
D_FF = 2816
DN_HEADS = 8
DN_HEAD_DIM = 64
DN_WIDTH = DN_HEADS * DN_HEAD_DIM
CONV_WIDTH = 4
CHUNK = 64
S5_GROUP_CH = 16
S5_GROUPS = 32
S5_WIDTH = S5_GROUPS * S5_GROUP_CH
S5_STATE = 64
S5_LANES = S5_GROUPS * S5_STATE
N_MOD = 9
EPS = 1e-6
N_SHARD = 4
FF_SHARD = D_FF // N_SHARD
BA_PAD = 128

ADAM_LR = 0.001
ADAM_B1 = 0.9
ADAM_B2 = 0.999
ADAM_EPS = 1e-08
ADAM_WD = 0.01
ADAM_STEP = 10

VMEM_BYTES_V7X = 64 * 1024 * 1024
SUBLANES = 8
LANES = 128


def _params(block_bytes, extra_bytes=0):
    need = 2 * block_bytes + extra_bytes + (4 << 20)
    return pltpu.CompilerParams(vmem_limit_bytes=int(min(max(need, 16 << 20), VMEM_BYTES_V7X - (8 << 20))))


def _nbytes(shape, dtype):
    return math.prod(shape) * jnp.dtype(dtype).itemsize


HBM_OPERAND_BYTES = 1 << 20


def _hbm(*args):
    return [pltpu.with_memory_space_constraint(a, pltpu.HBM) if _nbytes(a.shape, a.dtype) >= HBM_OPERAND_BYTES else a
            for a in args]


_NN = (((1,), (0,)), ((), ()))
_NT = (((1,), (1,)), ((), ()))
_TN = (((0,), (0,)), ((), ()))


def _mm_act(pairs, mode, *, name, out_sharded=False, reduce_shards=False, out_dtype=F32, add=None, tm=512):
    n_tok = pairs[0][0].shape[1]
    n_out = pairs[0][1].shape[2] if mode == "nn" else pairs[0][1].shape[1]
    tm = min(tm, n_tok)
    tn = n_out if n_out <= 1536 else 1024
    assert n_tok % tm == 0 and n_out % tn == 0
    n_so = N_SHARD if out_sharded else 1
    n_red = N_SHARD if reduce_shards else 1
    grid = (n_so, n_tok // tm, n_out // tn, n_red)
    dims = _NN if mode == "nn" else _NT

    def shard_of(n_sh):
        if n_sh == 1:
            return lambda s, r: 0
        return (lambda s, r: s) if out_sharded else (lambda s, r: r)

    in_specs, args, blk = [], [], 0
    for a, b in pairs:
        k_dim = a.shape[2]
        sa, sb = shard_of(a.shape[0]), shard_of(b.shape[0])
        in_specs.append(pl.BlockSpec((1, tm, k_dim), lambda s, i, j, r, sa=sa: (sa(s, r), i, 0)))
        if mode == "nn":
            assert b.shape[1] == k_dim
            in_specs.append(pl.BlockSpec((1, k_dim, tn), lambda s, i, j, r, sb=sb: (sb(s, r), 0, j)))
        else:
            assert b.shape[2] == k_dim
            in_specs.append(pl.BlockSpec((1, tn, k_dim), lambda s, i, j, r, sb=sb: (sb(s, r), j, 0)))
        args += [a, b]
        blk += _nbytes((tm, k_dim), a.dtype) + _nbytes((k_dim, tn), b.dtype)
    if add is not None:
        in_specs.append(pl.BlockSpec((1, tm, tn), lambda s, i, j, r: (s, i, j)))
        args.append(add)
        blk += _nbytes((tm, tn), F32)
    blk += _nbytes((tm, tn), out_dtype)
    n_pairs = len(pairs)

    def body(*refs):
        out_ref = refs[2 * n_pairs + (add is not None)]
        acc = None
        for k in range(n_pairs):
            a = refs[2 * k][0].astype(MXU_DTYPE)
            b = refs[2 * k + 1][0].astype(MXU_DTYPE)
            d = lax.dot_general(a, b, dims, preferred_element_type=F32)
            acc = d if acc is None else acc + d

        def finish(total):
            if add is not None:
                total = total + refs[2 * n_pairs][0]
            out_ref[0] = total.astype(out_dtype)

        if n_red == 1:
            finish(acc)
        else:
            acc_ref = refs[-1]
            r = pl.program_id(3)

            @pl.when(r == 0)
            def _():
                acc_ref[...] = acc

            @pl.when(r > 0)
            def _():
                acc_ref[...] += acc

            @pl.when(r == n_red - 1)
            def _():
                finish(acc_ref[...])

    return pl.pallas_call(
        body,
        name=name,
        grid=grid,
        in_specs=in_specs,
        out_specs=pl.BlockSpec((1, tm, tn), lambda s, i, j, r: (s, i, j)),
        out_shape=jax.ShapeDtypeStruct((n_so, n_tok, n_out), out_dtype),
        scratch_shapes=[pltpu.VMEM((tm, tn), F32)] if n_red > 1 else [],
        compiler_params=_params(blk, 3 * _nbytes((tm, tn), F32)),
    )(*_hbm(*args))


def _mm_tn(a, b, *, name, tt=512):
    n_tok, k_dim = a.shape[1], a.shape[2]
    n_out = b.shape[2]
    tt = min(tt, n_tok)
    tk = k_dim if k_dim <= 1536 else 1024
    tn = n_out if n_out <= 1536 else 1024
    assert n_tok % tt == 0 and k_dim % tk == 0 and n_out % tn == 0
    n_so = max(a.shape[0], b.shape[0])
    sa = (lambda s: s) if a.shape[0] > 1 else (lambda s: 0)
    sb = (lambda s: s) if b.shape[0] > 1 else (lambda s: 0)
    grid = (n_so, k_dim // tk, n_out // tn, n_tok // tt)

    def body(a_ref, b_ref, out_ref):
        d = lax.dot_general(a_ref[0].astype(MXU_DTYPE), b_ref[0].astype(MXU_DTYPE), _TN, preferred_element_type=F32)
        t = pl.program_id(3)

        @pl.when(t == 0)
        def _():
            out_ref[0] = d

        @pl.when(t > 0)
        def _():
            out_ref[0] += d

    blk = _nbytes((tt, tk), a.dtype) + _nbytes((tt, tn), b.dtype) + _nbytes((tk, tn), F32)
    return pl.pallas_call(
        body,
        name=name,
        grid=grid,
        in_specs=[
            pl.BlockSpec((1, tt, tk), lambda s, ki, nj, t: (sa(s), t, ki)),
            pl.BlockSpec((1, tt, tn), lambda s, ki, nj, t: (sb(s), t, nj)),
        ],
        out_specs=pl.BlockSpec((1, tk, tn), lambda s, ki, nj, t: (s, ki, nj)),
        out_shape=jax.ShapeDtypeStruct((n_so, k_dim, n_out), F32),
        compiler_params=_params(blk, 2 * _nbytes((tk, tn), F32) + _nbytes((tt, tk), F32)),
    )(*_hbm(a, b))


@functools.partial(jax.custom_vjp, nondiff_argnums=(2,))
def _mdot(a, b, dims):
    return lax.dot_general(a.astype(MXU_DTYPE), b.astype(MXU_DTYPE), dims, preferred_element_type=F32)


def _mdot_fwd(a, b, dims):
    return _mdot(a, b, dims), (a, b)


def _mdot_bwd(dims, res, g):
    a, b = res
    (ca, cb), (ba, bb) = dims
    nb = len(ba)
    assert tuple(ba) == tuple(range(nb)) and tuple(bb) == tuple(range(nb)) and len(ca) == 1 and a.ndim == nb + 2
    batch = (tuple(range(nb)), tuple(range(nb)))
    ra, rb = nb, nb + 1
    a_free = (set(range(nb, nb + 2)) - set(ca)).pop()
    b_free = (set(range(nb, nb + 2)) - set(cb)).pop()
    if a_free < ca[0]:
        da = _mdot(g, b, (((rb,), (b_free,)), batch))
    else:
        da = _mdot(b, g, (((b_free,), (rb,)), batch))
    if b_free > cb[0]:
        db = _mdot(a, g, (((a_free,), (ra,)), batch))
    else:
        db = _mdot(g, a, (((ra,), (a_free,)), batch))
    return da.astype(a.dtype), db.astype(b.dtype)


_mdot.defvjp(_mdot_fwd, _mdot_bwd)


def _rms(x, gain):
    return x * lax.rsqrt(jnp.mean(x * x, axis=-1, keepdims=True) + EPS) * gain


def _pre_fn(coef, x_in, f, gate, gain, shift, scale):
    x_new = x_in if f is None else x_in + coef * gate * f
    return x_new, _rms(x_new, gain) * (1.0 + scale) + shift


def _row_spec(ts):
    return pl.BlockSpec((1, ts, D_MODEL), lambda b, j: (b, j, 0))


_BATCH_VEC = pl.BlockSpec((1, 1, D_MODEL), lambda b, j: (b, 0, 0))
_ONE_VEC = pl.BlockSpec((1, D_MODEL), lambda b, j: (0, 0))


def _pre(x_in, f, gate, gain, shift, scale, coef, *, name, ts=512):
    n_b, n_s, _ = x_in.shape
    ts = min(ts, n_s)
    has_res = f is not None

    def body(*refs):
        if has_res:
            x_ref, f_ref, gate_ref, gain_ref, sh_ref, sc_ref, xn_ref, a_ref = refs
            x_new, a = _pre_fn(coef, x_ref[0], f_ref[0], gate_ref[0], gain_ref[...], sh_ref[0], sc_ref[0])
            xn_ref[0] = x_new
        else:
            x_ref, gain_ref, sh_ref, sc_ref, a_ref = refs
            _, a = _pre_fn(coef, x_ref[0], None, None, gain_ref[...], sh_ref[0], sc_ref[0])
        a_ref[0] = a.astype(a_ref.dtype)

    row = _row_spec(ts)
    if has_res:
        args = (x_in, f, gate, gain, shift, scale)
        in_specs = [row, row, _BATCH_VEC, _ONE_VEC, _BATCH_VEC, _BATCH_VEC]
        out_specs = (row, row)
        out_shape = (jax.ShapeDtypeStruct(x_in.shape, F32), jax.ShapeDtypeStruct(x_in.shape, MXU_DTYPE))
    else:
        args = (x_in, gain, shift, scale)
        in_specs = [row, _ONE_VEC, _BATCH_VEC, _BATCH_VEC]
        out_specs = row
        out_shape = jax.ShapeDtypeStruct(x_in.shape, MXU_DTYPE)
    return pl.pallas_call(
        body, name=name, grid=(n_b, n_s // ts), in_specs=in_specs, out_specs=out_specs, out_shape=out_shape,
        compiler_params=_params(5 * _nbytes((ts, D_MODEL), F32), 4 * _nbytes((ts, D_MODEL), F32)),
    )(*_hbm(*args))


def _accumulate(ref, value, first):
    @pl.when(first)
    def _():
        ref[...] = value

    @pl.when(jnp.logical_not(first))
    def _():
        ref[...] += value


def _pre_bwd(x_in, f, gate, gain, shift, scale, coef, da, dx_up, *, name, ts=512):
    n_b, n_s, _ = x_in.shape
    ts = min(ts, n_s)
    has_res = f is not None
    has_up = dx_up is not None

    def body(*refs):
        refs = list(refs)
        x_ref = refs.pop(0)
        f_ref, gate_ref = (refs.pop(0), refs.pop(0)) if has_res else (None, None)
        gain_ref, sh_ref, sc_ref, da_ref = refs.pop(0), refs.pop(0), refs.pop(0), refs.pop(0)
        up_ref = refs.pop(0) if has_up else None
        dx_ref = refs.pop(0)
        df_ref, dgate_ref = (refs.pop(0), refs.pop(0)) if has_res else (None, None)
        dgain_ref, dsh_ref, dsc_ref = refs
        b, j = pl.program_id(0), pl.program_id(1)
        da_v = da_ref[0].astype(F32)
        up_v = up_ref[0] if has_up else jnp.zeros((ts, D_MODEL), F32)
        if has_res:
            fn = functools.partial(_pre_fn, coef)
            _, pull = jax.vjp(fn, x_ref[0], f_ref[0], gate_ref[0], gain_ref[...], sh_ref[0], sc_ref[0])
            dx, df, dgate, dgain, dsh, dsc = pull((up_v, da_v))
            df_ref[0] = df.astype(df_ref.dtype)
            _accumulate(dgate_ref, dgate[None], j == 0)
        else:
            fn = lambda x, g, sh, sc: _pre_fn(coef, x, None, None, g, sh, sc)
            _, pull = jax.vjp(fn, x_ref[0], gain_ref[...], sh_ref[0], sc_ref[0])
            dx, dgain, dsh, dsc = pull((up_v, da_v))
        dx_ref[0] = dx
        _accumulate(dgain_ref, dgain, jnp.logical_and(b == 0, j == 0))
        _accumulate(dsh_ref, dsh[None], j == 0)
        _accumulate(dsc_ref, dsc[None], j == 0)

    row = _row_spec(ts)
    args, in_specs = [x_in], [row]
    if has_res:
        args += [f, gate]
        in_specs += [row, _BATCH_VEC]
    args += [gain, shift, scale, da]
    in_specs += [_ONE_VEC, _BATCH_VEC, _BATCH_VEC, row]
    if has_up:
        args.append(dx_up)
        in_specs.append(row)
    vec = jax.ShapeDtypeStruct((n_b, 1, D_MODEL), F32)
    out_shape, out_specs = [jax.ShapeDtypeStruct(x_in.shape, F32)], [row]
    if has_res:
        out_shape += [jax.ShapeDtypeStruct(x_in.shape, MXU_DTYPE), vec]
        out_specs += [row, _BATCH_VEC]
    out_shape += [jax.ShapeDtypeStruct((1, D_MODEL), F32), vec, vec]
    out_specs += [_ONE_VEC, _BATCH_VEC, _BATCH_VEC]
    return pl.pallas_call(
        body, name=name, grid=(n_b, n_s // ts), in_specs=in_specs, out_specs=tuple(out_specs), out_shape=tuple(out_shape),
        compiler_params=_params(6 * _nbytes((ts, D_MODEL), F32), 8 * _nbytes((ts, D_MODEL), F32)),
    )(*_hbm(*args))


def _final_fn(x_in, f, gate, gain, target):
    x_new = x_in + 0.5 * gate * f
    err = jnp.square(_rms(x_new, gain) - target)
    return 0.5 * jnp.sum(jnp.mean(err, axis=-1))


def _final(x_in, f, gate, gain, target, *, name, ts=512):
    n_b, n_s, _ = x_in.shape
    ts = min(ts, n_s)

    def body(x_ref, f_ref, gate_ref, gain_ref, t_ref, loss_ref, dx_ref, df_ref, dgate_ref, dgain_ref):
        b, j = pl.program_id(0), pl.program_id(1)
        loss, (dx, df, dgate, dgain) = jax.value_and_grad(_final_fn, argnums=(0, 1, 2, 3))(
            x_ref[0], f_ref[0], gate_ref[0], gain_ref[...], t_ref[0])
        first = jnp.logical_and(b == 0, j == 0)
        _accumulate(loss_ref, jnp.reshape(loss, (1, 1)), first)
        dx_ref[0] = dx
        df_ref[0] = df.astype(df_ref.dtype)
        _accumulate(dgate_ref, dgate[None], j == 0)
        _accumulate(dgain_ref, dgain, first)

    row = _row_spec(ts)
    return pl.pallas_call(
        body, name=name, grid=(n_b, n_s // ts),
        in_specs=[row, row, _BATCH_VEC, _ONE_VEC, row],
        out_specs=(pl.BlockSpec((1, 1), lambda b, j: (0, 0)), row, row, _BATCH_VEC, _ONE_VEC),
        out_shape=(jax.ShapeDtypeStruct((1, 1), F32), jax.ShapeDtypeStruct(x_in.shape, F32),
                   jax.ShapeDtypeStruct(x_in.shape, MXU_DTYPE), jax.ShapeDtypeStruct((n_b, 1, D_MODEL), F32),
                   jax.ShapeDtypeStruct((1, D_MODEL), F32)),
        compiler_params=_params(5 * _nbytes((ts, D_MODEL), F32), 8 * _nbytes((ts, D_MODEL), F32)),
    )(*_hbm(x_in, f, gate, gain, target))


FFN_TOKENS = 1024


def _ffn_up(a, w1s, w3s, *, name, tm=FFN_TOKENS):
    n_tok = a.shape[0]
    tm = min(tm, n_tok)

    def body(a_ref, w1_ref, w3_ref, h1_ref, h3_ref, g_ref):
        av = a_ref[...].astype(MXU_DTYPE)
        h1 = lax.dot_general(av, w1_ref[0].astype(MXU_DTYPE), _NT, preferred_element_type=F32)
        h3 = lax.dot_general(av, w3_ref[0].astype(MXU_DTYPE), _NT, preferred_element_type=F32)
        h1_ref[0] = h1.astype(h1_ref.dtype)
        h3_ref[0] = h3.astype(h3_ref.dtype)
        g_ref[0] = (jax.nn.silu(h1) * h3).astype(g_ref.dtype)

    w_spec = pl.BlockSpec((1, FF_SHARD, D_MODEL), lambda s, i: (s, 0, 0))
    h_spec = pl.BlockSpec((1, tm, FF_SHARD), lambda s, i: (s, i, 0))
    h_shape = jax.ShapeDtypeStruct((N_SHARD, n_tok, FF_SHARD), MXU_DTYPE)
    blk = _nbytes((tm, D_MODEL), a.dtype) + 2 * _nbytes((D_MODEL, FF_SHARD), w1s.dtype) + 3 * _nbytes((tm, FF_SHARD), MXU_DTYPE)
    return pl.pallas_call(
        body, name=name, grid=(N_SHARD, n_tok // tm),
        in_specs=[pl.BlockSpec((tm, D_MODEL), lambda s, i: (i, 0)), w_spec, w_spec],
        out_specs=(h_spec, h_spec, h_spec), out_shape=(h_shape, h_shape, h_shape),
        compiler_params=_params(blk, 6 * _nbytes((tm, FF_SHARD), F32)),
    )(*_hbm(a, w1s, w3s))


def _ffn_down_bwd(df, w2s, h1, h3, *, name, tm=FFN_TOKENS):
    n_tok = df.shape[0]
    tm = min(tm, n_tok)

    def body(df_ref, w2_ref, h1_ref, h3_ref, dh1_ref, dh3_ref):
        dg = lax.dot_general(df_ref[...].astype(MXU_DTYPE), w2_ref[0].astype(MXU_DTYPE), _NT, preferred_element_type=F32)
        h1v = h1_ref[0].astype(F32)
        h3v = h3_ref[0].astype(F32)
        sig = jax.nn.sigmoid(h1v)
        dh3_ref[0] = (dg * (h1v * sig)).astype(dh3_ref.dtype)
        dh1_ref[0] = (dg * h3v * (sig * (1.0 + h1v * (1.0 - sig)))).astype(dh1_ref.dtype)

    h_spec = pl.BlockSpec((1, tm, FF_SHARD), lambda s, i: (s, i, 0))
    h_shape = jax.ShapeDtypeStruct((N_SHARD, n_tok, FF_SHARD), MXU_DTYPE)
    blk = _nbytes((tm, D_MODEL), df.dtype) + _nbytes((FF_SHARD, D_MODEL), w2s.dtype) + 4 * _nbytes((tm, FF_SHARD), MXU_DTYPE)
    return pl.pallas_call(
        body, name=name, grid=(N_SHARD, n_tok // tm),
        in_specs=[pl.BlockSpec((tm, D_MODEL), lambda s, i: (i, 0)),
                  pl.BlockSpec((1, FF_SHARD, D_MODEL), lambda s, i: (s, 0, 0)), h_spec, h_spec],
        out_specs=(h_spec, h_spec), out_shape=(h_shape, h_shape),
        compiler_params=_params(blk, 8 * _nbytes((tm, FF_SHARD), F32)),
    )(*_hbm(df, w2s, h1, h3))


def _ffn_fwd(a, w1s, w3s, w2s, tag):
    h1, h3, g = _ffn_up(a, w1s, w3s, name=f"{tag}_up")
    f = _mm_act([(g, w2s)], "nn", reduce_shards=True, tm=FFN_TOKENS, name=f"{tag}_down")[0]
    return f, (h1, h3, g)


def _ffn_bwd(a, w1s, w3s, w2s, saved, df, tag):
    h1, h3, g = saved
    dh1, dh3 = _ffn_down_bwd(df, w2s, h1, h3, name=f"{tag}_down_bwd")
    da = _mm_act([(dh1, w1s), (dh3, w3s)], "nn", reduce_shards=True, tm=FFN_TOKENS, name=f"{tag}_up_bwd")[0]
    a3 = a[None]
    dw1 = _mm_tn(dh1, a3, tt=FFN_TOKENS, name=f"{tag}_dw1")
    dw3 = _mm_tn(dh3, a3, tt=FFN_TOKENS, name=f"{tag}_dw3")
    dw2 = _mm_tn(g, df[None], tt=FFN_TOKENS, name=f"{tag}_dw2")
    return da, dw1, dw3, dw2


CONV_LANES = 256


def _shift_down(x, d):
    if d == 0:
        return x
    row = lax.broadcasted_iota(jnp.int32, x.shape, 0)
    return jnp.where(row >= d, pltpu.roll(x, d, 0), 0.0)


def _shift_up(x, d):
    if d == 0:
        return x
    n = x.shape[0]
    row = lax.broadcasted_iota(jnp.int32, x.shape, 0)
    return jnp.where(row < n - d, pltpu.roll(x, n - d, 0), 0.0)


def _conv_pre(x, w):
    acc = None
    for j in range(CONV_WIDTH):
        term = w[j:j + 1, :] * _shift_down(x, CONV_WIDTH - 1 - j)
        acc = term if acc is None else acc + term
    return acc


def _conv_fwd(x, w, *, name):
    n_b, n_s, n_c = x.shape
    spec = pl.BlockSpec((1, n_s, CONV_LANES), lambda b, cj: (b, 0, cj))

    def body(x_ref, w_ref, o_ref):
        o_ref[0] = jax.nn.silu(_conv_pre(x_ref[0], w_ref[...]))

    return pl.pallas_call(
        body, name=name, grid=(n_b, n_c // CONV_LANES),
        in_specs=[spec, pl.BlockSpec((CONV_WIDTH, CONV_LANES), lambda b, cj: (0, cj))],
        out_specs=spec, out_shape=jax.ShapeDtypeStruct(x.shape, F32),
        compiler_params=_params(2 * _nbytes((n_s, CONV_LANES), F32), 6 * _nbytes((n_s, CONV_LANES), F32)),
    )(*_hbm(x, w))


def _conv_bwd(x, w, dout, *, name):
    n_b, n_s, n_c = x.shape
    per_part = DN_WIDTH // CONV_LANES
    spec = pl.BlockSpec((1, n_s, CONV_LANES), lambda cj, b: (b, 0, cj))
    do_spec = pl.BlockSpec((1, 1, n_s, CONV_LANES), lambda cj, b: (cj // per_part, b, 0, cj % per_part))
    w_spec = pl.BlockSpec((CONV_WIDTH, CONV_LANES), lambda cj, b: (0, cj))

    def body(x_ref, w_ref, do_ref, dx_ref, dw_ref):
        xv, wv = x_ref[0], w_ref[...]
        pre = _conv_pre(xv, wv)
        sig = jax.nn.sigmoid(pre)
        dpre = do_ref[0, 0] * (sig * (1.0 + pre * (1.0 - sig)))
        dx = None
        first = pl.program_id(1) == 0
        for j in range(CONV_WIDTH):
            d = CONV_WIDTH - 1 - j
            term = wv[j:j + 1, :] * _shift_up(dpre, d)
            dx = term if dx is None else dx + term
            dwj = jnp.sum(dpre * _shift_down(xv, d), axis=0, keepdims=True)
            _accumulate(dw_ref.at[j:j + 1, :], dwj, first)
        dx_ref[0] = dx.astype(dx_ref.dtype)

    return pl.pallas_call(
        body, name=name, grid=(n_c // CONV_LANES, n_b),
        in_specs=[spec, w_spec, do_spec], out_specs=(spec, w_spec),
        out_shape=(jax.ShapeDtypeStruct(x.shape, MXU_DTYPE), jax.ShapeDtypeStruct((CONV_WIDTH, n_c), F32)),
        compiler_params=_params(3 * _nbytes((n_s, CONV_LANES), F32), 8 * _nbytes((n_s, CONV_LANES), F32)),
    )(*_hbm(x, w, dout))


_BNT = (((2,), (2,)), ((0,), (0,)))
_BNN = (((2,), (1,)), ((0,), (0,)))
_BTN = (((1,), (1,)), ((0,), (0,)))
DN_PREP_CHUNKS = 8
DN_SCAN_HEADS = 4
N_DOUBLINGS = 5


def _fdot(a, b, dims):
    return lax.dot_general(a, b, dims, precision=lax.Precision.HIGHEST, preferred_element_type=F32)


def _hdot(a, b, dims):
    return lax.dot_general(a, b, dims, precision=lax.Precision.HIGH, preferred_element_type=F32)


def _solve_by_doubling(a, rhs_u, rhs_w):
    row = lax.broadcasted_iota(jnp.int32, (CHUNK, CHUNK), 0)
    col = lax.broadcasted_iota(jnp.int32, (CHUNK, CHUNK), 1)
    inv = jnp.where(row == col, 1.0, 0.0) - a
    power = a
    for _ in range(N_DOUBLINGS):
        power = _hdot(power, power, _BNN)
        inv = inv + _hdot(inv, power, _BNN)
    return _hdot(inv, rhs_u, _BNN), _hdot(inv, rhs_w, _BNN), inv


@jax.custom_vjp
def _solve_saved(a, rhs_u, rhs_w, inv, u, w):
    return u, w


def _solve_saved_fwd(a, rhs_u, rhs_w, inv, u, w):
    return (u, w), (inv, u, w)


def _solve_saved_bwd(res, cts):
    inv, u, w = res
    gu = _hdot(inv, cts[0], _BTN)
    gw = _hdot(inv, cts[1], _BTN)
    da = -(_hdot(gu, u, _BNT) + _hdot(gw, w, _BNT))
    return da, gu, gw, jnp.zeros_like(inv), jnp.zeros_like(u), jnp.zeros_like(w)


_solve_saved.defvjp(_solve_saved_fwd, _solve_saved_bwd)


def _dn_prep_fn(solve, qc, kc, vc, bl, lac, lar, a_log, dt_bias):
    q = qc * lax.rsqrt(jnp.sum(qc * qc, axis=-1, keepdims=True) + EPS) * (DN_HEAD_DIM ** -0.5)
    k = kc * lax.rsqrt(jnp.sum(kc * kc, axis=-1, keepdims=True) + EPS)
    beta = jax.nn.sigmoid(bl)
    neg_a = -jnp.exp(a_log)
    lgc = neg_a * jax.nn.softplus(lac + dt_bias)
    lgr = neg_a * jax.nn.softplus(lar + dt_bias)
    row = lax.broadcasted_iota(jnp.int32, (CHUNK, CHUNK), 0)
    col = lax.broadcasted_iota(jnp.int32, (CHUNK, CHUNK), 1)
    causal, strict = row >= col, row > col
    g_c = jnp.sum(jnp.where(causal, lgr, 0.0), axis=-1, keepdims=True)
    g_r = jnp.sum(jnp.where(row <= col, lgc, 0.0), axis=-2, keepdims=True)
    decay = jnp.exp(jnp.where(causal, g_c - g_r, -jnp.inf))
    kb = k * beta
    a = jnp.where(strict, _mdot(kb, k, _BNT) * decay, 0.0)
    u, w, extra = solve(a, vc * beta, kb * jnp.exp(g_c))
    attn = _mdot(q, k, _BNT) * decay
    g_last = jnp.sum(lgc, axis=-2, keepdims=True)
    return q * jnp.exp(g_c), k * jnp.exp(g_last - g_c), u, w, attn, g_last, extra


PAIR = 2
PAIR_LANES = PAIR * DN_HEAD_DIM


def _dn_prep_specs(n_cb):
    tok = n_cb * CHUNK
    wide = pl.BlockSpec((1, PAIR, tok, DN_HEAD_DIM), lambda p, b, j: (b, p, j, 0))
    col = pl.BlockSpec((1, PAIR, tok, 1), lambda p, b, j: (b, p, j, 0))
    rowv = pl.BlockSpec((1, PAIR, n_cb, 1, CHUNK), lambda p, b, j: (b, p, j, 0, 0))
    one = pl.BlockSpec((1, PAIR, n_cb, 1, 1), lambda p, b, j: (b, p, j, 0, 0))
    head = pl.BlockSpec((PAIR, 1, 1), lambda p, b, j: (p, 0, 0))
    lanes = lambda part: pl.BlockSpec((1, tok, PAIR_LANES), lambda p, b, j: (b, j, part * (DN_HEADS // PAIR) + p))
    return wide, col, rowv, one, head, lanes


def _split_pair(x, n_cb):
    halves = [x[:, h * DN_HEAD_DIM:(h + 1) * DN_HEAD_DIM].reshape(n_cb, CHUNK, DN_HEAD_DIM) for h in range(PAIR)]
    return jnp.concatenate(halves, axis=0)


def _join_pair(chunks, tok):
    per_head = chunks.reshape(PAIR, tok, DN_HEAD_DIM)
    return jnp.concatenate([per_head[h] for h in range(PAIR)], axis=-1)


def _dn_prep_load(n_cb, q_ref, k_ref, v_ref, bl_ref, lac_ref, lar_ref, al_ref, dt_ref):
    colv = lambda r: r[0].reshape(PAIR * n_cb, CHUNK, 1)
    return (_split_pair(q_ref[0], n_cb), _split_pair(k_ref[0], n_cb), _split_pair(v_ref[0], n_cb), colv(bl_ref),
            colv(lac_ref), lar_ref[0].reshape(PAIR * n_cb, 1, CHUNK), al_ref[...], dt_ref[...])


def _dn_prep_pair_fn(n_cb, solve, qc, kc, vc, bl, lac, lar, a_log, dt_bias):
    per_chunk = lambda t: jnp.broadcast_to(t[:, None], (PAIR, n_cb, 1, 1)).reshape(PAIR * n_cb, 1, 1)
    return _dn_prep_fn(solve, qc, kc, vc, bl, lac, lar, per_chunk(a_log), per_chunk(dt_bias))


def _dn_prep(qkv, bl, lac, lar, a_log, dt_bias, *, name):
    n_b, n_s, _ = qkv.shape
    n_cb = min(DN_PREP_CHUNKS, n_s // CHUNK)
    tok = n_cb * CHUNK
    wide, col, rowv, one, head, lanes = _dn_prep_specs(n_cb)

    def body(*refs):
        outs = _dn_prep_pair_fn(n_cb, _solve_by_doubling, *_dn_prep_load(n_cb, *refs[:8]))
        for ref, val in zip(refs[8:13], outs[:5]):
            ref[0] = val.reshape(PAIR, tok, DN_HEAD_DIM)
        refs[13][0] = outs[5].reshape(PAIR, n_cb, 1, 1)
        refs[14][0] = outs[6].reshape(PAIR, tok, DN_HEAD_DIM)

    big = jax.ShapeDtypeStruct((n_b, DN_HEADS, n_s, DN_HEAD_DIM), F32)
    return pl.pallas_call(
        body, name=name, grid=(DN_HEADS // PAIR, n_b, n_s // tok),
        in_specs=[lanes(0), lanes(1), lanes(2), col, col, rowv, head, head],
        out_specs=(wide, wide, wide, wide, wide, one, wide),
        out_shape=(big, big, big, big, big, jax.ShapeDtypeStruct((n_b, DN_HEADS, n_s // CHUNK, 1, 1), F32), big),
        compiler_params=_params(11 * PAIR * _nbytes((tok, LANES), F32), 48 * PAIR * _nbytes((tok, LANES), F32)),
    )(*_hbm(qkv, qkv, qkv, bl, lac, lar, a_log, dt_bias))


def _dn_prep_bwd(qkv, bl, lac, lar, a_log, dt_bias, inv, u, w, cts, *, name):
    n_b, n_s, _ = qkv.shape
    n_cb = min(DN_PREP_CHUNKS, n_s // CHUNK)
    tok = n_cb * CHUNK
    wide, col, rowv, one, head, lanes = _dn_prep_specs(n_cb)

    def body(*refs):
        prim = _dn_prep_load(n_cb, *refs[:8])
        chunks = lambda r: r[0].reshape(PAIR * n_cb, CHUNK, DN_HEAD_DIM)
        inv_v, u_v, w_v = chunks(refs[8]), chunks(refs[9]), chunks(refs[10])
        ct = tuple(chunks(r) for r in refs[11:16]) + (refs[16][0].reshape(PAIR * n_cb, 1, 1),)

        def fn(*args):
            solve = lambda a, ru, rw: _solve_saved(a, ru, rw, inv_v, u_v, w_v) + (None,)
            return _dn_prep_pair_fn(n_cb, solve, *args)[:6]

        _, pull = jax.vjp(fn, *prim)
        dq, dk, dv, dbl, dlac, dlar, dal, ddt = pull(ct)
        outs = refs[17:]
        for part, val in enumerate((dq, dk, dv)):
            outs[0][part, 0] = _join_pair(val, tok)
        outs[1][0] = dbl.reshape(PAIR, tok, 1)
        outs[2][0] = dlac.reshape(PAIR, tok, 1)
        outs[3][0] = dlar.reshape(PAIR, n_cb, 1, CHUNK)
        first = jnp.logical_and(pl.program_id(1) == 0, pl.program_id(2) == 0)
        _accumulate(outs[4], dal, first)
        _accumulate(outs[5], ddt, first)

    dqkv_spec = pl.BlockSpec((3, 1, tok, PAIR_LANES), lambda p, b, j: (0, b, j, p))
    return pl.pallas_call(
        body, name=name, grid=(DN_HEADS // PAIR, n_b, n_s // tok),
        in_specs=[lanes(0), lanes(1), lanes(2), col, col, rowv, head, head, wide, wide, wide, wide, wide, wide, wide, wide, one],
        out_specs=(dqkv_spec, col, col, rowv, head, head),
        out_shape=(jax.ShapeDtypeStruct((3, n_b, n_s, DN_WIDTH), F32), jax.ShapeDtypeStruct(bl.shape, F32),
                   jax.ShapeDtypeStruct(lac.shape, F32), jax.ShapeDtypeStruct(lar.shape, F32),
                   jax.ShapeDtypeStruct(a_log.shape, F32), jax.ShapeDtypeStruct(dt_bias.shape, F32)),
        compiler_params=_params(21 * PAIR * _nbytes((tok, LANES), F32), 64 * PAIR * _nbytes((tok, LANES), F32)),
    )(*_hbm(qkv, qkv, qkv, bl, lac, lar, a_log, dt_bias, inv, u, w, *cts))


def _dn_step(state, q, k, u, w, a, gl):
    v_new = u - _mdot(w, state, _BNN)
    o = _mdot(q, state, _BNN) + _mdot(a, v_new, _BNN)
    return state * jnp.exp(gl) + _mdot(k, v_new, _BTN), o


def _dn_scan_specs(n_cb, n_blocks, reverse):
    tok = n_cb * CHUNK
    jj = (lambda j: n_blocks - 1 - j) if reverse else (lambda j: j)
    wide = pl.BlockSpec((1, DN_SCAN_HEADS, tok, DN_HEAD_DIM), lambda b, h, j: (b, h, jj(j), 0))
    one = pl.BlockSpec((1, DN_SCAN_HEADS, n_cb, 1, 1), lambda b, h, j: (b, h, jj(j), 0, 0))
    st = pl.BlockSpec((1, DN_SCAN_HEADS, n_cb, DN_HEAD_DIM, DN_HEAD_DIM), lambda b, h, j: (b, h, jj(j), 0, 0))
    return wide, one, st


def _dn_scan(qd, kd, u, w, attn, g_last, *, name):
    n_b, n_h, n_s, _ = qd.shape
    n_cb = min(DN_PREP_CHUNKS, n_s // CHUNK)
    n_blocks = n_s // (n_cb * CHUNK)
    wide, one, st = _dn_scan_specs(n_cb, n_blocks, False)

    def body(qd_ref, kd_ref, u_ref, w_ref, a_ref, gl_ref, o_ref, st_ref, state_ref):
        @pl.when(pl.program_id(2) == 0)
        def _():
            state_ref[...] = jnp.zeros(state_ref.shape, F32)

        def step(n, state):
            rows = pl.ds(pl.multiple_of(n * CHUNK, CHUNK), CHUNK)
            st_ref[0, :, n] = state
            state, o = _dn_step(state, qd_ref[0, :, rows, :], kd_ref[0, :, rows, :], u_ref[0, :, rows, :],
                                w_ref[0, :, rows, :], a_ref[0, :, rows, :], gl_ref[0, :, n])
            o_ref[0, :, rows, :] = o
            return state

        state_ref[...] = lax.fori_loop(0, n_cb, step, state_ref[...])

    return pl.pallas_call(
        body, name=name, grid=(n_b, n_h // DN_SCAN_HEADS, n_blocks),
        in_specs=[wide, wide, wide, wide, wide, one], out_specs=(wide, st),
        out_shape=(jax.ShapeDtypeStruct(qd.shape, F32),
                   jax.ShapeDtypeStruct((n_b, n_h, n_s // CHUNK, DN_HEAD_DIM, DN_HEAD_DIM), F32)),
        scratch_shapes=[pltpu.VMEM((DN_SCAN_HEADS, DN_HEAD_DIM, DN_HEAD_DIM), F32)],
        compiler_params=_params(8 * _nbytes((DN_SCAN_HEADS, n_cb * CHUNK, LANES), F32), 8 << 20),
    )(*_hbm(qd, kd, u, w, attn, g_last))


def _dn_scan_bwd(qd, kd, u, w, attn, g_last, states, do, *, name):
    n_b, n_h, n_s, _ = qd.shape
    n_cb = min(DN_PREP_CHUNKS, n_s // CHUNK)
    n_blocks = n_s // (n_cb * CHUNK)
    wide, one, st = _dn_scan_specs(n_cb, n_blocks, True)

    def body(qd_ref, kd_ref, u_ref, w_ref, a_ref, gl_ref, st_ref, do_ref,
             dq_ref, dk_ref, du_ref, dw_ref, da_ref, dgl_ref, dstate_ref):
        @pl.when(pl.program_id(2) == 0)
        def _():
            dstate_ref[...] = jnp.zeros(dstate_ref.shape, F32)

        def step(i, dstate):
            n = n_cb - 1 - i
            rows = pl.ds(pl.multiple_of(n * CHUNK, CHUNK), CHUNK)
            _, pull = jax.vjp(_dn_step, st_ref[0, :, n], qd_ref[0, :, rows, :], kd_ref[0, :, rows, :],
                              u_ref[0, :, rows, :], w_ref[0, :, rows, :], a_ref[0, :, rows, :], gl_ref[0, :, n])
            dstate, dq, dk, du, dw, da, dgl = pull((dstate, do_ref[0, :, rows, :]))
            dq_ref[0, :, rows, :] = dq
            dk_ref[0, :, rows, :] = dk
            du_ref[0, :, rows, :] = du
            dw_ref[0, :, rows, :] = dw
            da_ref[0, :, rows, :] = da
            dgl_ref[0, :, n] = dgl
            return dstate

        dstate_ref[...] = lax.fori_loop(0, n_cb, step, dstate_ref[...])

    big = jax.ShapeDtypeStruct(qd.shape, F32)
    return pl.pallas_call(
        body, name=name, grid=(n_b, n_h // DN_SCAN_HEADS, n_blocks),
        in_specs=[wide, wide, wide, wide, wide, one, st, wide],
        out_specs=(wide, wide, wide, wide, wide, one),
        out_shape=(big, big, big, big, big, jax.ShapeDtypeStruct(g_last.shape, F32)),
        scratch_shapes=[pltpu.VMEM((DN_SCAN_HEADS, DN_HEAD_DIM, DN_HEAD_DIM), F32)],
        compiler_params=_params(13 * _nbytes((DN_SCAN_HEADS, n_cb * CHUNK, LANES), F32), 8 << 20),
    )(*_hbm(qd, kd, u, w, attn, g_last, states, do))


def _dn_post_fn(o, z, gain):
    return o * lax.rsqrt(jnp.mean(o * o, axis=-1, keepdims=True) + EPS) * gain * jax.nn.silu(z)


_HEAD_ROWS = lambda n_s: pl.BlockSpec((1, PAIR, n_s, DN_HEAD_DIM), lambda b, p: (b, p, 0, 0))
_PAIR_LANES = lambda n_s: pl.BlockSpec((1, n_s, PAIR_LANES), lambda b, p: (b, 0, p))
_HEAD_GAIN = pl.BlockSpec((1, DN_HEAD_DIM), lambda b, p: (0, 0))


def _pair_heads(x):
    return jnp.stack([x[:, h * DN_HEAD_DIM:(h + 1) * DN_HEAD_DIM] for h in range(PAIR)])


def _pair_lanes(x):
    return jnp.concatenate([x[h] for h in range(PAIR)], axis=-1)


def _dn_post(o, z, gain, *, name):
    n_b, _, n_s, _ = o.shape

    def body(o_ref, z_ref, g_ref, out_ref):
        out = _dn_post_fn(o_ref[0], _pair_heads(z_ref[0]), g_ref[...])
        out_ref[0] = _pair_lanes(out).astype(out_ref.dtype)

    lanes = _PAIR_LANES(n_s)
    return pl.pallas_call(
        body, name=name, grid=(n_b, DN_HEADS // PAIR), in_specs=[_HEAD_ROWS(n_s), lanes, _HEAD_GAIN], out_specs=lanes,
        out_shape=jax.ShapeDtypeStruct(z.shape, MXU_DTYPE),
        compiler_params=_params(3 * PAIR * _nbytes((n_s, LANES), F32), 6 * PAIR * _nbytes((n_s, LANES), F32)),
    )(*_hbm(o, z, gain))


def _dn_post_bwd(o, z, gain, dout, *, name):
    n_b, _, n_s, _ = o.shape

    def body(o_ref, z_ref, g_ref, dout_ref, do_ref, dz_ref, dg_ref):
        _, pull = jax.vjp(_dn_post_fn, o_ref[0], _pair_heads(z_ref[0]), g_ref[...])
        do, dz, dg = pull(_pair_heads(dout_ref[0].astype(F32)))
        do_ref[0] = do
        dz_ref[0] = _pair_lanes(dz).astype(dz_ref.dtype)
        _accumulate(dg_ref, dg, jnp.logical_and(pl.program_id(0) == 0, pl.program_id(1) == 0))

    rows, lanes = _HEAD_ROWS(n_s), _PAIR_LANES(n_s)
    return pl.pallas_call(
        body, name=name, grid=(n_b, DN_HEADS // PAIR), in_specs=[rows, lanes, _HEAD_GAIN, lanes],
        out_specs=(rows, lanes, _HEAD_GAIN),
        out_shape=(jax.ShapeDtypeStruct(o.shape, F32), jax.ShapeDtypeStruct(z.shape, MXU_DTYPE),
                   jax.ShapeDtypeStruct((1, DN_HEAD_DIM), F32)),
        compiler_params=_params(5 * PAIR * _nbytes((n_s, LANES), F32), 10 * PAIR * _nbytes((n_s, LANES), F32)),
    )(*_hbm(o, z, gain, dout))


S5_SCAN_LANES = 256
TILE_ROWS = SUBLANES


def _s5_prep_fn(lam_re, lam_im, log_step, bt_re, bt_im, c_im):
    lr = jnp.minimum(lam_re, -1e-4)
    step = jnp.exp(log_step)
    mag = jnp.exp(lr * step)
    ang = lam_im * step
    lb_re = mag * jnp.cos(ang)
    lb_im = mag * jnp.sin(ang)
    den = lr * lr + lam_im * lam_im
    coef_re = ((lb_re - 1.0) * lr + lb_im * lam_im) / den
    coef_im = (lb_im * lr - (lb_re - 1.0) * lam_im) / den
    return (lb_re, lb_im, coef_re * bt_re - coef_im * bt_im, coef_re * bt_im + coef_im * bt_re, -c_im)


def _s5_prep(lam_re, lam_im, log_step, bt_re, bt_im, c_im, *, name):
    def body(*refs):
        outs = _s5_prep_fn(*(r[...] for r in refs[:6]))
        for ref, val in zip(refs[6:], outs):
            ref[...] = val

    vec = jax.ShapeDtypeStruct(lam_re.shape, F32)
    mat = jax.ShapeDtypeStruct(bt_re.shape, F32)
    return pl.pallas_call(body, name=name, out_shape=(vec, vec, mat, mat, mat))(lam_re, lam_im, log_step, bt_re, bt_im, c_im)


def _s5_prep_bwd(lam_re, lam_im, log_step, bt_re, bt_im, c_im, cts, *, name):
    def body(*refs):
        _, pull = jax.vjp(_s5_prep_fn, *(r[...] for r in refs[:6]))
        grads = pull(tuple(r[...] for r in refs[6:11]))
        for ref, val in zip(refs[11:], grads):
            ref[...] = val

    shapes = tuple(jax.ShapeDtypeStruct(a.shape, F32) for a in (lam_re, lam_im, log_step, bt_re, bt_im, c_im))
    return pl.pallas_call(body, name=name, out_shape=shapes)(lam_re, lam_im, log_step, bt_re, bt_im, c_im, *cts)


def _cmul(ar, ai, br, bi):
    return ar * br - ai * bi, ar * bi + ai * br


def _s5_powers(lr, li):
    pows = [(lr, li)]
    for _ in range(TILE_ROWS - 1):
        pows.append(_cmul(pows[-1][0], pows[-1][1], lr, li))
    return pows


def _s5_carry_table(pows, n_lanes, reverse):
    row = lax.broadcasted_iota(jnp.int32, (TILE_ROWS, n_lanes), 0)
    t_re = jnp.zeros((TILE_ROWS, n_lanes), F32)
    t_im = jnp.zeros((TILE_ROWS, n_lanes), F32)
    for r in range(TILE_ROWS):
        p_re, p_im = pows[TILE_ROWS - 1 - r] if reverse else pows[r]
        t_re = jnp.where(row == r, p_re, t_re)
        t_im = jnp.where(row == r, p_im, t_im)
    return t_re, t_im


def _s5_tile(y_re, y_im, pows, reverse):
    d = 1
    while d < TILE_ROWS:
        p_re, p_im = pows[d - 1]
        if reverse:
            s_re, s_im = _shift_up(y_re, d), _shift_up(y_im, d)
        else:
            s_re, s_im = _shift_down(y_re, d), _shift_down(y_im, d)
        m_re, m_im = _cmul(p_re, p_im, s_re, s_im)
        y_re, y_im = y_re + m_re, y_im + m_im
        d *= 2
    return y_re, y_im


S5_BLOCKS = N_SHARD
S5_BLOCK_CH = S5_WIDTH // S5_BLOCKS
S5_BLOCK_LANES = S5_LANES // S5_BLOCKS
SCAN_PER_BLOCK = S5_BLOCK_LANES // S5_SCAN_LANES


def _s5_scan_specs(n_s, order):
    L = S5_SCAN_LANES

    def cat_spec(part):
        return pl.BlockSpec((1, 1, n_s, L), lambda *g: (order(*g)[1] // SCAN_PER_BLOCK, order(*g)[0], 0,
                                                        part * SCAN_PER_BLOCK + order(*g)[1] % SCAN_PER_BLOCK))

    one = pl.BlockSpec((1, 1, n_s, L), lambda *g: (order(*g)[1] // SCAN_PER_BLOCK, order(*g)[0], 0,
                                                   order(*g)[1] % SCAN_PER_BLOCK))
    lam = pl.BlockSpec((1, L), lambda *g: (0, order(*g)[1]))
    return cat_spec, one, lam


def _s5_scan(bu, lb_re, lb_im, *, name):
    n_blk, n_b, n_s, _ = bu.shape
    n_lb = S5_LANES // S5_SCAN_LANES
    n_tiles = n_s // TILE_ROWS
    L = S5_SCAN_LANES

    def body(re_ref, im_ref, lr_ref, li_ref, xr_ref, xi_ref):
        pows = _s5_powers(lr_ref[...], li_ref[...])
        t_re, t_im = _s5_carry_table(pows, L, False)

        def step(i, carry):
            rows = pl.ds(pl.multiple_of(i * TILE_ROWS, TILE_ROWS), TILE_ROWS)
            y_re, y_im = _s5_tile(re_ref[0, 0, rows, :], im_ref[0, 0, rows, :], pows, False)
            c_re, c_im = _cmul(t_re, t_im, carry[0], carry[1])
            y_re, y_im = y_re + c_re, y_im + c_im
            xr_ref[0, 0, rows, :] = y_re
            xi_ref[0, 0, rows, :] = y_im
            return y_re[TILE_ROWS - 1:, :], y_im[TILE_ROWS - 1:, :]

        zero = jnp.zeros((1, L), F32)
        lax.fori_loop(0, n_tiles, step, (zero, zero))

    cat_spec, one, lam = _s5_scan_specs(n_s, lambda b, j: (b, j))
    x_shape = jax.ShapeDtypeStruct((n_blk, n_b, n_s, S5_BLOCK_LANES), F32)
    return pl.pallas_call(
        body, name=name, grid=(n_b, n_lb),
        in_specs=[cat_spec(0), cat_spec(1), lam, lam],
        out_specs=(one, one), out_shape=(x_shape, x_shape),
        compiler_params=_params(4 * _nbytes((n_s, L), F32), 4 << 20),
    )(*_hbm(bu, bu, lb_re, lb_im))


def _s5_scan_bwd(dx, x_re, x_im, lb_re, lb_im, *, name):
    n_blk, n_b, n_s, _ = dx.shape
    n_lb = S5_LANES // S5_SCAN_LANES
    n_tiles = n_s // TILE_ROWS
    L = S5_SCAN_LANES

    def body(dr_ref, di_ref, xr_ref, xi_ref, lr_ref, li_ref, ar_ref, ai_ref, dlr_ref, dli_ref):
        pows = _s5_powers(lr_ref[...], -li_ref[...])
        t_re, t_im = _s5_carry_table(pows, L, True)
        row = lax.broadcasted_iota(jnp.int32, (TILE_ROWS, L), 0)

        def step(k, carry):
            c_re, c_im, s_re, s_im = carry
            i = n_tiles - 1 - k
            rows = pl.ds(pl.multiple_of(i * TILE_ROWS, TILE_ROWS), TILE_ROWS)
            a_re, a_im = _s5_tile(dr_ref[0, 0, rows, :], di_ref[0, 0, rows, :], pows, True)
            m_re, m_im = _cmul(t_re, t_im, c_re, c_im)
            a_re, a_im = a_re + m_re, a_im + m_im
            ar_ref[0, 0, rows, :] = a_re.astype(ar_ref.dtype)
            ai_ref[0, 0, rows, :] = a_im.astype(ai_ref.dtype)
            prev = pl.ds(pl.multiple_of(jnp.maximum(i - 1, 0) * TILE_ROWS, TILE_ROWS), TILE_ROWS)
            keep = jnp.where(i > 0, 1.0, 0.0)
            last_re = xr_ref[0, 0, prev, :][TILE_ROWS - 1:, :] * keep
            last_im = xi_ref[0, 0, prev, :][TILE_ROWS - 1:, :] * keep
            xp_re = jnp.where(row == 0, last_re, _shift_down(xr_ref[0, 0, rows, :], 1))
            xp_im = jnp.where(row == 0, last_im, _shift_down(xi_ref[0, 0, rows, :], 1))
            s_re = s_re + a_re * xp_re + a_im * xp_im
            s_im = s_im + a_im * xp_re - a_re * xp_im
            return a_re[:1, :], a_im[:1, :], s_re, s_im

        zero = jnp.zeros((1, L), F32)
        zt = jnp.zeros((TILE_ROWS, L), F32)
        _, _, s_re, s_im = lax.fori_loop(0, n_tiles, step, (zero, zero, zt, zt))
        first = pl.program_id(1) == 0
        _accumulate(dlr_ref, jnp.sum(s_re, axis=0, keepdims=True), first)
        _accumulate(dli_ref, jnp.sum(s_im, axis=0, keepdims=True), first)

    cat_spec, one, lam = _s5_scan_specs(n_s, lambda j, b: (b, j))
    a_shape = jax.ShapeDtypeStruct((n_blk, n_b, n_s, S5_BLOCK_LANES), MXU_DTYPE)
    lam_shape = jax.ShapeDtypeStruct((1, S5_LANES), F32)
    return pl.pallas_call(
        body, name=name, grid=(n_lb, n_b),
        in_specs=[cat_spec(0), cat_spec(1), one, one, lam, lam],
        out_specs=(one, one, lam, lam),
        out_shape=(a_shape, a_shape, lam_shape, lam_shape),
        compiler_params=_params(5 * _nbytes((n_s, L), F32), 4 << 20),
    )(*_hbm(dx, dx, x_re, x_im, lb_re, lb_im))


def _scan_rows(i):
    return pl.ds(pl.multiple_of(i * TILE_ROWS, TILE_ROWS), TILE_ROWS)


def _s5_mix_specs(n_s, order):
    jb = lambda *g: order(*g)[0]
    bb = lambda *g: order(*g)[1]
    act = pl.BlockSpec((1, 1, n_s, S5_BLOCK_CH), lambda *g: (bb(*g), 0, 0, jb(*g)))
    state = pl.BlockSpec((1, 1, n_s, S5_BLOCK_LANES), lambda *g: (jb(*g), bb(*g), 0, 0))
    lam = pl.BlockSpec((1, S5_BLOCK_LANES), lambda *g: (0, jb(*g)))
    w_in = pl.BlockSpec((1, S5_BLOCK_CH, S5_BLOCK_LANES), lambda *g: (jb(*g), 0, 0))
    w_out = pl.BlockSpec((1, S5_BLOCK_LANES, S5_BLOCK_CH), lambda *g: (jb(*g), 0, 0))
    return act, state, lam, w_in, w_out


def _s5_mix(u, wb_re, wb_im, lb_re, lb_im, wc_re, wc_im, *, name):
    n_b, n_s, _ = u.shape
    n_blk = S5_BLOCKS
    lanes = lambda t: t[:, None]
    n_tiles = n_s // TILE_ROWS
    L = S5_BLOCK_LANES

    def body(u_ref, wbr_ref, wbi_ref, lr_ref, li_ref, wcr_ref, wci_ref, y_ref, xr_ref, xi_ref):
        uv = u_ref[0, 0].astype(MXU_DTYPE)
        xr_ref[0, 0] = lax.dot_general(uv, wbr_ref[0].astype(MXU_DTYPE), _NN, preferred_element_type=F32)
        xi_ref[0, 0] = lax.dot_general(uv, wbi_ref[0].astype(MXU_DTYPE), _NN, preferred_element_type=F32)
        pows = _s5_powers(lr_ref[...], li_ref[...])
        t_re, t_im = _s5_carry_table(pows, L, False)

        def step(i, carry):
            rows = _scan_rows(i)
            y_re, y_im = _s5_tile(xr_ref[0, 0, rows, :], xi_ref[0, 0, rows, :], pows, False)
            c_re, c_im = _cmul(t_re, t_im, carry[0], carry[1])
            y_re, y_im = y_re + c_re, y_im + c_im
            xr_ref[0, 0, rows, :] = y_re
            xi_ref[0, 0, rows, :] = y_im
            return y_re[TILE_ROWS - 1:, :], y_im[TILE_ROWS - 1:, :]

        zero = jnp.zeros((1, L), F32)
        lax.fori_loop(0, n_tiles, step, (zero, zero))
        y_ref[0, 0] = (
            lax.dot_general(xr_ref[0, 0].astype(MXU_DTYPE), wcr_ref[0].astype(MXU_DTYPE), _NN, preferred_element_type=F32)
            + lax.dot_general(xi_ref[0, 0].astype(MXU_DTYPE), wci_ref[0].astype(MXU_DTYPE), _NN, preferred_element_type=F32))

    act, state, lam, w_in, w_out = _s5_mix_specs(n_s, lambda b, j: (j, b))
    x_shape = jax.ShapeDtypeStruct((n_blk, n_b, n_s, L), F32)
    return pl.pallas_call(
        body, name=name, grid=(n_b, n_blk),
        in_specs=[act, w_in, w_in, lam, lam, w_out, w_out], out_specs=(act, state, state),
        out_shape=(jax.ShapeDtypeStruct((n_b, 1, n_s, S5_WIDTH), F32), x_shape, x_shape),
        compiler_params=_params(2 * _nbytes((n_s, L), F32) + 2 * _nbytes((n_s, S5_BLOCK_CH), F32), 3 * _nbytes((n_s, L), F32)),
    )(*_hbm(lanes(u), wb_re, wb_im, lb_re, lb_im, wc_re, wc_im))


def _s5_mix_bwd(dy, du_skip, u, x_re, x_im, wb_re, wb_im, lb_re, lb_im, wc_re, wc_im, *, name):
    n_b, n_s, _ = u.shape
    n_blk = S5_BLOCKS
    lanes = lambda t: t[:, None]
    n_tiles = n_s // TILE_ROWS
    L = S5_BLOCK_LANES

    def body(dy_ref, ds_ref, u_ref, xr_ref, xi_ref, wbr_ref, wbi_ref, lr_ref, li_ref, wcr_ref, wci_ref,
             du_ref, dwbr_ref, dwbi_ref, dlr_ref, dli_ref, dwcr_ref, dwci_ref, ar_ref, ai_ref):
        dyv = dy_ref[0, 0].astype(MXU_DTYPE)
        ar_ref[...] = lax.dot_general(dyv, wcr_ref[0].astype(MXU_DTYPE), _NT, preferred_element_type=F32)
        ai_ref[...] = lax.dot_general(dyv, wci_ref[0].astype(MXU_DTYPE), _NT, preferred_element_type=F32)
        pows = _s5_powers(lr_ref[...], -li_ref[...])
        t_re, t_im = _s5_carry_table(pows, L, True)
        row = lax.broadcasted_iota(jnp.int32, (TILE_ROWS, L), 0)

        def step(k, carry):
            c_re, c_im, s_re, s_im = carry
            i = n_tiles - 1 - k
            rows = _scan_rows(i)
            a_re, a_im = _s5_tile(ar_ref[rows, :], ai_ref[rows, :], pows, True)
            m_re, m_im = _cmul(t_re, t_im, c_re, c_im)
            a_re, a_im = a_re + m_re, a_im + m_im
            ar_ref[rows, :] = a_re
            ai_ref[rows, :] = a_im
            prev = _scan_rows(jnp.maximum(i - 1, 0))
            keep = jnp.where(i > 0, 1.0, 0.0)
            last_re = xr_ref[0, 0, prev, :][TILE_ROWS - 1:, :] * keep
            last_im = xi_ref[0, 0, prev, :][TILE_ROWS - 1:, :] * keep
            xp_re = jnp.where(row == 0, last_re, _shift_down(xr_ref[0, 0, rows, :], 1))
            xp_im = jnp.where(row == 0, last_im, _shift_down(xi_ref[0, 0, rows, :], 1))
            s_re = s_re + a_re * xp_re + a_im * xp_im
            s_im = s_im + a_im * xp_re - a_re * xp_im
            return a_re[:1, :], a_im[:1, :], s_re, s_im

        zero = jnp.zeros((1, L), F32)
        zt = jnp.zeros((TILE_ROWS, L), F32)
        _, _, s_re, s_im = lax.fori_loop(0, n_tiles, step, (zero, zero, zt, zt))
        first = pl.program_id(1) == 0
        _accumulate(dlr_ref, jnp.sum(s_re, axis=0, keepdims=True), first)
        _accumulate(dli_ref, jnp.sum(s_im, axis=0, keepdims=True), first)
        a_re, a_im = ar_ref[...].astype(MXU_DTYPE), ai_ref[...].astype(MXU_DTYPE)
        du = (lax.dot_general(a_re, wbr_ref[0].astype(MXU_DTYPE), _NT, preferred_element_type=F32)
              + lax.dot_general(a_im, wbi_ref[0].astype(MXU_DTYPE), _NT, preferred_element_type=F32))
        du_ref[0, 0] = (du + ds_ref[0, 0]).astype(du_ref.dtype)
        uv = u_ref[0, 0].astype(MXU_DTYPE)
        _accumulate(dwbr_ref, lax.dot_general(uv, a_re, _TN, preferred_element_type=F32)[None], first)
        _accumulate(dwbi_ref, lax.dot_general(uv, a_im, _TN, preferred_element_type=F32)[None], first)
        _accumulate(dwcr_ref, lax.dot_general(xr_ref[0, 0].astype(MXU_DTYPE), dyv, _TN, preferred_element_type=F32)[None], first)
        _accumulate(dwci_ref, lax.dot_general(xi_ref[0, 0].astype(MXU_DTYPE), dyv, _TN, preferred_element_type=F32)[None], first)

    act, state, lam, w_in, w_out = _s5_mix_specs(n_s, lambda j, b: (j, b))
    lam_shape = jax.ShapeDtypeStruct((1, S5_LANES), F32)
    return pl.pallas_call(
        body, name=name, grid=(n_blk, n_b),
        in_specs=[act, act, act, state, state, w_in, w_in, lam, lam, w_out, w_out],
        out_specs=(act, w_in, w_in, lam, lam, w_out, w_out),
        out_shape=(jax.ShapeDtypeStruct((n_b, 1, n_s, S5_WIDTH), MXU_DTYPE), jax.ShapeDtypeStruct(wb_re.shape, F32),
                   jax.ShapeDtypeStruct(wb_im.shape, F32), lam_shape, lam_shape,
                   jax.ShapeDtypeStruct(wc_re.shape, F32), jax.ShapeDtypeStruct(wc_im.shape, F32)),
        scratch_shapes=[pltpu.VMEM((n_s, L), F32), pltpu.VMEM((n_s, L), F32)],
        compiler_params=_params(2 * _nbytes((n_s, L), F32) + 4 * _nbytes((n_s, S5_BLOCK_CH), F32), 5 * _nbytes((n_s, L), F32)),
    )(*_hbm(lanes(dy), lanes(du_skip), lanes(u), x_re, x_im, wb_re, wb_im, lb_re, lb_im, wc_re, wc_im))


def _s5_out_fn(ymm, u, d_skip, w_glu, b_glu):
    y = jax.nn.gelu(ymm + d_skip * u)
    return y * jax.nn.sigmoid(_mdot(y, w_glu, _NN) + b_glu)


def _s5_out_specs(tm):
    rows = pl.BlockSpec((tm, S5_WIDTH), lambda i: (i, 0))
    vec = pl.BlockSpec((1, S5_WIDTH), lambda i: (0, 0))
    mat = pl.BlockSpec((S5_WIDTH, S5_WIDTH), lambda i: (0, 0))
    return rows, vec, mat


def _s5_out(ymm, u, d_skip, w_glu, b_glu, *, name, tm=512):
    n_tok = ymm.shape[0]
    tm = min(tm, n_tok)
    rows, vec, mat = _s5_out_specs(tm)

    def body(y_ref, u_ref, d_ref, w_ref, b_ref, o_ref):
        o_ref[...] = _s5_out_fn(y_ref[...], u_ref[...], d_ref[...], w_ref[...], b_ref[...]).astype(o_ref.dtype)

    return pl.pallas_call(
        body, name=name, grid=(n_tok // tm,), in_specs=[rows, rows, vec, mat, vec], out_specs=rows,
        out_shape=jax.ShapeDtypeStruct((n_tok, S5_WIDTH), MXU_DTYPE),
        compiler_params=_params(4 * _nbytes((tm, S5_WIDTH), F32), 8 * _nbytes((tm, S5_WIDTH), F32)),
    )(*_hbm(ymm, u, d_skip, w_glu, b_glu))


def _s5_out_bwd(ymm, u, d_skip, w_glu, b_glu, dout, *, name, tm=512):
    n_tok = ymm.shape[0]
    tm = min(tm, n_tok)
    rows, vec, mat = _s5_out_specs(tm)

    def body(y_ref, u_ref, d_ref, w_ref, b_ref, do_ref, dy_ref, du_ref, dd_ref, dw_ref, db_ref):
        _, pull = jax.vjp(_s5_out_fn, y_ref[...], u_ref[...], d_ref[...], w_ref[...].astype(F32), b_ref[...])
        dy, du, dd, dw, db = pull(do_ref[...])
        dy_ref[...] = dy.astype(dy_ref.dtype)
        du_ref[...] = du
        first = pl.program_id(0) == 0
        _accumulate(dd_ref, dd, first)
        _accumulate(dw_ref, dw, first)
        _accumulate(db_ref, db, first)

    return pl.pallas_call(
        body, name=name, grid=(n_tok // tm,), in_specs=[rows, rows, vec, mat, vec, rows],
        out_specs=(rows, rows, vec, mat, vec),
        out_shape=(jax.ShapeDtypeStruct(ymm.shape, MXU_DTYPE), jax.ShapeDtypeStruct(ymm.shape, F32),
                   jax.ShapeDtypeStruct((1, S5_WIDTH), F32), jax.ShapeDtypeStruct((S5_WIDTH, S5_WIDTH), F32),
                   jax.ShapeDtypeStruct((1, S5_WIDTH), F32)),
        compiler_params=_params(6 * _nbytes((tm, S5_WIDTH), F32), 12 * _nbytes((tm, S5_WIDTH), F32)),
    )(*_hbm(ymm, u, d_skip, w_glu, b_glu, dout))


def _merge_fn(ga, gb, ya, yb):
    return jax.nn.sigmoid(ga) * ya + jax.nn.sigmoid(gb) * yb


def _merge(gab, ya, yb, *, name, tm=512):
    n_tok = ya.shape[0]
    tm = min(tm, n_tok)
    rows = pl.BlockSpec((tm, D_MODEL), lambda i: (i, 0))

    def body(ga_ref, gb_ref, ya_ref, yb_ref, o_ref):
        o_ref[...] = _merge_fn(ga_ref[...], gb_ref[...], ya_ref[...], yb_ref[...]).astype(o_ref.dtype)

    return pl.pallas_call(
        body, name=name, grid=(n_tok // tm,),
        in_specs=[rows, pl.BlockSpec((tm, D_MODEL), lambda i: (i, 1)), rows, rows], out_specs=rows,
        out_shape=jax.ShapeDtypeStruct(ya.shape, MXU_DTYPE),
        compiler_params=_params(5 * _nbytes((tm, D_MODEL), F32), 4 * _nbytes((tm, D_MODEL), F32)),
    )(*_hbm(gab, gab, ya, yb))


def _merge_bwd(gab, ya, yb, dout, *, name, tm=512):
    n_tok = ya.shape[0]
    tm = min(tm, n_tok)
    rows = pl.BlockSpec((tm, D_MODEL), lambda i: (i, 0))

    def body(ga_ref, gb_ref, ya_ref, yb_ref, do_ref, *out_refs):
        _, pull = jax.vjp(_merge_fn, ga_ref[...], gb_ref[...], ya_ref[...], yb_ref[...])
        for ref, val in zip(out_refs, pull(do_ref[...])):
            ref[...] = val.astype(ref.dtype)

    shape = jax.ShapeDtypeStruct(ya.shape, MXU_DTYPE)
    return pl.pallas_call(
        body, name=name, grid=(n_tok // tm,),
        in_specs=[rows, pl.BlockSpec((tm, D_MODEL), lambda i: (i, 1)), rows, rows, rows],
        out_specs=(rows, rows, rows, rows), out_shape=(shape, shape, shape, shape),
        compiler_params=_params(7 * _nbytes((tm, D_MODEL), F32), 6 * _nbytes((tm, D_MODEL), F32)),
    )(*_hbm(gab, gab, ya, yb, dout))


ADA_SHARD = N_MOD * D_MODEL // N_SHARD


def _ada_fwd(c_pad, w_s, b_s, *, name):
    n_r = c_pad.shape[0]

    def body(c_ref, w_ref, b_ref, o_ref):
        sc = jax.nn.silu(c_ref[...]).astype(MXU_DTYPE)
        o_ref[0] = lax.dot_general(sc, w_ref[0].astype(MXU_DTYPE), _NN, preferred_element_type=F32) + b_ref[0]

    return pl.pallas_call(
        body, name=name, grid=(N_SHARD,),
        in_specs=[pl.BlockSpec((n_r, D_MODEL), lambda s: (0, 0)),
                  pl.BlockSpec((1, D_MODEL, ADA_SHARD), lambda s: (s, 0, 0)),
                  pl.BlockSpec((1, 1, ADA_SHARD), lambda s: (s, 0, 0))],
        out_specs=pl.BlockSpec((1, n_r, ADA_SHARD), lambda s: (s, 0, 0)),
        out_shape=jax.ShapeDtypeStruct((N_SHARD, n_r, ADA_SHARD), F32),
        compiler_params=_params(_nbytes((D_MODEL, ADA_SHARD), w_s.dtype), 1 << 20),
    )(*_hbm(c_pad, w_s, b_s))


def _ada_bwd(c_pad, dmod_s, *, name):
    n_r = c_pad.shape[0]

    def body(c_ref, d_ref, dw_ref, db_ref):
        sc = jax.nn.silu(c_ref[...])
        dm = d_ref[0]
        dw_ref[0] = _fdot(sc, dm, _TN)
        db_ref[0] = jnp.sum(dm, axis=0, keepdims=True)

    return pl.pallas_call(
        body, name=name, grid=(N_SHARD,),
        in_specs=[pl.BlockSpec((n_r, D_MODEL), lambda s: (0, 0)), pl.BlockSpec((1, n_r, ADA_SHARD), lambda s: (s, 0, 0))],
        out_specs=(pl.BlockSpec((1, D_MODEL, ADA_SHARD), lambda s: (s, 0, 0)),
                   pl.BlockSpec((1, 1, ADA_SHARD), lambda s: (s, 0, 0))),
        out_shape=(jax.ShapeDtypeStruct((N_SHARD, D_MODEL, ADA_SHARD), F32),
                   jax.ShapeDtypeStruct((N_SHARD, 1, ADA_SHARD), F32)),
        compiler_params=_params(_nbytes((D_MODEL, ADA_SHARD), F32), 2 * _nbytes((D_MODEL, ADA_SHARD), F32)),
    )(*_hbm(c_pad, dmod_s))


def _heads(t, n_b, n_s):
    return t.reshape(n_b, n_s, DN_HEADS, DN_HEAD_DIM).transpose(0, 2, 1, 3)


def _unheads(t):
    n_b, _, n_s, _ = t.shape
    return t.transpose(0, 2, 1, 3).reshape(n_b, n_s, DN_WIDTH)


def _block_diag(blocks):
    n_per = S5_GROUPS // S5_BLOCKS
    _, n_r, n_c = blocks.shape
    b4 = blocks.reshape(S5_BLOCKS, n_per, n_r, n_c)
    eye = jnp.eye(n_per, dtype=blocks.dtype)
    return (b4[:, :, :, None, :] * eye[None, :, None, :, None]).reshape(S5_BLOCKS, n_per * n_r, n_per * n_c)


def _diag_blocks(mat, n_r, n_c):
    n_per = S5_GROUPS // S5_BLOCKS
    m5 = mat.reshape(S5_BLOCKS, n_per, n_r, n_per, n_c)
    eye = jnp.eye(n_per, dtype=mat.dtype)
    return jnp.sum(m5 * eye[None, :, None, :, None], axis=3).reshape(S5_GROUPS, n_r, n_c)


def _local_step(x, c, target, wts):
    n_b, n_s, _ = x.shape
    n_tok = n_b * n_s
    flat = lambda t: t.reshape(n_tok, t.shape[-1])
    unflat = lambda t: t.reshape(n_b, n_s, t.shape[-1])
    n_chunks = n_s // CHUNK

    c_pad = jnp.zeros((SUBLANES, D_MODEL), F32).at[:n_b].set(c)
    mod_s = _ada_fwd(c_pad, wts["w_ada"], wts["b_ada"], name="ada_fwd")
    mod = mod_s.transpose(1, 0, 2).reshape(SUBLANES, N_MOD * D_MODEL)[:n_b]
    sh1, sc1, gt1, sh2, sc2, gt2, sh3, sc3, gt3 = [m[:, None, :] for m in jnp.split(mod, N_MOD, axis=-1)]

    a1 = _pre(x, None, None, wts["g_ffn1"], sh1, sc1, 0.0, name="pre1")
    f1, ffn1_saved = _ffn_fwd(flat(a1), wts["w1_ffn1"], wts["w3_ffn1"], wts["w2_ffn1"], "ffn1")
    x1, a2 = _pre(x, unflat(f1), gt1, wts["g_mix"], sh2, sc2, 0.5, name="pre2")
    u = flat(a2)[None]
    p_qkv = _mm_act([(u, wts["w_qkv"])], "nn", name="in_qkv")[0]
    p_z = _mm_act([(u, wts["w_z"])], "nn", name="in_z")[0]
    p_gab = _mm_act([(u, wts["w_gab"])], "nn", name="in_gab")[0]
    p_s5 = _mm_act([(u, wts["w_s5"])], "nn", name="in_s5")[0]
    p_ba = _mm_act([(u, wts["w_ba"])], "nn", name="in_ba")[0]

    qkv_c = _conv_fwd(unflat(p_qkv), wts["conv_qkv"], name="conv_fwd")
    z_tok = unflat(p_z)
    ba = p_ba.reshape(n_b, n_s, BA_PAD)
    bl = ba[:, :, :DN_HEADS].transpose(0, 2, 1)[..., None]
    lac = ba[:, :, DN_HEADS:2 * DN_HEADS].transpose(0, 2, 1)[..., None]
    lar = lac.reshape(n_b, DN_HEADS, n_chunks, 1, CHUNK)
    a_log, dt_bias = wts["a_log"], wts["dt_bias"]
    dn_in = (qkv_c, bl, lac, lar, a_log, dt_bias)
    qd, kd, uc, wc, attn, g_last, dn_inv = _dn_prep(*dn_in, name="dn_prep")
    o, states = _dn_scan(qd, kd, uc, wc, attn, g_last, name="dn_scan")
    og = _dn_post(o, z_tok, wts["g_onorm"], name="dn_post")
    og_t = og.reshape(1, n_tok, DN_WIDTH)
    ya = _mm_act([(og_t, wts["w_proj_a"])], "nn", name="proj_a")[0]

    s5p_in = (wts["lam_re"], wts["lam_im"], wts["log_step"], wts["bt_re"], wts["bt_im"], wts["c_im"])
    lb_re, lb_im, bb_re, bb_im, c_neg = _s5_prep(*s5p_in, name="s5_prep")
    wb_re, wb_im = _block_diag(bb_re), _block_diag(bb_im)
    wc_re = _block_diag(wts["c_re"].transpose(0, 2, 1))
    wc_im = _block_diag(c_neg.transpose(0, 2, 1))
    lbr, lbi = lb_re.reshape(1, S5_LANES), lb_im.reshape(1, S5_LANES)
    s5_w = (wb_re, wb_im, lbr, lbi, wc_re, wc_im)
    ymm, x_re, x_im = _s5_mix(unflat(p_s5), *s5_w, name="s5_mix")
    ymm = ymm.reshape(n_tok, S5_WIDTH)
    y2 = _s5_out(ymm, p_s5, wts["d_skip"], wts["w_glu"], wts["b_glu"], name="s5_out")
    yb = _mm_act([(y2[None], wts["w_proj_b"])], "nn", name="proj_b")[0]

    merged = _merge(p_gab, ya, yb, name="merge")
    m_out = _mm_act([(merged[None], wts["w_out"])], "nn", name="mix_out")[0]
    x2, a3 = _pre(x1, unflat(m_out), gt2, wts["g_ffn2"], sh3, sc3, 1.0, name="pre3")
    f3, ffn2_saved = _ffn_fwd(flat(a3), wts["w1_ffn2"], wts["w3_ffn2"], wts["w2_ffn2"], "ffn2")

    g = {}
    loss, dx2_res, df3, dgt3, g["g_final"] = _final(x2, unflat(f3), gt3, wts["g_final"], target, name="final")
    da3, g["w1_ffn2"], g["w3_ffn2"], g["w2_ffn2"] = _ffn_bwd(
        flat(a3), wts["w1_ffn2"], wts["w3_ffn2"], wts["w2_ffn2"], ffn2_saved, flat(df3), "ffn2")
    dx1_res, dm_out, dgt2, g["g_ffn2"], dsh3, dsc3 = _pre_bwd(
        x1, unflat(m_out), gt2, wts["g_ffn2"], sh3, sc3, 1.0, unflat(da3), dx2_res, name="pre3_bwd")
    dm_out = flat(dm_out)[None]
    dmerged = _mm_act([(dm_out, wts["w_out"])], "nt", name="mix_out_bwd")[0]
    g["w_out"] = _mm_tn(merged[None], dm_out, name="dw_out")[0]
    dga, dgb, dya, dyb = _merge_bwd(p_gab, ya, yb, dmerged, name="merge_bwd")

    dy2 = _mm_act([(dyb[None], wts["w_proj_b"])], "nt", name="proj_b_bwd")[0]
    g["w_proj_b"] = _mm_tn(y2[None], dyb[None], name="dw_proj_b")[0]
    dymm, du_skip, g["d_skip"], g["w_glu"], g["b_glu"] = _s5_out_bwd(
        ymm, p_s5, wts["d_skip"], wts["w_glu"], wts["b_glu"], dy2, name="s5_out_bwd")
    dp_s5, dwb_re, dwb_im, dlb_re, dlb_im, dwc_re, dwc_im = _s5_mix_bwd(
        unflat(dymm), unflat(du_skip), unflat(p_s5), x_re, x_im, *s5_w, name="s5_mix_bwd")
    dp_s5 = dp_s5.reshape(n_tok, S5_WIDTH)
    g["c_re"] = _diag_blocks(dwc_re, S5_STATE, S5_GROUP_CH).transpose(0, 2, 1)
    s5_cts = (dlb_re.reshape(lb_re.shape), dlb_im.reshape(lb_im.shape),
              _diag_blocks(dwb_re, S5_GROUP_CH, S5_STATE), _diag_blocks(dwb_im, S5_GROUP_CH, S5_STATE),
              _diag_blocks(dwc_im, S5_STATE, S5_GROUP_CH).transpose(0, 2, 1))
    g["lam_re"], g["lam_im"], g["log_step"], g["bt_re"], g["bt_im"], g["c_im"] = _s5_prep_bwd(
        *s5p_in, s5_cts, name="s5_prep_bwd")

    dog = _mm_act([(dya[None], wts["w_proj_a"])], "nt", name="proj_a_bwd")[0]
    g["w_proj_a"] = _mm_tn(og_t, dya[None], name="dw_proj_a")[0]
    do, dz, g["g_onorm"] = _dn_post_bwd(o, z_tok, wts["g_onorm"], unflat(dog), name="dn_post_bwd")
    scan_cts = _dn_scan_bwd(qd, kd, uc, wc, attn, g_last, states, do, name="dn_scan_bwd")
    dqkv_c, dbl, dlac, dlar, g["a_log"], g["dt_bias"] = _dn_prep_bwd(*dn_in, dn_inv, uc, wc, scan_cts, name="dn_prep_bwd")
    dqkv, g["conv_qkv"] = _conv_bwd(unflat(p_qkv), wts["conv_qkv"], dqkv_c, name="conv_bwd")
    dla = dlac[..., 0] + dlar.reshape(n_b, DN_HEADS, n_s)
    dba = jnp.concatenate([dbl[..., 0].transpose(0, 2, 1), dla.transpose(0, 2, 1),
                           jnp.zeros((n_b, n_s, BA_PAD - 2 * DN_HEADS), F32)], axis=-1).astype(MXU_DTYPE)

    dps = {"w_qkv": flat(dqkv)[None], "w_z": flat(dz)[None], "w_ga": dga[None], "w_gb": dgb[None],
           "w_s5": dp_s5[None], "w_ba": flat(dba)[None]}
    w_ga, w_gb = wts["w_gab"][:, :, :D_MODEL], wts["w_gab"][:, :, D_MODEL:]
    w_of = dict(wts, w_ga=w_ga, w_gb=w_gb)
    du = _mm_act([(dps[k], w_of[k]) for k in dps], "nt", name="in_bwd")[0]
    for k in dps:
        g[k] = _mm_tn(u, dps[k], name=f"d{k}")[0]
    dx0_res, df1, dgt1, g["g_mix"], dsh2, dsc2 = _pre_bwd(
        x, unflat(f1), gt1, wts["g_mix"], sh2, sc2, 0.5, unflat(du), dx1_res, name="pre2_bwd")
    da1, g["w1_ffn1"], g["w3_ffn1"], g["w2_ffn1"] = _ffn_bwd(
        flat(a1), wts["w1_ffn1"], wts["w3_ffn1"], wts["w2_ffn1"], ffn1_saved, flat(df1), "ffn1")
    grad_x, g["g_ffn1"], dsh1, dsc1 = _pre_bwd(
        x, None, None, wts["g_ffn1"], sh1, sc1, 0.0, unflat(da1), dx0_res, name="pre1_bwd")

    dmod = jnp.concatenate([t[:, 0, :] for t in (dsh1, dsc1, dgt1, dsh2, dsc2, dgt2, dsh3, dsc3, dgt3)], axis=-1)
    return loss, grad_x, g, dmod


def _ada_grads(c_rows, dmod_rows):
    n_r = c_rows.shape[0]
    n_pad = -n_r % SUBLANES
    c_pad = jnp.pad(c_rows, ((0, n_pad), (0, 0)))
    dmod_s = jnp.pad(dmod_rows, ((0, n_pad), (0, 0))).reshape(n_r + n_pad, N_SHARD, ADA_SHARD).transpose(1, 0, 2)
    dw, db = _ada_bwd(c_pad, dmod_s, name="ada_bwd")
    return dw, db.reshape(1, N_MOD * D_MODEL)


IN_SPLITS = (("w_qkv", 3 * DN_WIDTH), ("w_z", DN_WIDTH), ("w_ba", 2 * DN_HEADS), ("w_s5", S5_WIDTH),
             ("w_ga", D_MODEL), ("w_gb", D_MODEL))
SHARDED = ("w_ada", "w1_ffn1", "w3_ffn1", "w2_ffn1", "w_in", "conv_qkv", "w_glu", "w_proj_a", "w_proj_b", "w_out",
           "w1_ffn2", "w3_ffn2", "w2_ffn2")
COLUMN_SHARDED = ("w_ada", "w1_ffn1", "w3_ffn1", "w_in", "conv_qkv", "w_proj_a", "w_proj_b", "w1_ffn2", "w3_ffn2")


def _cat_columns(stack):
    return stack.transpose(1, 0, 2).reshape(stack.shape[1], N_SHARD * stack.shape[2])


def _split_columns(full):
    n_r, n_c = full.shape
    return full.reshape(n_r, N_SHARD, n_c // N_SHARD).transpose(1, 0, 2)


def _gathered_weights(st, rep):
    w = {k: st[k] for k in ("w_ada", "w1_ffn1", "w3_ffn1", "w2_ffn1", "w1_ffn2", "w3_ffn2", "w2_ffn2")}
    w["b_ada"] = rep["b_ada"].reshape(N_SHARD, 1, ADA_SHARD)
    for k in ("g_ffn1", "g_mix", "g_ffn2", "g_final"):
        w[k] = rep[k].reshape(1, D_MODEL)
    w_in = _cat_columns(st["w_in"])
    start = 0
    for k, size in IN_SPLITS:
        w[k] = w_in[None, :, start:start + size]
        start += size
    w["w_gab"] = jnp.concatenate([w.pop("w_ga"), w.pop("w_gb")], axis=-1)
    w["w_ba"] = jnp.pad(w["w_ba"], ((0, 0), (0, 0), (0, BA_PAD - 2 * DN_HEADS)))
    w["conv_qkv"] = _cat_columns(st["conv_qkv"])
    w["a_log"] = rep["a_log"].reshape(DN_HEADS, 1, 1)
    w["dt_bias"] = rep["dt_bias"].reshape(DN_HEADS, 1, 1)
    w["g_onorm"] = rep["g_onorm"].reshape(1, DN_HEAD_DIM)
    w["lam_re"] = rep["lam_re"].reshape(S5_GROUPS, 1, S5_STATE)
    w["lam_im"] = rep["lam_im"].reshape(S5_GROUPS, 1, S5_STATE)
    w["log_step"] = rep["log_step"].reshape(S5_GROUPS, 1, 1)
    w["bt_re"] = rep["b_re"][0].transpose(0, 2, 1)
    w["bt_im"] = rep["b_im"][0].transpose(0, 2, 1)
    w["c_re"] = rep["c_re"][0]
    w["c_im"] = rep["c_im"][0]
    w["d_skip"] = rep["d_skip"].reshape(1, S5_WIDTH)
    w["b_glu"] = rep["b_glu"].reshape(1, S5_WIDTH)
    w["w_glu"] = st["w_glu"].reshape(S5_WIDTH, S5_WIDTH)
    w["w_proj_a"] = _cat_columns(st["w_proj_a"])[None]
    w["w_proj_b"] = _cat_columns(st["w_proj_b"])[None]
    w["w_out"] = st["w_out"].reshape(1, D_MODEL, D_MODEL)
    return w


def _grads_to_problem_layout(g):
    st = {k: g[k] for k in ("w1_ffn1", "w3_ffn1", "w2_ffn1", "w1_ffn2", "w3_ffn2", "w2_ffn2")}
    w_in = jnp.concatenate([g[k][:, :size] for k, size in IN_SPLITS], axis=1)
    st["w_in"] = _split_columns(w_in)
    st["w_glu"] = g["w_glu"].reshape(N_SHARD, S5_WIDTH // N_SHARD, S5_WIDTH)
    st["w_proj_a"] = _split_columns(g["w_proj_a"])
    st["w_proj_b"] = _split_columns(g["w_proj_b"])
    st["w_out"] = g["w_out"].reshape(N_SHARD, D_MODEL // N_SHARD, D_MODEL)
    small = {
        "g_ffn1": g["g_ffn1"], "g_mix": g["g_mix"], "g_ffn2": g["g_ffn2"], "g_final": g["g_final"].reshape(D_MODEL),
        "conv_qkv": g["conv_qkv"][None],
        "a_log": g["a_log"].reshape(1, DN_HEADS), "dt_bias": g["dt_bias"].reshape(1, DN_HEADS),
        "g_onorm": g["g_onorm"],
        "lam_re": g["lam_re"].reshape(1, S5_GROUPS, S5_STATE), "lam_im": g["lam_im"].reshape(1, S5_GROUPS, S5_STATE),
        "log_step": g["log_step"].reshape(1, S5_GROUPS),
        "b_re": g["bt_re"].transpose(0, 2, 1)[None], "b_im": g["bt_im"].transpose(0, 2, 1)[None],
        "c_re": g["c_re"][None], "c_im": g["c_im"][None],
        "d_skip": g["d_skip"], "b_glu": g["b_glu"],
    }
    return st, small


ELEMENTWISE_BLOCK_BYTES = 1 << 20


def _row_tile(n_rows, n_cols, n_lead=1, multiple=SUBLANES):
    best = None
    for t in range(multiple, n_rows + 1, multiple):
        if n_rows % t == 0 and n_lead * t * n_cols * 4 <= ELEMENTWISE_BLOCK_BYTES:
            best = t
    return best if best is not None else n_rows


def _add_sibling_half(g4, recv, my_c, *, name):
    n_sh, _, n_h, n_c = g4.shape
    th = _row_tile(n_h, n_c, multiple=2 * SUBLANES)

    def body(c_ref, g_ref, r_ref, o_ref):
        o_ref[0] = (g_ref[0, 0] + r_ref[0]).astype(o_ref.dtype)

    grid_spec = pltpu.PrefetchScalarGridSpec(
        num_scalar_prefetch=1, grid=(n_sh, n_h // th),
        in_specs=[pl.BlockSpec((1, 1, th, n_c), lambda s, i, c_ref: (s, c_ref[0], i, 0)),
                  pl.BlockSpec((1, th, n_c), lambda s, i, c_ref: (s, i, 0))],
        out_specs=pl.BlockSpec((1, th, n_c), lambda s, i, c_ref: (s, i, 0)))
    return pl.pallas_call(
        body, name=name, grid_spec=grid_spec, out_shape=jax.ShapeDtypeStruct((n_sh, n_h, n_c), MXU_DTYPE),
        compiler_params=_params(3 * _nbytes((th, n_c), F32)),
    )(*_hbm(my_c, g4, recv))


def _sum_slots(parts, *, name):
    n_p, n_r, n_c = parts.shape
    th = _row_tile(n_r, n_c, n_p)

    def body(p_ref, o_ref):
        total = p_ref[0].astype(F32)
        for k in range(1, n_p):
            total = total + p_ref[k].astype(F32)
        o_ref[...] = total

    return pl.pallas_call(
        body, name=name, grid=(n_r // th,),
        in_specs=[pl.BlockSpec((n_p, th, n_c), lambda i: (0, i, 0))],
        out_specs=pl.BlockSpec((th, n_c), lambda i: (i, 0)),
        out_shape=jax.ShapeDtypeStruct((n_r, n_c), F32),
        compiler_params=_params((n_p + 1) * _nbytes((th, n_c), F32)),
    )(*_hbm(parts))


def _cast_into_slot(w, place, dtype, *, name):
    n_r, n_c = w.shape
    th = _row_tile(n_r, n_c, multiple=2 * SUBLANES)

    def body(p_ref, w_ref, o_ref):
        o_ref[0] = w_ref[...].astype(o_ref.dtype)

    grid_spec = pltpu.PrefetchScalarGridSpec(
        num_scalar_prefetch=1, grid=(n_r // th,),
        in_specs=[pl.BlockSpec((th, n_c), lambda i, p: (i, 0))],
        out_specs=pl.BlockSpec((1, th, n_c), lambda i, p: (p[1], i, 0)))
    return pl.pallas_call(
        body, name=name, grid_spec=grid_spec, out_shape=jax.ShapeDtypeStruct((N_SHARD, n_r, n_c), dtype),
        compiler_params=_params(2 * _nbytes((th, n_c), F32)),
    )(*_hbm(place, w))


def _sum_chips(own, parts, place, *, name):
    n_sh, n_h, n_c = own.shape
    th = _row_tile(n_h, n_c, n_sh, multiple=2 * SUBLANES)

    def body(p_ref, own_ref, a_ref, b_ref, c_ref, o_ref):
        o_ref[0] = ((own_ref[0].astype(F32) + a_ref[0].astype(F32)) + b_ref[0].astype(F32)) + c_ref[0].astype(F32)

    slab = lambda k: pl.BlockSpec((1, th, n_c), lambda i, p, k=k: (p[k], i, 0))
    grid_spec = pltpu.PrefetchScalarGridSpec(
        num_scalar_prefetch=1, grid=(n_h // th,),
        in_specs=[slab(1), slab(2), slab(3), slab(4)], out_specs=slab(0))
    return pl.pallas_call(
        body, name=name, grid_spec=grid_spec, out_shape=jax.ShapeDtypeStruct((2, n_h, n_c), F32),
        compiler_params=_params(5 * _nbytes((th, n_c), F32)),
    )(*_hbm(place, own, parts, parts, parts))


def _adamw(w, g, m, v, *, name):
    n_r, n_c = w.shape
    th = _row_tile(n_r, n_c)
    bias1 = 1.0 - ADAM_B1 ** ADAM_STEP
    bias2 = 1.0 - ADAM_B2 ** ADAM_STEP

    def body(w_ref, g_ref, m_ref, v_ref, d_ref, mo_ref, vo_ref):
        gv = g_ref[...]
        m_new = ADAM_B1 * m_ref[...] + (1.0 - ADAM_B1) * gv
        v_new = ADAM_B2 * v_ref[...] + (1.0 - ADAM_B2) * jnp.square(gv)
        d_ref[...] = -ADAM_LR * ((m_new / bias1) / (jnp.sqrt(v_new / bias2) + ADAM_EPS) + ADAM_WD * w_ref[...])
        mo_ref[...] = m_new
        vo_ref[...] = v_new

    spec = pl.BlockSpec((th, n_c), lambda i: (i, 0))
    shape = jax.ShapeDtypeStruct((n_r, n_c), F32)
    return pl.pallas_call(
        body, name=name, grid=(n_r // th,), in_specs=[spec] * 4, out_specs=(spec,) * 3, out_shape=(shape,) * 3,
        compiler_params=_params(7 * _nbytes((th, n_c), F32)),
    )(*_hbm(w, g, m, v))


CHIP_FLIPS = ((1, 0), (0, 1), (1, 1))
DEVICE_FLIPS = tuple((fx, fy, fc) for fx in (0, 1) for fy in (0, 1) for fc in (0, 1))[1:]


def _exchange(ins, out_shapes, plan, n_local, n_remote, *, name, aliased=False):
    n_in, n_out = len(ins), len(out_shapes)

    def body(*refs):
        in_refs, out_refs = refs[:n_in], refs[n_in:n_in + n_out]
        send_sems, recv_sems, local_sems = refs[n_in + n_out:]
        me = (lax.axis_index("x"), lax.axis_index("y"), lax.axis_index("c"))
        local, remote = plan(in_refs, out_refs, me)
        assert len(local) == n_local and len(remote) == n_remote
        here = [pltpu.make_async_copy(src, dst, local_sems.at[i]) for i, (src, dst) in enumerate(local)]
        for cp in here:
            cp.start()
        sends = [pltpu.make_async_remote_copy(src_ref=src, dst_ref=dst, send_sem=send_sems.at[i], recv_sem=recv_sems.at[i],
                                              device_id=peer, device_id_type=pl.DeviceIdType.MESH)
                 for i, (src, dst, _, peer) in enumerate(remote)]
        for cp in sends:
            cp.start()
        for i, (src, _, landing, peer) in enumerate(remote):
            pltpu.make_async_remote_copy(src_ref=src, dst_ref=landing, send_sem=send_sems.at[i], recv_sem=recv_sems.at[i],
                                         device_id=peer, device_id_type=pl.DeviceIdType.MESH).wait_recv()
        for cp in sends:
            cp.wait_send()
        for cp in here:
            cp.wait()

    any_spec = pl.BlockSpec(memory_space=pl.ANY)
    return pl.pallas_call(
        body, name=name, in_specs=[any_spec] * n_in, out_specs=tuple([any_spec] * n_out), out_shape=tuple(out_shapes),
        scratch_shapes=[pltpu.SemaphoreType.DMA((n_remote,)), pltpu.SemaphoreType.DMA((n_remote,)),
                        pltpu.SemaphoreType.DMA((max(n_local, 1),))],
        input_output_aliases={k: k for k in range(n_in)} if aliased else {},
    )(*ins)


def _gather_shards(stacks, *, name):
    n = len(stacks)
    halved = [a.shape[1] >= 32 for a in stacks]
    n_ici = len(CHIP_FLIPS) * n
    n_pass = len(CHIP_FLIPS) * sum(halved)

    def body(*refs):
        outs = refs[n:2 * n]
        send_sems, recv_sems = refs[2 * n:]
        x, y, c = lax.axis_index("x"), lax.axis_index("y"), lax.axis_index("c")
        mine = 2 * x + y

        def rows(k, slot, half):
            if not halved[k]:
                return outs[k].at[slot]
            n_h = stacks[k].shape[1] // 2
            return outs[k].at[slot, pl.ds(pl.multiple_of(half * n_h, 16), n_h)]

        def copy(i, src, dst, peer):
            return pltpu.make_async_remote_copy(src_ref=src, dst_ref=dst, send_sem=send_sems.at[i], recv_sem=recv_sems.at[i],
                                                device_id=peer, device_id_type=pl.DeviceIdType.MESH)

        started = []
        for j, (fx, fy) in enumerate(CHIP_FLIPS):
            for k in range(n):
                cp = copy(j * n + k, rows(k, mine, c), rows(k, mine, c), (x ^ fx, y ^ fy, c))
                cp.start()
                started.append(cp)
        i_pass = n_ici
        expect = []
        for j, (fx, fy) in enumerate(CHIP_FLIPS):
            peer_chip = 2 * (x ^ fx) + (y ^ fy)
            for k in range(n):
                landed = rows(k, peer_chip, c)
                copy(j * n + k, landed, landed, (x ^ fx, y ^ fy, c)).wait_recv()
                if halved[k]:
                    cp = copy(i_pass, landed, landed, (x, y, 1 - c))
                    cp.start()
                    started.append(cp)
                    expect.append((i_pass, rows(k, peer_chip, 1 - c)))
                    i_pass += 1
        for i, landing in expect:
            copy(i, landing, landing, (x, y, 1 - c)).wait_recv()
        for cp in started:
            cp.wait_send()

    any_spec = pl.BlockSpec(memory_space=pl.ANY)
    n_sem = n_ici + n_pass
    return pl.pallas_call(
        body, name=name, in_specs=[any_spec] * n, out_specs=tuple([any_spec] * n),
        out_shape=tuple(jax.ShapeDtypeStruct(a.shape, a.dtype) for a in stacks),
        scratch_shapes=[pltpu.SemaphoreType.DMA((n_sem,)), pltpu.SemaphoreType.DMA((n_sem,))],
        input_output_aliases={k: k for k in range(n)},
    )(*stacks)


def _swap_sibling_halves(g4s, *, name):
    n = len(g4s)

    def plan(in_refs, out_refs, me):
        x, y, c = me
        remote = [(in_refs[k].at[:, 1 - c], out_refs[k], out_refs[k], (x, y, 1 - c)) for k in range(n)]
        return [], remote

    shapes = [jax.ShapeDtypeStruct((a.shape[0],) + a.shape[2:], a.dtype) for a in g4s]
    return _exchange(g4s, shapes, plan, 0, n, name=name)


def _scatter_to_chips(hs, *, name):
    n = len(hs)

    def plan(in_refs, out_refs, me):
        x, y, c = me
        mine = 2 * x + y
        remote = []
        for fx, fy in CHIP_FLIPS:
            px, py = x ^ fx, y ^ fy
            peer = 2 * px + py
            for k in range(n):
                remote.append((in_refs[k].at[peer], out_refs[k].at[mine], out_refs[k].at[peer], (px, py, c)))
        return [], remote

    shapes = [jax.ShapeDtypeStruct(a.shape, a.dtype) for a in hs]
    return _exchange(hs, shapes, plan, 0, len(CHIP_FLIPS) * n, name=name)


def _join_sibling_halves(rs, *, name):
    n = len(rs)

    def plan(in_refs, out_refs, me):
        x, y, c = me
        remote = [(out_refs[k].at[c], out_refs[k].at[c], out_refs[k].at[1 - c], (x, y, 1 - c)) for k in range(n)]
        return [], remote

    shapes = [jax.ShapeDtypeStruct(a.shape, a.dtype) for a in rs]
    return _exchange(rs, shapes, plan, 0, n, name=name, aliased=True)


def _gather_all_devices(packed, *, name):
    def plan(in_refs, out_refs, me):
        x, y, c = me
        mine = 4 * x + 2 * y + c
        remote = []
        for fx, fy, fc in DEVICE_FLIPS:
            px, py, pc = x ^ fx, y ^ fy, c ^ fc
            remote.append((in_refs[0], out_refs[0].at[mine], out_refs[0].at[4 * px + 2 * py + pc], (px, py, pc)))
        return [(in_refs[0], out_refs[0].at[mine])], remote

    shape = jax.ShapeDtypeStruct((2 * N_SHARD,) + packed.shape, packed.dtype)
    return _exchange([packed], [shape], plan, 1, len(DEVICE_FLIPS), name=name)[0]


WEIGHT_NAMES = ("w_ada", "b_ada", "g_ffn1", "w1_ffn1", "w3_ffn1", "w2_ffn1", "g_mix", "w_in", "conv_qkv", "a_log",
                "dt_bias", "g_onorm", "lam_re", "lam_im", "log_step", "b_re", "b_im", "c_re", "c_im", "d_skip", "w_glu",
                "b_glu", "w_proj_a", "w_proj_b", "w_out", "g_ffn2", "w1_ffn2", "w3_ffn2", "w2_ffn2", "g_final")
LARGE = tuple(n for n in SHARDED if n != "conv_qkv")
SMALL = tuple(n for n in WEIGHT_NAMES if n not in LARGE)
REDUCED_LARGE = tuple(n for n in LARGE if n != "w_ada")
REDUCED_SMALL = tuple(n for n in SMALL if n != "b_ada")
PACK_ROW = SUBLANES * LANES


def _pack(arrays):
    flat = jnp.concatenate([a.reshape(-1) for a in arrays])
    n_pad = -flat.shape[0] % PACK_ROW
    return jnp.pad(flat, (0, n_pad)).reshape(-1, LANES)


def _unpack(packed, shapes):
    flat = packed.reshape(-1)
    out, start = [], 0
    for s in shapes:
        size = math.prod(s)
        out.append(flat[start:start + size].reshape(s))
        start += size
    return out


def _unpack_slots(gathered, shapes):
    flat = gathered.reshape(gathered.shape[0], -1)
    out, start = [], 0
    for s in shapes:
        size = math.prod(s)
        out.append(flat[:, start:start + size].reshape((gathered.shape[0],) + tuple(s)))
        start += size
    return out


TRANSPOSED = ("w1_ffn1", "w3_ffn1", "w1_ffn2", "w3_ffn2")


def _to_internal(name, a):
    return jnp.swapaxes(a[0], 0, 1) if name in TRANSPOSED else a[0]


def _from_internal(name, a):
    return (jnp.swapaxes(a, 0, 1) if name in TRANSPOSED else a)[None]


def _step(x, c, target, weights, m_in, v_in):
    xi, yi, ci = lax.axis_index("x"), lax.axis_index("y"), lax.axis_index("c")
    my_chip = 2 * xi + yi

    others = [k + (k >= my_chip).astype(jnp.int32) for k in range(N_SHARD - 1)]
    place = jnp.stack([ci, my_chip] + others).astype(jnp.int32)

    slots = [_cast_into_slot(_to_internal(n, weights[n]), place, F32 if n == "conv_qkv" else MXU_DTYPE, name=f"cast_{n}")
             for n in SHARDED]
    stacks = dict(zip(SHARDED, _gather_shards(slots, name="gather_weights")))
    rep = {n: weights[n] for n in WEIGHT_NAMES if n not in SHARDED}
    loss, grad_x, g, dmod = _local_step(x, c, target, _gathered_weights(stacks, rep))
    g_stacks, g_small = _grads_to_problem_layout(g)

    g4s = [g_stacks[n].reshape(N_SHARD, 2, g_stacks[n].shape[1] // 2, g_stacks[n].shape[2]) for n in REDUCED_LARGE]
    from_sibling = _swap_sibling_halves(g4s, name="swap_sibling_halves")
    chip_sums = [_add_sibling_half(a, r, place, name=f"chip_sum_{n}") for n, a, r in zip(REDUCED_LARGE, g4s, from_sibling)]
    from_chips = _scatter_to_chips(chip_sums, name="scatter_to_chips")
    reduced = [_sum_chips(h, p, place, name=f"sum_chips_{n}") for n, h, p in zip(REDUCED_LARGE, chip_sums, from_chips)]
    joined = _join_sibling_halves(reduced, name="join_sibling_halves")
    grads_2d = {n: j.reshape(2 * j.shape[1], j.shape[2]) for n, j in zip(REDUCED_LARGE, joined)}
    grads = {n: _from_internal(n, a) for n, a in grads_2d.items()}

    summed_shapes = [g_small[n].shape for n in REDUCED_SMALL] + [(1, 1)]
    packed = _pack([g_small[n] for n in REDUCED_SMALL] + [loss, c, dmod])
    gathered = _gather_all_devices(packed, name="gather_small")
    *small_grads, loss_sum = _unpack(_sum_slots(gathered, name="sum_small"), summed_shapes)
    grads.update(zip(REDUCED_SMALL, small_grads))
    n_conv = weights["conv_qkv"].shape[-1]
    grads["conv_qkv"] = lax.dynamic_slice_in_dim(grads["conv_qkv"], my_chip * n_conv, n_conv, axis=2)
    n_dev = gathered.shape[0]
    rows_of = lambda t: t.reshape(n_dev * t.shape[1], t.shape[2])
    _, c_all, dmod_all = _unpack_slots(gathered, [(sum(math.prod(s) for s in summed_shapes),), c.shape, dmod.shape])
    dw_ada, grads["b_ada"] = _ada_grads(rows_of(c_all), rows_of(dmod_all))
    grads_2d["w_ada"] = lax.dynamic_index_in_dim(dw_ada, my_chip, axis=0, keepdims=False)
    grads["w_ada"] = grads_2d["w_ada"][None]

    delta, new_m, new_v = {}, {}, {}
    grads_2d["conv_qkv"] = grads["conv_qkv"][0]
    for n in LARGE + ("conv_qkv",):
        outs = _adamw(_to_internal(n, weights[n]), grads_2d[n], _to_internal(n, m_in[n]), _to_internal(n, v_in[n]),
                      name=f"adamw_{n}")
        delta[n], new_m[n], new_v[n] = [_from_internal(n, o) for o in outs]
    packed_names = tuple(n for n in SMALL if n != "conv_qkv")
    shapes = [weights[n].shape for n in packed_names]
    outs = _adamw(*[_pack([d[n] for n in packed_names]) for d in (weights, grads, m_in, v_in)], name="adamw_small")
    for d, o in zip((delta, new_m, new_v), outs):
        d.update(zip(packed_names, _unpack(o, shapes)))
    return (loss_sum.reshape(()), grad_x, *[grads[n] for n in WEIGHT_NAMES], *[delta[n] for n in WEIGHT_NAMES],
            *[new_m[n] for n in WEIGHT_NAMES], *[new_v[n] for n in WEIGHT_NAMES])


def kernel(x, c, w_ada, b_ada, g_ffn1, w1_ffn1, w3_ffn1, w2_ffn1, g_mix, w_in, conv_qkv, a_log, dt_bias, g_onorm, lam_re, lam_im, log_step, b_re, b_im, c_re, c_im, d_skip, w_glu, b_glu, w_proj_a, w_proj_b, w_out, g_ffn2, w1_ffn2, w3_ffn2, w2_ffn2, g_final, loss_target, m_w_ada, m_b_ada, m_g_ffn1, m_w1_ffn1, m_w3_ffn1, m_w2_ffn1, m_g_mix, m_w_in, m_conv_qkv, m_a_log, m_dt_bias, m_g_onorm, m_lam_re, m_lam_im, m_log_step, m_b_re, m_b_im, m_c_re, m_c_im, m_d_skip, m_w_glu, m_b_glu, m_w_proj_a, m_w_proj_b, m_w_out, m_g_ffn2, m_w1_ffn2, m_w3_ffn2, m_w2_ffn2, m_g_final, v_w_ada, v_b_ada, v_g_ffn1, v_w1_ffn1, v_w3_ffn1, v_w2_ffn1, v_g_mix, v_w_in, v_conv_qkv, v_a_log, v_dt_bias, v_g_onorm, v_lam_re, v_lam_im, v_log_step, v_b_re, v_b_im, v_c_re, v_c_im, v_d_skip, v_w_glu, v_b_glu, v_w_proj_a, v_w_proj_b, v_w_out, v_g_ffn2, v_w1_ffn2, v_w3_ffn2, v_w2_ffn2, v_g_final):
    w_vals = (w_ada, b_ada, g_ffn1, w1_ffn1, w3_ffn1, w2_ffn1, g_mix, w_in, conv_qkv, a_log, dt_bias, g_onorm, lam_re, lam_im, log_step, b_re, b_im, c_re, c_im, d_skip, w_glu, b_glu, w_proj_a, w_proj_b, w_out, g_ffn2, w1_ffn2, w3_ffn2, w2_ffn2, g_final)
    m_vals = (m_w_ada, m_b_ada, m_g_ffn1, m_w1_ffn1, m_w3_ffn1, m_w2_ffn1, m_g_mix, m_w_in, m_conv_qkv, m_a_log, m_dt_bias, m_g_onorm, m_lam_re, m_lam_im, m_log_step, m_b_re, m_b_im, m_c_re, m_c_im, m_d_skip, m_w_glu, m_b_glu, m_w_proj_a, m_w_proj_b, m_w_out, m_g_ffn2, m_w1_ffn2, m_w3_ffn2, m_w2_ffn2, m_g_final)
    v_vals = (v_w_ada, v_b_ada, v_g_ffn1, v_w1_ffn1, v_w3_ffn1, v_w2_ffn1, v_g_mix, v_w_in, v_conv_qkv, v_a_log, v_dt_bias, v_g_onorm, v_lam_re, v_lam_im, v_log_step, v_b_re, v_b_im, v_c_re, v_c_im, v_d_skip, v_w_glu, v_b_glu, v_w_proj_a, v_w_proj_b, v_w_out, v_g_ffn2, v_w1_ffn2, v_w3_ffn2, v_w2_ffn2, v_g_final)
    return _step(x, c, loss_target, dict(zip(WEIGHT_NAMES, w_vals)), dict(zip(WEIGHT_NAMES, m_vals)),
                 dict(zip(WEIGHT_NAMES, v_vals)))
```

```python
import functools
import math

import jax
import jax.numpy as jnp
from jax import lax
from jax.experimental import pallas as pl
from jax.experimental.pallas import tpu as pltpu

F32 = jnp.float32
BF16 = jnp.bfloat16
MXU_DTYPE = BF16

D_MODEL = 1024
D_FF = 2816
DN_HEADS = 8
DN_HEAD_DIM = 64
DN_WIDTH = DN_HEADS * DN_HEAD_DIM
CONV_WIDTH = 4
CHUNK = 64
S5_GROUP_CH = 16
S5_GROUPS = 32
S5_WIDTH = S5_GROUPS * S5_GROUP_CH
S5_STATE = 64
S5_LANES = S5_GROUPS * S5_STATE
N_MOD = 9
EPS = 1e-6
N_SHARD = 4
FF_SHARD = D_FF // N_SHARD
BA_PAD = 128

ADAM_LR = 0.001
ADAM_B1 = 0.9
ADAM_B2 = 0.999
ADAM_EPS = 1e-08
ADAM_WD = 0.01
ADAM_STEP = 10

VMEM_BYTES_V7X = 64 * 1024 * 1024
SUBLANES = 8
LANES = 128


def _params(block_bytes, extra_bytes=0):
    need = 2 * block_bytes + extra_bytes + (4 << 20)
    return pltpu.CompilerParams(vmem_limit_bytes=int(min(max(need, 16 << 20), VMEM_BYTES_V7X - (8 << 20))))


def _nbytes(shape, dtype):
    return math.prod(shape) * jnp.dtype(dtype).itemsize


HBM_OPERAND_BYTES = 1 << 20


def _hbm(*args):
    return [pltpu.with_memory_space_constraint(a, pltpu.HBM) if _nbytes(a.shape, a.dtype) >= HBM_OPERAND_BYTES else a
            for a in args]


_NN = (((1,), (0,)), ((), ()))
_NT = (((1,), (1,)), ((), ()))
_TN = (((0,), (0,)), ((), ()))


LHS_ROW_BYTES = 4096


def _mm_act(pairs, mode, *, name, out_sharded=False, reduce_shards=False, out_dtype=F32, add=None, tm=None):
    n_tok = pairs[0][0].shape[1]
    n_out = pairs[0][1].shape[2] if mode == "nn" else pairs[0][1].shape[1]
    if tm is None:
        row_bytes = sum(a.shape[2] * jnp.dtype(a.dtype).itemsize for a, _ in pairs)
        tm = 1024 if row_bytes <= LHS_ROW_BYTES else 512
    tm = min(tm, n_tok)
    tn = n_out if n_out <= 1536 else 1024
    assert n_tok % tm == 0 and n_out % tn == 0
    n_so = N_SHARD if out_sharded else 1
    n_red = N_SHARD if reduce_shards else 1
    grid = (n_so, n_tok // tm, n_out // tn, n_red)
    dims = _NN if mode == "nn" else _NT

    def shard_of(n_sh):
        if n_sh == 1:
            return lambda s, r: 0
        return (lambda s, r: s) if out_sharded else (lambda s, r: r)

    in_specs, args, blk = [], [], 0
    for a, b in pairs:
        k_dim = a.shape[2]
        sa, sb = shard_of(a.shape[0]), shard_of(b.shape[0])
        in_specs.append(pl.BlockSpec((1, tm, k_dim), lambda s, i, j, r, sa=sa: (sa(s, r), i, 0)))
        if mode == "nn":
            assert b.shape[1] == k_dim
            in_specs.append(pl.BlockSpec((1, k_dim, tn), lambda s, i, j, r, sb=sb: (sb(s, r), 0, j)))
        else:
            assert b.shape[2] == k_dim
            in_specs.append(pl.BlockSpec((1, tn, k_dim), lambda s, i, j, r, sb=sb: (sb(s, r), j, 0)))
        args += [a, b]
        blk += _nbytes((tm, k_dim), a.dtype) + _nbytes((k_dim, tn), b.dtype)
    if add is not None:
        in_specs.append(pl.BlockSpec((1, tm, tn), lambda s, i, j, r: (s, i, j)))
        args.append(add)
        blk += _nbytes((tm, tn), F32)
    blk += _nbytes((tm, tn), out_dtype)
    n_pairs = len(pairs)

    def body(*refs):
        out_ref = refs[2 * n_pairs + (add is not None)]
        acc = None
        for k in range(n_pairs):
            a = refs[2 * k][0].astype(MXU_DTYPE)
            b = refs[2 * k + 1][0].astype(MXU_DTYPE)
            d = lax.dot_general(a, b, dims, preferred_element_type=F32)
            acc = d if acc is None else acc + d

        def finish(total):
            if add is not None:
                total = total + refs[2 * n_pairs][0]
            out_ref[0] = total.astype(out_dtype)

        if n_red == 1:
            finish(acc)
        else:
            acc_ref = refs[-1]
            r = pl.program_id(3)

            @pl.when(r == 0)
            def _():
                acc_ref[...] = acc

            @pl.when(r > 0)
            def _():
                acc_ref[...] += acc

            @pl.when(r == n_red - 1)
            def _():
                finish(acc_ref[...])

    return pl.pallas_call(
        body,
        name=name,
        grid=grid,
        in_specs=in_specs,
        out_specs=pl.BlockSpec((1, tm, tn), lambda s, i, j, r: (s, i, j)),
        out_shape=jax.ShapeDtypeStruct((n_so, n_tok, n_out), out_dtype),
        scratch_shapes=[pltpu.VMEM((tm, tn), F32)] if n_red > 1 else [],
        compiler_params=_params(blk, 3 * _nbytes((tm, tn), F32)),
    )(*_hbm(*args))


def _mm_tn(a, b, *, name, tt=1024):
    n_tok, k_dim = a.shape[1], a.shape[2]
    n_out = b.shape[2]
    tt = min(tt, n_tok)
    tk = k_dim if k_dim <= 1536 else 1024
    tn = n_out if n_out <= 1536 else 1024
    assert n_tok % tt == 0 and k_dim % tk == 0 and n_out % tn == 0
    n_so = max(a.shape[0], b.shape[0])
    sa = (lambda s: s) if a.shape[0] > 1 else (lambda s: 0)
    sb = (lambda s: s) if b.shape[0] > 1 else (lambda s: 0)
    grid = (n_so, k_dim // tk, n_out // tn, n_tok // tt)

    def body(a_ref, b_ref, out_ref):
        d = lax.dot_general(a_ref[0].astype(MXU_DTYPE), b_ref[0].astype(MXU_DTYPE), _TN, preferred_element_type=F32)
        t = pl.program_id(3)

        @pl.when(t == 0)
        def _():
            out_ref[0] = d

        @pl.when(t > 0)
        def _():
            out_ref[0] += d

    blk = _nbytes((tt, tk), a.dtype) + _nbytes((tt, tn), b.dtype) + _nbytes((tk, tn), F32)
    return pl.pallas_call(
        body,
        name=name,
        grid=grid,
        in_specs=[
            pl.BlockSpec((1, tt, tk), lambda s, ki, nj, t: (sa(s), t, ki)),
            pl.BlockSpec((1, tt, tn), lambda s, ki, nj, t: (sb(s), t, nj)),
        ],
        out_specs=pl.BlockSpec((1, tk, tn), lambda s, ki, nj, t: (s, ki, nj)),
        out_shape=jax.ShapeDtypeStruct((n_so, k_dim, n_out), F32),
        compiler_params=_params(blk, 2 * _nbytes((tk, tn), F32) + _nbytes((tt, tk), F32)),
    )(*_hbm(a, b))


@functools.partial(jax.custom_vjp, nondiff_argnums=(2,))
def _mdot(a, b, dims):
    return lax.dot_general(a.astype(MXU_DTYPE), b.astype(MXU_DTYPE), dims, preferred_element_type=F32)


def _mdot_fwd(a, b, dims):
    return _mdot(a, b, dims), (a, b)


def _mdot_bwd(dims, res, g):
    a, b = res
    (ca, cb), (ba, bb) = dims
    nb = len(ba)
    assert tuple(ba) == tuple(range(nb)) and tuple(bb) == tuple(range(nb)) and len(ca) == 1 and a.ndim == nb + 2
    batch = (tuple(range(nb)), tuple(range(nb)))
    ra, rb = nb, nb + 1
    a_free = (set(range(nb, nb + 2)) - set(ca)).pop()
    b_free = (set(range(nb, nb + 2)) - set(cb)).pop()
    if a_free < ca[0]:
        da = _mdot(g, b, (((rb,), (b_free,)), batch))
    else:
        da = _mdot(b, g, (((b_free,), (rb,)), batch))
    if b_free > cb[0]:
        db = _mdot(a, g, (((a_free,), (ra,)), batch))
    else:
        db = _mdot(g, a, (((ra,), (a_free,)), batch))
    return da.astype(a.dtype), db.astype(b.dtype)


_mdot.defvjp(_mdot_fwd, _mdot_bwd)


def _rms(x, gain):
    return x * lax.rsqrt(jnp.mean(x * x, axis=-1, keepdims=True) + EPS) * gain


def _pre_fn(coef, x_in, f, gate, gain, shift, scale):
    x_new = x_in if f is None else x_in + coef * gate * f
    return x_new, _rms(x_new, gain) * (1.0 + scale) + shift


def _row_spec(ts):
    return pl.BlockSpec((1, ts, D_MODEL), lambda b, j: (b, j, 0))


_BATCH_VEC = pl.BlockSpec((1, 1, D_MODEL), lambda b, j: (b, 0, 0))
_ONE_VEC = pl.BlockSpec((1, D_MODEL), lambda b, j: (0, 0))


def _pre(x_in, f, gate, gain, shift, scale, coef, *, name, ts=512):
    n_b, n_s, _ = x_in.shape
    ts = min(ts, n_s)
    has_res = f is not None

    def body(*refs):
        if has_res:
            x_ref, f_ref, gate_ref, gain_ref, sh_ref, sc_ref, xn_ref, a_ref = refs
            x_new, a = _pre_fn(coef, x_ref[0], f_ref[0], gate_ref[0], gain_ref[...], sh_ref[0], sc_ref[0])
            xn_ref[0] = x_new
        else:
            x_ref, gain_ref, sh_ref, sc_ref, a_ref = refs
            _, a = _pre_fn(coef, x_ref[0], None, None, gain_ref[...], sh_ref[0], sc_ref[0])
        a_ref[0] = a.astype(a_ref.dtype)

    row = _row_spec(ts)
    if has_res:
        args = (x_in, f, gate, gain, shift, scale)
        in_specs = [row, row, _BATCH_VEC, _ONE_VEC, _BATCH_VEC, _BATCH_VEC]
        out_specs = (row, row)
        out_shape = (jax.ShapeDtypeStruct(x_in.shape, F32), jax.ShapeDtypeStruct(x_in.shape, MXU_DTYPE))
    else:
        args = (x_in, gain, shift, scale)
        in_specs = [row, _ONE_VEC, _BATCH_VEC, _BATCH_VEC]
        out_specs = row
        out_shape = jax.ShapeDtypeStruct(x_in.shape, MXU_DTYPE)
    return pl.pallas_call(
        body, name=name, grid=(n_b, n_s // ts), in_specs=in_specs, out_specs=out_specs, out_shape=out_shape,
        compiler_params=_params(5 * _nbytes((ts, D_MODEL), F32), 4 * _nbytes((ts, D_MODEL), F32)),
    )(*_hbm(*args))


def _accumulate(ref, value, first):
    @pl.when(first)
    def _():
        ref[...] = value

    @pl.when(jnp.logical_not(first))
    def _():
        ref[...] += value


def _pre_bwd(x_in, f, gate, gain, shift, scale, coef, da, dx_up, *, name, ts=512):
    n_b, n_s, _ = x_in.shape
    ts = min(ts, n_s)
    has_res = f is not None
    has_up = dx_up is not None

    def body(*refs):
        refs = list(refs)
        x_ref = refs.pop(0)
        f_ref, gate_ref = (refs.pop(0), refs.pop(0)) if has_res else (None, None)
        gain_ref, sh_ref, sc_ref, da_ref = refs.pop(0), refs.pop(0), refs.pop(0), refs.pop(0)
        up_ref = refs.pop(0) if has_up else None
        dx_ref = refs.pop(0)
        df_ref, dgate_ref = (refs.pop(0), refs.pop(0)) if has_res else (None, None)
        dgain_ref, dsh_ref, dsc_ref = refs
        b, j = pl.program_id(0), pl.program_id(1)
        da_v = da_ref[0].astype(F32)
        up_v = up_ref[0] if has_up else jnp.zeros((ts, D_MODEL), F32)
        if has_res:
            fn = functools.partial(_pre_fn, coef)
            _, pull = jax.vjp(fn, x_ref[0], f_ref[0], gate_ref[0], gain_ref[...], sh_ref[0], sc_ref[0])
            dx, df, dgate, dgain, dsh, dsc = pull((up_v, da_v))
            df_ref[0] = df.astype(df_ref.dtype)
            _accumulate(dgate_ref, dgate[None], j == 0)
        else:
            fn = lambda x, g, sh, sc: _pre_fn(coef, x, None, None, g, sh, sc)
            _, pull = jax.vjp(fn, x_ref[0], gain_ref[...], sh_ref[0], sc_ref[0])
            dx, dgain, dsh, dsc = pull((up_v, da_v))
        dx_ref[0] = dx
        _accumulate(dgain_ref, dgain, jnp.logical_and(b == 0, j == 0))
        _accumulate(dsh_ref, dsh[None], j == 0)
        _accumulate(dsc_ref, dsc[None], j == 0)

    row = _row_spec(ts)
    args, in_specs = [x_in], [row]
    if has_res:
        args += [f, gate]
        in_specs += [row, _BATCH_VEC]
    args += [gain, shift, scale, da]
    in_specs += [_ONE_VEC, _BATCH_VEC, _BATCH_VEC, row]
    if has_up:
        args.append(dx_up)
        in_specs.append(row)
    vec = jax.ShapeDtypeStruct((n_b, 1, D_MODEL), F32)
    out_shape, out_specs = [jax.ShapeDtypeStruct(x_in.shape, F32)], [row]
    if has_res:
        out_shape += [jax.ShapeDtypeStruct(x_in.shape, MXU_DTYPE), vec]
        out_specs += [row, _BATCH_VEC]
    out_shape += [jax.ShapeDtypeStruct((1, D_MODEL), F32), vec, vec]
    out_specs += [_ONE_VEC, _BATCH_VEC, _BATCH_VEC]
    return pl.pallas_call(
        body, name=name, grid=(n_b, n_s // ts), in_specs=in_specs, out_specs=tuple(out_specs), out_shape=tuple(out_shape),
        compiler_params=_params(6 * _nbytes((ts, D_MODEL), F32), 8 * _nbytes((ts, D_MODEL), F32)),
    )(*_hbm(*args))


def _final_fn(x_in, f, gate, gain, target):
    x_new = x_in + 0.5 * gate * f
    err = jnp.square(_rms(x_new, gain) - target)
    return 0.5 * jnp.sum(jnp.mean(err, axis=-1))


def _final(x_in, f, gate, gain, target, *, name, ts=512):
    n_b, n_s, _ = x_in.shape
    ts = min(ts, n_s)

    def body(x_ref, f_ref, gate_ref, gain_ref, t_ref, loss_ref, dx_ref, df_ref, dgate_ref, dgain_ref):
        b, j = pl.program_id(0), pl.program_id(1)
        loss, (dx, df, dgate, dgain) = jax.value_and_grad(_final_fn, argnums=(0, 1, 2, 3))(
            x_ref[0], f_ref[0], gate_ref[0], gain_ref[...], t_ref[0])
        first = jnp.logical_and(b == 0, j == 0)
        _accumulate(loss_ref, jnp.reshape(loss, (1, 1)), first)
        dx_ref[0] = dx
        df_ref[0] = df.astype(df_ref.dtype)
        _accumulate(dgate_ref, dgate[None], j == 0)
        _accumulate(dgain_ref, dgain, first)

    row = _row_spec(ts)
    return pl.pallas_call(
        body, name=name, grid=(n_b, n_s // ts),
        in_specs=[row, row, _BATCH_VEC, _ONE_VEC, row],
        out_specs=(pl.BlockSpec((1, 1), lambda b, j: (0, 0)), row, row, _BATCH_VEC, _ONE_VEC),
        out_shape=(jax.ShapeDtypeStruct((1, 1), F32), jax.ShapeDtypeStruct(x_in.shape, F32),
                   jax.ShapeDtypeStruct(x_in.shape, MXU_DTYPE), jax.ShapeDtypeStruct((n_b, 1, D_MODEL), F32),
                   jax.ShapeDtypeStruct((1, D_MODEL), F32)),
        compiler_params=_params(5 * _nbytes((ts, D_MODEL), F32), 8 * _nbytes((ts, D_MODEL), F32)),
    )(*_hbm(x_in, f, gate, gain, target))


FFN_TOKENS = 1024


def _ffn_up(a, w1s, w3s, *, name, tm=FFN_TOKENS):
    n_tok = a.shape[0]
    tm = min(tm, n_tok)

    def body(a_ref, w1_ref, w3_ref, h1_ref, h3_ref, g_ref):
        av = a_ref[...].astype(MXU_DTYPE)
        h1 = lax.dot_general(av, w1_ref[0].astype(MXU_DTYPE), _NT, preferred_element_type=F32)
        h3 = lax.dot_general(av, w3_ref[0].astype(MXU_DTYPE), _NT, preferred_element_type=F32)
        h1_ref[0] = h1.astype(h1_ref.dtype)
        h3_ref[0] = h3.astype(h3_ref.dtype)
        g_ref[0] = (jax.nn.silu(h1) * h3).astype(g_ref.dtype)

    w_spec = pl.BlockSpec((1, FF_SHARD, D_MODEL), lambda s, i: (s, 0, 0))
    h_spec = pl.BlockSpec((1, tm, FF_SHARD), lambda s, i: (s, i, 0))
    h_shape = jax.ShapeDtypeStruct((N_SHARD, n_tok, FF_SHARD), MXU_DTYPE)
    blk = _nbytes((tm, D_MODEL), a.dtype) + 2 * _nbytes((D_MODEL, FF_SHARD), w1s.dtype) + 3 * _nbytes((tm, FF_SHARD), MXU_DTYPE)
    return pl.pallas_call(
        body, name=name, grid=(N_SHARD, n_tok // tm),
        in_specs=[pl.BlockSpec((tm, D_MODEL), lambda s, i: (i, 0)), w_spec, w_spec],
        out_specs=(h_spec, h_spec, h_spec), out_shape=(h_shape, h_shape, h_shape),
        compiler_params=_params(blk, 6 * _nbytes((tm, FF_SHARD), F32)),
    )(*_hbm(a, w1s, w3s))


def _ffn_down_bwd(df, w2s, h1, h3, *, name, tm=FFN_TOKENS):
    n_tok = df.shape[0]
    tm = min(tm, n_tok)

    def body(df_ref, w2_ref, h1_ref, h3_ref, dh1_ref, dh3_ref):
        dg = lax.dot_general(df_ref[...].astype(MXU_DTYPE), w2_ref[0].astype(MXU_DTYPE), _NT, preferred_element_type=F32)
        h1v = h1_ref[0].astype(F32)
        h3v = h3_ref[0].astype(F32)
        sig = jax.nn.sigmoid(h1v)
        dh3_ref[0] = (dg * (h1v * sig)).astype(dh3_ref.dtype)
        dh1_ref[0] = (dg * h3v * (sig * (1.0 + h1v * (1.0 - sig)))).astype(dh1_ref.dtype)

    h_spec = pl.BlockSpec((1, tm, FF_SHARD), lambda s, i: (s, i, 0))
    h_shape = jax.ShapeDtypeStruct((N_SHARD, n_tok, FF_SHARD), MXU_DTYPE)
    blk = _nbytes((tm, D_MODEL), df.dtype) + _nbytes((FF_SHARD, D_MODEL), w2s.dtype) + 4 * _nbytes((tm, FF_SHARD), MXU_DTYPE)
    return pl.pallas_call(
        body, name=name, grid=(N_SHARD, n_tok // tm),
        in_specs=[pl.BlockSpec((tm, D_MODEL), lambda s, i: (i, 0)),
                  pl.BlockSpec((1, FF_SHARD, D_MODEL), lambda s, i: (s, 0, 0)), h_spec, h_spec],
        out_specs=(h_spec, h_spec), out_shape=(h_shape, h_shape),
        compiler_params=_params(blk, 8 * _nbytes((tm, FF_SHARD), F32)),
    )(*_hbm(df, w2s, h1, h3))


def _ffn_fwd(a, w1s, w3s, w2s, tag):
    h1, h3, g = _ffn_up(a, w1s, w3s, name=f"{tag}_up")
    f = _mm_act([(g, w2s)], "nn", reduce_shards=True, tm=FFN_TOKENS, name=f"{tag}_down")[0]
    return f, (h1, h3, g)


def _ffn_bwd(a, w1s, w3s, w2s, saved, df, tag):
    h1, h3, g = saved
    dh1, dh3 = _ffn_down_bwd(df, w2s, h1, h3, name=f"{tag}_down_bwd")
    da = _mm_act([(dh1, w1s), (dh3, w3s)], "nn", reduce_shards=True, tm=FFN_TOKENS, name=f"{tag}_up_bwd")[0]
    a3 = a[None]
    dw1 = _mm_tn(dh1, a3, tt=FFN_TOKENS, name=f"{tag}_dw1")
    dw3 = _mm_tn(dh3, a3, tt=FFN_TOKENS, name=f"{tag}_dw3")
    dw2 = _mm_tn(g, df[None], tt=FFN_TOKENS, name=f"{tag}_dw2")
    return da, dw1, dw3, dw2


CONV_LANES = 256


def _shift_down(x, d):
    if d == 0:
        return x
    row = lax.broadcasted_iota(jnp.int32, x.shape, 0)
    return jnp.where(row >= d, pltpu.roll(x, d, 0), 0.0)


def _shift_up(x, d):
    if d == 0:
        return x
    n = x.shape[0]
    row = lax.broadcasted_iota(jnp.int32, x.shape, 0)
    return jnp.where(row < n - d, pltpu.roll(x, n - d, 0), 0.0)


def _conv_pre(x, w):
    acc = None
    for j in range(CONV_WIDTH):
        term = w[j:j + 1, :] * _shift_down(x, CONV_WIDTH - 1 - j)
        acc = term if acc is None else acc + term
    return acc


def _conv_fwd(x, w, *, name):
    n_b, n_s, n_c = x.shape
    spec = pl.BlockSpec((1, n_s, CONV_LANES), lambda b, cj: (b, 0, cj))

    def body(x_ref, w_ref, o_ref):
        o_ref[0] = jax.nn.silu(_conv_pre(x_ref[0], w_ref[...]))

    return pl.pallas_call(
        body, name=name, grid=(n_b, n_c // CONV_LANES),
        in_specs=[spec, pl.BlockSpec((CONV_WIDTH, CONV_LANES), lambda b, cj: (0, cj))],
        out_specs=spec, out_shape=jax.ShapeDtypeStruct(x.shape, F32),
        compiler_params=_params(2 * _nbytes((n_s, CONV_LANES), F32), 6 * _nbytes((n_s, CONV_LANES), F32)),
    )(*_hbm(x, w))


def _conv_bwd(x, w, dout, *, name):
    n_b, n_s, n_c = x.shape
    per_part = DN_WIDTH // CONV_LANES
    spec = pl.BlockSpec((1, n_s, CONV_LANES), lambda cj, b: (b, 0, cj))
    do_spec = pl.BlockSpec((1, 1, n_s, CONV_LANES), lambda cj, b: (cj // per_part, b, 0, cj % per_part))
    w_spec = pl.BlockSpec((CONV_WIDTH, CONV_LANES), lambda cj, b: (0, cj))

    def body(x_ref, w_ref, do_ref, dx_ref, dw_ref):
        xv, wv = x_ref[0], w_ref[...]
        pre = _conv_pre(xv, wv)
        sig = jax.nn.sigmoid(pre)
        dpre = do_ref[0, 0] * (sig * (1.0 + pre * (1.0 - sig)))
        dx = None
        first = pl.program_id(1) == 0
        for j in range(CONV_WIDTH):
            d = CONV_WIDTH - 1 - j
            ahead = _shift_up(dpre, d)
            term = wv[j:j + 1, :] * ahead
            dx = term if dx is None else dx + term
            dwj = jnp.sum(ahead * xv, axis=0, keepdims=True)
            _accumulate(dw_ref.at[j:j + 1, :], dwj, first)
        dx_ref[0] = dx.astype(dx_ref.dtype)

    return pl.pallas_call(
        body, name=name, grid=(n_c // CONV_LANES, n_b),
        in_specs=[spec, w_spec, do_spec], out_specs=(spec, w_spec),
        out_shape=(jax.ShapeDtypeStruct(x.shape, MXU_DTYPE), jax.ShapeDtypeStruct((CONV_WIDTH, n_c), F32)),
        compiler_params=_params(3 * _nbytes((n_s, CONV_LANES), F32), 8 * _nbytes((n_s, CONV_LANES), F32)),
    )(*_hbm(x, w, dout))


_BNT = (((2,), (2,)), ((0,), (0,)))
_BNN = (((2,), (1,)), ((0,), (0,)))
_BTN = (((1,), (1,)), ((0,), (0,)))
DN_PREP_CHUNKS = 8
DN_SCAN_HEADS = 4
N_DOUBLINGS = 5


def _fdot(a, b, dims):
    return lax.dot_general(a, b, dims, precision=lax.Precision.HIGHEST, preferred_element_type=F32)


def _hdot(a, b, dims):
    return lax.dot_general(a, b, dims, precision=lax.Precision.HIGH, preferred_element_type=F32)


def _solve_by_doubling(a, rhs_u, rhs_w):
    row = lax.broadcasted_iota(jnp.int32, (CHUNK, CHUNK), 0)
    col = lax.broadcasted_iota(jnp.int32, (CHUNK, CHUNK), 1)
    inv = jnp.where(row == col, 1.0, 0.0) - a
    power = a
    for _ in range(N_DOUBLINGS):
        power = _hdot(power, power, _BNN)
        inv = inv + _hdot(inv, power, _BNN)
    return _hdot(inv, rhs_u, _BNN), _hdot(inv, rhs_w, _BNN), inv


@jax.custom_vjp
def _solve_saved(a, rhs_u, rhs_w, inv, u, w):
    return u, w


def _solve_saved_fwd(a, rhs_u, rhs_w, inv, u, w):
    return (u, w), (inv, u, w)


def _solve_saved_bwd(res, cts):
    inv, u, w = res
    gu = _hdot(inv, cts[0], _BTN)
    gw = _hdot(inv, cts[1], _BTN)
    da = -(_hdot(gu, u, _BNT) + _hdot(gw, w, _BNT))
    return da, gu, gw, jnp.zeros_like(inv), jnp.zeros_like(u), jnp.zeros_like(w)


_solve_saved.defvjp(_solve_saved_fwd, _solve_saved_bwd)


def _dn_prep_fn(solve, qc, kc, vc, bl, lac, lar, a_log, dt_bias):
    q = qc * lax.rsqrt(jnp.sum(qc * qc, axis=-1, keepdims=True) + EPS) * (DN_HEAD_DIM ** -0.5)
    k = kc * lax.rsqrt(jnp.sum(kc * kc, axis=-1, keepdims=True) + EPS)
    beta = jax.nn.sigmoid(bl)
    neg_a = -jnp.exp(a_log)
    lgc = neg_a * jax.nn.softplus(lac + dt_bias)
    lgr = neg_a * jax.nn.softplus(lar + dt_bias)
    row = lax.broadcasted_iota(jnp.int32, (CHUNK, CHUNK), 0)
    col = lax.broadcasted_iota(jnp.int32, (CHUNK, CHUNK), 1)
    causal, strict = row >= col, row > col
    g_c = jnp.sum(jnp.where(causal, lgr, 0.0), axis=-1, keepdims=True)
    g_r = jnp.sum(jnp.where(row <= col, lgc, 0.0), axis=-2, keepdims=True)
    decay = jnp.exp(jnp.where(causal, g_c - g_r, -jnp.inf))
    kb = k * beta
    a = jnp.where(strict, _mdot(kb, k, _BNT) * decay, 0.0)
    u, w, extra = solve(a, vc * beta, kb * jnp.exp(g_c))
    attn = _mdot(q, k, _BNT) * decay
    g_last = jnp.sum(lgc, axis=-2, keepdims=True)
    return q * jnp.exp(g_c), k * jnp.exp(g_last - g_c), u, w, attn, g_last, extra


PAIR = 2
PAIR_LANES = PAIR * DN_HEAD_DIM


def _dn_prep_specs(n_cb):
    tok = n_cb * CHUNK
    wide = pl.BlockSpec((1, PAIR, tok, DN_HEAD_DIM), lambda p, b, j: (b, p, j, 0))
    col = pl.BlockSpec((1, PAIR, tok, 1), lambda p, b, j: (b, p, j, 0))
    rowv = pl.BlockSpec((1, PAIR, n_cb, 1, CHUNK), lambda p, b, j: (b, p, j, 0, 0))
    one = pl.BlockSpec((1, PAIR, n_cb, 1, 1), lambda p, b, j: (b, p, j, 0, 0))
    head = pl.BlockSpec((PAIR, 1, 1), lambda p, b, j: (p, 0, 0))
    lanes = lambda part: pl.BlockSpec((1, tok, PAIR_LANES), lambda p, b, j: (b, j, part * (DN_HEADS // PAIR) + p))
    return wide, col, rowv, one, head, lanes


def _split_pair(x, n_cb):
    halves = [x[:, h * DN_HEAD_DIM:(h + 1) * DN_HEAD_DIM].reshape(n_cb, CHUNK, DN_HEAD_DIM) for h in range(PAIR)]
    return jnp.concatenate(halves, axis=0)


def _join_pair(chunks, tok):
    per_head = chunks.reshape(PAIR, tok, DN_HEAD_DIM)
    return jnp.concatenate([per_head[h] for h in range(PAIR)], axis=-1)


def _dn_prep_load(n_cb, q_ref, k_ref, v_ref, bl_ref, lac_ref, lar_ref, al_ref, dt_ref):
    colv = lambda r: r[0].reshape(PAIR * n_cb, CHUNK, 1)
    return (_split_pair(q_ref[0], n_cb), _split_pair(k_ref[0], n_cb), _split_pair(v_ref[0], n_cb), colv(bl_ref),
            colv(lac_ref), lar_ref[0].reshape(PAIR * n_cb, 1, CHUNK), al_ref[...], dt_ref[...])


def _dn_prep_pair_fn(n_cb, solve, qc, kc, vc, bl, lac, lar, a_log, dt_bias):
    per_chunk = lambda t: jnp.broadcast_to(t[:, None], (PAIR, n_cb, 1, 1)).reshape(PAIR * n_cb, 1, 1)
    return _dn_prep_fn(solve, qc, kc, vc, bl, lac, lar, per_chunk(a_log), per_chunk(dt_bias))


def _dn_prep(qkv, bl, lac, lar, a_log, dt_bias, *, name):
    n_b, n_s, _ = qkv.shape
    n_cb = min(DN_PREP_CHUNKS, n_s // CHUNK)
    tok = n_cb * CHUNK
    wide, col, rowv, one, head, lanes = _dn_prep_specs(n_cb)

    def body(*refs):
        outs = _dn_prep_pair_fn(n_cb, _solve_by_doubling, *_dn_prep_load(n_cb, *refs[:8]))
        for ref, val in zip(refs[8:13], outs[:5]):
            ref[0] = val.reshape(PAIR, tok, DN_HEAD_DIM)
        refs[13][0] = outs[5].reshape(PAIR, n_cb, 1, 1)
        refs[14][0] = outs[6].reshape(PAIR, tok, DN_HEAD_DIM)

    big = jax.ShapeDtypeStruct((n_b, DN_HEADS, n_s, DN_HEAD_DIM), F32)
    return pl.pallas_call(
        body, name=name, grid=(DN_HEADS // PAIR, n_b, n_s // tok),
        in_specs=[lanes(0), lanes(1), lanes(2), col, col, rowv, head, head],
        out_specs=(wide, wide, wide, wide, wide, one, wide),
        out_shape=(big, big, big, big, big, jax.ShapeDtypeStruct((n_b, DN_HEADS, n_s // CHUNK, 1, 1), F32), big),
        compiler_params=_params(11 * PAIR * _nbytes((tok, LANES), F32), 48 * PAIR * _nbytes((tok, LANES), F32)),
    )(*_hbm(qkv, qkv, qkv, bl, lac, lar, a_log, dt_bias))


def _dn_prep_bwd(qkv, bl, lac, lar, a_log, dt_bias, inv, u, w, cts, *, name):
    n_b, n_s, _ = qkv.shape
    n_cb = min(DN_PREP_CHUNKS, n_s // CHUNK)
    tok = n_cb * CHUNK
    wide, col, rowv, one, head, lanes = _dn_prep_specs(n_cb)

    def body(*refs):
        prim = _dn_prep_load(n_cb, *refs[:8])
        chunks = lambda r: r[0].reshape(PAIR * n_cb, CHUNK, DN_HEAD_DIM)
        inv_v, u_v, w_v = chunks(refs[8]), chunks(refs[9]), chunks(refs[10])
        ct = tuple(chunks(r) for r in refs[11:16]) + (refs[16][0].reshape(PAIR * n_cb, 1, 1),)

        def fn(*args):
            solve = lambda a, ru, rw: _solve_saved(a, ru, rw, inv_v, u_v, w_v) + (None,)
            return _dn_prep_pair_fn(n_cb, solve, *args)[:6]

        _, pull = jax.vjp(fn, *prim)
        dq, dk, dv, dbl, dlac, dlar, dal, ddt = pull(ct)
        outs = refs[17:]
        for part, val in enumerate((dq, dk, dv)):
            outs[0][part, 0] = _join_pair(val, tok)
        outs[1][0] = dbl.reshape(PAIR, tok, 1)
        outs[2][0] = dlac.reshape(PAIR, tok, 1)
        outs[3][0] = dlar.reshape(PAIR, n_cb, 1, CHUNK)
        first = jnp.logical_and(pl.program_id(1) == 0, pl.program_id(2) == 0)
        _accumulate(outs[4], dal, first)
        _accumulate(outs[5], ddt, first)

    dqkv_spec = pl.BlockSpec((3, 1, tok, PAIR_LANES), lambda p, b, j: (0, b, j, p))
    return pl.pallas_call(
        body, name=name, grid=(DN_HEADS // PAIR, n_b, n_s // tok),
        in_specs=[lanes(0), lanes(1), lanes(2), col, col, rowv, head, head, wide, wide, wide, wide, wide, wide, wide, wide, one],
        out_specs=(dqkv_spec, col, col, rowv, head, head),
        out_shape=(jax.ShapeDtypeStruct((3, n_b, n_s, DN_WIDTH), F32), jax.ShapeDtypeStruct(bl.shape, F32),
                   jax.ShapeDtypeStruct(lac.shape, F32), jax.ShapeDtypeStruct(lar.shape, F32),
                   jax.ShapeDtypeStruct(a_log.shape, F32), jax.ShapeDtypeStruct(dt_bias.shape, F32)),
        compiler_params=_params(21 * PAIR * _nbytes((tok, LANES), F32), 64 * PAIR * _nbytes((tok, LANES), F32)),
    )(*_hbm(qkv, qkv, qkv, bl, lac, lar, a_log, dt_bias, inv, u, w, *cts))


def _dn_step(state, q, k, u, w, a, gl):
    v_new = u - _mdot(w, state, _BNN)
    o = _mdot(q, state, _BNN) + _mdot(a, v_new, _BNN)
    return state * jnp.exp(gl) + _mdot(k, v_new, _BTN), o


def _dn_scan_specs(n_cb, n_blocks, reverse):
    tok = n_cb * CHUNK
    jj = (lambda j: n_blocks - 1 - j) if reverse else (lambda j: j)
    wide = pl.BlockSpec((1, DN_SCAN_HEADS, tok, DN_HEAD_DIM), lambda b, h, j: (b, h, jj(j), 0))
    one = pl.BlockSpec((1, DN_SCAN_HEADS, n_cb, 1, 1), lambda b, h, j: (b, h, jj(j), 0, 0))
    st = pl.BlockSpec((1, DN_SCAN_HEADS, n_cb, DN_HEAD_DIM, DN_HEAD_DIM), lambda b, h, j: (b, h, jj(j), 0, 0))
    return wide, one, st


def _dn_scan(qd, kd, u, w, attn, g_last, *, name):
    n_b, n_h, n_s, _ = qd.shape
    n_cb = min(DN_PREP_CHUNKS, n_s // CHUNK)
    n_blocks = n_s // (n_cb * CHUNK)
    wide, one, st = _dn_scan_specs(n_cb, n_blocks, False)

    def body(qd_ref, kd_ref, u_ref, w_ref, a_ref, gl_ref, o_ref, st_ref, state_ref):
        @pl.when(pl.program_id(2) == 0)
        def _():
            state_ref[...] = jnp.zeros(state_ref.shape, F32)

        def step(n, state):
            rows = pl.ds(pl.multiple_of(n * CHUNK, CHUNK), CHUNK)
            st_ref[0, :, n] = state
            state, o = _dn_step(state, qd_ref[0, :, rows, :], kd_ref[0, :, rows, :], u_ref[0, :, rows, :],
                                w_ref[0, :, rows, :], a_ref[0, :, rows, :], gl_ref[0, :, n])
            o_ref[0, :, rows, :] = o
            return state

        state_ref[...] = lax.fori_loop(0, n_cb, step, state_ref[...])

    return pl.pallas_call(
        body, name=name, grid=(n_b, n_h // DN_SCAN_HEADS, n_blocks),
        in_specs=[wide, wide, wide, wide, wide, one], out_specs=(wide, st),
        out_shape=(jax.ShapeDtypeStruct(qd.shape, F32),
                   jax.ShapeDtypeStruct((n_b, n_h, n_s // CHUNK, DN_HEAD_DIM, DN_HEAD_DIM), F32)),
        scratch_shapes=[pltpu.VMEM((DN_SCAN_HEADS, DN_HEAD_DIM, DN_HEAD_DIM), F32)],
        compiler_params=_params(8 * _nbytes((DN_SCAN_HEADS, n_cb * CHUNK, LANES), F32), 8 << 20),
    )(*_hbm(qd, kd, u, w, attn, g_last))


def _dn_scan_bwd(qd, kd, u, w, attn, g_last, states, do, *, name):
    n_b, n_h, n_s, _ = qd.shape
    n_cb = min(DN_PREP_CHUNKS, n_s // CHUNK)
    n_blocks = n_s // (n_cb * CHUNK)
    wide, one, st = _dn_scan_specs(n_cb, n_blocks, True)

    def body(qd_ref, kd_ref, u_ref, w_ref, a_ref, gl_ref, st_ref, do_ref,
             dq_ref, dk_ref, du_ref, dw_ref, da_ref, dgl_ref, dstate_ref):
        @pl.when(pl.program_id(2) == 0)
        def _():
            dstate_ref[...] = jnp.zeros(dstate_ref.shape, F32)

        def step(i, dstate):
            n = n_cb - 1 - i
            rows = pl.ds(pl.multiple_of(n * CHUNK, CHUNK), CHUNK)
            _, pull = jax.vjp(_dn_step, st_ref[0, :, n], qd_ref[0, :, rows, :], kd_ref[0, :, rows, :],
                              u_ref[0, :, rows, :], w_ref[0, :, rows, :], a_ref[0, :, rows, :], gl_ref[0, :, n])
            dstate, dq, dk, du, dw, da, dgl = pull((dstate, do_ref[0, :, rows, :]))
            dq_ref[0, :, rows, :] = dq
            dk_ref[0, :, rows, :] = dk
            du_ref[0, :, rows, :] = du
            dw_ref[0, :, rows, :] = dw
            da_ref[0, :, rows, :] = da
            dgl_ref[0, :, n] = dgl
            return dstate

        dstate_ref[...] = lax.fori_loop(0, n_cb, step, dstate_ref[...])

    big = jax.ShapeDtypeStruct(qd.shape, F32)
    return pl.pallas_call(
        body, name=name, grid=(n_b, n_h // DN_SCAN_HEADS, n_blocks),
        in_specs=[wide, wide, wide, wide, wide, one, st, wide],
        out_specs=(wide, wide, wide, wide, wide, one),
        out_shape=(big, big, big, big, big, jax.ShapeDtypeStruct(g_last.shape, F32)),
        scratch_shapes=[pltpu.VMEM((DN_SCAN_HEADS, DN_HEAD_DIM, DN_HEAD_DIM), F32)],
        compiler_params=_params(13 * _nbytes((DN_SCAN_HEADS, n_cb * CHUNK, LANES), F32), 8 << 20),
    )(*_hbm(qd, kd, u, w, attn, g_last, states, do))


def _dn_post_fn(o, z, gain):
    return o * lax.rsqrt(jnp.mean(o * o, axis=-1, keepdims=True) + EPS) * gain * jax.nn.silu(z)


_HEAD_ROWS = lambda n_s: pl.BlockSpec((1, PAIR, n_s, DN_HEAD_DIM), lambda b, p: (b, p, 0, 0))
_PAIR_LANES = lambda n_s: pl.BlockSpec((1, n_s, PAIR_LANES), lambda b, p: (b, 0, p))
_HEAD_GAIN = pl.BlockSpec((1, DN_HEAD_DIM), lambda b, p: (0, 0))


def _pair_heads(x):
    return jnp.stack([x[:, h * DN_HEAD_DIM:(h + 1) * DN_HEAD_DIM] for h in range(PAIR)])


def _pair_lanes(x):
    return jnp.concatenate([x[h] for h in range(PAIR)], axis=-1)


def _dn_post(o, z, gain, *, name):
    n_b, _, n_s, _ = o.shape

    def body(o_ref, z_ref, g_ref, out_ref):
        out = _dn_post_fn(o_ref[0], _pair_heads(z_ref[0]), g_ref[...])
        out_ref[0] = _pair_lanes(out).astype(out_ref.dtype)

    lanes = _PAIR_LANES(n_s)
    return pl.pallas_call(
        body, name=name, grid=(n_b, DN_HEADS // PAIR), in_specs=[_HEAD_ROWS(n_s), lanes, _HEAD_GAIN], out_specs=lanes,
        out_shape=jax.ShapeDtypeStruct(z.shape, MXU_DTYPE),
        compiler_params=_params(3 * PAIR * _nbytes((n_s, LANES), F32), 6 * PAIR * _nbytes((n_s, LANES), F32)),
    )(*_hbm(o, z, gain))


def _dn_post_bwd(o, z, gain, dout, *, name):
    n_b, _, n_s, _ = o.shape

    def body(o_ref, z_ref, g_ref, dout_ref, do_ref, dz_ref, dg_ref):
        _, pull = jax.vjp(_dn_post_fn, o_ref[0], _pair_heads(z_ref[0]), g_ref[...])
        do, dz, dg = pull(_pair_heads(dout_ref[0].astype(F32)))
        do_ref[0] = do
        dz_ref[0] = _pair_lanes(dz).astype(dz_ref.dtype)
        _accumulate(dg_ref, dg, jnp.logical_and(pl.program_id(0) == 0, pl.program_id(1) == 0))

    rows, lanes = _HEAD_ROWS(n_s), _PAIR_LANES(n_s)
    return pl.pallas_call(
        body, name=name, grid=(n_b, DN_HEADS // PAIR), in_specs=[rows, lanes, _HEAD_GAIN, lanes],
        out_specs=(rows, lanes, _HEAD_GAIN),
        out_shape=(jax.ShapeDtypeStruct(o.shape, F32), jax.ShapeDtypeStruct(z.shape, MXU_DTYPE),
                   jax.ShapeDtypeStruct((1, DN_HEAD_DIM), F32)),
        compiler_params=_params(5 * PAIR * _nbytes((n_s, LANES), F32), 10 * PAIR * _nbytes((n_s, LANES), F32)),
    )(*_hbm(o, z, gain, dout))


S5_SCAN_LANES = 256
TILE_ROWS = SUBLANES


def _s5_prep_fn(lam_re, lam_im, log_step, bt_re, bt_im, c_im):
    lr = jnp.minimum(lam_re, -1e-4)
    step = jnp.exp(log_step)
    mag = jnp.exp(lr * step)
    ang = lam_im * step
    lb_re = mag * jnp.cos(ang)
    lb_im = mag * jnp.sin(ang)
    den = lr * lr + lam_im * lam_im
    coef_re = ((lb_re - 1.0) * lr + lb_im * lam_im) / den
    coef_im = (lb_im * lr - (lb_re - 1.0) * lam_im) / den
    return (lb_re, lb_im, coef_re * bt_re - coef_im * bt_im, coef_re * bt_im + coef_im * bt_re, -c_im)


def _s5_prep(lam_re, lam_im, log_step, bt_re, bt_im, c_im, *, name):
    def body(*refs):
        outs = _s5_prep_fn(*(r[...] for r in refs[:6]))
        for ref, val in zip(refs[6:], outs):
            ref[...] = val

    vec = jax.ShapeDtypeStruct(lam_re.shape, F32)
    mat = jax.ShapeDtypeStruct(bt_re.shape, F32)
    return pl.pallas_call(body, name=name, out_shape=(vec, vec, mat, mat, mat))(lam_re, lam_im, log_step, bt_re, bt_im, c_im)


def _s5_prep_bwd(lam_re, lam_im, log_step, bt_re, bt_im, c_im, cts, *, name):
    def body(*refs):
        _, pull = jax.vjp(_s5_prep_fn, *(r[...] for r in refs[:6]))
        grads = pull(tuple(r[...] for r in refs[6:11]))
        for ref, val in zip(refs[11:], grads):
            ref[...] = val

    shapes = tuple(jax.ShapeDtypeStruct(a.shape, F32) for a in (lam_re, lam_im, log_step, bt_re, bt_im, c_im))
    return pl.pallas_call(body, name=name, out_shape=shapes)(lam_re, lam_im, log_step, bt_re, bt_im, c_im, *cts)


def _cmul(ar, ai, br, bi):
    return ar * br - ai * bi, ar * bi + ai * br


def _s5_powers(lr, li):
    pows = [(lr, li)]
    for _ in range(TILE_ROWS - 1):
        pows.append(_cmul(pows[-1][0], pows[-1][1], lr, li))
    return pows


def _s5_carry_table(pows, n_lanes, reverse):
    row = lax.broadcasted_iota(jnp.int32, (TILE_ROWS, n_lanes), 0)
    t_re = jnp.zeros((TILE_ROWS, n_lanes), F32)
    t_im = jnp.zeros((TILE_ROWS, n_lanes), F32)
    for r in range(TILE_ROWS):
        p_re, p_im = pows[TILE_ROWS - 1 - r] if reverse else pows[r]
        t_re = jnp.where(row == r, p_re, t_re)
        t_im = jnp.where(row == r, p_im, t_im)
    return t_re, t_im


def _s5_tile(y_re, y_im, pows, reverse):
    d = 1
    while d < TILE_ROWS:
        p_re, p_im = pows[d - 1]
        if reverse:
            s_re, s_im = _shift_up(y_re, d), _shift_up(y_im, d)
        else:
            s_re, s_im = _shift_down(y_re, d), _shift_down(y_im, d)
        m_re, m_im = _cmul(p_re, p_im, s_re, s_im)
        y_re, y_im = y_re + m_re, y_im + m_im
        d *= 2
    return y_re, y_im


S5_BLOCKS = N_SHARD
S5_BLOCK_CH = S5_WIDTH // S5_BLOCKS
S5_BLOCK_LANES = S5_LANES // S5_BLOCKS
SCAN_PER_BLOCK = S5_BLOCK_LANES // S5_SCAN_LANES


def _s5_scan_specs(n_s, order):
    L = S5_SCAN_LANES

    def cat_spec(part):
        return pl.BlockSpec((1, 1, n_s, L), lambda *g: (order(*g)[1] // SCAN_PER_BLOCK, order(*g)[0], 0,
                                                        part * SCAN_PER_BLOCK + order(*g)[1] % SCAN_PER_BLOCK))

    one = pl.BlockSpec((1, 1, n_s, L), lambda *g: (order(*g)[1] // SCAN_PER_BLOCK, order(*g)[0], 0,
                                                   order(*g)[1] % SCAN_PER_BLOCK))
    lam = pl.BlockSpec((1, L), lambda *g: (0, order(*g)[1]))
    return cat_spec, one, lam


def _s5_scan(bu, lb_re, lb_im, *, name):
    n_blk, n_b, n_s, _ = bu.shape
    n_lb = S5_LANES // S5_SCAN_LANES
    n_tiles = n_s // TILE_ROWS
    L = S5_SCAN_LANES

    def body(re_ref, im_ref, lr_ref, li_ref, xr_ref, xi_ref):
        pows = _s5_powers(lr_ref[...], li_ref[...])
        t_re, t_im = _s5_carry_table(pows, L, False)

        def step(i, carry):
            rows = pl.ds(pl.multiple_of(i * TILE_ROWS, TILE_ROWS), TILE_ROWS)
            y_re, y_im = _s5_tile(re_ref[0, 0, rows, :], im_ref[0, 0, rows, :], pows, False)
            c_re, c_im = _cmul(t_re, t_im, carry[0], carry[1])
            y_re, y_im = y_re + c_re, y_im + c_im
            xr_ref[0, 0, rows, :] = y_re
            xi_ref[0, 0, rows, :] = y_im
            return y_re[TILE_ROWS - 1:, :], y_im[TILE_ROWS - 1:, :]

        zero = jnp.zeros((1, L), F32)
        lax.fori_loop(0, n_tiles, step, (zero, zero))

    cat_spec, one, lam = _s5_scan_specs(n_s, lambda b, j: (b, j))
    x_shape = jax.ShapeDtypeStruct((n_blk, n_b, n_s, S5_BLOCK_LANES), F32)
    return pl.pallas_call(
        body, name=name, grid=(n_b, n_lb),
        in_specs=[cat_spec(0), cat_spec(1), lam, lam],
        out_specs=(one, one), out_shape=(x_shape, x_shape),
        compiler_params=_params(4 * _nbytes((n_s, L), F32), 4 << 20),
    )(*_hbm(bu, bu, lb_re, lb_im))


def _s5_scan_bwd(dx, x_re, x_im, lb_re, lb_im, *, name):
    n_blk, n_b, n_s, _ = dx.shape
    n_lb = S5_LANES // S5_SCAN_LANES
    n_tiles = n_s // TILE_ROWS
    L = S5_SCAN_LANES

    def body(dr_ref, di_ref, xr_ref, xi_ref, lr_ref, li_ref, ar_ref, ai_ref, dlr_ref, dli_ref):
        pows = _s5_powers(lr_ref[...], -li_ref[...])
        t_re, t_im = _s5_carry_table(pows, L, True)
        row = lax.broadcasted_iota(jnp.int32, (TILE_ROWS, L), 0)

        def step(k, carry):
            c_re, c_im, s_re, s_im = carry
            i = n_tiles - 1 - k
            rows = pl.ds(pl.multiple_of(i * TILE_ROWS, TILE_ROWS), TILE_ROWS)
            a_re, a_im = _s5_tile(dr_ref[0, 0, rows, :], di_ref[0, 0, rows, :], pows, True)
            m_re, m_im = _cmul(t_re, t_im, c_re, c_im)
            a_re, a_im = a_re + m_re, a_im + m_im
            ar_ref[0, 0, rows, :] = a_re.astype(ar_ref.dtype)
            ai_ref[0, 0, rows, :] = a_im.astype(ai_ref.dtype)
            prev = pl.ds(pl.multiple_of(jnp.maximum(i - 1, 0) * TILE_ROWS, TILE_ROWS), TILE_ROWS)
            keep = jnp.where(i > 0, 1.0, 0.0)
            last_re = xr_ref[0, 0, prev, :][TILE_ROWS - 1:, :] * keep
            last_im = xi_ref[0, 0, prev, :][TILE_ROWS - 1:, :] * keep
            xp_re = jnp.where(row == 0, last_re, _shift_down(xr_ref[0, 0, rows, :], 1))
            xp_im = jnp.where(row == 0, last_im, _shift_down(xi_ref[0, 0, rows, :], 1))
            s_re = s_re + a_re * xp_re + a_im * xp_im
            s_im = s_im + a_im * xp_re - a_re * xp_im
            return a_re[:1, :], a_im[:1, :], s_re, s_im

        zero = jnp.zeros((1, L), F32)
        zt = jnp.zeros((TILE_ROWS, L), F32)
        _, _, s_re, s_im = lax.fori_loop(0, n_tiles, step, (zero, zero, zt, zt))
        first = pl.program_id(1) == 0
        _accumulate(dlr_ref, jnp.sum(s_re, axis=0, keepdims=True), first)
        _accumulate(dli_ref, jnp.sum(s_im, axis=0, keepdims=True), first)

    cat_spec, one, lam = _s5_scan_specs(n_s, lambda j, b: (b, j))
    a_shape = jax.ShapeDtypeStruct((n_blk, n_b, n_s, S5_BLOCK_LANES), MXU_DTYPE)
    lam_shape = jax.ShapeDtypeStruct((1, S5_LANES), F32)
    return pl.pallas_call(
        body, name=name, grid=(n_lb, n_b),
        in_specs=[cat_spec(0), cat_spec(1), one, one, lam, lam],
        out_specs=(one, one, lam, lam),
        out_shape=(a_shape, a_shape, lam_shape, lam_shape),
        compiler_params=_params(5 * _nbytes((n_s, L), F32), 4 << 20),
    )(*_hbm(dx, dx, x_re, x_im, lb_re, lb_im))


def _scan_rows(i):
    return pl.ds(pl.multiple_of(i * TILE_ROWS, TILE_ROWS), TILE_ROWS)


def _s5_mix_specs(n_s, order):
    jb = lambda *g: order(*g)[0]
    bb = lambda *g: order(*g)[1]
    act = pl.BlockSpec((1, 1, n_s, S5_BLOCK_CH), lambda *g: (bb(*g), 0, 0, jb(*g)))
    state = pl.BlockSpec((1, 1, n_s, S5_BLOCK_LANES), lambda *g: (jb(*g), bb(*g), 0, 0))
    lam = pl.BlockSpec((1, S5_BLOCK_LANES), lambda *g: (0, jb(*g)))
    w_in = pl.BlockSpec((1, S5_BLOCK_CH, S5_BLOCK_LANES), lambda *g: (jb(*g), 0, 0))
    w_out = pl.BlockSpec((1, S5_BLOCK_LANES, S5_BLOCK_CH), lambda *g: (jb(*g), 0, 0))
    return act, state, lam, w_in, w_out


def _s5_mix(u, wb_re, wb_im, lb_re, lb_im, wc_re, wc_im, *, name):
    n_b, n_s, _ = u.shape
    n_blk = S5_BLOCKS
    lanes = lambda t: t[:, None]
    n_tiles = n_s // TILE_ROWS
    L = S5_BLOCK_LANES

    def body(u_ref, wbr_ref, wbi_ref, lr_ref, li_ref, wcr_ref, wci_ref, y_ref, xr_ref, xi_ref):
        uv = u_ref[0, 0].astype(MXU_DTYPE)
        xr_ref[0, 0] = lax.dot_general(uv, wbr_ref[0].astype(MXU_DTYPE), _NN, preferred_element_type=F32)
        xi_ref[0, 0] = lax.dot_general(uv, wbi_ref[0].astype(MXU_DTYPE), _NN, preferred_element_type=F32)
        pows = _s5_powers(lr_ref[...], li_ref[...])
        t_re, t_im = _s5_carry_table(pows, L, False)

        def step(i, carry):
            rows = _scan_rows(i)
            y_re, y_im = _s5_tile(xr_ref[0, 0, rows, :], xi_ref[0, 0, rows, :], pows, False)
            c_re, c_im = _cmul(t_re, t_im, carry[0], carry[1])
            y_re, y_im = y_re + c_re, y_im + c_im
            xr_ref[0, 0, rows, :] = y_re
            xi_ref[0, 0, rows, :] = y_im
            return y_re[TILE_ROWS - 1:, :], y_im[TILE_ROWS - 1:, :]

        zero = jnp.zeros((1, L), F32)
        lax.fori_loop(0, n_tiles, step, (zero, zero))
        y_ref[0, 0] = (
            lax.dot_general(xr_ref[0, 0].astype(MXU_DTYPE), wcr_ref[0].astype(MXU_DTYPE), _NN, preferred_element_type=F32)
            + lax.dot_general(xi_ref[0, 0].astype(MXU_DTYPE), wci_ref[0].astype(MXU_DTYPE), _NN, preferred_element_type=F32))

    act, state, lam, w_in, w_out = _s5_mix_specs(n_s, lambda b, j: (j, b))
    x_shape = jax.ShapeDtypeStruct((n_blk, n_b, n_s, L), F32)
    return pl.pallas_call(
        body, name=name, grid=(n_b, n_blk),
        in_specs=[act, w_in, w_in, lam, lam, w_out, w_out], out_specs=(act, state, state),
        out_shape=(jax.ShapeDtypeStruct((n_b, 1, n_s, S5_WIDTH), F32), x_shape, x_shape),
        compiler_params=_params(2 * _nbytes((n_s, L), F32) + 2 * _nbytes((n_s, S5_BLOCK_CH), F32), 3 * _nbytes((n_s, L), F32)),
    )(*_hbm(lanes(u), wb_re, wb_im, lb_re, lb_im, wc_re, wc_im))


def _s5_mix_bwd(dy, du_skip, u, x_re, x_im, wb_re, wb_im, lb_re, lb_im, wc_re, wc_im, *, name):
    n_b, n_s, _ = u.shape
    n_blk = S5_BLOCKS
    lanes = lambda t: t[:, None]
    n_tiles = n_s // TILE_ROWS
    L = S5_BLOCK_LANES

    def body(dy_ref, ds_ref, u_ref, xr_ref, xi_ref, wbr_ref, wbi_ref, lr_ref, li_ref, wcr_ref, wci_ref,
             du_ref, dwbr_ref, dwbi_ref, dlr_ref, dli_ref, dwcr_ref, dwci_ref, ar_ref, ai_ref):
        dyv = dy_ref[0, 0].astype(MXU_DTYPE)
        ar_ref[...] = lax.dot_general(dyv, wcr_ref[0].astype(MXU_DTYPE), _NT, preferred_element_type=F32)
        ai_ref[...] = lax.dot_general(dyv, wci_ref[0].astype(MXU_DTYPE), _NT, preferred_element_type=F32)
        pows = _s5_powers(lr_ref[...], -li_ref[...])
        t_re, t_im = _s5_carry_table(pows, L, True)
        row = lax.broadcasted_iota(jnp.int32, (TILE_ROWS, L), 0)

        def step(k, carry):
            c_re, c_im, s_re, s_im = carry
            i = n_tiles - 1 - k
            rows = _scan_rows(i)
            a_re, a_im = _s5_tile(ar_ref[rows, :], ai_ref[rows, :], pows, True)
            m_re, m_im = _cmul(t_re, t_im, c_re, c_im)
            a_re, a_im = a_re + m_re, a_im + m_im
            ar_ref[rows, :] = a_re
            ai_ref[rows, :] = a_im
            prev = _scan_rows(jnp.maximum(i - 1, 0))
            keep = jnp.where(i > 0, 1.0, 0.0)
            last_re = xr_ref[0, 0, prev, :][TILE_ROWS - 1:, :] * keep
            last_im = xi_ref[0, 0, prev, :][TILE_ROWS - 1:, :] * keep
            xp_re = jnp.where(row == 0, last_re, _shift_down(xr_ref[0, 0, rows, :], 1))
            xp_im = jnp.where(row == 0, last_im, _shift_down(xi_ref[0, 0, rows, :], 1))
            s_re = s_re + a_re * xp_re + a_im * xp_im
            s_im = s_im + a_im * xp_re - a_re * xp_im
            return a_re[:1, :], a_im[:1, :], s_re, s_im

        zero = jnp.zeros((1, L), F32)
        zt = jnp.zeros((TILE_ROWS, L), F32)
        _, _, s_re, s_im = lax.fori_loop(0, n_tiles, step, (zero, zero, zt, zt))
        first = pl.program_id(1) == 0
        _accumulate(dlr_ref, jnp.sum(s_re, axis=0, keepdims=True), first)
        _accumulate(dli_ref, jnp.sum(s_im, axis=0, keepdims=True), first)
        a_re, a_im = ar_ref[...].astype(MXU_DTYPE), ai_ref[...].astype(MXU_DTYPE)
        du = (lax.dot_general(a_re, wbr_ref[0].astype(MXU_DTYPE), _NT, preferred_element_type=F32)
              + lax.dot_general(a_im, wbi_ref[0].astype(MXU_DTYPE), _NT, preferred_element_type=F32))
        du_ref[0, 0] = (du + ds_ref[0, 0]).astype(du_ref.dtype)
        uv = u_ref[0, 0].astype(MXU_DTYPE)
        _accumulate(dwbr_ref, lax.dot_general(uv, a_re, _TN, preferred_element_type=F32)[None], first)
        _accumulate(dwbi_ref, lax.dot_general(uv, a_im, _TN, preferred_element_type=F32)[None], first)
        _accumulate(dwcr_ref, lax.dot_general(xr_ref[0, 0].astype(MXU_DTYPE), dyv, _TN, preferred_element_type=F32)[None], first)
        _accumulate(dwci_ref, lax.dot_general(xi_ref[0, 0].astype(MXU_DTYPE), dyv, _TN, preferred_element_type=F32)[None], first)

    act, state, lam, w_in, w_out = _s5_mix_specs(n_s, lambda j, b: (j, b))
    lam_shape = jax.ShapeDtypeStruct((1, S5_LANES), F32)
    return pl.pallas_call(
        body, name=name, grid=(n_blk, n_b),
        in_specs=[act, act, act, state, state, w_in, w_in, lam, lam, w_out, w_out],
        out_specs=(act, w_in, w_in, lam, lam, w_out, w_out),
        out_shape=(jax.ShapeDtypeStruct((n_b, 1, n_s, S5_WIDTH), MXU_DTYPE), jax.ShapeDtypeStruct(wb_re.shape, F32),
                   jax.ShapeDtypeStruct(wb_im.shape, F32), lam_shape, lam_shape,
                   jax.ShapeDtypeStruct(wc_re.shape, F32), jax.ShapeDtypeStruct(wc_im.shape, F32)),
        scratch_shapes=[pltpu.VMEM((n_s, L), F32), pltpu.VMEM((n_s, L), F32)],
        compiler_params=_params(2 * _nbytes((n_s, L), F32) + 4 * _nbytes((n_s, S5_BLOCK_CH), F32), 5 * _nbytes((n_s, L), F32)),
    )(*_hbm(lanes(dy), lanes(du_skip), lanes(u), x_re, x_im, wb_re, wb_im, lb_re, lb_im, wc_re, wc_im))


def _s5_out_fn(ymm, u, d_skip, w_glu, b_glu):
    y = jax.nn.gelu(ymm + d_skip * u)
    return y * jax.nn.sigmoid(_mdot(y, w_glu, _NN) + b_glu)


def _s5_out_specs(tm):
    rows = pl.BlockSpec((tm, S5_WIDTH), lambda i: (i, 0))
    vec = pl.BlockSpec((1, S5_WIDTH), lambda i: (0, 0))
    mat = pl.BlockSpec((S5_WIDTH, S5_WIDTH), lambda i: (0, 0))
    return rows, vec, mat


def _s5_out(ymm, u, d_skip, w_glu, b_glu, *, name, tm=512):
    n_tok = ymm.shape[0]
    tm = min(tm, n_tok)
    rows, vec, mat = _s5_out_specs(tm)

    def body(y_ref, u_ref, d_ref, w_ref, b_ref, o_ref):
        o_ref[...] = _s5_out_fn(y_ref[...], u_ref[...], d_ref[...], w_ref[...], b_ref[...]).astype(o_ref.dtype)

    return pl.pallas_call(
        body, name=name, grid=(n_tok // tm,), in_specs=[rows, rows, vec, mat, vec], out_specs=rows,
        out_shape=jax.ShapeDtypeStruct((n_tok, S5_WIDTH), MXU_DTYPE),
        compiler_params=_params(4 * _nbytes((tm, S5_WIDTH), F32), 8 * _nbytes((tm, S5_WIDTH), F32)),
    )(*_hbm(ymm, u, d_skip, w_glu, b_glu))


def _s5_out_bwd(ymm, u, d_skip, w_glu, b_glu, dout, *, name, tm=512):
    n_tok = ymm.shape[0]
    tm = min(tm, n_tok)
    rows, vec, mat = _s5_out_specs(tm)

    def body(y_ref, u_ref, d_ref, w_ref, b_ref, do_ref, dy_ref, du_ref, dd_ref, dw_ref, db_ref):
        _, pull = jax.vjp(_s5_out_fn, y_ref[...], u_ref[...], d_ref[...], w_ref[...].astype(F32), b_ref[...])
        dy, du, dd, dw, db = pull(do_ref[...])
        dy_ref[...] = dy.astype(dy_ref.dtype)
        du_ref[...] = du
        first = pl.program_id(0) == 0
        _accumulate(dd_ref, dd, first)
        _accumulate(dw_ref, dw, first)
        _accumulate(db_ref, db, first)

    return pl.pallas_call(
        body, name=name, grid=(n_tok // tm,), in_specs=[rows, rows, vec, mat, vec, rows],
        out_specs=(rows, rows, vec, mat, vec),
        out_shape=(jax.ShapeDtypeStruct(ymm.shape, MXU_DTYPE), jax.ShapeDtypeStruct(ymm.shape, F32),
                   jax.ShapeDtypeStruct((1, S5_WIDTH), F32), jax.ShapeDtypeStruct((S5_WIDTH, S5_WIDTH), F32),
                   jax.ShapeDtypeStruct((1, S5_WIDTH), F32)),
        compiler_params=_params(6 * _nbytes((tm, S5_WIDTH), F32), 12 * _nbytes((tm, S5_WIDTH), F32)),
    )(*_hbm(ymm, u, d_skip, w_glu, b_glu, dout))


def _merge_fn(ga, gb, ya, yb):
    return jax.nn.sigmoid(ga) * ya + jax.nn.sigmoid(gb) * yb


def _merge(gab, ya, yb, *, name, tm=512):
    n_tok = ya.shape[0]
    tm = min(tm, n_tok)
    rows = pl.BlockSpec((tm, D_MODEL), lambda i: (i, 0))

    def body(ga_ref, gb_ref, ya_ref, yb_ref, o_ref):
        o_ref[...] = _merge_fn(ga_ref[...], gb_ref[...], ya_ref[...], yb_ref[...]).astype(o_ref.dtype)

    return pl.pallas_call(
        body, name=name, grid=(n_tok // tm,),
        in_specs=[rows, pl.BlockSpec((tm, D_MODEL), lambda i: (i, 1)), rows, rows], out_specs=rows,
        out_shape=jax.ShapeDtypeStruct(ya.shape, MXU_DTYPE),
        compiler_params=_params(5 * _nbytes((tm, D_MODEL), F32), 4 * _nbytes((tm, D_MODEL), F32)),
    )(*_hbm(gab, gab, ya, yb))


def _merge_bwd(gab, ya, yb, dout, *, name, tm=512):
    n_tok = ya.shape[0]
    tm = min(tm, n_tok)
    rows = pl.BlockSpec((tm, D_MODEL), lambda i: (i, 0))

    def body(ga_ref, gb_ref, ya_ref, yb_ref, do_ref, *out_refs):
        _, pull = jax.vjp(_merge_fn, ga_ref[...], gb_ref[...], ya_ref[...], yb_ref[...])
        for ref, val in zip(out_refs, pull(do_ref[...])):
            ref[...] = val.astype(ref.dtype)

    shape = jax.ShapeDtypeStruct(ya.shape, MXU_DTYPE)
    return pl.pallas_call(
        body, name=name, grid=(n_tok // tm,),
        in_specs=[rows, pl.BlockSpec((tm, D_MODEL), lambda i: (i, 1)), rows, rows, rows],
        out_specs=(rows, rows, rows, rows), out_shape=(shape, shape, shape, shape),
        compiler_params=_params(7 * _nbytes((tm, D_MODEL), F32), 6 * _nbytes((tm, D_MODEL), F32)),
    )(*_hbm(gab, gab, ya, yb, dout))


ADA_SHARD = N_MOD * D_MODEL // N_SHARD


def _ada_fwd(c_pad, w_s, b_s, *, name):
    n_r = c_pad.shape[0]

    def body(c_ref, w_ref, b_ref, o_ref):
        sc = jax.nn.silu(c_ref[...]).astype(MXU_DTYPE)
        o_ref[0] = lax.dot_general(sc, w_ref[0].astype(MXU_DTYPE), _NN, preferred_element_type=F32) + b_ref[0]

    return pl.pallas_call(
        body, name=name, grid=(N_SHARD,),
        in_specs=[pl.BlockSpec((n_r, D_MODEL), lambda s: (0, 0)),
                  pl.BlockSpec((1, D_MODEL, ADA_SHARD), lambda s: (s, 0, 0)),
                  pl.BlockSpec((1, 1, ADA_SHARD), lambda s: (s, 0, 0))],
        out_specs=pl.BlockSpec((1, n_r, ADA_SHARD), lambda s: (s, 0, 0)),
        out_shape=jax.ShapeDtypeStruct((N_SHARD, n_r, ADA_SHARD), F32),
        compiler_params=_params(_nbytes((D_MODEL, ADA_SHARD), w_s.dtype), 1 << 20),
    )(*_hbm(c_pad, w_s, b_s))


def _ada_bwd(c_pad, dmod_s, *, name):
    n_r = c_pad.shape[0]

    def body(c_ref, d_ref, dw_ref, db_ref):
        sc = jax.nn.silu(c_ref[...])
        dm = d_ref[0]
        dw_ref[0] = _fdot(sc, dm, _TN)
        db_ref[0] = jnp.sum(dm, axis=0, keepdims=True)

    return pl.pallas_call(
        body, name=name, grid=(N_SHARD,),
        in_specs=[pl.BlockSpec((n_r, D_MODEL), lambda s: (0, 0)), pl.BlockSpec((1, n_r, ADA_SHARD), lambda s: (s, 0, 0))],
        out_specs=(pl.BlockSpec((1, D_MODEL, ADA_SHARD), lambda s: (s, 0, 0)),
                   pl.BlockSpec((1, 1, ADA_SHARD), lambda s: (s, 0, 0))),
        out_shape=(jax.ShapeDtypeStruct((N_SHARD, D_MODEL, ADA_SHARD), F32),
                   jax.ShapeDtypeStruct((N_SHARD, 1, ADA_SHARD), F32)),
        compiler_params=_params(_nbytes((D_MODEL, ADA_SHARD), F32), 2 * _nbytes((D_MODEL, ADA_SHARD), F32)),
    )(*_hbm(c_pad, dmod_s))


def _heads(t, n_b, n_s):
    return t.reshape(n_b, n_s, DN_HEADS, DN_HEAD_DIM).transpose(0, 2, 1, 3)


def _unheads(t):
    n_b, _, n_s, _ = t.shape
    return t.transpose(0, 2, 1, 3).reshape(n_b, n_s, DN_WIDTH)


def _block_diag(blocks):
    n_per = S5_GROUPS // S5_BLOCKS
    _, n_r, n_c = blocks.shape
    b4 = blocks.reshape(S5_BLOCKS, n_per, n_r, n_c)
    eye = jnp.eye(n_per, dtype=blocks.dtype)
    return (b4[:, :, :, None, :] * eye[None, :, None, :, None]).reshape(S5_BLOCKS, n_per * n_r, n_per * n_c)


def _diag_blocks(mat, n_r, n_c):
    n_per = S5_GROUPS // S5_BLOCKS
    m5 = mat.reshape(S5_BLOCKS, n_per, n_r, n_per, n_c)
    eye = jnp.eye(n_per, dtype=mat.dtype)
    return jnp.sum(m5 * eye[None, :, None, :, None], axis=3).reshape(S5_GROUPS, n_r, n_c)


def _local_step(x, c, target, wts):
    n_b, n_s, _ = x.shape
    n_tok = n_b * n_s
    flat = lambda t: t.reshape(n_tok, t.shape[-1])
    unflat = lambda t: t.reshape(n_b, n_s, t.shape[-1])
    n_chunks = n_s // CHUNK

    c_pad = jnp.zeros((SUBLANES, D_MODEL), F32).at[:n_b].set(c)
    mod_s = _ada_fwd(c_pad, wts["w_ada"], wts["b_ada"], name="ada_fwd")
    mod = mod_s.transpose(1, 0, 2).reshape(SUBLANES, N_MOD * D_MODEL)[:n_b]
    sh1, sc1, gt1, sh2, sc2, gt2, sh3, sc3, gt3 = [m[:, None, :] for m in jnp.split(mod, N_MOD, axis=-1)]

    a1 = _pre(x, None, None, wts["g_ffn1"], sh1, sc1, 0.0, name="pre1")
    f1, ffn1_saved = _ffn_fwd(flat(a1), wts["w1_ffn1"], wts["w3_ffn1"], wts["w2_ffn1"], "ffn1")
    x1, a2 = _pre(x, unflat(f1), gt1, wts["g_mix"], sh2, sc2, 0.5, name="pre2")
    u = flat(a2)[None]
    p_qkv = _mm_act([(u, wts["w_qkv"])], "nt", name="in_qkv")[0]
    p_z = _mm_act([(u, wts["w_z"])], "nt", name="in_z")[0]
    p_gab = _mm_act([(u, wts["w_gab"])], "nt", name="in_gab")[0]
    p_s5 = _mm_act([(u, wts["w_s5"])], "nt", name="in_s5")[0]
    p_ba = _mm_act([(u, wts["w_ba"])], "nt", name="in_ba")[0]

    qkv_c = _conv_fwd(unflat(p_qkv), wts["conv_qkv"], name="conv_fwd")
    z_tok = unflat(p_z)
    ba = p_ba.reshape(n_b, n_s, BA_PAD)
    bl = ba[:, :, :DN_HEADS].transpose(0, 2, 1)[..., None]
    lac = ba[:, :, DN_HEADS:2 * DN_HEADS].transpose(0, 2, 1)[..., None]
    lar = lac.reshape(n_b, DN_HEADS, n_chunks, 1, CHUNK)
    a_log, dt_bias = wts["a_log"], wts["dt_bias"]
    dn_in = (qkv_c, bl, lac, lar, a_log, dt_bias)
    qd, kd, uc, wc, attn, g_last, dn_inv = _dn_prep(*dn_in, name="dn_prep")
    o, states = _dn_scan(qd, kd, uc, wc, attn, g_last, name="dn_scan")
    og = _dn_post(o, z_tok, wts["g_onorm"], name="dn_post")
    og_t = og.reshape(1, n_tok, DN_WIDTH)
    ya = _mm_act([(og_t, wts["w_proj_a"])], "nn", name="proj_a")[0]

    s5p_in = (wts["lam_re"], wts["lam_im"], wts["log_step"], wts["bt_re"], wts["bt_im"], wts["c_im"])
    lb_re, lb_im, bb_re, bb_im, c_neg = _s5_prep(*s5p_in, name="s5_prep")
    wb_re, wb_im = _block_diag(bb_re), _block_diag(bb_im)
    wc_re = _block_diag(wts["c_re"].transpose(0, 2, 1))
    wc_im = _block_diag(c_neg.transpose(0, 2, 1))
    lbr, lbi = lb_re.reshape(1, S5_LANES), lb_im.reshape(1, S5_LANES)
    s5_w = (wb_re, wb_im, lbr, lbi, wc_re, wc_im)
    ymm, x_re, x_im = _s5_mix(unflat(p_s5), *s5_w, name="s5_mix")
    ymm = ymm.reshape(n_tok, S5_WIDTH)
    y2 = _s5_out(ymm, p_s5, wts["d_skip"], wts["w_glu"], wts["b_glu"], name="s5_out")
    yb = _mm_act([(y2[None], wts["w_proj_b"])], "nn", name="proj_b")[0]

    merged = _merge(p_gab, ya, yb, name="merge")
    m_out = _mm_act([(merged[None], wts["w_out"])], "nn", name="mix_out")[0]
    x2, a3 = _pre(x1, unflat(m_out), gt2, wts["g_ffn2"], sh3, sc3, 1.0, name="pre3")
    f3, ffn2_saved = _ffn_fwd(flat(a3), wts["w1_ffn2"], wts["w3_ffn2"], wts["w2_ffn2"], "ffn2")

    g = {}
    loss, dx2_res, df3, dgt3, g["g_final"] = _final(x2, unflat(f3), gt3, wts["g_final"], target, name="final")
    da3, g["w1_ffn2"], g["w3_ffn2"], g["w2_ffn2"] = _ffn_bwd(
        flat(a3), wts["w1_ffn2"], wts["w3_ffn2"], wts["w2_ffn2"], ffn2_saved, flat(df3), "ffn2")
    dx1_res, dm_out, dgt2, g["g_ffn2"], dsh3, dsc3 = _pre_bwd(
        x1, unflat(m_out), gt2, wts["g_ffn2"], sh3, sc3, 1.0, unflat(da3), dx2_res, name="pre3_bwd")
    dm_out = flat(dm_out)[None]
    dmerged = _mm_act([(dm_out, wts["w_out"])], "nt", name="mix_out_bwd")[0]
    g["w_out"] = _mm_tn(merged[None], dm_out, name="dw_out")[0]
    dga, dgb, dya, dyb = _merge_bwd(p_gab, ya, yb, dmerged, name="merge_bwd")

    dy2 = _mm_act([(dyb[None], wts["w_proj_b"])], "nt", name="proj_b_bwd")[0]
    g["w_proj_b"] = _mm_tn(y2[None], dyb[None], name="dw_proj_b")[0]
    dymm, du_skip, g["d_skip"], g["w_glu"], g["b_glu"] = _s5_out_bwd(
        ymm, p_s5, wts["d_skip"], wts["w_glu"], wts["b_glu"], dy2, name="s5_out_bwd")
    dp_s5, dwb_re, dwb_im, dlb_re, dlb_im, dwc_re, dwc_im = _s5_mix_bwd(
        unflat(dymm), unflat(du_skip), unflat(p_s5), x_re, x_im, *s5_w, name="s5_mix_bwd")
    dp_s5 = dp_s5.reshape(n_tok, S5_WIDTH)
    g["c_re"] = _diag_blocks(dwc_re, S5_STATE, S5_GROUP_CH).transpose(0, 2, 1)
    s5_cts = (dlb_re.reshape(lb_re.shape), dlb_im.reshape(lb_im.shape),
              _diag_blocks(dwb_re, S5_GROUP_CH, S5_STATE), _diag_blocks(dwb_im, S5_GROUP_CH, S5_STATE),
              _diag_blocks(dwc_im, S5_STATE, S5_GROUP_CH).transpose(0, 2, 1))
    g["lam_re"], g["lam_im"], g["log_step"], g["bt_re"], g["bt_im"], g["c_im"] = _s5_prep_bwd(
        *s5p_in, s5_cts, name="s5_prep_bwd")

    dog = _mm_act([(dya[None], wts["w_proj_a"])], "nt", name="proj_a_bwd")[0]
    g["w_proj_a"] = _mm_tn(og_t, dya[None], name="dw_proj_a")[0]
    do, dz, g["g_onorm"] = _dn_post_bwd(o, z_tok, wts["g_onorm"], unflat(dog), name="dn_post_bwd")
    scan_cts = _dn_scan_bwd(qd, kd, uc, wc, attn, g_last, states, do, name="dn_scan_bwd")
    dqkv_c, dbl, dlac, dlar, g["a_log"], g["dt_bias"] = _dn_prep_bwd(*dn_in, dn_inv, uc, wc, scan_cts, name="dn_prep_bwd")
    dqkv, g["conv_qkv"] = _conv_bwd(unflat(p_qkv), wts["conv_qkv"], dqkv_c, name="conv_bwd")
    dla = dlac[..., 0] + dlar.reshape(n_b, DN_HEADS, n_s)
    dba = jnp.concatenate([dbl[..., 0].transpose(0, 2, 1), dla.transpose(0, 2, 1),
                           jnp.zeros((n_b, n_s, BA_PAD - 2 * DN_HEADS), F32)], axis=-1).astype(MXU_DTYPE)

    dps = {"w_qkv": flat(dqkv)[None], "w_z": flat(dz)[None], "w_ga": dga[None], "w_gb": dgb[None],
           "w_s5": dp_s5[None], "w_ba": flat(dba)[None]}
    w_ga, w_gb = wts["w_gab"][:, :D_MODEL], wts["w_gab"][:, D_MODEL:]
    w_of = dict(wts, w_ga=w_ga, w_gb=w_gb)
    du = _mm_act([(dps[k], w_of[k]) for k in dps], "nn", name="in_bwd")[0]
    for k in dps:
        g[k] = _mm_tn(dps[k], u, name=f"d{k}")[0]
    dx0_res, df1, dgt1, g["g_mix"], dsh2, dsc2 = _pre_bwd(
        x, unflat(f1), gt1, wts["g_mix"], sh2, sc2, 0.5, unflat(du), dx1_res, name="pre2_bwd")
    da1, g["w1_ffn1"], g["w3_ffn1"], g["w2_ffn1"] = _ffn_bwd(
        flat(a1), wts["w1_ffn1"], wts["w3_ffn1"], wts["w2_ffn1"], ffn1_saved, flat(df1), "ffn1")
    grad_x, g["g_ffn1"], dsh1, dsc1 = _pre_bwd(
        x, None, None, wts["g_ffn1"], sh1, sc1, 0.0, unflat(da1), dx0_res, name="pre1_bwd")

    dmod = jnp.concatenate([t[:, 0, :] for t in (dsh1, dsc1, dgt1, dsh2, dsc2, dgt2, dsh3, dsc3, dgt3)], axis=-1)
    return loss, grad_x, g, dmod


def _ada_grads(c_rows, dmod_rows):
    n_r = c_rows.shape[0]
    n_pad = -n_r % SUBLANES
    c_pad = jnp.pad(c_rows, ((0, n_pad), (0, 0)))
    dmod_s = jnp.pad(dmod_rows, ((0, n_pad), (0, 0))).reshape(n_r + n_pad, N_SHARD, ADA_SHARD).transpose(1, 0, 2)
    dw, db = _ada_bwd(c_pad, dmod_s, name="ada_bwd")
    return dw, db.reshape(1, N_MOD * D_MODEL)


IN_SPLITS = (("w_qkv", 3 * DN_WIDTH), ("w_z", DN_WIDTH), ("w_ba", 2 * DN_HEADS), ("w_s5", S5_WIDTH),
             ("w_ga", D_MODEL), ("w_gb", D_MODEL))
SHARDED = ("w_ada", "w1_ffn1", "w3_ffn1", "w2_ffn1", "w_in", "conv_qkv", "w_glu", "w_proj_a", "w_proj_b", "w_out",
           "w1_ffn2", "w3_ffn2", "w2_ffn2")
COLUMN_SHARDED = ("w_ada", "w1_ffn1", "w3_ffn1", "w_in", "conv_qkv", "w_proj_a", "w_proj_b", "w1_ffn2", "w3_ffn2")


def _cat_columns(stack):
    return stack.transpose(1, 0, 2).reshape(stack.shape[1], N_SHARD * stack.shape[2])


def _split_columns(full):
    n_r, n_c = full.shape
    return full.reshape(n_r, N_SHARD, n_c // N_SHARD).transpose(1, 0, 2)


def _gathered_weights(st, rep):
    w = {k: st[k] for k in ("w_ada", "w1_ffn1", "w3_ffn1", "w2_ffn1", "w1_ffn2", "w3_ffn2", "w2_ffn2")}
    w["b_ada"] = rep["b_ada"].reshape(N_SHARD, 1, ADA_SHARD)
    for k in ("g_ffn1", "g_mix", "g_ffn2", "g_final"):
        w[k] = rep[k].reshape(1, D_MODEL)
    w_in_t = st["w_in"].reshape(N_SHARD * st["w_in"].shape[1], D_MODEL)
    start = 0
    for k, size in IN_SPLITS:
        w[k] = w_in_t[None, start:start + size]
        start += size
    w["w_gab"] = jnp.concatenate([w.pop("w_ga"), w.pop("w_gb")], axis=1)
    w["w_ba"] = jnp.pad(w["w_ba"], ((0, 0), (0, BA_PAD - 2 * DN_HEADS), (0, 0)))
    w["conv_qkv"] = _cat_columns(st["conv_qkv"])
    w["a_log"] = rep["a_log"].reshape(DN_HEADS, 1, 1)
    w["dt_bias"] = rep["dt_bias"].reshape(DN_HEADS, 1, 1)
    w["g_onorm"] = rep["g_onorm"].reshape(1, DN_HEAD_DIM)
    w["lam_re"] = rep["lam_re"].reshape(S5_GROUPS, 1, S5_STATE)
    w["lam_im"] = rep["lam_im"].reshape(S5_GROUPS, 1, S5_STATE)
    w["log_step"] = rep["log_step"].reshape(S5_GROUPS, 1, 1)
    w["bt_re"] = rep["b_re"][0].transpose(0, 2, 1)
    w["bt_im"] = rep["b_im"][0].transpose(0, 2, 1)
    w["c_re"] = rep["c_re"][0]
    w["c_im"] = rep["c_im"][0]
    w["d_skip"] = rep["d_skip"].reshape(1, S5_WIDTH)
    w["b_glu"] = rep["b_glu"].reshape(1, S5_WIDTH)
    w["w_glu"] = st["w_glu"].reshape(S5_WIDTH, S5_WIDTH)
    w["w_proj_a"] = _cat_columns(st["w_proj_a"])[None]
    w["w_proj_b"] = _cat_columns(st["w_proj_b"])[None]
    w["w_out"] = st["w_out"].reshape(1, D_MODEL, D_MODEL)
    return w


def _grads_to_problem_layout(g):
    st = {k: g[k] for k in ("w1_ffn1", "w3_ffn1", "w2_ffn1", "w1_ffn2", "w3_ffn2", "w2_ffn2")}
    w_in_t = jnp.concatenate([g[k][:size] for k, size in IN_SPLITS], axis=0)
    st["w_in"] = w_in_t.reshape(N_SHARD, w_in_t.shape[0] // N_SHARD, D_MODEL)
    st["w_glu"] = g["w_glu"].reshape(N_SHARD, S5_WIDTH // N_SHARD, S5_WIDTH)
    st["w_proj_a"] = _split_columns(g["w_proj_a"])
    st["w_proj_b"] = _split_columns(g["w_proj_b"])
    st["w_out"] = g["w_out"].reshape(N_SHARD, D_MODEL // N_SHARD, D_MODEL)
    small = {
        "g_ffn1": g["g_ffn1"], "g_mix": g["g_mix"], "g_ffn2": g["g_ffn2"], "g_final": g["g_final"].reshape(D_MODEL),
        "conv_qkv": g["conv_qkv"][None],
        "a_log": g["a_log"].reshape(1, DN_HEADS), "dt_bias": g["dt_bias"].reshape(1, DN_HEADS),
        "g_onorm": g["g_onorm"],
        "lam_re": g["lam_re"].reshape(1, S5_GROUPS, S5_STATE), "lam_im": g["lam_im"].reshape(1, S5_GROUPS, S5_STATE),
        "log_step": g["log_step"].reshape(1, S5_GROUPS),
        "b_re": g["bt_re"].transpose(0, 2, 1)[None], "b_im": g["bt_im"].transpose(0, 2, 1)[None],
        "c_re": g["c_re"][None], "c_im": g["c_im"][None],
        "d_skip": g["d_skip"], "b_glu": g["b_glu"],
    }
    return st, small


ELEMENTWISE_BLOCK_BYTES = 1 << 20


def _row_tile(n_rows, n_cols, n_lead=1, multiple=SUBLANES):
    best = None
    for t in range(multiple, n_rows + 1, multiple):
        if n_rows % t == 0 and n_lead * t * n_cols * 4 <= ELEMENTWISE_BLOCK_BYTES:
            best = t
    return best if best is not None else n_rows


def _add_sibling_half(g4, recv, my_c, *, name):
    n_sh, _, n_h, n_c = g4.shape
    th = _row_tile(n_h, n_c, multiple=2 * SUBLANES)

    def body(c_ref, g_ref, r_ref, o_ref):
        o_ref[0] = (g_ref[0, 0] + r_ref[0]).astype(o_ref.dtype)

    grid_spec = pltpu.PrefetchScalarGridSpec(
        num_scalar_prefetch=1, grid=(n_sh, n_h // th),
        in_specs=[pl.BlockSpec((1, 1, th, n_c), lambda s, i, c_ref: (s, c_ref[0], i, 0)),
                  pl.BlockSpec((1, th, n_c), lambda s, i, c_ref: (s, i, 0))],
        out_specs=pl.BlockSpec((1, th, n_c), lambda s, i, c_ref: (s, i, 0)))
    return pl.pallas_call(
        body, name=name, grid_spec=grid_spec, out_shape=jax.ShapeDtypeStruct((n_sh, n_h, n_c), MXU_DTYPE),
        compiler_params=_params(3 * _nbytes((th, n_c), F32)),
    )(*_hbm(my_c, g4, recv))


def _sum_slots(parts, *, name):
    n_p, n_r, n_c = parts.shape
    th = _row_tile(n_r, n_c, n_p)

    def body(p_ref, o_ref):
        total = p_ref[0].astype(F32)
        for k in range(1, n_p):
            total = total + p_ref[k].astype(F32)
        o_ref[...] = total

    return pl.pallas_call(
        body, name=name, grid=(n_r // th,),
        in_specs=[pl.BlockSpec((n_p, th, n_c), lambda i: (0, i, 0))],
        out_specs=pl.BlockSpec((th, n_c), lambda i: (i, 0)),
        out_shape=jax.ShapeDtypeStruct((n_r, n_c), F32),
        compiler_params=_params((n_p + 1) * _nbytes((th, n_c), F32)),
    )(*_hbm(parts))


def _cast_into_slot(w, place, dtype, *, name):
    n_r, n_c = w.shape
    th = _row_tile(n_r, n_c, multiple=2 * SUBLANES)

    def body(p_ref, w_ref, o_ref):
        o_ref[0] = w_ref[...].astype(o_ref.dtype)

    grid_spec = pltpu.PrefetchScalarGridSpec(
        num_scalar_prefetch=1, grid=(n_r // th,),
        in_specs=[pl.BlockSpec((th, n_c), lambda i, p: (i, 0))],
        out_specs=pl.BlockSpec((1, th, n_c), lambda i, p: (p[1], i, 0)))
    return pl.pallas_call(
        body, name=name, grid_spec=grid_spec, out_shape=jax.ShapeDtypeStruct((N_SHARD, n_r, n_c), dtype),
        compiler_params=_params(2 * _nbytes((th, n_c), F32)),
    )(*_hbm(place, w))


def _sum_chips(own, parts, place, *, name):
    n_sh, n_h, n_c = own.shape
    th = _row_tile(n_h, n_c, n_sh, multiple=2 * SUBLANES)

    def body(p_ref, own_ref, a_ref, b_ref, c_ref, o_ref):
        o_ref[0] = ((own_ref[0].astype(F32) + a_ref[0].astype(F32)) + b_ref[0].astype(F32)) + c_ref[0].astype(F32)

    slab = lambda k: pl.BlockSpec((1, th, n_c), lambda i, p, k=k: (p[k], i, 0))
    grid_spec = pltpu.PrefetchScalarGridSpec(
        num_scalar_prefetch=1, grid=(n_h // th,),
        in_specs=[slab(1), slab(2), slab(3), slab(4)], out_specs=slab(0))
    return pl.pallas_call(
        body, name=name, grid_spec=grid_spec, out_shape=jax.ShapeDtypeStruct((2, n_h, n_c), F32),
        compiler_params=_params(5 * _nbytes((th, n_c), F32)),
    )(*_hbm(place, own, parts, parts, parts))


def _adamw(w, g, m, v, *, name):
    n_r, n_c = w.shape
    th = _row_tile(n_r, n_c)
    tc = n_c
    if th == n_r and n_c % LANES == 0:
        tc = max(t for t in range(LANES, n_c + 1, LANES) if n_c % t == 0 and (n_r * t * 4 <= ELEMENTWISE_BLOCK_BYTES or t == LANES))
    bias1 = 1.0 - ADAM_B1 ** ADAM_STEP
    bias2 = 1.0 - ADAM_B2 ** ADAM_STEP

    def body(w_ref, g_ref, m_ref, v_ref, d_ref, mo_ref, vo_ref):
        gv = g_ref[...]
        m_new = ADAM_B1 * m_ref[...] + (1.0 - ADAM_B1) * gv
        v_new = ADAM_B2 * v_ref[...] + (1.0 - ADAM_B2) * jnp.square(gv)
        d_ref[...] = -ADAM_LR * ((m_new / bias1) / (jnp.sqrt(v_new / bias2) + ADAM_EPS) + ADAM_WD * w_ref[...])
        mo_ref[...] = m_new
        vo_ref[...] = v_new

    spec = pl.BlockSpec((th, tc), lambda i, j: (i, j))
    shape = jax.ShapeDtypeStruct((n_r, n_c), F32)
    return pl.pallas_call(
        body, name=name, grid=(n_r // th, n_c // tc), in_specs=[spec] * 4, out_specs=(spec,) * 3, out_shape=(shape,) * 3,
        compiler_params=_params(7 * _nbytes((th, tc), F32)),
    )(*_hbm(w, g, m, v))


CHIP_FLIPS = ((1, 0), (0, 1), (1, 1))
DEVICE_FLIPS = tuple((fx, fy, fc) for fx in (0, 1) for fy in (0, 1) for fc in (0, 1))[1:]


def _exchange(ins, out_shapes, plan, n_local, n_remote, *, name, aliased=False):
    n_in, n_out = len(ins), len(out_shapes)

    def body(*refs):
        in_refs, out_refs = refs[:n_in], refs[n_in:n_in + n_out]
        send_sems, recv_sems, local_sems = refs[n_in + n_out:]
        me = (lax.axis_index("x"), lax.axis_index("y"), lax.axis_index("c"))
        local, remote = plan(in_refs, out_refs, me)
        assert len(local) == n_local and len(remote) == n_remote
        here = [pltpu.make_async_copy(src, dst, local_sems.at[i]) for i, (src, dst) in enumerate(local)]
        for cp in here:
            cp.start()
        sends = [pltpu.make_async_remote_copy(src_ref=src, dst_ref=dst, send_sem=send_sems.at[i], recv_sem=recv_sems.at[i],
                                              device_id=peer, device_id_type=pl.DeviceIdType.MESH)
                 for i, (src, dst, _, peer) in enumerate(remote)]
        for cp in sends:
            cp.start()
        for i, (src, _, landing, peer) in enumerate(remote):
            pltpu.make_async_remote_copy(src_ref=src, dst_ref=landing, send_sem=send_sems.at[i], recv_sem=recv_sems.at[i],
                                         device_id=peer, device_id_type=pl.DeviceIdType.MESH).wait_recv()
        for cp in sends:
            cp.wait_send()
        for cp in here:
            cp.wait()

    any_spec = pl.BlockSpec(memory_space=pl.ANY)
    return pl.pallas_call(
        body, name=name, in_specs=[any_spec] * n_in, out_specs=tuple([any_spec] * n_out), out_shape=tuple(out_shapes),
        scratch_shapes=[pltpu.SemaphoreType.DMA((n_remote,)), pltpu.SemaphoreType.DMA((n_remote,)),
                        pltpu.SemaphoreType.DMA((max(n_local, 1),))],
        input_output_aliases={k: k for k in range(n_in)} if aliased else {},
    )(*ins)


def _gather_shards(stacks, *, name):
    n = len(stacks)
    halved = [a.shape[1] % 32 == 0 for a in stacks]
    n_ici = len(CHIP_FLIPS) * n
    n_pass = len(CHIP_FLIPS) * sum(halved)

    def body(*refs):
        outs = refs[n:2 * n]
        send_sems, recv_sems = refs[2 * n:]
        x, y, c = lax.axis_index("x"), lax.axis_index("y"), lax.axis_index("c")
        mine = 2 * x + y

        def rows(k, slot, half):
            if not halved[k]:
                return outs[k].at[slot]
            n_h = stacks[k].shape[1] // 2
            return outs[k].at[slot, pl.ds(pl.multiple_of(half * n_h, 16), n_h)]

        def copy(i, src, dst, peer):
            return pltpu.make_async_remote_copy(src_ref=src, dst_ref=dst, send_sem=send_sems.at[i], recv_sem=recv_sems.at[i],
                                                device_id=peer, device_id_type=pl.DeviceIdType.MESH)

        started = []
        for j, (fx, fy) in enumerate(CHIP_FLIPS):
            for k in range(n):
                cp = copy(j * n + k, rows(k, mine, c), rows(k, mine, c), (x ^ fx, y ^ fy, c))
                cp.start()
                started.append(cp)
        i_pass = n_ici
        expect = []
        for j, (fx, fy) in enumerate(CHIP_FLIPS):
            peer_chip = 2 * (x ^ fx) + (y ^ fy)
            for k in range(n):
                landed = rows(k, peer_chip, c)
                copy(j * n + k, landed, landed, (x ^ fx, y ^ fy, c)).wait_recv()
                if halved[k]:
                    cp = copy(i_pass, landed, landed, (x, y, 1 - c))
                    cp.start()
                    started.append(cp)
                    expect.append((i_pass, rows(k, peer_chip, 1 - c)))
                    i_pass += 1
        for i, landing in expect:
            copy(i, landing, landing, (x, y, 1 - c)).wait_recv()
        for cp in started:
            cp.wait_send()

    any_spec = pl.BlockSpec(memory_space=pl.ANY)
    n_sem = n_ici + n_pass
    return pl.pallas_call(
        body, name=name, in_specs=[any_spec] * n, out_specs=tuple([any_spec] * n),
        out_shape=tuple(jax.ShapeDtypeStruct(a.shape, a.dtype) for a in stacks),
        scratch_shapes=[pltpu.SemaphoreType.DMA((n_sem,)), pltpu.SemaphoreType.DMA((n_sem,))],
        input_output_aliases={k: k for k in range(n)},
    )(*stacks)


def _swap_sibling_halves(g4s, *, name):
    n = len(g4s)

    def plan(in_refs, out_refs, me):
        x, y, c = me
        remote = [(in_refs[k].at[:, 1 - c], out_refs[k], out_refs[k], (x, y, 1 - c)) for k in range(n)]
        return [], remote

    shapes = [jax.ShapeDtypeStruct((a.shape[0],) + a.shape[2:], a.dtype) for a in g4s]
    return _exchange(g4s, shapes, plan, 0, n, name=name)


def _scatter_to_chips(hs, *, name):
    n = len(hs)

    def plan(in_refs, out_refs, me):
        x, y, c = me
        mine = 2 * x + y
        remote = []
        for fx, fy in CHIP_FLIPS:
            px, py = x ^ fx, y ^ fy
            peer = 2 * px + py
            for k in range(n):
                remote.append((in_refs[k].at[peer], out_refs[k].at[mine], out_refs[k].at[peer], (px, py, c)))
        return [], remote

    shapes = [jax.ShapeDtypeStruct(a.shape, a.dtype) for a in hs]
    return _exchange(hs, shapes, plan, 0, len(CHIP_FLIPS) * n, name=name)


def _join_sibling_halves(rs, *, name):
    n = len(rs)

    def plan(in_refs, out_refs, me):
        x, y, c = me
        remote = [(out_refs[k].at[c], out_refs[k].at[c], out_refs[k].at[1 - c], (x, y, 1 - c)) for k in range(n)]
        return [], remote

    shapes = [jax.ShapeDtypeStruct(a.shape, a.dtype) for a in rs]
    return _exchange(rs, shapes, plan, 0, n, name=name, aliased=True)


def _gather_all_devices(packed, *, name):
    def plan(in_refs, out_refs, me):
        x, y, c = me
        mine = 4 * x + 2 * y + c
        remote = []
        for fx, fy, fc in DEVICE_FLIPS:
            px, py, pc = x ^ fx, y ^ fy, c ^ fc
            remote.append((in_refs[0], out_refs[0].at[mine], out_refs[0].at[4 * px + 2 * py + pc], (px, py, pc)))
        return [(in_refs[0], out_refs[0].at[mine])], remote

    shape = jax.ShapeDtypeStruct((2 * N_SHARD,) + packed.shape, packed.dtype)
    return _exchange([packed], [shape], plan, 1, len(DEVICE_FLIPS), name=name)[0]


WEIGHT_NAMES = ("w_ada", "b_ada", "g_ffn1", "w1_ffn1", "w3_ffn1", "w2_ffn1", "g_mix", "w_in", "conv_qkv", "a_log",
                "dt_bias", "g_onorm", "lam_re", "lam_im", "log_step", "b_re", "b_im", "c_re", "c_im", "d_skip", "w_glu",
                "b_glu", "w_proj_a", "w_proj_b", "w_out", "g_ffn2", "w1_ffn2", "w3_ffn2", "w2_ffn2", "g_final")
LARGE = tuple(n for n in SHARDED if n != "conv_qkv")
SMALL = tuple(n for n in WEIGHT_NAMES if n not in LARGE)
REDUCED_LARGE = tuple(n for n in LARGE if n != "w_ada")
REDUCED_SMALL = tuple(n for n in SMALL if n != "b_ada")
PACK_ROW = SUBLANES * LANES


def _pack(arrays):
    flat = jnp.concatenate([a.reshape(-1) for a in arrays])
    n_pad = -flat.shape[0] % PACK_ROW
    return jnp.pad(flat, (0, n_pad)).reshape(-1, LANES)


def _unpack(packed, shapes):
    flat = packed.reshape(-1)
    out, start = [], 0
    for s in shapes:
        size = math.prod(s)
        out.append(flat[start:start + size].reshape(s))
        start += size
    return out


def _unpack_slots(gathered, shapes):
    flat = gathered.reshape(gathered.shape[0], -1)
    out, start = [], 0
    for s in shapes:
        size = math.prod(s)
        out.append(flat[:, start:start + size].reshape((gathered.shape[0],) + tuple(s)))
        start += size
    return out


TRANSPOSED = ("w1_ffn1", "w3_ffn1", "w1_ffn2", "w3_ffn2", "w_in")


def _to_internal(name, a):
    return jnp.swapaxes(a[0], 0, 1) if name in TRANSPOSED else a[0]


def _from_internal(name, a):
    return (jnp.swapaxes(a, 0, 1) if name in TRANSPOSED else a)[None]


def _step(x, c, target, weights, m_in, v_in):
    xi, yi, ci = lax.axis_index("x"), lax.axis_index("y"), lax.axis_index("c")
    my_chip = 2 * xi + yi

    others = [k + (k >= my_chip).astype(jnp.int32) for k in range(N_SHARD - 1)]
    place = jnp.stack([ci, my_chip] + others).astype(jnp.int32)

    slots = [_cast_into_slot(_to_internal(n, weights[n]), place, F32 if n == "conv_qkv" else MXU_DTYPE, name=f"cast_{n}")
             for n in SHARDED]
    stacks = dict(zip(SHARDED, _gather_shards(slots, name="gather_weights")))
    rep = {n: weights[n] for n in WEIGHT_NAMES if n not in SHARDED}
    loss, grad_x, g, dmod = _local_step(x, c, target, _gathered_weights(stacks, rep))
    g_stacks, g_small = _grads_to_problem_layout(g)

    g4s = [g_stacks[n].reshape(N_SHARD, 2, g_stacks[n].shape[1] // 2, g_stacks[n].shape[2]) for n in REDUCED_LARGE]
    from_sibling = _swap_sibling_halves(g4s, name="swap_sibling_halves")
    chip_sums = [_add_sibling_half(a, r, place, name=f"chip_sum_{n}") for n, a, r in zip(REDUCED_LARGE, g4s, from_sibling)]
    from_chips = _scatter_to_chips(chip_sums, name="scatter_to_chips")
    reduced = [_sum_chips(h, p, place, name=f"sum_chips_{n}") for n, h, p in zip(REDUCED_LARGE, chip_sums, from_chips)]
    joined = _join_sibling_halves(reduced, name="join_sibling_halves")
    grads_2d = {n: j.reshape(2 * j.shape[1], j.shape[2]) for n, j in zip(REDUCED_LARGE, joined)}
    grads = {n: _from_internal(n, a) for n, a in grads_2d.items()}

    summed_shapes = [g_small[n].shape for n in REDUCED_SMALL] + [(1, 1)]
    packed = _pack([g_small[n] for n in REDUCED_SMALL] + [loss, c, dmod])
    gathered = _gather_all_devices(packed, name="gather_small")
    *small_grads, loss_sum = _unpack(_sum_slots(gathered, name="sum_small"), summed_shapes)
    grads.update(zip(REDUCED_SMALL, small_grads))
    n_conv = weights["conv_qkv"].shape[-1]
    grads["conv_qkv"] = lax.dynamic_slice_in_dim(grads["conv_qkv"], my_chip * n_conv, n_conv, axis=2)
    n_dev = gathered.shape[0]
    rows_of = lambda t: t.reshape(n_dev * t.shape[1], t.shape[2])
    _, c_all, dmod_all = _unpack_slots(gathered, [(sum(math.prod(s) for s in summed_shapes),), c.shape, dmod.shape])
    dw_ada, grads["b_ada"] = _ada_grads(rows_of(c_all), rows_of(dmod_all))
    grads_2d["w_ada"] = lax.dynamic_index_in_dim(dw_ada, my_chip, axis=0, keepdims=False)
    grads["w_ada"] = grads_2d["w_ada"][None]

    delta, new_m, new_v = {}, {}, {}
    grads_2d["conv_qkv"] = grads["conv_qkv"][0]
    for n in LARGE + ("conv_qkv",):
        outs = _adamw(_to_internal(n, weights[n]), grads_2d[n], _to_internal(n, m_in[n]), _to_internal(n, v_in[n]),
                      name=f"adamw_{n}")
        delta[n], new_m[n], new_v[n] = [_from_internal(n, o) for o in outs]
    packed_names = tuple(n for n in SMALL if n != "conv_qkv")
    shapes = [weights[n].shape for n in packed_names]
    outs = _adamw(*[_pack([d[n] for n in packed_names]) for d in (weights, grads, m_in, v_in)], name="adamw_small")
    for d, o in zip((delta, new_m, new_v), outs):
        d.update(zip(packed_names, _unpack(o, shapes)))
    return (loss_sum.reshape(()), grad_x, *[grads[n] for n in WEIGHT_NAMES], *[delta[n] for n in WEIGHT_NAMES],
            *[new_m[n] for n in WEIGHT_NAMES], *[new_v[n] for n in WEIGHT_NAMES])


def kernel(x, c, w_ada, b_ada, g_ffn1, w1_ffn1, w3_ffn1, w2_ffn1, g_mix, w_in, conv_qkv, a_log, dt_bias, g_onorm, lam_re, lam_im, log_step, b_re, b_im, c_re, c_im, d_skip, w_glu, b_glu, w_proj_a, w_proj_b, w_out, g_ffn2, w1_ffn2, w3_ffn2, w2_ffn2, g_final, loss_target, m_w_ada, m_b_ada, m_g_ffn1, m_w1_ffn1, m_w3_ffn1, m_w2_ffn1, m_g_mix, m_w_in, m_conv_qkv, m_a_log, m_dt_bias, m_g_onorm, m_lam_re, m_lam_im, m_log_step, m_b_re, m_b_im, m_c_re, m_c_im, m_d_skip, m_w_glu, m_b_glu, m_w_proj_a, m_w_proj_b, m_w_out, m_g_ffn2, m_w1_ffn2, m_w3_ffn2, m_w2_ffn2, m_g_final, v_w_ada, v_b_ada, v_g_ffn1, v_w1_ffn1, v_w3_ffn1, v_w2_ffn1, v_g_mix, v_w_in, v_conv_qkv, v_a_log, v_dt_bias, v_g_onorm, v_lam_re, v_lam_im, v_log_step, v_b_re, v_b_im, v_c_re, v_c_im, v_d_skip, v_w_glu, v_b_glu, v_w_proj_a, v_w_proj_b, v_w_out, v_g_ffn2, v_w1_ffn2, v_w3_ffn2, v_w2_ffn2, v_g_final):
    w_vals = (w_ada, b_ada, g_ffn1, w1_ffn1, w3_ffn1, w2_ffn1, g_mix, w_in, conv_qkv, a_log, dt_bias, g_onorm, lam_re, lam_im, log_step, b_re, b_im, c_re, c_im, d_skip, w_glu, b_glu, w_proj_a, w_proj_b, w_out, g_ffn2, w1_ffn2, w3_ffn2, w2_ffn2, g_final)
    m_vals = (m_w_ada, m_b_ada, m_g_ffn1, m_w1_ffn1, m_w3_ffn1, m_w2_ffn1, m_g_mix, m_w_in, m_conv_qkv, m_a_log, m_dt_bias, m_g_onorm, m_lam_re, m_lam_im, m_log_step, m_b_re, m_b_im, m_c_re, m_c_im, m_d_skip, m_w_glu, m_b_glu, m_w_proj_a, m_w_proj_b, m_w_out, m_g_ffn2, m_w1_ffn2, m_w3_ffn2, m_w2_ffn2, m_g_final)
    v_vals = (v_w_ada, v_b_ada, v_g_ffn1, v_w1_ffn1, v_w3_ffn1, v_w2_ffn1, v_g_mix, v_w_in, v_conv_qkv, v_a_log, v_dt_bias, v_g_onorm, v_lam_re, v_lam_im, v_log_step, v_b_re, v_b_im, v_c_re, v_c_im, v_d_skip, v_w_glu, v_b_glu, v_w_proj_a, v_w_proj_b, v_w_out, v_g_ffn2, v_w1_ffn2, v_w3_ffn2, v_w2_ffn2, v_g_final)
    return _step(x, c, loss_target, dict(zip(WEIGHT_NAMES, w_vals)), dict(zip(WEIGHT_NAMES, m_vals)),
                 dict(zip(WEIGHT_NAMES, v_vals)))
```

```python
import functools
import math

import jax
import jax.numpy as jnp
from jax import lax
from jax.experimental import pallas as pl
from jax.experimental.pallas import tpu as pltpu

F32 = jnp.float32
BF16 = jnp.bfloat16
MXU_DTYPE = BF16

D_MODEL = 1024
D_FF = 2816
DN_HEADS = 8
DN_HEAD_DIM = 64
DN_WIDTH = DN_HEADS * DN_HEAD_DIM
CONV_WIDTH = 4
CHUNK = 64
S5_GROUP_CH = 16
S5_GROUPS = 32
S5_WIDTH = S5_GROUPS * S5_GROUP_CH
S5_STATE = 64
S5_LANES = S5_GROUPS * S5_STATE
N_MOD = 9
EPS = 1e-6
N_SHARD = 4
FF_SHARD = D_FF // N_SHARD
BA_PAD = 128

ADAM_LR = 0.001
ADAM_B1 = 0.9
ADAM_B2 = 0.999
ADAM_EPS = 1e-08
ADAM_WD = 0.01
ADAM_STEP = 10

VMEM_BYTES_V7X = 64 * 1024 * 1024
SUBLANES = 8
LANES = 128


def _params(block_bytes, extra_bytes=0):
    need = 2 * block_bytes + extra_bytes + (4 << 20)
    return pltpu.CompilerParams(vmem_limit_bytes=int(min(max(need, 16 << 20), VMEM_BYTES_V7X - (8 << 20))))


def _nbytes(shape, dtype):
    return math.prod(shape) * jnp.dtype(dtype).itemsize


HBM_OPERAND_BYTES = 1 << 20


def _hbm(*args):
    return [pltpu.with_memory_space_constraint(a, pltpu.HBM) if _nbytes(a.shape, a.dtype) >= HBM_OPERAND_BYTES else a
            for a in args]


_NN = (((1,), (0,)), ((), ()))
_NT = (((1,), (1,)), ((), ()))
_TN = (((0,), (0,)), ((), ()))


LHS_ROW_BYTES = 4096


def _mm_act(pairs, mode, *, name, out_sharded=False, reduce_shards=False, out_dtype=F32, add=None, tm=None):
    n_tok = pairs[0][0].shape[1]
    n_out = pairs[0][1].shape[2] if mode == "nn" else pairs[0][1].shape[1]
    if tm is None:
        row_bytes = sum(a.shape[2] * jnp.dtype(a.dtype).itemsize for a, _ in pairs)
        tm = 1024 if row_bytes <= LHS_ROW_BYTES else 512
    tm = min(tm, n_tok)
    tn = n_out if n_out <= 1536 else 1024
    assert n_tok % tm == 0 and n_out % tn == 0
    n_so = N_SHARD if out_sharded else 1
    n_red = N_SHARD if reduce_shards else 1
    grid = (n_so, n_tok // tm, n_out // tn, n_red)
    dims = _NN if mode == "nn" else _NT

    def shard_of(n_sh):
        if n_sh == 1:
            return lambda s, r: 0
        return (lambda s, r: s) if out_sharded else (lambda s, r: r)

    in_specs, args, blk = [], [], 0
    for a, b in pairs:
        k_dim = a.shape[2]
        sa, sb = shard_of(a.shape[0]), shard_of(b.shape[0])
        in_specs.append(pl.BlockSpec((1, tm, k_dim), lambda s, i, j, r, sa=sa: (sa(s, r), i, 0)))
        if mode == "nn":
            assert b.shape[1] == k_dim
            in_specs.append(pl.BlockSpec((1, k_dim, tn), lambda s, i, j, r, sb=sb: (sb(s, r), 0, j)))
        else:
            assert b.shape[2] == k_dim
            in_specs.append(pl.BlockSpec((1, tn, k_dim), lambda s, i, j, r, sb=sb: (sb(s, r), j, 0)))
        args += [a, b]
        blk += _nbytes((tm, k_dim), a.dtype) + _nbytes((k_dim, tn), b.dtype)
    if add is not None:
        in_specs.append(pl.BlockSpec((1, tm, tn), lambda s, i, j, r: (s, i, j)))
        args.append(add)
        blk += _nbytes((tm, tn), F32)
    blk += _nbytes((tm, tn), out_dtype)
    n_pairs = len(pairs)

    def body(*refs):
        out_ref = refs[2 * n_pairs + (add is not None)]
        acc = None
        for k in range(n_pairs):
            a = refs[2 * k][0].astype(MXU_DTYPE)
            b = refs[2 * k + 1][0].astype(MXU_DTYPE)
            d = lax.dot_general(a, b, dims, preferred_element_type=F32)
            acc = d if acc is None else acc + d

        def finish(total):
            if add is not None:
                total = total + refs[2 * n_pairs][0]
            out_ref[0] = total.astype(out_dtype)

        if n_red == 1:
            finish(acc)
        else:
            acc_ref = refs[-1]
            r = pl.program_id(3)

            @pl.when(r == 0)
            def _():
                acc_ref[...] = acc

            @pl.when(r > 0)
            def _():
                acc_ref[...] += acc

            @pl.when(r == n_red - 1)
            def _():
                finish(acc_ref[...])

    return pl.pallas_call(
        body,
        name=name,
        grid=grid,
        in_specs=in_specs,
        out_specs=pl.BlockSpec((1, tm, tn), lambda s, i, j, r: (s, i, j)),
        out_shape=jax.ShapeDtypeStruct((n_so, n_tok, n_out), out_dtype),
        scratch_shapes=[pltpu.VMEM((tm, tn), F32)] if n_red > 1 else [],
        compiler_params=_params(blk, 3 * _nbytes((tm, tn), F32)),
    )(*_hbm(*args))


def _mm_tn(a, b, *, name, tt=1024):
    n_tok, k_dim = a.shape[1], a.shape[2]
    n_out = b.shape[2]
    tt = min(tt, n_tok)
    tk = k_dim if k_dim <= 1536 else 1024
    tn = n_out if n_out <= 1536 else 1024
    assert n_tok % tt == 0 and k_dim % tk == 0 and n_out % tn == 0
    n_so = max(a.shape[0], b.shape[0])
    sa = (lambda s: s) if a.shape[0] > 1 else (lambda s: 0)
    sb = (lambda s: s) if b.shape[0] > 1 else (lambda s: 0)
    grid = (n_so, k_dim // tk, n_out // tn, n_tok // tt)

    def body(a_ref, b_ref, out_ref):
        d = lax.dot_general(a_ref[0].astype(MXU_DTYPE), b_ref[0].astype(MXU_DTYPE), _TN, preferred_element_type=F32)
        t = pl.program_id(3)

        @pl.when(t == 0)
        def _():
            out_ref[0] = d

        @pl.when(t > 0)
        def _():
            out_ref[0] += d

    blk = _nbytes((tt, tk), a.dtype) + _nbytes((tt, tn), b.dtype) + _nbytes((tk, tn), F32)
    return pl.pallas_call(
        body,
        name=name,
        grid=grid,
        in_specs=[
            pl.BlockSpec((1, tt, tk), lambda s, ki, nj, t: (sa(s), t, ki)),
            pl.BlockSpec((1, tt, tn), lambda s, ki, nj, t: (sb(s), t, nj)),
        ],
        out_specs=pl.BlockSpec((1, tk, tn), lambda s, ki, nj, t: (s, ki, nj)),
        out_shape=jax.ShapeDtypeStruct((n_so, k_dim, n_out), F32),
        compiler_params=_params(blk, 2 * _nbytes((tk, tn), F32) + _nbytes((tt, tk), F32)),
    )(*_hbm(a, b))


@functools.partial(jax.custom_vjp, nondiff_argnums=(2,))
def _mdot(a, b, dims):
    return lax.dot_general(a.astype(MXU_DTYPE), b.astype(MXU_DTYPE), dims, preferred_element_type=F32)


def _mdot_fwd(a, b, dims):
    return _mdot(a, b, dims), (a, b)


def _mdot_bwd(dims, res, g):
    a, b = res
    (ca, cb), (ba, bb) = dims
    nb = len(ba)
    assert tuple(ba) == tuple(range(nb)) and tuple(bb) == tuple(range(nb)) and len(ca) == 1 and a.ndim == nb + 2
    batch = (tuple(range(nb)), tuple(range(nb)))
    ra, rb = nb, nb + 1
    a_free = (set(range(nb, nb + 2)) - set(ca)).pop()
    b_free = (set(range(nb, nb + 2)) - set(cb)).pop()
    if a_free < ca[0]:
        da = _mdot(g, b, (((rb,), (b_free,)), batch))
    else:
        da = _mdot(b, g, (((b_free,), (rb,)), batch))
    if b_free > cb[0]:
        db = _mdot(a, g, (((a_free,), (ra,)), batch))
    else:
        db = _mdot(g, a, (((ra,), (a_free,)), batch))
    return da.astype(a.dtype), db.astype(b.dtype)


_mdot.defvjp(_mdot_fwd, _mdot_bwd)


def _rms(x, gain):
    return x * lax.rsqrt(jnp.mean(x * x, axis=-1, keepdims=True) + EPS) * gain


def _pre_fn(coef, x_in, f, gate, gain, shift, scale):
    x_new = x_in if f is None else x_in + coef * gate * f
    return x_new, _rms(x_new, gain) * (1.0 + scale) + shift


def _row_spec(ts):
    return pl.BlockSpec((1, ts, D_MODEL), lambda b, j: (b, j, 0))


_BATCH_VEC = pl.BlockSpec((1, 1, D_MODEL), lambda b, j: (b, 0, 0))
_ONE_VEC = pl.BlockSpec((1, D_MODEL), lambda b, j: (0, 0))


def _pre(x_in, f, gate, gain, shift, scale, coef, *, name, ts=512):
    n_b, n_s, _ = x_in.shape
    ts = min(ts, n_s)
    has_res = f is not None

    def body(*refs):
        if has_res:
            x_ref, f_ref, gate_ref, gain_ref, sh_ref, sc_ref, xn_ref, a_ref = refs
            x_new, a = _pre_fn(coef, x_ref[0], f_ref[0], gate_ref[0], gain_ref[...], sh_ref[0], sc_ref[0])
            xn_ref[0] = x_new
        else:
            x_ref, gain_ref, sh_ref, sc_ref, a_ref = refs
            _, a = _pre_fn(coef, x_ref[0], None, None, gain_ref[...], sh_ref[0], sc_ref[0])
        a_ref[0] = a.astype(a_ref.dtype)

    row = _row_spec(ts)
    if has_res:
        args = (x_in, f, gate, gain, shift, scale)
        in_specs = [row, row, _BATCH_VEC, _ONE_VEC, _BATCH_VEC, _BATCH_VEC]
        out_specs = (row, row)
        out_shape = (jax.ShapeDtypeStruct(x_in.shape, F32), jax.ShapeDtypeStruct(x_in.shape, MXU_DTYPE))
    else:
        args = (x_in, gain, shift, scale)
        in_specs = [row, _ONE_VEC, _BATCH_VEC, _BATCH_VEC]
        out_specs = row
        out_shape = jax.ShapeDtypeStruct(x_in.shape, MXU_DTYPE)
    return pl.pallas_call(
        body, name=name, grid=(n_b, n_s // ts), in_specs=in_specs, out_specs=out_specs, out_shape=out_shape,
        compiler_params=_params(5 * _nbytes((ts, D_MODEL), F32), 4 * _nbytes((ts, D_MODEL), F32)),
    )(*_hbm(*args))


def _accumulate(ref, value, first):
    @pl.when(first)
    def _():
        ref[...] = value

    @pl.when(jnp.logical_not(first))
    def _():
        ref[...] += value


def _pre_bwd(x_in, f, gate, gain, shift, scale, coef, da, dx_up, *, name, ts=512):
    n_b, n_s, _ = x_in.shape
    ts = min(ts, n_s)
    has_res = f is not None
    has_up = dx_up is not None

    def body(*refs):
        refs = list(refs)
        x_ref = refs.pop(0)
        f_ref, gate_ref = (refs.pop(0), refs.pop(0)) if has_res else (None, None)
        gain_ref, sh_ref, sc_ref, da_ref = refs.pop(0), refs.pop(0), refs.pop(0), refs.pop(0)
        up_ref = refs.pop(0) if has_up else None
        dx_ref = refs.pop(0)
        df_ref, dgate_ref = (refs.pop(0), refs.pop(0)) if has_res else (None, None)
        dgain_ref, dsh_ref, dsc_ref = refs
        b, j = pl.program_id(0), pl.program_id(1)
        da_v = da_ref[0].astype(F32)
        up_v = up_ref[0] if has_up else jnp.zeros((ts, D_MODEL), F32)
        if has_res:
            fn = functools.partial(_pre_fn, coef)
            _, pull = jax.vjp(fn, x_ref[0], f_ref[0], gate_ref[0], gain_ref[...], sh_ref[0], sc_ref[0])
            dx, df, dgate, dgain, dsh, dsc = pull((up_v, da_v))
            df_ref[0] = df.astype(df_ref.dtype)
            _accumulate(dgate_ref, dgate[None], j == 0)
        else:
            fn = lambda x, g, sh, sc: _pre_fn(coef, x, None, None, g, sh, sc)
            _, pull = jax.vjp(fn, x_ref[0], gain_ref[...], sh_ref[0], sc_ref[0])
            dx, dgain, dsh, dsc = pull((up_v, da_v))
        dx_ref[0] = dx
        _accumulate(dgain_ref, dgain, jnp.logical_and(b == 0, j == 0))
        _accumulate(dsh_ref, dsh[None], j == 0)
        _accumulate(dsc_ref, dsc[None], j == 0)

    row = _row_spec(ts)
    args, in_specs = [x_in], [row]
    if has_res:
        args += [f, gate]
        in_specs += [row, _BATCH_VEC]
    args += [gain, shift, scale, da]
    in_specs += [_ONE_VEC, _BATCH_VEC, _BATCH_VEC, row]
    if has_up:
        args.append(dx_up)
        in_specs.append(row)
    vec = jax.ShapeDtypeStruct((n_b, 1, D_MODEL), F32)
    out_shape, out_specs = [jax.ShapeDtypeStruct(x_in.shape, F32)], [row]
    if has_res:
        out_shape += [jax.ShapeDtypeStruct(x_in.shape, MXU_DTYPE), vec]
        out_specs += [row, _BATCH_VEC]
    out_shape += [jax.ShapeDtypeStruct((1, D_MODEL), F32), vec, vec]
    out_specs += [_ONE_VEC, _BATCH_VEC, _BATCH_VEC]
    return pl.pallas_call(
        body, name=name, grid=(n_b, n_s // ts), in_specs=in_specs, out_specs=tuple(out_specs), out_shape=tuple(out_shape),
        compiler_params=_params(6 * _nbytes((ts, D_MODEL), F32), 8 * _nbytes((ts, D_MODEL), F32)),
    )(*_hbm(*args))


def _final_fn(x_in, f, gate, gain, target):
    x_new = x_in + 0.5 * gate * f
    err = jnp.square(_rms(x_new, gain) - target)
    return 0.5 * jnp.sum(jnp.mean(err, axis=-1))


def _final(x_in, f, gate, gain, target, *, name, ts=512):
    n_b, n_s, _ = x_in.shape
    ts = min(ts, n_s)

    def body(x_ref, f_ref, gate_ref, gain_ref, t_ref, loss_ref, dx_ref, df_ref, dgate_ref, dgain_ref):
        b, j = pl.program_id(0), pl.program_id(1)
        loss, (dx, df, dgate, dgain) = jax.value_and_grad(_final_fn, argnums=(0, 1, 2, 3))(
            x_ref[0], f_ref[0], gate_ref[0], gain_ref[...], t_ref[0])
        first = jnp.logical_and(b == 0, j == 0)
        _accumulate(loss_ref, jnp.reshape(loss, (1, 1)), first)
        dx_ref[0] = dx
        df_ref[0] = df.astype(df_ref.dtype)
        _accumulate(dgate_ref, dgate[None], j == 0)
        _accumulate(dgain_ref, dgain, first)

    row = _row_spec(ts)
    return pl.pallas_call(
        body, name=name, grid=(n_b, n_s // ts),
        in_specs=[row, row, _BATCH_VEC, _ONE_VEC, row],
        out_specs=(pl.BlockSpec((1, 1), lambda b, j: (0, 0)), row, row, _BATCH_VEC, _ONE_VEC),
        out_shape=(jax.ShapeDtypeStruct((1, 1), F32), jax.ShapeDtypeStruct(x_in.shape, F32),
                   jax.ShapeDtypeStruct(x_in.shape, MXU_DTYPE), jax.ShapeDtypeStruct((n_b, 1, D_MODEL), F32),
                   jax.ShapeDtypeStruct((1, D_MODEL), F32)),
        compiler_params=_params(5 * _nbytes((ts, D_MODEL), F32), 8 * _nbytes((ts, D_MODEL), F32)),
    )(*_hbm(x_in, f, gate, gain, target))


FFN_TOKENS = 1024


def _ffn_up(a, w1s, w3s, *, name, tm=FFN_TOKENS):
    n_tok = a.shape[0]
    tm = min(tm, n_tok)

    def body(a_ref, w1_ref, w3_ref, h1_ref, h3_ref, g_ref):
        av = a_ref[...].astype(MXU_DTYPE)
        h1 = lax.dot_general(av, w1_ref[0].astype(MXU_DTYPE), _NT, preferred_element_type=F32)
        h3 = lax.dot_general(av, w3_ref[0].astype(MXU_DTYPE), _NT, preferred_element_type=F32)
        h1_ref[0] = h1.astype(h1_ref.dtype)
        h3_ref[0] = h3.astype(h3_ref.dtype)
        g_ref[0] = (jax.nn.silu(h1) * h3).astype(g_ref.dtype)

    w_spec = pl.BlockSpec((1, FF_SHARD, D_MODEL), lambda s, i: (s, 0, 0))
    h_spec = pl.BlockSpec((1, tm, FF_SHARD), lambda s, i: (s, i, 0))
    h_shape = jax.ShapeDtypeStruct((N_SHARD, n_tok, FF_SHARD), MXU_DTYPE)
    blk = _nbytes((tm, D_MODEL), a.dtype) + 2 * _nbytes((D_MODEL, FF_SHARD), w1s.dtype) + 3 * _nbytes((tm, FF_SHARD), MXU_DTYPE)
    return pl.pallas_call(
        body, name=name, grid=(N_SHARD, n_tok // tm),
        in_specs=[pl.BlockSpec((tm, D_MODEL), lambda s, i: (i, 0)), w_spec, w_spec],
        out_specs=(h_spec, h_spec, h_spec), out_shape=(h_shape, h_shape, h_shape),
        compiler_params=_params(blk, 6 * _nbytes((tm, FF_SHARD), F32)),
    )(*_hbm(a, w1s, w3s))


def _ffn_down_bwd(df, w2s, h1, h3, *, name, tm=FFN_TOKENS):
    n_tok = df.shape[0]
    tm = min(tm, n_tok)

    def body(df_ref, w2_ref, h1_ref, h3_ref, dh1_ref, dh3_ref):
        dg = lax.dot_general(df_ref[...].astype(MXU_DTYPE), w2_ref[0].astype(MXU_DTYPE), _NT, preferred_element_type=F32)
        h1v = h1_ref[0].astype(F32)
        h3v = h3_ref[0].astype(F32)
        sig = jax.nn.sigmoid(h1v)
        dh3_ref[0] = (dg * (h1v * sig)).astype(dh3_ref.dtype)
        dh1_ref[0] = (dg * h3v * (sig * (1.0 + h1v * (1.0 - sig)))).astype(dh1_ref.dtype)

    h_spec = pl.BlockSpec((1, tm, FF_SHARD), lambda s, i: (s, i, 0))
    h_shape = jax.ShapeDtypeStruct((N_SHARD, n_tok, FF_SHARD), MXU_DTYPE)
    blk = _nbytes((tm, D_MODEL), df.dtype) + _nbytes((FF_SHARD, D_MODEL), w2s.dtype) + 4 * _nbytes((tm, FF_SHARD), MXU_DTYPE)
    return pl.pallas_call(
        body, name=name, grid=(N_SHARD, n_tok // tm),
        in_specs=[pl.BlockSpec((tm, D_MODEL), lambda s, i: (i, 0)),
                  pl.BlockSpec((1, FF_SHARD, D_MODEL), lambda s, i: (s, 0, 0)), h_spec, h_spec],
        out_specs=(h_spec, h_spec), out_shape=(h_shape, h_shape),
        compiler_params=_params(blk, 8 * _nbytes((tm, FF_SHARD), F32)),
    )(*_hbm(df, w2s, h1, h3))


def _ffn_fwd(a, w1s, w3s, w2s, tag):
    h1, h3, g = _ffn_up(a, w1s, w3s, name=f"{tag}_up")
    f = _mm_act([(g, w2s)], "nn", reduce_shards=True, tm=FFN_TOKENS, name=f"{tag}_down")[0]
    return f, (h1, h3, g)


def _ffn_bwd(a, w1s, w3s, w2s, saved, df, tag):
    h1, h3, g = saved
    dh1, dh3 = _ffn_down_bwd(df, w2s, h1, h3, name=f"{tag}_down_bwd")
    da = _mm_act([(dh1, w1s), (dh3, w3s)], "nn", reduce_shards=True, tm=FFN_TOKENS, name=f"{tag}_up_bwd")[0]
    a3 = a[None]
    dw1 = _mm_tn(dh1, a3, tt=FFN_TOKENS, name=f"{tag}_dw1")
    dw3 = _mm_tn(dh3, a3, tt=FFN_TOKENS, name=f"{tag}_dw3")
    dw2 = _mm_tn(g, df[None], tt=FFN_TOKENS, name=f"{tag}_dw2")
    return da, dw1, dw3, dw2


CONV_LANES = 256


def _shift_down(x, d):
    if d == 0:
        return x
    row = lax.broadcasted_iota(jnp.int32, x.shape, 0)
    return jnp.where(row >= d, pltpu.roll(x, d, 0), 0.0)


def _shift_up(x, d):
    if d == 0:
        return x
    n = x.shape[0]
    row = lax.broadcasted_iota(jnp.int32, x.shape, 0)
    return jnp.where(row < n - d, pltpu.roll(x, n - d, 0), 0.0)


def _conv_pre(x, w):
    acc = None
    for j in range(CONV_WIDTH):
        term = w[j:j + 1, :] * _shift_down(x, CONV_WIDTH - 1 - j)
        acc = term if acc is None else acc + term
    return acc


def _conv_fwd(x, w, *, name):
    n_b, n_s, n_c = x.shape
    spec = pl.BlockSpec((1, n_s, CONV_LANES), lambda b, cj: (b, 0, cj))

    def body(x_ref, w_ref, o_ref):
        o_ref[0] = jax.nn.silu(_conv_pre(x_ref[0], w_ref[...]))

    return pl.pallas_call(
        body, name=name, grid=(n_b, n_c // CONV_LANES),
        in_specs=[spec, pl.BlockSpec((CONV_WIDTH, CONV_LANES), lambda b, cj: (0, cj))],
        out_specs=spec, out_shape=jax.ShapeDtypeStruct(x.shape, F32),
        compiler_params=_params(2 * _nbytes((n_s, CONV_LANES), F32), 6 * _nbytes((n_s, CONV_LANES), F32)),
    )(*_hbm(x, w))


def _conv_bwd(x, w, dout, *, name):
    n_b, n_s, n_c = x.shape
    per_part = DN_WIDTH // CONV_LANES
    spec = pl.BlockSpec((1, n_s, CONV_LANES), lambda cj, b: (b, 0, cj))
    do_spec = pl.BlockSpec((1, 1, n_s, CONV_LANES), lambda cj, b: (cj // per_part, b, 0, cj % per_part))
    w_spec = pl.BlockSpec((CONV_WIDTH, CONV_LANES), lambda cj, b: (0, cj))

    def body(x_ref, w_ref, do_ref, dx_ref, dw_ref):
        xv, wv = x_ref[0], w_ref[...]
        pre = _conv_pre(xv, wv)
        sig = jax.nn.sigmoid(pre)
        dpre = do_ref[0, 0] * (sig * (1.0 + pre * (1.0 - sig)))
        dx = None
        first = pl.program_id(1) == 0
        for j in range(CONV_WIDTH):
            d = CONV_WIDTH - 1 - j
            ahead = _shift_up(dpre, d)
            term = wv[j:j + 1, :] * ahead
            dx = term if dx is None else dx + term
            dwj = jnp.sum(ahead * xv, axis=0, keepdims=True)
            _accumulate(dw_ref.at[j:j + 1, :], dwj, first)
        dx_ref[0] = dx.astype(dx_ref.dtype)

    return pl.pallas_call(
        body, name=name, grid=(n_c // CONV_LANES, n_b),
        in_specs=[spec, w_spec, do_spec], out_specs=(spec, w_spec),
        out_shape=(jax.ShapeDtypeStruct(x.shape, MXU_DTYPE), jax.ShapeDtypeStruct((CONV_WIDTH, n_c), F32)),
        compiler_params=_params(3 * _nbytes((n_s, CONV_LANES), F32), 8 * _nbytes((n_s, CONV_LANES), F32)),
    )(*_hbm(x, w, dout))


_BNT = (((2,), (2,)), ((0,), (0,)))
_BNN = (((2,), (1,)), ((0,), (0,)))
_BTN = (((1,), (1,)), ((0,), (0,)))
DN_PREP_CHUNKS = 8
DN_SCAN_HEADS = 4
N_DOUBLINGS = 5


def _fdot(a, b, dims):
    return lax.dot_general(a, b, dims, precision=lax.Precision.HIGHEST, preferred_element_type=F32)


def _hdot(a, b, dims):
    return lax.dot_general(a, b, dims, precision=lax.Precision.HIGH, preferred_element_type=F32)


def _solve_by_doubling(a, rhs_u, rhs_w):
    row = lax.broadcasted_iota(jnp.int32, (CHUNK, CHUNK), 0)
    col = lax.broadcasted_iota(jnp.int32, (CHUNK, CHUNK), 1)
    inv = jnp.where(row == col, 1.0, 0.0) - a
    power = a
    for _ in range(N_DOUBLINGS):
        power = _hdot(power, power, _BNN)
        inv = inv + _hdot(inv, power, _BNN)
    return _hdot(inv, rhs_u, _BNN), _hdot(inv, rhs_w, _BNN), inv


@jax.custom_vjp
def _solve_saved(a, rhs_u, rhs_w, inv, u, w):
    return u, w


def _solve_saved_fwd(a, rhs_u, rhs_w, inv, u, w):
    return (u, w), (inv, u, w)


def _solve_saved_bwd(res, cts):
    inv, u, w = res
    gu = _hdot(inv, cts[0], _BTN)
    gw = _hdot(inv, cts[1], _BTN)
    da = -(_hdot(gu, u, _BNT) + _hdot(gw, w, _BNT))
    return da, gu, gw, jnp.zeros_like(inv), jnp.zeros_like(u), jnp.zeros_like(w)


_solve_saved.defvjp(_solve_saved_fwd, _solve_saved_bwd)


def _dn_prep_fn(solve, qc, kc, vc, bl, lac, lar, a_log, dt_bias):
    q = qc * lax.rsqrt(jnp.sum(qc * qc, axis=-1, keepdims=True) + EPS) * (DN_HEAD_DIM ** -0.5)
    k = kc * lax.rsqrt(jnp.sum(kc * kc, axis=-1, keepdims=True) + EPS)
    beta = jax.nn.sigmoid(bl)
    neg_a = -jnp.exp(a_log)
    lgc = neg_a * jax.nn.softplus(lac + dt_bias)
    lgr = neg_a * jax.nn.softplus(lar + dt_bias)
    row = lax.broadcasted_iota(jnp.int32, (CHUNK, CHUNK), 0)
    col = lax.broadcasted_iota(jnp.int32, (CHUNK, CHUNK), 1)
    causal, strict = row >= col, row > col
    g_c = jnp.sum(jnp.where(causal, lgr, 0.0), axis=-1, keepdims=True)
    g_r = jnp.sum(jnp.where(row <= col, lgc, 0.0), axis=-2, keepdims=True)
    decay = jnp.exp(jnp.where(causal, g_c - g_r, -jnp.inf))
    kb = k * beta
    a = jnp.where(strict, _mdot(kb, k, _BNT) * decay, 0.0)
    u, w, extra = solve(a, vc * beta, kb * jnp.exp(g_c))
    attn = _mdot(q, k, _BNT) * decay
    g_last = jnp.sum(lgc, axis=-2, keepdims=True)
    return q * jnp.exp(g_c), k * jnp.exp(g_last - g_c), u, w, attn, g_last, extra


PAIR = 2
PAIR_LANES = PAIR * DN_HEAD_DIM


def _dn_prep_specs(n_cb):
    tok = n_cb * CHUNK
    wide = pl.BlockSpec((1, PAIR, tok, DN_HEAD_DIM), lambda p, b, j: (b, p, j, 0))
    rowv = pl.BlockSpec((1, PAIR, n_cb, 1, CHUNK), lambda p, b, j: (b, p, j, 0, 0))
    one = pl.BlockSpec((1, PAIR, n_cb, 1, 1), lambda p, b, j: (b, p, j, 0, 0))
    head = pl.BlockSpec((PAIR, 1, 1), lambda p, b, j: (p, 0, 0))
    lanes = lambda part: pl.BlockSpec((1, tok, PAIR_LANES), lambda p, b, j: (b, j, part * (DN_HEADS // PAIR) + p))
    return wide, rowv, one, head, lanes


def _split_pair(x, n_cb):
    halves = [x[:, h * DN_HEAD_DIM:(h + 1) * DN_HEAD_DIM].reshape(n_cb, CHUNK, DN_HEAD_DIM) for h in range(PAIR)]
    return jnp.concatenate(halves, axis=0)


def _join_pair(chunks, tok):
    per_head = chunks.reshape(PAIR, tok, DN_HEAD_DIM)
    return jnp.concatenate([per_head[h] for h in range(PAIR)], axis=-1)


def _dn_prep_load(n_cb, q_ref, k_ref, v_ref, blr_ref, lar_ref, al_ref, dt_ref):
    rowf = lambda r: r[0].reshape(PAIR * n_cb, 1, CHUNK)
    return (_split_pair(q_ref[0], n_cb), _split_pair(k_ref[0], n_cb), _split_pair(v_ref[0], n_cb), rowf(blr_ref),
            rowf(lar_ref), al_ref[...], dt_ref[...])


def _dn_prep_pair_fn(n_cb, solve, qc, kc, vc, blr, lar, a_log, dt_bias):
    per_chunk = lambda t: jnp.broadcast_to(t[:, None], (PAIR, n_cb, 1, 1)).reshape(PAIR * n_cb, 1, 1)
    eye = lax.broadcasted_iota(jnp.int32, (CHUNK, CHUNK), 0) == lax.broadcasted_iota(jnp.int32, (CHUNK, CHUNK), 1)
    to_col = lambda r: jnp.sum(jnp.where(eye, r, 0.0), axis=-1, keepdims=True)
    return _dn_prep_fn(solve, qc, kc, vc, to_col(blr), to_col(lar), lar, per_chunk(a_log), per_chunk(dt_bias))


def _dn_prep(qkv, blr, lar, a_log, dt_bias, *, name):
    n_b, n_s, _ = qkv.shape
    n_cb = min(DN_PREP_CHUNKS, n_s // CHUNK)
    tok = n_cb * CHUNK
    wide, rowv, one, head, lanes = _dn_prep_specs(n_cb)

    def body(*refs):
        outs = _dn_prep_pair_fn(n_cb, _solve_by_doubling, *_dn_prep_load(n_cb, *refs[:7]))
        for ref, val in zip(refs[7:12], outs[:5]):
            ref[0] = val.reshape(PAIR, tok, DN_HEAD_DIM)
        refs[12][0] = outs[5].reshape(PAIR, n_cb, 1, 1)
        refs[13][0] = outs[6].reshape(PAIR, tok, DN_HEAD_DIM)

    big = jax.ShapeDtypeStruct((n_b, DN_HEADS, n_s, DN_HEAD_DIM), F32)
    return pl.pallas_call(
        body, name=name, grid=(DN_HEADS // PAIR, n_b, n_s // tok),
        in_specs=[lanes(0), lanes(1), lanes(2), rowv, rowv, head, head],
        out_specs=(wide, wide, wide, wide, wide, one, wide),
        out_shape=(big, big, big, big, big, jax.ShapeDtypeStruct((n_b, DN_HEADS, n_s // CHUNK, 1, 1), F32), big),
        compiler_params=_params(11 * PAIR * _nbytes((tok, LANES), F32), 48 * PAIR * _nbytes((tok, LANES), F32)),
    )(*_hbm(qkv, qkv, qkv, blr, lar, a_log, dt_bias))


def _dn_prep_bwd(qkv, blr, lar, a_log, dt_bias, inv, u, w, cts, *, name):
    n_b, n_s, _ = qkv.shape
    n_cb = min(DN_PREP_CHUNKS, n_s // CHUNK)
    tok = n_cb * CHUNK
    wide, rowv, one, head, lanes = _dn_prep_specs(n_cb)

    def body(*refs):
        prim = _dn_prep_load(n_cb, *refs[:7])
        chunks = lambda r: r[0].reshape(PAIR * n_cb, CHUNK, DN_HEAD_DIM)
        inv_v, u_v, w_v = chunks(refs[7]), chunks(refs[8]), chunks(refs[9])
        ct = tuple(chunks(r) for r in refs[10:15]) + (refs[15][0].reshape(PAIR * n_cb, 1, 1),)

        def fn(*args):
            solve = lambda a, ru, rw: _solve_saved(a, ru, rw, inv_v, u_v, w_v) + (None,)
            return _dn_prep_pair_fn(n_cb, solve, *args)[:6]

        _, pull = jax.vjp(fn, *prim)
        dq, dk, dv, dblr, dlar, dal, ddt = pull(ct)
        outs = refs[16:]
        for part, val in enumerate((dq, dk, dv)):
            outs[0][part, 0] = _join_pair(val, tok)
        outs[1][0] = dblr.reshape(PAIR, n_cb, 1, CHUNK)
        outs[2][0] = dlar.reshape(PAIR, n_cb, 1, CHUNK)
        first = jnp.logical_and(pl.program_id(1) == 0, pl.program_id(2) == 0)
        _accumulate(outs[3], dal, first)
        _accumulate(outs[4], ddt, first)

    dqkv_spec = pl.BlockSpec((3, 1, tok, PAIR_LANES), lambda p, b, j: (0, b, j, p))
    return pl.pallas_call(
        body, name=name, grid=(DN_HEADS // PAIR, n_b, n_s // tok),
        in_specs=[lanes(0), lanes(1), lanes(2), rowv, rowv, head, head, wide, wide, wide, wide, wide, wide, wide, wide, one],
        out_specs=(dqkv_spec, rowv, rowv, head, head),
        out_shape=(jax.ShapeDtypeStruct((3, n_b, n_s, DN_WIDTH), F32), jax.ShapeDtypeStruct(blr.shape, F32),
                   jax.ShapeDtypeStruct(lar.shape, F32), jax.ShapeDtypeStruct(a_log.shape, F32),
                   jax.ShapeDtypeStruct(dt_bias.shape, F32)),
        compiler_params=_params(21 * PAIR * _nbytes((tok, LANES), F32), 64 * PAIR * _nbytes((tok, LANES), F32)),
    )(*_hbm(qkv, qkv, qkv, blr, lar, a_log, dt_bias, inv, u, w, *cts))


def _dn_step(state, q, k, u, w, a, gl):
    v_new = u - _mdot(w, state, _BNN)
    o = _mdot(q, state, _BNN) + _mdot(a, v_new, _BNN)
    return state * jnp.exp(gl) + _mdot(k, v_new, _BTN), o


def _dn_scan_specs(n_cb, n_blocks, reverse):
    tok = n_cb * CHUNK
    jj = (lambda j: n_blocks - 1 - j) if reverse else (lambda j: j)
    wide = pl.BlockSpec((1, DN_SCAN_HEADS, tok, DN_HEAD_DIM), lambda b, h, j: (b, h, jj(j), 0))
    one = pl.BlockSpec((1, DN_SCAN_HEADS, n_cb, 1, 1), lambda b, h, j: (b, h, jj(j), 0, 0))
    st = pl.BlockSpec((1, DN_SCAN_HEADS, n_cb, DN_HEAD_DIM, DN_HEAD_DIM), lambda b, h, j: (b, h, jj(j), 0, 0))
    return wide, one, st


def _dn_scan(qd, kd, u, w, attn, g_last, *, name):
    n_b, n_h, n_s, _ = qd.shape
    n_cb = min(DN_PREP_CHUNKS, n_s // CHUNK)
    n_blocks = n_s // (n_cb * CHUNK)
    wide, one, st = _dn_scan_specs(n_cb, n_blocks, False)

    def body(qd_ref, kd_ref, u_ref, w_ref, a_ref, gl_ref, o_ref, st_ref, state_ref):
        @pl.when(pl.program_id(2) == 0)
        def _():
            state_ref[...] = jnp.zeros(state_ref.shape, F32)

        def step(n, state):
            rows = pl.ds(pl.multiple_of(n * CHUNK, CHUNK), CHUNK)
            st_ref[0, :, n] = state
            state, o = _dn_step(state, qd_ref[0, :, rows, :], kd_ref[0, :, rows, :], u_ref[0, :, rows, :],
                                w_ref[0, :, rows, :], a_ref[0, :, rows, :], gl_ref[0, :, n])
            o_ref[0, :, rows, :] = o
            return state

        state_ref[...] = lax.fori_loop(0, n_cb, step, state_ref[...])

    return pl.pallas_call(
        body, name=name, grid=(n_b, n_h // DN_SCAN_HEADS, n_blocks),
        in_specs=[wide, wide, wide, wide, wide, one], out_specs=(wide, st),
        out_shape=(jax.ShapeDtypeStruct(qd.shape, F32),
                   jax.ShapeDtypeStruct((n_b, n_h, n_s // CHUNK, DN_HEAD_DIM, DN_HEAD_DIM), F32)),
        scratch_shapes=[pltpu.VMEM((DN_SCAN_HEADS, DN_HEAD_DIM, DN_HEAD_DIM), F32)],
        compiler_params=_params(8 * _nbytes((DN_SCAN_HEADS, n_cb * CHUNK, LANES), F32), 8 << 20),
    )(*_hbm(qd, kd, u, w, attn, g_last))


def _dn_scan_bwd(qd, kd, u, w, attn, g_last, states, do, *, name):
    n_b, n_h, n_s, _ = qd.shape
    n_cb = min(DN_PREP_CHUNKS, n_s // CHUNK)
    n_blocks = n_s // (n_cb * CHUNK)
    wide, one, st = _dn_scan_specs(n_cb, n_blocks, True)

    def body(qd_ref, kd_ref, u_ref, w_ref, a_ref, gl_ref, st_ref, do_ref,
             dq_ref, dk_ref, du_ref, dw_ref, da_ref, dgl_ref, dstate_ref):
        @pl.when(pl.program_id(2) == 0)
        def _():
            dstate_ref[...] = jnp.zeros(dstate_ref.shape, F32)

        def step(i, dstate):
            n = n_cb - 1 - i
            rows = pl.ds(pl.multiple_of(n * CHUNK, CHUNK), CHUNK)
            _, pull = jax.vjp(_dn_step, st_ref[0, :, n], qd_ref[0, :, rows, :], kd_ref[0, :, rows, :],
                              u_ref[0, :, rows, :], w_ref[0, :, rows, :], a_ref[0, :, rows, :], gl_ref[0, :, n])
            dstate, dq, dk, du, dw, da, dgl = pull((dstate, do_ref[0, :, rows, :]))
            dq_ref[0, :, rows, :] = dq
            dk_ref[0, :, rows, :] = dk
            du_ref[0, :, rows, :] = du
            dw_ref[0, :, rows, :] = dw
            da_ref[0, :, rows, :] = da
            dgl_ref[0, :, n] = dgl
            return dstate

        dstate_ref[...] = lax.fori_loop(0, n_cb, step, dstate_ref[...])

    big = jax.ShapeDtypeStruct(qd.shape, F32)
    return pl.pallas_call(
        body, name=name, grid=(n_b, n_h // DN_SCAN_HEADS, n_blocks),
        in_specs=[wide, wide, wide, wide, wide, one, st, wide],
        out_specs=(wide, wide, wide, wide, wide, one),
        out_shape=(big, big, big, big, big, jax.ShapeDtypeStruct(g_last.shape, F32)),
        scratch_shapes=[pltpu.VMEM((DN_SCAN_HEADS, DN_HEAD_DIM, DN_HEAD_DIM), F32)],
        compiler_params=_params(13 * _nbytes((DN_SCAN_HEADS, n_cb * CHUNK, LANES), F32), 8 << 20),
    )(*_hbm(qd, kd, u, w, attn, g_last, states, do))


def _dn_post_fn(o, z, gain):
    return o * lax.rsqrt(jnp.mean(o * o, axis=-1, keepdims=True) + EPS) * gain * jax.nn.silu(z)


_HEAD_ROWS = lambda n_s: pl.BlockSpec((1, PAIR, n_s, DN_HEAD_DIM), lambda b, p: (b, p, 0, 0))
_PAIR_LANES = lambda n_s: pl.BlockSpec((1, n_s, PAIR_LANES), lambda b, p: (b, 0, p))
_HEAD_GAIN = pl.BlockSpec((1, DN_HEAD_DIM), lambda b, p: (0, 0))


def _pair_heads(x):
    return jnp.stack([x[:, h * DN_HEAD_DIM:(h + 1) * DN_HEAD_DIM] for h in range(PAIR)])


def _pair_lanes(x):
    return jnp.concatenate([x[h] for h in range(PAIR)], axis=-1)


def _dn_post(o, z, gain, *, name):
    n_b, _, n_s, _ = o.shape

    def body(o_ref, z_ref, g_ref, out_ref):
        out = _dn_post_fn(o_ref[0], _pair_heads(z_ref[0]), g_ref[...])
        out_ref[0] = _pair_lanes(out).astype(out_ref.dtype)

    lanes = _PAIR_LANES(n_s)
    return pl.pallas_call(
        body, name=name, grid=(n_b, DN_HEADS // PAIR), in_specs=[_HEAD_ROWS(n_s), lanes, _HEAD_GAIN], out_specs=lanes,
        out_shape=jax.ShapeDtypeStruct(z.shape, MXU_DTYPE),
        compiler_params=_params(3 * PAIR * _nbytes((n_s, LANES), F32), 6 * PAIR * _nbytes((n_s, LANES), F32)),
    )(*_hbm(o, z, gain))


def _dn_post_bwd(o, z, gain, dout, *, name):
    n_b, _, n_s, _ = o.shape

    def body(o_ref, z_ref, g_ref, dout_ref, do_ref, dz_ref, dg_ref):
        _, pull = jax.vjp(_dn_post_fn, o_ref[0], _pair_heads(z_ref[0]), g_ref[...])
        do, dz, dg = pull(_pair_heads(dout_ref[0].astype(F32)))
        do_ref[0] = do
        dz_ref[0] = _pair_lanes(dz).astype(dz_ref.dtype)
        _accumulate(dg_ref, dg, jnp.logical_and(pl.program_id(0) == 0, pl.program_id(1) == 0))

    rows, lanes = _HEAD_ROWS(n_s), _PAIR_LANES(n_s)
    return pl.pallas_call(
        body, name=name, grid=(n_b, DN_HEADS // PAIR), in_specs=[rows, lanes, _HEAD_GAIN, lanes],
        out_specs=(rows, lanes, _HEAD_GAIN),
        out_shape=(jax.ShapeDtypeStruct(o.shape, F32), jax.ShapeDtypeStruct(z.shape, MXU_DTYPE),
                   jax.ShapeDtypeStruct((1, DN_HEAD_DIM), F32)),
        compiler_params=_params(5 * PAIR * _nbytes((n_s, LANES), F32), 10 * PAIR * _nbytes((n_s, LANES), F32)),
    )(*_hbm(o, z, gain, dout))


S5_SCAN_LANES = 256
TILE_ROWS = SUBLANES


def _s5_prep_fn(lam_re, lam_im, log_step, bt_re, bt_im, c_im):
    lr = jnp.minimum(lam_re, -1e-4)
    step = jnp.exp(log_step)
    mag = jnp.exp(lr * step)
    ang = lam_im * step
    lb_re = mag * jnp.cos(ang)
    lb_im = mag * jnp.sin(ang)
    den = lr * lr + lam_im * lam_im
    coef_re = ((lb_re - 1.0) * lr + lb_im * lam_im) / den
    coef_im = (lb_im * lr - (lb_re - 1.0) * lam_im) / den
    return (lb_re, lb_im, coef_re * bt_re - coef_im * bt_im, coef_re * bt_im + coef_im * bt_re, -c_im)


def _s5_prep(lam_re, lam_im, log_step, bt_re, bt_im, c_im, *, name):
    def body(*refs):
        outs = _s5_prep_fn(*(r[...] for r in refs[:6]))
        for ref, val in zip(refs[6:], outs):
            ref[...] = val

    vec = jax.ShapeDtypeStruct(lam_re.shape, F32)
    mat = jax.ShapeDtypeStruct(bt_re.shape, F32)
    return pl.pallas_call(body, name=name, out_shape=(vec, vec, mat, mat, mat))(lam_re, lam_im, log_step, bt_re, bt_im, c_im)


def _s5_prep_bwd(lam_re, lam_im, log_step, bt_re, bt_im, c_im, cts, *, name):
    def body(*refs):
        _, pull = jax.vjp(_s5_prep_fn, *(r[...] for r in refs[:6]))
        grads = pull(tuple(r[...] for r in refs[6:11]))
        for ref, val in zip(refs[11:], grads):
            ref[...] = val

    shapes = tuple(jax.ShapeDtypeStruct(a.shape, F32) for a in (lam_re, lam_im, log_step, bt_re, bt_im, c_im))
    return pl.pallas_call(body, name=name, out_shape=shapes)(lam_re, lam_im, log_step, bt_re, bt_im, c_im, *cts)


def _cmul(ar, ai, br, bi):
    return ar * br - ai * bi, ar * bi + ai * br


def _s5_powers(lr, li):
    pows = [(lr, li)]
    for _ in range(TILE_ROWS - 1):
        pows.append(_cmul(pows[-1][0], pows[-1][1], lr, li))
    return pows


def _s5_carry_table(pows, n_lanes, reverse):
    row = lax.broadcasted_iota(jnp.int32, (TILE_ROWS, n_lanes), 0)
    t_re = jnp.zeros((TILE_ROWS, n_lanes), F32)
    t_im = jnp.zeros((TILE_ROWS, n_lanes), F32)
    for r in range(TILE_ROWS):
        p_re, p_im = pows[TILE_ROWS - 1 - r] if reverse else pows[r]
        t_re = jnp.where(row == r, p_re, t_re)
        t_im = jnp.where(row == r, p_im, t_im)
    return t_re, t_im


def _s5_tile(y_re, y_im, pows, reverse):
    d = 1
    while d < TILE_ROWS:
        p_re, p_im = pows[d - 1]
        if reverse:
            s_re, s_im = _shift_up(y_re, d), _shift_up(y_im, d)
        else:
            s_re, s_im = _shift_down(y_re, d), _shift_down(y_im, d)
        m_re, m_im = _cmul(p_re, p_im, s_re, s_im)
        y_re, y_im = y_re + m_re, y_im + m_im
        d *= 2
    return y_re, y_im


S5_BLOCKS = N_SHARD
S5_BLOCK_CH = S5_WIDTH // S5_BLOCKS
S5_BLOCK_LANES = S5_LANES // S5_BLOCKS
SCAN_PER_BLOCK = S5_BLOCK_LANES // S5_SCAN_LANES


def _s5_scan_specs(n_s, order):
    L = S5_SCAN_LANES

    def cat_spec(part):
        return pl.BlockSpec((1, 1, n_s, L), lambda *g: (order(*g)[1] // SCAN_PER_BLOCK, order(*g)[0], 0,
                                                        part * SCAN_PER_BLOCK + order(*g)[1] % SCAN_PER_BLOCK))

    one = pl.BlockSpec((1, 1, n_s, L), lambda *g: (order(*g)[1] // SCAN_PER_BLOCK, order(*g)[0], 0,
                                                   order(*g)[1] % SCAN_PER_BLOCK))
    lam = pl.BlockSpec((1, L), lambda *g: (0, order(*g)[1]))
    return cat_spec, one, lam


def _s5_scan(bu, lb_re, lb_im, *, name):
    n_blk, n_b, n_s, _ = bu.shape
    n_lb = S5_LANES // S5_SCAN_LANES
    n_tiles = n_s // TILE_ROWS
    L = S5_SCAN_LANES

    def body(re_ref, im_ref, lr_ref, li_ref, xr_ref, xi_ref):
        pows = _s5_powers(lr_ref[...], li_ref[...])
        t_re, t_im = _s5_carry_table(pows, L, False)

        def step(i, carry):
            rows = pl.ds(pl.multiple_of(i * TILE_ROWS, TILE_ROWS), TILE_ROWS)
            y_re, y_im = _s5_tile(re_ref[0, 0, rows, :], im_ref[0, 0, rows, :], pows, False)
            c_re, c_im = _cmul(t_re, t_im, carry[0], carry[1])
            y_re, y_im = y_re + c_re, y_im + c_im
            xr_ref[0, 0, rows, :] = y_re
            xi_ref[0, 0, rows, :] = y_im
            return y_re[TILE_ROWS - 1:, :], y_im[TILE_ROWS - 1:, :]

        zero = jnp.zeros((1, L), F32)
        lax.fori_loop(0, n_tiles, step, (zero, zero))

    cat_spec, one, lam = _s5_scan_specs(n_s, lambda b, j: (b, j))
    x_shape = jax.ShapeDtypeStruct((n_blk, n_b, n_s, S5_BLOCK_LANES), F32)
    return pl.pallas_call(
        body, name=name, grid=(n_b, n_lb),
        in_specs=[cat_spec(0), cat_spec(1), lam, lam],
        out_specs=(one, one), out_shape=(x_shape, x_shape),
        compiler_params=_params(4 * _nbytes((n_s, L), F32), 4 << 20),
    )(*_hbm(bu, bu, lb_re, lb_im))


def _s5_scan_bwd(dx, x_re, x_im, lb_re, lb_im, *, name):
    n_blk, n_b, n_s, _ = dx.shape
    n_lb = S5_LANES // S5_SCAN_LANES
    n_tiles = n_s // TILE_ROWS
    L = S5_SCAN_LANES

    def body(dr_ref, di_ref, xr_ref, xi_ref, lr_ref, li_ref, ar_ref, ai_ref, dlr_ref, dli_ref):
        pows = _s5_powers(lr_ref[...], -li_ref[...])
        t_re, t_im = _s5_carry_table(pows, L, True)
        row = lax.broadcasted_iota(jnp.int32, (TILE_ROWS, L), 0)

        def step(k, carry):
            c_re, c_im, s_re, s_im = carry
            i = n_tiles - 1 - k
            rows = pl.ds(pl.multiple_of(i * TILE_ROWS, TILE_ROWS), TILE_ROWS)
            a_re, a_im = _s5_tile(dr_ref[0, 0, rows, :], di_ref[0, 0, rows, :], pows, True)
            m_re, m_im = _cmul(t_re, t_im, c_re, c_im)
            a_re, a_im = a_re + m_re, a_im + m_im
            ar_ref[0, 0, rows, :] = a_re.astype(ar_ref.dtype)
            ai_ref[0, 0, rows, :] = a_im.astype(ai_ref.dtype)
            prev = pl.ds(pl.multiple_of(jnp.maximum(i - 1, 0) * TILE_ROWS, TILE_ROWS), TILE_ROWS)
            keep = jnp.where(i > 0, 1.0, 0.0)
            last_re = xr_ref[0, 0, prev, :][TILE_ROWS - 1:, :] * keep
            last_im = xi_ref[0, 0, prev, :][TILE_ROWS - 1:, :] * keep
            xp_re = jnp.where(row == 0, last_re, _shift_down(xr_ref[0, 0, rows, :], 1))
            xp_im = jnp.where(row == 0, last_im, _shift_down(xi_ref[0, 0, rows, :], 1))
            s_re = s_re + a_re * xp_re + a_im * xp_im
            s_im = s_im + a_im * xp_re - a_re * xp_im
            return a_re[:1, :], a_im[:1, :], s_re, s_im

        zero = jnp.zeros((1, L), F32)
        zt = jnp.zeros((TILE_ROWS, L), F32)
        _, _, s_re, s_im = lax.fori_loop(0, n_tiles, step, (zero, zero, zt, zt))
        first = pl.program_id(1) == 0
        _accumulate(dlr_ref, jnp.sum(s_re, axis=0, keepdims=True), first)
        _accumulate(dli_ref, jnp.sum(s_im, axis=0, keepdims=True), first)

    cat_spec, one, lam = _s5_scan_specs(n_s, lambda j, b: (b, j))
    a_shape = jax.ShapeDtypeStruct((n_blk, n_b, n_s, S5_BLOCK_LANES), MXU_DTYPE)
    lam_shape = jax.ShapeDtypeStruct((1, S5_LANES), F32)
    return pl.pallas_call(
        body, name=name, grid=(n_lb, n_b),
        in_specs=[cat_spec(0), cat_spec(1), one, one, lam, lam],
        out_specs=(one, one, lam, lam),
        out_shape=(a_shape, a_shape, lam_shape, lam_shape),
        compiler_params=_params(5 * _nbytes((n_s, L), F32), 4 << 20),
    )(*_hbm(dx, dx, x_re, x_im, lb_re, lb_im))


def _scan_rows(i):
    return pl.ds(pl.multiple_of(i * TILE_ROWS, TILE_ROWS), TILE_ROWS)


def _s5_mix_specs(n_s, order):
    jb = lambda *g: order(*g)[0]
    bb = lambda *g: order(*g)[1]
    act = pl.BlockSpec((1, 1, n_s, S5_BLOCK_CH), lambda *g: (bb(*g), 0, 0, jb(*g)))
    state = pl.BlockSpec((1, 1, n_s, S5_BLOCK_LANES), lambda *g: (jb(*g), bb(*g), 0, 0))
    lam = pl.BlockSpec((1, S5_BLOCK_LANES), lambda *g: (0, jb(*g)))
    w_in = pl.BlockSpec((1, S5_BLOCK_CH, S5_BLOCK_LANES), lambda *g: (jb(*g), 0, 0))
    w_out = pl.BlockSpec((1, S5_BLOCK_LANES, S5_BLOCK_CH), lambda *g: (jb(*g), 0, 0))
    return act, state, lam, w_in, w_out


def _s5_mix(u, wb_re, wb_im, lb_re, lb_im, wc_re, wc_im, *, name):
    n_b, n_s, _ = u.shape
    n_blk = S5_BLOCKS
    lanes = lambda t: t[:, None]
    n_tiles = n_s // TILE_ROWS
    L = S5_BLOCK_LANES

    def body(u_ref, wbr_ref, wbi_ref, lr_ref, li_ref, wcr_ref, wci_ref, y_ref, xr_ref, xi_ref):
        uv = u_ref[0, 0].astype(MXU_DTYPE)
        xr_ref[0, 0] = lax.dot_general(uv, wbr_ref[0].astype(MXU_DTYPE), _NN, preferred_element_type=F32)
        xi_ref[0, 0] = lax.dot_general(uv, wbi_ref[0].astype(MXU_DTYPE), _NN, preferred_element_type=F32)
        pows = _s5_powers(lr_ref[...], li_ref[...])
        t_re, t_im = _s5_carry_table(pows, L, False)

        def step(i, carry):
            rows = _scan_rows(i)
            y_re, y_im = _s5_tile(xr_ref[0, 0, rows, :], xi_ref[0, 0, rows, :], pows, False)
            c_re, c_im = _cmul(t_re, t_im, carry[0], carry[1])
            y_re, y_im = y_re + c_re, y_im + c_im
            xr_ref[0, 0, rows, :] = y_re
            xi_ref[0, 0, rows, :] = y_im
            return y_re[TILE_ROWS - 1:, :], y_im[TILE_ROWS - 1:, :]

        zero = jnp.zeros((1, L), F32)
        lax.fori_loop(0, n_tiles, step, (zero, zero))
        y_ref[0, 0] = (
            lax.dot_general(xr_ref[0, 0].astype(MXU_DTYPE), wcr_ref[0].astype(MXU_DTYPE), _NN, preferred_element_type=F32)
            + lax.dot_general(xi_ref[0, 0].astype(MXU_DTYPE), wci_ref[0].astype(MXU_DTYPE), _NN, preferred_element_type=F32))

    act, state, lam, w_in, w_out = _s5_mix_specs(n_s, lambda b, j: (j, b))
    x_shape = jax.ShapeDtypeStruct((n_blk, n_b, n_s, L), F32)
    return pl.pallas_call(
        body, name=name, grid=(n_b, n_blk),
        in_specs=[act, w_in, w_in, lam, lam, w_out, w_out], out_specs=(act, state, state),
        out_shape=(jax.ShapeDtypeStruct((n_b, 1, n_s, S5_WIDTH), F32), x_shape, x_shape),
        compiler_params=_params(2 * _nbytes((n_s, L), F32) + 2 * _nbytes((n_s, S5_BLOCK_CH), F32), 3 * _nbytes((n_s, L), F32)),
    )(*_hbm(lanes(u), wb_re, wb_im, lb_re, lb_im, wc_re, wc_im))


def _s5_mix_bwd(dy, du_skip, u, x_re, x_im, wb_re, wb_im, lb_re, lb_im, wc_re, wc_im, *, name):
    n_b, n_s, _ = u.shape
    n_blk = S5_BLOCKS
    lanes = lambda t: t[:, None]
    n_tiles = n_s // TILE_ROWS
    L = S5_BLOCK_LANES

    def body(dy_ref, ds_ref, u_ref, xr_ref, xi_ref, wbr_ref, wbi_ref, lr_ref, li_ref, wcr_ref, wci_ref,
             du_ref, dwbr_ref, dwbi_ref, dlr_ref, dli_ref, dwcr_ref, dwci_ref, ar_ref, ai_ref):
        dyv = dy_ref[0, 0].astype(MXU_DTYPE)
        ar_ref[...] = lax.dot_general(dyv, wcr_ref[0].astype(MXU_DTYPE), _NT, preferred_element_type=F32)
        ai_ref[...] = lax.dot_general(dyv, wci_ref[0].astype(MXU_DTYPE), _NT, preferred_element_type=F32)
        pows = _s5_powers(lr_ref[...], -li_ref[...])
        t_re, t_im = _s5_carry_table(pows, L, True)
        row = lax.broadcasted_iota(jnp.int32, (TILE_ROWS, L), 0)

        def step(k, carry):
            c_re, c_im, s_re, s_im = carry
            i = n_tiles - 1 - k
            rows = _scan_rows(i)
            a_re, a_im = _s5_tile(ar_ref[rows, :], ai_ref[rows, :], pows, True)
            m_re, m_im = _cmul(t_re, t_im, c_re, c_im)
            a_re, a_im = a_re + m_re, a_im + m_im
            ar_ref[rows, :] = a_re
            ai_ref[rows, :] = a_im
            prev = _scan_rows(jnp.maximum(i - 1, 0))
            keep = jnp.where(i > 0, 1.0, 0.0)
            last_re = xr_ref[0, 0, prev, :][TILE_ROWS - 1:, :] * keep
            last_im = xi_ref[0, 0, prev, :][TILE_ROWS - 1:, :] * keep
            xp_re = jnp.where(row == 0, last_re, _shift_down(xr_ref[0, 0, rows, :], 1))
            xp_im = jnp.where(row == 0, last_im, _shift_down(xi_ref[0, 0, rows, :], 1))
            s_re = s_re + a_re * xp_re + a_im * xp_im
            s_im = s_im + a_im * xp_re - a_re * xp_im
            return a_re[:1, :], a_im[:1, :], s_re, s_im

        zero = jnp.zeros((1, L), F32)
        zt = jnp.zeros((TILE_ROWS, L), F32)
        _, _, s_re, s_im = lax.fori_loop(0, n_tiles, step, (zero, zero, zt, zt))
        first = pl.program_id(1) == 0
        _accumulate(dlr_ref, jnp.sum(s_re, axis=0, keepdims=True), first)
        _accumulate(dli_ref, jnp.sum(s_im, axis=0, keepdims=True), first)
        a_re, a_im = ar_ref[...].astype(MXU_DTYPE), ai_ref[...].astype(MXU_DTYPE)
        du = (lax.dot_general(a_re, wbr_ref[0].astype(MXU_DTYPE), _NT, preferred_element_type=F32)
              + lax.dot_general(a_im, wbi_ref[0].astype(MXU_DTYPE), _NT, preferred_element_type=F32))
        du_ref[0, 0] = (du + ds_ref[0, 0]).astype(du_ref.dtype)
        uv = u_ref[0, 0].astype(MXU_DTYPE)
        _accumulate(dwbr_ref, lax.dot_general(uv, a_re, _TN, preferred_element_type=F32)[None], first)
        _accumulate(dwbi_ref, lax.dot_general(uv, a_im, _TN, preferred_element_type=F32)[None], first)
        _accumulate(dwcr_ref, lax.dot_general(xr_ref[0, 0].astype(MXU_DTYPE), dyv, _TN, preferred_element_type=F32)[None], first)
        _accumulate(dwci_ref, lax.dot_general(xi_ref[0, 0].astype(MXU_DTYPE), dyv, _TN, preferred_element_type=F32)[None], first)

    act, state, lam, w_in, w_out = _s5_mix_specs(n_s, lambda j, b: (j, b))
    lam_shape = jax.ShapeDtypeStruct((1, S5_LANES), F32)
    return pl.pallas_call(
        body, name=name, grid=(n_blk, n_b),
        in_specs=[act, act, act, state, state, w_in, w_in, lam, lam, w_out, w_out],
        out_specs=(act, w_in, w_in, lam, lam, w_out, w_out),
        out_shape=(jax.ShapeDtypeStruct((n_b, 1, n_s, S5_WIDTH), MXU_DTYPE), jax.ShapeDtypeStruct(wb_re.shape, F32),
                   jax.ShapeDtypeStruct(wb_im.shape, F32), lam_shape, lam_shape,
                   jax.ShapeDtypeStruct(wc_re.shape, F32), jax.ShapeDtypeStruct(wc_im.shape, F32)),
        scratch_shapes=[pltpu.VMEM((n_s, L), F32), pltpu.VMEM((n_s, L), F32)],
        compiler_params=_params(2 * _nbytes((n_s, L), F32) + 4 * _nbytes((n_s, S5_BLOCK_CH), F32), 5 * _nbytes((n_s, L), F32)),
    )(*_hbm(lanes(dy), lanes(du_skip), lanes(u), x_re, x_im, wb_re, wb_im, lb_re, lb_im, wc_re, wc_im))


def _s5_out_fn(ymm, u, d_skip, w_glu, b_glu):
    y = jax.nn.gelu(ymm + d_skip * u)
    return y * jax.nn.sigmoid(_mdot(y, w_glu, _NN) + b_glu)


def _s5_out_specs(tm):
    rows = pl.BlockSpec((tm, S5_WIDTH), lambda i: (i, 0))
    vec = pl.BlockSpec((1, S5_WIDTH), lambda i: (0, 0))
    mat = pl.BlockSpec((S5_WIDTH, S5_WIDTH), lambda i: (0, 0))
    return rows, vec, mat


def _s5_out(ymm, u, d_skip, w_glu, b_glu, *, name, tm=512):
    n_tok = ymm.shape[0]
    tm = min(tm, n_tok)
    rows, vec, mat = _s5_out_specs(tm)

    def body(y_ref, u_ref, d_ref, w_ref, b_ref, o_ref):
        o_ref[...] = _s5_out_fn(y_ref[...], u_ref[...], d_ref[...], w_ref[...], b_ref[...]).astype(o_ref.dtype)

    return pl.pallas_call(
        body, name=name, grid=(n_tok // tm,), in_specs=[rows, rows, vec, mat, vec], out_specs=rows,
        out_shape=jax.ShapeDtypeStruct((n_tok, S5_WIDTH), MXU_DTYPE),
        compiler_params=_params(4 * _nbytes((tm, S5_WIDTH), F32), 8 * _nbytes((tm, S5_WIDTH), F32)),
    )(*_hbm(ymm, u, d_skip, w_glu, b_glu))


def _s5_out_bwd(ymm, u, d_skip, w_glu, b_glu, dout, *, name, tm=512):
    n_tok = ymm.shape[0]
    tm = min(tm, n_tok)
    rows, vec, mat = _s5_out_specs(tm)

    def body(y_ref, u_ref, d_ref, w_ref, b_ref, do_ref, dy_ref, du_ref, dd_ref, dw_ref, db_ref):
        _, pull = jax.vjp(_s5_out_fn, y_ref[...], u_ref[...], d_ref[...], w_ref[...].astype(F32), b_ref[...])
        dy, du, dd, dw, db = pull(do_ref[...])
        dy_ref[...] = dy.astype(dy_ref.dtype)
        du_ref[...] = du
        first = pl.program_id(0) == 0
        _accumulate(dd_ref, dd, first)
        _accumulate(dw_ref, dw, first)
        _accumulate(db_ref, db, first)

    return pl.pallas_call(
        body, name=name, grid=(n_tok // tm,), in_specs=[rows, rows, vec, mat, vec, rows],
        out_specs=(rows, rows, vec, mat, vec),
        out_shape=(jax.ShapeDtypeStruct(ymm.shape, MXU_DTYPE), jax.ShapeDtypeStruct(ymm.shape, F32),
                   jax.ShapeDtypeStruct((1, S5_WIDTH), F32), jax.ShapeDtypeStruct((S5_WIDTH, S5_WIDTH), F32),
                   jax.ShapeDtypeStruct((1, S5_WIDTH), F32)),
        compiler_params=_params(6 * _nbytes((tm, S5_WIDTH), F32), 12 * _nbytes((tm, S5_WIDTH), F32)),
    )(*_hbm(ymm, u, d_skip, w_glu, b_glu, dout))


def _merge_fn(ga, gb, ya, yb):
    return jax.nn.sigmoid(ga) * ya + jax.nn.sigmoid(gb) * yb


def _merge(gab, ya, yb, *, name, tm=512):
    n_tok = ya.shape[0]
    tm = min(tm, n_tok)
    rows = pl.BlockSpec((tm, D_MODEL), lambda i: (i, 0))

    def body(ga_ref, gb_ref, ya_ref, yb_ref, o_ref):
        o_ref[...] = _merge_fn(ga_ref[...], gb_ref[...], ya_ref[...], yb_ref[...]).astype(o_ref.dtype)

    return pl.pallas_call(
        body, name=name, grid=(n_tok // tm,),
        in_specs=[rows, pl.BlockSpec((tm, D_MODEL), lambda i: (i, 1)), rows, rows], out_specs=rows,
        out_shape=jax.ShapeDtypeStruct(ya.shape, MXU_DTYPE),
        compiler_params=_params(5 * _nbytes((tm, D_MODEL), F32), 4 * _nbytes((tm, D_MODEL), F32)),
    )(*_hbm(gab, gab, ya, yb))


def _merge_bwd(gab, ya, yb, dout, *, name, tm=512):
    n_tok = ya.shape[0]
    tm = min(tm, n_tok)
    rows = pl.BlockSpec((tm, D_MODEL), lambda i: (i, 0))

    def body(ga_ref, gb_ref, ya_ref, yb_ref, do_ref, *out_refs):
        _, pull = jax.vjp(_merge_fn, ga_ref[...], gb_ref[...], ya_ref[...], yb_ref[...])
        for ref, val in zip(out_refs, pull(do_ref[...])):
            ref[...] = val.astype(ref.dtype)

    shape = jax.ShapeDtypeStruct(ya.shape, MXU_DTYPE)
    return pl.pallas_call(
        body, name=name, grid=(n_tok // tm,),
        in_specs=[rows, pl.BlockSpec((tm, D_MODEL), lambda i: (i, 1)), rows, rows, rows],
        out_specs=(rows, rows, rows, rows), out_shape=(shape, shape, shape, shape),
        compiler_params=_params(7 * _nbytes((tm, D_MODEL), F32), 6 * _nbytes((tm, D_MODEL), F32)),
    )(*_hbm(gab, gab, ya, yb, dout))


ADA_SHARD = N_MOD * D_MODEL // N_SHARD


def _ada_fwd(c_pad, w_s, b_s, *, name):
    n_r = c_pad.shape[0]

    def body(c_ref, w_ref, b_ref, o_ref):
        sc = jax.nn.silu(c_ref[...]).astype(MXU_DTYPE)
        o_ref[0] = lax.dot_general(sc, w_ref[0].astype(MXU_DTYPE), _NN, preferred_element_type=F32) + b_ref[0]

    return pl.pallas_call(
        body, name=name, grid=(N_SHARD,),
        in_specs=[pl.BlockSpec((n_r, D_MODEL), lambda s: (0, 0)),
                  pl.BlockSpec((1, D_MODEL, ADA_SHARD), lambda s: (s, 0, 0)),
                  pl.BlockSpec((1, 1, ADA_SHARD), lambda s: (s, 0, 0))],
        out_specs=pl.BlockSpec((1, n_r, ADA_SHARD), lambda s: (s, 0, 0)),
        out_shape=jax.ShapeDtypeStruct((N_SHARD, n_r, ADA_SHARD), F32),
        compiler_params=_params(_nbytes((D_MODEL, ADA_SHARD), w_s.dtype), 1 << 20),
    )(*_hbm(c_pad, w_s, b_s))


def _ada_bwd(c_pad, dmod_s, *, name):
    n_r = c_pad.shape[0]

    def body(c_ref, d_ref, dw_ref, db_ref):
        sc = jax.nn.silu(c_ref[...])
        dm = d_ref[0]
        dw_ref[0] = _fdot(sc, dm, _TN)
        db_ref[0] = jnp.sum(dm, axis=0, keepdims=True)

    return pl.pallas_call(
        body, name=name, grid=(N_SHARD,),
        in_specs=[pl.BlockSpec((n_r, D_MODEL), lambda s: (0, 0)), pl.BlockSpec((1, n_r, ADA_SHARD), lambda s: (s, 0, 0))],
        out_specs=(pl.BlockSpec((1, D_MODEL, ADA_SHARD), lambda s: (s, 0, 0)),
                   pl.BlockSpec((1, 1, ADA_SHARD), lambda s: (s, 0, 0))),
        out_shape=(jax.ShapeDtypeStruct((N_SHARD, D_MODEL, ADA_SHARD), F32),
                   jax.ShapeDtypeStruct((N_SHARD, 1, ADA_SHARD), F32)),
        compiler_params=_params(_nbytes((D_MODEL, ADA_SHARD), F32), 2 * _nbytes((D_MODEL, ADA_SHARD), F32)),
    )(*_hbm(c_pad, dmod_s))


def _heads(t, n_b, n_s):
    return t.reshape(n_b, n_s, DN_HEADS, DN_HEAD_DIM).transpose(0, 2, 1, 3)


def _unheads(t):
    n_b, _, n_s, _ = t.shape
    return t.transpose(0, 2, 1, 3).reshape(n_b, n_s, DN_WIDTH)


def _block_diag(blocks):
    n_per = S5_GROUPS // S5_BLOCKS
    _, n_r, n_c = blocks.shape
    b4 = blocks.reshape(S5_BLOCKS, n_per, n_r, n_c)
    eye = jnp.eye(n_per, dtype=blocks.dtype)
    return (b4[:, :, :, None, :] * eye[None, :, None, :, None]).reshape(S5_BLOCKS, n_per * n_r, n_per * n_c)


def _diag_blocks(mat, n_r, n_c):
    n_per = S5_GROUPS // S5_BLOCKS
    m5 = mat.reshape(S5_BLOCKS, n_per, n_r, n_per, n_c)
    eye = jnp.eye(n_per, dtype=mat.dtype)
    return jnp.sum(m5 * eye[None, :, None, :, None], axis=3).reshape(S5_GROUPS, n_r, n_c)


def _local_step(x, c, target, wts):
    n_b, n_s, _ = x.shape
    n_tok = n_b * n_s
    flat = lambda t: t.reshape(n_tok, t.shape[-1])
    unflat = lambda t: t.reshape(n_b, n_s, t.shape[-1])
    n_chunks = n_s // CHUNK

    c_pad = jnp.zeros((SUBLANES, D_MODEL), F32).at[:n_b].set(c)
    mod_s = _ada_fwd(c_pad, wts["w_ada"], wts["b_ada"], name="ada_fwd")
    mod = mod_s.transpose(1, 0, 2).reshape(SUBLANES, N_MOD * D_MODEL)[:n_b]
    sh1, sc1, gt1, sh2, sc2, gt2, sh3, sc3, gt3 = [m[:, None, :] for m in jnp.split(mod, N_MOD, axis=-1)]

    a1 = _pre(x, None, None, wts["g_ffn1"], sh1, sc1, 0.0, name="pre1")
    f1, ffn1_saved = _ffn_fwd(flat(a1), wts["w1_ffn1"], wts["w3_ffn1"], wts["w2_ffn1"], "ffn1")
    x1, a2 = _pre(x, unflat(f1), gt1, wts["g_mix"], sh2, sc2, 0.5, name="pre2")
    u = flat(a2)[None]
    p_qkv = _mm_act([(u, wts["w_qkv"])], "nt", name="in_qkv")[0]
    p_z = _mm_act([(u, wts["w_z"])], "nt", name="in_z")[0]
    p_gab = _mm_act([(u, wts["w_gab"])], "nt", name="in_gab")[0]
    p_s5 = _mm_act([(u, wts["w_s5"])], "nt", name="in_s5")[0]
    p_ba = _mm_act([(u, wts["w_ba"])], "nt", name="in_ba")[0]

    qkv_c = _conv_fwd(unflat(p_qkv), wts["conv_qkv"], name="conv_fwd")
    z_tok = unflat(p_z)
    ba = p_ba.reshape(n_b, n_s, BA_PAD)
    head_rows = lambda t: t.transpose(0, 2, 1).reshape(n_b, DN_HEADS, n_chunks, 1, CHUNK)
    blr = head_rows(ba[:, :, :DN_HEADS])
    lar = head_rows(ba[:, :, DN_HEADS:2 * DN_HEADS])
    a_log, dt_bias = wts["a_log"], wts["dt_bias"]
    dn_in = (qkv_c, blr, lar, a_log, dt_bias)
    qd, kd, uc, wc, attn, g_last, dn_inv = _dn_prep(*dn_in, name="dn_prep")
    o, states = _dn_scan(qd, kd, uc, wc, attn, g_last, name="dn_scan")
    og = _dn_post(o, z_tok, wts["g_onorm"], name="dn_post")
    og_t = og.reshape(1, n_tok, DN_WIDTH)
    ya = _mm_act([(og_t, wts["w_proj_a"])], "nn", name="proj_a")[0]

    s5p_in = (wts["lam_re"], wts["lam_im"], wts["log_step"], wts["bt_re"], wts["bt_im"], wts["c_im"])
    lb_re, lb_im, bb_re, bb_im, c_neg = _s5_prep(*s5p_in, name="s5_prep")
    wb_re, wb_im = _block_diag(bb_re), _block_diag(bb_im)
    wc_re = _block_diag(wts["c_re"].transpose(0, 2, 1))
    wc_im = _block_diag(c_neg.transpose(0, 2, 1))
    lbr, lbi = lb_re.reshape(1, S5_LANES), lb_im.reshape(1, S5_LANES)
    s5_w = (wb_re, wb_im, lbr, lbi, wc_re, wc_im)
    ymm, x_re, x_im = _s5_mix(unflat(p_s5), *s5_w, name="s5_mix")
    ymm = ymm.reshape(n_tok, S5_WIDTH)
    y2 = _s5_out(ymm, p_s5, wts["d_skip"], wts["w_glu"], wts["b_glu"], name="s5_out")
    yb = _mm_act([(y2[None], wts["w_proj_b"])], "nn", name="proj_b")[0]

    merged = _merge(p_gab, ya, yb, name="merge")
    m_out = _mm_act([(merged[None], wts["w_out"])], "nn", name="mix_out")[0]
    x2, a3 = _pre(x1, unflat(m_out), gt2, wts["g_ffn2"], sh3, sc3, 1.0, name="pre3")
    f3, ffn2_saved = _ffn_fwd(flat(a3), wts["w1_ffn2"], wts["w3_ffn2"], wts["w2_ffn2"], "ffn2")

    g = {}
    loss, dx2_res, df3, dgt3, g["g_final"] = _final(x2, unflat(f3), gt3, wts["g_final"], target, name="final")
    da3, g["w1_ffn2"], g["w3_ffn2"], g["w2_ffn2"] = _ffn_bwd(
        flat(a3), wts["w1_ffn2"], wts["w3_ffn2"], wts["w2_ffn2"], ffn2_saved, flat(df3), "ffn2")
    dx1_res, dm_out, dgt2, g["g_ffn2"], dsh3, dsc3 = _pre_bwd(
        x1, unflat(m_out), gt2, wts["g_ffn2"], sh3, sc3, 1.0, unflat(da3), dx2_res, name="pre3_bwd")
    dm_out = flat(dm_out)[None]
    dmerged = _mm_act([(dm_out, wts["w_out"])], "nt", name="mix_out_bwd")[0]
    g["w_out"] = _mm_tn(merged[None], dm_out, name="dw_out")[0]
    dga, dgb, dya, dyb = _merge_bwd(p_gab, ya, yb, dmerged, name="merge_bwd")

    dy2 = _mm_act([(dyb[None], wts["w_proj_b"])], "nt", name="proj_b_bwd")[0]
    g["w_proj_b"] = _mm_tn(y2[None], dyb[None], name="dw_proj_b")[0]
    dymm, du_skip, g["d_skip"], g["w_glu"], g["b_glu"] = _s5_out_bwd(
        ymm, p_s5, wts["d_skip"], wts["w_glu"], wts["b_glu"], dy2, name="s5_out_bwd")
    dp_s5, dwb_re, dwb_im, dlb_re, dlb_im, dwc_re, dwc_im = _s5_mix_bwd(
        unflat(dymm), unflat(du_skip), unflat(p_s5), x_re, x_im, *s5_w, name="s5_mix_bwd")
    dp_s5 = dp_s5.reshape(n_tok, S5_WIDTH)
    g["c_re"] = _diag_blocks(dwc_re, S5_STATE, S5_GROUP_CH).transpose(0, 2, 1)
    s5_cts = (dlb_re.reshape(lb_re.shape), dlb_im.reshape(lb_im.shape),
              _diag_blocks(dwb_re, S5_GROUP_CH, S5_STATE), _diag_blocks(dwb_im, S5_GROUP_CH, S5_STATE),
              _diag_blocks(dwc_im, S5_STATE, S5_GROUP_CH).transpose(0, 2, 1))
    g["lam_re"], g["lam_im"], g["log_step"], g["bt_re"], g["bt_im"], g["c_im"] = _s5_prep_bwd(
        *s5p_in, s5_cts, name="s5_prep_bwd")

    dog = _mm_act([(dya[None], wts["w_proj_a"])], "nt", name="proj_a_bwd")[0]
    g["w_proj_a"] = _mm_tn(og_t, dya[None], name="dw_proj_a")[0]
    do, dz, g["g_onorm"] = _dn_post_bwd(o, z_tok, wts["g_onorm"], unflat(dog), name="dn_post_bwd")
    scan_cts = _dn_scan_bwd(qd, kd, uc, wc, attn, g_last, states, do, name="dn_scan_bwd")
    dqkv_c, dblr, dlar, g["a_log"], g["dt_bias"] = _dn_prep_bwd(*dn_in, dn_inv, uc, wc, scan_cts, name="dn_prep_bwd")
    dqkv, g["conv_qkv"] = _conv_bwd(unflat(p_qkv), wts["conv_qkv"], dqkv_c, name="conv_bwd")
    token_cols = lambda t: t.reshape(n_b, DN_HEADS, n_s).transpose(0, 2, 1)
    dba = jnp.concatenate([token_cols(dblr), token_cols(dlar),
                           jnp.zeros((n_b, n_s, BA_PAD - 2 * DN_HEADS), F32)], axis=-1).astype(MXU_DTYPE)

    dps = {"w_qkv": flat(dqkv)[None], "w_z": flat(dz)[None], "w_ga": dga[None], "w_gb": dgb[None],
           "w_s5": dp_s5[None], "w_ba": flat(dba)[None]}
    w_ga, w_gb = wts["w_gab"][:, :D_MODEL], wts["w_gab"][:, D_MODEL:]
    w_of = dict(wts, w_ga=w_ga, w_gb=w_gb)
    du = _mm_act([(dps[k], w_of[k]) for k in dps], "nn", name="in_bwd")[0]
    for k in dps:
        g[k] = _mm_tn(dps[k], u, name=f"d{k}")[0]
    dx0_res, df1, dgt1, g["g_mix"], dsh2, dsc2 = _pre_bwd(
        x, unflat(f1), gt1, wts["g_mix"], sh2, sc2, 0.5, unflat(du), dx1_res, name="pre2_bwd")
    da1, g["w1_ffn1"], g["w3_ffn1"], g["w2_ffn1"] = _ffn_bwd(
        flat(a1), wts["w1_ffn1"], wts["w3_ffn1"], wts["w2_ffn1"], ffn1_saved, flat(df1), "ffn1")
    grad_x, g["g_ffn1"], dsh1, dsc1 = _pre_bwd(
        x, None, None, wts["g_ffn1"], sh1, sc1, 0.0, unflat(da1), dx0_res, name="pre1_bwd")

    dmod = jnp.concatenate([t[:, 0, :] for t in (dsh1, dsc1, dgt1, dsh2, dsc2, dgt2, dsh3, dsc3, dgt3)], axis=-1)
    return loss, grad_x, g, dmod


def _ada_grads(c_rows, dmod_rows):
    n_r = c_rows.shape[0]
    n_pad = -n_r % SUBLANES
    c_pad = jnp.pad(c_rows, ((0, n_pad), (0, 0)))
    dmod_s = jnp.pad(dmod_rows, ((0, n_pad), (0, 0))).reshape(n_r + n_pad, N_SHARD, ADA_SHARD).transpose(1, 0, 2)
    dw, db = _ada_bwd(c_pad, dmod_s, name="ada_bwd")
    return dw, db.reshape(1, N_MOD * D_MODEL)


IN_SPLITS = (("w_qkv", 3 * DN_WIDTH), ("w_z", DN_WIDTH), ("w_ba", 2 * DN_HEADS), ("w_s5", S5_WIDTH),
             ("w_ga", D_MODEL), ("w_gb", D_MODEL))
SHARDED = ("w_ada", "w1_ffn1", "w3_ffn1", "w2_ffn1", "w_in", "conv_qkv", "w_glu", "w_proj_a", "w_proj_b", "w_out",
           "w1_ffn2", "w3_ffn2", "w2_ffn2")
COLUMN_SHARDED = ("w_ada", "w1_ffn1", "w3_ffn1", "w_in", "conv_qkv", "w_proj_a", "w_proj_b", "w1_ffn2", "w3_ffn2")


def _cat_columns(stack):
    return stack.transpose(1, 0, 2).reshape(stack.shape[1], N_SHARD * stack.shape[2])


def _split_columns(full):
    n_r, n_c = full.shape
    return full.reshape(n_r, N_SHARD, n_c // N_SHARD).transpose(1, 0, 2)


def _gathered_weights(st, rep):
    w = {k: st[k] for k in ("w_ada", "w1_ffn1", "w3_ffn1", "w2_ffn1", "w1_ffn2", "w3_ffn2", "w2_ffn2")}
    w["b_ada"] = rep["b_ada"].reshape(N_SHARD, 1, ADA_SHARD)
    for k in ("g_ffn1", "g_mix", "g_ffn2", "g_final"):
        w[k] = rep[k].reshape(1, D_MODEL)
    w_in_t = st["w_in"].reshape(N_SHARD * st["w_in"].shape[1], D_MODEL)
    start = 0
    for k, size in IN_SPLITS:
        w[k] = w_in_t[None, start:start + size]
        start += size
    w["w_gab"] = jnp.concatenate([w.pop("w_ga"), w.pop("w_gb")], axis=1)
    w["w_ba"] = jnp.pad(w["w_ba"], ((0, 0), (0, BA_PAD - 2 * DN_HEADS), (0, 0)))
    w["conv_qkv"] = _cat_columns(st["conv_qkv"])
    w["a_log"] = rep["a_log"].reshape(DN_HEADS, 1, 1)
    w["dt_bias"] = rep["dt_bias"].reshape(DN_HEADS, 1, 1)
    w["g_onorm"] = rep["g_onorm"].reshape(1, DN_HEAD_DIM)
    w["lam_re"] = rep["lam_re"].reshape(S5_GROUPS, 1, S5_STATE)
    w["lam_im"] = rep["lam_im"].reshape(S5_GROUPS, 1, S5_STATE)
    w["log_step"] = rep["log_step"].reshape(S5_GROUPS, 1, 1)
    w["bt_re"] = rep["b_re"][0].transpose(0, 2, 1)
    w["bt_im"] = rep["b_im"][0].transpose(0, 2, 1)
    w["c_re"] = rep["c_re"][0]
    w["c_im"] = rep["c_im"][0]
    w["d_skip"] = rep["d_skip"].reshape(1, S5_WIDTH)
    w["b_glu"] = rep["b_glu"].reshape(1, S5_WIDTH)
    w["w_glu"] = st["w_glu"].reshape(S5_WIDTH, S5_WIDTH)
    w["w_proj_a"] = _cat_columns(st["w_proj_a"])[None]
    w["w_proj_b"] = _cat_columns(st["w_proj_b"])[None]
    w["w_out"] = st["w_out"].reshape(1, D_MODEL, D_MODEL)
    return w


def _grads_to_problem_layout(g):
    st = {k: g[k] for k in ("w1_ffn1", "w3_ffn1", "w2_ffn1", "w1_ffn2", "w3_ffn2", "w2_ffn2")}
    w_in_t = jnp.concatenate([g[k][:size] for k, size in IN_SPLITS], axis=0)
    st["w_in"] = w_in_t.reshape(N_SHARD, w_in_t.shape[0] // N_SHARD, D_MODEL)
    st["w_glu"] = g["w_glu"].reshape(N_SHARD, S5_WIDTH // N_SHARD, S5_WIDTH)
    st["w_proj_a"] = _split_columns(g["w_proj_a"])
    st["w_proj_b"] = _split_columns(g["w_proj_b"])
    st["w_out"] = g["w_out"].reshape(N_SHARD, D_MODEL // N_SHARD, D_MODEL)
    small = {
        "g_ffn1": g["g_ffn1"], "g_mix": g["g_mix"], "g_ffn2": g["g_ffn2"], "g_final": g["g_final"].reshape(D_MODEL),
        "conv_qkv": g["conv_qkv"][None],
        "a_log": g["a_log"].reshape(1, DN_HEADS), "dt_bias": g["dt_bias"].reshape(1, DN_HEADS),
        "g_onorm": g["g_onorm"],
        "lam_re": g["lam_re"].reshape(1, S5_GROUPS, S5_STATE), "lam_im": g["lam_im"].reshape(1, S5_GROUPS, S5_STATE),
        "log_step": g["log_step"].reshape(1, S5_GROUPS),
        "b_re": g["bt_re"].transpose(0, 2, 1)[None], "b_im": g["bt_im"].transpose(0, 2, 1)[None],
        "c_re": g["c_re"][None], "c_im": g["c_im"][None],
        "d_skip": g["d_skip"], "b_glu": g["b_glu"],
    }
    return st, small


ELEMENTWISE_BLOCK_BYTES = 1 << 20


def _row_tile(n_rows, n_cols, n_lead=1, multiple=SUBLANES):
    best = None
    for t in range(multiple, n_rows + 1, multiple):
        if n_rows % t == 0 and n_lead * t * n_cols * 4 <= ELEMENTWISE_BLOCK_BYTES:
            best = t
    return best if best is not None else n_rows


def _add_sibling_half(g4, recv, my_c, *, name):
    n_sh, _, n_h, n_c = g4.shape
    th = _row_tile(n_h, n_c, multiple=2 * SUBLANES)

    def body(c_ref, g_ref, r_ref, o_ref):
        o_ref[0] = (g_ref[0, 0] + r_ref[0]).astype(o_ref.dtype)

    grid_spec = pltpu.PrefetchScalarGridSpec(
        num_scalar_prefetch=1, grid=(n_sh, n_h // th),
        in_specs=[pl.BlockSpec((1, 1, th, n_c), lambda s, i, c_ref: (s, c_ref[0], i, 0)),
                  pl.BlockSpec((1, th, n_c), lambda s, i, c_ref: (s, i, 0))],
        out_specs=pl.BlockSpec((1, th, n_c), lambda s, i, c_ref: (s, i, 0)))
    return pl.pallas_call(
        body, name=name, grid_spec=grid_spec, out_shape=jax.ShapeDtypeStruct((n_sh, n_h, n_c), MXU_DTYPE),
        compiler_params=_params(3 * _nbytes((th, n_c), F32)),
    )(*_hbm(my_c, g4, recv))


def _sum_slots(parts, *, name):
    n_p, n_r, n_c = parts.shape
    th = _row_tile(n_r, n_c, n_p)

    def body(p_ref, o_ref):
        total = p_ref[0].astype(F32)
        for k in range(1, n_p):
            total = total + p_ref[k].astype(F32)
        o_ref[...] = total

    return pl.pallas_call(
        body, name=name, grid=(n_r // th,),
        in_specs=[pl.BlockSpec((n_p, th, n_c), lambda i: (0, i, 0))],
        out_specs=pl.BlockSpec((th, n_c), lambda i: (i, 0)),
        out_shape=jax.ShapeDtypeStruct((n_r, n_c), F32),
        compiler_params=_params((n_p + 1) * _nbytes((th, n_c), F32)),
    )(*_hbm(parts))


def _cast_into_slot(w, place, dtype, *, name):
    n_r, n_c = w.shape
    th = _row_tile(n_r, n_c, multiple=2 * SUBLANES)

    def body(p_ref, w_ref, o_ref):
        o_ref[0] = w_ref[...].astype(o_ref.dtype)

    grid_spec = pltpu.PrefetchScalarGridSpec(
        num_scalar_prefetch=1, grid=(n_r // th,),
        in_specs=[pl.BlockSpec((th, n_c), lambda i, p: (i, 0))],
        out_specs=pl.BlockSpec((1, th, n_c), lambda i, p: (p[1], i, 0)))
    return pl.pallas_call(
        body, name=name, grid_spec=grid_spec, out_shape=jax.ShapeDtypeStruct((N_SHARD, n_r, n_c), dtype),
        compiler_params=_params(2 * _nbytes((th, n_c), F32)),
    )(*_hbm(place, w))


def _sum_chips(own, parts, place, *, name):
    n_sh, n_h, n_c = own.shape
    th = _row_tile(n_h, n_c, n_sh, multiple=2 * SUBLANES)

    def body(p_ref, own_ref, a_ref, b_ref, c_ref, o_ref):
        o_ref[0] = ((own_ref[0].astype(F32) + a_ref[0].astype(F32)) + b_ref[0].astype(F32)) + c_ref[0].astype(F32)

    slab = lambda k: pl.BlockSpec((1, th, n_c), lambda i, p, k=k: (p[k], i, 0))
    grid_spec = pltpu.PrefetchScalarGridSpec(
        num_scalar_prefetch=1, grid=(n_h // th,),
        in_specs=[slab(1), slab(2), slab(3), slab(4)], out_specs=slab(0))
    return pl.pallas_call(
        body, name=name, grid_spec=grid_spec, out_shape=jax.ShapeDtypeStruct((2, n_h, n_c), F32),
        compiler_params=_params(5 * _nbytes((th, n_c), F32)),
    )(*_hbm(place, own, parts, parts, parts))


def _adamw(w, g, m, v, *, name):
    n_r, n_c = w.shape
    th = _row_tile(n_r, n_c)
    tc = n_c
    if th == n_r and n_c % LANES == 0:
        tc = max(t for t in range(LANES, n_c + 1, LANES) if n_c % t == 0 and (n_r * t * 4 <= ELEMENTWISE_BLOCK_BYTES or t == LANES))
    bias1 = 1.0 - ADAM_B1 ** ADAM_STEP
    bias2 = 1.0 - ADAM_B2 ** ADAM_STEP

    def body(w_ref, g_ref, m_ref, v_ref, d_ref, mo_ref, vo_ref):
        gv = g_ref[...]
        m_new = ADAM_B1 * m_ref[...] + (1.0 - ADAM_B1) * gv
        v_new = ADAM_B2 * v_ref[...] + (1.0 - ADAM_B2) * jnp.square(gv)
        d_ref[...] = -ADAM_LR * ((m_new / bias1) / (jnp.sqrt(v_new / bias2) + ADAM_EPS) + ADAM_WD * w_ref[...])
        mo_ref[...] = m_new
        vo_ref[...] = v_new

    spec = pl.BlockSpec((th, tc), lambda i, j: (i, j))
    shape = jax.ShapeDtypeStruct((n_r, n_c), F32)
    return pl.pallas_call(
        body, name=name, grid=(n_r // th, n_c // tc), in_specs=[spec] * 4, out_specs=(spec,) * 3, out_shape=(shape,) * 3,
        compiler_params=_params(7 * _nbytes((th, tc), F32)),
    )(*_hbm(w, g, m, v))


CHIP_FLIPS = ((1, 0), (0, 1), (1, 1))
DEVICE_FLIPS = tuple((fx, fy, fc) for fx in (0, 1) for fy in (0, 1) for fc in (0, 1))[1:]


def _exchange(ins, out_shapes, plan, n_local, n_remote, *, name, aliased=False):
    n_in, n_out = len(ins), len(out_shapes)

    def body(*refs):
        in_refs, out_refs = refs[:n_in], refs[n_in:n_in + n_out]
        send_sems, recv_sems, local_sems = refs[n_in + n_out:]
        me = (lax.axis_index("x"), lax.axis_index("y"), lax.axis_index("c"))
        local, remote = plan(in_refs, out_refs, me)
        assert len(local) == n_local and len(remote) == n_remote
        here = [pltpu.make_async_copy(src, dst, local_sems.at[i]) for i, (src, dst) in enumerate(local)]
        for cp in here:
            cp.start()
        sends = [pltpu.make_async_remote_copy(src_ref=src, dst_ref=dst, send_sem=send_sems.at[i], recv_sem=recv_sems.at[i],
                                              device_id=peer, device_id_type=pl.DeviceIdType.MESH)
                 for i, (src, dst, _, peer) in enumerate(remote)]
        for cp in sends:
            cp.start()
        for i, (src, _, landing, peer) in enumerate(remote):
            pltpu.make_async_remote_copy(src_ref=src, dst_ref=landing, send_sem=send_sems.at[i], recv_sem=recv_sems.at[i],
                                         device_id=peer, device_id_type=pl.DeviceIdType.MESH).wait_recv()
        for cp in sends:
            cp.wait_send()
        for cp in here:
            cp.wait()

    any_spec = pl.BlockSpec(memory_space=pl.ANY)
    return pl.pallas_call(
        body, name=name, in_specs=[any_spec] * n_in, out_specs=tuple([any_spec] * n_out), out_shape=tuple(out_shapes),
        scratch_shapes=[pltpu.SemaphoreType.DMA((n_remote,)), pltpu.SemaphoreType.DMA((n_remote,)),
                        pltpu.SemaphoreType.DMA((max(n_local, 1),))],
        input_output_aliases={k: k for k in range(n_in)} if aliased else {},
    )(*ins)


def _gather_shards(stacks, *, name):
    n = len(stacks)
    halved = [a.shape[1] % 32 == 0 for a in stacks]
    n_ici = len(CHIP_FLIPS) * n
    n_pass = len(CHIP_FLIPS) * sum(halved)

    def body(*refs):
        outs = refs[n:2 * n]
        send_sems, recv_sems = refs[2 * n:]
        x, y, c = lax.axis_index("x"), lax.axis_index("y"), lax.axis_index("c")
        mine = 2 * x + y

        def rows(k, slot, half):
            if not halved[k]:
                return outs[k].at[slot]
            n_h = stacks[k].shape[1] // 2
            return outs[k].at[slot, pl.ds(pl.multiple_of(half * n_h, 16), n_h)]

        def copy(i, src, dst, peer):
            return pltpu.make_async_remote_copy(src_ref=src, dst_ref=dst, send_sem=send_sems.at[i], recv_sem=recv_sems.at[i],
                                                device_id=peer, device_id_type=pl.DeviceIdType.MESH)

        started = []
        for j, (fx, fy) in enumerate(CHIP_FLIPS):
            for k in range(n):
                cp = copy(j * n + k, rows(k, mine, c), rows(k, mine, c), (x ^ fx, y ^ fy, c))
                cp.start()
                started.append(cp)
        i_pass = n_ici
        expect = []
        for j, (fx, fy) in enumerate(CHIP_FLIPS):
            peer_chip = 2 * (x ^ fx) + (y ^ fy)
            for k in range(n):
                landed = rows(k, peer_chip, c)
                copy(j * n + k, landed, landed, (x ^ fx, y ^ fy, c)).wait_recv()
                if halved[k]:
                    cp = copy(i_pass, landed, landed, (x, y, 1 - c))
                    cp.start()
                    started.append(cp)
                    expect.append((i_pass, rows(k, peer_chip, 1 - c)))
                    i_pass += 1
        for i, landing in expect:
            copy(i, landing, landing, (x, y, 1 - c)).wait_recv()
        for cp in started:
            cp.wait_send()

    any_spec = pl.BlockSpec(memory_space=pl.ANY)
    n_sem = n_ici + n_pass
    return pl.pallas_call(
        body, name=name, in_specs=[any_spec] * n, out_specs=tuple([any_spec] * n),
        out_shape=tuple(jax.ShapeDtypeStruct(a.shape, a.dtype) for a in stacks),
        scratch_shapes=[pltpu.SemaphoreType.DMA((n_sem,)), pltpu.SemaphoreType.DMA((n_sem,))],
        input_output_aliases={k: k for k in range(n)},
    )(*stacks)


def _swap_sibling_halves(g4s, *, name):
    n = len(g4s)

    def plan(in_refs, out_refs, me):
        x, y, c = me
        remote = [(in_refs[k].at[:, 1 - c], out_refs[k], out_refs[k], (x, y, 1 - c)) for k in range(n)]
        return [], remote

    shapes = [jax.ShapeDtypeStruct((a.shape[0],) + a.shape[2:], a.dtype) for a in g4s]
    return _exchange(g4s, shapes, plan, 0, n, name=name)


def _scatter_to_chips(hs, *, name):
    n = len(hs)

    def plan(in_refs, out_refs, me):
        x, y, c = me
        mine = 2 * x + y
        remote = []
        for fx, fy in CHIP_FLIPS:
            px, py = x ^ fx, y ^ fy
            peer = 2 * px + py
            for k in range(n):
                remote.append((in_refs[k].at[peer], out_refs[k].at[mine], out_refs[k].at[peer], (px, py, c)))
        return [], remote

    shapes = [jax.ShapeDtypeStruct(a.shape, a.dtype) for a in hs]
    return _exchange(hs, shapes, plan, 0, len(CHIP_FLIPS) * n, name=name)


def _join_sibling_halves(rs, *, name):
    n = len(rs)

    def plan(in_refs, out_refs, me):
        x, y, c = me
        remote = [(out_refs[k].at[c], out_refs[k].at[c], out_refs[k].at[1 - c], (x, y, 1 - c)) for k in range(n)]
        return [], remote

    shapes = [jax.ShapeDtypeStruct(a.shape, a.dtype) for a in rs]
    return _exchange(rs, shapes, plan, 0, n, name=name, aliased=True)


def _gather_all_devices(packed, *, name):
    def plan(in_refs, out_refs, me):
        x, y, c = me
        mine = 4 * x + 2 * y + c
        remote = []
        for fx, fy, fc in DEVICE_FLIPS:
            px, py, pc = x ^ fx, y ^ fy, c ^ fc
            remote.append((in_refs[0], out_refs[0].at[mine], out_refs[0].at[4 * px + 2 * py + pc], (px, py, pc)))
        return [(in_refs[0], out_refs[0].at[mine])], remote

    shape = jax.ShapeDtypeStruct((2 * N_SHARD,) + packed.shape, packed.dtype)
    return _exchange([packed], [shape], plan, 1, len(DEVICE_FLIPS), name=name)[0]


WEIGHT_NAMES = ("w_ada", "b_ada", "g_ffn1", "w1_ffn1", "w3_ffn1", "w2_ffn1", "g_mix", "w_in", "conv_qkv", "a_log",
                "dt_bias", "g_onorm", "lam_re", "lam_im", "log_step", "b_re", "b_im", "c_re", "c_im", "d_skip", "w_glu",
                "b_glu", "w_proj_a", "w_proj_b", "w_out", "g_ffn2", "w1_ffn2", "w3_ffn2", "w2_ffn2", "g_final")
LARGE = tuple(n for n in SHARDED if n != "conv_qkv")
SMALL = tuple(n for n in WEIGHT_NAMES if n not in LARGE)
REDUCED_LARGE = tuple(n for n in LARGE if n != "w_ada")
REDUCED_SMALL = tuple(n for n in SMALL if n != "b_ada")
PACK_ROW = SUBLANES * LANES


def _pack(arrays):
    flat = jnp.concatenate([a.reshape(-1) for a in arrays])
    n_pad = -flat.shape[0] % PACK_ROW
    return jnp.pad(flat, (0, n_pad)).reshape(-1, LANES)


def _unpack(packed, shapes):
    flat = packed.reshape(-1)
    out, start = [], 0
    for s in shapes:
        size = math.prod(s)
        out.append(flat[start:start + size].reshape(s))
        start += size
    return out


def _unpack_slots(gathered, shapes):
    flat = gathered.reshape(gathered.shape[0], -1)
    out, start = [], 0
    for s in shapes:
        size = math.prod(s)
        out.append(flat[:, start:start + size].reshape((gathered.shape[0],) + tuple(s)))
        start += size
    return out


TRANSPOSED = ("w1_ffn1", "w3_ffn1", "w1_ffn2", "w3_ffn2", "w_in")


def _to_internal(name, a):
    return jnp.swapaxes(a[0], 0, 1) if name in TRANSPOSED else a[0]


def _from_internal(name, a):
    return (jnp.swapaxes(a, 0, 1) if name in TRANSPOSED else a)[None]


def _step(x, c, target, weights, m_in, v_in):
    xi, yi, ci = lax.axis_index("x"), lax.axis_index("y"), lax.axis_index("c")
    my_chip = 2 * xi + yi

    others = [k + (k >= my_chip).astype(jnp.int32) for k in range(N_SHARD - 1)]
    place = jnp.stack([ci, my_chip] + others).astype(jnp.int32)

    slots = [_cast_into_slot(_to_internal(n, weights[n]), place, F32 if n == "conv_qkv" else MXU_DTYPE, name=f"cast_{n}")
             for n in SHARDED]
    stacks = dict(zip(SHARDED, _gather_shards(slots, name="gather_weights")))
    rep = {n: weights[n] for n in WEIGHT_NAMES if n not in SHARDED}
    loss, grad_x, g, dmod = _local_step(x, c, target, _gathered_weights(stacks, rep))
    g_stacks, g_small = _grads_to_problem_layout(g)

    g4s = [g_stacks[n].reshape(N_SHARD, 2, g_stacks[n].shape[1] // 2, g_stacks[n].shape[2]) for n in REDUCED_LARGE]
    from_sibling = _swap_sibling_halves(g4s, name="swap_sibling_halves")
    chip_sums = [_add_sibling_half(a, r, place, name=f"chip_sum_{n}") for n, a, r in zip(REDUCED_LARGE, g4s, from_sibling)]
    from_chips = _scatter_to_chips(chip_sums, name="scatter_to_chips")
    reduced = [_sum_chips(h, p, place, name=f"sum_chips_{n}") for n, h, p in zip(REDUCED_LARGE, chip_sums, from_chips)]
    joined = _join_sibling_halves(reduced, name="join_sibling_halves")
    grads_2d = {n: j.reshape(2 * j.shape[1], j.shape[2]) for n, j in zip(REDUCED_LARGE, joined)}
    grads = {n: _from_internal(n, a) for n, a in grads_2d.items()}

    summed_shapes = [g_small[n].shape for n in REDUCED_SMALL] + [(1, 1)]
    packed = _pack([g_small[n] for n in REDUCED_SMALL] + [loss, c, dmod])
    gathered = _gather_all_devices(packed, name="gather_small")
    *small_grads, loss_sum = _unpack(_sum_slots(gathered, name="sum_small"), summed_shapes)
    grads.update(zip(REDUCED_SMALL, small_grads))
    n_conv = weights["conv_qkv"].shape[-1]
    grads["conv_qkv"] = lax.dynamic_slice_in_dim(grads["conv_qkv"], my_chip * n_conv, n_conv, axis=2)
    n_dev = gathered.shape[0]
    rows_of = lambda t: t.reshape(n_dev * t.shape[1], t.shape[2])
    _, c_all, dmod_all = _unpack_slots(gathered, [(sum(math.prod(s) for s in summed_shapes),), c.shape, dmod.shape])
    dw_ada, grads["b_ada"] = _ada_grads(rows_of(c_all), rows_of(dmod_all))
    grads_2d["w_ada"] = lax.dynamic_index_in_dim(dw_ada, my_chip, axis=0, keepdims=False)
    grads["w_ada"] = grads_2d["w_ada"][None]

    delta, new_m, new_v = {}, {}, {}
    grads_2d["conv_qkv"] = grads["conv_qkv"][0]
    for n in LARGE + ("conv_qkv",):
        outs = _adamw(_to_internal(n, weights[n]), grads_2d[n], _to_internal(n, m_in[n]), _to_internal(n, v_in[n]),
                      name=f"adamw_{n}")
        delta[n], new_m[n], new_v[n] = [_from_internal(n, o) for o in outs]
    packed_names = tuple(n for n in SMALL if n != "conv_qkv")
    shapes = [weights[n].shape for n in packed_names]
    outs = _adamw(*[_pack([d[n] for n in packed_names]) for d in (weights, grads, m_in, v_in)], name="adamw_small")
    for d, o in zip((delta, new_m, new_v), outs):
        d.update(zip(packed_names, _unpack(o, shapes)))
    return (loss_sum.reshape(()), grad_x, *[grads[n] for n in WEIGHT_NAMES], *[delta[n] for n in WEIGHT_NAMES],
            *[new_m[n] for n in WEIGHT_NAMES], *[new_v[n] for n in WEIGHT_NAMES])


def kernel(x, c, w_ada, b_ada, g_ffn1, w1_ffn1, w3_ffn1, w2_ffn1, g_mix, w_in, conv_qkv, a_log, dt_bias, g_onorm, lam_re, lam_im, log_step, b_re, b_im, c_re, c_im, d_skip, w_glu, b_glu, w_proj_a, w_proj_b, w_out, g_ffn2, w1_ffn2, w3_ffn2, w2_ffn2, g_final, loss_target, m_w_ada, m_b_ada, m_g_ffn1, m_w1_ffn1, m_w3_ffn1, m_w2_ffn1, m_g_mix, m_w_in, m_conv_qkv, m_a_log, m_dt_bias, m_g_onorm, m_lam_re, m_lam_im, m_log_step, m_b_re, m_b_im, m_c_re, m_c_im, m_d_skip, m_w_glu, m_b_glu, m_w_proj_a, m_w_proj_b, m_w_out, m_g_ffn2, m_w1_ffn2, m_w3_ffn2, m_w2_ffn2, m_g_final, v_w_ada, v_b_ada, v_g_ffn1, v_w1_ffn1, v_w3_ffn1, v_w2_ffn1, v_g_mix, v_w_in, v_conv_qkv, v_a_log, v_dt_bias, v_g_onorm, v_lam_re, v_lam_im, v_log_step, v_b_re, v_b_im, v_c_re, v_c_im, v_d_skip, v_w_glu, v_b_glu, v_w_proj_a, v_w_proj_b, v_w_out, v_g_ffn2, v_w1_ffn2, v_w3_ffn2, v_w2_ffn2, v_g_final):
    w_vals = (w_ada, b_ada, g_ffn1, w1_ffn1, w3_ffn1, w2_ffn1, g_mix, w_in, conv_qkv, a_log, dt_bias, g_onorm, lam_re, lam_im, log_step, b_re, b_im, c_re, c_im, d_skip, w_glu, b_glu, w_proj_a, w_proj_b, w_out, g_ffn2, w1_ffn2, w3_ffn2, w2_ffn2, g_final)
    m_vals = (m_w_ada, m_b_ada, m_g_ffn1, m_w1_ffn1, m_w3_ffn1, m_w2_ffn1, m_g_mix, m_w_in, m_conv_qkv, m_a_log, m_dt_bias, m_g_onorm, m_lam_re, m_lam_im, m_log_step, m_b_re, m_b_im, m_c_re, m_c_im, m_d_skip, m_w_glu, m_b_glu, m_w_proj_a, m_w_proj_b, m_w_out, m_g_ffn2, m_w1_ffn2, m_w3_ffn2, m_w2_ffn2, m_g_final)
    v_vals = (v_w_ada, v_b_ada, v_g_ffn1, v_w1_ffn1, v_w3_ffn1, v_w2_ffn1, v_g_mix, v_w_in, v_conv_qkv, v_a_log, v_dt_bias, v_g_onorm, v_lam_re, v_lam_im, v_log_step, v_b_re, v_b_im, v_c_re, v_c_im, v_d_skip, v_w_glu, v_b_glu, v_w_proj_a, v_w_proj_b, v_w_out, v_g_ffn2, v_w1_ffn2, v_w3_ffn2, v_w2_ffn2, v_g_final)
    return _step(x, c, loss_target, dict(zip(WEIGHT_NAMES, w_vals)), dict(zip(WEIGHT_NAMES, m_vals)),
                 dict(zip(WEIGHT_NAMES, v_vals)))
```

```python
import functools
import math

import jax
import jax.numpy as jnp
from jax import lax
from jax.experimental import pallas as pl
from jax.experimental.pallas import tpu as pltpu

F32 = jnp.float32
BF16 = jnp.bfloat16
MXU_DTYPE = BF16

D_MODEL = 1024
D_FF = 2816
DN_HEADS = 8
DN_HEAD_DIM = 64
DN_WIDTH = DN_HEADS * DN_HEAD_DIM
CONV_WIDTH = 4
CHUNK = 64
S5_GROUP_CH = 16
S5_GROUPS = 32
S5_WIDTH = S5_GROUPS * S5_GROUP_CH
S5_STATE = 64
S5_LANES = S5_GROUPS * S5_STATE
N_MOD = 9
EPS = 1e-6
N_SHARD = 4
FF_SHARD = D_FF // N_SHARD
BA_PAD = 128

ADAM_LR = 0.001
ADAM_B1 = 0.9
ADAM_B2 = 0.999
ADAM_EPS = 1e-08
ADAM_WD = 0.01
ADAM_STEP = 10

VMEM_BYTES_V7X = 64 * 1024 * 1024
SUBLANES = 8
LANES = 128


def _params(block_bytes, extra_bytes=0):
    need = 2 * block_bytes + extra_bytes + (4 << 20)
    return pltpu.CompilerParams(vmem_limit_bytes=int(min(max(need, 16 << 20), VMEM_BYTES_V7X - (8 << 20))))


def _nbytes(shape, dtype):
    return math.prod(shape) * jnp.dtype(dtype).itemsize


HBM_OPERAND_BYTES = 1 << 20


def _hbm(*args):
    return [pltpu.with_memory_space_constraint(a, pltpu.HBM) if _nbytes(a.shape, a.dtype) >= HBM_OPERAND_BYTES else a
            for a in args]


_NN = (((1,), (0,)), ((), ()))
_NT = (((1,), (1,)), ((), ()))
_TN = (((0,), (0,)), ((), ()))


LHS_ROW_BYTES = 4096


def _mm_act(pairs, mode, *, name, out_sharded=False, reduce_shards=False, out_dtype=F32, add=None, tm=None):
    n_tok = pairs[0][0].shape[1]
    n_out = pairs[0][1].shape[2] if mode == "nn" else pairs[0][1].shape[1]
    if tm is None:
        row_bytes = sum(a.shape[2] * jnp.dtype(a.dtype).itemsize for a, _ in pairs)
        tm = 1024 if row_bytes <= LHS_ROW_BYTES else 512
    tm = min(tm, n_tok)
    tn = n_out if n_out <= 1536 else 1024
    assert n_tok % tm == 0 and n_out % tn == 0
    n_so = N_SHARD if out_sharded else 1
    n_red = N_SHARD if reduce_shards else 1
    grid = (n_so, n_tok // tm, n_out // tn, n_red)
    dims = _NN if mode == "nn" else _NT

    def shard_of(n_sh):
        if n_sh == 1:
            return lambda s, r: 0
        return (lambda s, r: s) if out_sharded else (lambda s, r: r)

    in_specs, args, blk = [], [], 0
    for a, b in pairs:
        k_dim = a.shape[2]
        sa, sb = shard_of(a.shape[0]), shard_of(b.shape[0])
        in_specs.append(pl.BlockSpec((1, tm, k_dim), lambda s, i, j, r, sa=sa: (sa(s, r), i, 0)))
        if mode == "nn":
            assert b.shape[1] == k_dim
            in_specs.append(pl.BlockSpec((1, k_dim, tn), lambda s, i, j, r, sb=sb: (sb(s, r), 0, j)))
        else:
            assert b.shape[2] == k_dim
            in_specs.append(pl.BlockSpec((1, tn, k_dim), lambda s, i, j, r, sb=sb: (sb(s, r), j, 0)))
        args += [a, b]
        blk += _nbytes((tm, k_dim), a.dtype) + _nbytes((k_dim, tn), b.dtype)
    if add is not None:
        in_specs.append(pl.BlockSpec((1, tm, tn), lambda s, i, j, r: (s, i, j)))
        args.append(add)
        blk += _nbytes((tm, tn), F32)
    blk += _nbytes((tm, tn), out_dtype)
    n_pairs = len(pairs)

    def body(*refs):
        out_ref = refs[2 * n_pairs + (add is not None)]
        acc = None
        for k in range(n_pairs):
            a = refs[2 * k][0].astype(MXU_DTYPE)
            b = refs[2 * k + 1][0].astype(MXU_DTYPE)
            d = lax.dot_general(a, b, dims, preferred_element_type=F32)
            acc = d if acc is None else acc + d

        def finish(total):
            if add is not None:
                total = total + refs[2 * n_pairs][0]
            out_ref[0] = total.astype(out_dtype)

        if n_red == 1:
            finish(acc)
        else:
            acc_ref = refs[-1]
            r = pl.program_id(3)

            @pl.when(r == 0)
            def _():
                acc_ref[...] = acc

            @pl.when(r > 0)
            def _():
                acc_ref[...] += acc

            @pl.when(r == n_red - 1)
            def _():
                finish(acc_ref[...])

    return pl.pallas_call(
        body,
        name=name,
        grid=grid,
        in_specs=in_specs,
        out_specs=pl.BlockSpec((1, tm, tn), lambda s, i, j, r: (s, i, j)),
        out_shape=jax.ShapeDtypeStruct((n_so, n_tok, n_out), out_dtype),
        scratch_shapes=[pltpu.VMEM((tm, tn), F32)] if n_red > 1 else [],
        compiler_params=_params(blk, 3 * _nbytes((tm, tn), F32)),
    )(*_hbm(*args))


def _mm_tn(a, b, *, name, tt=1024):
    n_tok, k_dim = a.shape[1], a.shape[2]
    n_out = b.shape[2]
    tt = min(tt, n_tok)
    tk = k_dim if k_dim <= 1536 else 1024
    tn = n_out if n_out <= 1536 else 1024
    assert n_tok % tt == 0 and k_dim % tk == 0 and n_out % tn == 0
    n_so = max(a.shape[0], b.shape[0])
    sa = (lambda s: s) if a.shape[0] > 1 else (lambda s: 0)
    sb = (lambda s: s) if b.shape[0] > 1 else (lambda s: 0)
    grid = (n_so, k_dim // tk, n_out // tn, n_tok // tt)

    def body(a_ref, b_ref, out_ref):
        d = lax.dot_general(a_ref[0].astype(MXU_DTYPE), b_ref[0].astype(MXU_DTYPE), _TN, preferred_element_type=F32)
        t = pl.program_id(3)

        @pl.when(t == 0)
        def _():
            out_ref[0] = d

        @pl.when(t > 0)
        def _():
            out_ref[0] += d

    blk = _nbytes((tt, tk), a.dtype) + _nbytes((tt, tn), b.dtype) + _nbytes((tk, tn), F32)
    return pl.pallas_call(
        body,
        name=name,
        grid=grid,
        in_specs=[
            pl.BlockSpec((1, tt, tk), lambda s, ki, nj, t: (sa(s), t, ki)),
            pl.BlockSpec((1, tt, tn), lambda s, ki, nj, t: (sb(s), t, nj)),
        ],
        out_specs=pl.BlockSpec((1, tk, tn), lambda s, ki, nj, t: (s, ki, nj)),
        out_shape=jax.ShapeDtypeStruct((n_so, k_dim, n_out), F32),
        compiler_params=_params(blk, 2 * _nbytes((tk, tn), F32) + _nbytes((tt, tk), F32)),
    )(*_hbm(a, b))


@functools.partial(jax.custom_vjp, nondiff_argnums=(2,))
def _mdot(a, b, dims):
    return lax.dot_general(a.astype(MXU_DTYPE), b.astype(MXU_DTYPE), dims, preferred_element_type=F32)


def _mdot_fwd(a, b, dims):
    return _mdot(a, b, dims), (a, b)


def _mdot_bwd(dims, res, g):
    a, b = res
    (ca, cb), (ba, bb) = dims
    nb = len(ba)
    assert tuple(ba) == tuple(range(nb)) and tuple(bb) == tuple(range(nb)) and len(ca) == 1 and a.ndim == nb + 2
    batch = (tuple(range(nb)), tuple(range(nb)))
    ra, rb = nb, nb + 1
    a_free = (set(range(nb, nb + 2)) - set(ca)).pop()
    b_free = (set(range(nb, nb + 2)) - set(cb)).pop()
    if a_free < ca[0]:
        da = _mdot(g, b, (((rb,), (b_free,)), batch))
    else:
        da = _mdot(b, g, (((b_free,), (rb,)), batch))
    if b_free > cb[0]:
        db = _mdot(a, g, (((a_free,), (ra,)), batch))
    else:
        db = _mdot(g, a, (((ra,), (a_free,)), batch))
    return da.astype(a.dtype), db.astype(b.dtype)


_mdot.defvjp(_mdot_fwd, _mdot_bwd)


def _rms(x, gain):
    return x * lax.rsqrt(jnp.mean(x * x, axis=-1, keepdims=True) + EPS) * gain


def _pre_fn(coef, x_in, f, gate, gain, shift, scale):
    x_new = x_in if f is None else x_in + coef * gate * f
    return x_new, _rms(x_new, gain) * (1.0 + scale) + shift


def _row_spec(ts):
    return pl.BlockSpec((1, ts, D_MODEL), lambda b, j: (b, j, 0))


_BATCH_VEC = pl.BlockSpec((1, 1, D_MODEL), lambda b, j: (b, 0, 0))
_ONE_VEC = pl.BlockSpec((1, D_MODEL), lambda b, j: (0, 0))


def _pre(x_in, f, gate, gain, shift, scale, coef, *, name, ts=512):
    n_b, n_s, _ = x_in.shape
    ts = min(ts, n_s)
    has_res = f is not None

    def body(*refs):
        if has_res:
            x_ref, f_ref, gate_ref, gain_ref, sh_ref, sc_ref, xn_ref, a_ref = refs
            x_new, a = _pre_fn(coef, x_ref[0], f_ref[0], gate_ref[0], gain_ref[...], sh_ref[0], sc_ref[0])
            xn_ref[0] = x_new
        else:
            x_ref, gain_ref, sh_ref, sc_ref, a_ref = refs
            _, a = _pre_fn(coef, x_ref[0], None, None, gain_ref[...], sh_ref[0], sc_ref[0])
        a_ref[0] = a.astype(a_ref.dtype)

    row = _row_spec(ts)
    if has_res:
        args = (x_in, f, gate, gain, shift, scale)
        in_specs = [row, row, _BATCH_VEC, _ONE_VEC, _BATCH_VEC, _BATCH_VEC]
        out_specs = (row, row)
        out_shape = (jax.ShapeDtypeStruct(x_in.shape, F32), jax.ShapeDtypeStruct(x_in.shape, MXU_DTYPE))
    else:
        args = (x_in, gain, shift, scale)
        in_specs = [row, _ONE_VEC, _BATCH_VEC, _BATCH_VEC]
        out_specs = row
        out_shape = jax.ShapeDtypeStruct(x_in.shape, MXU_DTYPE)
    return pl.pallas_call(
        body, name=name, grid=(n_b, n_s // ts), in_specs=in_specs, out_specs=out_specs, out_shape=out_shape,
        compiler_params=_params(5 * _nbytes((ts, D_MODEL), F32), 4 * _nbytes((ts, D_MODEL), F32)),
    )(*_hbm(*args))


def _accumulate(ref, value, first):
    @pl.when(first)
    def _():
        ref[...] = value

    @pl.when(jnp.logical_not(first))
    def _():
        ref[...] += value


def _pre_bwd(x_in, f, gate, gain, shift, scale, coef, da, dx_up, *, name, ts=512):
    n_b, n_s, _ = x_in.shape
    ts = min(ts, n_s)
    has_res = f is not None
    has_up = dx_up is not None

    def body(*refs):
        refs = list(refs)
        x_ref = refs.pop(0)
        f_ref, gate_ref = (refs.pop(0), refs.pop(0)) if has_res else (None, None)
        gain_ref, sh_ref, sc_ref, da_ref = refs.pop(0), refs.pop(0), refs.pop(0), refs.pop(0)
        up_ref = refs.pop(0) if has_up else None
        dx_ref = refs.pop(0)
        df_ref, dgate_ref = (refs.pop(0), refs.pop(0)) if has_res else (None, None)
        dgain_ref, dsh_ref, dsc_ref = refs
        b, j = pl.program_id(0), pl.program_id(1)
        da_v = da_ref[0].astype(F32)
        up_v = up_ref[0] if has_up else jnp.zeros((ts, D_MODEL), F32)
        if has_res:
            fn = functools.partial(_pre_fn, coef)
            _, pull = jax.vjp(fn, x_ref[0], f_ref[0], gate_ref[0], gain_ref[...], sh_ref[0], sc_ref[0])
            dx, df, dgate, dgain, dsh, dsc = pull((up_v, da_v))
            df_ref[0] = df.astype(df_ref.dtype)
            _accumulate(dgate_ref, dgate[None], j == 0)
        else:
            fn = lambda x, g, sh, sc: _pre_fn(coef, x, None, None, g, sh, sc)
            _, pull = jax.vjp(fn, x_ref[0], gain_ref[...], sh_ref[0], sc_ref[0])
            dx, dgain, dsh, dsc = pull((up_v, da_v))
        dx_ref[0] = dx
        _accumulate(dgain_ref, dgain, jnp.logical_and(b == 0, j == 0))
        _accumulate(dsh_ref, dsh[None], j == 0)
        _accumulate(dsc_ref, dsc[None], j == 0)

    row = _row_spec(ts)
    args, in_specs = [x_in], [row]
    if has_res:
        args += [f, gate]
        in_specs += [row, _BATCH_VEC]
    args += [gain, shift, scale, da]
    in_specs += [_ONE_VEC, _BATCH_VEC, _BATCH_VEC, row]
    if has_up:
        args.append(dx_up)
        in_specs.append(row)
    vec = jax.ShapeDtypeStruct((n_b, 1, D_MODEL), F32)
    out_shape, out_specs = [jax.ShapeDtypeStruct(x_in.shape, F32)], [row]
    if has_res:
        out_shape += [jax.ShapeDtypeStruct(x_in.shape, MXU_DTYPE), vec]
        out_specs += [row, _BATCH_VEC]
    out_shape += [jax.ShapeDtypeStruct((1, D_MODEL), F32), vec, vec]
    out_specs += [_ONE_VEC, _BATCH_VEC, _BATCH_VEC]
    return pl.pallas_call(
        body, name=name, grid=(n_b, n_s // ts), in_specs=in_specs, out_specs=tuple(out_specs), out_shape=tuple(out_shape),
        compiler_params=_params(6 * _nbytes((ts, D_MODEL), F32), 8 * _nbytes((ts, D_MODEL), F32)),
    )(*_hbm(*args))


def _final_fn(x_in, f, gate, gain, target):
    x_new = x_in + 0.5 * gate * f
    err = jnp.square(_rms(x_new, gain) - target)
    return 0.5 * jnp.sum(jnp.mean(err, axis=-1))


def _final(x_in, f, gate, gain, target, *, name, ts=512):
    n_b, n_s, _ = x_in.shape
    ts = min(ts, n_s)

    def body(x_ref, f_ref, gate_ref, gain_ref, t_ref, loss_ref, dx_ref, df_ref, dgate_ref, dgain_ref):
        b, j = pl.program_id(0), pl.program_id(1)
        loss, (dx, df, dgate, dgain) = jax.value_and_grad(_final_fn, argnums=(0, 1, 2, 3))(
            x_ref[0], f_ref[0], gate_ref[0], gain_ref[...], t_ref[0])
        first = jnp.logical_and(b == 0, j == 0)
        _accumulate(loss_ref, jnp.reshape(loss, (1, 1)), first)
        dx_ref[0] = dx
        df_ref[0] = df.astype(df_ref.dtype)
        _accumulate(dgate_ref, dgate[None], j == 0)
        _accumulate(dgain_ref, dgain, first)

    row = _row_spec(ts)
    return pl.pallas_call(
        body, name=name, grid=(n_b, n_s // ts),
        in_specs=[row, row, _BATCH_VEC, _ONE_VEC, row],
        out_specs=(pl.BlockSpec((1, 1), lambda b, j: (0, 0)), row, row, _BATCH_VEC, _ONE_VEC),
        out_shape=(jax.ShapeDtypeStruct((1, 1), F32), jax.ShapeDtypeStruct(x_in.shape, F32),
                   jax.ShapeDtypeStruct(x_in.shape, MXU_DTYPE), jax.ShapeDtypeStruct((n_b, 1, D_MODEL), F32),
                   jax.ShapeDtypeStruct((1, D_MODEL), F32)),
        compiler_params=_params(5 * _nbytes((ts, D_MODEL), F32), 8 * _nbytes((ts, D_MODEL), F32)),
    )(*_hbm(x_in, f, gate, gain, target))


FFN_TOKENS = 1024


def _ffn_up(a, w1s, w3s, *, name, tm=FFN_TOKENS):
    n_tok = a.shape[0]
    tm = min(tm, n_tok)

    def body(a_ref, w1_ref, w3_ref, h1_ref, h3_ref, g_ref):
        av = a_ref[...].astype(MXU_DTYPE)
        h1 = lax.dot_general(av, w1_ref[0].astype(MXU_DTYPE), _NT, preferred_element_type=F32)
        h3 = lax.dot_general(av, w3_ref[0].astype(MXU_DTYPE), _NT, preferred_element_type=F32)
        h1_ref[0] = h1.astype(h1_ref.dtype)
        h3_ref[0] = h3.astype(h3_ref.dtype)
        g_ref[0] = (jax.nn.silu(h1) * h3).astype(g_ref.dtype)

    w_spec = pl.BlockSpec((1, FF_SHARD, D_MODEL), lambda s, i: (s, 0, 0))
    h_spec = pl.BlockSpec((1, tm, FF_SHARD), lambda s, i: (s, i, 0))
    h_shape = jax.ShapeDtypeStruct((N_SHARD, n_tok, FF_SHARD), MXU_DTYPE)
    blk = _nbytes((tm, D_MODEL), a.dtype) + 2 * _nbytes((D_MODEL, FF_SHARD), w1s.dtype) + 3 * _nbytes((tm, FF_SHARD), MXU_DTYPE)
    return pl.pallas_call(
        body, name=name, grid=(N_SHARD, n_tok // tm),
        in_specs=[pl.BlockSpec((tm, D_MODEL), lambda s, i: (i, 0)), w_spec, w_spec],
        out_specs=(h_spec, h_spec, h_spec), out_shape=(h_shape, h_shape, h_shape),
        compiler_params=_params(blk, 6 * _nbytes((tm, FF_SHARD), F32)),
    )(*_hbm(a, w1s, w3s))


def _ffn_down_bwd(df, w2s, h1, h3, *, name, tm=FFN_TOKENS):
    n_tok = df.shape[0]
    tm = min(tm, n_tok)

    def body(df_ref, w2_ref, h1_ref, h3_ref, dh1_ref, dh3_ref):
        dg = lax.dot_general(df_ref[...].astype(MXU_DTYPE), w2_ref[0].astype(MXU_DTYPE), _NT, preferred_element_type=F32)
        h1v = h1_ref[0].astype(F32)
        h3v = h3_ref[0].astype(F32)
        sig = jax.nn.sigmoid(h1v)
        dh3_ref[0] = (dg * (h1v * sig)).astype(dh3_ref.dtype)
        dh1_ref[0] = (dg * h3v * (sig * (1.0 + h1v * (1.0 - sig)))).astype(dh1_ref.dtype)

    h_spec = pl.BlockSpec((1, tm, FF_SHARD), lambda s, i: (s, i, 0))
    h_shape = jax.ShapeDtypeStruct((N_SHARD, n_tok, FF_SHARD), MXU_DTYPE)
    blk = _nbytes((tm, D_MODEL), df.dtype) + _nbytes((FF_SHARD, D_MODEL), w2s.dtype) + 4 * _nbytes((tm, FF_SHARD), MXU_DTYPE)
    return pl.pallas_call(
        body, name=name, grid=(N_SHARD, n_tok // tm),
        in_specs=[pl.BlockSpec((tm, D_MODEL), lambda s, i: (i, 0)),
                  pl.BlockSpec((1, FF_SHARD, D_MODEL), lambda s, i: (s, 0, 0)), h_spec, h_spec],
        out_specs=(h_spec, h_spec), out_shape=(h_shape, h_shape),
        compiler_params=_params(blk, 8 * _nbytes((tm, FF_SHARD), F32)),
    )(*_hbm(df, w2s, h1, h3))


def _ffn_fwd(a, w1s, w3s, w2s, tag):
    h1, h3, g = _ffn_up(a, w1s, w3s, name=f"{tag}_up")
    f = _mm_act([(g, w2s)], "nn", reduce_shards=True, tm=FFN_TOKENS, name=f"{tag}_down")[0]
    return f, (h1, h3, g)


def _ffn_bwd(a, w1s, w3s, w2s, saved, df, tag):
    h1, h3, g = saved
    dh1, dh3 = _ffn_down_bwd(df, w2s, h1, h3, name=f"{tag}_down_bwd")
    da = _mm_act([(dh1, w1s), (dh3, w3s)], "nn", reduce_shards=True, tm=FFN_TOKENS, name=f"{tag}_up_bwd")[0]
    a3 = a[None]
    dw1 = _mm_tn(dh1, a3, tt=FFN_TOKENS, name=f"{tag}_dw1")
    dw3 = _mm_tn(dh3, a3, tt=FFN_TOKENS, name=f"{tag}_dw3")
    dw2 = _mm_tn(g, df[None], tt=FFN_TOKENS, name=f"{tag}_dw2")
    return da, dw1, dw3, dw2


CONV_LANES = 256


def _shift_down(x, d):
    if d == 0:
        return x
    row = lax.broadcasted_iota(jnp.int32, x.shape, 0)
    return jnp.where(row >= d, pltpu.roll(x, d, 0), 0.0)


def _shift_up(x, d):
    if d == 0:
        return x
    n = x.shape[0]
    row = lax.broadcasted_iota(jnp.int32, x.shape, 0)
    return jnp.where(row < n - d, pltpu.roll(x, n - d, 0), 0.0)


def _conv_pre(x, w):
    acc = None
    for j in range(CONV_WIDTH):
        term = w[j:j + 1, :] * _shift_down(x, CONV_WIDTH - 1 - j)
        acc = term if acc is None else acc + term
    return acc


def _conv_fwd(x, w, *, name):
    n_b, n_s, n_c = x.shape
    spec = pl.BlockSpec((1, n_s, CONV_LANES), lambda b, cj: (b, 0, cj))

    def body(x_ref, w_ref, o_ref):
        o_ref[0] = jax.nn.silu(_conv_pre(x_ref[0], w_ref[...]))

    return pl.pallas_call(
        body, name=name, grid=(n_b, n_c // CONV_LANES),
        in_specs=[spec, pl.BlockSpec((CONV_WIDTH, CONV_LANES), lambda b, cj: (0, cj))],
        out_specs=spec, out_shape=jax.ShapeDtypeStruct(x.shape, F32),
        compiler_params=_params(2 * _nbytes((n_s, CONV_LANES), F32), 6 * _nbytes((n_s, CONV_LANES), F32)),
    )(*_hbm(x, w))


def _conv_bwd(x, w, dout, *, name):
    n_b, n_s, n_c = x.shape
    per_part = DN_WIDTH // CONV_LANES
    spec = pl.BlockSpec((1, n_s, CONV_LANES), lambda cj, b: (b, 0, cj))
    do_spec = pl.BlockSpec((1, 1, n_s, CONV_LANES), lambda cj, b: (cj // per_part, b, 0, cj % per_part))
    w_spec = pl.BlockSpec((CONV_WIDTH, CONV_LANES), lambda cj, b: (0, cj))

    def body(x_ref, w_ref, do_ref, dx_ref, dw_ref):
        xv, wv = x_ref[0], w_ref[...]
        pre = _conv_pre(xv, wv)
        sig = jax.nn.sigmoid(pre)
        dpre = do_ref[0, 0] * (sig * (1.0 + pre * (1.0 - sig)))
        dx = None
        first = pl.program_id(1) == 0
        for j in range(CONV_WIDTH):
            d = CONV_WIDTH - 1 - j
            ahead = _shift_up(dpre, d)
            term = wv[j:j + 1, :] * ahead
            dx = term if dx is None else dx + term
            dwj = jnp.sum(ahead * xv, axis=0, keepdims=True)
            _accumulate(dw_ref.at[j:j + 1, :], dwj, first)
        dx_ref[0] = dx.astype(dx_ref.dtype)

    return pl.pallas_call(
        body, name=name, grid=(n_c // CONV_LANES, n_b),
        in_specs=[spec, w_spec, do_spec], out_specs=(spec, w_spec),
        out_shape=(jax.ShapeDtypeStruct(x.shape, MXU_DTYPE), jax.ShapeDtypeStruct((CONV_WIDTH, n_c), F32)),
        compiler_params=_params(3 * _nbytes((n_s, CONV_LANES), F32), 8 * _nbytes((n_s, CONV_LANES), F32)),
    )(*_hbm(x, w, dout))


_BNT = (((2,), (2,)), ((0,), (0,)))
_BNN = (((2,), (1,)), ((0,), (0,)))
_BTN = (((1,), (1,)), ((0,), (0,)))
DN_PREP_CHUNKS = 8
DN_SCAN_HEADS = 4
N_DOUBLINGS = 5


def _fdot(a, b, dims):
    return lax.dot_general(a, b, dims, precision=lax.Precision.HIGHEST, preferred_element_type=F32)


def _hdot(a, b, dims):
    return lax.dot_general(a, b, dims, precision=lax.Precision.HIGH, preferred_element_type=F32)


def _solve_by_doubling(a, rhs_u, rhs_w):
    row = lax.broadcasted_iota(jnp.int32, (CHUNK, CHUNK), 0)
    col = lax.broadcasted_iota(jnp.int32, (CHUNK, CHUNK), 1)
    inv = jnp.where(row == col, 1.0, 0.0) - a
    power = a
    for _ in range(N_DOUBLINGS):
        power = _hdot(power, power, _BNN)
        inv = inv + _hdot(inv, power, _BNN)
    return _hdot(inv, rhs_u, _BNN), _hdot(inv, rhs_w, _BNN), inv


@jax.custom_vjp
def _solve_saved(a, rhs_u, rhs_w, inv, u, w):
    return u, w


def _solve_saved_fwd(a, rhs_u, rhs_w, inv, u, w):
    return (u, w), (inv, u, w)


def _solve_saved_bwd(res, cts):
    inv, u, w = res
    gu = _hdot(inv, cts[0], _BTN)
    gw = _hdot(inv, cts[1], _BTN)
    da = -(_hdot(gu, u, _BNT) + _hdot(gw, w, _BNT))
    return da, gu, gw, jnp.zeros_like(inv), jnp.zeros_like(u), jnp.zeros_like(w)


_solve_saved.defvjp(_solve_saved_fwd, _solve_saved_bwd)


def _dn_prep_fn(solve, qc, kc, vc, bl, lac, lar, a_log, dt_bias):
    q = qc * lax.rsqrt(jnp.sum(qc * qc, axis=-1, keepdims=True) + EPS) * (DN_HEAD_DIM ** -0.5)
    k = kc * lax.rsqrt(jnp.sum(kc * kc, axis=-1, keepdims=True) + EPS)
    beta = jax.nn.sigmoid(bl)
    neg_a = -jnp.exp(a_log)
    lgc = neg_a * jax.nn.softplus(lac + dt_bias)
    lgr = neg_a * jax.nn.softplus(lar + dt_bias)
    row = lax.broadcasted_iota(jnp.int32, (CHUNK, CHUNK), 0)
    col = lax.broadcasted_iota(jnp.int32, (CHUNK, CHUNK), 1)
    causal, strict = row >= col, row > col
    g_c = jnp.sum(jnp.where(causal, lgr, 0.0), axis=-1, keepdims=True)
    g_r = jnp.sum(jnp.where(row <= col, lgc, 0.0), axis=-2, keepdims=True)
    decay = jnp.exp(jnp.where(causal, g_c - g_r, -jnp.inf))
    kb = k * beta
    a = jnp.where(strict, _mdot(kb, k, _BNT) * decay, 0.0)
    u, w, extra = solve(a, vc * beta, kb * jnp.exp(g_c))
    attn = _mdot(q, k, _BNT) * decay
    g_last = jnp.sum(lgc, axis=-2, keepdims=True)
    return q * jnp.exp(g_c), k * jnp.exp(g_last - g_c), u, w, attn, g_last, extra


PAIR = 2
PAIR_LANES = PAIR * DN_HEAD_DIM


def _dn_prep_specs(n_cb):
    tok = n_cb * CHUNK
    wide = pl.BlockSpec((1, PAIR, tok, DN_HEAD_DIM), lambda p, b, j: (b, p, j, 0))
    rowv = pl.BlockSpec((1, PAIR, n_cb, 1, CHUNK), lambda p, b, j: (b, p, j, 0, 0))
    one = pl.BlockSpec((1, PAIR, n_cb, 1, 1), lambda p, b, j: (b, p, j, 0, 0))
    head = pl.BlockSpec((PAIR, 1, 1), lambda p, b, j: (p, 0, 0))
    lanes = lambda part: pl.BlockSpec((1, tok, PAIR_LANES), lambda p, b, j: (b, j, part * (DN_HEADS // PAIR) + p))
    return wide, rowv, one, head, lanes


def _split_pair(x, n_cb):
    halves = [x[:, h * DN_HEAD_DIM:(h + 1) * DN_HEAD_DIM].reshape(n_cb, CHUNK, DN_HEAD_DIM) for h in range(PAIR)]
    return jnp.concatenate(halves, axis=0)


def _join_pair(chunks, tok):
    per_head = chunks.reshape(PAIR, tok, DN_HEAD_DIM)
    return jnp.concatenate([per_head[h] for h in range(PAIR)], axis=-1)


def _dn_prep_load(n_cb, q_ref, k_ref, v_ref, blr_ref, lar_ref, al_ref, dt_ref):
    rowf = lambda r: r[0].reshape(PAIR * n_cb, 1, CHUNK)
    return (_split_pair(q_ref[0], n_cb), _split_pair(k_ref[0], n_cb), _split_pair(v_ref[0], n_cb), rowf(blr_ref),
            rowf(lar_ref), al_ref[...], dt_ref[...])


def _dn_prep_pair_fn(n_cb, solve, qc, kc, vc, blr, lar, a_log, dt_bias):
    per_chunk = lambda t: jnp.broadcast_to(t[:, None], (PAIR, n_cb, 1, 1)).reshape(PAIR * n_cb, 1, 1)
    eye = lax.broadcasted_iota(jnp.int32, (CHUNK, CHUNK), 0) == lax.broadcasted_iota(jnp.int32, (CHUNK, CHUNK), 1)
    to_col = lambda r: jnp.sum(jnp.where(eye, r, 0.0), axis=-1, keepdims=True)
    return _dn_prep_fn(solve, qc, kc, vc, to_col(blr), to_col(lar), lar, per_chunk(a_log), per_chunk(dt_bias))


def _dn_prep(qkv, blr, lar, a_log, dt_bias, *, name):
    n_b, n_s, _ = qkv.shape
    n_cb = min(DN_PREP_CHUNKS, n_s // CHUNK)
    tok = n_cb * CHUNK
    wide, rowv, one, head, lanes = _dn_prep_specs(n_cb)

    def body(*refs):
        outs = _dn_prep_pair_fn(n_cb, _solve_by_doubling, *_dn_prep_load(n_cb, *refs[:7]))
        for ref, val in zip(refs[7:12], outs[:5]):
            ref[0] = val.reshape(PAIR, tok, DN_HEAD_DIM)
        refs[12][0] = outs[5].reshape(PAIR, n_cb, 1, 1)
        refs[13][0] = outs[6].reshape(PAIR, tok, DN_HEAD_DIM)

    big = jax.ShapeDtypeStruct((n_b, DN_HEADS, n_s, DN_HEAD_DIM), F32)
    return pl.pallas_call(
        body, name=name, grid=(DN_HEADS // PAIR, n_b, n_s // tok),
        in_specs=[lanes(0), lanes(1), lanes(2), rowv, rowv, head, head],
        out_specs=(wide, wide, wide, wide, wide, one, wide),
        out_shape=(big, big, big, big, big, jax.ShapeDtypeStruct((n_b, DN_HEADS, n_s // CHUNK, 1, 1), F32), big),
        compiler_params=_params(11 * PAIR * _nbytes((tok, LANES), F32), 48 * PAIR * _nbytes((tok, LANES), F32)),
    )(*_hbm(qkv, qkv, qkv, blr, lar, a_log, dt_bias))


def _dn_prep_bwd(qkv, blr, lar, a_log, dt_bias, inv, u, w, cts, *, name):
    n_b, n_s, _ = qkv.shape
    n_cb = min(DN_PREP_CHUNKS, n_s // CHUNK)
    tok = n_cb * CHUNK
    wide, rowv, one, head, lanes = _dn_prep_specs(n_cb)

    def body(*refs):
        prim = _dn_prep_load(n_cb, *refs[:7])
        chunks = lambda r: r[0].reshape(PAIR * n_cb, CHUNK, DN_HEAD_DIM)
        inv_v, u_v, w_v = chunks(refs[7]), chunks(refs[8]), chunks(refs[9])
        ct = tuple(chunks(r) for r in refs[10:15]) + (refs[15][0].reshape(PAIR * n_cb, 1, 1),)

        def fn(*args):
            solve = lambda a, ru, rw: _solve_saved(a, ru, rw, inv_v, u_v, w_v) + (None,)
            return _dn_prep_pair_fn(n_cb, solve, *args)[:6]

        _, pull = jax.vjp(fn, *prim)
        dq, dk, dv, dblr, dlar, dal, ddt = pull(ct)
        outs = refs[16:]
        for part, val in enumerate((dq, dk, dv)):
            outs[0][part, 0] = _join_pair(val, tok)
        outs[1][0] = dblr.reshape(PAIR, n_cb, 1, CHUNK)
        outs[2][0] = dlar.reshape(PAIR, n_cb, 1, CHUNK)
        first = jnp.logical_and(pl.program_id(1) == 0, pl.program_id(2) == 0)
        _accumulate(outs[3], dal, first)
        _accumulate(outs[4], ddt, first)

    dqkv_spec = pl.BlockSpec((3, 1, tok, PAIR_LANES), lambda p, b, j: (0, b, j, p))
    return pl.pallas_call(
        body, name=name, grid=(DN_HEADS // PAIR, n_b, n_s // tok),
        in_specs=[lanes(0), lanes(1), lanes(2), rowv, rowv, head, head, wide, wide, wide, wide, wide, wide, wide, wide, one],
        out_specs=(dqkv_spec, rowv, rowv, head, head),
        out_shape=(jax.ShapeDtypeStruct((3, n_b, n_s, DN_WIDTH), F32), jax.ShapeDtypeStruct(blr.shape, F32),
                   jax.ShapeDtypeStruct(lar.shape, F32), jax.ShapeDtypeStruct(a_log.shape, F32),
                   jax.ShapeDtypeStruct(dt_bias.shape, F32)),
        compiler_params=_params(21 * PAIR * _nbytes((tok, LANES), F32), 64 * PAIR * _nbytes((tok, LANES), F32)),
    )(*_hbm(qkv, qkv, qkv, blr, lar, a_log, dt_bias, inv, u, w, *cts))


def _dn_step(state, q, k, u, w, a, gl):
    v_new = u - _mdot(w, state, _BNN)
    o = _mdot(q, state, _BNN) + _mdot(a, v_new, _BNN)
    return state * jnp.exp(gl) + _mdot(k, v_new, _BTN), o


def _dn_scan_specs(n_cb, n_blocks, reverse):
    tok = n_cb * CHUNK
    jj = (lambda j: n_blocks - 1 - j) if reverse else (lambda j: j)
    wide = pl.BlockSpec((1, DN_SCAN_HEADS, tok, DN_HEAD_DIM), lambda b, h, j: (b, h, jj(j), 0))
    one = pl.BlockSpec((1, DN_SCAN_HEADS, n_cb, 1, 1), lambda b, h, j: (b, h, jj(j), 0, 0))
    st = pl.BlockSpec((1, DN_SCAN_HEADS, n_cb, DN_HEAD_DIM, DN_HEAD_DIM), lambda b, h, j: (b, h, jj(j), 0, 0))
    return wide, one, st


def _dn_scan(qd, kd, u, w, attn, g_last, *, name):
    n_b, n_h, n_s, _ = qd.shape
    n_cb = min(DN_PREP_CHUNKS, n_s // CHUNK)
    n_blocks = n_s // (n_cb * CHUNK)
    wide, one, st = _dn_scan_specs(n_cb, n_blocks, False)

    def body(qd_ref, kd_ref, u_ref, w_ref, a_ref, gl_ref, o_ref, st_ref, state_ref):
        @pl.when(pl.program_id(2) == 0)
        def _():
            state_ref[...] = jnp.zeros(state_ref.shape, F32)

        def step(n, state):
            rows = pl.ds(pl.multiple_of(n * CHUNK, CHUNK), CHUNK)
            st_ref[0, :, n] = state
            state, o = _dn_step(state, qd_ref[0, :, rows, :], kd_ref[0, :, rows, :], u_ref[0, :, rows, :],
                                w_ref[0, :, rows, :], a_ref[0, :, rows, :], gl_ref[0, :, n])
            o_ref[0, :, rows, :] = o
            return state

        state_ref[...] = lax.fori_loop(0, n_cb, step, state_ref[...])

    return pl.pallas_call(
        body, name=name, grid=(n_b, n_h // DN_SCAN_HEADS, n_blocks),
        in_specs=[wide, wide, wide, wide, wide, one], out_specs=(wide, st),
        out_shape=(jax.ShapeDtypeStruct(qd.shape, F32),
                   jax.ShapeDtypeStruct((n_b, n_h, n_s // CHUNK, DN_HEAD_DIM, DN_HEAD_DIM), F32)),
        scratch_shapes=[pltpu.VMEM((DN_SCAN_HEADS, DN_HEAD_DIM, DN_HEAD_DIM), F32)],
        compiler_params=_params(8 * _nbytes((DN_SCAN_HEADS, n_cb * CHUNK, LANES), F32), 8 << 20),
    )(*_hbm(qd, kd, u, w, attn, g_last))


def _dn_scan_bwd(qd, kd, u, w, attn, g_last, states, do, *, name):
    n_b, n_h, n_s, _ = qd.shape
    n_cb = min(DN_PREP_CHUNKS, n_s // CHUNK)
    n_blocks = n_s // (n_cb * CHUNK)
    wide, one, st = _dn_scan_specs(n_cb, n_blocks, True)

    def body(qd_ref, kd_ref, u_ref, w_ref, a_ref, gl_ref, st_ref, do_ref,
             dq_ref, dk_ref, du_ref, dw_ref, da_ref, dgl_ref, dstate_ref):
        @pl.when(pl.program_id(2) == 0)
        def _():
            dstate_ref[...] = jnp.zeros(dstate_ref.shape, F32)

        def step(i, dstate):
            n = n_cb - 1 - i
            rows = pl.ds(pl.multiple_of(n * CHUNK, CHUNK), CHUNK)
            _, pull = jax.vjp(_dn_step, st_ref[0, :, n], qd_ref[0, :, rows, :], kd_ref[0, :, rows, :],
                              u_ref[0, :, rows, :], w_ref[0, :, rows, :], a_ref[0, :, rows, :], gl_ref[0, :, n])
            dstate, dq, dk, du, dw, da, dgl = pull((dstate, do_ref[0, :, rows, :]))
            dq_ref[0, :, rows, :] = dq
            dk_ref[0, :, rows, :] = dk
            du_ref[0, :, rows, :] = du
            dw_ref[0, :, rows, :] = dw
            da_ref[0, :, rows, :] = da
            dgl_ref[0, :, n] = dgl
            return dstate

        dstate_ref[...] = lax.fori_loop(0, n_cb, step, dstate_ref[...])

    big = jax.ShapeDtypeStruct(qd.shape, F32)
    return pl.pallas_call(
        body, name=name, grid=(n_b, n_h // DN_SCAN_HEADS, n_blocks),
        in_specs=[wide, wide, wide, wide, wide, one, st, wide],
        out_specs=(wide, wide, wide, wide, wide, one),
        out_shape=(big, big, big, big, big, jax.ShapeDtypeStruct(g_last.shape, F32)),
        scratch_shapes=[pltpu.VMEM((DN_SCAN_HEADS, DN_HEAD_DIM, DN_HEAD_DIM), F32)],
        compiler_params=_params(13 * _nbytes((DN_SCAN_HEADS, n_cb * CHUNK, LANES), F32), 8 << 20),
    )(*_hbm(qd, kd, u, w, attn, g_last, states, do))


def _dn_post_fn(o, z, gain):
    return o * lax.rsqrt(jnp.mean(o * o, axis=-1, keepdims=True) + EPS) * gain * jax.nn.silu(z)


_HEAD_ROWS = lambda n_s: pl.BlockSpec((1, PAIR, n_s, DN_HEAD_DIM), lambda b, p: (b, p, 0, 0))
_PAIR_LANES = lambda n_s: pl.BlockSpec((1, n_s, PAIR_LANES), lambda b, p: (b, 0, p))
_HEAD_GAIN = pl.BlockSpec((1, DN_HEAD_DIM), lambda b, p: (0, 0))


def _pair_heads(x):
    return jnp.stack([x[:, h * DN_HEAD_DIM:(h + 1) * DN_HEAD_DIM] for h in range(PAIR)])


def _pair_lanes(x):
    return jnp.concatenate([x[h] for h in range(PAIR)], axis=-1)


def _dn_post(o, z, gain, *, name):
    n_b, _, n_s, _ = o.shape

    def body(o_ref, z_ref, g_ref, out_ref):
        out = _dn_post_fn(o_ref[0], _pair_heads(z_ref[0]), g_ref[...])
        out_ref[0] = _pair_lanes(out).astype(out_ref.dtype)

    lanes = _PAIR_LANES(n_s)
    return pl.pallas_call(
        body, name=name, grid=(n_b, DN_HEADS // PAIR), in_specs=[_HEAD_ROWS(n_s), lanes, _HEAD_GAIN], out_specs=lanes,
        out_shape=jax.ShapeDtypeStruct(z.shape, MXU_DTYPE),
        compiler_params=_params(3 * PAIR * _nbytes((n_s, LANES), F32), 6 * PAIR * _nbytes((n_s, LANES), F32)),
    )(*_hbm(o, z, gain))


def _dn_post_bwd(o, z, gain, dout, *, name):
    n_b, _, n_s, _ = o.shape

    def body(o_ref, z_ref, g_ref, dout_ref, do_ref, dz_ref, dg_ref):
        _, pull = jax.vjp(_dn_post_fn, o_ref[0], _pair_heads(z_ref[0]), g_ref[...])
        do, dz, dg = pull(_pair_heads(dout_ref[0].astype(F32)))
        do_ref[0] = do
        dz_ref[0] = _pair_lanes(dz).astype(dz_ref.dtype)
        _accumulate(dg_ref, dg, jnp.logical_and(pl.program_id(0) == 0, pl.program_id(1) == 0))

    rows, lanes = _HEAD_ROWS(n_s), _PAIR_LANES(n_s)
    return pl.pallas_call(
        body, name=name, grid=(n_b, DN_HEADS // PAIR), in_specs=[rows, lanes, _HEAD_GAIN, lanes],
        out_specs=(rows, lanes, _HEAD_GAIN),
        out_shape=(jax.ShapeDtypeStruct(o.shape, F32), jax.ShapeDtypeStruct(z.shape, MXU_DTYPE),
                   jax.ShapeDtypeStruct((1, DN_HEAD_DIM), F32)),
        compiler_params=_params(5 * PAIR * _nbytes((n_s, LANES), F32), 10 * PAIR * _nbytes((n_s, LANES), F32)),
    )(*_hbm(o, z, gain, dout))


S5_SCAN_LANES = 256
TILE_ROWS = SUBLANES


def _s5_prep_fn(lam_re, lam_im, log_step, bt_re, bt_im, c_im):
    lr = jnp.minimum(lam_re, -1e-4)
    step = jnp.exp(log_step)
    mag = jnp.exp(lr * step)
    ang = lam_im * step
    lb_re = mag * jnp.cos(ang)
    lb_im = mag * jnp.sin(ang)
    den = lr * lr + lam_im * lam_im
    coef_re = ((lb_re - 1.0) * lr + lb_im * lam_im) / den
    coef_im = (lb_im * lr - (lb_re - 1.0) * lam_im) / den
    return (lb_re, lb_im, coef_re * bt_re - coef_im * bt_im, coef_re * bt_im + coef_im * bt_re, -c_im)


def _s5_prep(lam_re, lam_im, log_step, bt_re, bt_im, c_im, *, name):
    def body(*refs):
        outs = _s5_prep_fn(*(r[...] for r in refs[:6]))
        for ref, val in zip(refs[6:], outs):
            ref[...] = val

    vec = jax.ShapeDtypeStruct(lam_re.shape, F32)
    mat = jax.ShapeDtypeStruct(bt_re.shape, F32)
    return pl.pallas_call(body, name=name, out_shape=(vec, vec, mat, mat, mat))(lam_re, lam_im, log_step, bt_re, bt_im, c_im)


def _s5_prep_bwd(lam_re, lam_im, log_step, bt_re, bt_im, c_im, cts, *, name):
    def body(*refs):
        _, pull = jax.vjp(_s5_prep_fn, *(r[...] for r in refs[:6]))
        grads = pull(tuple(r[...] for r in refs[6:11]))
        for ref, val in zip(refs[11:], grads):
            ref[...] = val

    shapes = tuple(jax.ShapeDtypeStruct(a.shape, F32) for a in (lam_re, lam_im, log_step, bt_re, bt_im, c_im))
    return pl.pallas_call(body, name=name, out_shape=shapes)(lam_re, lam_im, log_step, bt_re, bt_im, c_im, *cts)


def _cmul(ar, ai, br, bi):
    return ar * br - ai * bi, ar * bi + ai * br


def _s5_powers(lr, li):
    pows = [(lr, li)]
    for _ in range(TILE_ROWS - 1):
        pows.append(_cmul(pows[-1][0], pows[-1][1], lr, li))
    return pows


def _s5_carry_table(pows, n_lanes, reverse):
    row = lax.broadcasted_iota(jnp.int32, (TILE_ROWS, n_lanes), 0)
    t_re = jnp.zeros((TILE_ROWS, n_lanes), F32)
    t_im = jnp.zeros((TILE_ROWS, n_lanes), F32)
    for r in range(TILE_ROWS):
        p_re, p_im = pows[TILE_ROWS - 1 - r] if reverse else pows[r]
        t_re = jnp.where(row == r, p_re, t_re)
        t_im = jnp.where(row == r, p_im, t_im)
    return t_re, t_im


def _s5_tile(y_re, y_im, pows, reverse):
    d = 1
    while d < TILE_ROWS:
        p_re, p_im = pows[d - 1]
        if reverse:
            s_re, s_im = _shift_up(y_re, d), _shift_up(y_im, d)
        else:
            s_re, s_im = _shift_down(y_re, d), _shift_down(y_im, d)
        m_re, m_im = _cmul(p_re, p_im, s_re, s_im)
        y_re, y_im = y_re + m_re, y_im + m_im
        d *= 2
    return y_re, y_im


S5_BLOCKS = N_SHARD
S5_BLOCK_CH = S5_WIDTH // S5_BLOCKS
S5_BLOCK_LANES = S5_LANES // S5_BLOCKS
SCAN_PER_BLOCK = S5_BLOCK_LANES // S5_SCAN_LANES


def _s5_scan_specs(n_s, order):
    L = S5_SCAN_LANES

    def cat_spec(part):
        return pl.BlockSpec((1, 1, n_s, L), lambda *g: (order(*g)[1] // SCAN_PER_BLOCK, order(*g)[0], 0,
                                                        part * SCAN_PER_BLOCK + order(*g)[1] % SCAN_PER_BLOCK))

    one = pl.BlockSpec((1, 1, n_s, L), lambda *g: (order(*g)[1] // SCAN_PER_BLOCK, order(*g)[0], 0,
                                                   order(*g)[1] % SCAN_PER_BLOCK))
    lam = pl.BlockSpec((1, L), lambda *g: (0, order(*g)[1]))
    return cat_spec, one, lam


def _s5_scan(bu, lb_re, lb_im, *, name):
    n_blk, n_b, n_s, _ = bu.shape
    n_lb = S5_LANES // S5_SCAN_LANES
    n_tiles = n_s // TILE_ROWS
    L = S5_SCAN_LANES

    def body(re_ref, im_ref, lr_ref, li_ref, xr_ref, xi_ref):
        pows = _s5_powers(lr_ref[...], li_ref[...])
        t_re, t_im = _s5_carry_table(pows, L, False)

        def step(i, carry):
            rows = pl.ds(pl.multiple_of(i * TILE_ROWS, TILE_ROWS), TILE_ROWS)
            y_re, y_im = _s5_tile(re_ref[0, 0, rows, :], im_ref[0, 0, rows, :], pows, False)
            c_re, c_im = _cmul(t_re, t_im, carry[0], carry[1])
            y_re, y_im = y_re + c_re, y_im + c_im
            xr_ref[0, 0, rows, :] = y_re
            xi_ref[0, 0, rows, :] = y_im
            return y_re[TILE_ROWS - 1:, :], y_im[TILE_ROWS - 1:, :]

        zero = jnp.zeros((1, L), F32)
        lax.fori_loop(0, n_tiles, step, (zero, zero))

    cat_spec, one, lam = _s5_scan_specs(n_s, lambda b, j: (b, j))
    x_shape = jax.ShapeDtypeStruct((n_blk, n_b, n_s, S5_BLOCK_LANES), F32)
    return pl.pallas_call(
        body, name=name, grid=(n_b, n_lb),
        in_specs=[cat_spec(0), cat_spec(1), lam, lam],
        out_specs=(one, one), out_shape=(x_shape, x_shape),
        compiler_params=_params(4 * _nbytes((n_s, L), F32), 4 << 20),
    )(*_hbm(bu, bu, lb_re, lb_im))


def _s5_scan_bwd(dx, x_re, x_im, lb_re, lb_im, *, name):
    n_blk, n_b, n_s, _ = dx.shape
    n_lb = S5_LANES // S5_SCAN_LANES
    n_tiles = n_s // TILE_ROWS
    L = S5_SCAN_LANES

    def body(dr_ref, di_ref, xr_ref, xi_ref, lr_ref, li_ref, ar_ref, ai_ref, dlr_ref, dli_ref):
        pows = _s5_powers(lr_ref[...], -li_ref[...])
        t_re, t_im = _s5_carry_table(pows, L, True)
        row = lax.broadcasted_iota(jnp.int32, (TILE_ROWS, L), 0)

        def step(k, carry):
            c_re, c_im, s_re, s_im = carry
            i = n_tiles - 1 - k
            rows = pl.ds(pl.multiple_of(i * TILE_ROWS, TILE_ROWS), TILE_ROWS)
            a_re, a_im = _s5_tile(dr_ref[0, 0, rows, :], di_ref[0, 0, rows, :], pows, True)
            m_re, m_im = _cmul(t_re, t_im, c_re, c_im)
            a_re, a_im = a_re + m_re, a_im + m_im
            ar_ref[0, 0, rows, :] = a_re.astype(ar_ref.dtype)
            ai_ref[0, 0, rows, :] = a_im.astype(ai_ref.dtype)
            prev = pl.ds(pl.multiple_of(jnp.maximum(i - 1, 0) * TILE_ROWS, TILE_ROWS), TILE_ROWS)
            keep = jnp.where(i > 0, 1.0, 0.0)
            last_re = xr_ref[0, 0, prev, :][TILE_ROWS - 1:, :] * keep
            last_im = xi_ref[0, 0, prev, :][TILE_ROWS - 1:, :] * keep
            xp_re = jnp.where(row == 0, last_re, _shift_down(xr_ref[0, 0, rows, :], 1))
            xp_im = jnp.where(row == 0, last_im, _shift_down(xi_ref[0, 0, rows, :], 1))
            s_re = s_re + a_re * xp_re + a_im * xp_im
            s_im = s_im + a_im * xp_re - a_re * xp_im
            return a_re[:1, :], a_im[:1, :], s_re, s_im

        zero = jnp.zeros((1, L), F32)
        zt = jnp.zeros((TILE_ROWS, L), F32)
        _, _, s_re, s_im = lax.fori_loop(0, n_tiles, step, (zero, zero, zt, zt))
        first = pl.program_id(1) == 0
        _accumulate(dlr_ref, jnp.sum(s_re, axis=0, keepdims=True), first)
        _accumulate(dli_ref, jnp.sum(s_im, axis=0, keepdims=True), first)

    cat_spec, one, lam = _s5_scan_specs(n_s, lambda j, b: (b, j))
    a_shape = jax.ShapeDtypeStruct((n_blk, n_b, n_s, S5_BLOCK_LANES), MXU_DTYPE)
    lam_shape = jax.ShapeDtypeStruct((1, S5_LANES), F32)
    return pl.pallas_call(
        body, name=name, grid=(n_lb, n_b),
        in_specs=[cat_spec(0), cat_spec(1), one, one, lam, lam],
        out_specs=(one, one, lam, lam),
        out_shape=(a_shape, a_shape, lam_shape, lam_shape),
        compiler_params=_params(5 * _nbytes((n_s, L), F32), 4 << 20),
    )(*_hbm(dx, dx, x_re, x_im, lb_re, lb_im))


def _scan_rows(i):
    return pl.ds(pl.multiple_of(i * TILE_ROWS, TILE_ROWS), TILE_ROWS)


def _s5_mix_specs(n_s, order):
    jb = lambda *g: order(*g)[0]
    bb = lambda *g: order(*g)[1]
    act = pl.BlockSpec((1, 1, n_s, S5_BLOCK_CH), lambda *g: (bb(*g), 0, 0, jb(*g)))
    state = pl.BlockSpec((1, 1, n_s, S5_BLOCK_LANES), lambda *g: (jb(*g), bb(*g), 0, 0))
    lam = pl.BlockSpec((1, S5_BLOCK_LANES), lambda *g: (0, jb(*g)))
    w_in = pl.BlockSpec((1, S5_BLOCK_CH, S5_BLOCK_LANES), lambda *g: (jb(*g), 0, 0))
    w_out = pl.BlockSpec((1, S5_BLOCK_LANES, S5_BLOCK_CH), lambda *g: (jb(*g), 0, 0))
    return act, state, lam, w_in, w_out


def _s5_mix(u, wb_re, wb_im, lb_re, lb_im, wc_re, wc_im, *, name):
    n_b, n_s, _ = u.shape
    n_blk = S5_BLOCKS
    lanes = lambda t: t[:, None]
    n_tiles = n_s // TILE_ROWS
    L = S5_BLOCK_LANES

    def body(u_ref, wbr_ref, wbi_ref, lr_ref, li_ref, wcr_ref, wci_ref, y_ref, xr_ref, xi_ref):
        uv = u_ref[0, 0].astype(MXU_DTYPE)
        xr_ref[0, 0] = lax.dot_general(uv, wbr_ref[0].astype(MXU_DTYPE), _NN, preferred_element_type=F32)
        xi_ref[0, 0] = lax.dot_general(uv, wbi_ref[0].astype(MXU_DTYPE), _NN, preferred_element_type=F32)
        pows = _s5_powers(lr_ref[...], li_ref[...])
        t_re, t_im = _s5_carry_table(pows, L, False)

        def step(i, carry):
            rows = _scan_rows(i)
            y_re, y_im = _s5_tile(xr_ref[0, 0, rows, :], xi_ref[0, 0, rows, :], pows, False)
            c_re, c_im = _cmul(t_re, t_im, carry[0], carry[1])
            y_re, y_im = y_re + c_re, y_im + c_im
            xr_ref[0, 0, rows, :] = y_re
            xi_ref[0, 0, rows, :] = y_im
            return y_re[TILE_ROWS - 1:, :], y_im[TILE_ROWS - 1:, :]

        zero = jnp.zeros((1, L), F32)
        lax.fori_loop(0, n_tiles, step, (zero, zero))
        y_ref[0, 0] = (
            lax.dot_general(xr_ref[0, 0].astype(MXU_DTYPE), wcr_ref[0].astype(MXU_DTYPE), _NN, preferred_element_type=F32)
            + lax.dot_general(xi_ref[0, 0].astype(MXU_DTYPE), wci_ref[0].astype(MXU_DTYPE), _NN, preferred_element_type=F32))

    act, state, lam, w_in, w_out = _s5_mix_specs(n_s, lambda b, j: (j, b))
    x_shape = jax.ShapeDtypeStruct((n_blk, n_b, n_s, L), F32)
    return pl.pallas_call(
        body, name=name, grid=(n_b, n_blk),
        in_specs=[act, w_in, w_in, lam, lam, w_out, w_out], out_specs=(act, state, state),
        out_shape=(jax.ShapeDtypeStruct((n_b, 1, n_s, S5_WIDTH), F32), x_shape, x_shape),
        compiler_params=_params(2 * _nbytes((n_s, L), F32) + 2 * _nbytes((n_s, S5_BLOCK_CH), F32), 3 * _nbytes((n_s, L), F32)),
    )(*_hbm(lanes(u), wb_re, wb_im, lb_re, lb_im, wc_re, wc_im))


def _s5_mix_bwd(dy, du_skip, u, x_re, x_im, wb_re, wb_im, lb_re, lb_im, wc_re, wc_im, *, name):
    n_b, n_s, _ = u.shape
    n_blk = S5_BLOCKS
    lanes = lambda t: t[:, None]
    n_tiles = n_s // TILE_ROWS
    L = S5_BLOCK_LANES

    def body(dy_ref, ds_ref, u_ref, xr_ref, xi_ref, wbr_ref, wbi_ref, lr_ref, li_ref, wcr_ref, wci_ref,
             du_ref, dwbr_ref, dwbi_ref, dlr_ref, dli_ref, dwcr_ref, dwci_ref, ar_ref, ai_ref):
        dyv = dy_ref[0, 0].astype(MXU_DTYPE)
        ar_ref[...] = lax.dot_general(dyv, wcr_ref[0].astype(MXU_DTYPE), _NT, preferred_element_type=F32)
        ai_ref[...] = lax.dot_general(dyv, wci_ref[0].astype(MXU_DTYPE), _NT, preferred_element_type=F32)
        pows = _s5_powers(lr_ref[...], -li_ref[...])
        t_re, t_im = _s5_carry_table(pows, L, True)
        row = lax.broadcasted_iota(jnp.int32, (TILE_ROWS, L), 0)

        def step(k, carry):
            c_re, c_im, s_re, s_im = carry
            i = n_tiles - 1 - k
            rows = _scan_rows(i)
            a_re, a_im = _s5_tile(ar_ref[rows, :], ai_ref[rows, :], pows, True)
            m_re, m_im = _cmul(t_re, t_im, c_re, c_im)
            a_re, a_im = a_re + m_re, a_im + m_im
            ar_ref[rows, :] = a_re
            ai_ref[rows, :] = a_im
            prev = _scan_rows(jnp.maximum(i - 1, 0))
            keep = jnp.where(i > 0, 1.0, 0.0)
            last_re = xr_ref[0, 0, prev, :][TILE_ROWS - 1:, :] * keep
            last_im = xi_ref[0, 0, prev, :][TILE_ROWS - 1:, :] * keep
            xp_re = jnp.where(row == 0, last_re, _shift_down(xr_ref[0, 0, rows, :], 1))
            xp_im = jnp.where(row == 0, last_im, _shift_down(xi_ref[0, 0, rows, :], 1))
            s_re = s_re + a_re * xp_re + a_im * xp_im
            s_im = s_im + a_im * xp_re - a_re * xp_im
            return a_re[:1, :], a_im[:1, :], s_re, s_im

        zero = jnp.zeros((1, L), F32)
        zt = jnp.zeros((TILE_ROWS, L), F32)
        _, _, s_re, s_im = lax.fori_loop(0, n_tiles, step, (zero, zero, zt, zt))
        first = pl.program_id(1) == 0
        _accumulate(dlr_ref, jnp.sum(s_re, axis=0, keepdims=True), first)
        _accumulate(dli_ref, jnp.sum(s_im, axis=0, keepdims=True), first)
        a_re, a_im = ar_ref[...].astype(MXU_DTYPE), ai_ref[...].astype(MXU_DTYPE)
        du = (lax.dot_general(a_re, wbr_ref[0].astype(MXU_DTYPE), _NT, preferred_element_type=F32)
              + lax.dot_general(a_im, wbi_ref[0].astype(MXU_DTYPE), _NT, preferred_element_type=F32))
        du_ref[0, 0] = (du + ds_ref[0, 0]).astype(du_ref.dtype)
        uv = u_ref[0, 0].astype(MXU_DTYPE)
        _accumulate(dwbr_ref, lax.dot_general(uv, a_re, _TN, preferred_element_type=F32)[None], first)
        _accumulate(dwbi_ref, lax.dot_general(uv, a_im, _TN, preferred_element_type=F32)[None], first)
        _accumulate(dwcr_ref, lax.dot_general(xr_ref[0, 0].astype(MXU_DTYPE), dyv, _TN, preferred_element_type=F32)[None], first)
        _accumulate(dwci_ref, lax.dot_general(xi_ref[0, 0].astype(MXU_DTYPE), dyv, _TN, preferred_element_type=F32)[None], first)

    act, state, lam, w_in, w_out = _s5_mix_specs(n_s, lambda j, b: (j, b))
    lam_shape = jax.ShapeDtypeStruct((1, S5_LANES), F32)
    return pl.pallas_call(
        body, name=name, grid=(n_blk, n_b),
        in_specs=[act, act, act, state, state, w_in, w_in, lam, lam, w_out, w_out],
        out_specs=(act, w_in, w_in, lam, lam, w_out, w_out),
        out_shape=(jax.ShapeDtypeStruct((n_b, 1, n_s, S5_WIDTH), MXU_DTYPE), jax.ShapeDtypeStruct(wb_re.shape, F32),
                   jax.ShapeDtypeStruct(wb_im.shape, F32), lam_shape, lam_shape,
                   jax.ShapeDtypeStruct(wc_re.shape, F32), jax.ShapeDtypeStruct(wc_im.shape, F32)),
        scratch_shapes=[pltpu.VMEM((n_s, L), F32), pltpu.VMEM((n_s, L), F32)],
        compiler_params=_params(2 * _nbytes((n_s, L), F32) + 4 * _nbytes((n_s, S5_BLOCK_CH), F32), 5 * _nbytes((n_s, L), F32)),
    )(*_hbm(lanes(dy), lanes(du_skip), lanes(u), x_re, x_im, wb_re, wb_im, lb_re, lb_im, wc_re, wc_im))


def _s5_out_fn(ymm, u, d_skip, w_glu, b_glu):
    y = jax.nn.gelu(ymm + d_skip * u)
    return y * jax.nn.sigmoid(_mdot(y, w_glu, _NN) + b_glu)


def _s5_out_specs(tm):
    rows = pl.BlockSpec((tm, S5_WIDTH), lambda i: (i, 0))
    vec = pl.BlockSpec((1, S5_WIDTH), lambda i: (0, 0))
    mat = pl.BlockSpec((S5_WIDTH, S5_WIDTH), lambda i: (0, 0))
    return rows, vec, mat


def _s5_out(ymm, u, d_skip, w_glu, b_glu, *, name, tm=512):
    n_tok = ymm.shape[0]
    tm = min(tm, n_tok)
    rows, vec, mat = _s5_out_specs(tm)

    def body(y_ref, u_ref, d_ref, w_ref, b_ref, o_ref):
        o_ref[...] = _s5_out_fn(y_ref[...], u_ref[...], d_ref[...], w_ref[...], b_ref[...]).astype(o_ref.dtype)

    return pl.pallas_call(
        body, name=name, grid=(n_tok // tm,), in_specs=[rows, rows, vec, mat, vec], out_specs=rows,
        out_shape=jax.ShapeDtypeStruct((n_tok, S5_WIDTH), MXU_DTYPE),
        compiler_params=_params(4 * _nbytes((tm, S5_WIDTH), F32), 8 * _nbytes((tm, S5_WIDTH), F32)),
    )(*_hbm(ymm, u, d_skip, w_glu, b_glu))


def _s5_out_bwd(ymm, u, d_skip, w_glu, b_glu, dout, *, name, tm=512):
    n_tok = ymm.shape[0]
    tm = min(tm, n_tok)
    rows, vec, mat = _s5_out_specs(tm)

    def body(y_ref, u_ref, d_ref, w_ref, b_ref, do_ref, dy_ref, du_ref, dd_ref, dw_ref, db_ref):
        _, pull = jax.vjp(_s5_out_fn, y_ref[...], u_ref[...], d_ref[...], w_ref[...].astype(F32), b_ref[...])
        dy, du, dd, dw, db = pull(do_ref[...])
        dy_ref[...] = dy.astype(dy_ref.dtype)
        du_ref[...] = du
        first = pl.program_id(0) == 0
        _accumulate(dd_ref, dd, first)
        _accumulate(dw_ref, dw, first)
        _accumulate(db_ref, db, first)

    return pl.pallas_call(
        body, name=name, grid=(n_tok // tm,), in_specs=[rows, rows, vec, mat, vec, rows],
        out_specs=(rows, rows, vec, mat, vec),
        out_shape=(jax.ShapeDtypeStruct(ymm.shape, MXU_DTYPE), jax.ShapeDtypeStruct(ymm.shape, F32),
                   jax.ShapeDtypeStruct((1, S5_WIDTH), F32), jax.ShapeDtypeStruct((S5_WIDTH, S5_WIDTH), F32),
                   jax.ShapeDtypeStruct((1, S5_WIDTH), F32)),
        compiler_params=_params(6 * _nbytes((tm, S5_WIDTH), F32), 12 * _nbytes((tm, S5_WIDTH), F32)),
    )(*_hbm(ymm, u, d_skip, w_glu, b_glu, dout))


def _merge_fn(ga, gb, ya, yb):
    return jax.nn.sigmoid(ga) * ya + jax.nn.sigmoid(gb) * yb


def _merge(gab, ya, yb, *, name, tm=512):
    n_tok = ya.shape[0]
    tm = min(tm, n_tok)
    rows = pl.BlockSpec((tm, D_MODEL), lambda i: (i, 0))

    def body(ga_ref, gb_ref, ya_ref, yb_ref, o_ref):
        o_ref[...] = _merge_fn(ga_ref[...], gb_ref[...], ya_ref[...], yb_ref[...]).astype(o_ref.dtype)

    return pl.pallas_call(
        body, name=name, grid=(n_tok // tm,),
        in_specs=[rows, pl.BlockSpec((tm, D_MODEL), lambda i: (i, 1)), rows, rows], out_specs=rows,
        out_shape=jax.ShapeDtypeStruct(ya.shape, MXU_DTYPE),
        compiler_params=_params(5 * _nbytes((tm, D_MODEL), F32), 4 * _nbytes((tm, D_MODEL), F32)),
    )(*_hbm(gab, gab, ya, yb))


def _merge_bwd(gab, ya, yb, dout, *, name, tm=512):
    n_tok = ya.shape[0]
    tm = min(tm, n_tok)
    rows = pl.BlockSpec((tm, D_MODEL), lambda i: (i, 0))

    def body(ga_ref, gb_ref, ya_ref, yb_ref, do_ref, *out_refs):
        _, pull = jax.vjp(_merge_fn, ga_ref[...], gb_ref[...], ya_ref[...], yb_ref[...])
        for ref, val in zip(out_refs, pull(do_ref[...])):
            ref[...] = val.astype(ref.dtype)

    shape = jax.ShapeDtypeStruct(ya.shape, MXU_DTYPE)
    return pl.pallas_call(
        body, name=name, grid=(n_tok // tm,),
        in_specs=[rows, pl.BlockSpec((tm, D_MODEL), lambda i: (i, 1)), rows, rows, rows],
        out_specs=(rows, rows, rows, rows), out_shape=(shape, shape, shape, shape),
        compiler_params=_params(7 * _nbytes((tm, D_MODEL), F32), 6 * _nbytes((tm, D_MODEL), F32)),
    )(*_hbm(gab, gab, ya, yb, dout))


ADA_SHARD = N_MOD * D_MODEL // N_SHARD


def _ada_fwd(c_pad, w_s, b_s, *, name):
    n_r = c_pad.shape[0]

    def body(c_ref, w_ref, b_ref, o_ref):
        sc = jax.nn.silu(c_ref[...]).astype(MXU_DTYPE)
        o_ref[0] = lax.dot_general(sc, w_ref[0].astype(MXU_DTYPE), _NN, preferred_element_type=F32) + b_ref[0]

    return pl.pallas_call(
        body, name=name, grid=(N_SHARD,),
        in_specs=[pl.BlockSpec((n_r, D_MODEL), lambda s: (0, 0)),
                  pl.BlockSpec((1, D_MODEL, ADA_SHARD), lambda s: (s, 0, 0)),
                  pl.BlockSpec((1, 1, ADA_SHARD), lambda s: (s, 0, 0))],
        out_specs=pl.BlockSpec((1, n_r, ADA_SHARD), lambda s: (s, 0, 0)),
        out_shape=jax.ShapeDtypeStruct((N_SHARD, n_r, ADA_SHARD), F32),
        compiler_params=_params(_nbytes((D_MODEL, ADA_SHARD), w_s.dtype), 1 << 20),
    )(*_hbm(c_pad, w_s, b_s))


def _ada_bwd(c_pad, dmod_s, *, name):
    n_r = c_pad.shape[0]

    def body(c_ref, d_ref, dw_ref, db_ref):
        sc = jax.nn.silu(c_ref[...])
        dm = d_ref[0]
        dw_ref[0] = _fdot(sc, dm, _TN)
        db_ref[0] = jnp.sum(dm, axis=0, keepdims=True)

    return pl.pallas_call(
        body, name=name, grid=(N_SHARD,),
        in_specs=[pl.BlockSpec((n_r, D_MODEL), lambda s: (0, 0)), pl.BlockSpec((1, n_r, ADA_SHARD), lambda s: (s, 0, 0))],
        out_specs=(pl.BlockSpec((1, D_MODEL, ADA_SHARD), lambda s: (s, 0, 0)),
                   pl.BlockSpec((1, 1, ADA_SHARD), lambda s: (s, 0, 0))),
        out_shape=(jax.ShapeDtypeStruct((N_SHARD, D_MODEL, ADA_SHARD), F32),
                   jax.ShapeDtypeStruct((N_SHARD, 1, ADA_SHARD), F32)),
        compiler_params=_params(_nbytes((D_MODEL, ADA_SHARD), F32), 2 * _nbytes((D_MODEL, ADA_SHARD), F32)),
    )(*_hbm(c_pad, dmod_s))


def _heads(t, n_b, n_s):
    return t.reshape(n_b, n_s, DN_HEADS, DN_HEAD_DIM).transpose(0, 2, 1, 3)


def _unheads(t):
    n_b, _, n_s, _ = t.shape
    return t.transpose(0, 2, 1, 3).reshape(n_b, n_s, DN_WIDTH)


def _block_diag(blocks):
    n_per = S5_GROUPS // S5_BLOCKS
    _, n_r, n_c = blocks.shape
    b4 = blocks.reshape(S5_BLOCKS, n_per, n_r, n_c)
    eye = jnp.eye(n_per, dtype=blocks.dtype)
    return (b4[:, :, :, None, :] * eye[None, :, None, :, None]).reshape(S5_BLOCKS, n_per * n_r, n_per * n_c)


def _diag_blocks(mat, n_r, n_c):
    n_per = S5_GROUPS // S5_BLOCKS
    m5 = mat.reshape(S5_BLOCKS, n_per, n_r, n_per, n_c)
    eye = jnp.eye(n_per, dtype=mat.dtype)
    return jnp.sum(m5 * eye[None, :, None, :, None], axis=3).reshape(S5_GROUPS, n_r, n_c)


def _local_step(x, c, target, wts):
    n_b, n_s, _ = x.shape
    n_tok = n_b * n_s
    flat = lambda t: t.reshape(n_tok, t.shape[-1])
    unflat = lambda t: t.reshape(n_b, n_s, t.shape[-1])
    n_chunks = n_s // CHUNK

    c_pad = jnp.zeros((SUBLANES, D_MODEL), F32).at[:n_b].set(c)
    mod_s = _ada_fwd(c_pad, wts["w_ada"], wts["b_ada"], name="ada_fwd")
    mod = mod_s.transpose(1, 0, 2).reshape(SUBLANES, N_MOD * D_MODEL)[:n_b]
    sh1, sc1, gt1, sh2, sc2, gt2, sh3, sc3, gt3 = [m[:, None, :] for m in jnp.split(mod, N_MOD, axis=-1)]

    a1 = _pre(x, None, None, wts["g_ffn1"], sh1, sc1, 0.0, name="pre1")
    f1, ffn1_saved = _ffn_fwd(flat(a1), wts["w1_ffn1"], wts["w3_ffn1"], wts["w2_ffn1"], "ffn1")
    x1, a2 = _pre(x, unflat(f1), gt1, wts["g_mix"], sh2, sc2, 0.5, name="pre2")
    u = flat(a2)[None]
    p_qkv = _mm_act([(u, wts["w_qkv"])], "nt", name="in_qkv")[0]
    p_z = _mm_act([(u, wts["w_z"])], "nt", name="in_z")[0]
    p_gab = _mm_act([(u, wts["w_gab"])], "nt", name="in_gab")[0]
    p_s5 = _mm_act([(u, wts["w_s5"])], "nt", name="in_s5")[0]
    p_ba = _mm_act([(u, wts["w_ba"])], "nt", name="in_ba")[0]

    qkv_c = _conv_fwd(unflat(p_qkv), wts["conv_qkv"], name="conv_fwd")
    z_tok = unflat(p_z)
    ba = p_ba.reshape(n_b, n_s, BA_PAD)
    head_rows = lambda t: t.transpose(0, 2, 1).reshape(n_b, DN_HEADS, n_chunks, 1, CHUNK)
    blr = head_rows(ba[:, :, :DN_HEADS])
    lar = head_rows(ba[:, :, DN_HEADS:2 * DN_HEADS])
    a_log, dt_bias = wts["a_log"], wts["dt_bias"]
    dn_in = (qkv_c, blr, lar, a_log, dt_bias)
    qd, kd, uc, wc, attn, g_last, dn_inv = _dn_prep(*dn_in, name="dn_prep")
    o, states = _dn_scan(qd, kd, uc, wc, attn, g_last, name="dn_scan")
    og = _dn_post(o, z_tok, wts["g_onorm"], name="dn_post")
    og_t = og.reshape(1, n_tok, DN_WIDTH)
    ya = _mm_act([(og_t, wts["w_proj_a"])], "nn", name="proj_a")[0]

    s5p_in = (wts["lam_re"], wts["lam_im"], wts["log_step"], wts["bt_re"], wts["bt_im"], wts["c_im"])
    lb_re, lb_im, bb_re, bb_im, c_neg = _s5_prep(*s5p_in, name="s5_prep")
    wb_re, wb_im = _block_diag(bb_re), _block_diag(bb_im)
    wc_re = _block_diag(wts["c_re"].transpose(0, 2, 1))
    wc_im = _block_diag(c_neg.transpose(0, 2, 1))
    lbr, lbi = lb_re.reshape(1, S5_LANES), lb_im.reshape(1, S5_LANES)
    s5_w = (wb_re, wb_im, lbr, lbi, wc_re, wc_im)
    ymm, x_re, x_im = _s5_mix(unflat(p_s5), *s5_w, name="s5_mix")
    ymm = ymm.reshape(n_tok, S5_WIDTH)
    y2 = _s5_out(ymm, p_s5, wts["d_skip"], wts["w_glu"], wts["b_glu"], name="s5_out")
    yb = _mm_act([(y2[None], wts["w_proj_b"])], "nn", name="proj_b")[0]

    merged = _merge(p_gab, ya, yb, name="merge")
    m_out = _mm_act([(merged[None], wts["w_out"])], "nn", name="mix_out")[0]
    x2, a3 = _pre(x1, unflat(m_out), gt2, wts["g_ffn2"], sh3, sc3, 1.0, name="pre3")
    f3, ffn2_saved = _ffn_fwd(flat(a3), wts["w1_ffn2"], wts["w3_ffn2"], wts["w2_ffn2"], "ffn2")

    g = {}
    loss, dx2_res, df3, dgt3, g["g_final"] = _final(x2, unflat(f3), gt3, wts["g_final"], target, name="final")
    da3, g["w1_ffn2"], g["w3_ffn2"], g["w2_ffn2"] = _ffn_bwd(
        flat(a3), wts["w1_ffn2"], wts["w3_ffn2"], wts["w2_ffn2"], ffn2_saved, flat(df3), "ffn2")
    dx1_res, dm_out, dgt2, g["g_ffn2"], dsh3, dsc3 = _pre_bwd(
        x1, unflat(m_out), gt2, wts["g_ffn2"], sh3, sc3, 1.0, unflat(da3), dx2_res, name="pre3_bwd")
    dm_out = flat(dm_out)[None]
    dmerged = _mm_act([(dm_out, wts["w_out"])], "nt", name="mix_out_bwd")[0]
    g["w_out"] = _mm_tn(merged[None], dm_out, name="dw_out")[0]
    dga, dgb, dya, dyb = _merge_bwd(p_gab, ya, yb, dmerged, name="merge_bwd")

    dy2 = _mm_act([(dyb[None], wts["w_proj_b"])], "nt", name="proj_b_bwd")[0]
    g["w_proj_b"] = _mm_tn(y2[None], dyb[None], name="dw_proj_b")[0]
    dymm, du_skip, g["d_skip"], g["w_glu"], g["b_glu"] = _s5_out_bwd(
        ymm, p_s5, wts["d_skip"], wts["w_glu"], wts["b_glu"], dy2, name="s5_out_bwd")
    dp_s5, dwb_re, dwb_im, dlb_re, dlb_im, dwc_re, dwc_im = _s5_mix_bwd(
        unflat(dymm), unflat(du_skip), unflat(p_s5), x_re, x_im, *s5_w, name="s5_mix_bwd")
    dp_s5 = dp_s5.reshape(n_tok, S5_WIDTH)
    g["c_re"] = _diag_blocks(dwc_re, S5_STATE, S5_GROUP_CH).transpose(0, 2, 1)
    s5_cts = (dlb_re.reshape(lb_re.shape), dlb_im.reshape(lb_im.shape),
              _diag_blocks(dwb_re, S5_GROUP_CH, S5_STATE), _diag_blocks(dwb_im, S5_GROUP_CH, S5_STATE),
              _diag_blocks(dwc_im, S5_STATE, S5_GROUP_CH).transpose(0, 2, 1))
    g["lam_re"], g["lam_im"], g["log_step"], g["bt_re"], g["bt_im"], g["c_im"] = _s5_prep_bwd(
        *s5p_in, s5_cts, name="s5_prep_bwd")

    dog = _mm_act([(dya[None], wts["w_proj_a"])], "nt", name="proj_a_bwd")[0]
    g["w_proj_a"] = _mm_tn(og_t, dya[None], name="dw_proj_a")[0]
    do, dz, g["g_onorm"] = _dn_post_bwd(o, z_tok, wts["g_onorm"], unflat(dog), name="dn_post_bwd")
    scan_cts = _dn_scan_bwd(qd, kd, uc, wc, attn, g_last, states, do, name="dn_scan_bwd")
    dqkv_c, dblr, dlar, g["a_log"], g["dt_bias"] = _dn_prep_bwd(*dn_in, dn_inv, uc, wc, scan_cts, name="dn_prep_bwd")
    dqkv, g["conv_qkv"] = _conv_bwd(unflat(p_qkv), wts["conv_qkv"], dqkv_c, name="conv_bwd")
    token_cols = lambda t: t.reshape(n_b, DN_HEADS, n_s).transpose(0, 2, 1)
    dba = jnp.concatenate([token_cols(dblr), token_cols(dlar),
                           jnp.zeros((n_b, n_s, BA_PAD - 2 * DN_HEADS), F32)], axis=-1).astype(MXU_DTYPE)

    dps = {"w_qkv": flat(dqkv)[None], "w_z": flat(dz)[None], "w_ga": dga[None], "w_gb": dgb[None],
           "w_s5": dp_s5[None], "w_ba": flat(dba)[None]}
    w_ga, w_gb = wts["w_gab"][:, :D_MODEL], wts["w_gab"][:, D_MODEL:]
    w_of = dict(wts, w_ga=w_ga, w_gb=w_gb)
    du = _mm_act([(dps[k], w_of[k]) for k in dps], "nn", name="in_bwd")[0]
    for k in dps:
        g[k] = _mm_tn(dps[k], u, name=f"d{k}")[0]
    dx0_res, df1, dgt1, g["g_mix"], dsh2, dsc2 = _pre_bwd(
        x, unflat(f1), gt1, wts["g_mix"], sh2, sc2, 0.5, unflat(du), dx1_res, name="pre2_bwd")
    da1, g["w1_ffn1"], g["w3_ffn1"], g["w2_ffn1"] = _ffn_bwd(
        flat(a1), wts["w1_ffn1"], wts["w3_ffn1"], wts["w2_ffn1"], ffn1_saved, flat(df1), "ffn1")
    grad_x, g["g_ffn1"], dsh1, dsc1 = _pre_bwd(
        x, None, None, wts["g_ffn1"], sh1, sc1, 0.0, unflat(da1), dx0_res, name="pre1_bwd")

    dmod = jnp.concatenate([t[:, 0, :] for t in (dsh1, dsc1, dgt1, dsh2, dsc2, dgt2, dsh3, dsc3, dgt3)], axis=-1)
    return loss, grad_x, g, dmod


def _ada_grads(c_rows, dmod_rows):
    n_r = c_rows.shape[0]
    n_pad = -n_r % SUBLANES
    c_pad = jnp.pad(c_rows, ((0, n_pad), (0, 0)))
    dmod_s = jnp.pad(dmod_rows, ((0, n_pad), (0, 0))).reshape(n_r + n_pad, N_SHARD, ADA_SHARD).transpose(1, 0, 2)
    dw, db = _ada_bwd(c_pad, dmod_s, name="ada_bwd")
    return dw, db.reshape(1, N_MOD * D_MODEL)


IN_SPLITS = (("w_qkv", 3 * DN_WIDTH), ("w_z", DN_WIDTH), ("w_ba", 2 * DN_HEADS), ("w_s5", S5_WIDTH),
             ("w_ga", D_MODEL), ("w_gb", D_MODEL))
SHARDED = ("w_ada", "w1_ffn1", "w3_ffn1", "w2_ffn1", "w_in", "conv_qkv", "w_glu", "w_proj_a", "w_proj_b", "w_out",
           "w1_ffn2", "w3_ffn2", "w2_ffn2")
COLUMN_SHARDED = ("w_ada", "w1_ffn1", "w3_ffn1", "w_in", "conv_qkv", "w_proj_a", "w_proj_b", "w1_ffn2", "w3_ffn2")


def _cat_columns(stack):
    return stack.transpose(1, 0, 2).reshape(stack.shape[1], N_SHARD * stack.shape[2])


def _split_columns(full):
    n_r, n_c = full.shape
    return full.reshape(n_r, N_SHARD, n_c // N_SHARD).transpose(1, 0, 2)


def _gathered_weights(st, rep):
    w = {k: st[k] for k in ("w_ada", "w1_ffn1", "w3_ffn1", "w2_ffn1", "w1_ffn2", "w3_ffn2", "w2_ffn2")}
    w["b_ada"] = rep["b_ada"].reshape(N_SHARD, 1, ADA_SHARD)
    for k in ("g_ffn1", "g_mix", "g_ffn2", "g_final"):
        w[k] = rep[k].reshape(1, D_MODEL)
    w_in_t = st["w_in"].reshape(N_SHARD * st["w_in"].shape[1], D_MODEL)
    start = 0
    for k, size in IN_SPLITS:
        w[k] = w_in_t[None, start:start + size]
        start += size
    w["w_gab"] = jnp.concatenate([w.pop("w_ga"), w.pop("w_gb")], axis=1)
    w["w_ba"] = jnp.pad(w["w_ba"], ((0, 0), (0, BA_PAD - 2 * DN_HEADS), (0, 0)))
    w["conv_qkv"] = _cat_columns(st["conv_qkv"])
    w["a_log"] = rep["a_log"].reshape(DN_HEADS, 1, 1)
    w["dt_bias"] = rep["dt_bias"].reshape(DN_HEADS, 1, 1)
    w["g_onorm"] = rep["g_onorm"].reshape(1, DN_HEAD_DIM)
    w["lam_re"] = rep["lam_re"].reshape(S5_GROUPS, 1, S5_STATE)
    w["lam_im"] = rep["lam_im"].reshape(S5_GROUPS, 1, S5_STATE)
    w["log_step"] = rep["log_step"].reshape(S5_GROUPS, 1, 1)
    w["bt_re"] = rep["b_re"][0].transpose(0, 2, 1)
    w["bt_im"] = rep["b_im"][0].transpose(0, 2, 1)
    w["c_re"] = rep["c_re"][0]
    w["c_im"] = rep["c_im"][0]
    w["d_skip"] = rep["d_skip"].reshape(1, S5_WIDTH)
    w["b_glu"] = rep["b_glu"].reshape(1, S5_WIDTH)
    w["w_glu"] = st["w_glu"].reshape(S5_WIDTH, S5_WIDTH)
    w["w_proj_a"] = _cat_columns(st["w_proj_a"])[None]
    w["w_proj_b"] = _cat_columns(st["w_proj_b"])[None]
    w["w_out"] = st["w_out"].reshape(1, D_MODEL, D_MODEL)
    return w


def _grads_to_problem_layout(g):
    st = {k: g[k] for k in ("w1_ffn1", "w3_ffn1", "w2_ffn1", "w1_ffn2", "w3_ffn2", "w2_ffn2")}
    w_in_t = jnp.concatenate([g[k][:size] for k, size in IN_SPLITS], axis=0)
    st["w_in"] = w_in_t.reshape(N_SHARD, w_in_t.shape[0] // N_SHARD, D_MODEL)
    st["w_glu"] = g["w_glu"].reshape(N_SHARD, S5_WIDTH // N_SHARD, S5_WIDTH)
    st["w_proj_a"] = _split_columns(g["w_proj_a"])
    st["w_proj_b"] = _split_columns(g["w_proj_b"])
    st["w_out"] = g["w_out"].reshape(N_SHARD, D_MODEL // N_SHARD, D_MODEL)
    small = {
        "g_ffn1": g["g_ffn1"], "g_mix": g["g_mix"], "g_ffn2": g["g_ffn2"], "g_final": g["g_final"].reshape(D_MODEL),
        "conv_qkv": g["conv_qkv"][None],
        "a_log": g["a_log"].reshape(1, DN_HEADS), "dt_bias": g["dt_bias"].reshape(1, DN_HEADS),
        "g_onorm": g["g_onorm"],
        "lam_re": g["lam_re"].reshape(1, S5_GROUPS, S5_STATE), "lam_im": g["lam_im"].reshape(1, S5_GROUPS, S5_STATE),
        "log_step": g["log_step"].reshape(1, S5_GROUPS),
        "b_re": g["bt_re"].transpose(0, 2, 1)[None], "b_im": g["bt_im"].transpose(0, 2, 1)[None],
        "c_re": g["c_re"][None], "c_im": g["c_im"][None],
        "d_skip": g["d_skip"], "b_glu": g["b_glu"],
    }
    return st, small


ELEMENTWISE_BLOCK_BYTES = 1 << 20


def _row_tile(n_rows, n_cols, n_lead=1, multiple=SUBLANES):
    best = None
    for t in range(multiple, n_rows + 1, multiple):
        if n_rows % t == 0 and n_lead * t * n_cols * 4 <= ELEMENTWISE_BLOCK_BYTES:
            best = t
    return best if best is not None else n_rows


def _add_sibling_half(g4, recv, my_c, *, name):
    n_sh, _, n_h, n_c = g4.shape
    th = _row_tile(n_h, n_c, multiple=2 * SUBLANES)

    def body(c_ref, g_ref, r_ref, o_ref):
        o_ref[0] = (g_ref[0, 0] + r_ref[0]).astype(o_ref.dtype)

    grid_spec = pltpu.PrefetchScalarGridSpec(
        num_scalar_prefetch=1, grid=(n_sh, n_h // th),
        in_specs=[pl.BlockSpec((1, 1, th, n_c), lambda s, i, c_ref: (s, c_ref[0], i, 0)),
                  pl.BlockSpec((1, th, n_c), lambda s, i, c_ref: (s, i, 0))],
        out_specs=pl.BlockSpec((1, th, n_c), lambda s, i, c_ref: (s, i, 0)))
    return pl.pallas_call(
        body, name=name, grid_spec=grid_spec, out_shape=jax.ShapeDtypeStruct((n_sh, n_h, n_c), MXU_DTYPE),
        compiler_params=_params(3 * _nbytes((th, n_c), F32)),
    )(*_hbm(my_c, g4, recv))


def _sum_slots(parts, *, name):
    n_p, n_r, n_c = parts.shape
    th = _row_tile(n_r, n_c, n_p)

    def body(p_ref, o_ref):
        total = p_ref[0].astype(F32)
        for k in range(1, n_p):
            total = total + p_ref[k].astype(F32)
        o_ref[...] = total

    return pl.pallas_call(
        body, name=name, grid=(n_r // th,),
        in_specs=[pl.BlockSpec((n_p, th, n_c), lambda i: (0, i, 0))],
        out_specs=pl.BlockSpec((th, n_c), lambda i: (i, 0)),
        out_shape=jax.ShapeDtypeStruct((n_r, n_c), F32),
        compiler_params=_params((n_p + 1) * _nbytes((th, n_c), F32)),
    )(*_hbm(parts))


def _cast_into_slot(w, place, dtype, *, name):
    n_r, n_c = w.shape
    th = _row_tile(n_r, n_c, multiple=2 * SUBLANES)

    def body(p_ref, w_ref, o_ref):
        o_ref[0] = w_ref[...].astype(o_ref.dtype)

    grid_spec = pltpu.PrefetchScalarGridSpec(
        num_scalar_prefetch=1, grid=(n_r // th,),
        in_specs=[pl.BlockSpec((th, n_c), lambda i, p: (i, 0))],
        out_specs=pl.BlockSpec((1, th, n_c), lambda i, p: (p[1], i, 0)))
    return pl.pallas_call(
        body, name=name, grid_spec=grid_spec, out_shape=jax.ShapeDtypeStruct((N_SHARD, n_r, n_c), dtype),
        compiler_params=_params(2 * _nbytes((th, n_c), F32)),
    )(*_hbm(place, w))


def _sum_chips(own, parts, place, *, name):
    n_sh, n_h, n_c = own.shape
    th = _row_tile(n_h, n_c, n_sh, multiple=2 * SUBLANES)

    def body(p_ref, own_ref, a_ref, b_ref, c_ref, o_ref):
        o_ref[0] = ((own_ref[0].astype(F32) + a_ref[0].astype(F32)) + b_ref[0].astype(F32)) + c_ref[0].astype(F32)

    slab = lambda k: pl.BlockSpec((1, th, n_c), lambda i, p, k=k: (p[k], i, 0))
    grid_spec = pltpu.PrefetchScalarGridSpec(
        num_scalar_prefetch=1, grid=(n_h // th,),
        in_specs=[slab(1), slab(2), slab(3), slab(4)], out_specs=slab(0))
    return pl.pallas_call(
        body, name=name, grid_spec=grid_spec, out_shape=jax.ShapeDtypeStruct((2, n_h, n_c), F32),
        compiler_params=_params(5 * _nbytes((th, n_c), F32)),
    )(*_hbm(place, own, parts, parts, parts))


def _adamw(w, g, m, v, *, name):
    n_r, n_c = w.shape
    th = _row_tile(n_r, n_c)
    tc = n_c
    if th == n_r and n_c % LANES == 0:
        tc = max(t for t in range(LANES, n_c + 1, LANES) if n_c % t == 0 and (n_r * t * 4 <= ELEMENTWISE_BLOCK_BYTES or t == LANES))
    bias1 = 1.0 - ADAM_B1 ** ADAM_STEP
    bias2 = 1.0 - ADAM_B2 ** ADAM_STEP

    def body(w_ref, g_ref, m_ref, v_ref, d_ref, mo_ref, vo_ref):
        gv = g_ref[...]
        m_new = ADAM_B1 * m_ref[...] + (1.0 - ADAM_B1) * gv
        v_new = ADAM_B2 * v_ref[...] + (1.0 - ADAM_B2) * jnp.square(gv)
        d_ref[...] = -ADAM_LR * ((m_new / bias1) / (jnp.sqrt(v_new / bias2) + ADAM_EPS) + ADAM_WD * w_ref[...])
        mo_ref[...] = m_new
        vo_ref[...] = v_new

    spec = pl.BlockSpec((th, tc), lambda i, j: (i, j))
    shape = jax.ShapeDtypeStruct((n_r, n_c), F32)
    return pl.pallas_call(
        body, name=name, grid=(n_r // th, n_c // tc), in_specs=[spec] * 4, out_specs=(spec,) * 3, out_shape=(shape,) * 3,
        compiler_params=_params(7 * _nbytes((th, tc), F32)),
    )(*_hbm(w, g, m, v))


CHIP_FLIPS = ((1, 0), (0, 1), (1, 1))
DEVICE_FLIPS = tuple((fx, fy, fc) for fx in (0, 1) for fy in (0, 1) for fc in (0, 1))[1:]


def _exchange(ins, out_shapes, plan, n_local, n_remote, *, name, aliased=False):
    n_in, n_out = len(ins), len(out_shapes)

    def body(*refs):
        in_refs, out_refs = refs[:n_in], refs[n_in:n_in + n_out]
        send_sems, recv_sems, local_sems = refs[n_in + n_out:]
        me = (lax.axis_index("x"), lax.axis_index("y"), lax.axis_index("c"))
        local, remote = plan(in_refs, out_refs, me)
        assert len(local) == n_local and len(remote) == n_remote
        here = [pltpu.make_async_copy(src, dst, local_sems.at[i]) for i, (src, dst) in enumerate(local)]
        for cp in here:
            cp.start()
        sends = [pltpu.make_async_remote_copy(src_ref=src, dst_ref=dst, send_sem=send_sems.at[i], recv_sem=recv_sems.at[i],
                                              device_id=peer, device_id_type=pl.DeviceIdType.MESH)
                 for i, (src, dst, _, peer) in enumerate(remote)]
        for cp in sends:
            cp.start()
        for i, (src, _, landing, peer) in enumerate(remote):
            pltpu.make_async_remote_copy(src_ref=src, dst_ref=landing, send_sem=send_sems.at[i], recv_sem=recv_sems.at[i],
                                         device_id=peer, device_id_type=pl.DeviceIdType.MESH).wait_recv()
        for cp in sends:
            cp.wait_send()
        for cp in here:
            cp.wait()

    any_spec = pl.BlockSpec(memory_space=pl.ANY)
    return pl.pallas_call(
        body, name=name, in_specs=[any_spec] * n_in, out_specs=tuple([any_spec] * n_out), out_shape=tuple(out_shapes),
        scratch_shapes=[pltpu.SemaphoreType.DMA((n_remote,)), pltpu.SemaphoreType.DMA((n_remote,)),
                        pltpu.SemaphoreType.DMA((max(n_local, 1),))],
        input_output_aliases={k: k for k in range(n_in)} if aliased else {},
    )(*ins)


def _gather_shards(stacks, *, name):
    n = len(stacks)
    halved = [a.shape[1] % 64 == 0 for a in stacks]
    unit_rows = [a.shape[1] // 2 if h else a.shape[1] for a, h in zip(stacks, halved)]
    part1_rows = [(r // 32) * 16 if r >= 32 else r for r in unit_rows]
    has_part2 = [p < r for p, r in zip(part1_rows, unit_rows)]
    n_sem = sum(2 + 1 + int(h2) + 3 * int(h) for h2, h in zip(has_part2, halved))

    def body(*refs):
        outs = refs[n:2 * n]
        send_sems, recv_sems = refs[2 * n:]
        x, y, c = lax.axis_index("x"), lax.axis_index("y"), lax.axis_index("c")
        mine, chip_x, chip_y, chip_d = 2 * x + y, 2 * (1 - x) + y, 2 * x + (1 - y), 2 * (1 - x) + (1 - y)
        to_x, to_y, sibling = (1 - x, y, c), (x, 1 - y, c), (x, y, 1 - c)

        def region(k, slot, half, part=None):
            start = half * unit_rows[k] if halved[k] else 0
            size = unit_rows[k]
            if part == 1:
                size = part1_rows[k]
            elif part == 2:
                start, size = start + part1_rows[k], unit_rows[k] - part1_rows[k]
            if not halved[k] and part is None:
                return outs[k].at[slot]
            if halved[k]:
                start = pl.multiple_of(start, 16)
            return outs[k].at[slot, pl.ds(start, size)]

        counter = [0]
        started, pending = [], []

        def send(region_of, peer, landing_of):
            i = counter[0]
            counter[0] += 1
            src = region_of
            cp = pltpu.make_async_remote_copy(src_ref=src, dst_ref=src, send_sem=send_sems.at[i], recv_sem=recv_sems.at[i],
                                              device_id=peer, device_id_type=pl.DeviceIdType.MESH)
            cp.start()
            started.append(cp)
            return pltpu.make_async_remote_copy(src_ref=landing_of, dst_ref=landing_of, send_sem=send_sems.at[i],
                                                recv_sem=recv_sems.at[i], device_id=peer, device_id_type=pl.DeviceIdType.MESH)

        from_x = [send(region(k, mine, c), to_x, region(k, chip_x, c)) for k in range(n)]
        from_y = [send(region(k, mine, c), to_y, region(k, chip_y, c)) for k in range(n)]
        diag = []
        for k in range(n):
            from_x[k].wait_recv()
            fwd = [send(region(k, chip_x, c, 1), to_y, region(k, chip_d, c, 1))]
            if halved[k]:
                pending.append(send(region(k, chip_x, c), sibling, region(k, chip_x, 1 - c)))
            from_y[k].wait_recv()
            if has_part2[k]:
                fwd.append(send(region(k, chip_y, c, 2), to_x, region(k, chip_d, c, 2)))
            if halved[k]:
                pending.append(send(region(k, chip_y, c), sibling, region(k, chip_y, 1 - c)))
            diag.append(fwd)
        for k in range(n):
            for landed in diag[k]:
                landed.wait_recv()
            if halved[k]:
                pending.append(send(region(k, chip_d, c), sibling, region(k, chip_d, 1 - c)))
        for landed in pending:
            landed.wait_recv()
        for cp in started:
            cp.wait_send()
        assert counter[0] == n_sem

    any_spec = pl.BlockSpec(memory_space=pl.ANY)
    return pl.pallas_call(
        body, name=name, in_specs=[any_spec] * n, out_specs=tuple([any_spec] * n),
        out_shape=tuple(jax.ShapeDtypeStruct(a.shape, a.dtype) for a in stacks),
        scratch_shapes=[pltpu.SemaphoreType.DMA((n_sem,)), pltpu.SemaphoreType.DMA((n_sem,))],
        input_output_aliases={k: k for k in range(n)},
    )(*stacks)


def _swap_sibling_halves(g4s, *, name):
    n = len(g4s)

    def plan(in_refs, out_refs, me):
        x, y, c = me
        remote = [(in_refs[k].at[:, 1 - c], out_refs[k], out_refs[k], (x, y, 1 - c)) for k in range(n)]
        return [], remote

    shapes = [jax.ShapeDtypeStruct((a.shape[0],) + a.shape[2:], a.dtype) for a in g4s]
    return _exchange(g4s, shapes, plan, 0, n, name=name)


def _scatter_to_chips(hs, *, name):
    n = len(hs)

    def plan(in_refs, out_refs, me):
        x, y, c = me
        mine = 2 * x + y
        remote = []
        for fx, fy in CHIP_FLIPS:
            px, py = x ^ fx, y ^ fy
            peer = 2 * px + py
            for k in range(n):
                remote.append((in_refs[k].at[peer], out_refs[k].at[mine], out_refs[k].at[peer], (px, py, c)))
        return [], remote

    shapes = [jax.ShapeDtypeStruct(a.shape, a.dtype) for a in hs]
    return _exchange(hs, shapes, plan, 0, len(CHIP_FLIPS) * n, name=name)


def _join_sibling_halves(rs, *, name):
    n = len(rs)

    def plan(in_refs, out_refs, me):
        x, y, c = me
        remote = [(out_refs[k].at[c], out_refs[k].at[c], out_refs[k].at[1 - c], (x, y, 1 - c)) for k in range(n)]
        return [], remote

    shapes = [jax.ShapeDtypeStruct(a.shape, a.dtype) for a in rs]
    return _exchange(rs, shapes, plan, 0, n, name=name, aliased=True)


def _gather_all_devices(packed, *, name):
    def plan(in_refs, out_refs, me):
        x, y, c = me
        mine = 4 * x + 2 * y + c
        remote = []
        for fx, fy, fc in DEVICE_FLIPS:
            px, py, pc = x ^ fx, y ^ fy, c ^ fc
            remote.append((in_refs[0], out_refs[0].at[mine], out_refs[0].at[4 * px + 2 * py + pc], (px, py, pc)))
        return [(in_refs[0], out_refs[0].at[mine])], remote

    shape = jax.ShapeDtypeStruct((2 * N_SHARD,) + packed.shape, packed.dtype)
    return _exchange([packed], [shape], plan, 1, len(DEVICE_FLIPS), name=name)[0]


WEIGHT_NAMES = ("w_ada", "b_ada", "g_ffn1", "w1_ffn1", "w3_ffn1", "w2_ffn1", "g_mix", "w_in", "conv_qkv", "a_log",
                "dt_bias", "g_onorm", "lam_re", "lam_im", "log_step", "b_re", "b_im", "c_re", "c_im", "d_skip", "w_glu",
                "b_glu", "w_proj_a", "w_proj_b", "w_out", "g_ffn2", "w1_ffn2", "w3_ffn2", "w2_ffn2", "g_final")
LARGE = tuple(n for n in SHARDED if n != "conv_qkv")
SMALL = tuple(n for n in WEIGHT_NAMES if n not in LARGE)
REDUCED_LARGE = tuple(n for n in LARGE if n != "w_ada")
REDUCED_SMALL = tuple(n for n in SMALL if n != "b_ada")
PACK_ROW = SUBLANES * LANES


def _pack(arrays):
    flat = jnp.concatenate([a.reshape(-1) for a in arrays])
    n_pad = -flat.shape[0] % PACK_ROW
    return jnp.pad(flat, (0, n_pad)).reshape(-1, LANES)


def _unpack(packed, shapes):
    flat = packed.reshape(-1)
    out, start = [], 0
    for s in shapes:
        size = math.prod(s)
        out.append(flat[start:start + size].reshape(s))
        start += size
    return out


def _unpack_slots(gathered, shapes):
    flat = gathered.reshape(gathered.shape[0], -1)
    out, start = [], 0
    for s in shapes:
        size = math.prod(s)
        out.append(flat[:, start:start + size].reshape((gathered.shape[0],) + tuple(s)))
        start += size
    return out


TRANSPOSED = ("w1_ffn1", "w3_ffn1", "w1_ffn2", "w3_ffn2", "w_in")


def _to_internal(name, a):
    return jnp.swapaxes(a[0], 0, 1) if name in TRANSPOSED else a[0]


def _from_internal(name, a):
    return (jnp.swapaxes(a, 0, 1) if name in TRANSPOSED else a)[None]


def _step(x, c, target, weights, m_in, v_in):
    xi, yi, ci = lax.axis_index("x"), lax.axis_index("y"), lax.axis_index("c")
    my_chip = 2 * xi + yi

    others = [k + (k >= my_chip).astype(jnp.int32) for k in range(N_SHARD - 1)]
    place = jnp.stack([ci, my_chip] + others).astype(jnp.int32)

    slots = [_cast_into_slot(_to_internal(n, weights[n]), place, F32 if n == "conv_qkv" else MXU_DTYPE, name=f"cast_{n}")
             for n in SHARDED]
    stacks = dict(zip(SHARDED, _gather_shards(slots, name="gather_weights")))
    rep = {n: weights[n] for n in WEIGHT_NAMES if n not in SHARDED}
    loss, grad_x, g, dmod = _local_step(x, c, target, _gathered_weights(stacks, rep))
    g_stacks, g_small = _grads_to_problem_layout(g)

    g4s = [g_stacks[n].reshape(N_SHARD, 2, g_stacks[n].shape[1] // 2, g_stacks[n].shape[2]) for n in REDUCED_LARGE]
    from_sibling = _swap_sibling_halves(g4s, name="swap_sibling_halves")
    chip_sums = [_add_sibling_half(a, r, place, name=f"chip_sum_{n}") for n, a, r in zip(REDUCED_LARGE, g4s, from_sibling)]
    from_chips = _scatter_to_chips(chip_sums, name="scatter_to_chips")
    reduced = [_sum_chips(h, p, place, name=f"sum_chips_{n}") for n, h, p in zip(REDUCED_LARGE, chip_sums, from_chips)]
    joined = _join_sibling_halves(reduced, name="join_sibling_halves")
    grads_2d = {n: j.reshape(2 * j.shape[1], j.shape[2]) for n, j in zip(REDUCED_LARGE, joined)}
    grads = {n: _from_internal(n, a) for n, a in grads_2d.items()}

    summed_shapes = [g_small[n].shape for n in REDUCED_SMALL] + [(1, 1)]
    packed = _pack([g_small[n] for n in REDUCED_SMALL] + [loss, c, dmod])
    gathered = _gather_all_devices(packed, name="gather_small")
    *small_grads, loss_sum = _unpack(_sum_slots(gathered, name="sum_small"), summed_shapes)
    grads.update(zip(REDUCED_SMALL, small_grads))
    n_conv = weights["conv_qkv"].shape[-1]
    grads["conv_qkv"] = lax.dynamic_slice_in_dim(grads["conv_qkv"], my_chip * n_conv, n_conv, axis=2)
    n_dev = gathered.shape[0]
    rows_of = lambda t: t.reshape(n_dev * t.shape[1], t.shape[2])
    _, c_all, dmod_all = _unpack_slots(gathered, [(sum(math.prod(s) for s in summed_shapes),), c.shape, dmod.shape])
    dw_ada, grads["b_ada"] = _ada_grads(rows_of(c_all), rows_of(dmod_all))
    grads_2d["w_ada"] = lax.dynamic_index_in_dim(dw_ada, my_chip, axis=0, keepdims=False)
    grads["w_ada"] = grads_2d["w_ada"][None]

    delta, new_m, new_v = {}, {}, {}
    grads_2d["conv_qkv"] = grads["conv_qkv"][0]
    for n in LARGE + ("conv_qkv",):
        outs = _adamw(_to_internal(n, weights[n]), grads_2d[n], _to_internal(n, m_in[n]), _to_internal(n, v_in[n]),
                      name=f"adamw_{n}")
        delta[n], new_m[n], new_v[n] = [_from_internal(n, o) for o in outs]
    packed_names = tuple(n for n in SMALL if n != "conv_qkv")
    shapes = [weights[n].shape for n in packed_names]
    outs = _adamw(*[_pack([d[n] for n in packed_names]) for d in (weights, grads, m_in, v_in)], name="adamw_small")
    for d, o in zip((delta, new_m, new_v), outs):
        d.update(zip(packed_names, _unpack(o, shapes)))
    return (loss_sum.reshape(()), grad_x, *[grads[n] for n in WEIGHT_NAMES], *[delta[n] for n in WEIGHT_NAMES],
            *[new_m[n] for n in WEIGHT_NAMES], *[new_v[n] for n in WEIGHT_NAMES])


def kernel(x, c, w_ada, b_ada, g_ffn1, w1_ffn1, w3_ffn1, w2_ffn1, g_mix, w_in, conv_qkv, a_log, dt_bias, g_onorm, lam_re, lam_im, log_step, b_re, b_im, c_re, c_im, d_skip, w_glu, b_glu, w_proj_a, w_proj_b, w_out, g_ffn2, w1_ffn2, w3_ffn2, w2_ffn2, g_final, loss_target, m_w_ada, m_b_ada, m_g_ffn1, m_w1_ffn1, m_w3_ffn1, m_w2_ffn1, m_g_mix, m_w_in, m_conv_qkv, m_a_log, m_dt_bias, m_g_onorm, m_lam_re, m_lam_im, m_log_step, m_b_re, m_b_im, m_c_re, m_c_im, m_d_skip, m_w_glu, m_b_glu, m_w_proj_a, m_w_proj_b, m_w_out, m_g_ffn2, m_w1_ffn2, m_w3_ffn2, m_w2_ffn2, m_g_final, v_w_ada, v_b_ada, v_g_ffn1, v_w1_ffn1, v_w3_ffn1, v_w2_ffn1, v_g_mix, v_w_in, v_conv_qkv, v_a_log, v_dt_bias, v_g_onorm, v_lam_re, v_lam_im, v_log_step, v_b_re, v_b_im, v_c_re, v_c_im, v_d_skip, v_w_glu, v_b_glu, v_w_proj_a, v_w_proj_b, v_w_out, v_g_ffn2, v_w1_ffn2, v_w3_ffn2, v_w2_ffn2, v_g_final):
    w_vals = (w_ada, b_ada, g_ffn1, w1_ffn1, w3_ffn1, w2_ffn1, g_mix, w_in, conv_qkv, a_log, dt_bias, g_onorm, lam_re, lam_im, log_step, b_re, b_im, c_re, c_im, d_skip, w_glu, b_glu, w_proj_a, w_proj_b, w_out, g_ffn2, w1_ffn2, w3_ffn2, w2_ffn2, g_final)
    m_vals = (m_w_ada, m_b_ada, m_g_ffn1, m_w1_ffn1, m_w3_ffn1, m_w2_ffn1, m_g_mix, m_w_in, m_conv_qkv, m_a_log, m_dt_bias, m_g_onorm, m_lam_re, m_lam_im, m_log_step, m_b_re, m_b_im, m_c_re, m_c_im, m_d_skip, m_w_glu, m_b_glu, m_w_proj_a, m_w_proj_b, m_w_out, m_g_ffn2, m_w1_ffn2, m_w3_ffn2, m_w2_ffn2, m_g_final)
    v_vals = (v_w_ada, v_b_ada, v_g_ffn1, v_w1_ffn1, v_w3_ffn1, v_w2_ffn1, v_g_mix, v_w_in, v_conv_qkv, v_a_log, v_dt_bias, v_g_onorm, v_lam_re, v_lam_im, v_log_step, v_b_re, v_b_im, v_c_re, v_c_im, v_d_skip, v_w_glu, v_b_glu, v_w_proj_a, v_w_proj_b, v_w_out, v_g_ffn2, v_w1_ffn2, v_w3_ffn2, v_w2_ffn2, v_g_final)
    return _step(x, c, loss_target, dict(zip(WEIGHT_NAMES, w_vals)), dict(zip(WEIGHT_NAMES, m_vals)),
                 dict(zip(WEIGHT_NAMES, v_vals)))
```

```python
import functools
import math

import jax
import jax.numpy as jnp
from jax import lax
from jax.experimental import pallas as pl
from jax.experimental.pallas import tpu as pltpu

F32 = jnp.float32
BF16 = jnp.bfloat16
MXU_DTYPE = BF16

D_MODEL = 1024
D_FF = 2816
DN_HEADS = 8
DN_HEAD_DIM = 64
DN_WIDTH = DN_HEADS * DN_HEAD_DIM
CONV_WIDTH = 4
CHUNK = 64
S5_GROUP_CH = 16
S5_GROUPS = 32
S5_WIDTH = S5_GROUPS * S5_GROUP_CH
S5_STATE = 64
S5_LANES = S5_GROUPS * S5_STATE
N_MOD = 9
EPS = 1e-6
N_SHARD = 4
FF_SHARD = D_FF // N_SHARD
BA_PAD = 128

ADAM_LR = 0.001
ADAM_B1 = 0.9
ADAM_B2 = 0.999
ADAM_EPS = 1e-08
ADAM_WD = 0.01
ADAM_STEP = 10

VMEM_BYTES_V7X = 64 * 1024 * 1024
SUBLANES = 8
LANES = 128


def _params(block_bytes, extra_bytes=0):
    need = 2 * block_bytes + extra_bytes + (4 << 20)
    return pltpu.CompilerParams(vmem_limit_bytes=int(min(max(need, 16 << 20), VMEM_BYTES_V7X - (8 << 20))))


def _nbytes(shape, dtype):
    return math.prod(shape) * jnp.dtype(dtype).itemsize


def _hbm(*args):
    return [pltpu.with_memory_space_constraint(a, pltpu.HBM) if jnp.issubdtype(a.dtype, jnp.floating) else a
            for a in args]


_NN = (((1,), (0,)), ((), ()))
_NT = (((1,), (1,)), ((), ()))
_TN = (((0,), (0,)), ((), ()))


LHS_ROW_BYTES = 4096


def _mm_act(pairs, mode, *, name, reduce_shards=False, out_dtype=F32, tm=None):
    n_tok = pairs[0][0].shape[1]
    n_out = pairs[0][1].shape[2] if mode == "nn" else pairs[0][1].shape[1]
    if tm is None:
        row_bytes = sum(a.shape[2] * jnp.dtype(a.dtype).itemsize for a, _ in pairs)
        tm = 1024 if row_bytes <= LHS_ROW_BYTES else 512
    tm = min(tm, n_tok)
    tn = n_out if n_out <= 1536 else 1024
    assert n_tok % tm == 0 and n_out % tn == 0
    n_red = N_SHARD if reduce_shards else 1
    grid = (n_tok // tm, n_out // tn, n_red)
    dims = _NN if mode == "nn" else _NT
    shard_of = lambda n_sh: (lambda r: 0) if n_sh == 1 else (lambda r: r)

    in_specs, args, blk = [], [], 0
    for a, b in pairs:
        k_dim = a.shape[2]
        sa, sb = shard_of(a.shape[0]), shard_of(b.shape[0])
        in_specs.append(pl.BlockSpec((1, tm, k_dim), lambda i, j, r, sa=sa: (sa(r), i, 0)))
        if mode == "nn":
            assert b.shape[1] == k_dim
            in_specs.append(pl.BlockSpec((1, k_dim, tn), lambda i, j, r, sb=sb: (sb(r), 0, j)))
        else:
            assert b.shape[2] == k_dim
            in_specs.append(pl.BlockSpec((1, tn, k_dim), lambda i, j, r, sb=sb: (sb(r), j, 0)))
        args += [a, b]
        blk += _nbytes((tm, k_dim), a.dtype) + _nbytes((k_dim, tn), b.dtype)
    blk += _nbytes((tm, tn), out_dtype)
    n_pairs = len(pairs)

    def body(*refs):
        out_ref = refs[2 * n_pairs]
        acc = None
        for k in range(n_pairs):
            a = refs[2 * k][0].astype(MXU_DTYPE)
            b = refs[2 * k + 1][0].astype(MXU_DTYPE)
            d = lax.dot_general(a, b, dims, preferred_element_type=F32)
            acc = d if acc is None else acc + d

        if n_red == 1:
            out_ref[0] = acc.astype(out_dtype)
        else:
            acc_ref = refs[-1]
            r = pl.program_id(2)

            @pl.when(r == 0)
            def _():
                acc_ref[...] = acc

            @pl.when(r > 0)
            def _():
                acc_ref[...] += acc

            @pl.when(r == n_red - 1)
            def _():
                out_ref[0] = acc_ref[...].astype(out_dtype)

    return pl.pallas_call(
        body,
        name=name,
        grid=grid,
        in_specs=in_specs,
        out_specs=pl.BlockSpec((1, tm, tn), lambda i, j, r: (0, i, j)),
        out_shape=jax.ShapeDtypeStruct((1, n_tok, n_out), out_dtype),
        scratch_shapes=[pltpu.VMEM((tm, tn), F32)] if n_red > 1 else [],
        compiler_params=_params(blk, 3 * _nbytes((tm, tn), F32)),
    )(*_hbm(*args))


def _mm_tn(a, b, *, name, tt=1024):
    n_tok, k_dim = a.shape[1], a.shape[2]
    n_out = b.shape[2]
    tt = min(tt, n_tok)
    tk = k_dim if k_dim <= 1536 else 1024
    tn = n_out if n_out <= 1536 else 1024
    assert n_tok % tt == 0 and k_dim % tk == 0 and n_out % tn == 0
    n_so = max(a.shape[0], b.shape[0])
    sa = (lambda s: s) if a.shape[0] > 1 else (lambda s: 0)
    sb = (lambda s: s) if b.shape[0] > 1 else (lambda s: 0)
    grid = (n_so, k_dim // tk, n_out // tn, n_tok // tt)

    def body(a_ref, b_ref, out_ref):
        d = lax.dot_general(a_ref[0].astype(MXU_DTYPE), b_ref[0].astype(MXU_DTYPE), _TN, preferred_element_type=F32)
        t = pl.program_id(3)

        @pl.when(t == 0)
        def _():
            out_ref[0] = d

        @pl.when(t > 0)
        def _():
            out_ref[0] += d

    blk = _nbytes((tt, tk), a.dtype) + _nbytes((tt, tn), b.dtype) + _nbytes((tk, tn), F32)
    return pl.pallas_call(
        body,
        name=name,
        grid=grid,
        in_specs=[
            pl.BlockSpec((1, tt, tk), lambda s, ki, nj, t: (sa(s), t, ki)),
            pl.BlockSpec((1, tt, tn), lambda s, ki, nj, t: (sb(s), t, nj)),
        ],
        out_specs=pl.BlockSpec((1, tk, tn), lambda s, ki, nj, t: (s, ki, nj)),
        out_shape=jax.ShapeDtypeStruct((n_so, k_dim, n_out), F32),
        compiler_params=_params(blk, 2 * _nbytes((tk, tn), F32) + _nbytes((tt, tk), F32)),
    )(*_hbm(a, b))


@functools.partial(jax.custom_vjp, nondiff_argnums=(2,))
def _mdot(a, b, dims):
    return lax.dot_general(a.astype(MXU_DTYPE), b.astype(MXU_DTYPE), dims, preferred_element_type=F32)


def _mdot_fwd(a, b, dims):
    return _mdot(a, b, dims), (a, b)


def _mdot_bwd(dims, res, g):
    a, b = res
    (ca, cb), (ba, bb) = dims
    nb = len(ba)
    assert tuple(ba) == tuple(range(nb)) and tuple(bb) == tuple(range(nb)) and len(ca) == 1 and a.ndim == nb + 2
    batch = (tuple(range(nb)), tuple(range(nb)))
    ra, rb = nb, nb + 1
    a_free = (set(range(nb, nb + 2)) - set(ca)).pop()
    b_free = (set(range(nb, nb + 2)) - set(cb)).pop()
    if a_free < ca[0]:
        da = _mdot(g, b, (((rb,), (b_free,)), batch))
    else:
        da = _mdot(b, g, (((b_free,), (rb,)), batch))
    if b_free > cb[0]:
        db = _mdot(a, g, (((a_free,), (ra,)), batch))
    else:
        db = _mdot(g, a, (((ra,), (a_free,)), batch))
    return da.astype(a.dtype), db.astype(b.dtype)


_mdot.defvjp(_mdot_fwd, _mdot_bwd)


def _rms(x, gain):
    return x * lax.rsqrt(jnp.mean(x * x, axis=-1, keepdims=True) + EPS) * gain


def _pre_fn(coef, x_in, f, gate, gain, shift, scale):
    x_new = x_in if f is None else x_in + coef * gate * f
    return x_new, _rms(x_new, gain) * (1.0 + scale) + shift


def _row_spec(ts):
    return pl.BlockSpec((1, ts, D_MODEL), lambda b, j: (b, j, 0))


_BATCH_VEC = pl.BlockSpec((1, 1, D_MODEL), lambda b, j: (b, 0, 0))
_ONE_VEC = pl.BlockSpec((1, D_MODEL), lambda b, j: (0, 0))


def _pre(x_in, f, gate, gain, shift, scale, coef, *, name, ts=512):
    n_b, n_s, _ = x_in.shape
    ts = min(ts, n_s)
    has_res = f is not None

    def body(*refs):
        if has_res:
            x_ref, f_ref, gate_ref, gain_ref, sh_ref, sc_ref, xn_ref, a_ref = refs
            x_new, a = _pre_fn(coef, x_ref[0], f_ref[0], gate_ref[0], gain_ref[...], sh_ref[0], sc_ref[0])
            xn_ref[0] = x_new
        else:
            x_ref, gain_ref, sh_ref, sc_ref, a_ref = refs
            _, a = _pre_fn(coef, x_ref[0], None, None, gain_ref[...], sh_ref[0], sc_ref[0])
        a_ref[0] = a.astype(a_ref.dtype)

    row = _row_spec(ts)
    if has_res:
        args = (x_in, f, gate, gain, shift, scale)
        in_specs = [row, row, _BATCH_VEC, _ONE_VEC, _BATCH_VEC, _BATCH_VEC]
        out_specs = (row, row)
        out_shape = (jax.ShapeDtypeStruct(x_in.shape, F32), jax.ShapeDtypeStruct(x_in.shape, MXU_DTYPE))
    else:
        args = (x_in, gain, shift, scale)
        in_specs = [row, _ONE_VEC, _BATCH_VEC, _BATCH_VEC]
        out_specs = row
        out_shape = jax.ShapeDtypeStruct(x_in.shape, MXU_DTYPE)
    return pl.pallas_call(
        body, name=name, grid=(n_b, n_s // ts), in_specs=in_specs, out_specs=out_specs, out_shape=out_shape,
        compiler_params=_params(5 * _nbytes((ts, D_MODEL), F32), 4 * _nbytes((ts, D_MODEL), F32)),
    )(*_hbm(*args))


def _accumulate(ref, value, first):
    @pl.when(first)
    def _():
        ref[...] = value

    @pl.when(jnp.logical_not(first))
    def _():
        ref[...] += value


def _pre_bwd(x_in, f, gate, gain, shift, scale, coef, da, dx_up, *, name, ts=512):
    n_b, n_s, _ = x_in.shape
    ts = min(ts, n_s)
    has_res = f is not None
    has_up = dx_up is not None

    def body(*refs):
        refs = list(refs)
        x_ref = refs.pop(0)
        f_ref, gate_ref = (refs.pop(0), refs.pop(0)) if has_res else (None, None)
        gain_ref, sh_ref, sc_ref, da_ref = refs.pop(0), refs.pop(0), refs.pop(0), refs.pop(0)
        up_ref = refs.pop(0) if has_up else None
        dx_ref = refs.pop(0)
        df_ref, dgate_ref = (refs.pop(0), refs.pop(0)) if has_res else (None, None)
        dgain_ref, dsh_ref, dsc_ref = refs
        b, j = pl.program_id(0), pl.program_id(1)
        da_v = da_ref[0].astype(F32)
        up_v = up_ref[0] if has_up else jnp.zeros((ts, D_MODEL), F32)
        if has_res:
            fn = functools.partial(_pre_fn, coef)
            _, pull = jax.vjp(fn, x_ref[0], f_ref[0], gate_ref[0], gain_ref[...], sh_ref[0], sc_ref[0])
            dx, df, dgate, dgain, dsh, dsc = pull((up_v, da_v))
            df_ref[0] = df.astype(df_ref.dtype)
            _accumulate(dgate_ref, dgate[None], j == 0)
        else:
            fn = lambda x, g, sh, sc: _pre_fn(coef, x, None, None, g, sh, sc)
            _, pull = jax.vjp(fn, x_ref[0], gain_ref[...], sh_ref[0], sc_ref[0])
            dx, dgain, dsh, dsc = pull((up_v, da_v))
        dx_ref[0] = dx
        _accumulate(dgain_ref, dgain, jnp.logical_and(b == 0, j == 0))
        _accumulate(dsh_ref, dsh[None], j == 0)
        _accumulate(dsc_ref, dsc[None], j == 0)

    row = _row_spec(ts)
    args, in_specs = [x_in], [row]
    if has_res:
        args += [f, gate]
        in_specs += [row, _BATCH_VEC]
    args += [gain, shift, scale, da]
    in_specs += [_ONE_VEC, _BATCH_VEC, _BATCH_VEC, row]
    if has_up:
        args.append(dx_up)
        in_specs.append(row)
    vec = jax.ShapeDtypeStruct((n_b, 1, D_MODEL), F32)
    out_shape, out_specs = [jax.ShapeDtypeStruct(x_in.shape, F32)], [row]
    if has_res:
        out_shape += [jax.ShapeDtypeStruct(x_in.shape, MXU_DTYPE), vec]
        out_specs += [row, _BATCH_VEC]
    out_shape += [jax.ShapeDtypeStruct((1, D_MODEL), F32), vec, vec]
    out_specs += [_ONE_VEC, _BATCH_VEC, _BATCH_VEC]
    return pl.pallas_call(
        body, name=name, grid=(n_b, n_s // ts), in_specs=in_specs, out_specs=tuple(out_specs), out_shape=tuple(out_shape),
        compiler_params=_params(6 * _nbytes((ts, D_MODEL), F32), 8 * _nbytes((ts, D_MODEL), F32)),
    )(*_hbm(*args))


def _final_fn(x_in, f, gate, gain, target):
    x_new = x_in + 0.5 * gate * f
    err = jnp.square(_rms(x_new, gain) - target)
    return 0.5 * jnp.sum(jnp.mean(err, axis=-1))


def _final(x_in, f, gate, gain, target, *, name, ts=512):
    n_b, n_s, _ = x_in.shape
    ts = min(ts, n_s)

    def body(x_ref, f_ref, gate_ref, gain_ref, t_ref, loss_ref, dx_ref, df_ref, dgate_ref, dgain_ref):
        b, j = pl.program_id(0), pl.program_id(1)
        loss, (dx, df, dgate, dgain) = jax.value_and_grad(_final_fn, argnums=(0, 1, 2, 3))(
            x_ref[0], f_ref[0], gate_ref[0], gain_ref[...], t_ref[0])
        first = jnp.logical_and(b == 0, j == 0)
        _accumulate(loss_ref, jnp.reshape(loss, (1, 1)), first)
        dx_ref[0] = dx
        df_ref[0] = df.astype(df_ref.dtype)
        _accumulate(dgate_ref, dgate[None], j == 0)
        _accumulate(dgain_ref, dgain, first)

    row = _row_spec(ts)
    return pl.pallas_call(
        body, name=name, grid=(n_b, n_s // ts),
        in_specs=[row, row, _BATCH_VEC, _ONE_VEC, row],
        out_specs=(pl.BlockSpec((1, 1), lambda b, j: (0, 0)), row, row, _BATCH_VEC, _ONE_VEC),
        out_shape=(jax.ShapeDtypeStruct((1, 1), F32), jax.ShapeDtypeStruct(x_in.shape, F32),
                   jax.ShapeDtypeStruct(x_in.shape, MXU_DTYPE), jax.ShapeDtypeStruct((n_b, 1, D_MODEL), F32),
                   jax.ShapeDtypeStruct((1, D_MODEL), F32)),
        compiler_params=_params(5 * _nbytes((ts, D_MODEL), F32), 8 * _nbytes((ts, D_MODEL), F32)),
    )(*_hbm(x_in, f, gate, gain, target))


FFN_TOKENS = 1024


def _ffn_up(a, w1s, w3s, *, name, tm=FFN_TOKENS):
    n_tok = a.shape[0]
    tm = min(tm, n_tok)

    def body(a_ref, w1_ref, w3_ref, h1_ref, h3_ref, g_ref):
        av = a_ref[...].astype(MXU_DTYPE)
        h1 = lax.dot_general(av, w1_ref[0].astype(MXU_DTYPE), _NT, preferred_element_type=F32)
        h3 = lax.dot_general(av, w3_ref[0].astype(MXU_DTYPE), _NT, preferred_element_type=F32)
        h1_ref[0] = h1.astype(h1_ref.dtype)
        h3_ref[0] = h3.astype(h3_ref.dtype)
        g_ref[0] = (jax.nn.silu(h1) * h3).astype(g_ref.dtype)

    w_spec = pl.BlockSpec((1, FF_SHARD, D_MODEL), lambda s, i: (s, 0, 0))
    h_spec = pl.BlockSpec((1, tm, FF_SHARD), lambda s, i: (s, i, 0))
    h_shape = jax.ShapeDtypeStruct((N_SHARD, n_tok, FF_SHARD), MXU_DTYPE)
    blk = _nbytes((tm, D_MODEL), a.dtype) + 2 * _nbytes((D_MODEL, FF_SHARD), w1s.dtype) + 3 * _nbytes((tm, FF_SHARD), MXU_DTYPE)
    return pl.pallas_call(
        body, name=name, grid=(N_SHARD, n_tok // tm),
        in_specs=[pl.BlockSpec((tm, D_MODEL), lambda s, i: (i, 0)), w_spec, w_spec],
        out_specs=(h_spec, h_spec, h_spec), out_shape=(h_shape, h_shape, h_shape),
        compiler_params=_params(blk, 6 * _nbytes((tm, FF_SHARD), F32)),
    )(*_hbm(a, w1s, w3s))


def _ffn_down_bwd(df, w2s, h1, h3, *, name, tm=FFN_TOKENS):
    n_tok = df.shape[0]
    tm = min(tm, n_tok)

    def body(df_ref, w2_ref, h1_ref, h3_ref, dh1_ref, dh3_ref):
        dg = lax.dot_general(df_ref[...].astype(MXU_DTYPE), w2_ref[0].astype(MXU_DTYPE), _NT, preferred_element_type=F32)
        h1v = h1_ref[0].astype(F32)
        h3v = h3_ref[0].astype(F32)
        sig = jax.nn.sigmoid(h1v)
        dh3_ref[0] = (dg * (h1v * sig)).astype(dh3_ref.dtype)
        dh1_ref[0] = (dg * h3v * (sig * (1.0 + h1v * (1.0 - sig)))).astype(dh1_ref.dtype)

    h_spec = pl.BlockSpec((1, tm, FF_SHARD), lambda s, i: (s, i, 0))
    h_shape = jax.ShapeDtypeStruct((N_SHARD, n_tok, FF_SHARD), MXU_DTYPE)
    blk = _nbytes((tm, D_MODEL), df.dtype) + _nbytes((FF_SHARD, D_MODEL), w2s.dtype) + 4 * _nbytes((tm, FF_SHARD), MXU_DTYPE)
    return pl.pallas_call(
        body, name=name, grid=(N_SHARD, n_tok // tm),
        in_specs=[pl.BlockSpec((tm, D_MODEL), lambda s, i: (i, 0)),
                  pl.BlockSpec((1, FF_SHARD, D_MODEL), lambda s, i: (s, 0, 0)), h_spec, h_spec],
        out_specs=(h_spec, h_spec), out_shape=(h_shape, h_shape),
        compiler_params=_params(blk, 8 * _nbytes((tm, FF_SHARD), F32)),
    )(*_hbm(df, w2s, h1, h3))


def _ffn_fwd(a, w1s, w3s, w2s, tag):
    h1, h3, g = _ffn_up(a, w1s, w3s, name=f"{tag}_up")
    f = _mm_act([(g, w2s)], "nn", reduce_shards=True, tm=FFN_TOKENS, name=f"{tag}_down")[0]
    return f, (h1, h3, g)


def _ffn_bwd(a, w1s, w3s, w2s, saved, df, tag):
    h1, h3, g = saved
    dh1, dh3 = _ffn_down_bwd(df, w2s, h1, h3, name=f"{tag}_down_bwd")
    da = _mm_act([(dh1, w1s), (dh3, w3s)], "nn", reduce_shards=True, tm=FFN_TOKENS, name=f"{tag}_up_bwd")[0]
    a3 = a[None]
    dw1 = _mm_tn(dh1, a3, tt=FFN_TOKENS, name=f"{tag}_dw1")
    dw3 = _mm_tn(dh3, a3, tt=FFN_TOKENS, name=f"{tag}_dw3")
    dw2 = _mm_tn(g, df[None], tt=FFN_TOKENS, name=f"{tag}_dw2")
    return da, dw1, dw3, dw2


CONV_LANES = 256


def _shift_down(x, d):
    if d == 0:
        return x
    row = lax.broadcasted_iota(jnp.int32, x.shape, 0)
    return jnp.where(row >= d, pltpu.roll(x, d, 0), 0.0)


def _shift_up(x, d):
    if d == 0:
        return x
    n = x.shape[0]
    row = lax.broadcasted_iota(jnp.int32, x.shape, 0)
    return jnp.where(row < n - d, pltpu.roll(x, n - d, 0), 0.0)


def _conv_pre(x, w):
    acc = None
    for j in range(CONV_WIDTH):
        term = w[j:j + 1, :] * _shift_down(x, CONV_WIDTH - 1 - j)
        acc = term if acc is None else acc + term
    return acc


def _conv_fwd(x, w, *, name):
    n_b, n_s, n_c = x.shape
    spec = pl.BlockSpec((1, n_s, CONV_LANES), lambda b, cj: (b, 0, cj))

    def body(x_ref, w_ref, o_ref):
        o_ref[0] = jax.nn.silu(_conv_pre(x_ref[0], w_ref[...]))

    return pl.pallas_call(
        body, name=name, grid=(n_b, n_c // CONV_LANES),
        in_specs=[spec, pl.BlockSpec((CONV_WIDTH, CONV_LANES), lambda b, cj: (0, cj))],
        out_specs=spec, out_shape=jax.ShapeDtypeStruct(x.shape, F32),
        compiler_params=_params(2 * _nbytes((n_s, CONV_LANES), F32), 6 * _nbytes((n_s, CONV_LANES), F32)),
    )(*_hbm(x, w))


def _conv_bwd(x, w, dout, *, name):
    n_b, n_s, n_c = x.shape
    per_part = DN_WIDTH // CONV_LANES
    spec = pl.BlockSpec((1, n_s, CONV_LANES), lambda cj, b: (b, 0, cj))
    do_spec = pl.BlockSpec((1, 1, n_s, CONV_LANES), lambda cj, b: (cj // per_part, b, 0, cj % per_part))
    w_spec = pl.BlockSpec((CONV_WIDTH, CONV_LANES), lambda cj, b: (0, cj))

    def body(x_ref, w_ref, do_ref, dx_ref, dw_ref):
        xv, wv = x_ref[0], w_ref[...]
        pre = _conv_pre(xv, wv)
        sig = jax.nn.sigmoid(pre)
        dpre = do_ref[0, 0] * (sig * (1.0 + pre * (1.0 - sig)))
        dx = None
        first = pl.program_id(1) == 0
        for j in range(CONV_WIDTH):
            d = CONV_WIDTH - 1 - j
            ahead = _shift_up(dpre, d)
            term = wv[j:j + 1, :] * ahead
            dx = term if dx is None else dx + term
            dwj = jnp.sum(ahead * xv, axis=0, keepdims=True)
            _accumulate(dw_ref.at[j:j + 1, :], dwj, first)
        dx_ref[0] = dx.astype(dx_ref.dtype)

    return pl.pallas_call(
        body, name=name, grid=(n_c // CONV_LANES, n_b),
        in_specs=[spec, w_spec, do_spec], out_specs=(spec, w_spec),
        out_shape=(jax.ShapeDtypeStruct(x.shape, MXU_DTYPE), jax.ShapeDtypeStruct((CONV_WIDTH, n_c), F32)),
        compiler_params=_params(3 * _nbytes((n_s, CONV_LANES), F32), 8 * _nbytes((n_s, CONV_LANES), F32)),
    )(*_hbm(x, w, dout))


_BNT = (((2,), (2,)), ((0,), (0,)))
_BNN = (((2,), (1,)), ((0,), (0,)))
_BTN = (((1,), (1,)), ((0,), (0,)))
DN_PREP_CHUNKS = 8
DN_SCAN_HEADS = 4
N_DOUBLINGS = 5


def _fdot(a, b, dims):
    return lax.dot_general(a, b, dims, precision=lax.Precision.HIGHEST, preferred_element_type=F32)


def _hdot(a, b, dims):
    return lax.dot_general(a, b, dims, precision=lax.Precision.HIGH, preferred_element_type=F32)


def _solve_by_doubling(a, rhs_u, rhs_w):
    row = lax.broadcasted_iota(jnp.int32, (CHUNK, CHUNK), 0)
    col = lax.broadcasted_iota(jnp.int32, (CHUNK, CHUNK), 1)
    inv = jnp.where(row == col, 1.0, 0.0) - a
    power = a
    for _ in range(N_DOUBLINGS):
        power = _hdot(power, power, _BNN)
        inv = inv + _hdot(inv, power, _BNN)
    return _hdot(inv, rhs_u, _BNN), _hdot(inv, rhs_w, _BNN), inv


@jax.custom_vjp
def _solve_saved(a, rhs_u, rhs_w, inv, u, w):
    return u, w


def _solve_saved_fwd(a, rhs_u, rhs_w, inv, u, w):
    return (u, w), (inv, u, w)


def _solve_saved_bwd(res, cts):
    inv, u, w = res
    gu = _hdot(inv, cts[0], _BTN)
    gw = _hdot(inv, cts[1], _BTN)
    da = -(_hdot(gu, u, _BNT) + _hdot(gw, w, _BNT))
    return da, gu, gw, jnp.zeros_like(inv), jnp.zeros_like(u), jnp.zeros_like(w)


_solve_saved.defvjp(_solve_saved_fwd, _solve_saved_bwd)


def _dn_prep_fn(solve, qc, kc, vc, bl, lac, lar, a_log, dt_bias):
    q = qc * lax.rsqrt(jnp.sum(qc * qc, axis=-1, keepdims=True) + EPS) * (DN_HEAD_DIM ** -0.5)
    k = kc * lax.rsqrt(jnp.sum(kc * kc, axis=-1, keepdims=True) + EPS)
    beta = jax.nn.sigmoid(bl)
    neg_a = -jnp.exp(a_log)
    lgc = neg_a * jax.nn.softplus(lac + dt_bias)
    lgr = neg_a * jax.nn.softplus(lar + dt_bias)
    row = lax.broadcasted_iota(jnp.int32, (CHUNK, CHUNK), 0)
    col = lax.broadcasted_iota(jnp.int32, (CHUNK, CHUNK), 1)
    causal, strict = row >= col, row > col
    g_c = jnp.sum(jnp.where(causal, lgr, 0.0), axis=-1, keepdims=True)
    g_r = jnp.sum(jnp.where(row <= col, lgc, 0.0), axis=-2, keepdims=True)
    decay = jnp.exp(jnp.where(causal, g_c - g_r, -jnp.inf))
    kb = k * beta
    a = jnp.where(strict, _mdot(kb, k, _BNT) * decay, 0.0)
    u, w, extra = solve(a, vc * beta, kb * jnp.exp(g_c))
    attn = _mdot(q, k, _BNT) * decay
    g_last = jnp.sum(lgc, axis=-2, keepdims=True)
    return q * jnp.exp(g_c), k * jnp.exp(g_last - g_c), u, w, attn, g_last, extra


PAIR = 2
PAIR_LANES = PAIR * DN_HEAD_DIM


def _dn_prep_specs(n_cb):
    tok = n_cb * CHUNK
    wide = pl.BlockSpec((1, PAIR, tok, DN_HEAD_DIM), lambda p, b, j: (b, p, j, 0))
    rowv = pl.BlockSpec((1, PAIR, n_cb, 1, CHUNK), lambda p, b, j: (b, p, j, 0, 0))
    one = pl.BlockSpec((1, PAIR, n_cb, 1, 1), lambda p, b, j: (b, p, j, 0, 0))
    head = pl.BlockSpec((PAIR, 1, 1), lambda p, b, j: (p, 0, 0))
    lanes = lambda part: pl.BlockSpec((1, tok, PAIR_LANES), lambda p, b, j: (b, j, part * (DN_HEADS // PAIR) + p))
    return wide, rowv, one, head, lanes


def _split_pair(x, n_cb):
    halves = [x[:, h * DN_HEAD_DIM:(h + 1) * DN_HEAD_DIM].reshape(n_cb, CHUNK, DN_HEAD_DIM) for h in range(PAIR)]
    return jnp.concatenate(halves, axis=0)


def _join_pair(chunks, tok):
    per_head = chunks.reshape(PAIR, tok, DN_HEAD_DIM)
    return jnp.concatenate([per_head[h] for h in range(PAIR)], axis=-1)


def _dn_prep_load(n_cb, q_ref, k_ref, v_ref, blr_ref, lar_ref, al_ref, dt_ref):
    rowf = lambda r: r[0].reshape(PAIR * n_cb, 1, CHUNK)
    return (_split_pair(q_ref[0], n_cb), _split_pair(k_ref[0], n_cb), _split_pair(v_ref[0], n_cb), rowf(blr_ref),
            rowf(lar_ref), al_ref[...], dt_ref[...])


def _dn_prep_pair_fn(n_cb, solve, qc, kc, vc, blr, lar, a_log, dt_bias):
    per_chunk = lambda t: jnp.broadcast_to(t[:, None], (PAIR, n_cb, 1, 1)).reshape(PAIR * n_cb, 1, 1)
    eye = lax.broadcasted_iota(jnp.int32, (CHUNK, CHUNK), 0) == lax.broadcasted_iota(jnp.int32, (CHUNK, CHUNK), 1)
    to_col = lambda r: jnp.sum(jnp.where(eye, r, 0.0), axis=-1, keepdims=True)
    return _dn_prep_fn(solve, qc, kc, vc, to_col(blr), to_col(lar), lar, per_chunk(a_log), per_chunk(dt_bias))


def _dn_prep(qkv, blr, lar, a_log, dt_bias, *, name):
    n_b, n_s, _ = qkv.shape
    n_cb = min(DN_PREP_CHUNKS, n_s // CHUNK)
    tok = n_cb * CHUNK
    wide, rowv, one, head, lanes = _dn_prep_specs(n_cb)

    def body(*refs):
        outs = _dn_prep_pair_fn(n_cb, _solve_by_doubling, *_dn_prep_load(n_cb, *refs[:7]))
        for ref, val in zip(refs[7:12], outs[:5]):
            ref[0] = val.reshape(PAIR, tok, DN_HEAD_DIM)
        refs[12][0] = outs[5].reshape(PAIR, n_cb, 1, 1)
        refs[13][0] = outs[6].reshape(PAIR, tok, DN_HEAD_DIM)

    big = jax.ShapeDtypeStruct((n_b, DN_HEADS, n_s, DN_HEAD_DIM), F32)
    return pl.pallas_call(
        body, name=name, grid=(DN_HEADS // PAIR, n_b, n_s // tok),
        in_specs=[lanes(0), lanes(1), lanes(2), rowv, rowv, head, head],
        out_specs=(wide, wide, wide, wide, wide, one, wide),
        out_shape=(big, big, big, big, big, jax.ShapeDtypeStruct((n_b, DN_HEADS, n_s // CHUNK, 1, 1), F32), big),
        compiler_params=_params(11 * PAIR * _nbytes((tok, LANES), F32), 48 * PAIR * _nbytes((tok, LANES), F32)),
    )(*_hbm(qkv, qkv, qkv, blr, lar, a_log, dt_bias))


def _dn_prep_bwd(qkv, blr, lar, a_log, dt_bias, inv, u, w, cts, *, name):
    n_b, n_s, _ = qkv.shape
    n_cb = min(DN_PREP_CHUNKS, n_s // CHUNK)
    tok = n_cb * CHUNK
    wide, rowv, one, head, lanes = _dn_prep_specs(n_cb)

    def body(*refs):
        prim = _dn_prep_load(n_cb, *refs[:7])
        chunks = lambda r: r[0].reshape(PAIR * n_cb, CHUNK, DN_HEAD_DIM)
        inv_v, u_v, w_v = chunks(refs[7]), chunks(refs[8]), chunks(refs[9])
        ct = tuple(chunks(r) for r in refs[10:15]) + (refs[15][0].reshape(PAIR * n_cb, 1, 1),)

        def fn(*args):
            solve = lambda a, ru, rw: _solve_saved(a, ru, rw, inv_v, u_v, w_v) + (None,)
            return _dn_prep_pair_fn(n_cb, solve, *args)[:6]

        _, pull = jax.vjp(fn, *prim)
        dq, dk, dv, dblr, dlar, dal, ddt = pull(ct)
        outs = refs[16:]
        for part, val in enumerate((dq, dk, dv)):
            outs[0][part, 0] = _join_pair(val, tok)
        outs[1][0] = dblr.reshape(PAIR, n_cb, 1, CHUNK)
        outs[2][0] = dlar.reshape(PAIR, n_cb, 1, CHUNK)
        first = jnp.logical_and(pl.program_id(1) == 0, pl.program_id(2) == 0)
        _accumulate(outs[3], dal, first)
        _accumulate(outs[4], ddt, first)

    dqkv_spec = pl.BlockSpec((3, 1, tok, PAIR_LANES), lambda p, b, j: (0, b, j, p))
    return pl.pallas_call(
        body, name=name, grid=(DN_HEADS // PAIR, n_b, n_s // tok),
        in_specs=[lanes(0), lanes(1), lanes(2), rowv, rowv, head, head, wide, wide, wide, wide, wide, wide, wide, wide, one],
        out_specs=(dqkv_spec, rowv, rowv, head, head),
        out_shape=(jax.ShapeDtypeStruct((3, n_b, n_s, DN_WIDTH), F32), jax.ShapeDtypeStruct(blr.shape, F32),
                   jax.ShapeDtypeStruct(lar.shape, F32), jax.ShapeDtypeStruct(a_log.shape, F32),
                   jax.ShapeDtypeStruct(dt_bias.shape, F32)),
        compiler_params=_params(21 * PAIR * _nbytes((tok, LANES), F32), 64 * PAIR * _nbytes((tok, LANES), F32)),
    )(*_hbm(qkv, qkv, qkv, blr, lar, a_log, dt_bias, inv, u, w, *cts))


def _dn_step(state, q, k, u, w, a, gl):
    v_new = u - _mdot(w, state, _BNN)
    o = _mdot(q, state, _BNN) + _mdot(a, v_new, _BNN)
    return state * jnp.exp(gl) + _mdot(k, v_new, _BTN), o


def _dn_scan_specs(n_cb, n_blocks, reverse):
    tok = n_cb * CHUNK
    jj = (lambda j: n_blocks - 1 - j) if reverse else (lambda j: j)
    wide = pl.BlockSpec((1, DN_SCAN_HEADS, tok, DN_HEAD_DIM), lambda b, h, j: (b, h, jj(j), 0))
    one = pl.BlockSpec((1, DN_SCAN_HEADS, n_cb, 1, 1), lambda b, h, j: (b, h, jj(j), 0, 0))
    st = pl.BlockSpec((1, DN_SCAN_HEADS, n_cb, DN_HEAD_DIM, DN_HEAD_DIM), lambda b, h, j: (b, h, jj(j), 0, 0))
    return wide, one, st


def _dn_scan(qd, kd, u, w, attn, g_last, *, name):
    n_b, n_h, n_s, _ = qd.shape
    n_cb = min(DN_PREP_CHUNKS, n_s // CHUNK)
    n_blocks = n_s // (n_cb * CHUNK)
    wide, one, st = _dn_scan_specs(n_cb, n_blocks, False)

    def body(qd_ref, kd_ref, u_ref, w_ref, a_ref, gl_ref, o_ref, st_ref, state_ref):
        @pl.when(pl.program_id(2) == 0)
        def _():
            state_ref[...] = jnp.zeros(state_ref.shape, F32)

        def step(n, state):
            rows = pl.ds(pl.multiple_of(n * CHUNK, CHUNK), CHUNK)
            st_ref[0, :, n] = state
            state, o = _dn_step(state, qd_ref[0, :, rows, :], kd_ref[0, :, rows, :], u_ref[0, :, rows, :],
                                w_ref[0, :, rows, :], a_ref[0, :, rows, :], gl_ref[0, :, n])
            o_ref[0, :, rows, :] = o
            return state

        state_ref[...] = lax.fori_loop(0, n_cb, step, state_ref[...])

    return pl.pallas_call(
        body, name=name, grid=(n_b, n_h // DN_SCAN_HEADS, n_blocks),
        in_specs=[wide, wide, wide, wide, wide, one], out_specs=(wide, st),
        out_shape=(jax.ShapeDtypeStruct(qd.shape, F32),
                   jax.ShapeDtypeStruct((n_b, n_h, n_s // CHUNK, DN_HEAD_DIM, DN_HEAD_DIM), F32)),
        scratch_shapes=[pltpu.VMEM((DN_SCAN_HEADS, DN_HEAD_DIM, DN_HEAD_DIM), F32)],
        compiler_params=_params(8 * _nbytes((DN_SCAN_HEADS, n_cb * CHUNK, LANES), F32), 8 << 20),
    )(*_hbm(qd, kd, u, w, attn, g_last))


def _dn_scan_bwd(qd, kd, u, w, attn, g_last, states, do, *, name):
    n_b, n_h, n_s, _ = qd.shape
    n_cb = min(DN_PREP_CHUNKS, n_s // CHUNK)
    n_blocks = n_s // (n_cb * CHUNK)
    wide, one, st = _dn_scan_specs(n_cb, n_blocks, True)

    def body(qd_ref, kd_ref, u_ref, w_ref, a_ref, gl_ref, st_ref, do_ref,
             dq_ref, dk_ref, du_ref, dw_ref, da_ref, dgl_ref, dstate_ref):
        @pl.when(pl.program_id(2) == 0)
        def _():
            dstate_ref[...] = jnp.zeros(dstate_ref.shape, F32)

        def step(i, dstate):
            n = n_cb - 1 - i
            rows = pl.ds(pl.multiple_of(n * CHUNK, CHUNK), CHUNK)
            _, pull = jax.vjp(_dn_step, st_ref[0, :, n], qd_ref[0, :, rows, :], kd_ref[0, :, rows, :],
                              u_ref[0, :, rows, :], w_ref[0, :, rows, :], a_ref[0, :, rows, :], gl_ref[0, :, n])
            dstate, dq, dk, du, dw, da, dgl = pull((dstate, do_ref[0, :, rows, :]))
            dq_ref[0, :, rows, :] = dq
            dk_ref[0, :, rows, :] = dk
            du_ref[0, :, rows, :] = du
            dw_ref[0, :, rows, :] = dw
            da_ref[0, :, rows, :] = da
            dgl_ref[0, :, n] = dgl
            return dstate

        dstate_ref[...] = lax.fori_loop(0, n_cb, step, dstate_ref[...])

    big = jax.ShapeDtypeStruct(qd.shape, F32)
    return pl.pallas_call(
        body, name=name, grid=(n_b, n_h // DN_SCAN_HEADS, n_blocks),
        in_specs=[wide, wide, wide, wide, wide, one, st, wide],
        out_specs=(wide, wide, wide, wide, wide, one),
        out_shape=(big, big, big, big, big, jax.ShapeDtypeStruct(g_last.shape, F32)),
        scratch_shapes=[pltpu.VMEM((DN_SCAN_HEADS, DN_HEAD_DIM, DN_HEAD_DIM), F32)],
        compiler_params=_params(13 * _nbytes((DN_SCAN_HEADS, n_cb * CHUNK, LANES), F32), 8 << 20),
    )(*_hbm(qd, kd, u, w, attn, g_last, states, do))


def _dn_post_fn(o, z, gain):
    return o * lax.rsqrt(jnp.mean(o * o, axis=-1, keepdims=True) + EPS) * gain * jax.nn.silu(z)


_HEAD_ROWS = lambda n_s: pl.BlockSpec((1, PAIR, n_s, DN_HEAD_DIM), lambda b, p: (b, p, 0, 0))
_PAIR_LANES = lambda n_s: pl.BlockSpec((1, n_s, PAIR_LANES), lambda b, p: (b, 0, p))
_HEAD_GAIN = pl.BlockSpec((1, DN_HEAD_DIM), lambda b, p: (0, 0))


def _pair_heads(x):
    return jnp.stack([x[:, h * DN_HEAD_DIM:(h + 1) * DN_HEAD_DIM] for h in range(PAIR)])


def _pair_lanes(x):
    return jnp.concatenate([x[h] for h in range(PAIR)], axis=-1)


def _dn_post(o, z, gain, *, name):
    n_b, _, n_s, _ = o.shape

    def body(o_ref, z_ref, g_ref, out_ref):
        out = _dn_post_fn(o_ref[0], _pair_heads(z_ref[0]), g_ref[...])
        out_ref[0] = _pair_lanes(out).astype(out_ref.dtype)

    lanes = _PAIR_LANES(n_s)
    return pl.pallas_call(
        body, name=name, grid=(n_b, DN_HEADS // PAIR), in_specs=[_HEAD_ROWS(n_s), lanes, _HEAD_GAIN], out_specs=lanes,
        out_shape=jax.ShapeDtypeStruct(z.shape, MXU_DTYPE),
        compiler_params=_params(3 * PAIR * _nbytes((n_s, LANES), F32), 6 * PAIR * _nbytes((n_s, LANES), F32)),
    )(*_hbm(o, z, gain))


def _dn_post_bwd(o, z, gain, dout, *, name):
    n_b, _, n_s, _ = o.shape

    def body(o_ref, z_ref, g_ref, dout_ref, do_ref, dz_ref, dg_ref):
        _, pull = jax.vjp(_dn_post_fn, o_ref[0], _pair_heads(z_ref[0]), g_ref[...])
        do, dz, dg = pull(_pair_heads(dout_ref[0].astype(F32)))
        do_ref[0] = do
        dz_ref[0] = _pair_lanes(dz).astype(dz_ref.dtype)
        _accumulate(dg_ref, dg, jnp.logical_and(pl.program_id(0) == 0, pl.program_id(1) == 0))

    rows, lanes = _HEAD_ROWS(n_s), _PAIR_LANES(n_s)
    return pl.pallas_call(
        body, name=name, grid=(n_b, DN_HEADS // PAIR), in_specs=[rows, lanes, _HEAD_GAIN, lanes],
        out_specs=(rows, lanes, _HEAD_GAIN),
        out_shape=(jax.ShapeDtypeStruct(o.shape, F32), jax.ShapeDtypeStruct(z.shape, MXU_DTYPE),
                   jax.ShapeDtypeStruct((1, DN_HEAD_DIM), F32)),
        compiler_params=_params(5 * PAIR * _nbytes((n_s, LANES), F32), 10 * PAIR * _nbytes((n_s, LANES), F32)),
    )(*_hbm(o, z, gain, dout))


TILE_ROWS = SUBLANES


def _s5_prep_fn(lam_re, lam_im, log_step, bt_re, bt_im, c_im):
    lr = jnp.minimum(lam_re, -1e-4)
    step = jnp.exp(log_step)
    mag = jnp.exp(lr * step)
    ang = lam_im * step
    lb_re = mag * jnp.cos(ang)
    lb_im = mag * jnp.sin(ang)
    den = lr * lr + lam_im * lam_im
    coef_re = ((lb_re - 1.0) * lr + lb_im * lam_im) / den
    coef_im = (lb_im * lr - (lb_re - 1.0) * lam_im) / den
    return (lb_re, lb_im, coef_re * bt_re - coef_im * bt_im, coef_re * bt_im + coef_im * bt_re, -c_im)


def _s5_prep(lam_re, lam_im, log_step, bt_re, bt_im, c_im, *, name):
    def body(*refs):
        outs = _s5_prep_fn(*(r[...] for r in refs[:6]))
        for ref, val in zip(refs[6:], outs):
            ref[...] = val

    vec = jax.ShapeDtypeStruct(lam_re.shape, F32)
    mat = jax.ShapeDtypeStruct(bt_re.shape, F32)
    return pl.pallas_call(body, name=name, out_shape=(vec, vec, mat, mat, mat))(lam_re, lam_im, log_step, bt_re, bt_im, c_im)


def _s5_prep_bwd(lam_re, lam_im, log_step, bt_re, bt_im, c_im, cts, *, name):
    def body(*refs):
        _, pull = jax.vjp(_s5_prep_fn, *(r[...] for r in refs[:6]))
        grads = pull(tuple(r[...] for r in refs[6:11]))
        for ref, val in zip(refs[11:], grads):
            ref[...] = val

    shapes = tuple(jax.ShapeDtypeStruct(a.shape, F32) for a in (lam_re, lam_im, log_step, bt_re, bt_im, c_im))
    return pl.pallas_call(body, name=name, out_shape=shapes)(lam_re, lam_im, log_step, bt_re, bt_im, c_im, *cts)


def _cmul(ar, ai, br, bi):
    return ar * br - ai * bi, ar * bi + ai * br


def _s5_powers(lr, li):
    pows = [(lr, li)]
    for _ in range(TILE_ROWS - 1):
        pows.append(_cmul(pows[-1][0], pows[-1][1], lr, li))
    return pows


def _s5_carry_table(pows, n_lanes, reverse):
    row = lax.broadcasted_iota(jnp.int32, (TILE_ROWS, n_lanes), 0)
    t_re = jnp.zeros((TILE_ROWS, n_lanes), F32)
    t_im = jnp.zeros((TILE_ROWS, n_lanes), F32)
    for r in range(TILE_ROWS):
        p_re, p_im = pows[TILE_ROWS - 1 - r] if reverse else pows[r]
        t_re = jnp.where(row == r, p_re, t_re)
        t_im = jnp.where(row == r, p_im, t_im)
    return t_re, t_im


def _s5_tile(y_re, y_im, pows, reverse):
    d = 1
    while d < TILE_ROWS:
        p_re, p_im = pows[d - 1]
        if reverse:
            s_re, s_im = _shift_up(y_re, d), _shift_up(y_im, d)
        else:
            s_re, s_im = _shift_down(y_re, d), _shift_down(y_im, d)
        m_re, m_im = _cmul(p_re, p_im, s_re, s_im)
        y_re, y_im = y_re + m_re, y_im + m_im
        d *= 2
    return y_re, y_im


S5_BLOCKS = N_SHARD
S5_BLOCK_CH = S5_WIDTH // S5_BLOCKS
S5_BLOCK_LANES = S5_LANES // S5_BLOCKS


def _scan_rows(i):
    return pl.ds(pl.multiple_of(i * TILE_ROWS, TILE_ROWS), TILE_ROWS)


def _s5_mix_specs(n_s, order):
    jb = lambda *g: order(*g)[0]
    bb = lambda *g: order(*g)[1]
    act = pl.BlockSpec((1, 1, n_s, S5_BLOCK_CH), lambda *g: (bb(*g), 0, 0, jb(*g)))
    state = pl.BlockSpec((1, 1, n_s, S5_BLOCK_LANES), lambda *g: (jb(*g), bb(*g), 0, 0))
    lam = pl.BlockSpec((1, S5_BLOCK_LANES), lambda *g: (0, jb(*g)))
    w_in = pl.BlockSpec((1, S5_BLOCK_CH, S5_BLOCK_LANES), lambda *g: (jb(*g), 0, 0))
    w_out = pl.BlockSpec((1, S5_BLOCK_LANES, S5_BLOCK_CH), lambda *g: (jb(*g), 0, 0))
    return act, state, lam, w_in, w_out


def _s5_mix(u, wb_re, wb_im, lb_re, lb_im, wc_re, wc_im, *, name):
    n_b, n_s, _ = u.shape
    n_blk = S5_BLOCKS
    lanes = lambda t: t[:, None]
    n_tiles = n_s // TILE_ROWS
    L = S5_BLOCK_LANES

    def body(u_ref, wbr_ref, wbi_ref, lr_ref, li_ref, wcr_ref, wci_ref, y_ref, xr_ref, xi_ref):
        uv = u_ref[0, 0].astype(MXU_DTYPE)
        xr_ref[0, 0] = lax.dot_general(uv, wbr_ref[0].astype(MXU_DTYPE), _NN, preferred_element_type=F32)
        xi_ref[0, 0] = lax.dot_general(uv, wbi_ref[0].astype(MXU_DTYPE), _NN, preferred_element_type=F32)
        pows = _s5_powers(lr_ref[...], li_ref[...])
        t_re, t_im = _s5_carry_table(pows, L, False)

        def step(i, carry):
            rows = _scan_rows(i)
            y_re, y_im = _s5_tile(xr_ref[0, 0, rows, :], xi_ref[0, 0, rows, :], pows, False)
            c_re, c_im = _cmul(t_re, t_im, carry[0], carry[1])
            y_re, y_im = y_re + c_re, y_im + c_im
            xr_ref[0, 0, rows, :] = y_re
            xi_ref[0, 0, rows, :] = y_im
            return y_re[TILE_ROWS - 1:, :], y_im[TILE_ROWS - 1:, :]

        zero = jnp.zeros((1, L), F32)
        lax.fori_loop(0, n_tiles, step, (zero, zero))
        y_ref[0, 0] = (
            lax.dot_general(xr_ref[0, 0].astype(MXU_DTYPE), wcr_ref[0].astype(MXU_DTYPE), _NN, preferred_element_type=F32)
            + lax.dot_general(xi_ref[0, 0].astype(MXU_DTYPE), wci_ref[0].astype(MXU_DTYPE), _NN, preferred_element_type=F32))

    act, state, lam, w_in, w_out = _s5_mix_specs(n_s, lambda b, j: (j, b))
    x_shape = jax.ShapeDtypeStruct((n_blk, n_b, n_s, L), F32)
    return pl.pallas_call(
        body, name=name, grid=(n_b, n_blk),
        in_specs=[act, w_in, w_in, lam, lam, w_out, w_out], out_specs=(act, state, state),
        out_shape=(jax.ShapeDtypeStruct((n_b, 1, n_s, S5_WIDTH), F32), x_shape, x_shape),
        compiler_params=_params(2 * _nbytes((n_s, L), F32) + 2 * _nbytes((n_s, S5_BLOCK_CH), F32), 3 * _nbytes((n_s, L), F32)),
    )(*_hbm(lanes(u), wb_re, wb_im, lb_re, lb_im, wc_re, wc_im))


def _s5_mix_bwd(dy, du_skip, u, x_re, x_im, wb_re, wb_im, lb_re, lb_im, wc_re, wc_im, *, name):
    n_b, n_s, _ = u.shape
    n_blk = S5_BLOCKS
    lanes = lambda t: t[:, None]
    n_tiles = n_s // TILE_ROWS
    L = S5_BLOCK_LANES

    def body(dy_ref, ds_ref, u_ref, xr_ref, xi_ref, wbr_ref, wbi_ref, lr_ref, li_ref, wcr_ref, wci_ref,
             du_ref, dwbr_ref, dwbi_ref, dlr_ref, dli_ref, dwcr_ref, dwci_ref, ar_ref, ai_ref):
        dyv = dy_ref[0, 0].astype(MXU_DTYPE)
        ar_ref[...] = lax.dot_general(dyv, wcr_ref[0].astype(MXU_DTYPE), _NT, preferred_element_type=F32)
        ai_ref[...] = lax.dot_general(dyv, wci_ref[0].astype(MXU_DTYPE), _NT, preferred_element_type=F32)
        pows = _s5_powers(lr_ref[...], -li_ref[...])
        t_re, t_im = _s5_carry_table(pows, L, True)
        row = lax.broadcasted_iota(jnp.int32, (TILE_ROWS, L), 0)

        def step(k, carry):
            c_re, c_im, s_re, s_im = carry
            i = n_tiles - 1 - k
            rows = _scan_rows(i)
            a_re, a_im = _s5_tile(ar_ref[rows, :], ai_ref[rows, :], pows, True)
            m_re, m_im = _cmul(t_re, t_im, c_re, c_im)
            a_re, a_im = a_re + m_re, a_im + m_im
            ar_ref[rows, :] = a_re
            ai_ref[rows, :] = a_im
            prev = _scan_rows(jnp.maximum(i - 1, 0))
            keep = jnp.where(i > 0, 1.0, 0.0)
            last_re = xr_ref[0, 0, prev, :][TILE_ROWS - 1:, :] * keep
            last_im = xi_ref[0, 0, prev, :][TILE_ROWS - 1:, :] * keep
            xp_re = jnp.where(row == 0, last_re, _shift_down(xr_ref[0, 0, rows, :], 1))
            xp_im = jnp.where(row == 0, last_im, _shift_down(xi_ref[0, 0, rows, :], 1))
            s_re = s_re + a_re * xp_re + a_im * xp_im
            s_im = s_im + a_im * xp_re - a_re * xp_im
            return a_re[:1, :], a_im[:1, :], s_re, s_im

        zero = jnp.zeros((1, L), F32)
        zt = jnp.zeros((TILE_ROWS, L), F32)
        _, _, s_re, s_im = lax.fori_loop(0, n_tiles, step, (zero, zero, zt, zt))
        first = pl.program_id(1) == 0
        _accumulate(dlr_ref, jnp.sum(s_re, axis=0, keepdims=True), first)
        _accumulate(dli_ref, jnp.sum(s_im, axis=0, keepdims=True), first)
        a_re, a_im = ar_ref[...].astype(MXU_DTYPE), ai_ref[...].astype(MXU_DTYPE)
        du = (lax.dot_general(a_re, wbr_ref[0].astype(MXU_DTYPE), _NT, preferred_element_type=F32)
              + lax.dot_general(a_im, wbi_ref[0].astype(MXU_DTYPE), _NT, preferred_element_type=F32))
        du_ref[0, 0] = (du + ds_ref[0, 0]).astype(du_ref.dtype)
        uv = u_ref[0, 0].astype(MXU_DTYPE)
        _accumulate(dwbr_ref, lax.dot_general(uv, a_re, _TN, preferred_element_type=F32)[None], first)
        _accumulate(dwbi_ref, lax.dot_general(uv, a_im, _TN, preferred_element_type=F32)[None], first)
        _accumulate(dwcr_ref, lax.dot_general(xr_ref[0, 0].astype(MXU_DTYPE), dyv, _TN, preferred_element_type=F32)[None], first)
        _accumulate(dwci_ref, lax.dot_general(xi_ref[0, 0].astype(MXU_DTYPE), dyv, _TN, preferred_element_type=F32)[None], first)

    act, state, lam, w_in, w_out = _s5_mix_specs(n_s, lambda j, b: (j, b))
    lam_shape = jax.ShapeDtypeStruct((1, S5_LANES), F32)
    return pl.pallas_call(
        body, name=name, grid=(n_blk, n_b),
        in_specs=[act, act, act, state, state, w_in, w_in, lam, lam, w_out, w_out],
        out_specs=(act, w_in, w_in, lam, lam, w_out, w_out),
        out_shape=(jax.ShapeDtypeStruct((n_b, 1, n_s, S5_WIDTH), MXU_DTYPE), jax.ShapeDtypeStruct(wb_re.shape, F32),
                   jax.ShapeDtypeStruct(wb_im.shape, F32), lam_shape, lam_shape,
                   jax.ShapeDtypeStruct(wc_re.shape, F32), jax.ShapeDtypeStruct(wc_im.shape, F32)),
        scratch_shapes=[pltpu.VMEM((n_s, L), F32), pltpu.VMEM((n_s, L), F32)],
        compiler_params=_params(2 * _nbytes((n_s, L), F32) + 4 * _nbytes((n_s, S5_BLOCK_CH), F32), 5 * _nbytes((n_s, L), F32)),
    )(*_hbm(lanes(dy), lanes(du_skip), lanes(u), x_re, x_im, wb_re, wb_im, lb_re, lb_im, wc_re, wc_im))


def _s5_out_fn(ymm, u, d_skip, w_glu, b_glu):
    y = jax.nn.gelu(ymm + d_skip * u)
    return y * jax.nn.sigmoid(_mdot(y, w_glu, _NN) + b_glu)


def _s5_out_specs(tm):
    rows = pl.BlockSpec((tm, S5_WIDTH), lambda i: (i, 0))
    vec = pl.BlockSpec((1, S5_WIDTH), lambda i: (0, 0))
    mat = pl.BlockSpec((S5_WIDTH, S5_WIDTH), lambda i: (0, 0))
    return rows, vec, mat


def _s5_out(ymm, u, d_skip, w_glu, b_glu, *, name, tm=512):
    n_tok = ymm.shape[0]
    tm = min(tm, n_tok)
    rows, vec, mat = _s5_out_specs(tm)

    def body(y_ref, u_ref, d_ref, w_ref, b_ref, o_ref):
        o_ref[...] = _s5_out_fn(y_ref[...], u_ref[...], d_ref[...], w_ref[...], b_ref[...]).astype(o_ref.dtype)

    return pl.pallas_call(
        body, name=name, grid=(n_tok // tm,), in_specs=[rows, rows, vec, mat, vec], out_specs=rows,
        out_shape=jax.ShapeDtypeStruct((n_tok, S5_WIDTH), MXU_DTYPE),
        compiler_params=_params(4 * _nbytes((tm, S5_WIDTH), F32), 8 * _nbytes((tm, S5_WIDTH), F32)),
    )(*_hbm(ymm, u, d_skip, w_glu, b_glu))


def _s5_out_bwd(ymm, u, d_skip, w_glu, b_glu, dout, *, name, tm=512):
    n_tok = ymm.shape[0]
    tm = min(tm, n_tok)
    rows, vec, mat = _s5_out_specs(tm)

    def body(y_ref, u_ref, d_ref, w_ref, b_ref, do_ref, dy_ref, du_ref, dd_ref, dw_ref, db_ref):
        _, pull = jax.vjp(_s5_out_fn, y_ref[...], u_ref[...], d_ref[...], w_ref[...].astype(F32), b_ref[...])
        dy, du, dd, dw, db = pull(do_ref[...])
        dy_ref[...] = dy.astype(dy_ref.dtype)
        du_ref[...] = du
        first = pl.program_id(0) == 0
        _accumulate(dd_ref, dd, first)
        _accumulate(dw_ref, dw, first)
        _accumulate(db_ref, db, first)

    return pl.pallas_call(
        body, name=name, grid=(n_tok // tm,), in_specs=[rows, rows, vec, mat, vec, rows],
        out_specs=(rows, rows, vec, mat, vec),
        out_shape=(jax.ShapeDtypeStruct(ymm.shape, MXU_DTYPE), jax.ShapeDtypeStruct(ymm.shape, F32),
                   jax.ShapeDtypeStruct((1, S5_WIDTH), F32), jax.ShapeDtypeStruct((S5_WIDTH, S5_WIDTH), F32),
                   jax.ShapeDtypeStruct((1, S5_WIDTH), F32)),
        compiler_params=_params(6 * _nbytes((tm, S5_WIDTH), F32), 12 * _nbytes((tm, S5_WIDTH), F32)),
    )(*_hbm(ymm, u, d_skip, w_glu, b_glu, dout))


def _merge_fn(ga, gb, ya, yb):
    return jax.nn.sigmoid(ga) * ya + jax.nn.sigmoid(gb) * yb


def _merge(gab, ya, yb, *, name, tm=512):
    n_tok = ya.shape[0]
    tm = min(tm, n_tok)
    rows = pl.BlockSpec((tm, D_MODEL), lambda i: (i, 0))

    def body(ga_ref, gb_ref, ya_ref, yb_ref, o_ref):
        o_ref[...] = _merge_fn(ga_ref[...], gb_ref[...], ya_ref[...], yb_ref[...]).astype(o_ref.dtype)

    return pl.pallas_call(
        body, name=name, grid=(n_tok // tm,),
        in_specs=[rows, pl.BlockSpec((tm, D_MODEL), lambda i: (i, 1)), rows, rows], out_specs=rows,
        out_shape=jax.ShapeDtypeStruct(ya.shape, MXU_DTYPE),
        compiler_params=_params(5 * _nbytes((tm, D_MODEL), F32), 4 * _nbytes((tm, D_MODEL), F32)),
    )(*_hbm(gab, gab, ya, yb))


def _merge_bwd(gab, ya, yb, dout, *, name, tm=512):
    n_tok = ya.shape[0]
    tm = min(tm, n_tok)
    rows = pl.BlockSpec((tm, D_MODEL), lambda i: (i, 0))

    def body(ga_ref, gb_ref, ya_ref, yb_ref, do_ref, *out_refs):
        _, pull = jax.vjp(_merge_fn, ga_ref[...], gb_ref[...], ya_ref[...], yb_ref[...])
        for ref, val in zip(out_refs, pull(do_ref[...])):
            ref[...] = val.astype(ref.dtype)

    shape = jax.ShapeDtypeStruct(ya.shape, MXU_DTYPE)
    return pl.pallas_call(
        body, name=name, grid=(n_tok // tm,),
        in_specs=[rows, pl.BlockSpec((tm, D_MODEL), lambda i: (i, 1)), rows, rows, rows],
        out_specs=(rows, rows, rows, rows), out_shape=(shape, shape, shape, shape),
        compiler_params=_params(7 * _nbytes((tm, D_MODEL), F32), 6 * _nbytes((tm, D_MODEL), F32)),
    )(*_hbm(gab, gab, ya, yb, dout))


ADA_SHARD = N_MOD * D_MODEL // N_SHARD


def _ada_fwd(c_pad, w_s, b_s, *, name):
    n_r = c_pad.shape[0]

    def body(c_ref, w_ref, b_ref, o_ref):
        sc = jax.nn.silu(c_ref[...]).astype(MXU_DTYPE)
        o_ref[0] = lax.dot_general(sc, w_ref[0].astype(MXU_DTYPE), _NN, preferred_element_type=F32) + b_ref[0]

    return pl.pallas_call(
        body, name=name, grid=(N_SHARD,),
        in_specs=[pl.BlockSpec((n_r, D_MODEL), lambda s: (0, 0)),
                  pl.BlockSpec((1, D_MODEL, ADA_SHARD), lambda s: (s, 0, 0)),
                  pl.BlockSpec((1, 1, ADA_SHARD), lambda s: (s, 0, 0))],
        out_specs=pl.BlockSpec((1, n_r, ADA_SHARD), lambda s: (s, 0, 0)),
        out_shape=jax.ShapeDtypeStruct((N_SHARD, n_r, ADA_SHARD), F32),
        compiler_params=_params(_nbytes((D_MODEL, ADA_SHARD), w_s.dtype), 1 << 20),
    )(*_hbm(c_pad, w_s, b_s))


def _ada_bwd(c_pad, dmod_s, *, name):
    n_r = c_pad.shape[0]

    def body(c_ref, d_ref, dw_ref, db_ref):
        sc = jax.nn.silu(c_ref[...])
        dm = d_ref[0]
        dw_ref[0] = _fdot(sc, dm, _TN)
        db_ref[0] = jnp.sum(dm, axis=0, keepdims=True)

    return pl.pallas_call(
        body, name=name, grid=(N_SHARD,),
        in_specs=[pl.BlockSpec((n_r, D_MODEL), lambda s: (0, 0)), pl.BlockSpec((1, n_r, ADA_SHARD), lambda s: (s, 0, 0))],
        out_specs=(pl.BlockSpec((1, D_MODEL, ADA_SHARD), lambda s: (s, 0, 0)),
                   pl.BlockSpec((1, 1, ADA_SHARD), lambda s: (s, 0, 0))),
        out_shape=(jax.ShapeDtypeStruct((N_SHARD, D_MODEL, ADA_SHARD), F32),
                   jax.ShapeDtypeStruct((N_SHARD, 1, ADA_SHARD), F32)),
        compiler_params=_params(_nbytes((D_MODEL, ADA_SHARD), F32), 2 * _nbytes((D_MODEL, ADA_SHARD), F32)),
    )(*_hbm(c_pad, dmod_s))


def _block_diag(blocks):
    n_per = S5_GROUPS // S5_BLOCKS
    _, n_r, n_c = blocks.shape
    b4 = blocks.reshape(S5_BLOCKS, n_per, n_r, n_c)
    eye = jnp.eye(n_per, dtype=blocks.dtype)
    return (b4[:, :, :, None, :] * eye[None, :, None, :, None]).reshape(S5_BLOCKS, n_per * n_r, n_per * n_c)


def _diag_blocks(mat, n_r, n_c):
    n_per = S5_GROUPS // S5_BLOCKS
    m5 = mat.reshape(S5_BLOCKS, n_per, n_r, n_per, n_c)
    eye = jnp.eye(n_per, dtype=mat.dtype)
    return jnp.sum(m5 * eye[None, :, None, :, None], axis=3).reshape(S5_GROUPS, n_r, n_c)


def _local_step(x, c, target, wts):
    n_b, n_s, _ = x.shape
    n_tok = n_b * n_s
    flat = lambda t: t.reshape(n_tok, t.shape[-1])
    unflat = lambda t: t.reshape(n_b, n_s, t.shape[-1])
    n_chunks = n_s // CHUNK

    c_pad = jnp.zeros((SUBLANES, D_MODEL), F32).at[:n_b].set(c)
    mod_s = _ada_fwd(c_pad, wts["w_ada"], wts["b_ada"], name="ada_fwd")
    mod = mod_s.transpose(1, 0, 2).reshape(SUBLANES, N_MOD * D_MODEL)[:n_b]
    sh1, sc1, gt1, sh2, sc2, gt2, sh3, sc3, gt3 = [m[:, None, :] for m in jnp.split(mod, N_MOD, axis=-1)]

    a1 = _pre(x, None, None, wts["g_ffn1"], sh1, sc1, 0.0, name="pre1")
    f1, ffn1_saved = _ffn_fwd(flat(a1), wts["w1_ffn1"], wts["w3_ffn1"], wts["w2_ffn1"], "ffn1")
    x1, a2 = _pre(x, unflat(f1), gt1, wts["g_mix"], sh2, sc2, 0.5, name="pre2")
    u = flat(a2)[None]
    p_qkv = _mm_act([(u, wts["w_qkv"])], "nt", name="in_qkv")[0]
    p_z = _mm_act([(u, wts["w_z"])], "nt", name="in_z")[0]
    p_gab = _mm_act([(u, wts["w_gab"])], "nt", name="in_gab")[0]
    p_s5 = _mm_act([(u, wts["w_s5"])], "nt", name="in_s5")[0]
    p_ba = _mm_act([(u, wts["w_ba"])], "nt", name="in_ba")[0]

    qkv_c = _conv_fwd(unflat(p_qkv), wts["conv_qkv"], name="conv_fwd")
    z_tok = unflat(p_z)
    ba = p_ba.reshape(n_b, n_s, BA_PAD)
    head_rows = lambda t: t.transpose(0, 2, 1).reshape(n_b, DN_HEADS, n_chunks, 1, CHUNK)
    blr = head_rows(ba[:, :, :DN_HEADS])
    lar = head_rows(ba[:, :, DN_HEADS:2 * DN_HEADS])
    a_log, dt_bias = wts["a_log"], wts["dt_bias"]
    dn_in = (qkv_c, blr, lar, a_log, dt_bias)
    qd, kd, uc, wc, attn, g_last, dn_inv = _dn_prep(*dn_in, name="dn_prep")
    o, states = _dn_scan(qd, kd, uc, wc, attn, g_last, name="dn_scan")
    og = _dn_post(o, z_tok, wts["g_onorm"], name="dn_post")
    og_t = og.reshape(1, n_tok, DN_WIDTH)
    ya = _mm_act([(og_t, wts["w_proj_a"])], "nn", name="proj_a")[0]

    s5p_in = (wts["lam_re"], wts["lam_im"], wts["log_step"], wts["bt_re"], wts["bt_im"], wts["c_im"])
    lb_re, lb_im, bb_re, bb_im, c_neg = _s5_prep(*s5p_in, name="s5_prep")
    wb_re, wb_im = _block_diag(bb_re), _block_diag(bb_im)
    wc_re = _block_diag(wts["c_re"].transpose(0, 2, 1))
    wc_im = _block_diag(c_neg.transpose(0, 2, 1))
    lbr, lbi = lb_re.reshape(1, S5_LANES), lb_im.reshape(1, S5_LANES)
    s5_w = (wb_re, wb_im, lbr, lbi, wc_re, wc_im)
    ymm, x_re, x_im = _s5_mix(unflat(p_s5), *s5_w, name="s5_mix")
    ymm = ymm.reshape(n_tok, S5_WIDTH)
    y2 = _s5_out(ymm, p_s5, wts["d_skip"], wts["w_glu"], wts["b_glu"], name="s5_out")
    yb = _mm_act([(y2[None], wts["w_proj_b"])], "nn", name="proj_b")[0]

    merged = _merge(p_gab, ya, yb, name="merge")
    m_out = _mm_act([(merged[None], wts["w_out"])], "nn", name="mix_out")[0]
    x2, a3 = _pre(x1, unflat(m_out), gt2, wts["g_ffn2"], sh3, sc3, 1.0, name="pre3")
    f3, ffn2_saved = _ffn_fwd(flat(a3), wts["w1_ffn2"], wts["w3_ffn2"], wts["w2_ffn2"], "ffn2")

    g = {}
    loss, dx2_res, df3, dgt3, g["g_final"] = _final(x2, unflat(f3), gt3, wts["g_final"], target, name="final")
    da3, g["w1_ffn2"], g["w3_ffn2"], g["w2_ffn2"] = _ffn_bwd(
        flat(a3), wts["w1_ffn2"], wts["w3_ffn2"], wts["w2_ffn2"], ffn2_saved, flat(df3), "ffn2")
    dx1_res, dm_out, dgt2, g["g_ffn2"], dsh3, dsc3 = _pre_bwd(
        x1, unflat(m_out), gt2, wts["g_ffn2"], sh3, sc3, 1.0, unflat(da3), dx2_res, name="pre3_bwd")
    dm_out = flat(dm_out)[None]
    dmerged = _mm_act([(dm_out, wts["w_out"])], "nt", name="mix_out_bwd")[0]
    g["w_out"] = _mm_tn(merged[None], dm_out, name="dw_out")[0]
    dga, dgb, dya, dyb = _merge_bwd(p_gab, ya, yb, dmerged, name="merge_bwd")

    dy2 = _mm_act([(dyb[None], wts["w_proj_b"])], "nt", name="proj_b_bwd")[0]
    g["w_proj_b"] = _mm_tn(y2[None], dyb[None], name="dw_proj_b")[0]
    dymm, du_skip, g["d_skip"], g["w_glu"], g["b_glu"] = _s5_out_bwd(
        ymm, p_s5, wts["d_skip"], wts["w_glu"], wts["b_glu"], dy2, name="s5_out_bwd")
    dp_s5, dwb_re, dwb_im, dlb_re, dlb_im, dwc_re, dwc_im = _s5_mix_bwd(
        unflat(dymm), unflat(du_skip), unflat(p_s5), x_re, x_im, *s5_w, name="s5_mix_bwd")
    dp_s5 = dp_s5.reshape(n_tok, S5_WIDTH)
    g["c_re"] = _diag_blocks(dwc_re, S5_STATE, S5_GROUP_CH).transpose(0, 2, 1)
    s5_cts = (dlb_re.reshape(lb_re.shape), dlb_im.reshape(lb_im.shape),
              _diag_blocks(dwb_re, S5_GROUP_CH, S5_STATE), _diag_blocks(dwb_im, S5_GROUP_CH, S5_STATE),
              _diag_blocks(dwc_im, S5_STATE, S5_GROUP_CH).transpose(0, 2, 1))
    g["lam_re"], g["lam_im"], g["log_step"], g["bt_re"], g["bt_im"], g["c_im"] = _s5_prep_bwd(
        *s5p_in, s5_cts, name="s5_prep_bwd")

    dog = _mm_act([(dya[None], wts["w_proj_a"])], "nt", name="proj_a_bwd")[0]
    g["w_proj_a"] = _mm_tn(og_t, dya[None], name="dw_proj_a")[0]
    do, dz, g["g_onorm"] = _dn_post_bwd(o, z_tok, wts["g_onorm"], unflat(dog), name="dn_post_bwd")
    scan_cts = _dn_scan_bwd(qd, kd, uc, wc, attn, g_last, states, do, name="dn_scan_bwd")
    dqkv_c, dblr, dlar, g["a_log"], g["dt_bias"] = _dn_prep_bwd(*dn_in, dn_inv, uc, wc, scan_cts, name="dn_prep_bwd")
    dqkv, g["conv_qkv"] = _conv_bwd(unflat(p_qkv), wts["conv_qkv"], dqkv_c, name="conv_bwd")
    token_cols = lambda t: t.reshape(n_b, DN_HEADS, n_s).transpose(0, 2, 1)
    dba = jnp.concatenate([token_cols(dblr), token_cols(dlar),
                           jnp.zeros((n_b, n_s, BA_PAD - 2 * DN_HEADS), F32)], axis=-1).astype(MXU_DTYPE)

    dps = {"w_qkv": flat(dqkv)[None], "w_z": flat(dz)[None], "w_ga": dga[None], "w_gb": dgb[None],
           "w_s5": dp_s5[None], "w_ba": flat(dba)[None]}
    w_ga, w_gb = wts["w_gab"][:, :D_MODEL], wts["w_gab"][:, D_MODEL:]
    w_of = dict(wts, w_ga=w_ga, w_gb=w_gb)
    du = _mm_act([(dps[k], w_of[k]) for k in dps], "nn", name="in_bwd")[0]
    for k in dps:
        g[k] = _mm_tn(dps[k], u, name=f"d{k}")[0]
    dx0_res, df1, dgt1, g["g_mix"], dsh2, dsc2 = _pre_bwd(
        x, unflat(f1), gt1, wts["g_mix"], sh2, sc2, 0.5, unflat(du), dx1_res, name="pre2_bwd")
    da1, g["w1_ffn1"], g["w3_ffn1"], g["w2_ffn1"] = _ffn_bwd(
        flat(a1), wts["w1_ffn1"], wts["w3_ffn1"], wts["w2_ffn1"], ffn1_saved, flat(df1), "ffn1")
    grad_x, g["g_ffn1"], dsh1, dsc1 = _pre_bwd(
        x, None, None, wts["g_ffn1"], sh1, sc1, 0.0, unflat(da1), dx0_res, name="pre1_bwd")

    dmod = jnp.concatenate([t[:, 0, :] for t in (dsh1, dsc1, dgt1, dsh2, dsc2, dgt2, dsh3, dsc3, dgt3)], axis=-1)
    return loss, grad_x, g, dmod


def _ada_grads(c_rows, dmod_rows):
    n_r = c_rows.shape[0]
    n_pad = -n_r % SUBLANES
    c_pad = jnp.pad(c_rows, ((0, n_pad), (0, 0)))
    dmod_s = jnp.pad(dmod_rows, ((0, n_pad), (0, 0))).reshape(n_r + n_pad, N_SHARD, ADA_SHARD).transpose(1, 0, 2)
    dw, db = _ada_bwd(c_pad, dmod_s, name="ada_bwd")
    return dw, db.reshape(1, N_MOD * D_MODEL)


IN_SPLITS = (("w_qkv", 3 * DN_WIDTH), ("w_z", DN_WIDTH), ("w_ba", 2 * DN_HEADS), ("w_s5", S5_WIDTH),
             ("w_ga", D_MODEL), ("w_gb", D_MODEL))
SHARDED = ("w_ada", "w1_ffn1", "w3_ffn1", "w2_ffn1", "w_in", "conv_qkv", "w_glu", "w_proj_a", "w_proj_b", "w_out",
           "w1_ffn2", "w3_ffn2", "w2_ffn2")


def _cat_columns(stack):
    return stack.transpose(1, 0, 2).reshape(stack.shape[1], N_SHARD * stack.shape[2])


def _split_columns(full):
    n_r, n_c = full.shape
    return full.reshape(n_r, N_SHARD, n_c // N_SHARD).transpose(1, 0, 2)


def _gathered_weights(st, rep):
    w = {k: st[k] for k in ("w_ada", "w1_ffn1", "w3_ffn1", "w2_ffn1", "w1_ffn2", "w3_ffn2", "w2_ffn2")}
    w["b_ada"] = rep["b_ada"].reshape(N_SHARD, 1, ADA_SHARD)
    for k in ("g_ffn1", "g_mix", "g_ffn2", "g_final"):
        w[k] = rep[k].reshape(1, D_MODEL)
    w_in_t = st["w_in"].reshape(N_SHARD * st["w_in"].shape[1], D_MODEL)
    start = 0
    for k, size in IN_SPLITS:
        w[k] = w_in_t[None, start:start + size]
        start += size
    w["w_gab"] = jnp.concatenate([w.pop("w_ga"), w.pop("w_gb")], axis=1)
    w["w_ba"] = jnp.pad(w["w_ba"], ((0, 0), (0, BA_PAD - 2 * DN_HEADS), (0, 0)))
    w["conv_qkv"] = _cat_columns(st["conv_qkv"])
    w["a_log"] = rep["a_log"].reshape(DN_HEADS, 1, 1)
    w["dt_bias"] = rep["dt_bias"].reshape(DN_HEADS, 1, 1)
    w["g_onorm"] = rep["g_onorm"].reshape(1, DN_HEAD_DIM)
    w["lam_re"] = rep["lam_re"].reshape(S5_GROUPS, 1, S5_STATE)
    w["lam_im"] = rep["lam_im"].reshape(S5_GROUPS, 1, S5_STATE)
    w["log_step"] = rep["log_step"].reshape(S5_GROUPS, 1, 1)
    w["bt_re"] = rep["b_re"][0].transpose(0, 2, 1)
    w["bt_im"] = rep["b_im"][0].transpose(0, 2, 1)
    w["c_re"] = rep["c_re"][0]
    w["c_im"] = rep["c_im"][0]
    w["d_skip"] = rep["d_skip"].reshape(1, S5_WIDTH)
    w["b_glu"] = rep["b_glu"].reshape(1, S5_WIDTH)
    w["w_glu"] = st["w_glu"].reshape(S5_WIDTH, S5_WIDTH)
    w["w_proj_a"] = _cat_columns(st["w_proj_a"])[None]
    w["w_proj_b"] = _cat_columns(st["w_proj_b"])[None]
    w["w_out"] = st["w_out"].reshape(1, D_MODEL, D_MODEL)
    return w


def _grads_to_problem_layout(g):
    st = {k: g[k] for k in ("w1_ffn1", "w3_ffn1", "w2_ffn1", "w1_ffn2", "w3_ffn2", "w2_ffn2")}
    w_in_t = jnp.concatenate([g[k][:size] for k, size in IN_SPLITS], axis=0)
    st["w_in"] = w_in_t.reshape(N_SHARD, w_in_t.shape[0] // N_SHARD, D_MODEL)
    st["w_glu"] = g["w_glu"].reshape(N_SHARD, S5_WIDTH // N_SHARD, S5_WIDTH)
    st["w_proj_a"] = _split_columns(g["w_proj_a"])
    st["w_proj_b"] = _split_columns(g["w_proj_b"])
    st["w_out"] = g["w_out"].reshape(N_SHARD, D_MODEL // N_SHARD, D_MODEL)
    small = {
        "g_ffn1": g["g_ffn1"], "g_mix": g["g_mix"], "g_ffn2": g["g_ffn2"], "g_final": g["g_final"].reshape(D_MODEL),
        "conv_qkv": g["conv_qkv"][None],
        "a_log": g["a_log"].reshape(1, DN_HEADS), "dt_bias": g["dt_bias"].reshape(1, DN_HEADS),
        "g_onorm": g["g_onorm"],
        "lam_re": g["lam_re"].reshape(1, S5_GROUPS, S5_STATE), "lam_im": g["lam_im"].reshape(1, S5_GROUPS, S5_STATE),
        "log_step": g["log_step"].reshape(1, S5_GROUPS),
        "b_re": g["bt_re"].transpose(0, 2, 1)[None], "b_im": g["bt_im"].transpose(0, 2, 1)[None],
        "c_re": g["c_re"][None], "c_im": g["c_im"][None],
        "d_skip": g["d_skip"], "b_glu": g["b_glu"],
    }
    return st, small


ELEMENTWISE_BLOCK_BYTES = 1 << 20


def _row_tile(n_rows, n_cols, n_lead=1, multiple=SUBLANES):
    best = None
    for t in range(multiple, n_rows + 1, multiple):
        if n_rows % t == 0 and n_lead * t * n_cols * 4 <= ELEMENTWISE_BLOCK_BYTES:
            best = t
    return best if best is not None else n_rows


def _add_sibling_half(g4, recv, my_c, *, name):
    n_sh, _, n_h, n_c = g4.shape
    th = _row_tile(n_h, n_c, multiple=2 * SUBLANES)

    def body(c_ref, g_ref, r_ref, o_ref):
        o_ref[0] = (g_ref[0, 0] + r_ref[0]).astype(o_ref.dtype)

    grid_spec = pltpu.PrefetchScalarGridSpec(
        num_scalar_prefetch=1, grid=(n_sh, n_h // th),
        in_specs=[pl.BlockSpec((1, 1, th, n_c), lambda s, i, c_ref: (s, c_ref[0], i, 0)),
                  pl.BlockSpec((1, th, n_c), lambda s, i, c_ref: (s, i, 0))],
        out_specs=pl.BlockSpec((1, th, n_c), lambda s, i, c_ref: (s, i, 0)))
    return pl.pallas_call(
        body, name=name, grid_spec=grid_spec, out_shape=jax.ShapeDtypeStruct((n_sh, n_h, n_c), MXU_DTYPE),
        compiler_params=_params(3 * _nbytes((th, n_c), F32)),
    )(*_hbm(my_c, g4, recv))


def _sum_slots(parts, *, name):
    n_p, n_r, n_c = parts.shape
    th = _row_tile(n_r, n_c, n_p)

    def body(p_ref, o_ref):
        total = p_ref[0].astype(F32)
        for k in range(1, n_p):
            total = total + p_ref[k].astype(F32)
        o_ref[...] = total

    return pl.pallas_call(
        body, name=name, grid=(n_r // th,),
        in_specs=[pl.BlockSpec((n_p, th, n_c), lambda i: (0, i, 0))],
        out_specs=pl.BlockSpec((th, n_c), lambda i: (i, 0)),
        out_shape=jax.ShapeDtypeStruct((n_r, n_c), F32),
        compiler_params=_params((n_p + 1) * _nbytes((th, n_c), F32)),
    )(*_hbm(parts))


def _cast_into_slot(w, place, dtype, *, name):
    n_r, n_c = w.shape
    th = _row_tile(n_r, n_c, multiple=2 * SUBLANES)

    def body(p_ref, w_ref, o_ref):
        o_ref[0] = w_ref[...].astype(o_ref.dtype)

    grid_spec = pltpu.PrefetchScalarGridSpec(
        num_scalar_prefetch=1, grid=(n_r // th,),
        in_specs=[pl.BlockSpec((th, n_c), lambda i, p: (i, 0))],
        out_specs=pl.BlockSpec((1, th, n_c), lambda i, p: (p[1], i, 0)))
    return pl.pallas_call(
        body, name=name, grid_spec=grid_spec, out_shape=jax.ShapeDtypeStruct((N_SHARD, n_r, n_c), dtype),
        compiler_params=_params(2 * _nbytes((th, n_c), F32)),
    )(*_hbm(place, w))


def _sum_chips(own, parts, place, *, name):
    n_sh, n_h, n_c = own.shape
    th = _row_tile(n_h, n_c, n_sh, multiple=2 * SUBLANES)

    def body(p_ref, own_ref, a_ref, b_ref, c_ref, o_ref):
        o_ref[0] = ((own_ref[0].astype(F32) + a_ref[0].astype(F32)) + b_ref[0].astype(F32)) + c_ref[0].astype(F32)

    slab = lambda k: pl.BlockSpec((1, th, n_c), lambda i, p, k=k: (p[k], i, 0))
    grid_spec = pltpu.PrefetchScalarGridSpec(
        num_scalar_prefetch=1, grid=(n_h // th,),
        in_specs=[slab(1), slab(2), slab(3), slab(4)], out_specs=slab(0))
    return pl.pallas_call(
        body, name=name, grid_spec=grid_spec, out_shape=jax.ShapeDtypeStruct((2, n_h, n_c), F32),
        compiler_params=_params(5 * _nbytes((th, n_c), F32)),
    )(*_hbm(place, own, parts, parts, parts))


def _adamw(w, g, m, v, *, name):
    n_r, n_c = w.shape
    th = _row_tile(n_r, n_c)
    tc = n_c
    if th == n_r and n_c % LANES == 0:
        tc = max(t for t in range(LANES, n_c + 1, LANES) if n_c % t == 0 and (n_r * t * 4 <= ELEMENTWISE_BLOCK_BYTES or t == LANES))
    bias1 = 1.0 - ADAM_B1 ** ADAM_STEP
    bias2 = 1.0 - ADAM_B2 ** ADAM_STEP

    def body(w_ref, g_ref, m_ref, v_ref, d_ref, mo_ref, vo_ref):
        gv = g_ref[...]
        m_new = ADAM_B1 * m_ref[...] + (1.0 - ADAM_B1) * gv
        v_new = ADAM_B2 * v_ref[...] + (1.0 - ADAM_B2) * jnp.square(gv)
        d_ref[...] = -ADAM_LR * ((m_new / bias1) / (jnp.sqrt(v_new / bias2) + ADAM_EPS) + ADAM_WD * w_ref[...])
        mo_ref[...] = m_new
        vo_ref[...] = v_new

    spec = pl.BlockSpec((th, tc), lambda i, j: (i, j))
    shape = jax.ShapeDtypeStruct((n_r, n_c), F32)
    return pl.pallas_call(
        body, name=name, grid=(n_r // th, n_c // tc), in_specs=[spec] * 4, out_specs=(spec,) * 3, out_shape=(shape,) * 3,
        compiler_params=_params(7 * _nbytes((th, tc), F32)),
    )(*_hbm(w, g, m, v))


CHIP_FLIPS = ((1, 0), (0, 1), (1, 1))
DEVICE_FLIPS = tuple((fx, fy, fc) for fx in (0, 1) for fy in (0, 1) for fc in (0, 1))[1:]


def _exchange(ins, out_shapes, plan, n_local, n_remote, *, name, aliased=False):
    n_in, n_out = len(ins), len(out_shapes)

    def body(*refs):
        in_refs, out_refs = refs[:n_in], refs[n_in:n_in + n_out]
        send_sems, recv_sems, local_sems = refs[n_in + n_out:]
        me = (lax.axis_index("x"), lax.axis_index("y"), lax.axis_index("c"))
        local, remote = plan(in_refs, out_refs, me)
        assert len(local) == n_local and len(remote) == n_remote
        here = [pltpu.make_async_copy(src, dst, local_sems.at[i]) for i, (src, dst) in enumerate(local)]
        for cp in here:
            cp.start()
        sends = [pltpu.make_async_remote_copy(src_ref=src, dst_ref=dst, send_sem=send_sems.at[i], recv_sem=recv_sems.at[i],
                                              device_id=peer, device_id_type=pl.DeviceIdType.MESH)
                 for i, (src, dst, _, peer) in enumerate(remote)]
        for cp in sends:
            cp.start()
        for i, (src, _, landing, peer) in enumerate(remote):
            pltpu.make_async_remote_copy(src_ref=src, dst_ref=landing, send_sem=send_sems.at[i], recv_sem=recv_sems.at[i],
                                         device_id=peer, device_id_type=pl.DeviceIdType.MESH).wait_recv()
        for cp in sends:
            cp.wait_send()
        for cp in here:
            cp.wait()

    any_spec = pl.BlockSpec(memory_space=pl.ANY)
    return pl.pallas_call(
        body, name=name, in_specs=[any_spec] * n_in, out_specs=tuple([any_spec] * n_out), out_shape=tuple(out_shapes),
        scratch_shapes=[pltpu.SemaphoreType.DMA((n_remote,)), pltpu.SemaphoreType.DMA((n_remote,)),
                        pltpu.SemaphoreType.DMA((max(n_local, 1),))],
        input_output_aliases={k: k for k in range(n_in)} if aliased else {},
    )(*ins)


def _gather_shards(stacks, *, name):
    n = len(stacks)
    halved = [a.shape[1] % 64 == 0 for a in stacks]
    unit_rows = [a.shape[1] // 2 if h else a.shape[1] for a, h in zip(stacks, halved)]
    part1_rows = [(r // 32) * 16 if r >= 32 else r for r in unit_rows]
    has_part2 = [p < r for p, r in zip(part1_rows, unit_rows)]
    n_sem = sum(2 + 1 + int(h2) + 3 * int(h) for h2, h in zip(has_part2, halved))

    def body(*refs):
        outs = refs[n:2 * n]
        send_sems, recv_sems = refs[2 * n:]
        x, y, c = lax.axis_index("x"), lax.axis_index("y"), lax.axis_index("c")
        mine, chip_x, chip_y, chip_d = 2 * x + y, 2 * (1 - x) + y, 2 * x + (1 - y), 2 * (1 - x) + (1 - y)
        to_x, to_y, sibling = (1 - x, y, c), (x, 1 - y, c), (x, y, 1 - c)

        def region(k, slot, half, part=None):
            start = half * unit_rows[k] if halved[k] else 0
            size = unit_rows[k]
            if part == 1:
                size = part1_rows[k]
            elif part == 2:
                start, size = start + part1_rows[k], unit_rows[k] - part1_rows[k]
            if not halved[k] and part is None:
                return outs[k].at[slot]
            if halved[k]:
                start = pl.multiple_of(start, 16)
            return outs[k].at[slot, pl.ds(start, size)]

        counter = [0]
        started, pending = [], []

        def send(region_of, peer, landing_of):
            i = counter[0]
            counter[0] += 1
            src = region_of
            cp = pltpu.make_async_remote_copy(src_ref=src, dst_ref=src, send_sem=send_sems.at[i], recv_sem=recv_sems.at[i],
                                              device_id=peer, device_id_type=pl.DeviceIdType.MESH)
            cp.start()
            started.append(cp)
            return pltpu.make_async_remote_copy(src_ref=landing_of, dst_ref=landing_of, send_sem=send_sems.at[i],
                                                recv_sem=recv_sems.at[i], device_id=peer, device_id_type=pl.DeviceIdType.MESH)

        from_x = [send(region(k, mine, c), to_x, region(k, chip_x, c)) for k in range(n)]
        from_y = [send(region(k, mine, c), to_y, region(k, chip_y, c)) for k in range(n)]
        diag = []
        for k in range(n):
            from_x[k].wait_recv()
            fwd = [send(region(k, chip_x, c, 1), to_y, region(k, chip_d, c, 1))]
            if halved[k]:
                pending.append(send(region(k, chip_x, c), sibling, region(k, chip_x, 1 - c)))
            from_y[k].wait_recv()
            if has_part2[k]:
                fwd.append(send(region(k, chip_y, c, 2), to_x, region(k, chip_d, c, 2)))
            if halved[k]:
                pending.append(send(region(k, chip_y, c), sibling, region(k, chip_y, 1 - c)))
            diag.append(fwd)
        for k in range(n):
            for landed in diag[k]:
                landed.wait_recv()
            if halved[k]:
                pending.append(send(region(k, chip_d, c), sibling, region(k, chip_d, 1 - c)))
        for landed in pending:
            landed.wait_recv()
        for cp in started:
            cp.wait_send()
        assert counter[0] == n_sem

    any_spec = pl.BlockSpec(memory_space=pl.ANY)
    return pl.pallas_call(
        body, name=name, in_specs=[any_spec] * n, out_specs=tuple([any_spec] * n),
        out_shape=tuple(jax.ShapeDtypeStruct(a.shape, a.dtype) for a in stacks),
        scratch_shapes=[pltpu.SemaphoreType.DMA((n_sem,)), pltpu.SemaphoreType.DMA((n_sem,))],
        input_output_aliases={k: k for k in range(n)},
    )(*stacks)


def _swap_sibling_halves(g4s, *, name):
    n = len(g4s)

    def plan(in_refs, out_refs, me):
        x, y, c = me
        remote = [(in_refs[k].at[:, 1 - c], out_refs[k], out_refs[k], (x, y, 1 - c)) for k in range(n)]
        return [], remote

    shapes = [jax.ShapeDtypeStruct((a.shape[0],) + a.shape[2:], a.dtype) for a in g4s]
    return _exchange(g4s, shapes, plan, 0, n, name=name)


def _scatter_to_chips(hs, *, name):
    n = len(hs)

    def plan(in_refs, out_refs, me):
        x, y, c = me
        mine = 2 * x + y
        remote = []
        for fx, fy in CHIP_FLIPS:
            px, py = x ^ fx, y ^ fy
            peer = 2 * px + py
            for k in range(n):
                remote.append((in_refs[k].at[peer], out_refs[k].at[mine], out_refs[k].at[peer], (px, py, c)))
        return [], remote

    shapes = [jax.ShapeDtypeStruct(a.shape, a.dtype) for a in hs]
    return _exchange(hs, shapes, plan, 0, len(CHIP_FLIPS) * n, name=name)


def _join_sibling_halves(rs, *, name):
    n = len(rs)

    def plan(in_refs, out_refs, me):
        x, y, c = me
        remote = [(out_refs[k].at[c], out_refs[k].at[c], out_refs[k].at[1 - c], (x, y, 1 - c)) for k in range(n)]
        return [], remote

    shapes = [jax.ShapeDtypeStruct(a.shape, a.dtype) for a in rs]
    return _exchange(rs, shapes, plan, 0, n, name=name, aliased=True)


def _gather_all_devices(packed, *, name):
    def plan(in_refs, out_refs, me):
        x, y, c = me
        mine = 4 * x + 2 * y + c
        remote = []
        for fx, fy, fc in DEVICE_FLIPS:
            px, py, pc = x ^ fx, y ^ fy, c ^ fc
            remote.append((in_refs[0], out_refs[0].at[mine], out_refs[0].at[4 * px + 2 * py + pc], (px, py, pc)))
        return [(in_refs[0], out_refs[0].at[mine])], remote

    shape = jax.ShapeDtypeStruct((2 * N_SHARD,) + packed.shape, packed.dtype)
    return _exchange([packed], [shape], plan, 1, len(DEVICE_FLIPS), name=name)[0]


WEIGHT_NAMES = ("w_ada", "b_ada", "g_ffn1", "w1_ffn1", "w3_ffn1", "w2_ffn1", "g_mix", "w_in", "conv_qkv", "a_log",
                "dt_bias", "g_onorm", "lam_re", "lam_im", "log_step", "b_re", "b_im", "c_re", "c_im", "d_skip", "w_glu",
                "b_glu", "w_proj_a", "w_proj_b", "w_out", "g_ffn2", "w1_ffn2", "w3_ffn2", "w2_ffn2", "g_final")
LARGE = tuple(n for n in SHARDED if n != "conv_qkv")
SMALL = tuple(n for n in WEIGHT_NAMES if n not in LARGE)
REDUCED_LARGE = tuple(n for n in LARGE if n != "w_ada")
REDUCED_SMALL = tuple(n for n in SMALL if n != "b_ada")
PACK_ROW = SUBLANES * LANES


def _pack(arrays):
    flat = jnp.concatenate([a.reshape(-1) for a in arrays])
    n_pad = -flat.shape[0] % PACK_ROW
    return jnp.pad(flat, (0, n_pad)).reshape(-1, LANES)


def _unpack(packed, shapes):
    flat = packed.reshape(-1)
    out, start = [], 0
    for s in shapes:
        size = math.prod(s)
        out.append(flat[start:start + size].reshape(s))
        start += size
    return out


def _unpack_slots(gathered, shapes):
    flat = gathered.reshape(gathered.shape[0], -1)
    out, start = [], 0
    for s in shapes:
        size = math.prod(s)
        out.append(flat[:, start:start + size].reshape((gathered.shape[0],) + tuple(s)))
        start += size
    return out


TRANSPOSED = ("w1_ffn1", "w3_ffn1", "w1_ffn2", "w3_ffn2", "w_in")


def _to_internal(name, a):
    return jnp.swapaxes(a[0], 0, 1) if name in TRANSPOSED else a[0]


def _from_internal(name, a):
    return (jnp.swapaxes(a, 0, 1) if name in TRANSPOSED else a)[None]


def _step(x, c, target, weights, m_in, v_in):
    xi, yi, ci = lax.axis_index("x"), lax.axis_index("y"), lax.axis_index("c")
    my_chip = 2 * xi + yi

    others = [k + (k >= my_chip).astype(jnp.int32) for k in range(N_SHARD - 1)]
    place = jnp.stack([ci, my_chip] + others).astype(jnp.int32)

    slots = [_cast_into_slot(_to_internal(n, weights[n]), place, F32 if n == "conv_qkv" else MXU_DTYPE, name=f"cast_{n}")
             for n in SHARDED]
    stacks = dict(zip(SHARDED, _gather_shards(slots, name="gather_weights")))
    rep = {n: weights[n] for n in WEIGHT_NAMES if n not in SHARDED}
    loss, grad_x, g, dmod = _local_step(x, c, target, _gathered_weights(stacks, rep))
    g_stacks, g_small = _grads_to_problem_layout(g)

    g4s = [g_stacks[n].reshape(N_SHARD, 2, g_stacks[n].shape[1] // 2, g_stacks[n].shape[2]) for n in REDUCED_LARGE]
    from_sibling = _swap_sibling_halves(g4s, name="swap_sibling_halves")
    chip_sums = [_add_sibling_half(a, r, place, name=f"chip_sum_{n}") for n, a, r in zip(REDUCED_LARGE, g4s, from_sibling)]
    from_chips = _scatter_to_chips(chip_sums, name="scatter_to_chips")
    reduced = [_sum_chips(h, p, place, name=f"sum_chips_{n}") for n, h, p in zip(REDUCED_LARGE, chip_sums, from_chips)]
    joined = _join_sibling_halves(reduced, name="join_sibling_halves")
    grads_2d = {n: j.reshape(2 * j.shape[1], j.shape[2]) for n, j in zip(REDUCED_LARGE, joined)}
    grads = {n: _from_internal(n, a) for n, a in grads_2d.items()}

    summed_shapes = [g_small[n].shape for n in REDUCED_SMALL] + [(1, 1)]
    packed = _pack([g_small[n] for n in REDUCED_SMALL] + [loss, c, dmod])
    gathered = _gather_all_devices(packed, name="gather_small")
    *small_grads, loss_sum = _unpack(_sum_slots(gathered, name="sum_small"), summed_shapes)
    grads.update(zip(REDUCED_SMALL, small_grads))
    n_conv = weights["conv_qkv"].shape[-1]
    grads["conv_qkv"] = lax.dynamic_slice_in_dim(grads["conv_qkv"], my_chip * n_conv, n_conv, axis=2)
    n_dev = gathered.shape[0]
    rows_of = lambda t: t.reshape(n_dev * t.shape[1], t.shape[2])
    _, c_all, dmod_all = _unpack_slots(gathered, [(sum(math.prod(s) for s in summed_shapes),), c.shape, dmod.shape])
    dw_ada, grads["b_ada"] = _ada_grads(rows_of(c_all), rows_of(dmod_all))
    grads_2d["w_ada"] = lax.dynamic_index_in_dim(dw_ada, my_chip, axis=0, keepdims=False)
    grads["w_ada"] = grads_2d["w_ada"][None]

    delta, new_m, new_v = {}, {}, {}
    grads_2d["conv_qkv"] = grads["conv_qkv"][0]
    for n in LARGE + ("conv_qkv",):
        outs = _adamw(_to_internal(n, weights[n]), grads_2d[n], _to_internal(n, m_in[n]), _to_internal(n, v_in[n]),
                      name=f"adamw_{n}")
        delta[n], new_m[n], new_v[n] = [_from_internal(n, o) for o in outs]
    packed_names = tuple(n for n in SMALL if n != "conv_qkv")
    shapes = [weights[n].shape for n in packed_names]
    outs = _adamw(*[_pack([d[n] for n in packed_names]) for d in (weights, grads, m_in, v_in)], name="adamw_small")
    for d, o in zip((delta, new_m, new_v), outs):
        d.update(zip(packed_names, _unpack(o, shapes)))
    return (loss_sum.reshape(()), grad_x, *[grads[n] for n in WEIGHT_NAMES], *[delta[n] for n in WEIGHT_NAMES],
            *[new_m[n] for n in WEIGHT_NAMES], *[new_v[n] for n in WEIGHT_NAMES])


def kernel(x, c, w_ada, b_ada, g_ffn1, w1_ffn1, w3_ffn1, w2_ffn1, g_mix, w_in, conv_qkv, a_log, dt_bias, g_onorm, lam_re, lam_im, log_step, b_re, b_im, c_re, c_im, d_skip, w_glu, b_glu, w_proj_a, w_proj_b, w_out, g_ffn2, w1_ffn2, w3_ffn2, w2_ffn2, g_final, loss_target, m_w_ada, m_b_ada, m_g_ffn1, m_w1_ffn1, m_w3_ffn1, m_w2_ffn1, m_g_mix, m_w_in, m_conv_qkv, m_a_log, m_dt_bias, m_g_onorm, m_lam_re, m_lam_im, m_log_step, m_b_re, m_b_im, m_c_re, m_c_im, m_d_skip, m_w_glu, m_b_glu, m_w_proj_a, m_w_proj_b, m_w_out, m_g_ffn2, m_w1_ffn2, m_w3_ffn2, m_w2_ffn2, m_g_final, v_w_ada, v_b_ada, v_g_ffn1, v_w1_ffn1, v_w3_ffn1, v_w2_ffn1, v_g_mix, v_w_in, v_conv_qkv, v_a_log, v_dt_bias, v_g_onorm, v_lam_re, v_lam_im, v_log_step, v_b_re, v_b_im, v_c_re, v_c_im, v_d_skip, v_w_glu, v_b_glu, v_w_proj_a, v_w_proj_b, v_w_out, v_g_ffn2, v_w1_ffn2, v_w3_ffn2, v_w2_ffn2, v_g_final):
    w_vals = (w_ada, b_ada, g_ffn1, w1_ffn1, w3_ffn1, w2_ffn1, g_mix, w_in, conv_qkv, a_log, dt_bias, g_onorm, lam_re, lam_im, log_step, b_re, b_im, c_re, c_im, d_skip, w_glu, b_glu, w_proj_a, w_proj_b, w_out, g_ffn2, w1_ffn2, w3_ffn2, w2_ffn2, g_final)
    m_vals = (m_w_ada, m_b_ada, m_g_ffn1, m_w1_ffn1, m_w3_ffn1, m_w2_ffn1, m_g_mix, m_w_in, m_conv_qkv, m_a_log, m_dt_bias, m_g_onorm, m_lam_re, m_lam_im, m_log_step, m_b_re, m_b_im, m_c_re, m_c_im, m_d_skip, m_w_glu, m_b_glu, m_w_proj_a, m_w_proj_b, m_w_out, m_g_ffn2, m_w1_ffn2, m_w3_ffn2, m_w2_ffn2, m_g_final)
    v_vals = (v_w_ada, v_b_ada, v_g_ffn1, v_w1_ffn1, v_w3_ffn1, v_w2_ffn1, v_g_mix, v_w_in, v_conv_qkv, v_a_log, v_dt_bias, v_g_onorm, v_lam_re, v_lam_im, v_log_step, v_b_re, v_b_im, v_c_re, v_c_im, v_d_skip, v_w_glu, v_b_glu, v_w_proj_a, v_w_proj_b, v_w_out, v_g_ffn2, v_w1_ffn2, v_w3_ffn2, v_w2_ffn2, v_g_final)
    return _step(x, c, loss_target, dict(zip(WEIGHT_NAMES, w_vals)), dict(zip(WEIGHT_NAMES, m_vals)),
                 dict(zip(WEIGHT_NAMES, v_vals)))
```

```python
import functools
import math

import jax
import jax.numpy as jnp
from jax import lax
from jax.experimental import pallas as pl
from jax.experimental.pallas import tpu as pltpu

F32 = jnp.float32
BF16 = jnp.bfloat16
MXU_DTYPE = BF16

D_MODEL = 1024
D_FF = 2816
DN_HEADS = 8
DN_HEAD_DIM = 64
DN_WIDTH = DN_HEADS * DN_HEAD_DIM
CONV_WIDTH = 4
CHUNK = 64
S5_GROUP_CH = 16
S5_GROUPS = 32
S5_WIDTH = S5_GROUPS * S5_GROUP_CH
S5_STATE = 64
S5_LANES = S5_GROUPS * S5_STATE
N_MOD = 9
EPS = 1e-6
N_SHARD = 4
FF_SHARD = D_FF // N_SHARD
BA_PAD = 128

ADAM_LR = 0.001
ADAM_B1 = 0.9
ADAM_B2 = 0.999
ADAM_EPS = 1e-08
ADAM_WD = 0.01
ADAM_STEP = 10

VMEM_BYTES_V7X = 64 * 1024 * 1024
SUBLANES = 8
LANES = 128


def _params(block_bytes, extra_bytes=0):
    need = 2 * block_bytes + extra_bytes + (4 << 20)
    return pltpu.CompilerParams(vmem_limit_bytes=int(min(max(need, 16 << 20), VMEM_BYTES_V7X - (8 << 20))))


def _nbytes(shape, dtype):
    return math.prod(shape) * jnp.dtype(dtype).itemsize


HBM_OPERAND_BYTES = 1 << 20


def _hbm(*args):
    return [pltpu.with_memory_space_constraint(a, pltpu.HBM) if _nbytes(a.shape, a.dtype) >= HBM_OPERAND_BYTES else a
            for a in args]


_NN = (((1,), (0,)), ((), ()))
_NT = (((1,), (1,)), ((), ()))
_TN = (((0,), (0,)), ((), ()))


LHS_ROW_BYTES = 4096


def _mm_act(pairs, mode, *, name, reduce_shards=False, out_dtype=F32, tm=None):
    n_tok = pairs[0][0].shape[1]
    n_out = pairs[0][1].shape[2] if mode == "nn" else pairs[0][1].shape[1]
    if tm is None:
        row_bytes = sum(a.shape[2] * jnp.dtype(a.dtype).itemsize for a, _ in pairs)
        tm = 1024 if row_bytes <= LHS_ROW_BYTES else 512
    tm = min(tm, n_tok)
    tn = n_out if n_out <= 1536 else 1024
    assert n_tok % tm == 0 and n_out % tn == 0
    n_red = N_SHARD if reduce_shards else 1
    grid = (n_tok // tm, n_out // tn, n_red)
    dims = _NN if mode == "nn" else _NT
    shard_of = lambda n_sh: (lambda r: 0) if n_sh == 1 else (lambda r: r)

    in_specs, args, blk = [], [], 0
    for a, b in pairs:
        k_dim = a.shape[2]
        sa, sb = shard_of(a.shape[0]), shard_of(b.shape[0])
        in_specs.append(pl.BlockSpec((1, tm, k_dim), lambda i, j, r, sa=sa: (sa(r), i, 0)))
        if mode == "nn":
            assert b.shape[1] == k_dim
            in_specs.append(pl.BlockSpec((1, k_dim, tn), lambda i, j, r, sb=sb: (sb(r), 0, j)))
        else:
            assert b.shape[2] == k_dim
            in_specs.append(pl.BlockSpec((1, tn, k_dim), lambda i, j, r, sb=sb: (sb(r), j, 0)))
        args += [a, b]
        blk += _nbytes((tm, k_dim), a.dtype) + _nbytes((k_dim, tn), b.dtype)
    blk += _nbytes((tm, tn), out_dtype)
    n_pairs = len(pairs)

    def body(*refs):
        out_ref = refs[2 * n_pairs]
        acc = None
        for k in range(n_pairs):
            a = refs[2 * k][0].astype(MXU_DTYPE)
            b = refs[2 * k + 1][0].astype(MXU_DTYPE)
            d = lax.dot_general(a, b, dims, preferred_element_type=F32)
            acc = d if acc is None else acc + d

        if n_red == 1:
            out_ref[0] = acc.astype(out_dtype)
        else:
            acc_ref = refs[-1]
            r = pl.program_id(2)

            @pl.when(r == 0)
            def _():
                acc_ref[...] = acc

            @pl.when(r > 0)
            def _():
                acc_ref[...] += acc

            @pl.when(r == n_red - 1)
            def _():
                out_ref[0] = acc_ref[...].astype(out_dtype)

    return pl.pallas_call(
        body,
        name=name,
        grid=grid,
        in_specs=in_specs,
        out_specs=pl.BlockSpec((1, tm, tn), lambda i, j, r: (0, i, j)),
        out_shape=jax.ShapeDtypeStruct((1, n_tok, n_out), out_dtype),
        scratch_shapes=[pltpu.VMEM((tm, tn), F32)] if n_red > 1 else [],
        compiler_params=_params(blk, 3 * _nbytes((tm, tn), F32)),
    )(*_hbm(*args))


def _mm_tn(a, b, *, name, tt=1024):
    n_tok, k_dim = a.shape[1], a.shape[2]
    n_out = b.shape[2]
    tt = min(tt, n_tok)
    tk = k_dim if k_dim <= 1536 else 1024
    tn = n_out if n_out <= 1536 else 1024
    assert n_tok % tt == 0 and k_dim % tk == 0 and n_out % tn == 0
    n_so = max(a.shape[0], b.shape[0])
    sa = (lambda s: s) if a.shape[0] > 1 else (lambda s: 0)
    sb = (lambda s: s) if b.shape[0] > 1 else (lambda s: 0)
    grid = (n_so, k_dim // tk, n_out // tn, n_tok // tt)

    def body(a_ref, b_ref, out_ref):
        d = lax.dot_general(a_ref[0].astype(MXU_DTYPE), b_ref[0].astype(MXU_DTYPE), _TN, preferred_element_type=F32)
        t = pl.program_id(3)

        @pl.when(t == 0)
        def _():
            out_ref[0] = d

        @pl.when(t > 0)
        def _():
            out_ref[0] += d

    blk = _nbytes((tt, tk), a.dtype) + _nbytes((tt, tn), b.dtype) + _nbytes((tk, tn), F32)
    return pl.pallas_call(
        body,
        name=name,
        grid=grid,
        in_specs=[
            pl.BlockSpec((1, tt, tk), lambda s, ki, nj, t: (sa(s), t, ki)),
            pl.BlockSpec((1, tt, tn), lambda s, ki, nj, t: (sb(s), t, nj)),
        ],
        out_specs=pl.BlockSpec((1, tk, tn), lambda s, ki, nj, t: (s, ki, nj)),
        out_shape=jax.ShapeDtypeStruct((n_so, k_dim, n_out), F32),
        compiler_params=_params(blk, 2 * _nbytes((tk, tn), F32) + _nbytes((tt, tk), F32)),
    )(*_hbm(a, b))


@functools.partial(jax.custom_vjp, nondiff_argnums=(2,))
def _mdot(a, b, dims):
    return lax.dot_general(a.astype(MXU_DTYPE), b.astype(MXU_DTYPE), dims, preferred_element_type=F32)


def _mdot_fwd(a, b, dims):
    return _mdot(a, b, dims), (a, b)


def _mdot_bwd(dims, res, g):
    a, b = res
    (ca, cb), (ba, bb) = dims
    nb = len(ba)
    assert tuple(ba) == tuple(range(nb)) and tuple(bb) == tuple(range(nb)) and len(ca) == 1 and a.ndim == nb + 2
    batch = (tuple(range(nb)), tuple(range(nb)))
    ra, rb = nb, nb + 1
    a_free = (set(range(nb, nb + 2)) - set(ca)).pop()
    b_free = (set(range(nb, nb + 2)) - set(cb)).pop()
    if a_free < ca[0]:
        da = _mdot(g, b, (((rb,), (b_free,)), batch))
    else:
        da = _mdot(b, g, (((b_free,), (rb,)), batch))
    if b_free > cb[0]:
        db = _mdot(a, g, (((a_free,), (ra,)), batch))
    else:
        db = _mdot(g, a, (((ra,), (a_free,)), batch))
    return da.astype(a.dtype), db.astype(b.dtype)


_mdot.defvjp(_mdot_fwd, _mdot_bwd)


def _rms(x, gain):
    return x * lax.rsqrt(jnp.mean(x * x, axis=-1, keepdims=True) + EPS) * gain


def _pre_fn(coef, x_in, f, gate, gain, shift, scale):
    x_new = x_in if f is None else x_in + coef * gate * f
    return x_new, _rms(x_new, gain) * (1.0 + scale) + shift


def _row_spec(ts):
    return pl.BlockSpec((1, ts, D_MODEL), lambda b, j: (b, j, 0))


_BATCH_VEC = pl.BlockSpec((1, 1, D_MODEL), lambda b, j: (b, 0, 0))
_ONE_VEC = pl.BlockSpec((1, D_MODEL), lambda b, j: (0, 0))


def _pre(x_in, f, gate, gain, shift, scale, coef, *, name, ts=512):
    n_b, n_s, _ = x_in.shape
    ts = min(ts, n_s)
    has_res = f is not None

    def body(*refs):
        if has_res:
            x_ref, f_ref, gate_ref, gain_ref, sh_ref, sc_ref, xn_ref, a_ref = refs
            x_new, a = _pre_fn(coef, x_ref[0], f_ref[0], gate_ref[0], gain_ref[...], sh_ref[0], sc_ref[0])
            xn_ref[0] = x_new
        else:
            x_ref, gain_ref, sh_ref, sc_ref, a_ref = refs
            _, a = _pre_fn(coef, x_ref[0], None, None, gain_ref[...], sh_ref[0], sc_ref[0])
        a_ref[0] = a.astype(a_ref.dtype)

    row = _row_spec(ts)
    if has_res:
        args = (x_in, f, gate, gain, shift, scale)
        in_specs = [row, row, _BATCH_VEC, _ONE_VEC, _BATCH_VEC, _BATCH_VEC]
        out_specs = (row, row)
        out_shape = (jax.ShapeDtypeStruct(x_in.shape, F32), jax.ShapeDtypeStruct(x_in.shape, MXU_DTYPE))
    else:
        args = (x_in, gain, shift, scale)
        in_specs = [row, _ONE_VEC, _BATCH_VEC, _BATCH_VEC]
        out_specs = row
        out_shape = jax.ShapeDtypeStruct(x_in.shape, MXU_DTYPE)
    return pl.pallas_call(
        body, name=name, grid=(n_b, n_s // ts), in_specs=in_specs, out_specs=out_specs, out_shape=out_shape,
        compiler_params=_params(5 * _nbytes((ts, D_MODEL), F32), 4 * _nbytes((ts, D_MODEL), F32)),
    )(*_hbm(*args))


def _accumulate(ref, value, first):
    @pl.when(first)
    def _():
        ref[...] = value

    @pl.when(jnp.logical_not(first))
    def _():
        ref[...] += value


def _pre_bwd(x_in, f, gate, gain, shift, scale, coef, da, dx_up, *, name, ts=512):
    n_b, n_s, _ = x_in.shape
    ts = min(ts, n_s)
    has_res = f is not None
    has_up = dx_up is not None

    def body(*refs):
        refs = list(refs)
        x_ref = refs.pop(0)
        f_ref, gate_ref = (refs.pop(0), refs.pop(0)) if has_res else (None, None)
        gain_ref, sh_ref, sc_ref, da_ref = refs.pop(0), refs.pop(0), refs.pop(0), refs.pop(0)
        up_ref = refs.pop(0) if has_up else None
        dx_ref = refs.pop(0)
        df_ref, dgate_ref = (refs.pop(0), refs.pop(0)) if has_res else (None, None)
        dgain_ref, dsh_ref, dsc_ref = refs
        b, j = pl.program_id(0), pl.program_id(1)
        da_v = da_ref[0].astype(F32)
        up_v = up_ref[0] if has_up else jnp.zeros((ts, D_MODEL), F32)
        if has_res:
            fn = functools.partial(_pre_fn, coef)
            _, pull = jax.vjp(fn, x_ref[0], f_ref[0], gate_ref[0], gain_ref[...], sh_ref[0], sc_ref[0])
            dx, df, dgate, dgain, dsh, dsc = pull((up_v, da_v))
            df_ref[0] = df.astype(df_ref.dtype)
            _accumulate(dgate_ref, dgate[None], j == 0)
        else:
            fn = lambda x, g, sh, sc: _pre_fn(coef, x, None, None, g, sh, sc)
            _, pull = jax.vjp(fn, x_ref[0], gain_ref[...], sh_ref[0], sc_ref[0])
            dx, dgain, dsh, dsc = pull((up_v, da_v))
        dx_ref[0] = dx
        _accumulate(dgain_ref, dgain, jnp.logical_and(b == 0, j == 0))
        _accumulate(dsh_ref, dsh[None], j == 0)
        _accumulate(dsc_ref, dsc[None], j == 0)

    row = _row_spec(ts)
    args, in_specs = [x_in], [row]
    if has_res:
        args += [f, gate]
        in_specs += [row, _BATCH_VEC]
    args += [gain, shift, scale, da]
    in_specs += [_ONE_VEC, _BATCH_VEC, _BATCH_VEC, row]
    if has_up:
        args.append(dx_up)
        in_specs.append(row)
    vec = jax.ShapeDtypeStruct((n_b, 1, D_MODEL), F32)
    out_shape, out_specs = [jax.ShapeDtypeStruct(x_in.shape, F32)], [row]
    if has_res:
        out_shape += [jax.ShapeDtypeStruct(x_in.shape, MXU_DTYPE), vec]
        out_specs += [row, _BATCH_VEC]
    out_shape += [jax.ShapeDtypeStruct((1, D_MODEL), F32), vec, vec]
    out_specs += [_ONE_VEC, _BATCH_VEC, _BATCH_VEC]
    return pl.pallas_call(
        body, name=name, grid=(n_b, n_s // ts), in_specs=in_specs, out_specs=tuple(out_specs), out_shape=tuple(out_shape),
        compiler_params=_params(6 * _nbytes((ts, D_MODEL), F32), 8 * _nbytes((ts, D_MODEL), F32)),
    )(*_hbm(*args))


def _final_fn(x_in, f, gate, gain, target):
    x_new = x_in + 0.5 * gate * f
    err = jnp.square(_rms(x_new, gain) - target)
    return 0.5 * jnp.sum(jnp.mean(err, axis=-1))


def _final(x_in, f, gate, gain, target, *, name, ts=512):
    n_b, n_s, _ = x_in.shape
    ts = min(ts, n_s)

    def body(x_ref, f_ref, gate_ref, gain_ref, t_ref, loss_ref, dx_ref, df_ref, dgate_ref, dgain_ref):
        b, j = pl.program_id(0), pl.program_id(1)
        loss, (dx, df, dgate, dgain) = jax.value_and_grad(_final_fn, argnums=(0, 1, 2, 3))(
            x_ref[0], f_ref[0], gate_ref[0], gain_ref[...], t_ref[0])
        first = jnp.logical_and(b == 0, j == 0)
        _accumulate(loss_ref, jnp.reshape(loss, (1, 1)), first)
        dx_ref[0] = dx
        df_ref[0] = df.astype(df_ref.dtype)
        _accumulate(dgate_ref, dgate[None], j == 0)
        _accumulate(dgain_ref, dgain, first)

    row = _row_spec(ts)
    return pl.pallas_call(
        body, name=name, grid=(n_b, n_s // ts),
        in_specs=[row, row, _BATCH_VEC, _ONE_VEC, row],
        out_specs=(pl.BlockSpec((1, 1), lambda b, j: (0, 0)), row, row, _BATCH_VEC, _ONE_VEC),
        out_shape=(jax.ShapeDtypeStruct((1, 1), F32), jax.ShapeDtypeStruct(x_in.shape, F32),
                   jax.ShapeDtypeStruct(x_in.shape, MXU_DTYPE), jax.ShapeDtypeStruct((n_b, 1, D_MODEL), F32),
                   jax.ShapeDtypeStruct((1, D_MODEL), F32)),
        compiler_params=_params(5 * _nbytes((ts, D_MODEL), F32), 8 * _nbytes((ts, D_MODEL), F32)),
    )(*_hbm(x_in, f, gate, gain, target))


FFN_TOKENS = 1024


def _ffn_up(a, w1s, w3s, *, name, tm=FFN_TOKENS):
    n_tok = a.shape[0]
    tm = min(tm, n_tok)

    def body(a_ref, w1_ref, w3_ref, h1_ref, h3_ref, g_ref):
        av = a_ref[...].astype(MXU_DTYPE)
        h1 = lax.dot_general(av, w1_ref[0].astype(MXU_DTYPE), _NT, preferred_element_type=F32)
        h3 = lax.dot_general(av, w3_ref[0].astype(MXU_DTYPE), _NT, preferred_element_type=F32)
        h1_ref[0] = h1.astype(h1_ref.dtype)
        h3_ref[0] = h3.astype(h3_ref.dtype)
        g_ref[0] = (jax.nn.silu(h1) * h3).astype(g_ref.dtype)

    w_spec = pl.BlockSpec((1, FF_SHARD, D_MODEL), lambda s, i: (s, 0, 0))
    h_spec = pl.BlockSpec((1, tm, FF_SHARD), lambda s, i: (s, i, 0))
    h_shape = jax.ShapeDtypeStruct((N_SHARD, n_tok, FF_SHARD), MXU_DTYPE)
    blk = _nbytes((tm, D_MODEL), a.dtype) + 2 * _nbytes((D_MODEL, FF_SHARD), w1s.dtype) + 3 * _nbytes((tm, FF_SHARD), MXU_DTYPE)
    return pl.pallas_call(
        body, name=name, grid=(N_SHARD, n_tok // tm),
        in_specs=[pl.BlockSpec((tm, D_MODEL), lambda s, i: (i, 0)), w_spec, w_spec],
        out_specs=(h_spec, h_spec, h_spec), out_shape=(h_shape, h_shape, h_shape),
        compiler_params=_params(blk, 6 * _nbytes((tm, FF_SHARD), F32)),
    )(*_hbm(a, w1s, w3s))


def _ffn_down_bwd(df, w2s, h1, h3, *, name, tm=FFN_TOKENS):
    n_tok = df.shape[0]
    tm = min(tm, n_tok)

    def body(df_ref, w2_ref, h1_ref, h3_ref, dh1_ref, dh3_ref):
        dg = lax.dot_general(df_ref[...].astype(MXU_DTYPE), w2_ref[0].astype(MXU_DTYPE), _NT, preferred_element_type=F32)
        h1v = h1_ref[0].astype(F32)
        h3v = h3_ref[0].astype(F32)
        sig = jax.nn.sigmoid(h1v)
        dh3_ref[0] = (dg * (h1v * sig)).astype(dh3_ref.dtype)
        dh1_ref[0] = (dg * h3v * (sig * (1.0 + h1v * (1.0 - sig)))).astype(dh1_ref.dtype)

    h_spec = pl.BlockSpec((1, tm, FF_SHARD), lambda s, i: (s, i, 0))
    h_shape = jax.ShapeDtypeStruct((N_SHARD, n_tok, FF_SHARD), MXU_DTYPE)
    blk = _nbytes((tm, D_MODEL), df.dtype) + _nbytes((FF_SHARD, D_MODEL), w2s.dtype) + 4 * _nbytes((tm, FF_SHARD), MXU_DTYPE)
    return pl.pallas_call(
        body, name=name, grid=(N_SHARD, n_tok // tm),
        in_specs=[pl.BlockSpec((tm, D_MODEL), lambda s, i: (i, 0)),
                  pl.BlockSpec((1, FF_SHARD, D_MODEL), lambda s, i: (s, 0, 0)), h_spec, h_spec],
        out_specs=(h_spec, h_spec), out_shape=(h_shape, h_shape),
        compiler_params=_params(blk, 8 * _nbytes((tm, FF_SHARD), F32)),
    )(*_hbm(df, w2s, h1, h3))


def _ffn_fwd(a, w1s, w3s, w2s, tag):
    h1, h3, g = _ffn_up(a, w1s, w3s, name=f"{tag}_up")
    f = _mm_act([(g, w2s)], "nn", reduce_shards=True, tm=FFN_TOKENS, name=f"{tag}_down")[0]
    return f, (h1, h3, g)


def _ffn_bwd(a, w1s, w3s, w2s, saved, df, tag):
    h1, h3, g = saved
    dh1, dh3 = _ffn_down_bwd(df, w2s, h1, h3, name=f"{tag}_down_bwd")
    da = _mm_act([(dh1, w1s), (dh3, w3s)], "nn", reduce_shards=True, tm=FFN_TOKENS, name=f"{tag}_up_bwd")[0]
    a3 = a[None]
    dw1 = _mm_tn(dh1, a3, tt=FFN_TOKENS, name=f"{tag}_dw1")
    dw3 = _mm_tn(dh3, a3, tt=FFN_TOKENS, name=f"{tag}_dw3")
    dw2 = _mm_tn(g, df[None], tt=FFN_TOKENS, name=f"{tag}_dw2")
    return da, dw1, dw3, dw2


CONV_LANES = 256


def _shift_down(x, d):
    if d == 0:
        return x
    row = lax.broadcasted_iota(jnp.int32, x.shape, 0)
    return jnp.where(row >= d, pltpu.roll(x, d, 0), 0.0)


def _shift_up(x, d):
    if d == 0:
        return x
    n = x.shape[0]
    row = lax.broadcasted_iota(jnp.int32, x.shape, 0)
    return jnp.where(row < n - d, pltpu.roll(x, n - d, 0), 0.0)


def _conv_pre(x, w):
    acc = None
    for j in range(CONV_WIDTH):
        term = w[j:j + 1, :] * _shift_down(x, CONV_WIDTH - 1 - j)
        acc = term if acc is None else acc + term
    return acc


def _conv_fwd(x, w, *, name):
    n_b, n_s, n_c = x.shape
    spec = pl.BlockSpec((1, n_s, CONV_LANES), lambda b, cj: (b, 0, cj))

    def body(x_ref, w_ref, o_ref):
        o_ref[0] = jax.nn.silu(_conv_pre(x_ref[0], w_ref[...]))

    return pl.pallas_call(
        body, name=name, grid=(n_b, n_c // CONV_LANES),
        in_specs=[spec, pl.BlockSpec((CONV_WIDTH, CONV_LANES), lambda b, cj: (0, cj))],
        out_specs=spec, out_shape=jax.ShapeDtypeStruct(x.shape, F32),
        compiler_params=_params(2 * _nbytes((n_s, CONV_LANES), F32), 6 * _nbytes((n_s, CONV_LANES), F32)),
    )(*_hbm(x, w))


def _conv_bwd(x, w, dout, *, name):
    n_b, n_s, n_c = x.shape
    per_part = DN_WIDTH // CONV_LANES
    spec = pl.BlockSpec((1, n_s, CONV_LANES), lambda cj, b: (b, 0, cj))
    do_spec = pl.BlockSpec((1, 1, n_s, CONV_LANES), lambda cj, b: (cj // per_part, b, 0, cj % per_part))
    w_spec = pl.BlockSpec((CONV_WIDTH, CONV_LANES), lambda cj, b: (0, cj))

    def body(x_ref, w_ref, do_ref, dx_ref, dw_ref):
        xv, wv = x_ref[0], w_ref[...]
        pre = _conv_pre(xv, wv)
        sig = jax.nn.sigmoid(pre)
        dpre = do_ref[0, 0] * (sig * (1.0 + pre * (1.0 - sig)))
        dx = None
        first = pl.program_id(1) == 0
        for j in range(CONV_WIDTH):
            d = CONV_WIDTH - 1 - j
            ahead = _shift_up(dpre, d)
            term = wv[j:j + 1, :] * ahead
            dx = term if dx is None else dx + term
            dwj = jnp.sum(ahead * xv, axis=0, keepdims=True)
            _accumulate(dw_ref.at[j:j + 1, :], dwj, first)
        dx_ref[0] = dx.astype(dx_ref.dtype)

    return pl.pallas_call(
        body, name=name, grid=(n_c // CONV_LANES, n_b),
        in_specs=[spec, w_spec, do_spec], out_specs=(spec, w_spec),
        out_shape=(jax.ShapeDtypeStruct(x.shape, MXU_DTYPE), jax.ShapeDtypeStruct((CONV_WIDTH, n_c), F32)),
        compiler_params=_params(3 * _nbytes((n_s, CONV_LANES), F32), 8 * _nbytes((n_s, CONV_LANES), F32)),
    )(*_hbm(x, w, dout))


_BNT = (((2,), (2,)), ((0,), (0,)))
_BNN = (((2,), (1,)), ((0,), (0,)))
_BTN = (((1,), (1,)), ((0,), (0,)))
DN_PREP_CHUNKS = 8
DN_SCAN_HEADS = 8
DN_SCAN_CHUNKS = 4
N_DOUBLINGS = 5


def _fdot(a, b, dims):
    return lax.dot_general(a, b, dims, precision=lax.Precision.HIGHEST, preferred_element_type=F32)


def _hdot(a, b, dims):
    return lax.dot_general(a, b, dims, precision=lax.Precision.HIGH, preferred_element_type=F32)


def _solve_by_doubling(a, rhs_u, rhs_w):
    row = lax.broadcasted_iota(jnp.int32, (CHUNK, CHUNK), 0)
    col = lax.broadcasted_iota(jnp.int32, (CHUNK, CHUNK), 1)
    inv = jnp.where(row == col, 1.0, 0.0) - a
    power = a
    for _ in range(N_DOUBLINGS):
        power = _hdot(power, power, _BNN)
        inv = inv + _hdot(inv, power, _BNN)
    return _hdot(inv, rhs_u, _BNN), _hdot(inv, rhs_w, _BNN), inv


@jax.custom_vjp
def _solve_saved(a, rhs_u, rhs_w, inv, u, w):
    return u, w


def _solve_saved_fwd(a, rhs_u, rhs_w, inv, u, w):
    return (u, w), (inv, u, w)


def _solve_saved_bwd(res, cts):
    inv, u, w = res
    gu = _hdot(inv, cts[0], _BTN)
    gw = _hdot(inv, cts[1], _BTN)
    da = -(_hdot(gu, u, _BNT) + _hdot(gw, w, _BNT))
    return da, gu, gw, jnp.zeros_like(inv), jnp.zeros_like(u), jnp.zeros_like(w)


_solve_saved.defvjp(_solve_saved_fwd, _solve_saved_bwd)


def _dn_prep_fn(solve, qc, kc, vc, bl, lac, lar, a_log, dt_bias):
    q = qc * lax.rsqrt(jnp.sum(qc * qc, axis=-1, keepdims=True) + EPS) * (DN_HEAD_DIM ** -0.5)
    k = kc * lax.rsqrt(jnp.sum(kc * kc, axis=-1, keepdims=True) + EPS)
    beta = jax.nn.sigmoid(bl)
    neg_a = -jnp.exp(a_log)
    lgc = neg_a * jax.nn.softplus(lac + dt_bias)
    lgr = neg_a * jax.nn.softplus(lar + dt_bias)
    row = lax.broadcasted_iota(jnp.int32, (CHUNK, CHUNK), 0)
    col = lax.broadcasted_iota(jnp.int32, (CHUNK, CHUNK), 1)
    causal, strict = row >= col, row > col
    g_c = jnp.sum(jnp.where(causal, lgr, 0.0), axis=-1, keepdims=True)
    g_r = jnp.sum(jnp.where(row <= col, lgc, 0.0), axis=-2, keepdims=True)
    decay = jnp.exp(jnp.where(causal, g_c - g_r, -jnp.inf))
    kb = k * beta
    a = jnp.where(strict, _mdot(kb, k, _BNT) * decay, 0.0)
    u, w, extra = solve(a, vc * beta, kb * jnp.exp(g_c))
    attn = _mdot(q, k, _BNT) * decay
    g_last = jnp.sum(lgc, axis=-2, keepdims=True)
    return q * jnp.exp(g_c), k * jnp.exp(g_last - g_c), u, w, attn, g_last, extra


PAIR = 2
PAIR_LANES = PAIR * DN_HEAD_DIM


def _dn_prep_specs(n_cb):
    tok = n_cb * CHUNK
    wide = pl.BlockSpec((1, PAIR, tok, DN_HEAD_DIM), lambda p, b, j: (b, p, j, 0))
    rowv = pl.BlockSpec((1, PAIR, n_cb, 1, CHUNK), lambda p, b, j: (b, p, j, 0, 0))
    one = pl.BlockSpec((1, PAIR, n_cb, 1, 1), lambda p, b, j: (b, p, j, 0, 0))
    head = pl.BlockSpec((PAIR, 1, 1), lambda p, b, j: (p, 0, 0))
    lanes = lambda part: pl.BlockSpec((1, tok, PAIR_LANES), lambda p, b, j: (b, j, part * (DN_HEADS // PAIR) + p))
    return wide, rowv, one, head, lanes


def _split_pair(x, n_cb):
    halves = [x[:, h * DN_HEAD_DIM:(h + 1) * DN_HEAD_DIM].reshape(n_cb, CHUNK, DN_HEAD_DIM) for h in range(PAIR)]
    return jnp.concatenate(halves, axis=0)


def _join_pair(chunks, tok):
    per_head = chunks.reshape(PAIR, tok, DN_HEAD_DIM)
    return jnp.concatenate([per_head[h] for h in range(PAIR)], axis=-1)


def _dn_prep_load(n_cb, q_ref, k_ref, v_ref, blr_ref, lar_ref, al_ref, dt_ref):
    rowf = lambda r: r[0].reshape(PAIR * n_cb, 1, CHUNK)
    return (_split_pair(q_ref[0], n_cb), _split_pair(k_ref[0], n_cb), _split_pair(v_ref[0], n_cb), rowf(blr_ref),
            rowf(lar_ref), al_ref[...], dt_ref[...])


def _dn_prep_pair_fn(n_cb, solve, qc, kc, vc, blr, lar, a_log, dt_bias):
    per_chunk = lambda t: jnp.broadcast_to(t[:, None], (PAIR, n_cb, 1, 1)).reshape(PAIR * n_cb, 1, 1)
    eye = lax.broadcasted_iota(jnp.int32, (CHUNK, CHUNK), 0) == lax.broadcasted_iota(jnp.int32, (CHUNK, CHUNK), 1)
    to_col = lambda r: jnp.sum(jnp.where(eye, r, 0.0), axis=-1, keepdims=True)
    return _dn_prep_fn(solve, qc, kc, vc, to_col(blr), to_col(lar), lar, per_chunk(a_log), per_chunk(dt_bias))


def _dn_prep(qkv, blr, lar, a_log, dt_bias, *, name):
    n_b, n_s, _ = qkv.shape
    n_cb = min(DN_PREP_CHUNKS, n_s // CHUNK)
    tok = n_cb * CHUNK
    wide, rowv, one, head, lanes = _dn_prep_specs(n_cb)

    def body(*refs):
        outs = _dn_prep_pair_fn(n_cb, _solve_by_doubling, *_dn_prep_load(n_cb, *refs[:7]))
        for ref, val in zip(refs[7:12], outs[:5]):
            ref[0] = val.reshape(PAIR, tok, DN_HEAD_DIM)
        refs[12][0] = outs[5].reshape(PAIR, n_cb, 1, 1)
        refs[13][0] = outs[6].reshape(PAIR, tok, DN_HEAD_DIM)

    big = jax.ShapeDtypeStruct((n_b, DN_HEADS, n_s, DN_HEAD_DIM), F32)
    return pl.pallas_call(
        body, name=name, grid=(DN_HEADS // PAIR, n_b, n_s // tok),
        in_specs=[lanes(0), lanes(1), lanes(2), rowv, rowv, head, head],
        out_specs=(wide, wide, wide, wide, wide, one, wide),
        out_shape=(big, big, big, big, big, jax.ShapeDtypeStruct((n_b, DN_HEADS, n_s // CHUNK, 1, 1), F32), big),
        compiler_params=_params(11 * PAIR * _nbytes((tok, LANES), F32), 48 * PAIR * _nbytes((tok, LANES), F32)),
    )(*_hbm(qkv, qkv, qkv, blr, lar, a_log, dt_bias))


def _dn_prep_bwd(qkv, blr, lar, a_log, dt_bias, inv, u, w, cts, *, name):
    n_b, n_s, _ = qkv.shape
    n_cb = min(DN_PREP_CHUNKS, n_s // CHUNK)
    tok = n_cb * CHUNK
    wide, rowv, one, head, lanes = _dn_prep_specs(n_cb)

    def body(*refs):
        prim = _dn_prep_load(n_cb, *refs[:7])
        chunks = lambda r: r[0].reshape(PAIR * n_cb, CHUNK, DN_HEAD_DIM)
        inv_v, u_v, w_v = chunks(refs[7]), chunks(refs[8]), chunks(refs[9])
        ct = tuple(chunks(r) for r in refs[10:15]) + (refs[15][0].reshape(PAIR * n_cb, 1, 1),)

        def fn(*args):
            solve = lambda a, ru, rw: _solve_saved(a, ru, rw, inv_v, u_v, w_v) + (None,)
            return _dn_prep_pair_fn(n_cb, solve, *args)[:6]

        _, pull = jax.vjp(fn, *prim)
        dq, dk, dv, dblr, dlar, dal, ddt = pull(ct)
        outs = refs[16:]
        for part, val in enumerate((dq, dk, dv)):
            outs[0][part, 0] = _join_pair(val, tok)
        outs[1][0] = dblr.reshape(PAIR, n_cb, 1, CHUNK)
        outs[2][0] = dlar.reshape(PAIR, n_cb, 1, CHUNK)
        first = jnp.logical_and(pl.program_id(1) == 0, pl.program_id(2) == 0)
        _accumulate(outs[3], dal, first)
        _accumulate(outs[4], ddt, first)

    dqkv_spec = pl.BlockSpec((3, 1, tok, PAIR_LANES), lambda p, b, j: (0, b, j, p))
    return pl.pallas_call(
        body, name=name, grid=(DN_HEADS // PAIR, n_b, n_s // tok),
        in_specs=[lanes(0), lanes(1), lanes(2), rowv, rowv, head, head, wide, wide, wide, wide, wide, wide, wide, wide, one],
        out_specs=(dqkv_spec, rowv, rowv, head, head),
        out_shape=(jax.ShapeDtypeStruct((3, n_b, n_s, DN_WIDTH), F32), jax.ShapeDtypeStruct(blr.shape, F32),
                   jax.ShapeDtypeStruct(lar.shape, F32), jax.ShapeDtypeStruct(a_log.shape, F32),
                   jax.ShapeDtypeStruct(dt_bias.shape, F32)),
        compiler_params=_params(21 * PAIR * _nbytes((tok, LANES), F32), 64 * PAIR * _nbytes((tok, LANES), F32)),
    )(*_hbm(qkv, qkv, qkv, blr, lar, a_log, dt_bias, inv, u, w, *cts))


def _dn_step(state, q, k, u, w, a, gl):
    v_new = u - _mdot(w, state, _BNN)
    o = _mdot(q, state, _BNN) + _mdot(a, v_new, _BNN)
    return state * jnp.exp(gl) + _mdot(k, v_new, _BTN), o


def _dn_scan_specs(n_cb, n_blocks, reverse):
    tok = n_cb * CHUNK
    jj = (lambda j: n_blocks - 1 - j) if reverse else (lambda j: j)
    wide = pl.BlockSpec((1, DN_SCAN_HEADS, tok, DN_HEAD_DIM), lambda b, h, j: (b, h, jj(j), 0))
    one = pl.BlockSpec((1, DN_SCAN_HEADS, n_cb, 1, 1), lambda b, h, j: (b, h, jj(j), 0, 0))
    st = pl.BlockSpec((1, DN_SCAN_HEADS, n_cb, DN_HEAD_DIM, DN_HEAD_DIM), lambda b, h, j: (b, h, jj(j), 0, 0))
    return wide, one, st


def _dn_scan(qd, kd, u, w, attn, g_last, *, name):
    n_b, n_h, n_s, _ = qd.shape
    n_cb = min(DN_SCAN_CHUNKS, n_s // CHUNK)
    n_blocks = n_s // (n_cb * CHUNK)
    wide, one, st = _dn_scan_specs(n_cb, n_blocks, False)

    def body(qd_ref, kd_ref, u_ref, w_ref, a_ref, gl_ref, o_ref, st_ref, state_ref):
        @pl.when(pl.program_id(2) == 0)
        def _():
            state_ref[...] = jnp.zeros(state_ref.shape, F32)

        def step(n, state):
            rows = pl.ds(pl.multiple_of(n * CHUNK, CHUNK), CHUNK)
            st_ref[0, :, n] = state
            state, o = _dn_step(state, qd_ref[0, :, rows, :], kd_ref[0, :, rows, :], u_ref[0, :, rows, :],
                                w_ref[0, :, rows, :], a_ref[0, :, rows, :], gl_ref[0, :, n])
            o_ref[0, :, rows, :] = o
            return state

        state_ref[...] = lax.fori_loop(0, n_cb, step, state_ref[...])

    return pl.pallas_call(
        body, name=name, grid=(n_b, n_h // DN_SCAN_HEADS, n_blocks),
        in_specs=[wide, wide, wide, wide, wide, one], out_specs=(wide, st),
        out_shape=(jax.ShapeDtypeStruct(qd.shape, F32),
                   jax.ShapeDtypeStruct((n_b, n_h, n_s // CHUNK, DN_HEAD_DIM, DN_HEAD_DIM), F32)),
        scratch_shapes=[pltpu.VMEM((DN_SCAN_HEADS, DN_HEAD_DIM, DN_HEAD_DIM), F32)],
        compiler_params=_params(8 * _nbytes((DN_SCAN_HEADS, n_cb * CHUNK, LANES), F32), 8 << 20),
    )(*_hbm(qd, kd, u, w, attn, g_last))


def _dn_scan_bwd(qd, kd, u, w, attn, g_last, states, do, *, name):
    n_b, n_h, n_s, _ = qd.shape
    n_cb = min(DN_SCAN_CHUNKS, n_s // CHUNK)
    n_blocks = n_s // (n_cb * CHUNK)
    wide, one, st = _dn_scan_specs(n_cb, n_blocks, True)

    def body(qd_ref, kd_ref, u_ref, w_ref, a_ref, gl_ref, st_ref, do_ref,
             dq_ref, dk_ref, du_ref, dw_ref, da_ref, dgl_ref, dstate_ref):
        @pl.when(pl.program_id(2) == 0)
        def _():
            dstate_ref[...] = jnp.zeros(dstate_ref.shape, F32)

        def step(i, dstate):
            n = n_cb - 1 - i
            rows = pl.ds(pl.multiple_of(n * CHUNK, CHUNK), CHUNK)
            _, pull = jax.vjp(_dn_step, st_ref[0, :, n], qd_ref[0, :, rows, :], kd_ref[0, :, rows, :],
                              u_ref[0, :, rows, :], w_ref[0, :, rows, :], a_ref[0, :, rows, :], gl_ref[0, :, n])
            dstate, dq, dk, du, dw, da, dgl = pull((dstate, do_ref[0, :, rows, :]))
            dq_ref[0, :, rows, :] = dq
            dk_ref[0, :, rows, :] = dk
            du_ref[0, :, rows, :] = du
            dw_ref[0, :, rows, :] = dw
            da_ref[0, :, rows, :] = da
            dgl_ref[0, :, n] = dgl
            return dstate

        dstate_ref[...] = lax.fori_loop(0, n_cb, step, dstate_ref[...])

    big = jax.ShapeDtypeStruct(qd.shape, F32)
    return pl.pallas_call(
        body, name=name, grid=(n_b, n_h // DN_SCAN_HEADS, n_blocks),
        in_specs=[wide, wide, wide, wide, wide, one, st, wide],
        out_specs=(wide, wide, wide, wide, wide, one),
        out_shape=(big, big, big, big, big, jax.ShapeDtypeStruct(g_last.shape, F32)),
        scratch_shapes=[pltpu.VMEM((DN_SCAN_HEADS, DN_HEAD_DIM, DN_HEAD_DIM), F32)],
        compiler_params=_params(13 * _nbytes((DN_SCAN_HEADS, n_cb * CHUNK, LANES), F32), 8 << 20),
    )(*_hbm(qd, kd, u, w, attn, g_last, states, do))


def _dn_post_fn(o, z, gain):
    return o * lax.rsqrt(jnp.mean(o * o, axis=-1, keepdims=True) + EPS) * gain * jax.nn.silu(z)


_HEAD_ROWS = lambda n_s: pl.BlockSpec((1, PAIR, n_s, DN_HEAD_DIM), lambda b, p: (b, p, 0, 0))
_PAIR_LANES = lambda n_s: pl.BlockSpec((1, n_s, PAIR_LANES), lambda b, p: (b, 0, p))
_HEAD_GAIN = pl.BlockSpec((1, DN_HEAD_DIM), lambda b, p: (0, 0))


def _pair_heads(x):
    return jnp.stack([x[:, h * DN_HEAD_DIM:(h + 1) * DN_HEAD_DIM] for h in range(PAIR)])


def _pair_lanes(x):
    return jnp.concatenate([x[h] for h in range(PAIR)], axis=-1)


def _dn_post(o, z, gain, *, name):
    n_b, _, n_s, _ = o.shape

    def body(o_ref, z_ref, g_ref, out_ref):
        out = _dn_post_fn(o_ref[0], _pair_heads(z_ref[0]), g_ref[...])
        out_ref[0] = _pair_lanes(out).astype(out_ref.dtype)

    lanes = _PAIR_LANES(n_s)
    return pl.pallas_call(
        body, name=name, grid=(n_b, DN_HEADS // PAIR), in_specs=[_HEAD_ROWS(n_s), lanes, _HEAD_GAIN], out_specs=lanes,
        out_shape=jax.ShapeDtypeStruct(z.shape, MXU_DTYPE),
        compiler_params=_params(3 * PAIR * _nbytes((n_s, LANES), F32), 6 * PAIR * _nbytes((n_s, LANES), F32)),
    )(*_hbm(o, z, gain))


def _dn_post_bwd(o, z, gain, dout, *, name):
    n_b, _, n_s, _ = o.shape

    def body(o_ref, z_ref, g_ref, dout_ref, do_ref, dz_ref, dg_ref):
        _, pull = jax.vjp(_dn_post_fn, o_ref[0], _pair_heads(z_ref[0]), g_ref[...])
        do, dz, dg = pull(_pair_heads(dout_ref[0].astype(F32)))
        do_ref[0] = do
        dz_ref[0] = _pair_lanes(dz).astype(dz_ref.dtype)
        _accumulate(dg_ref, dg, jnp.logical_and(pl.program_id(0) == 0, pl.program_id(1) == 0))

    rows, lanes = _HEAD_ROWS(n_s), _PAIR_LANES(n_s)
    return pl.pallas_call(
        body, name=name, grid=(n_b, DN_HEADS // PAIR), in_specs=[rows, lanes, _HEAD_GAIN, lanes],
        out_specs=(rows, lanes, _HEAD_GAIN),
        out_shape=(jax.ShapeDtypeStruct(o.shape, F32), jax.ShapeDtypeStruct(z.shape, MXU_DTYPE),
                   jax.ShapeDtypeStruct((1, DN_HEAD_DIM), F32)),
        compiler_params=_params(5 * PAIR * _nbytes((n_s, LANES), F32), 10 * PAIR * _nbytes((n_s, LANES), F32)),
    )(*_hbm(o, z, gain, dout))


TILE_ROWS = SUBLANES


def _s5_prep_fn(lam_re, lam_im, log_step, bt_re, bt_im, c_im):
    lr = jnp.minimum(lam_re, -1e-4)
    step = jnp.exp(log_step)
    mag = jnp.exp(lr * step)
    ang = lam_im * step
    lb_re = mag * jnp.cos(ang)
    lb_im = mag * jnp.sin(ang)
    den = lr * lr + lam_im * lam_im
    coef_re = ((lb_re - 1.0) * lr + lb_im * lam_im) / den
    coef_im = (lb_im * lr - (lb_re - 1.0) * lam_im) / den
    return (lb_re, lb_im, coef_re * bt_re - coef_im * bt_im, coef_re * bt_im + coef_im * bt_re, -c_im)


def _s5_prep(lam_re, lam_im, log_step, bt_re, bt_im, c_im, *, name):
    def body(*refs):
        outs = _s5_prep_fn(*(r[...] for r in refs[:6]))
        for ref, val in zip(refs[6:], outs):
            ref[...] = val

    vec = jax.ShapeDtypeStruct(lam_re.shape, F32)
    mat = jax.ShapeDtypeStruct(bt_re.shape, F32)
    return pl.pallas_call(body, name=name, out_shape=(vec, vec, mat, mat, mat))(lam_re, lam_im, log_step, bt_re, bt_im, c_im)


def _s5_prep_bwd(lam_re, lam_im, log_step, bt_re, bt_im, c_im, cts, *, name):
    def body(*refs):
        _, pull = jax.vjp(_s5_prep_fn, *(r[...] for r in refs[:6]))
        grads = pull(tuple(r[...] for r in refs[6:11]))
        for ref, val in zip(refs[11:], grads):
            ref[...] = val

    shapes = tuple(jax.ShapeDtypeStruct(a.shape, F32) for a in (lam_re, lam_im, log_step, bt_re, bt_im, c_im))
    return pl.pallas_call(body, name=name, out_shape=shapes)(lam_re, lam_im, log_step, bt_re, bt_im, c_im, *cts)


def _cmul(ar, ai, br, bi):
    return ar * br - ai * bi, ar * bi + ai * br


def _s5_powers(lr, li):
    pows = [(lr, li)]
    for _ in range(TILE_ROWS - 1):
        pows.append(_cmul(pows[-1][0], pows[-1][1], lr, li))
    return pows


def _s5_carry_table(pows, n_lanes, reverse):
    row = lax.broadcasted_iota(jnp.int32, (TILE_ROWS, n_lanes), 0)
    t_re = jnp.zeros((TILE_ROWS, n_lanes), F32)
    t_im = jnp.zeros((TILE_ROWS, n_lanes), F32)
    for r in range(TILE_ROWS):
        p_re, p_im = pows[TILE_ROWS - 1 - r] if reverse else pows[r]
        t_re = jnp.where(row == r, p_re, t_re)
        t_im = jnp.where(row == r, p_im, t_im)
    return t_re, t_im


def _s5_tile(y_re, y_im, pows, reverse):
    d = 1
    while d < TILE_ROWS:
        p_re, p_im = pows[d - 1]
        if reverse:
            s_re, s_im = _shift_up(y_re, d), _shift_up(y_im, d)
        else:
            s_re, s_im = _shift_down(y_re, d), _shift_down(y_im, d)
        m_re, m_im = _cmul(p_re, p_im, s_re, s_im)
        y_re, y_im = y_re + m_re, y_im + m_im
        d *= 2
    return y_re, y_im


S5_BLOCKS = N_SHARD
S5_BLOCK_CH = S5_WIDTH // S5_BLOCKS
S5_BLOCK_LANES = S5_LANES // S5_BLOCKS


def _scan_rows(i):
    return pl.ds(pl.multiple_of(i * TILE_ROWS, TILE_ROWS), TILE_ROWS)


def _s5_mix_specs(n_s, order):
    jb = lambda *g: order(*g)[0]
    bb = lambda *g: order(*g)[1]
    act = pl.BlockSpec((1, 1, n_s, S5_BLOCK_CH), lambda *g: (bb(*g), 0, 0, jb(*g)))
    state = pl.BlockSpec((1, 1, n_s, S5_BLOCK_LANES), lambda *g: (jb(*g), bb(*g), 0, 0))
    lam = pl.BlockSpec((1, S5_BLOCK_LANES), lambda *g: (0, jb(*g)))
    w_in = pl.BlockSpec((1, S5_BLOCK_CH, S5_BLOCK_LANES), lambda *g: (jb(*g), 0, 0))
    w_out = pl.BlockSpec((1, S5_BLOCK_LANES, S5_BLOCK_CH), lambda *g: (jb(*g), 0, 0))
    return act, state, lam, w_in, w_out


def _s5_mix(u, wb_re, wb_im, lb_re, lb_im, wc_re, wc_im, *, name):
    n_b, n_s, _ = u.shape
    n_blk = S5_BLOCKS
    lanes = lambda t: t[:, None]
    n_tiles = n_s // TILE_ROWS
    L = S5_BLOCK_LANES

    def body(u_ref, wbr_ref, wbi_ref, lr_ref, li_ref, wcr_ref, wci_ref, y_ref, xr_ref, xi_ref):
        uv = u_ref[0, 0].astype(MXU_DTYPE)
        xr_ref[0, 0] = lax.dot_general(uv, wbr_ref[0].astype(MXU_DTYPE), _NN, preferred_element_type=F32)
        xi_ref[0, 0] = lax.dot_general(uv, wbi_ref[0].astype(MXU_DTYPE), _NN, preferred_element_type=F32)
        pows = _s5_powers(lr_ref[...], li_ref[...])
        t_re, t_im = _s5_carry_table(pows, L, False)

        def step(i, carry):
            rows = _scan_rows(i)
            y_re, y_im = _s5_tile(xr_ref[0, 0, rows, :], xi_ref[0, 0, rows, :], pows, False)
            c_re, c_im = _cmul(t_re, t_im, carry[0], carry[1])
            y_re, y_im = y_re + c_re, y_im + c_im
            xr_ref[0, 0, rows, :] = y_re
            xi_ref[0, 0, rows, :] = y_im
            return y_re[TILE_ROWS - 1:, :], y_im[TILE_ROWS - 1:, :]

        zero = jnp.zeros((1, L), F32)
        lax.fori_loop(0, n_tiles, step, (zero, zero), unroll=2)
        y_ref[0, 0] = (
            lax.dot_general(xr_ref[0, 0].astype(MXU_DTYPE), wcr_ref[0].astype(MXU_DTYPE), _NN, preferred_element_type=F32)
            + lax.dot_general(xi_ref[0, 0].astype(MXU_DTYPE), wci_ref[0].astype(MXU_DTYPE), _NN, preferred_element_type=F32))

    act, state, lam, w_in, w_out = _s5_mix_specs(n_s, lambda b, j: (j, b))
    x_shape = jax.ShapeDtypeStruct((n_blk, n_b, n_s, L), F32)
    return pl.pallas_call(
        body, name=name, grid=(n_b, n_blk),
        in_specs=[act, w_in, w_in, lam, lam, w_out, w_out], out_specs=(act, state, state),
        out_shape=(jax.ShapeDtypeStruct((n_b, 1, n_s, S5_WIDTH), F32), x_shape, x_shape),
        compiler_params=_params(2 * _nbytes((n_s, L), F32) + 2 * _nbytes((n_s, S5_BLOCK_CH), F32), 3 * _nbytes((n_s, L), F32)),
    )(*_hbm(lanes(u), wb_re, wb_im, lb_re, lb_im, wc_re, wc_im))


def _s5_mix_bwd(dy, du_skip, u, x_re, x_im, wb_re, wb_im, lb_re, lb_im, wc_re, wc_im, *, name):
    n_b, n_s, _ = u.shape
    n_blk = S5_BLOCKS
    lanes = lambda t: t[:, None]
    n_tiles = n_s // TILE_ROWS
    L = S5_BLOCK_LANES

    def body(dy_ref, ds_ref, u_ref, xr_ref, xi_ref, wbr_ref, wbi_ref, lr_ref, li_ref, wcr_ref, wci_ref,
             du_ref, dwbr_ref, dwbi_ref, dlr_ref, dli_ref, dwcr_ref, dwci_ref, ar_ref, ai_ref):
        dyv = dy_ref[0, 0].astype(MXU_DTYPE)
        ar_ref[...] = lax.dot_general(dyv, wcr_ref[0].astype(MXU_DTYPE), _NT, preferred_element_type=F32)
        ai_ref[...] = lax.dot_general(dyv, wci_ref[0].astype(MXU_DTYPE), _NT, preferred_element_type=F32)
        pows = _s5_powers(lr_ref[...], -li_ref[...])
        t_re, t_im = _s5_carry_table(pows, L, True)
        row = lax.broadcasted_iota(jnp.int32, (TILE_ROWS, L), 0)

        def step(k, carry):
            c_re, c_im, s_re, s_im = carry
            i = n_tiles - 1 - k
            rows = _scan_rows(i)
            a_re, a_im = _s5_tile(ar_ref[rows, :], ai_ref[rows, :], pows, True)
            m_re, m_im = _cmul(t_re, t_im, c_re, c_im)
            a_re, a_im = a_re + m_re, a_im + m_im
            ar_ref[rows, :] = a_re
            ai_ref[rows, :] = a_im
            prev = _scan_rows(jnp.maximum(i - 1, 0))
            keep = jnp.where(i > 0, 1.0, 0.0)
            last_re = xr_ref[0, 0, prev, :][TILE_ROWS - 1:, :] * keep
            last_im = xi_ref[0, 0, prev, :][TILE_ROWS - 1:, :] * keep
            xp_re = jnp.where(row == 0, last_re, _shift_down(xr_ref[0, 0, rows, :], 1))
            xp_im = jnp.where(row == 0, last_im, _shift_down(xi_ref[0, 0, rows, :], 1))
            s_re = s_re + a_re * xp_re + a_im * xp_im
            s_im = s_im + a_im * xp_re - a_re * xp_im
            return a_re[:1, :], a_im[:1, :], s_re, s_im

        zero = jnp.zeros((1, L), F32)
        zt = jnp.zeros((TILE_ROWS, L), F32)
        _, _, s_re, s_im = lax.fori_loop(0, n_tiles, step, (zero, zero, zt, zt), unroll=2)
        first = pl.program_id(1) == 0
        _accumulate(dlr_ref, jnp.sum(s_re, axis=0, keepdims=True), first)
        _accumulate(dli_ref, jnp.sum(s_im, axis=0, keepdims=True), first)
        a_re, a_im = ar_ref[...].astype(MXU_DTYPE), ai_ref[...].astype(MXU_DTYPE)
        du = (lax.dot_general(a_re, wbr_ref[0].astype(MXU_DTYPE), _NT, preferred_element_type=F32)
              + lax.dot_general(a_im, wbi_ref[0].astype(MXU_DTYPE), _NT, preferred_element_type=F32))
        du_ref[0, 0] = (du + ds_ref[0, 0]).astype(du_ref.dtype)
        uv = u_ref[0, 0].astype(MXU_DTYPE)
        _accumulate(dwbr_ref, lax.dot_general(uv, a_re, _TN, preferred_element_type=F32)[None], first)
        _accumulate(dwbi_ref, lax.dot_general(uv, a_im, _TN, preferred_element_type=F32)[None], first)
        _accumulate(dwcr_ref, lax.dot_general(xr_ref[0, 0].astype(MXU_DTYPE), dyv, _TN, preferred_element_type=F32)[None], first)
        _accumulate(dwci_ref, lax.dot_general(xi_ref[0, 0].astype(MXU_DTYPE), dyv, _TN, preferred_element_type=F32)[None], first)

    act, state, lam, w_in, w_out = _s5_mix_specs(n_s, lambda j, b: (j, b))
    lam_shape = jax.ShapeDtypeStruct((1, S5_LANES), F32)
    return pl.pallas_call(
        body, name=name, grid=(n_blk, n_b),
        in_specs=[act, act, act, state, state, w_in, w_in, lam, lam, w_out, w_out],
        out_specs=(act, w_in, w_in, lam, lam, w_out, w_out),
        out_shape=(jax.ShapeDtypeStruct((n_b, 1, n_s, S5_WIDTH), MXU_DTYPE), jax.ShapeDtypeStruct(wb_re.shape, F32),
                   jax.ShapeDtypeStruct(wb_im.shape, F32), lam_shape, lam_shape,
                   jax.ShapeDtypeStruct(wc_re.shape, F32), jax.ShapeDtypeStruct(wc_im.shape, F32)),
        scratch_shapes=[pltpu.VMEM((n_s, L), F32), pltpu.VMEM((n_s, L), F32)],
        compiler_params=_params(2 * _nbytes((n_s, L), F32) + 4 * _nbytes((n_s, S5_BLOCK_CH), F32), 5 * _nbytes((n_s, L), F32)),
    )(*_hbm(lanes(dy), lanes(du_skip), lanes(u), x_re, x_im, wb_re, wb_im, lb_re, lb_im, wc_re, wc_im))


def _s5_out_fn(ymm, u, d_skip, w_glu, b_glu):
    y = jax.nn.gelu(ymm + d_skip * u)
    return y * jax.nn.sigmoid(_mdot(y, w_glu, _NN) + b_glu)


def _s5_out_specs(tm):
    rows = pl.BlockSpec((tm, S5_WIDTH), lambda i: (i, 0))
    vec = pl.BlockSpec((1, S5_WIDTH), lambda i: (0, 0))
    mat = pl.BlockSpec((S5_WIDTH, S5_WIDTH), lambda i: (0, 0))
    return rows, vec, mat


def _s5_out(ymm, u, d_skip, w_glu, b_glu, *, name, tm=512):
    n_tok = ymm.shape[0]
    tm = min(tm, n_tok)
    rows, vec, mat = _s5_out_specs(tm)

    def body(y_ref, u_ref, d_ref, w_ref, b_ref, o_ref):
        o_ref[...] = _s5_out_fn(y_ref[...], u_ref[...], d_ref[...], w_ref[...], b_ref[...]).astype(o_ref.dtype)

    return pl.pallas_call(
        body, name=name, grid=(n_tok // tm,), in_specs=[rows, rows, vec, mat, vec], out_specs=rows,
        out_shape=jax.ShapeDtypeStruct((n_tok, S5_WIDTH), MXU_DTYPE),
        compiler_params=_params(4 * _nbytes((tm, S5_WIDTH), F32), 8 * _nbytes((tm, S5_WIDTH), F32)),
    )(*_hbm(ymm, u, d_skip, w_glu, b_glu))


def _s5_out_bwd(ymm, u, d_skip, w_glu, b_glu, dout, *, name, tm=512):
    n_tok = ymm.shape[0]
    tm = min(tm, n_tok)
    rows, vec, mat = _s5_out_specs(tm)

    def body(y_ref, u_ref, d_ref, w_ref, b_ref, do_ref, dy_ref, du_ref, dd_ref, dw_ref, db_ref):
        _, pull = jax.vjp(_s5_out_fn, y_ref[...], u_ref[...], d_ref[...], w_ref[...].astype(F32), b_ref[...])
        dy, du, dd, dw, db = pull(do_ref[...])
        dy_ref[...] = dy.astype(dy_ref.dtype)
        du_ref[...] = du
        first = pl.program_id(0) == 0
        _accumulate(dd_ref, dd, first)
        _accumulate(dw_ref, dw, first)
        _accumulate(db_ref, db, first)

    return pl.pallas_call(
        body, name=name, grid=(n_tok // tm,), in_specs=[rows, rows, vec, mat, vec, rows],
        out_specs=(rows, rows, vec, mat, vec),
        out_shape=(jax.ShapeDtypeStruct(ymm.shape, MXU_DTYPE), jax.ShapeDtypeStruct(ymm.shape, F32),
                   jax.ShapeDtypeStruct((1, S5_WIDTH), F32), jax.ShapeDtypeStruct((S5_WIDTH, S5_WIDTH), F32),
                   jax.ShapeDtypeStruct((1, S5_WIDTH), F32)),
        compiler_params=_params(6 * _nbytes((tm, S5_WIDTH), F32), 12 * _nbytes((tm, S5_WIDTH), F32)),
    )(*_hbm(ymm, u, d_skip, w_glu, b_glu, dout))


def _merge_fn(ga, gb, ya, yb):
    return jax.nn.sigmoid(ga) * ya + jax.nn.sigmoid(gb) * yb


def _merge(gab, ya, yb, *, name, tm=512):
    n_tok = ya.shape[0]
    tm = min(tm, n_tok)
    rows = pl.BlockSpec((tm, D_MODEL), lambda i: (i, 0))

    def body(ga_ref, gb_ref, ya_ref, yb_ref, o_ref):
        o_ref[...] = _merge_fn(ga_ref[...], gb_ref[...], ya_ref[...], yb_ref[...]).astype(o_ref.dtype)

    return pl.pallas_call(
        body, name=name, grid=(n_tok // tm,),
        in_specs=[rows, pl.BlockSpec((tm, D_MODEL), lambda i: (i, 1)), rows, rows], out_specs=rows,
        out_shape=jax.ShapeDtypeStruct(ya.shape, MXU_DTYPE),
        compiler_params=_params(5 * _nbytes((tm, D_MODEL), F32), 4 * _nbytes((tm, D_MODEL), F32)),
    )(*_hbm(gab, gab, ya, yb))


def _merge_bwd(gab, ya, yb, dout, *, name, tm=512):
    n_tok = ya.shape[0]
    tm = min(tm, n_tok)
    rows = pl.BlockSpec((tm, D_MODEL), lambda i: (i, 0))

    def body(ga_ref, gb_ref, ya_ref, yb_ref, do_ref, *out_refs):
        _, pull = jax.vjp(_merge_fn, ga_ref[...], gb_ref[...], ya_ref[...], yb_ref[...])
        for ref, val in zip(out_refs, pull(do_ref[...])):
            ref[...] = val.astype(ref.dtype)

    shape = jax.ShapeDtypeStruct(ya.shape, MXU_DTYPE)
    return pl.pallas_call(
        body, name=name, grid=(n_tok // tm,),
        in_specs=[rows, pl.BlockSpec((tm, D_MODEL), lambda i: (i, 1)), rows, rows, rows],
        out_specs=(rows, rows, rows, rows), out_shape=(shape, shape, shape, shape),
        compiler_params=_params(7 * _nbytes((tm, D_MODEL), F32), 6 * _nbytes((tm, D_MODEL), F32)),
    )(*_hbm(gab, gab, ya, yb, dout))


ADA_SHARD = N_MOD * D_MODEL // N_SHARD


def _ada_fwd(c_pad, w_s, b_s, *, name):
    n_r = c_pad.shape[0]

    def body(c_ref, w_ref, b_ref, o_ref):
        sc = jax.nn.silu(c_ref[...]).astype(MXU_DTYPE)
        o_ref[0] = lax.dot_general(sc, w_ref[0].astype(MXU_DTYPE), _NN, preferred_element_type=F32) + b_ref[0]

    return pl.pallas_call(
        body, name=name, grid=(N_SHARD,),
        in_specs=[pl.BlockSpec((n_r, D_MODEL), lambda s: (0, 0)),
                  pl.BlockSpec((1, D_MODEL, ADA_SHARD), lambda s: (s, 0, 0)),
                  pl.BlockSpec((1, 1, ADA_SHARD), lambda s: (s, 0, 0))],
        out_specs=pl.BlockSpec((1, n_r, ADA_SHARD), lambda s: (s, 0, 0)),
        out_shape=jax.ShapeDtypeStruct((N_SHARD, n_r, ADA_SHARD), F32),
        compiler_params=_params(_nbytes((D_MODEL, ADA_SHARD), w_s.dtype), 1 << 20),
    )(*_hbm(c_pad, w_s, b_s))


def _ada_bwd(c_pad, dmod_s, *, name):
    n_r = c_pad.shape[0]

    def body(c_ref, d_ref, dw_ref, db_ref):
        sc = jax.nn.silu(c_ref[...])
        dm = d_ref[0]
        dw_ref[0] = _fdot(sc, dm, _TN)
        db_ref[0] = jnp.sum(dm, axis=0, keepdims=True)

    return pl.pallas_call(
        body, name=name, grid=(N_SHARD,),
        in_specs=[pl.BlockSpec((n_r, D_MODEL), lambda s: (0, 0)), pl.BlockSpec((1, n_r, ADA_SHARD), lambda s: (s, 0, 0))],
        out_specs=(pl.BlockSpec((1, D_MODEL, ADA_SHARD), lambda s: (s, 0, 0)),
                   pl.BlockSpec((1, 1, ADA_SHARD), lambda s: (s, 0, 0))),
        out_shape=(jax.ShapeDtypeStruct((N_SHARD, D_MODEL, ADA_SHARD), F32),
                   jax.ShapeDtypeStruct((N_SHARD, 1, ADA_SHARD), F32)),
        compiler_params=_params(_nbytes((D_MODEL, ADA_SHARD), F32), 2 * _nbytes((D_MODEL, ADA_SHARD), F32)),
    )(*_hbm(c_pad, dmod_s))


def _block_diag(blocks):
    n_per = S5_GROUPS // S5_BLOCKS
    _, n_r, n_c = blocks.shape
    b4 = blocks.reshape(S5_BLOCKS, n_per, n_r, n_c)
    eye = jnp.eye(n_per, dtype=blocks.dtype)
    return (b4[:, :, :, None, :] * eye[None, :, None, :, None]).reshape(S5_BLOCKS, n_per * n_r, n_per * n_c)


def _diag_blocks(mat, n_r, n_c):
    n_per = S5_GROUPS // S5_BLOCKS
    m5 = mat.reshape(S5_BLOCKS, n_per, n_r, n_per, n_c)
    eye = jnp.eye(n_per, dtype=mat.dtype)
    return jnp.sum(m5 * eye[None, :, None, :, None], axis=3).reshape(S5_GROUPS, n_r, n_c)


def _local_step(x, c, target, wts):
    n_b, n_s, _ = x.shape
    n_tok = n_b * n_s
    flat = lambda t: t.reshape(n_tok, t.shape[-1])
    unflat = lambda t: t.reshape(n_b, n_s, t.shape[-1])
    n_chunks = n_s // CHUNK

    c_pad = jnp.zeros((SUBLANES, D_MODEL), F32).at[:n_b].set(c)
    mod_s = _ada_fwd(c_pad, wts["w_ada"], wts["b_ada"], name="ada_fwd")
    mod = mod_s.transpose(1, 0, 2).reshape(SUBLANES, N_MOD * D_MODEL)[:n_b]
    sh1, sc1, gt1, sh2, sc2, gt2, sh3, sc3, gt3 = [m[:, None, :] for m in jnp.split(mod, N_MOD, axis=-1)]

    a1 = _pre(x, None, None, wts["g_ffn1"], sh1, sc1, 0.0, name="pre1")
    f1, ffn1_saved = _ffn_fwd(flat(a1), wts["w1_ffn1"], wts["w3_ffn1"], wts["w2_ffn1"], "ffn1")
    x1, a2 = _pre(x, unflat(f1), gt1, wts["g_mix"], sh2, sc2, 0.5, name="pre2")
    u = flat(a2)[None]
    p_qkv = _mm_act([(u, wts["w_qkv"])], "nt", name="in_qkv")[0]
    p_z = _mm_act([(u, wts["w_z"])], "nt", name="in_z")[0]
    p_gab = _mm_act([(u, wts["w_gab"])], "nt", name="in_gab")[0]
    p_s5 = _mm_act([(u, wts["w_s5"])], "nt", name="in_s5")[0]
    p_ba = _mm_act([(u, wts["w_ba"])], "nt", name="in_ba")[0]

    qkv_c = _conv_fwd(unflat(p_qkv), wts["conv_qkv"], name="conv_fwd")
    z_tok = unflat(p_z)
    ba = p_ba.reshape(n_b, n_s, BA_PAD)
    head_rows = lambda t: t.transpose(0, 2, 1).reshape(n_b, DN_HEADS, n_chunks, 1, CHUNK)
    blr = head_rows(ba[:, :, :DN_HEADS])
    lar = head_rows(ba[:, :, DN_HEADS:2 * DN_HEADS])
    a_log, dt_bias = wts["a_log"], wts["dt_bias"]
    dn_in = (qkv_c, blr, lar, a_log, dt_bias)
    qd, kd, uc, wc, attn, g_last, dn_inv = _dn_prep(*dn_in, name="dn_prep")
    o, states = _dn_scan(qd, kd, uc, wc, attn, g_last, name="dn_scan")
    og = _dn_post(o, z_tok, wts["g_onorm"], name="dn_post")
    og_t = og.reshape(1, n_tok, DN_WIDTH)
    ya = _mm_act([(og_t, wts["w_proj_a"])], "nn", name="proj_a")[0]

    s5p_in = (wts["lam_re"], wts["lam_im"], wts["log_step"], wts["bt_re"], wts["bt_im"], wts["c_im"])
    lb_re, lb_im, bb_re, bb_im, c_neg = _s5_prep(*s5p_in, name="s5_prep")
    wb_re, wb_im = _block_diag(bb_re), _block_diag(bb_im)
    wc_re = _block_diag(wts["c_re"].transpose(0, 2, 1))
    wc_im = _block_diag(c_neg.transpose(0, 2, 1))
    lbr, lbi = lb_re.reshape(1, S5_LANES), lb_im.reshape(1, S5_LANES)
    s5_w = (wb_re, wb_im, lbr, lbi, wc_re, wc_im)
    ymm, x_re, x_im = _s5_mix(unflat(p_s5), *s5_w, name="s5_mix")
    ymm = ymm.reshape(n_tok, S5_WIDTH)
    y2 = _s5_out(ymm, p_s5, wts["d_skip"], wts["w_glu"], wts["b_glu"], name="s5_out")
    yb = _mm_act([(y2[None], wts["w_proj_b"])], "nn", name="proj_b")[0]

    merged = _merge(p_gab, ya, yb, name="merge")
    m_out = _mm_act([(merged[None], wts["w_out"])], "nn", name="mix_out")[0]
    x2, a3 = _pre(x1, unflat(m_out), gt2, wts["g_ffn2"], sh3, sc3, 1.0, name="pre3")
    f3, ffn2_saved = _ffn_fwd(flat(a3), wts["w1_ffn2"], wts["w3_ffn2"], wts["w2_ffn2"], "ffn2")

    g = {}
    loss, dx2_res, df3, dgt3, g["g_final"] = _final(x2, unflat(f3), gt3, wts["g_final"], target, name="final")
    da3, g["w1_ffn2"], g["w3_ffn2"], g["w2_ffn2"] = _ffn_bwd(
        flat(a3), wts["w1_ffn2"], wts["w3_ffn2"], wts["w2_ffn2"], ffn2_saved, flat(df3), "ffn2")
    dx1_res, dm_out, dgt2, g["g_ffn2"], dsh3, dsc3 = _pre_bwd(
        x1, unflat(m_out), gt2, wts["g_ffn2"], sh3, sc3, 1.0, unflat(da3), dx2_res, name="pre3_bwd")
    dm_out = flat(dm_out)[None]
    dmerged = _mm_act([(dm_out, wts["w_out"])], "nt", name="mix_out_bwd")[0]
    g["w_out"] = _mm_tn(merged[None], dm_out, name="dw_out")[0]
    dga, dgb, dya, dyb = _merge_bwd(p_gab, ya, yb, dmerged, name="merge_bwd")

    dy2 = _mm_act([(dyb[None], wts["w_proj_b"])], "nt", name="proj_b_bwd")[0]
    g["w_proj_b"] = _mm_tn(y2[None], dyb[None], name="dw_proj_b")[0]
    dymm, du_skip, g["d_skip"], g["w_glu"], g["b_glu"] = _s5_out_bwd(
        ymm, p_s5, wts["d_skip"], wts["w_glu"], wts["b_glu"], dy2, name="s5_out_bwd")
    dp_s5, dwb_re, dwb_im, dlb_re, dlb_im, dwc_re, dwc_im = _s5_mix_bwd(
        unflat(dymm), unflat(du_skip), unflat(p_s5), x_re, x_im, *s5_w, name="s5_mix_bwd")
    dp_s5 = dp_s5.reshape(n_tok, S5_WIDTH)
    g["c_re"] = _diag_blocks(dwc_re, S5_STATE, S5_GROUP_CH).transpose(0, 2, 1)
    s5_cts = (dlb_re.reshape(lb_re.shape), dlb_im.reshape(lb_im.shape),
              _diag_blocks(dwb_re, S5_GROUP_CH, S5_STATE), _diag_blocks(dwb_im, S5_GROUP_CH, S5_STATE),
              _diag_blocks(dwc_im, S5_STATE, S5_GROUP_CH).transpose(0, 2, 1))
    g["lam_re"], g["lam_im"], g["log_step"], g["bt_re"], g["bt_im"], g["c_im"] = _s5_prep_bwd(
        *s5p_in, s5_cts, name="s5_prep_bwd")

    dog = _mm_act([(dya[None], wts["w_proj_a"])], "nt", name="proj_a_bwd")[0]
    g["w_proj_a"] = _mm_tn(og_t, dya[None], name="dw_proj_a")[0]
    do, dz, g["g_onorm"] = _dn_post_bwd(o, z_tok, wts["g_onorm"], unflat(dog), name="dn_post_bwd")
    scan_cts = _dn_scan_bwd(qd, kd, uc, wc, attn, g_last, states, do, name="dn_scan_bwd")
    dqkv_c, dblr, dlar, g["a_log"], g["dt_bias"] = _dn_prep_bwd(*dn_in, dn_inv, uc, wc, scan_cts, name="dn_prep_bwd")
    dqkv, g["conv_qkv"] = _conv_bwd(unflat(p_qkv), wts["conv_qkv"], dqkv_c, name="conv_bwd")
    token_cols = lambda t: t.reshape(n_b, DN_HEADS, n_s).transpose(0, 2, 1)
    dba = jnp.concatenate([token_cols(dblr), token_cols(dlar),
                           jnp.zeros((n_b, n_s, BA_PAD - 2 * DN_HEADS), F32)], axis=-1).astype(MXU_DTYPE)

    dps = {"w_qkv": flat(dqkv)[None], "w_z": flat(dz)[None], "w_ga": dga[None], "w_gb": dgb[None],
           "w_s5": dp_s5[None], "w_ba": flat(dba)[None]}
    w_ga, w_gb = wts["w_gab"][:, :D_MODEL], wts["w_gab"][:, D_MODEL:]
    w_of = dict(wts, w_ga=w_ga, w_gb=w_gb)
    du = _mm_act([(dps[k], w_of[k]) for k in dps], "nn", name="in_bwd")[0]
    for k in dps:
        g[k] = _mm_tn(dps[k], u, name=f"d{k}")[0]
    dx0_res, df1, dgt1, g["g_mix"], dsh2, dsc2 = _pre_bwd(
        x, unflat(f1), gt1, wts["g_mix"], sh2, sc2, 0.5, unflat(du), dx1_res, name="pre2_bwd")
    da1, g["w1_ffn1"], g["w3_ffn1"], g["w2_ffn1"] = _ffn_bwd(
        flat(a1), wts["w1_ffn1"], wts["w3_ffn1"], wts["w2_ffn1"], ffn1_saved, flat(df1), "ffn1")
    grad_x, g["g_ffn1"], dsh1, dsc1 = _pre_bwd(
        x, None, None, wts["g_ffn1"], sh1, sc1, 0.0, unflat(da1), dx0_res, name="pre1_bwd")

    dmod = jnp.concatenate([t[:, 0, :] for t in (dsh1, dsc1, dgt1, dsh2, dsc2, dgt2, dsh3, dsc3, dgt3)], axis=-1)
    return loss, grad_x, g, dmod


def _ada_grads(c_rows, dmod_rows):
    n_r = c_rows.shape[0]
    n_pad = -n_r % SUBLANES
    c_pad = jnp.pad(c_rows, ((0, n_pad), (0, 0)))
    dmod_s = jnp.pad(dmod_rows, ((0, n_pad), (0, 0))).reshape(n_r + n_pad, N_SHARD, ADA_SHARD).transpose(1, 0, 2)
    dw, db = _ada_bwd(c_pad, dmod_s, name="ada_bwd")
    return dw, db.reshape(1, N_MOD * D_MODEL)


IN_SPLITS = (("w_qkv", 3 * DN_WIDTH), ("w_z", DN_WIDTH), ("w_ba", 2 * DN_HEADS), ("w_s5", S5_WIDTH),
             ("w_ga", D_MODEL), ("w_gb", D_MODEL))
SHARDED = ("w_ada", "w1_ffn1", "w3_ffn1", "w2_ffn1", "w_in", "conv_qkv", "w_glu", "w_proj_a", "w_proj_b", "w_out",
           "w1_ffn2", "w3_ffn2", "w2_ffn2")


def _cat_columns(stack):
    return stack.transpose(1, 0, 2).reshape(stack.shape[1], N_SHARD * stack.shape[2])


def _split_columns(full):
    n_r, n_c = full.shape
    return full.reshape(n_r, N_SHARD, n_c // N_SHARD).transpose(1, 0, 2)


def _gathered_weights(st, rep):
    w = {k: st[k] for k in ("w_ada", "w1_ffn1", "w3_ffn1", "w2_ffn1", "w1_ffn2", "w3_ffn2", "w2_ffn2")}
    w["b_ada"] = rep["b_ada"].reshape(N_SHARD, 1, ADA_SHARD)
    for k in ("g_ffn1", "g_mix", "g_ffn2", "g_final"):
        w[k] = rep[k].reshape(1, D_MODEL)
    w_in_t = st["w_in"].reshape(N_SHARD * st["w_in"].shape[1], D_MODEL)
    start = 0
    for k, size in IN_SPLITS:
        w[k] = w_in_t[None, start:start + size]
        start += size
    w["w_gab"] = jnp.concatenate([w.pop("w_ga"), w.pop("w_gb")], axis=1)
    w["w_ba"] = jnp.pad(w["w_ba"], ((0, 0), (0, BA_PAD - 2 * DN_HEADS), (0, 0)))
    w["conv_qkv"] = _cat_columns(st["conv_qkv"])
    w["a_log"] = rep["a_log"].reshape(DN_HEADS, 1, 1)
    w["dt_bias"] = rep["dt_bias"].reshape(DN_HEADS, 1, 1)
    w["g_onorm"] = rep["g_onorm"].reshape(1, DN_HEAD_DIM)
    w["lam_re"] = rep["lam_re"].reshape(S5_GROUPS, 1, S5_STATE)
    w["lam_im"] = rep["lam_im"].reshape(S5_GROUPS, 1, S5_STATE)
    w["log_step"] = rep["log_step"].reshape(S5_GROUPS, 1, 1)
    w["bt_re"] = rep["b_re"][0].transpose(0, 2, 1)
    w["bt_im"] = rep["b_im"][0].transpose(0, 2, 1)
    w["c_re"] = rep["c_re"][0]
    w["c_im"] = rep["c_im"][0]
    w["d_skip"] = rep["d_skip"].reshape(1, S5_WIDTH)
    w["b_glu"] = rep["b_glu"].reshape(1, S5_WIDTH)
    w["w_glu"] = st["w_glu"].reshape(S5_WIDTH, S5_WIDTH)
    w["w_proj_a"] = _cat_columns(st["w_proj_a"])[None]
    w["w_proj_b"] = _cat_columns(st["w_proj_b"])[None]
    w["w_out"] = st["w_out"].reshape(1, D_MODEL, D_MODEL)
    return w


def _grads_to_problem_layout(g):
    st = {k: g[k] for k in ("w1_ffn1", "w3_ffn1", "w2_ffn1", "w1_ffn2", "w3_ffn2", "w2_ffn2")}
    w_in_t = jnp.concatenate([g[k][:size] for k, size in IN_SPLITS], axis=0)
    st["w_in"] = w_in_t.reshape(N_SHARD, w_in_t.shape[0] // N_SHARD, D_MODEL)
    st["w_glu"] = g["w_glu"].reshape(N_SHARD, S5_WIDTH // N_SHARD, S5_WIDTH)
    st["w_proj_a"] = _split_columns(g["w_proj_a"])
    st["w_proj_b"] = _split_columns(g["w_proj_b"])
    st["w_out"] = g["w_out"].reshape(N_SHARD, D_MODEL // N_SHARD, D_MODEL)
    small = {
        "g_ffn1": g["g_ffn1"], "g_mix": g["g_mix"], "g_ffn2": g["g_ffn2"], "g_final": g["g_final"].reshape(D_MODEL),
        "conv_qkv": g["conv_qkv"][None],
        "a_log": g["a_log"].reshape(1, DN_HEADS), "dt_bias": g["dt_bias"].reshape(1, DN_HEADS),
        "g_onorm": g["g_onorm"],
        "lam_re": g["lam_re"].reshape(1, S5_GROUPS, S5_STATE), "lam_im": g["lam_im"].reshape(1, S5_GROUPS, S5_STATE),
        "log_step": g["log_step"].reshape(1, S5_GROUPS),
        "b_re": g["bt_re"].transpose(0, 2, 1)[None], "b_im": g["bt_im"].transpose(0, 2, 1)[None],
        "c_re": g["c_re"][None], "c_im": g["c_im"][None],
        "d_skip": g["d_skip"], "b_glu": g["b_glu"],
    }
    return st, small


ELEMENTWISE_BLOCK_BYTES = 1 << 20


def _row_tile(n_rows, n_cols, n_lead=1, multiple=SUBLANES):
    best = None
    for t in range(multiple, n_rows + 1, multiple):
        if n_rows % t == 0 and n_lead * t * n_cols * 4 <= ELEMENTWISE_BLOCK_BYTES:
            best = t
    return best if best is not None else n_rows


def _add_sibling_half(g4, recv, my_c, *, name):
    n_sh, _, n_h, n_c = g4.shape
    th = _row_tile(n_h, n_c, multiple=2 * SUBLANES)

    def body(c_ref, g_ref, r_ref, o_ref):
        o_ref[0] = (g_ref[0, 0] + r_ref[0]).astype(o_ref.dtype)

    grid_spec = pltpu.PrefetchScalarGridSpec(
        num_scalar_prefetch=1, grid=(n_sh, n_h // th),
        in_specs=[pl.BlockSpec((1, 1, th, n_c), lambda s, i, c_ref: (s, c_ref[0], i, 0)),
                  pl.BlockSpec((1, th, n_c), lambda s, i, c_ref: (s, i, 0))],
        out_specs=pl.BlockSpec((1, th, n_c), lambda s, i, c_ref: (s, i, 0)))
    return pl.pallas_call(
        body, name=name, grid_spec=grid_spec, out_shape=jax.ShapeDtypeStruct((n_sh, n_h, n_c), MXU_DTYPE),
        compiler_params=_params(3 * _nbytes((th, n_c), F32)),
    )(*_hbm(my_c, g4, recv))


def _sum_slots(parts, *, name):
    n_p, n_r, n_c = parts.shape
    th = _row_tile(n_r, n_c, n_p)

    def body(p_ref, o_ref):
        total = p_ref[0].astype(F32)
        for k in range(1, n_p):
            total = total + p_ref[k].astype(F32)
        o_ref[...] = total

    return pl.pallas_call(
        body, name=name, grid=(n_r // th,),
        in_specs=[pl.BlockSpec((n_p, th, n_c), lambda i: (0, i, 0))],
        out_specs=pl.BlockSpec((th, n_c), lambda i: (i, 0)),
        out_shape=jax.ShapeDtypeStruct((n_r, n_c), F32),
        compiler_params=_params((n_p + 1) * _nbytes((th, n_c), F32)),
    )(*_hbm(parts))


def _cast_into_slot(w, place, dtype, *, name):
    n_r, n_c = w.shape
    th = _row_tile(n_r, n_c, multiple=2 * SUBLANES)

    def body(p_ref, w_ref, o_ref):
        o_ref[0] = w_ref[...].astype(o_ref.dtype)

    grid_spec = pltpu.PrefetchScalarGridSpec(
        num_scalar_prefetch=1, grid=(n_r // th,),
        in_specs=[pl.BlockSpec((th, n_c), lambda i, p: (i, 0))],
        out_specs=pl.BlockSpec((1, th, n_c), lambda i, p: (p[1], i, 0)))
    return pl.pallas_call(
        body, name=name, grid_spec=grid_spec, out_shape=jax.ShapeDtypeStruct((N_SHARD, n_r, n_c), dtype),
        compiler_params=_params(2 * _nbytes((th, n_c), F32)),
    )(*_hbm(place, w))


def _sum_chips(own, parts, place, *, name):
    n_sh, n_h, n_c = own.shape
    th = _row_tile(n_h, n_c, n_sh, multiple=2 * SUBLANES)

    def body(p_ref, own_ref, a_ref, b_ref, c_ref, o_ref):
        o_ref[0] = ((own_ref[0].astype(F32) + a_ref[0].astype(F32)) + b_ref[0].astype(F32)) + c_ref[0].astype(F32)

    slab = lambda k: pl.BlockSpec((1, th, n_c), lambda i, p, k=k: (p[k], i, 0))
    grid_spec = pltpu.PrefetchScalarGridSpec(
        num_scalar_prefetch=1, grid=(n_h // th,),
        in_specs=[slab(1), slab(2), slab(3), slab(4)], out_specs=slab(0))
    return pl.pallas_call(
        body, name=name, grid_spec=grid_spec, out_shape=jax.ShapeDtypeStruct((2, n_h, n_c), F32),
        compiler_params=_params(5 * _nbytes((th, n_c), F32)),
    )(*_hbm(place, own, parts, parts, parts))


def _adamw(w, g, m, v, *, name):
    n_r, n_c = w.shape
    th = _row_tile(n_r, n_c)
    tc = n_c
    if th == n_r and n_c % LANES == 0:
        tc = max(t for t in range(LANES, n_c + 1, LANES) if n_c % t == 0 and (n_r * t * 4 <= ELEMENTWISE_BLOCK_BYTES or t == LANES))
    bias1 = 1.0 - ADAM_B1 ** ADAM_STEP
    bias2 = 1.0 - ADAM_B2 ** ADAM_STEP

    def body(w_ref, g_ref, m_ref, v_ref, d_ref, mo_ref, vo_ref):
        gv = g_ref[...]
        m_new = ADAM_B1 * m_ref[...] + (1.0 - ADAM_B1) * gv
        v_new = ADAM_B2 * v_ref[...] + (1.0 - ADAM_B2) * jnp.square(gv)
        d_ref[...] = -ADAM_LR * ((m_new / bias1) / (jnp.sqrt(v_new / bias2) + ADAM_EPS) + ADAM_WD * w_ref[...])
        mo_ref[...] = m_new
        vo_ref[...] = v_new

    spec = pl.BlockSpec((th, tc), lambda i, j: (i, j))
    shape = jax.ShapeDtypeStruct((n_r, n_c), F32)
    return pl.pallas_call(
        body, name=name, grid=(n_r // th, n_c // tc), in_specs=[spec] * 4, out_specs=(spec,) * 3, out_shape=(shape,) * 3,
        compiler_params=_params(7 * _nbytes((th, tc), F32)),
    )(*_hbm(w, g, m, v))


CHIP_FLIPS = ((1, 0), (0, 1), (1, 1))
DEVICE_FLIPS = tuple((fx, fy, fc) for fx in (0, 1) for fy in (0, 1) for fc in (0, 1))[1:]


def _exchange(ins, out_shapes, plan, n_local, n_remote, *, name, aliased=False):
    n_in, n_out = len(ins), len(out_shapes)

    def body(*refs):
        in_refs, out_refs = refs[:n_in], refs[n_in:n_in + n_out]
        send_sems, recv_sems, local_sems = refs[n_in + n_out:]
        me = (lax.axis_index("x"), lax.axis_index("y"), lax.axis_index("c"))
        local, remote = plan(in_refs, out_refs, me)
        assert len(local) == n_local and len(remote) == n_remote
        here = [pltpu.make_async_copy(src, dst, local_sems.at[i]) for i, (src, dst) in enumerate(local)]
        for cp in here:
            cp.start()
        sends = [pltpu.make_async_remote_copy(src_ref=src, dst_ref=dst, send_sem=send_sems.at[i], recv_sem=recv_sems.at[i],
                                              device_id=peer, device_id_type=pl.DeviceIdType.MESH)
                 for i, (src, dst, _, peer) in enumerate(remote)]
        for cp in sends:
            cp.start()
        for i, (src, _, landing, peer) in enumerate(remote):
            pltpu.make_async_remote_copy(src_ref=src, dst_ref=landing, send_sem=send_sems.at[i], recv_sem=recv_sems.at[i],
                                         device_id=peer, device_id_type=pl.DeviceIdType.MESH).wait_recv()
        for cp in sends:
            cp.wait_send()
        for cp in here:
            cp.wait()

    any_spec = pl.BlockSpec(memory_space=pl.ANY)
    return pl.pallas_call(
        body, name=name, in_specs=[any_spec] * n_in, out_specs=tuple([any_spec] * n_out), out_shape=tuple(out_shapes),
        scratch_shapes=[pltpu.SemaphoreType.DMA((n_remote,)), pltpu.SemaphoreType.DMA((n_remote,)),
                        pltpu.SemaphoreType.DMA((max(n_local, 1),))],
        input_output_aliases={k: k for k in range(n_in)} if aliased else {},
    )(*ins)


def _gather_shards(stacks, *, name):
    n = len(stacks)
    halved = [a.shape[1] % 64 == 0 for a in stacks]
    unit_rows = [a.shape[1] // 2 if h else a.shape[1] for a, h in zip(stacks, halved)]
    part1_rows = [(r // 32) * 16 if r >= 32 else r for r in unit_rows]
    has_part2 = [p < r for p, r in zip(part1_rows, unit_rows)]
    n_sem = sum(2 + 1 + int(h2) + 3 * int(h) for h2, h in zip(has_part2, halved))

    def body(*refs):
        outs = refs[n:2 * n]
        send_sems, recv_sems = refs[2 * n:]
        x, y, c = lax.axis_index("x"), lax.axis_index("y"), lax.axis_index("c")
        mine, chip_x, chip_y, chip_d = 2 * x + y, 2 * (1 - x) + y, 2 * x + (1 - y), 2 * (1 - x) + (1 - y)
        to_x, to_y, sibling = (1 - x, y, c), (x, 1 - y, c), (x, y, 1 - c)

        def region(k, slot, half, part=None):
            start = half * unit_rows[k] if halved[k] else 0
            size = unit_rows[k]
            if part == 1:
                size = part1_rows[k]
            elif part == 2:
                start, size = start + part1_rows[k], unit_rows[k] - part1_rows[k]
            if not halved[k] and part is None:
                return outs[k].at[slot]
            if halved[k]:
                start = pl.multiple_of(start, 16)
            return outs[k].at[slot, pl.ds(start, size)]

        counter = [0]
        started, pending = [], []

        def send(region_of, peer, landing_of):
            i = counter[0]
            counter[0] += 1
            src = region_of
            cp = pltpu.make_async_remote_copy(src_ref=src, dst_ref=src, send_sem=send_sems.at[i], recv_sem=recv_sems.at[i],
                                              device_id=peer, device_id_type=pl.DeviceIdType.MESH)
            cp.start()
            started.append(cp)
            return pltpu.make_async_remote_copy(src_ref=landing_of, dst_ref=landing_of, send_sem=send_sems.at[i],
                                                recv_sem=recv_sems.at[i], device_id=peer, device_id_type=pl.DeviceIdType.MESH)

        from_x = [send(region(k, mine, c), to_x, region(k, chip_x, c)) for k in range(n)]
        from_y = [send(region(k, mine, c), to_y, region(k, chip_y, c)) for k in range(n)]
        diag = []
        for k in range(n):
            from_x[k].wait_recv()
            fwd = [send(region(k, chip_x, c, 1), to_y, region(k, chip_d, c, 1))]
            if halved[k]:
                pending.append(send(region(k, chip_x, c), sibling, region(k, chip_x, 1 - c)))
            from_y[k].wait_recv()
            if has_part2[k]:
                fwd.append(send(region(k, chip_y, c, 2), to_x, region(k, chip_d, c, 2)))
            if halved[k]:
                pending.append(send(region(k, chip_y, c), sibling, region(k, chip_y, 1 - c)))
            diag.append(fwd)
        for k in range(n):
            for landed in diag[k]:
                landed.wait_recv()
            if halved[k]:
                pending.append(send(region(k, chip_d, c), sibling, region(k, chip_d, 1 - c)))
        for landed in pending:
            landed.wait_recv()
        for cp in started:
            cp.wait_send()
        assert counter[0] == n_sem

    any_spec = pl.BlockSpec(memory_space=pl.ANY)
    return pl.pallas_call(
        body, name=name, in_specs=[any_spec] * n, out_specs=tuple([any_spec] * n),
        out_shape=tuple(jax.ShapeDtypeStruct(a.shape, a.dtype) for a in stacks),
        scratch_shapes=[pltpu.SemaphoreType.DMA((n_sem,)), pltpu.SemaphoreType.DMA((n_sem,))],
        input_output_aliases={k: k for k in range(n)},
    )(*stacks)


def _swap_sibling_halves(g4s, *, name):
    n = len(g4s)

    def plan(in_refs, out_refs, me):
        x, y, c = me
        remote = [(in_refs[k].at[:, 1 - c], out_refs[k], out_refs[k], (x, y, 1 - c)) for k in range(n)]
        return [], remote

    shapes = [jax.ShapeDtypeStruct((a.shape[0],) + a.shape[2:], a.dtype) for a in g4s]
    return _exchange(g4s, shapes, plan, 0, n, name=name)


def _scatter_to_chips(hs, *, name):
    n = len(hs)

    def plan(in_refs, out_refs, me):
        x, y, c = me
        mine = 2 * x + y
        remote = []
        for fx, fy in CHIP_FLIPS:
            px, py = x ^ fx, y ^ fy
            peer = 2 * px + py
            for k in range(n):
                remote.append((in_refs[k].at[peer], out_refs[k].at[mine], out_refs[k].at[peer], (px, py, c)))
        return [], remote

    shapes = [jax.ShapeDtypeStruct(a.shape, a.dtype) for a in hs]
    return _exchange(hs, shapes, plan, 0, len(CHIP_FLIPS) * n, name=name)


def _join_sibling_halves(rs, *, name):
    n = len(rs)

    def plan(in_refs, out_refs, me):
        x, y, c = me
        remote = [(out_refs[k].at[c], out_refs[k].at[c], out_refs[k].at[1 - c], (x, y, 1 - c)) for k in range(n)]
        return [], remote

    shapes = [jax.ShapeDtypeStruct(a.shape, a.dtype) for a in rs]
    return _exchange(rs, shapes, plan, 0, n, name=name, aliased=True)


def _gather_all_devices(packed, *, name):
    def plan(in_refs, out_refs, me):
        x, y, c = me
        mine = 4 * x + 2 * y + c
        remote = []
        for fx, fy, fc in DEVICE_FLIPS:
            px, py, pc = x ^ fx, y ^ fy, c ^ fc
            remote.append((in_refs[0], out_refs[0].at[mine], out_refs[0].at[4 * px + 2 * py + pc], (px, py, pc)))
        return [(in_refs[0], out_refs[0].at[mine])], remote

    shape = jax.ShapeDtypeStruct((2 * N_SHARD,) + packed.shape, packed.dtype)
    return _exchange([packed], [shape], plan, 1, len(DEVICE_FLIPS), name=name)[0]


WEIGHT_NAMES = ("w_ada", "b_ada", "g_ffn1", "w1_ffn1", "w3_ffn1", "w2_ffn1", "g_mix", "w_in", "conv_qkv", "a_log",
                "dt_bias", "g_onorm", "lam_re", "lam_im", "log_step", "b_re", "b_im", "c_re", "c_im", "d_skip", "w_glu",
                "b_glu", "w_proj_a", "w_proj_b", "w_out", "g_ffn2", "w1_ffn2", "w3_ffn2", "w2_ffn2", "g_final")
LARGE = tuple(n for n in SHARDED if n != "conv_qkv")
SMALL = tuple(n for n in WEIGHT_NAMES if n not in LARGE)
REDUCED_LARGE = tuple(n for n in LARGE if n != "w_ada")
REDUCED_SMALL = tuple(n for n in SMALL if n != "b_ada")
PACK_ROW = SUBLANES * LANES


def _pack(arrays):
    flat = jnp.concatenate([a.reshape(-1) for a in arrays])
    n_pad = -flat.shape[0] % PACK_ROW
    return jnp.pad(flat, (0, n_pad)).reshape(-1, LANES)


def _unpack(packed, shapes):
    flat = packed.reshape(-1)
    out, start = [], 0
    for s in shapes:
        size = math.prod(s)
        out.append(flat[start:start + size].reshape(s))
        start += size
    return out


def _unpack_slots(gathered, shapes):
    flat = gathered.reshape(gathered.shape[0], -1)
    out, start = [], 0
    for s in shapes:
        size = math.prod(s)
        out.append(flat[:, start:start + size].reshape((gathered.shape[0],) + tuple(s)))
        start += size
    return out


TRANSPOSED = ("w1_ffn1", "w3_ffn1", "w1_ffn2", "w3_ffn2", "w_in")


def _to_internal(name, a):
    return jnp.swapaxes(a[0], 0, 1) if name in TRANSPOSED else a[0]


def _from_internal(name, a):
    return (jnp.swapaxes(a, 0, 1) if name in TRANSPOSED else a)[None]


def _step(x, c, target, weights, m_in, v_in):
    xi, yi, ci = lax.axis_index("x"), lax.axis_index("y"), lax.axis_index("c")
    my_chip = 2 * xi + yi

    others = [k + (k >= my_chip).astype(jnp.int32) for k in range(N_SHARD - 1)]
    place = jnp.stack([ci, my_chip] + others).astype(jnp.int32)

    slots = [_cast_into_slot(_to_internal(n, weights[n]), place, F32 if n == "conv_qkv" else MXU_DTYPE, name=f"cast_{n}")
             for n in SHARDED]
    stacks = dict(zip(SHARDED, _gather_shards(slots, name="gather_weights")))
    rep = {n: weights[n] for n in WEIGHT_NAMES if n not in SHARDED}
    loss, grad_x, g, dmod = _local_step(x, c, target, _gathered_weights(stacks, rep))
    g_stacks, g_small = _grads_to_problem_layout(g)

    g4s = [g_stacks[n].reshape(N_SHARD, 2, g_stacks[n].shape[1] // 2, g_stacks[n].shape[2]) for n in REDUCED_LARGE]
    from_sibling = _swap_sibling_halves(g4s, name="swap_sibling_halves")
    chip_sums = [_add_sibling_half(a, r, place, name=f"chip_sum_{n}") for n, a, r in zip(REDUCED_LARGE, g4s, from_sibling)]
    from_chips = _scatter_to_chips(chip_sums, name="scatter_to_chips")
    reduced = [_sum_chips(h, p, place, name=f"sum_chips_{n}") for n, h, p in zip(REDUCED_LARGE, chip_sums, from_chips)]
    joined = _join_sibling_halves(reduced, name="join_sibling_halves")
    grads_2d = {n: j.reshape(2 * j.shape[1], j.shape[2]) for n, j in zip(REDUCED_LARGE, joined)}
    grads = {n: _from_internal(n, a) for n, a in grads_2d.items()}

    summed_shapes = [g_small[n].shape for n in REDUCED_SMALL] + [(1, 1)]
    packed = _pack([g_small[n] for n in REDUCED_SMALL] + [loss, c, dmod])
    gathered = _gather_all_devices(packed, name="gather_small")
    *small_grads, loss_sum = _unpack(_sum_slots(gathered, name="sum_small"), summed_shapes)
    grads.update(zip(REDUCED_SMALL, small_grads))
    n_conv = weights["conv_qkv"].shape[-1]
    grads["conv_qkv"] = lax.dynamic_slice_in_dim(grads["conv_qkv"], my_chip * n_conv, n_conv, axis=2)
    n_dev = gathered.shape[0]
    rows_of = lambda t: t.reshape(n_dev * t.shape[1], t.shape[2])
    _, c_all, dmod_all = _unpack_slots(gathered, [(sum(math.prod(s) for s in summed_shapes),), c.shape, dmod.shape])
    dw_ada, grads["b_ada"] = _ada_grads(rows_of(c_all), rows_of(dmod_all))
    grads_2d["w_ada"] = lax.dynamic_index_in_dim(dw_ada, my_chip, axis=0, keepdims=False)
    grads["w_ada"] = grads_2d["w_ada"][None]

    delta, new_m, new_v = {}, {}, {}
    grads_2d["conv_qkv"] = grads["conv_qkv"][0]
    for n in LARGE + ("conv_qkv",):
        outs = _adamw(_to_internal(n, weights[n]), grads_2d[n], _to_internal(n, m_in[n]), _to_internal(n, v_in[n]),
                      name=f"adamw_{n}")
        delta[n], new_m[n], new_v[n] = [_from_internal(n, o) for o in outs]
    packed_names = tuple(n for n in SMALL if n != "conv_qkv")
    shapes = [weights[n].shape for n in packed_names]
    outs = _adamw(*[_pack([d[n] for n in packed_names]) for d in (weights, grads, m_in, v_in)], name="adamw_small")
    for d, o in zip((delta, new_m, new_v), outs):
        d.update(zip(packed_names, _unpack(o, shapes)))
    return (loss_sum.reshape(()), grad_x, *[grads[n] for n in WEIGHT_NAMES], *[delta[n] for n in WEIGHT_NAMES],
            *[new_m[n] for n in WEIGHT_NAMES], *[new_v[n] for n in WEIGHT_NAMES])


def kernel(x, c, w_ada, b_ada, g_ffn1, w1_ffn1, w3_ffn1, w2_ffn1, g_mix, w_in, conv_qkv, a_log, dt_bias, g_onorm, lam_re, lam_im, log_step, b_re, b_im, c_re, c_im, d_skip, w_glu, b_glu, w_proj_a, w_proj_b, w_out, g_ffn2, w1_ffn2, w3_ffn2, w2_ffn2, g_final, loss_target, m_w_ada, m_b_ada, m_g_ffn1, m_w1_ffn1, m_w3_ffn1, m_w2_ffn1, m_g_mix, m_w_in, m_conv_qkv, m_a_log, m_dt_bias, m_g_onorm, m_lam_re, m_lam_im, m_log_step, m_b_re, m_b_im, m_c_re, m_c_im, m_d_skip, m_w_glu, m_b_glu, m_w_proj_a, m_w_proj_b, m_w_out, m_g_ffn2, m_w1_ffn2, m_w3_ffn2, m_w2_ffn2, m_g_final, v_w_ada, v_b_ada, v_g_ffn1, v_w1_ffn1, v_w3_ffn1, v_w2_ffn1, v_g_mix, v_w_in, v_conv_qkv, v_a_log, v_dt_bias, v_g_onorm, v_lam_re, v_lam_im, v_log_step, v_b_re, v_b_im, v_c_re, v_c_im, v_d_skip, v_w_glu, v_b_glu, v_w_proj_a, v_w_proj_b, v_w_out, v_g_ffn2, v_w1_ffn2, v_w3_ffn2, v_w2_ffn2, v_g_final):
    w_vals = (w_ada, b_ada, g_ffn1, w1_ffn1, w3_ffn1, w2_ffn1, g_mix, w_in, conv_qkv, a_log, dt_bias, g_onorm, lam_re, lam_im, log_step, b_re, b_im, c_re, c_im, d_skip, w_glu, b_glu, w_proj_a, w_proj_b, w_out, g_ffn2, w1_ffn2, w3_ffn2, w2_ffn2, g_final)
    m_vals = (m_w_ada, m_b_ada, m_g_ffn1, m_w1_ffn1, m_w3_ffn1, m_w2_ffn1, m_g_mix, m_w_in, m_conv_qkv, m_a_log, m_dt_bias, m_g_onorm, m_lam_re, m_lam_im, m_log_step, m_b_re, m_b_im, m_c_re, m_c_im, m_d_skip, m_w_glu, m_b_glu, m_w_proj_a, m_w_proj_b, m_w_out, m_g_ffn2, m_w1_ffn2, m_w3_ffn2, m_w2_ffn2, m_g_final)
    v_vals = (v_w_ada, v_b_ada, v_g_ffn1, v_w1_ffn1, v_w3_ffn1, v_w2_ffn1, v_g_mix, v_w_in, v_conv_qkv, v_a_log, v_dt_bias, v_g_onorm, v_lam_re, v_lam_im, v_log_step, v_b_re, v_b_im, v_c_re, v_c_im, v_d_skip, v_w_glu, v_b_glu, v_w_proj_a, v_w_proj_b, v_w_out, v_g_ffn2, v_w1_ffn2, v_w3_ffn2, v_w2_ffn2, v_g_final)
    return _step(x, c, loss_target, dict(zip(WEIGHT_NAMES, w_vals)), dict(zip(WEIGHT_NAMES, m_vals)),
                 dict(zip(WEIGHT_NAMES, v_vals)))
```

```python
import functools
import math

import jax
import jax.numpy as jnp
from jax import lax
from jax.experimental import pallas as pl
from jax.experimental.pallas import tpu as pltpu

F32 = jnp.float32
BF16 = jnp.bfloat16
MXU_DTYPE = BF16

D_MODEL = 1024
D_FF = 2816
DN_HEADS = 8
DN_HEAD_DIM = 64
DN_WIDTH = DN_HEADS * DN_HEAD_DIM
CONV_WIDTH = 4
CHUNK = 64
S5_GROUP_CH = 16
S5_GROUPS = 32
S5_WIDTH = S5_GROUPS * S5_GROUP_CH
S5_STATE = 64
S5_LANES = S5_GROUPS * S5_STATE
N_MOD = 9
EPS = 1e-6
N_SHARD = 4
FF_SHARD = D_FF // N_SHARD
BA_PAD = 128

ADAM_LR = 0.001
ADAM_B1 = 0.9
ADAM_B2 = 0.999
ADAM_EPS = 1e-08
ADAM_WD = 0.01
ADAM_STEP = 10

VMEM_BYTES_V7X = 64 * 1024 * 1024
SUBLANES = 8
LANES = 128


def _params(block_bytes, extra_bytes=0):
    need = 2 * block_bytes + extra_bytes + (4 << 20)
    return pltpu.CompilerParams(vmem_limit_bytes=int(min(max(need, 16 << 20), VMEM_BYTES_V7X - (8 << 20))))


def _nbytes(shape, dtype):
    return math.prod(shape) * jnp.dtype(dtype).itemsize


HBM_OPERAND_BYTES = 1 << 20


def _hbm(*args):
    return [pltpu.with_memory_space_constraint(a, pltpu.HBM) if _nbytes(a.shape, a.dtype) >= HBM_OPERAND_BYTES else a
            for a in args]


_NN = (((1,), (0,)), ((), ()))
_NT = (((1,), (1,)), ((), ()))
_TN = (((0,), (0,)), ((), ()))


LHS_ROW_BYTES = 4096


def _mm_act(pairs, mode, *, name, reduce_shards=False, out_dtype=F32, tm=None):
    n_tok = pairs[0][0].shape[1]
    n_out = pairs[0][1].shape[2] if mode == "nn" else pairs[0][1].shape[1]
    if tm is None:
        row_bytes = sum(a.shape[2] * jnp.dtype(a.dtype).itemsize for a, _ in pairs)
        tm = 1024 if row_bytes <= LHS_ROW_BYTES else 512
    tm = min(tm, n_tok)
    tn = n_out if n_out <= 1536 else 1024
    assert n_tok % tm == 0 and n_out % tn == 0
    n_red = N_SHARD if reduce_shards else 1
    grid = (n_tok // tm, n_out // tn, n_red)
    dims = _NN if mode == "nn" else _NT
    shard_of = lambda n_sh: (lambda r: 0) if n_sh == 1 else (lambda r: r)

    in_specs, args, blk = [], [], 0
    for a, b in pairs:
        k_dim = a.shape[2]
        sa, sb = shard_of(a.shape[0]), shard_of(b.shape[0])
        in_specs.append(pl.BlockSpec((1, tm, k_dim), lambda i, j, r, sa=sa: (sa(r), i, 0)))
        if mode == "nn":
            assert b.shape[1] == k_dim
            in_specs.append(pl.BlockSpec((1, k_dim, tn), lambda i, j, r, sb=sb: (sb(r), 0, j)))
        else:
            assert b.shape[2] == k_dim
            in_specs.append(pl.BlockSpec((1, tn, k_dim), lambda i, j, r, sb=sb: (sb(r), j, 0)))
        args += [a, b]
        blk += _nbytes((tm, k_dim), a.dtype) + _nbytes((k_dim, tn), b.dtype)
    blk += _nbytes((tm, tn), out_dtype)
    n_pairs = len(pairs)

    def body(*refs):
        out_ref = refs[2 * n_pairs]
        acc = None
        for k in range(n_pairs):
            a = refs[2 * k][0].astype(MXU_DTYPE)
            b = refs[2 * k + 1][0].astype(MXU_DTYPE)
            d = lax.dot_general(a, b, dims, preferred_element_type=F32)
            acc = d if acc is None else acc + d

        if n_red == 1:
            out_ref[0] = acc.astype(out_dtype)
        else:
            acc_ref = refs[-1]
            r = pl.program_id(2)

            @pl.when(r == 0)
            def _():
                acc_ref[...] = acc

            @pl.when(r > 0)
            def _():
                acc_ref[...] += acc

            @pl.when(r == n_red - 1)
            def _():
                out_ref[0] = acc_ref[...].astype(out_dtype)

    return pl.pallas_call(
        body,
        name=name,
        grid=grid,
        in_specs=in_specs,
        out_specs=pl.BlockSpec((1, tm, tn), lambda i, j, r: (0, i, j)),
        out_shape=jax.ShapeDtypeStruct((1, n_tok, n_out), out_dtype),
        scratch_shapes=[pltpu.VMEM((tm, tn), F32)] if n_red > 1 else [],
        compiler_params=_params(blk, 3 * _nbytes((tm, tn), F32)),
    )(*_hbm(*args))


def _mm_tn(a, b, *, name, tt=1024):
    n_tok, k_dim = a.shape[1], a.shape[2]
    n_out = b.shape[2]
    tt = min(tt, n_tok)
    tk = k_dim if k_dim <= 1536 else 1024
    tn = n_out if n_out <= 1536 else 1024
    assert n_tok % tt == 0 and k_dim % tk == 0 and n_out % tn == 0
    n_so = max(a.shape[0], b.shape[0])
    sa = (lambda s: s) if a.shape[0] > 1 else (lambda s: 0)
    sb = (lambda s: s) if b.shape[0] > 1 else (lambda s: 0)
    grid = (n_so, k_dim // tk, n_out // tn, n_tok // tt)

    def body(a_ref, b_ref, out_ref):
        d = lax.dot_general(a_ref[0].astype(MXU_DTYPE), b_ref[0].astype(MXU_DTYPE), _TN, preferred_element_type=F32)
        t = pl.program_id(3)

        @pl.when(t == 0)
        def _():
            out_ref[0] = d

        @pl.when(t > 0)
        def _():
            out_ref[0] += d

    blk = _nbytes((tt, tk), a.dtype) + _nbytes((tt, tn), b.dtype) + _nbytes((tk, tn), F32)
    return pl.pallas_call(
        body,
        name=name,
        grid=grid,
        in_specs=[
            pl.BlockSpec((1, tt, tk), lambda s, ki, nj, t: (sa(s), t, ki)),
            pl.BlockSpec((1, tt, tn), lambda s, ki, nj, t: (sb(s), t, nj)),
        ],
        out_specs=pl.BlockSpec((1, tk, tn), lambda s, ki, nj, t: (s, ki, nj)),
        out_shape=jax.ShapeDtypeStruct((n_so, k_dim, n_out), F32),
        compiler_params=_params(blk, 2 * _nbytes((tk, tn), F32) + _nbytes((tt, tk), F32)),
    )(*_hbm(a, b))


@functools.partial(jax.custom_vjp, nondiff_argnums=(2,))
def _mdot(a, b, dims):
    return lax.dot_general(a.astype(MXU_DTYPE), b.astype(MXU_DTYPE), dims, preferred_element_type=F32)


def _mdot_fwd(a, b, dims):
    return _mdot(a, b, dims), (a, b)


def _mdot_bwd(dims, res, g):
    a, b = res
    (ca, cb), (ba, bb) = dims
    nb = len(ba)
    assert tuple(ba) == tuple(range(nb)) and tuple(bb) == tuple(range(nb)) and len(ca) == 1 and a.ndim == nb + 2
    batch = (tuple(range(nb)), tuple(range(nb)))
    ra, rb = nb, nb + 1
    a_free = (set(range(nb, nb + 2)) - set(ca)).pop()
    b_free = (set(range(nb, nb + 2)) - set(cb)).pop()
    if a_free < ca[0]:
        da = _mdot(g, b, (((rb,), (b_free,)), batch))
    else:
        da = _mdot(b, g, (((b_free,), (rb,)), batch))
    if b_free > cb[0]:
        db = _mdot(a, g, (((a_free,), (ra,)), batch))
    else:
        db = _mdot(g, a, (((ra,), (a_free,)), batch))
    return da.astype(a.dtype), db.astype(b.dtype)


_mdot.defvjp(_mdot_fwd, _mdot_bwd)


def _rms(x, gain):
    return x * lax.rsqrt(jnp.mean(x * x, axis=-1, keepdims=True) + EPS) * gain


def _pre_fn(coef, x_in, f, gate, gain, shift, scale):
    x_new = x_in if f is None else x_in + coef * gate * f
    return x_new, _rms(x_new, gain) * (1.0 + scale) + shift


def _row_spec(ts):
    return pl.BlockSpec((1, ts, D_MODEL), lambda b, j: (b, j, 0))


_BATCH_VEC = pl.BlockSpec((1, 1, D_MODEL), lambda b, j: (b, 0, 0))
_ONE_VEC = pl.BlockSpec((1, D_MODEL), lambda b, j: (0, 0))


def _pre(x_in, f, gate, gain, shift, scale, coef, *, name, ts=512):
    n_b, n_s, _ = x_in.shape
    ts = min(ts, n_s)
    has_res = f is not None

    def body(*refs):
        if has_res:
            x_ref, f_ref, gate_ref, gain_ref, sh_ref, sc_ref, xn_ref, a_ref = refs
            x_new, a = _pre_fn(coef, x_ref[0], f_ref[0], gate_ref[0], gain_ref[...], sh_ref[0], sc_ref[0])
            xn_ref[0] = x_new
        else:
            x_ref, gain_ref, sh_ref, sc_ref, a_ref = refs
            _, a = _pre_fn(coef, x_ref[0], None, None, gain_ref[...], sh_ref[0], sc_ref[0])
        a_ref[0] = a.astype(a_ref.dtype)

    row = _row_spec(ts)
    if has_res:
        args = (x_in, f, gate, gain, shift, scale)
        in_specs = [row, row, _BATCH_VEC, _ONE_VEC, _BATCH_VEC, _BATCH_VEC]
        out_specs = (row, row)
        out_shape = (jax.ShapeDtypeStruct(x_in.shape, F32), jax.ShapeDtypeStruct(x_in.shape, MXU_DTYPE))
    else:
        args = (x_in, gain, shift, scale)
        in_specs = [row, _ONE_VEC, _BATCH_VEC, _BATCH_VEC]
        out_specs = row
        out_shape = jax.ShapeDtypeStruct(x_in.shape, MXU_DTYPE)
    return pl.pallas_call(
        body, name=name, grid=(n_b, n_s // ts), in_specs=in_specs, out_specs=out_specs, out_shape=out_shape,
        compiler_params=_params(5 * _nbytes((ts, D_MODEL), F32), 4 * _nbytes((ts, D_MODEL), F32)),
    )(*_hbm(*args))


def _accumulate(ref, value, first):
    @pl.when(first)
    def _():
        ref[...] = value

    @pl.when(jnp.logical_not(first))
    def _():
        ref[...] += value


def _pre_bwd(x_in, f, gate, gain, shift, scale, coef, da, dx_up, *, name, ts=512):
    n_b, n_s, _ = x_in.shape
    ts = min(ts, n_s)
    has_res = f is not None
    has_up = dx_up is not None

    def body(*refs):
        refs = list(refs)
        x_ref = refs.pop(0)
        f_ref, gate_ref = (refs.pop(0), refs.pop(0)) if has_res else (None, None)
        gain_ref, sh_ref, sc_ref, da_ref = refs.pop(0), refs.pop(0), refs.pop(0), refs.pop(0)
        up_ref = refs.pop(0) if has_up else None
        dx_ref = refs.pop(0)
        df_ref, dgate_ref = (refs.pop(0), refs.pop(0)) if has_res else (None, None)
        dgain_ref, dsh_ref, dsc_ref = refs
        b, j = pl.program_id(0), pl.program_id(1)
        da_v = da_ref[0].astype(F32)
        up_v = up_ref[0] if has_up else jnp.zeros((ts, D_MODEL), F32)
        if has_res:
            fn = functools.partial(_pre_fn, coef)
            _, pull = jax.vjp(fn, x_ref[0], f_ref[0], gate_ref[0], gain_ref[...], sh_ref[0], sc_ref[0])
            dx, df, dgate, dgain, dsh, dsc = pull((up_v, da_v))
            df_ref[0] = df.astype(df_ref.dtype)
            _accumulate(dgate_ref, dgate[None], j == 0)
        else:
            fn = lambda x, g, sh, sc: _pre_fn(coef, x, None, None, g, sh, sc)
            _, pull = jax.vjp(fn, x_ref[0], gain_ref[...], sh_ref[0], sc_ref[0])
            dx, dgain, dsh, dsc = pull((up_v, da_v))
        dx_ref[0] = dx
        _accumulate(dgain_ref, dgain, jnp.logical_and(b == 0, j == 0))
        _accumulate(dsh_ref, dsh[None], j == 0)
        _accumulate(dsc_ref, dsc[None], j == 0)

    row = _row_spec(ts)
    args, in_specs = [x_in], [row]
    if has_res:
        args += [f, gate]
        in_specs += [row, _BATCH_VEC]
    args += [gain, shift, scale, da]
    in_specs += [_ONE_VEC, _BATCH_VEC, _BATCH_VEC, row]
    if has_up:
        args.append(dx_up)
        in_specs.append(row)
    vec = jax.ShapeDtypeStruct((n_b, 1, D_MODEL), F32)
    out_shape, out_specs = [jax.ShapeDtypeStruct(x_in.shape, F32)], [row]
    if has_res:
        out_shape += [jax.ShapeDtypeStruct(x_in.shape, MXU_DTYPE), vec]
        out_specs += [row, _BATCH_VEC]
    out_shape += [jax.ShapeDtypeStruct((1, D_MODEL), F32), vec, vec]
    out_specs += [_ONE_VEC, _BATCH_VEC, _BATCH_VEC]
    return pl.pallas_call(
        body, name=name, grid=(n_b, n_s // ts), in_specs=in_specs, out_specs=tuple(out_specs), out_shape=tuple(out_shape),
        compiler_params=_params(6 * _nbytes((ts, D_MODEL), F32), 8 * _nbytes((ts, D_MODEL), F32)),
    )(*_hbm(*args))


def _final_fn(x_in, f, gate, gain, target):
    x_new = x_in + 0.5 * gate * f
    err = jnp.square(_rms(x_new, gain) - target)
    return 0.5 * jnp.sum(jnp.mean(err, axis=-1))


def _final(x_in, f, gate, gain, target, *, name, ts=512):
    n_b, n_s, _ = x_in.shape
    ts = min(ts, n_s)

    def body(x_ref, f_ref, gate_ref, gain_ref, t_ref, loss_ref, dx_ref, df_ref, dgate_ref, dgain_ref):
        b, j = pl.program_id(0), pl.program_id(1)
        loss, (dx, df, dgate, dgain) = jax.value_and_grad(_final_fn, argnums=(0, 1, 2, 3))(
            x_ref[0], f_ref[0], gate_ref[0], gain_ref[...], t_ref[0])
        first = jnp.logical_and(b == 0, j == 0)
        _accumulate(loss_ref, jnp.reshape(loss, (1, 1)), first)
        dx_ref[0] = dx
        df_ref[0] = df.astype(df_ref.dtype)
        _accumulate(dgate_ref, dgate[None], j == 0)
        _accumulate(dgain_ref, dgain, first)

    row = _row_spec(ts)
    return pl.pallas_call(
        body, name=name, grid=(n_b, n_s // ts),
        in_specs=[row, row, _BATCH_VEC, _ONE_VEC, row],
        out_specs=(pl.BlockSpec((1, 1), lambda b, j: (0, 0)), row, row, _BATCH_VEC, _ONE_VEC),
        out_shape=(jax.ShapeDtypeStruct((1, 1), F32), jax.ShapeDtypeStruct(x_in.shape, F32),
                   jax.ShapeDtypeStruct(x_in.shape, MXU_DTYPE), jax.ShapeDtypeStruct((n_b, 1, D_MODEL), F32),
                   jax.ShapeDtypeStruct((1, D_MODEL), F32)),
        compiler_params=_params(5 * _nbytes((ts, D_MODEL), F32), 8 * _nbytes((ts, D_MODEL), F32)),
    )(*_hbm(x_in, f, gate, gain, target))


FFN_TOKENS = 1024


def _ffn_up(a, w1s, w3s, *, name, tm=FFN_TOKENS):
    n_tok = a.shape[0]
    tm = min(tm, n_tok)

    def body(a_ref, w1_ref, w3_ref, h1_ref, h3_ref, g_ref):
        av = a_ref[...].astype(MXU_DTYPE)
        h1 = lax.dot_general(av, w1_ref[0].astype(MXU_DTYPE), _NT, preferred_element_type=F32)
        h3 = lax.dot_general(av, w3_ref[0].astype(MXU_DTYPE), _NT, preferred_element_type=F32)
        h1_ref[0] = h1.astype(h1_ref.dtype)
        h3_ref[0] = h3.astype(h3_ref.dtype)
        g_ref[0] = (jax.nn.silu(h1) * h3).astype(g_ref.dtype)

    w_spec = pl.BlockSpec((1, FF_SHARD, D_MODEL), lambda s, i: (s, 0, 0))
    h_spec = pl.BlockSpec((1, tm, FF_SHARD), lambda s, i: (s, i, 0))
    h_shape = jax.ShapeDtypeStruct((N_SHARD, n_tok, FF_SHARD), MXU_DTYPE)
    blk = _nbytes((tm, D_MODEL), a.dtype) + 2 * _nbytes((D_MODEL, FF_SHARD), w1s.dtype) + 3 * _nbytes((tm, FF_SHARD), MXU_DTYPE)
    return pl.pallas_call(
        body, name=name, grid=(N_SHARD, n_tok // tm),
        in_specs=[pl.BlockSpec((tm, D_MODEL), lambda s, i: (i, 0)), w_spec, w_spec],
        out_specs=(h_spec, h_spec, h_spec), out_shape=(h_shape, h_shape, h_shape),
        compiler_params=_params(blk, 6 * _nbytes((tm, FF_SHARD), F32)),
    )(*_hbm(a, w1s, w3s))


def _ffn_down_bwd(df, w2s, h1, h3, *, name, tm=FFN_TOKENS):
    n_tok = df.shape[0]
    tm = min(tm, n_tok)

    def body(df_ref, w2_ref, h1_ref, h3_ref, dh1_ref, dh3_ref):
        dg = lax.dot_general(df_ref[...].astype(MXU_DTYPE), w2_ref[0].astype(MXU_DTYPE), _NT, preferred_element_type=F32)
        h1v = h1_ref[0].astype(F32)
        h3v = h3_ref[0].astype(F32)
        sig = jax.nn.sigmoid(h1v)
        dh3_ref[0] = (dg * (h1v * sig)).astype(dh3_ref.dtype)
        dh1_ref[0] = (dg * h3v * (sig * (1.0 + h1v * (1.0 - sig)))).astype(dh1_ref.dtype)

    h_spec = pl.BlockSpec((1, tm, FF_SHARD), lambda s, i: (s, i, 0))
    h_shape = jax.ShapeDtypeStruct((N_SHARD, n_tok, FF_SHARD), MXU_DTYPE)
    blk = _nbytes((tm, D_MODEL), df.dtype) + _nbytes((FF_SHARD, D_MODEL), w2s.dtype) + 4 * _nbytes((tm, FF_SHARD), MXU_DTYPE)
    return pl.pallas_call(
        body, name=name, grid=(N_SHARD, n_tok // tm),
        in_specs=[pl.BlockSpec((tm, D_MODEL), lambda s, i: (i, 0)),
                  pl.BlockSpec((1, FF_SHARD, D_MODEL), lambda s, i: (s, 0, 0)), h_spec, h_spec],
        out_specs=(h_spec, h_spec), out_shape=(h_shape, h_shape),
        compiler_params=_params(blk, 8 * _nbytes((tm, FF_SHARD), F32)),
    )(*_hbm(df, w2s, h1, h3))


def _ffn_fwd(a, w1s, w3s, w2s, tag):
    h1, h3, g = _ffn_up(a, w1s, w3s, name=f"{tag}_up")
    f = _mm_act([(g, w2s)], "nn", reduce_shards=True, tm=FFN_TOKENS, name=f"{tag}_down")[0]
    return f, (h1, h3, g)


def _ffn_bwd(a, w1s, w3s, w2s, saved, df, tag):
    h1, h3, g = saved
    dh1, dh3 = _ffn_down_bwd(df, w2s, h1, h3, name=f"{tag}_down_bwd")
    da = _mm_act([(dh1, w1s), (dh3, w3s)], "nn", reduce_shards=True, tm=FFN_TOKENS, name=f"{tag}_up_bwd")[0]
    a3 = a[None]
    dw1 = _mm_tn(dh1, a3, tt=FFN_TOKENS, name=f"{tag}_dw1")
    dw3 = _mm_tn(dh3, a3, tt=FFN_TOKENS, name=f"{tag}_dw3")
    dw2 = _mm_tn(g, df[None], tt=FFN_TOKENS, name=f"{tag}_dw2")
    return da, dw1, dw3, dw2


CONV_LANES = 256


def _shift_down(x, d):
    if d == 0:
        return x
    row = lax.broadcasted_iota(jnp.int32, x.shape, 0)
    return jnp.where(row >= d, pltpu.roll(x, d, 0), 0.0)


def _shift_up(x, d):
    if d == 0:
        return x
    n = x.shape[0]
    row = lax.broadcasted_iota(jnp.int32, x.shape, 0)
    return jnp.where(row < n - d, pltpu.roll(x, n - d, 0), 0.0)


def _conv_pre(x, w):
    acc = None
    for j in range(CONV_WIDTH):
        term = w[j:j + 1, :] * _shift_down(x, CONV_WIDTH - 1 - j)
        acc = term if acc is None else acc + term
    return acc


def _conv_fwd(x, w, *, name):
    n_b, n_s, n_c = x.shape
    spec = pl.BlockSpec((1, n_s, CONV_LANES), lambda b, cj: (b, 0, cj))

    def body(x_ref, w_ref, o_ref):
        o_ref[0] = jax.nn.silu(_conv_pre(x_ref[0], w_ref[...]))

    return pl.pallas_call(
        body, name=name, grid=(n_b, n_c // CONV_LANES),
        in_specs=[spec, pl.BlockSpec((CONV_WIDTH, CONV_LANES), lambda b, cj: (0, cj))],
        out_specs=spec, out_shape=jax.ShapeDtypeStruct(x.shape, F32),
        compiler_params=_params(2 * _nbytes((n_s, CONV_LANES), F32), 6 * _nbytes((n_s, CONV_LANES), F32)),
    )(*_hbm(x, w))


def _conv_bwd(x, w, dout, *, name):
    n_b, n_s, n_c = x.shape
    per_part = DN_WIDTH // CONV_LANES
    spec = pl.BlockSpec((1, n_s, CONV_LANES), lambda cj, b: (b, 0, cj))
    do_spec = pl.BlockSpec((1, 1, n_s, CONV_LANES), lambda cj, b: (cj // per_part, b, 0, cj % per_part))
    w_spec = pl.BlockSpec((CONV_WIDTH, CONV_LANES), lambda cj, b: (0, cj))

    def body(x_ref, w_ref, do_ref, dx_ref, dw_ref):
        xv, wv = x_ref[0], w_ref[...]
        pre = _conv_pre(xv, wv)
        sig = jax.nn.sigmoid(pre)
        dpre = do_ref[0, 0] * (sig * (1.0 + pre * (1.0 - sig)))
        dx = None
        first = pl.program_id(1) == 0
        for j in range(CONV_WIDTH):
            d = CONV_WIDTH - 1 - j
            ahead = _shift_up(dpre, d)
            term = wv[j:j + 1, :] * ahead
            dx = term if dx is None else dx + term
            dwj = jnp.sum(ahead * xv, axis=0, keepdims=True)
            _accumulate(dw_ref.at[j:j + 1, :], dwj, first)
        dx_ref[0] = dx.astype(dx_ref.dtype)

    return pl.pallas_call(
        body, name=name, grid=(n_c // CONV_LANES, n_b),
        in_specs=[spec, w_spec, do_spec], out_specs=(spec, w_spec),
        out_shape=(jax.ShapeDtypeStruct(x.shape, MXU_DTYPE), jax.ShapeDtypeStruct((CONV_WIDTH, n_c), F32)),
        compiler_params=_params(3 * _nbytes((n_s, CONV_LANES), F32), 8 * _nbytes((n_s, CONV_LANES), F32)),
    )(*_hbm(x, w, dout))


_BNT = (((2,), (2,)), ((0,), (0,)))
_BNN = (((2,), (1,)), ((0,), (0,)))
_BTN = (((1,), (1,)), ((0,), (0,)))
DN_PREP_CHUNKS = 8
DN_SCAN_HEADS = 16
DN_SCAN_CHUNKS = 4
N_DOUBLINGS = 5


def _fdot(a, b, dims):
    return lax.dot_general(a, b, dims, precision=lax.Precision.HIGHEST, preferred_element_type=F32)


def _hdot(a, b, dims):
    return lax.dot_general(a, b, dims, precision=lax.Precision.HIGH, preferred_element_type=F32)


def _solve_by_doubling(a, rhs_u, rhs_w):
    row = lax.broadcasted_iota(jnp.int32, (CHUNK, CHUNK), 0)
    col = lax.broadcasted_iota(jnp.int32, (CHUNK, CHUNK), 1)
    inv = jnp.where(row == col, 1.0, 0.0) - a
    power = a
    for _ in range(N_DOUBLINGS):
        power = _hdot(power, power, _BNN)
        inv = inv + _hdot(inv, power, _BNN)
    return _hdot(inv, rhs_u, _BNN), _hdot(inv, rhs_w, _BNN), inv


@jax.custom_vjp
def _solve_saved(a, rhs_u, rhs_w, inv, u, w):
    return u, w


def _solve_saved_fwd(a, rhs_u, rhs_w, inv, u, w):
    return (u, w), (inv, u, w)


def _solve_saved_bwd(res, cts):
    inv, u, w = res
    gu = _hdot(inv, cts[0], _BTN)
    gw = _hdot(inv, cts[1], _BTN)
    da = -(_hdot(gu, u, _BNT) + _hdot(gw, w, _BNT))
    return da, gu, gw, jnp.zeros_like(inv), jnp.zeros_like(u), jnp.zeros_like(w)


_solve_saved.defvjp(_solve_saved_fwd, _solve_saved_bwd)


def _dn_prep_fn(solve, qc, kc, vc, bl, lac, lar, a_log, dt_bias):
    q = qc * lax.rsqrt(jnp.sum(qc * qc, axis=-1, keepdims=True) + EPS) * (DN_HEAD_DIM ** -0.5)
    k = kc * lax.rsqrt(jnp.sum(kc * kc, axis=-1, keepdims=True) + EPS)
    beta = jax.nn.sigmoid(bl)
    neg_a = -jnp.exp(a_log)
    lgc = neg_a * jax.nn.softplus(lac + dt_bias)
    lgr = neg_a * jax.nn.softplus(lar + dt_bias)
    row = lax.broadcasted_iota(jnp.int32, (CHUNK, CHUNK), 0)
    col = lax.broadcasted_iota(jnp.int32, (CHUNK, CHUNK), 1)
    causal, strict = row >= col, row > col
    g_c = jnp.sum(jnp.where(causal, lgr, 0.0), axis=-1, keepdims=True)
    g_r = jnp.sum(jnp.where(row <= col, lgc, 0.0), axis=-2, keepdims=True)
    decay = jnp.exp(jnp.where(causal, g_c - g_r, -jnp.inf))
    kb = k * beta
    a = jnp.where(strict, _mdot(kb, k, _BNT) * decay, 0.0)
    u, w, extra = solve(a, vc * beta, kb * jnp.exp(g_c))
    attn = _mdot(q, k, _BNT) * decay
    g_last = jnp.sum(lgc, axis=-2, keepdims=True)
    return q * jnp.exp(g_c), k * jnp.exp(g_last - g_c), u, w, attn, g_last, extra


PAIR = 2
PAIR_LANES = PAIR * DN_HEAD_DIM


def _dn_prep_specs(n_cb):
    tok = n_cb * CHUNK
    wide = pl.BlockSpec((1, 1, tok, PAIR_LANES), lambda p, b, j: (b, p, j, 0))
    rowv = pl.BlockSpec((1, PAIR, n_cb, 1, CHUNK), lambda p, b, j: (b, p, j, 0, 0))
    one = pl.BlockSpec((1, PAIR, n_cb, 1, 1), lambda p, b, j: (b, p, j, 0, 0))
    head = pl.BlockSpec((PAIR, 1, 1), lambda p, b, j: (p, 0, 0))
    lanes = lambda part: pl.BlockSpec((1, tok, PAIR_LANES), lambda p, b, j: (b, j, part * (DN_HEADS // PAIR) + p))
    return wide, rowv, one, head, lanes


def _split_pair(x, n_cb):
    halves = [x[:, h * DN_HEAD_DIM:(h + 1) * DN_HEAD_DIM].reshape(n_cb, CHUNK, DN_HEAD_DIM) for h in range(PAIR)]
    return jnp.concatenate(halves, axis=0)


def _join_pair(chunks, tok):
    per_head = chunks.reshape(PAIR, tok, DN_HEAD_DIM)
    return jnp.concatenate([per_head[h] for h in range(PAIR)], axis=-1)


def _dn_prep_load(n_cb, q_ref, k_ref, v_ref, blr_ref, lar_ref, al_ref, dt_ref):
    rowf = lambda r: r[0].reshape(PAIR * n_cb, 1, CHUNK)
    return (_split_pair(q_ref[0], n_cb), _split_pair(k_ref[0], n_cb), _split_pair(v_ref[0], n_cb), rowf(blr_ref),
            rowf(lar_ref), al_ref[...], dt_ref[...])


def _dn_prep_pair_fn(n_cb, solve, qc, kc, vc, blr, lar, a_log, dt_bias):
    per_chunk = lambda t: jnp.broadcast_to(t[:, None], (PAIR, n_cb, 1, 1)).reshape(PAIR * n_cb, 1, 1)
    eye = lax.broadcasted_iota(jnp.int32, (CHUNK, CHUNK), 0) == lax.broadcasted_iota(jnp.int32, (CHUNK, CHUNK), 1)
    to_col = lambda r: jnp.sum(jnp.where(eye, r, 0.0), axis=-1, keepdims=True)
    return _dn_prep_fn(solve, qc, kc, vc, to_col(blr), to_col(lar), lar, per_chunk(a_log), per_chunk(dt_bias))


def _dn_prep(qkv, blr, lar, a_log, dt_bias, *, name):
    n_b, n_s, _ = qkv.shape
    n_cb = min(DN_PREP_CHUNKS, n_s // CHUNK)
    tok = n_cb * CHUNK
    wide, rowv, one, head, lanes = _dn_prep_specs(n_cb)

    def body(*refs):
        outs = _dn_prep_pair_fn(n_cb, _solve_by_doubling, *_dn_prep_load(n_cb, *refs[:7]))
        for ref, val in zip(refs[7:12], outs[:5]):
            ref[0, 0] = _join_pair(val, tok)
        refs[12][0] = outs[5].reshape(PAIR, n_cb, 1, 1)
        refs[13][0, 0] = _join_pair(outs[6], tok)

    big = jax.ShapeDtypeStruct((n_b, DN_HEADS // PAIR, n_s, PAIR_LANES), F32)
    return pl.pallas_call(
        body, name=name, grid=(DN_HEADS // PAIR, n_b, n_s // tok),
        in_specs=[lanes(0), lanes(1), lanes(2), rowv, rowv, head, head],
        out_specs=(wide, wide, wide, wide, wide, one, wide),
        out_shape=(big, big, big, big, big, jax.ShapeDtypeStruct((n_b, DN_HEADS, n_s // CHUNK, 1, 1), F32), big),
        compiler_params=_params(11 * PAIR * _nbytes((tok, LANES), F32), 48 * PAIR * _nbytes((tok, LANES), F32)),
    )(*_hbm(qkv, qkv, qkv, blr, lar, a_log, dt_bias))


def _dn_prep_bwd(qkv, blr, lar, a_log, dt_bias, inv, u, w, cts, *, name):
    n_b, n_s, _ = qkv.shape
    n_cb = min(DN_PREP_CHUNKS, n_s // CHUNK)
    tok = n_cb * CHUNK
    wide, rowv, one, head, lanes = _dn_prep_specs(n_cb)

    def body(*refs):
        prim = _dn_prep_load(n_cb, *refs[:7])
        chunks = lambda r: _split_pair(r[0, 0], n_cb)
        inv_v, u_v, w_v = chunks(refs[7]), chunks(refs[8]), chunks(refs[9])
        ct = tuple(chunks(r) for r in refs[10:15]) + (refs[15][0].reshape(PAIR * n_cb, 1, 1),)

        def fn(*args):
            solve = lambda a, ru, rw: _solve_saved(a, ru, rw, inv_v, u_v, w_v) + (None,)
            return _dn_prep_pair_fn(n_cb, solve, *args)[:6]

        _, pull = jax.vjp(fn, *prim)
        dq, dk, dv, dblr, dlar, dal, ddt = pull(ct)
        outs = refs[16:]
        for part, val in enumerate((dq, dk, dv)):
            outs[0][part, 0] = _join_pair(val, tok)
        outs[1][0] = dblr.reshape(PAIR, n_cb, 1, CHUNK)
        outs[2][0] = dlar.reshape(PAIR, n_cb, 1, CHUNK)
        first = jnp.logical_and(pl.program_id(1) == 0, pl.program_id(2) == 0)
        _accumulate(outs[3], dal, first)
        _accumulate(outs[4], ddt, first)

    dqkv_spec = pl.BlockSpec((3, 1, tok, PAIR_LANES), lambda p, b, j: (0, b, j, p))
    return pl.pallas_call(
        body, name=name, grid=(DN_HEADS // PAIR, n_b, n_s // tok),
        in_specs=[lanes(0), lanes(1), lanes(2), rowv, rowv, head, head, wide, wide, wide, wide, wide, wide, wide, wide, one],
        out_specs=(dqkv_spec, rowv, rowv, head, head),
        out_shape=(jax.ShapeDtypeStruct((3, n_b, n_s, DN_WIDTH), F32), jax.ShapeDtypeStruct(blr.shape, F32),
                   jax.ShapeDtypeStruct(lar.shape, F32), jax.ShapeDtypeStruct(a_log.shape, F32),
                   jax.ShapeDtypeStruct(dt_bias.shape, F32)),
        compiler_params=_params(21 * PAIR * _nbytes((tok, LANES), F32), 64 * PAIR * _nbytes((tok, LANES), F32)),
    )(*_hbm(qkv, qkv, qkv, blr, lar, a_log, dt_bias, inv, u, w, *cts))


def _dn_step(state, q, k, u, w, a, gl):
    v_new = u - _mdot(w, state, _BNN)
    o = _mdot(q, state, _BNN) + _mdot(a, v_new, _BNN)
    return state * jnp.exp(gl) + _mdot(k, v_new, _BTN), o


def _dn_scan_specs(n_cb, n_blocks, reverse, n_seq):
    tok = n_cb * CHUNK
    jj = (lambda j: n_blocks - 1 - j) if reverse else (lambda j: j)
    wide = pl.BlockSpec((1, n_seq // PAIR, tok, PAIR_LANES), lambda b, j: (b, 0, jj(j), 0))
    one = pl.BlockSpec((1, n_seq, n_cb, 1, 1), lambda b, j: (b, 0, jj(j), 0, 0))
    st = pl.BlockSpec((1, n_seq, n_cb, DN_HEAD_DIM, DN_HEAD_DIM), lambda b, j: (b, 0, jj(j), 0, 0))
    return wide, one, st


def _scan_fold(n_b):
    fold = max(1, DN_SCAN_HEADS // DN_HEADS)
    return fold if n_b % fold == 0 else 1


def _fold_rows(arrays, fold):
    return [a.reshape((a.shape[0] // fold, fold * a.shape[1]) + a.shape[2:]) for a in arrays]


def _to_scan_order(per_head, fold):
    n_b, rest = per_head.shape[0], per_head.shape[2:]
    t = per_head.reshape((n_b // fold, fold, DN_HEADS // PAIR, PAIR) + rest)
    t = jnp.moveaxis(t, 3, 1)
    return t.reshape((n_b // fold, fold * DN_HEADS) + rest)


def _from_scan_order(t, fold):
    n_bf, rest = t.shape[0], t.shape[2:]
    t = t.reshape((n_bf, PAIR, fold, DN_HEADS // PAIR) + rest)
    t = jnp.moveaxis(t, 1, 3)
    return t.reshape((n_bf * fold, DN_HEADS) + rest)


def _unpack_seqs(x):
    return jnp.concatenate([x[:, :, :DN_HEAD_DIM], x[:, :, DN_HEAD_DIM:]], axis=0)


def _pack_seqs(x):
    n_p = x.shape[0] // PAIR
    return jnp.concatenate([x[:n_p], x[n_p:]], axis=-1)


def _dn_scan(qd, kd, u, w, attn, g_last, *, name):
    shape = qd.shape
    fold = _scan_fold(shape[0])
    qd, kd, u, w, attn = _fold_rows([qd, kd, u, w, attn], fold)
    g_last = _to_scan_order(g_last, fold)
    n_b, n_pk, n_s, _ = qd.shape
    n_seq = PAIR * n_pk
    n_cb = min(DN_SCAN_CHUNKS, n_s // CHUNK)
    n_blocks = n_s // (n_cb * CHUNK)
    wide, one, st = _dn_scan_specs(n_cb, n_blocks, False, n_seq)

    def body(qd_ref, kd_ref, u_ref, w_ref, a_ref, gl_ref, o_ref, st_ref, state_ref):
        @pl.when(pl.program_id(1) == 0)
        def _():
            state_ref[...] = jnp.zeros(state_ref.shape, F32)

        def step(n, state):
            rows = pl.ds(pl.multiple_of(n * CHUNK, CHUNK), CHUNK)
            seqs = lambda r: _unpack_seqs(r[0, :, rows, :])
            st_ref[0, :, n] = state
            state, o = _dn_step(state, seqs(qd_ref), seqs(kd_ref), seqs(u_ref), seqs(w_ref), seqs(a_ref), gl_ref[0, :, n])
            o_ref[0, :, rows, :] = _pack_seqs(o)
            return state

        state_ref[...] = lax.fori_loop(0, n_cb, step, state_ref[...])

    o, states = pl.pallas_call(
        body, name=name, grid=(n_b, n_blocks),
        in_specs=[wide, wide, wide, wide, wide, one], out_specs=(wide, st),
        out_shape=(jax.ShapeDtypeStruct(qd.shape, F32),
                   jax.ShapeDtypeStruct((n_b, n_seq, n_s // CHUNK, DN_HEAD_DIM, DN_HEAD_DIM), F32)),
        scratch_shapes=[pltpu.VMEM((n_seq, DN_HEAD_DIM, DN_HEAD_DIM), F32)],
        compiler_params=_params(6 * _nbytes((n_pk, n_cb * CHUNK, PAIR_LANES), F32)
                                + _nbytes((n_seq, n_cb * CHUNK, LANES), F32), 8 << 20),
    )(*_hbm(qd, kd, u, w, attn, g_last))
    return o.reshape(shape), states


def _dn_scan_bwd(qd, kd, u, w, attn, g_last, states, do, *, name):
    shape = qd.shape
    fold = _scan_fold(shape[0])
    qd, kd, u, w, attn, do = _fold_rows([qd, kd, u, w, attn, do], fold)
    g_last = _to_scan_order(g_last, fold)
    n_b, n_pk, n_s, _ = qd.shape
    n_seq = PAIR * n_pk
    n_cb = min(DN_SCAN_CHUNKS, n_s // CHUNK)
    n_blocks = n_s // (n_cb * CHUNK)
    wide, one, st = _dn_scan_specs(n_cb, n_blocks, True, n_seq)

    def body(qd_ref, kd_ref, u_ref, w_ref, a_ref, gl_ref, st_ref, do_ref,
             dq_ref, dk_ref, du_ref, dw_ref, da_ref, dgl_ref, dstate_ref):
        @pl.when(pl.program_id(1) == 0)
        def _():
            dstate_ref[...] = jnp.zeros(dstate_ref.shape, F32)

        def step(i, dstate):
            n = n_cb - 1 - i
            rows = pl.ds(pl.multiple_of(n * CHUNK, CHUNK), CHUNK)
            seqs = lambda r: _unpack_seqs(r[0, :, rows, :])
            _, pull = jax.vjp(_dn_step, st_ref[0, :, n], seqs(qd_ref), seqs(kd_ref), seqs(u_ref), seqs(w_ref),
                              seqs(a_ref), gl_ref[0, :, n])
            dstate, dq, dk, du, dw, da, dgl = pull((dstate, seqs(do_ref)))
            for ref, val in zip((dq_ref, dk_ref, du_ref, dw_ref, da_ref), (dq, dk, du, dw, da)):
                ref[0, :, rows, :] = _pack_seqs(val)
            dgl_ref[0, :, n] = dgl
            return dstate

        dstate_ref[...] = lax.fori_loop(0, n_cb, step, dstate_ref[...])

    big = jax.ShapeDtypeStruct(qd.shape, F32)
    outs = pl.pallas_call(
        body, name=name, grid=(n_b, n_blocks),
        in_specs=[wide, wide, wide, wide, wide, one, st, wide],
        out_specs=(wide, wide, wide, wide, wide, one),
        out_shape=(big, big, big, big, big, jax.ShapeDtypeStruct(g_last.shape, F32)),
        scratch_shapes=[pltpu.VMEM((n_seq, DN_HEAD_DIM, DN_HEAD_DIM), F32)],
        compiler_params=_params(11 * _nbytes((n_pk, n_cb * CHUNK, PAIR_LANES), F32)
                                + _nbytes((n_seq, n_cb * CHUNK, LANES), F32), 8 << 20),
    )(*_hbm(qd, kd, u, w, attn, g_last, states, do))
    return tuple(o.reshape(shape) for o in outs[:5]) + (_from_scan_order(outs[5], fold),)


def _dn_post_fn(o, z, gain):
    return o * lax.rsqrt(jnp.mean(o * o, axis=-1, keepdims=True) + EPS) * gain * jax.nn.silu(z)


_HEAD_ROWS = lambda n_s: pl.BlockSpec((1, 1, n_s, PAIR_LANES), lambda b, p: (b, p, 0, 0))
_PAIR_LANES = lambda n_s: pl.BlockSpec((1, n_s, PAIR_LANES), lambda b, p: (b, 0, p))
_HEAD_GAIN = pl.BlockSpec((1, DN_HEAD_DIM), lambda b, p: (0, 0))


def _pair_heads(x):
    return jnp.stack([x[:, h * DN_HEAD_DIM:(h + 1) * DN_HEAD_DIM] for h in range(PAIR)])


def _pair_lanes(x):
    return jnp.concatenate([x[h] for h in range(PAIR)], axis=-1)


def _dn_post(o, z, gain, *, name):
    n_b, _, n_s, _ = o.shape

    def body(o_ref, z_ref, g_ref, out_ref):
        out = _dn_post_fn(_pair_heads(o_ref[0, 0]), _pair_heads(z_ref[0]), g_ref[...])
        out_ref[0] = _pair_lanes(out).astype(out_ref.dtype)

    lanes = _PAIR_LANES(n_s)
    return pl.pallas_call(
        body, name=name, grid=(n_b, DN_HEADS // PAIR), in_specs=[_HEAD_ROWS(n_s), lanes, _HEAD_GAIN], out_specs=lanes,
        out_shape=jax.ShapeDtypeStruct(z.shape, MXU_DTYPE),
        compiler_params=_params(3 * PAIR * _nbytes((n_s, LANES), F32), 6 * PAIR * _nbytes((n_s, LANES), F32)),
    )(*_hbm(o, z, gain))


def _dn_post_bwd(o, z, gain, dout, *, name):
    n_b, _, n_s, _ = o.shape

    def body(o_ref, z_ref, g_ref, dout_ref, do_ref, dz_ref, dg_ref):
        _, pull = jax.vjp(_dn_post_fn, _pair_heads(o_ref[0, 0]), _pair_heads(z_ref[0]), g_ref[...])
        do, dz, dg = pull(_pair_heads(dout_ref[0].astype(F32)))
        do_ref[0, 0] = _pair_lanes(do)
        dz_ref[0] = _pair_lanes(dz).astype(dz_ref.dtype)
        _accumulate(dg_ref, dg, jnp.logical_and(pl.program_id(0) == 0, pl.program_id(1) == 0))

    rows, lanes = _HEAD_ROWS(n_s), _PAIR_LANES(n_s)
    return pl.pallas_call(
        body, name=name, grid=(n_b, DN_HEADS // PAIR), in_specs=[rows, lanes, _HEAD_GAIN, lanes],
        out_specs=(rows, lanes, _HEAD_GAIN),
        out_shape=(jax.ShapeDtypeStruct(o.shape, F32), jax.ShapeDtypeStruct(z.shape, MXU_DTYPE),
                   jax.ShapeDtypeStruct((1, DN_HEAD_DIM), F32)),
        compiler_params=_params(5 * PAIR * _nbytes((n_s, LANES), F32), 10 * PAIR * _nbytes((n_s, LANES), F32)),
    )(*_hbm(o, z, gain, dout))


TILE_ROWS = SUBLANES


def _s5_prep_fn(lam_re, lam_im, log_step, bt_re, bt_im, c_im):
    lr = jnp.minimum(lam_re, -1e-4)
    step = jnp.exp(log_step)
    mag = jnp.exp(lr * step)
    ang = lam_im * step
    lb_re = mag * jnp.cos(ang)
    lb_im = mag * jnp.sin(ang)
    den = lr * lr + lam_im * lam_im
    coef_re = ((lb_re - 1.0) * lr + lb_im * lam_im) / den
    coef_im = (lb_im * lr - (lb_re - 1.0) * lam_im) / den
    return (lb_re, lb_im, coef_re * bt_re - coef_im * bt_im, coef_re * bt_im + coef_im * bt_re, -c_im)


def _s5_prep(lam_re, lam_im, log_step, bt_re, bt_im, c_im, *, name):
    def body(*refs):
        outs = _s5_prep_fn(*(r[...] for r in refs[:6]))
        for ref, val in zip(refs[6:], outs):
            ref[...] = val

    vec = jax.ShapeDtypeStruct(lam_re.shape, F32)
    mat = jax.ShapeDtypeStruct(bt_re.shape, F32)
    return pl.pallas_call(body, name=name, out_shape=(vec, vec, mat, mat, mat))(lam_re, lam_im, log_step, bt_re, bt_im, c_im)


def _s5_prep_bwd(lam_re, lam_im, log_step, bt_re, bt_im, c_im, cts, *, name):
    def body(*refs):
        _, pull = jax.vjp(_s5_prep_fn, *(r[...] for r in refs[:6]))
        grads = pull(tuple(r[...] for r in refs[6:11]))
        for ref, val in zip(refs[11:], grads):
            ref[...] = val

    shapes = tuple(jax.ShapeDtypeStruct(a.shape, F32) for a in (lam_re, lam_im, log_step, bt_re, bt_im, c_im))
    return pl.pallas_call(body, name=name, out_shape=shapes)(lam_re, lam_im, log_step, bt_re, bt_im, c_im, *cts)


def _cmul(ar, ai, br, bi):
    return ar * br - ai * bi, ar * bi + ai * br


def _s5_powers(lr, li):
    pows = [(lr, li)]
    for _ in range(TILE_ROWS - 1):
        pows.append(_cmul(pows[-1][0], pows[-1][1], lr, li))
    return pows


def _s5_carry_table(pows, n_lanes, reverse):
    row = lax.broadcasted_iota(jnp.int32, (TILE_ROWS, n_lanes), 0)
    t_re = jnp.zeros((TILE_ROWS, n_lanes), F32)
    t_im = jnp.zeros((TILE_ROWS, n_lanes), F32)
    for r in range(TILE_ROWS):
        p_re, p_im = pows[TILE_ROWS - 1 - r] if reverse else pows[r]
        t_re = jnp.where(row == r, p_re, t_re)
        t_im = jnp.where(row == r, p_im, t_im)
    return t_re, t_im


def _s5_tile(y_re, y_im, pows, reverse):
    d = 1
    while d < TILE_ROWS:
        p_re, p_im = pows[d - 1]
        if reverse:
            s_re, s_im = _shift_up(y_re, d), _shift_up(y_im, d)
        else:
            s_re, s_im = _shift_down(y_re, d), _shift_down(y_im, d)
        m_re, m_im = _cmul(p_re, p_im, s_re, s_im)
        y_re, y_im = y_re + m_re, y_im + m_im
        d *= 2
    return y_re, y_im


S5_BLOCKS = N_SHARD
S5_BLOCK_CH = S5_WIDTH // S5_BLOCKS
S5_BLOCK_LANES = S5_LANES // S5_BLOCKS


def _scan_rows(i):
    return pl.ds(pl.multiple_of(i * TILE_ROWS, TILE_ROWS), TILE_ROWS)


def _s5_mix_specs(n_s, order):
    jb = lambda *g: order(*g)[0]
    bb = lambda *g: order(*g)[1]
    act = pl.BlockSpec((1, 1, n_s, S5_BLOCK_CH), lambda *g: (bb(*g), 0, 0, jb(*g)))
    state = pl.BlockSpec((1, 1, n_s, S5_BLOCK_LANES), lambda *g: (jb(*g), bb(*g), 0, 0))
    lam = pl.BlockSpec((1, S5_BLOCK_LANES), lambda *g: (0, jb(*g)))
    w_in = pl.BlockSpec((1, S5_BLOCK_CH, S5_BLOCK_LANES), lambda *g: (jb(*g), 0, 0))
    w_out = pl.BlockSpec((1, S5_BLOCK_LANES, S5_BLOCK_CH), lambda *g: (jb(*g), 0, 0))
    return act, state, lam, w_in, w_out


def _s5_mix(u, wb_re, wb_im, lb_re, lb_im, wc_re, wc_im, *, name):
    n_b, n_s, _ = u.shape
    n_blk = S5_BLOCKS
    lanes = lambda t: t[:, None]
    n_tiles = n_s // TILE_ROWS
    L = S5_BLOCK_LANES

    def body(u_ref, wbr_ref, wbi_ref, lr_ref, li_ref, wcr_ref, wci_ref, y_ref, xr_ref, xi_ref):
        uv = u_ref[0, 0].astype(MXU_DTYPE)
        xr_ref[0, 0] = lax.dot_general(uv, wbr_ref[0].astype(MXU_DTYPE), _NN, preferred_element_type=F32)
        xi_ref[0, 0] = lax.dot_general(uv, wbi_ref[0].astype(MXU_DTYPE), _NN, preferred_element_type=F32)
        pows = _s5_powers(lr_ref[...], li_ref[...])
        t_re, t_im = _s5_carry_table(pows, L, False)

        def step(i, carry):
            rows = _scan_rows(i)
            y_re, y_im = _s5_tile(xr_ref[0, 0, rows, :], xi_ref[0, 0, rows, :], pows, False)
            c_re, c_im = _cmul(t_re, t_im, carry[0], carry[1])
            y_re, y_im = y_re + c_re, y_im + c_im
            xr_ref[0, 0, rows, :] = y_re
            xi_ref[0, 0, rows, :] = y_im
            return y_re[TILE_ROWS - 1:, :], y_im[TILE_ROWS - 1:, :]

        zero = jnp.zeros((1, L), F32)
        lax.fori_loop(0, n_tiles, step, (zero, zero), unroll=2)
        y_ref[0, 0] = (
            lax.dot_general(xr_ref[0, 0].astype(MXU_DTYPE), wcr_ref[0].astype(MXU_DTYPE), _NN, preferred_element_type=F32)
            + lax.dot_general(xi_ref[0, 0].astype(MXU_DTYPE), wci_ref[0].astype(MXU_DTYPE), _NN, preferred_element_type=F32))

    act, state, lam, w_in, w_out = _s5_mix_specs(n_s, lambda b, j: (j, b))
    x_shape = jax.ShapeDtypeStruct((n_blk, n_b, n_s, L), F32)
    return pl.pallas_call(
        body, name=name, grid=(n_b, n_blk),
        in_specs=[act, w_in, w_in, lam, lam, w_out, w_out], out_specs=(act, state, state),
        out_shape=(jax.ShapeDtypeStruct((n_b, 1, n_s, S5_WIDTH), F32), x_shape, x_shape),
        compiler_params=_params(2 * _nbytes((n_s, L), F32) + 2 * _nbytes((n_s, S5_BLOCK_CH), F32), 3 * _nbytes((n_s, L), F32)),
    )(*_hbm(lanes(u), wb_re, wb_im, lb_re, lb_im, wc_re, wc_im))


def _s5_mix_bwd(dy, du_skip, u, x_re, x_im, wb_re, wb_im, lb_re, lb_im, wc_re, wc_im, *, name):
    n_b, n_s, _ = u.shape
    n_blk = S5_BLOCKS
    lanes = lambda t: t[:, None]
    n_tiles = n_s // TILE_ROWS
    L = S5_BLOCK_LANES

    def body(dy_ref, ds_ref, u_ref, xr_ref, xi_ref, wbr_ref, wbi_ref, lr_ref, li_ref, wcr_ref, wci_ref,
             du_ref, dwbr_ref, dwbi_ref, dlr_ref, dli_ref, dwcr_ref, dwci_ref, ar_ref, ai_ref):
        dyv = dy_ref[0, 0].astype(MXU_DTYPE)
        ar_ref[...] = lax.dot_general(dyv, wcr_ref[0].astype(MXU_DTYPE), _NT, preferred_element_type=F32)
        ai_ref[...] = lax.dot_general(dyv, wci_ref[0].astype(MXU_DTYPE), _NT, preferred_element_type=F32)
        pows = _s5_powers(lr_ref[...], -li_ref[...])
        t_re, t_im = _s5_carry_table(pows, L, True)
        row = lax.broadcasted_iota(jnp.int32, (TILE_ROWS, L), 0)

        def step(k, carry):
            c_re, c_im, s_re, s_im = carry
            i = n_tiles - 1 - k
            rows = _scan_rows(i)
            a_re, a_im = _s5_tile(ar_ref[rows, :], ai_ref[rows, :], pows, True)
            m_re, m_im = _cmul(t_re, t_im, c_re, c_im)
            a_re, a_im = a_re + m_re, a_im + m_im
            ar_ref[rows, :] = a_re
            ai_ref[rows, :] = a_im
            prev = _scan_rows(jnp.maximum(i - 1, 0))
            keep = jnp.where(i > 0, 1.0, 0.0)
            last_re = xr_ref[0, 0, prev, :][TILE_ROWS - 1:, :] * keep
            last_im = xi_ref[0, 0, prev, :][TILE_ROWS - 1:, :] * keep
            xp_re = jnp.where(row == 0, last_re, _shift_down(xr_ref[0, 0, rows, :], 1))
            xp_im = jnp.where(row == 0, last_im, _shift_down(xi_ref[0, 0, rows, :], 1))
            s_re = s_re + a_re * xp_re + a_im * xp_im
            s_im = s_im + a_im * xp_re - a_re * xp_im
            return a_re[:1, :], a_im[:1, :], s_re, s_im

        zero = jnp.zeros((1, L), F32)
        zt = jnp.zeros((TILE_ROWS, L), F32)
        _, _, s_re, s_im = lax.fori_loop(0, n_tiles, step, (zero, zero, zt, zt), unroll=2)
        first = pl.program_id(1) == 0
        _accumulate(dlr_ref, jnp.sum(s_re, axis=0, keepdims=True), first)
        _accumulate(dli_ref, jnp.sum(s_im, axis=0, keepdims=True), first)
        a_re, a_im = ar_ref[...].astype(MXU_DTYPE), ai_ref[...].astype(MXU_DTYPE)
        du = (lax.dot_general(a_re, wbr_ref[0].astype(MXU_DTYPE), _NT, preferred_element_type=F32)
              + lax.dot_general(a_im, wbi_ref[0].astype(MXU_DTYPE), _NT, preferred_element_type=F32))
        du_ref[0, 0] = (du + ds_ref[0, 0]).astype(du_ref.dtype)
        uv = u_ref[0, 0].astype(MXU_DTYPE)
        _accumulate(dwbr_ref, lax.dot_general(uv, a_re, _TN, preferred_element_type=F32)[None], first)
        _accumulate(dwbi_ref, lax.dot_general(uv, a_im, _TN, preferred_element_type=F32)[None], first)
        _accumulate(dwcr_ref, lax.dot_general(xr_ref[0, 0].astype(MXU_DTYPE), dyv, _TN, preferred_element_type=F32)[None], first)
        _accumulate(dwci_ref, lax.dot_general(xi_ref[0, 0].astype(MXU_DTYPE), dyv, _TN, preferred_element_type=F32)[None], first)

    act, state, lam, w_in, w_out = _s5_mix_specs(n_s, lambda j, b: (j, b))
    lam_shape = jax.ShapeDtypeStruct((1, S5_LANES), F32)
    return pl.pallas_call(
        body, name=name, grid=(n_blk, n_b),
        in_specs=[act, act, act, state, state, w_in, w_in, lam, lam, w_out, w_out],
        out_specs=(act, w_in, w_in, lam, lam, w_out, w_out),
        out_shape=(jax.ShapeDtypeStruct((n_b, 1, n_s, S5_WIDTH), MXU_DTYPE), jax.ShapeDtypeStruct(wb_re.shape, F32),
                   jax.ShapeDtypeStruct(wb_im.shape, F32), lam_shape, lam_shape,
                   jax.ShapeDtypeStruct(wc_re.shape, F32), jax.ShapeDtypeStruct(wc_im.shape, F32)),
        scratch_shapes=[pltpu.VMEM((n_s, L), F32), pltpu.VMEM((n_s, L), F32)],
        compiler_params=_params(2 * _nbytes((n_s, L), F32) + 4 * _nbytes((n_s, S5_BLOCK_CH), F32), 5 * _nbytes((n_s, L), F32)),
    )(*_hbm(lanes(dy), lanes(du_skip), lanes(u), x_re, x_im, wb_re, wb_im, lb_re, lb_im, wc_re, wc_im))


def _s5_out_fn(ymm, u, d_skip, w_glu, b_glu):
    y = jax.nn.gelu(ymm + d_skip * u)
    return y * jax.nn.sigmoid(_mdot(y, w_glu, _NN) + b_glu)


def _s5_out_specs(tm):
    rows = pl.BlockSpec((tm, S5_WIDTH), lambda i: (i, 0))
    vec = pl.BlockSpec((1, S5_WIDTH), lambda i: (0, 0))
    mat = pl.BlockSpec((S5_WIDTH, S5_WIDTH), lambda i: (0, 0))
    return rows, vec, mat


def _s5_out(ymm, u, d_skip, w_glu, b_glu, *, name, tm=512):
    n_tok = ymm.shape[0]
    tm = min(tm, n_tok)
    rows, vec, mat = _s5_out_specs(tm)

    def body(y_ref, u_ref, d_ref, w_ref, b_ref, o_ref):
        o_ref[...] = _s5_out_fn(y_ref[...], u_ref[...], d_ref[...], w_ref[...], b_ref[...]).astype(o_ref.dtype)

    return pl.pallas_call(
        body, name=name, grid=(n_tok // tm,), in_specs=[rows, rows, vec, mat, vec], out_specs=rows,
        out_shape=jax.ShapeDtypeStruct((n_tok, S5_WIDTH), MXU_DTYPE),
        compiler_params=_params(4 * _nbytes((tm, S5_WIDTH), F32), 8 * _nbytes((tm, S5_WIDTH), F32)),
    )(*_hbm(ymm, u, d_skip, w_glu, b_glu))


def _s5_out_bwd(ymm, u, d_skip, w_glu, b_glu, dout, *, name, tm=512):
    n_tok = ymm.shape[0]
    tm = min(tm, n_tok)
    rows, vec, mat = _s5_out_specs(tm)

    def body(y_ref, u_ref, d_ref, w_ref, b_ref, do_ref, dy_ref, du_ref, dd_ref, dw_ref, db_ref):
        _, pull = jax.vjp(_s5_out_fn, y_ref[...], u_ref[...], d_ref[...], w_ref[...].astype(F32), b_ref[...])
        dy, du, dd, dw, db = pull(do_ref[...])
        dy_ref[...] = dy.astype(dy_ref.dtype)
        du_ref[...] = du
        first = pl.program_id(0) == 0
        _accumulate(dd_ref, dd, first)
        _accumulate(dw_ref, dw, first)
        _accumulate(db_ref, db, first)

    return pl.pallas_call(
        body, name=name, grid=(n_tok // tm,), in_specs=[rows, rows, vec, mat, vec, rows],
        out_specs=(rows, rows, vec, mat, vec),
        out_shape=(jax.ShapeDtypeStruct(ymm.shape, MXU_DTYPE), jax.ShapeDtypeStruct(ymm.shape, F32),
                   jax.ShapeDtypeStruct((1, S5_WIDTH), F32), jax.ShapeDtypeStruct((S5_WIDTH, S5_WIDTH), F32),
                   jax.ShapeDtypeStruct((1, S5_WIDTH), F32)),
        compiler_params=_params(6 * _nbytes((tm, S5_WIDTH), F32), 12 * _nbytes((tm, S5_WIDTH), F32)),
    )(*_hbm(ymm, u, d_skip, w_glu, b_glu, dout))


def _merge_fn(ga, gb, ya, yb):
    return jax.nn.sigmoid(ga) * ya + jax.nn.sigmoid(gb) * yb


def _merge(gab, ya, yb, *, name, tm=512):
    n_tok = ya.shape[0]
    tm = min(tm, n_tok)
    rows = pl.BlockSpec((tm, D_MODEL), lambda i: (i, 0))

    def body(ga_ref, gb_ref, ya_ref, yb_ref, o_ref):
        o_ref[...] = _merge_fn(ga_ref[...], gb_ref[...], ya_ref[...], yb_ref[...]).astype(o_ref.dtype)

    return pl.pallas_call(
        body, name=name, grid=(n_tok // tm,),
        in_specs=[rows, pl.BlockSpec((tm, D_MODEL), lambda i: (i, 1)), rows, rows], out_specs=rows,
        out_shape=jax.ShapeDtypeStruct(ya.shape, MXU_DTYPE),
        compiler_params=_params(5 * _nbytes((tm, D_MODEL), F32), 4 * _nbytes((tm, D_MODEL), F32)),
    )(*_hbm(gab, gab, ya, yb))


def _merge_bwd(gab, ya, yb, dout, *, name, tm=512):
    n_tok = ya.shape[0]
    tm = min(tm, n_tok)
    rows = pl.BlockSpec((tm, D_MODEL), lambda i: (i, 0))

    def body(ga_ref, gb_ref, ya_ref, yb_ref, do_ref, *out_refs):
        _, pull = jax.vjp(_merge_fn, ga_ref[...], gb_ref[...], ya_ref[...], yb_ref[...])
        for ref, val in zip(out_refs, pull(do_ref[...])):
            ref[...] = val.astype(ref.dtype)

    shape = jax.ShapeDtypeStruct(ya.shape, MXU_DTYPE)
    return pl.pallas_call(
        body, name=name, grid=(n_tok // tm,),
        in_specs=[rows, pl.BlockSpec((tm, D_MODEL), lambda i: (i, 1)), rows, rows, rows],
        out_specs=(rows, rows, rows, rows), out_shape=(shape, shape, shape, shape),
        compiler_params=_params(7 * _nbytes((tm, D_MODEL), F32), 6 * _nbytes((tm, D_MODEL), F32)),
    )(*_hbm(gab, gab, ya, yb, dout))


ADA_SHARD = N_MOD * D_MODEL // N_SHARD


def _ada_fwd(c_pad, w_s, b_s, *, name):
    n_r = c_pad.shape[0]

    def body(c_ref, w_ref, b_ref, o_ref):
        sc = jax.nn.silu(c_ref[...]).astype(MXU_DTYPE)
        o_ref[0] = lax.dot_general(sc, w_ref[0].astype(MXU_DTYPE), _NN, preferred_element_type=F32) + b_ref[0]

    return pl.pallas_call(
        body, name=name, grid=(N_SHARD,),
        in_specs=[pl.BlockSpec((n_r, D_MODEL), lambda s: (0, 0)),
                  pl.BlockSpec((1, D_MODEL, ADA_SHARD), lambda s: (s, 0, 0)),
                  pl.BlockSpec((1, 1, ADA_SHARD), lambda s: (s, 0, 0))],
        out_specs=pl.BlockSpec((1, n_r, ADA_SHARD), lambda s: (s, 0, 0)),
        out_shape=jax.ShapeDtypeStruct((N_SHARD, n_r, ADA_SHARD), F32),
        compiler_params=_params(_nbytes((D_MODEL, ADA_SHARD), w_s.dtype), 1 << 20),
    )(*_hbm(c_pad, w_s, b_s))


def _ada_bwd(c_pad, dmod_s, *, name):
    n_r = c_pad.shape[0]

    def body(c_ref, d_ref, dw_ref, db_ref):
        sc = jax.nn.silu(c_ref[...])
        dm = d_ref[0]
        dw_ref[0] = _fdot(sc, dm, _TN)
        db_ref[0] = jnp.sum(dm, axis=0, keepdims=True)

    return pl.pallas_call(
        body, name=name, grid=(N_SHARD,),
        in_specs=[pl.BlockSpec((n_r, D_MODEL), lambda s: (0, 0)), pl.BlockSpec((1, n_r, ADA_SHARD), lambda s: (s, 0, 0))],
        out_specs=(pl.BlockSpec((1, D_MODEL, ADA_SHARD), lambda s: (s, 0, 0)),
                   pl.BlockSpec((1, 1, ADA_SHARD), lambda s: (s, 0, 0))),
        out_shape=(jax.ShapeDtypeStruct((N_SHARD, D_MODEL, ADA_SHARD), F32),
                   jax.ShapeDtypeStruct((N_SHARD, 1, ADA_SHARD), F32)),
        compiler_params=_params(_nbytes((D_MODEL, ADA_SHARD), F32), 2 * _nbytes((D_MODEL, ADA_SHARD), F32)),
    )(*_hbm(c_pad, dmod_s))


def _block_diag(blocks):
    n_per = S5_GROUPS // S5_BLOCKS
    _, n_r, n_c = blocks.shape
    b4 = blocks.reshape(S5_BLOCKS, n_per, n_r, n_c)
    eye = jnp.eye(n_per, dtype=blocks.dtype)
    return (b4[:, :, :, None, :] * eye[None, :, None, :, None]).reshape(S5_BLOCKS, n_per * n_r, n_per * n_c)


def _diag_blocks(mat, n_r, n_c):
    n_per = S5_GROUPS // S5_BLOCKS
    m5 = mat.reshape(S5_BLOCKS, n_per, n_r, n_per, n_c)
    eye = jnp.eye(n_per, dtype=mat.dtype)
    return jnp.sum(m5 * eye[None, :, None, :, None], axis=3).reshape(S5_GROUPS, n_r, n_c)


def _local_step(x, c, target, wts):
    n_b, n_s, _ = x.shape
    n_tok = n_b * n_s
    flat = lambda t: t.reshape(n_tok, t.shape[-1])
    unflat = lambda t: t.reshape(n_b, n_s, t.shape[-1])
    n_chunks = n_s // CHUNK

    c_pad = jnp.zeros((SUBLANES, D_MODEL), F32).at[:n_b].set(c)
    mod_s = _ada_fwd(c_pad, wts["w_ada"], wts["b_ada"], name="ada_fwd")
    mod = mod_s.transpose(1, 0, 2).reshape(SUBLANES, N_MOD * D_MODEL)[:n_b]
    sh1, sc1, gt1, sh2, sc2, gt2, sh3, sc3, gt3 = [m[:, None, :] for m in jnp.split(mod, N_MOD, axis=-1)]

    a1 = _pre(x, None, None, wts["g_ffn1"], sh1, sc1, 0.0, name="pre1")
    f1, ffn1_saved = _ffn_fwd(flat(a1), wts["w1_ffn1"], wts["w3_ffn1"], wts["w2_ffn1"], "ffn1")
    x1, a2 = _pre(x, unflat(f1), gt1, wts["g_mix"], sh2, sc2, 0.5, name="pre2")
    u = flat(a2)[None]
    p_qkv = _mm_act([(u, wts["w_qkv"])], "nt", name="in_qkv")[0]
    p_z = _mm_act([(u, wts["w_z"])], "nt", name="in_z")[0]
    p_gab = _mm_act([(u, wts["w_gab"])], "nt", name="in_gab")[0]
    p_s5 = _mm_act([(u, wts["w_s5"])], "nt", name="in_s5")[0]
    p_ba = _mm_act([(u, wts["w_ba"])], "nt", name="in_ba")[0]

    qkv_c = _conv_fwd(unflat(p_qkv), wts["conv_qkv"], name="conv_fwd")
    z_tok = unflat(p_z)
    ba = p_ba.reshape(n_b, n_s, BA_PAD)
    head_rows = lambda t: t.transpose(0, 2, 1).reshape(n_b, DN_HEADS, n_chunks, 1, CHUNK)
    blr = head_rows(ba[:, :, :DN_HEADS])
    lar = head_rows(ba[:, :, DN_HEADS:2 * DN_HEADS])
    a_log, dt_bias = wts["a_log"], wts["dt_bias"]
    dn_in = (qkv_c, blr, lar, a_log, dt_bias)
    qd, kd, uc, wc, attn, g_last, dn_inv = _dn_prep(*dn_in, name="dn_prep")
    o, states = _dn_scan(qd, kd, uc, wc, attn, g_last, name="dn_scan")
    og = _dn_post(o, z_tok, wts["g_onorm"], name="dn_post")
    og_t = og.reshape(1, n_tok, DN_WIDTH)
    ya = _mm_act([(og_t, wts["w_proj_a"])], "nn", name="proj_a")[0]

    s5p_in = (wts["lam_re"], wts["lam_im"], wts["log_step"], wts["bt_re"], wts["bt_im"], wts["c_im"])
    lb_re, lb_im, bb_re, bb_im, c_neg = _s5_prep(*s5p_in, name="s5_prep")
    wb_re, wb_im = _block_diag(bb_re), _block_diag(bb_im)
    wc_re = _block_diag(wts["c_re"].transpose(0, 2, 1))
    wc_im = _block_diag(c_neg.transpose(0, 2, 1))
    lbr, lbi = lb_re.reshape(1, S5_LANES), lb_im.reshape(1, S5_LANES)
    s5_w = (wb_re, wb_im, lbr, lbi, wc_re, wc_im)
    ymm, x_re, x_im = _s5_mix(unflat(p_s5), *s5_w, name="s5_mix")
    ymm = ymm.reshape(n_tok, S5_WIDTH)
    y2 = _s5_out(ymm, p_s5, wts["d_skip"], wts["w_glu"], wts["b_glu"], name="s5_out")
    yb = _mm_act([(y2[None], wts["w_proj_b"])], "nn", name="proj_b")[0]

    merged = _merge(p_gab, ya, yb, name="merge")
    m_out = _mm_act([(merged[None], wts["w_out"])], "nn", name="mix_out")[0]
    x2, a3 = _pre(x1, unflat(m_out), gt2, wts["g_ffn2"], sh3, sc3, 1.0, name="pre3")
    f3, ffn2_saved = _ffn_fwd(flat(a3), wts["w1_ffn2"], wts["w3_ffn2"], wts["w2_ffn2"], "ffn2")

    g = {}
    loss, dx2_res, df3, dgt3, g["g_final"] = _final(x2, unflat(f3), gt3, wts["g_final"], target, name="final")
    da3, g["w1_ffn2"], g["w3_ffn2"], g["w2_ffn2"] = _ffn_bwd(
        flat(a3), wts["w1_ffn2"], wts["w3_ffn2"], wts["w2_ffn2"], ffn2_saved, flat(df3), "ffn2")
    dx1_res, dm_out, dgt2, g["g_ffn2"], dsh3, dsc3 = _pre_bwd(
        x1, unflat(m_out), gt2, wts["g_ffn2"], sh3, sc3, 1.0, unflat(da3), dx2_res, name="pre3_bwd")
    dm_out = flat(dm_out)[None]
    dmerged = _mm_act([(dm_out, wts["w_out"])], "nt", name="mix_out_bwd")[0]
    g["w_out"] = _mm_tn(merged[None], dm_out, name="dw_out")[0]
    dga, dgb, dya, dyb = _merge_bwd(p_gab, ya, yb, dmerged, name="merge_bwd")

    dy2 = _mm_act([(dyb[None], wts["w_proj_b"])], "nt", name="proj_b_bwd")[0]
    g["w_proj_b"] = _mm_tn(y2[None], dyb[None], name="dw_proj_b")[0]
    dymm, du_skip, g["d_skip"], g["w_glu"], g["b_glu"] = _s5_out_bwd(
        ymm, p_s5, wts["d_skip"], wts["w_glu"], wts["b_glu"], dy2, name="s5_out_bwd")
    dp_s5, dwb_re, dwb_im, dlb_re, dlb_im, dwc_re, dwc_im = _s5_mix_bwd(
        unflat(dymm), unflat(du_skip), unflat(p_s5), x_re, x_im, *s5_w, name="s5_mix_bwd")
    dp_s5 = dp_s5.reshape(n_tok, S5_WIDTH)
    g["c_re"] = _diag_blocks(dwc_re, S5_STATE, S5_GROUP_CH).transpose(0, 2, 1)
    s5_cts = (dlb_re.reshape(lb_re.shape), dlb_im.reshape(lb_im.shape),
              _diag_blocks(dwb_re, S5_GROUP_CH, S5_STATE), _diag_blocks(dwb_im, S5_GROUP_CH, S5_STATE),
              _diag_blocks(dwc_im, S5_STATE, S5_GROUP_CH).transpose(0, 2, 1))
    g["lam_re"], g["lam_im"], g["log_step"], g["bt_re"], g["bt_im"], g["c_im"] = _s5_prep_bwd(
        *s5p_in, s5_cts, name="s5_prep_bwd")

    dog = _mm_act([(dya[None], wts["w_proj_a"])], "nt", name="proj_a_bwd")[0]
    g["w_proj_a"] = _mm_tn(og_t, dya[None], name="dw_proj_a")[0]
    do, dz, g["g_onorm"] = _dn_post_bwd(o, z_tok, wts["g_onorm"], unflat(dog), name="dn_post_bwd")
    scan_cts = _dn_scan_bwd(qd, kd, uc, wc, attn, g_last, states, do, name="dn_scan_bwd")
    dqkv_c, dblr, dlar, g["a_log"], g["dt_bias"] = _dn_prep_bwd(*dn_in, dn_inv, uc, wc, scan_cts, name="dn_prep_bwd")
    dqkv, g["conv_qkv"] = _conv_bwd(unflat(p_qkv), wts["conv_qkv"], dqkv_c, name="conv_bwd")
    token_cols = lambda t: t.reshape(n_b, DN_HEADS, n_s).transpose(0, 2, 1)
    dba = jnp.concatenate([token_cols(dblr), token_cols(dlar),
                           jnp.zeros((n_b, n_s, BA_PAD - 2 * DN_HEADS), F32)], axis=-1).astype(MXU_DTYPE)

    dps = {"w_qkv": flat(dqkv)[None], "w_z": flat(dz)[None], "w_ga": dga[None], "w_gb": dgb[None],
           "w_s5": dp_s5[None], "w_ba": flat(dba)[None]}
    w_ga, w_gb = wts["w_gab"][:, :D_MODEL], wts["w_gab"][:, D_MODEL:]
    w_of = dict(wts, w_ga=w_ga, w_gb=w_gb)
    du = _mm_act([(dps[k], w_of[k]) for k in dps], "nn", name="in_bwd")[0]
    for k in dps:
        g[k] = _mm_tn(dps[k], u, name=f"d{k}")[0]
    dx0_res, df1, dgt1, g["g_mix"], dsh2, dsc2 = _pre_bwd(
        x, unflat(f1), gt1, wts["g_mix"], sh2, sc2, 0.5, unflat(du), dx1_res, name="pre2_bwd")
    da1, g["w1_ffn1"], g["w3_ffn1"], g["w2_ffn1"] = _ffn_bwd(
        flat(a1), wts["w1_ffn1"], wts["w3_ffn1"], wts["w2_ffn1"], ffn1_saved, flat(df1), "ffn1")
    grad_x, g["g_ffn1"], dsh1, dsc1 = _pre_bwd(
        x, None, None, wts["g_ffn1"], sh1, sc1, 0.0, unflat(da1), dx0_res, name="pre1_bwd")

    dmod = jnp.concatenate([t[:, 0, :] for t in (dsh1, dsc1, dgt1, dsh2, dsc2, dgt2, dsh3, dsc3, dgt3)], axis=-1)
    return loss, grad_x, g, dmod


def _ada_grads(c_rows, dmod_rows):
    n_r = c_rows.shape[0]
    n_pad = -n_r % SUBLANES
    c_pad = jnp.pad(c_rows, ((0, n_pad), (0, 0)))
    dmod_s = jnp.pad(dmod_rows, ((0, n_pad), (0, 0))).reshape(n_r + n_pad, N_SHARD, ADA_SHARD).transpose(1, 0, 2)
    dw, db = _ada_bwd(c_pad, dmod_s, name="ada_bwd")
    return dw, db.reshape(1, N_MOD * D_MODEL)


IN_SPLITS = (("w_qkv", 3 * DN_WIDTH), ("w_z", DN_WIDTH), ("w_ba", 2 * DN_HEADS), ("w_s5", S5_WIDTH),
             ("w_ga", D_MODEL), ("w_gb", D_MODEL))
SHARDED = ("w_ada", "w1_ffn1", "w3_ffn1", "w2_ffn1", "w_in", "conv_qkv", "w_glu", "w_proj_a", "w_proj_b", "w_out",
           "w1_ffn2", "w3_ffn2", "w2_ffn2")


def _cat_columns(stack):
    return stack.transpose(1, 0, 2).reshape(stack.shape[1], N_SHARD * stack.shape[2])


def _split_columns(full):
    n_r, n_c = full.shape
    return full.reshape(n_r, N_SHARD, n_c // N_SHARD).transpose(1, 0, 2)


def _gathered_weights(st, rep):
    w = {k: st[k] for k in ("w_ada", "w1_ffn1", "w3_ffn1", "w2_ffn1", "w1_ffn2", "w3_ffn2", "w2_ffn2")}
    w["b_ada"] = rep["b_ada"].reshape(N_SHARD, 1, ADA_SHARD)
    for k in ("g_ffn1", "g_mix", "g_ffn2", "g_final"):
        w[k] = rep[k].reshape(1, D_MODEL)
    w_in_t = st["w_in"].reshape(N_SHARD * st["w_in"].shape[1], D_MODEL)
    start = 0
    for k, size in IN_SPLITS:
        w[k] = w_in_t[None, start:start + size]
        start += size
    w["w_gab"] = jnp.concatenate([w.pop("w_ga"), w.pop("w_gb")], axis=1)
    w["w_ba"] = jnp.pad(w["w_ba"], ((0, 0), (0, BA_PAD - 2 * DN_HEADS), (0, 0)))
    w["conv_qkv"] = _cat_columns(st["conv_qkv"])
    w["a_log"] = rep["a_log"].reshape(DN_HEADS, 1, 1)
    w["dt_bias"] = rep["dt_bias"].reshape(DN_HEADS, 1, 1)
    w["g_onorm"] = rep["g_onorm"].reshape(1, DN_HEAD_DIM)
    w["lam_re"] = rep["lam_re"].reshape(S5_GROUPS, 1, S5_STATE)
    w["lam_im"] = rep["lam_im"].reshape(S5_GROUPS, 1, S5_STATE)
    w["log_step"] = rep["log_step"].reshape(S5_GROUPS, 1, 1)
    w["bt_re"] = rep["b_re"][0].transpose(0, 2, 1)
    w["bt_im"] = rep["b_im"][0].transpose(0, 2, 1)
    w["c_re"] = rep["c_re"][0]
    w["c_im"] = rep["c_im"][0]
    w["d_skip"] = rep["d_skip"].reshape(1, S5_WIDTH)
    w["b_glu"] = rep["b_glu"].reshape(1, S5_WIDTH)
    w["w_glu"] = st["w_glu"].reshape(S5_WIDTH, S5_WIDTH)
    w["w_proj_a"] = _cat_columns(st["w_proj_a"])[None]
    w["w_proj_b"] = _cat_columns(st["w_proj_b"])[None]
    w["w_out"] = st["w_out"].reshape(1, D_MODEL, D_MODEL)
    return w


def _grads_to_problem_layout(g):
    st = {k: g[k] for k in ("w1_ffn1", "w3_ffn1", "w2_ffn1", "w1_ffn2", "w3_ffn2", "w2_ffn2")}
    w_in_t = jnp.concatenate([g[k][:size] for k, size in IN_SPLITS], axis=0)
    st["w_in"] = w_in_t.reshape(N_SHARD, w_in_t.shape[0] // N_SHARD, D_MODEL)
    st["w_glu"] = g["w_glu"].reshape(N_SHARD, S5_WIDTH // N_SHARD, S5_WIDTH)
    st["w_proj_a"] = _split_columns(g["w_proj_a"])
    st["w_proj_b"] = _split_columns(g["w_proj_b"])
    st["w_out"] = g["w_out"].reshape(N_SHARD, D_MODEL // N_SHARD, D_MODEL)
    small = {
        "g_ffn1": g["g_ffn1"], "g_mix": g["g_mix"], "g_ffn2": g["g_ffn2"], "g_final": g["g_final"].reshape(D_MODEL),
        "conv_qkv": g["conv_qkv"][None],
        "a_log": g["a_log"].reshape(1, DN_HEADS), "dt_bias": g["dt_bias"].reshape(1, DN_HEADS),
        "g_onorm": g["g_onorm"],
        "lam_re": g["lam_re"].reshape(1, S5_GROUPS, S5_STATE), "lam_im": g["lam_im"].reshape(1, S5_GROUPS, S5_STATE),
        "log_step": g["log_step"].reshape(1, S5_GROUPS),
        "b_re": g["bt_re"].transpose(0, 2, 1)[None], "b_im": g["bt_im"].transpose(0, 2, 1)[None],
        "c_re": g["c_re"][None], "c_im": g["c_im"][None],
        "d_skip": g["d_skip"], "b_glu": g["b_glu"],
    }
    return st, small


ELEMENTWISE_BLOCK_BYTES = 1 << 20


def _row_tile(n_rows, n_cols, n_lead=1, multiple=SUBLANES):
    best = None
    for t in range(multiple, n_rows + 1, multiple):
        if n_rows % t == 0 and n_lead * t * n_cols * 4 <= ELEMENTWISE_BLOCK_BYTES:
            best = t
    return best if best is not None else n_rows


def _add_sibling_half(g4, recv, my_c, *, name):
    n_sh, _, n_h, n_c = g4.shape
    th = _row_tile(n_h, n_c, multiple=2 * SUBLANES)

    def body(c_ref, g_ref, r_ref, o_ref):
        o_ref[0] = (g_ref[0, 0] + r_ref[0]).astype(o_ref.dtype)

    grid_spec = pltpu.PrefetchScalarGridSpec(
        num_scalar_prefetch=1, grid=(n_sh, n_h // th),
        in_specs=[pl.BlockSpec((1, 1, th, n_c), lambda s, i, c_ref: (s, c_ref[0], i, 0)),
                  pl.BlockSpec((1, th, n_c), lambda s, i, c_ref: (s, i, 0))],
        out_specs=pl.BlockSpec((1, th, n_c), lambda s, i, c_ref: (s, i, 0)))
    return pl.pallas_call(
        body, name=name, grid_spec=grid_spec, out_shape=jax.ShapeDtypeStruct((n_sh, n_h, n_c), MXU_DTYPE),
        compiler_params=_params(3 * _nbytes((th, n_c), F32)),
    )(*_hbm(my_c, g4, recv))


def _sum_slots(parts, *, name):
    n_p, n_r, n_c = parts.shape
    th = _row_tile(n_r, n_c, n_p)

    def body(p_ref, o_ref):
        total = p_ref[0].astype(F32)
        for k in range(1, n_p):
            total = total + p_ref[k].astype(F32)
        o_ref[...] = total

    return pl.pallas_call(
        body, name=name, grid=(n_r // th,),
        in_specs=[pl.BlockSpec((n_p, th, n_c), lambda i: (0, i, 0))],
        out_specs=pl.BlockSpec((th, n_c), lambda i: (i, 0)),
        out_shape=jax.ShapeDtypeStruct((n_r, n_c), F32),
        compiler_params=_params((n_p + 1) * _nbytes((th, n_c), F32)),
    )(*_hbm(parts))


def _cast_into_slot(w, place, dtype, *, name):
    n_r, n_c = w.shape
    th = _row_tile(n_r, n_c, multiple=2 * SUBLANES)

    def body(p_ref, w_ref, o_ref):
        o_ref[0] = w_ref[...].astype(o_ref.dtype)

    grid_spec = pltpu.PrefetchScalarGridSpec(
        num_scalar_prefetch=1, grid=(n_r // th,),
        in_specs=[pl.BlockSpec((th, n_c), lambda i, p: (i, 0))],
        out_specs=pl.BlockSpec((1, th, n_c), lambda i, p: (p[1], i, 0)))
    return pl.pallas_call(
        body, name=name, grid_spec=grid_spec, out_shape=jax.ShapeDtypeStruct((N_SHARD, n_r, n_c), dtype),
        compiler_params=_params(2 * _nbytes((th, n_c), F32)),
    )(*_hbm(place, w))


def _sum_chips(own, parts, place, *, name):
    n_sh, n_h, n_c = own.shape
    th = _row_tile(n_h, n_c, n_sh, multiple=2 * SUBLANES)

    def body(p_ref, own_ref, a_ref, b_ref, c_ref, o_ref):
        o_ref[0] = ((own_ref[0].astype(F32) + a_ref[0].astype(F32)) + b_ref[0].astype(F32)) + c_ref[0].astype(F32)

    slab = lambda k: pl.BlockSpec((1, th, n_c), lambda i, p, k=k: (p[k], i, 0))
    grid_spec = pltpu.PrefetchScalarGridSpec(
        num_scalar_prefetch=1, grid=(n_h // th,),
        in_specs=[slab(1), slab(2), slab(3), slab(4)], out_specs=slab(0))
    return pl.pallas_call(
        body, name=name, grid_spec=grid_spec, out_shape=jax.ShapeDtypeStruct((2, n_h, n_c), F32),
        compiler_params=_params(5 * _nbytes((th, n_c), F32)),
    )(*_hbm(place, own, parts, parts, parts))


def _adamw(w, g, m, v, *, name):
    n_r, n_c = w.shape
    th = _row_tile(n_r, n_c)
    tc = n_c
    if th == n_r and n_c % LANES == 0:
        tc = max(t for t in range(LANES, n_c + 1, LANES) if n_c % t == 0 and (n_r * t * 4 <= ELEMENTWISE_BLOCK_BYTES or t == LANES))
    bias1 = 1.0 - ADAM_B1 ** ADAM_STEP
    bias2 = 1.0 - ADAM_B2 ** ADAM_STEP

    def body(w_ref, g_ref, m_ref, v_ref, d_ref, mo_ref, vo_ref):
        gv = g_ref[...]
        m_new = ADAM_B1 * m_ref[...] + (1.0 - ADAM_B1) * gv
        v_new = ADAM_B2 * v_ref[...] + (1.0 - ADAM_B2) * jnp.square(gv)
        d_ref[...] = -ADAM_LR * ((m_new / bias1) / (jnp.sqrt(v_new / bias2) + ADAM_EPS) + ADAM_WD * w_ref[...])
        mo_ref[...] = m_new
        vo_ref[...] = v_new

    spec = pl.BlockSpec((th, tc), lambda i, j: (i, j))
    shape = jax.ShapeDtypeStruct((n_r, n_c), F32)
    return pl.pallas_call(
        body, name=name, grid=(n_r // th, n_c // tc), in_specs=[spec] * 4, out_specs=(spec,) * 3, out_shape=(shape,) * 3,
        compiler_params=_params(7 * _nbytes((th, tc), F32)),
    )(*_hbm(w, g, m, v))


CHIP_FLIPS = ((1, 0), (0, 1), (1, 1))
DEVICE_FLIPS = tuple((fx, fy, fc) for fx in (0, 1) for fy in (0, 1) for fc in (0, 1))[1:]


def _exchange(ins, out_shapes, plan, n_local, n_remote, *, name, aliased=False):
    n_in, n_out = len(ins), len(out_shapes)

    def body(*refs):
        in_refs, out_refs = refs[:n_in], refs[n_in:n_in + n_out]
        send_sems, recv_sems, local_sems = refs[n_in + n_out:]
        me = (lax.axis_index("x"), lax.axis_index("y"), lax.axis_index("c"))
        local, remote = plan(in_refs, out_refs, me)
        assert len(local) == n_local and len(remote) == n_remote
        here = [pltpu.make_async_copy(src, dst, local_sems.at[i]) for i, (src, dst) in enumerate(local)]
        for cp in here:
            cp.start()
        sends = [pltpu.make_async_remote_copy(src_ref=src, dst_ref=dst, send_sem=send_sems.at[i], recv_sem=recv_sems.at[i],
                                              device_id=peer, device_id_type=pl.DeviceIdType.MESH)
                 for i, (src, dst, _, peer) in enumerate(remote)]
        for cp in sends:
            cp.start()
        for i, (src, _, landing, peer) in enumerate(remote):
            pltpu.make_async_remote_copy(src_ref=src, dst_ref=landing, send_sem=send_sems.at[i], recv_sem=recv_sems.at[i],
                                         device_id=peer, device_id_type=pl.DeviceIdType.MESH).wait_recv()
        for cp in sends:
            cp.wait_send()
        for cp in here:
            cp.wait()

    any_spec = pl.BlockSpec(memory_space=pl.ANY)
    return pl.pallas_call(
        body, name=name, in_specs=[any_spec] * n_in, out_specs=tuple([any_spec] * n_out), out_shape=tuple(out_shapes),
        scratch_shapes=[pltpu.SemaphoreType.DMA((n_remote,)), pltpu.SemaphoreType.DMA((n_remote,)),
                        pltpu.SemaphoreType.DMA((max(n_local, 1),))],
        input_output_aliases={k: k for k in range(n_in)} if aliased else {},
    )(*ins)


def _gather_shards(stacks, *, name):
    n = len(stacks)
    halved = [a.shape[1] % 64 == 0 for a in stacks]
    unit_rows = [a.shape[1] // 2 if h else a.shape[1] for a, h in zip(stacks, halved)]
    part1_rows = [(r // 32) * 16 if r >= 32 else r for r in unit_rows]
    has_part2 = [p < r for p, r in zip(part1_rows, unit_rows)]
    n_sem = sum(2 + 1 + int(h2) + 3 * int(h) for h2, h in zip(has_part2, halved))

    def body(*refs):
        outs = refs[n:2 * n]
        send_sems, recv_sems = refs[2 * n:]
        x, y, c = lax.axis_index("x"), lax.axis_index("y"), lax.axis_index("c")
        mine, chip_x, chip_y, chip_d = 2 * x + y, 2 * (1 - x) + y, 2 * x + (1 - y), 2 * (1 - x) + (1 - y)
        to_x, to_y, sibling = (1 - x, y, c), (x, 1 - y, c), (x, y, 1 - c)

        def region(k, slot, half, part=None):
            start = half * unit_rows[k] if halved[k] else 0
            size = unit_rows[k]
            if part == 1:
                size = part1_rows[k]
            elif part == 2:
                start, size = start + part1_rows[k], unit_rows[k] - part1_rows[k]
            if not halved[k] and part is None:
                return outs[k].at[slot]
            if halved[k]:
                start = pl.multiple_of(start, 16)
            return outs[k].at[slot, pl.ds(start, size)]

        counter = [0]
        started, pending = [], []

        def send(region_of, peer, landing_of):
            i = counter[0]
            counter[0] += 1
            src = region_of
            cp = pltpu.make_async_remote_copy(src_ref=src, dst_ref=src, send_sem=send_sems.at[i], recv_sem=recv_sems.at[i],
                                              device_id=peer, device_id_type=pl.DeviceIdType.MESH)
            cp.start()
            started.append(cp)
            return pltpu.make_async_remote_copy(src_ref=landing_of, dst_ref=landing_of, send_sem=send_sems.at[i],
                                                recv_sem=recv_sems.at[i], device_id=peer, device_id_type=pl.DeviceIdType.MESH)

        from_x = [send(region(k, mine, c), to_x, region(k, chip_x, c)) for k in range(n)]
        from_y = [send(region(k, mine, c), to_y, region(k, chip_y, c)) for k in range(n)]
        diag = []
        for k in range(n):
            from_x[k].wait_recv()
            fwd = [send(region(k, chip_x, c, 1), to_y, region(k, chip_d, c, 1))]
            if halved[k]:
                pending.append(send(region(k, chip_x, c), sibling, region(k, chip_x, 1 - c)))
            from_y[k].wait_recv()
            if has_part2[k]:
                fwd.append(send(region(k, chip_y, c, 2), to_x, region(k, chip_d, c, 2)))
            if halved[k]:
                pending.append(send(region(k, chip_y, c), sibling, region(k, chip_y, 1 - c)))
            diag.append(fwd)
        for k in range(n):
            for landed in diag[k]:
                landed.wait_recv()
            if halved[k]:
                pending.append(send(region(k, chip_d, c), sibling, region(k, chip_d, 1 - c)))
        for landed in pending:
            landed.wait_recv()
        for cp in started:
            cp.wait_send()
        assert counter[0] == n_sem

    any_spec = pl.BlockSpec(memory_space=pl.ANY)
    return pl.pallas_call(
        body, name=name, in_specs=[any_spec] * n, out_specs=tuple([any_spec] * n),
        out_shape=tuple(jax.ShapeDtypeStruct(a.shape, a.dtype) for a in stacks),
        scratch_shapes=[pltpu.SemaphoreType.DMA((n_sem,)), pltpu.SemaphoreType.DMA((n_sem,))],
        input_output_aliases={k: k for k in range(n)},
    )(*stacks)


def _swap_sibling_halves(g4s, *, name):
    n = len(g4s)

    def plan(in_refs, out_refs, me):
        x, y, c = me
        remote = [(in_refs[k].at[:, 1 - c], out_refs[k], out_refs[k], (x, y, 1 - c)) for k in range(n)]
        return [], remote

    shapes = [jax.ShapeDtypeStruct((a.shape[0],) + a.shape[2:], a.dtype) for a in g4s]
    return _exchange(g4s, shapes, plan, 0, n, name=name)


def _scatter_to_chips(hs, *, name):
    n = len(hs)

    def plan(in_refs, out_refs, me):
        x, y, c = me
        mine = 2 * x + y
        remote = []
        for fx, fy in CHIP_FLIPS:
            px, py = x ^ fx, y ^ fy
            peer = 2 * px + py
            for k in range(n):
                remote.append((in_refs[k].at[peer], out_refs[k].at[mine], out_refs[k].at[peer], (px, py, c)))
        return [], remote

    shapes = [jax.ShapeDtypeStruct(a.shape, a.dtype) for a in hs]
    return _exchange(hs, shapes, plan, 0, len(CHIP_FLIPS) * n, name=name)


def _join_sibling_halves(rs, *, name):
    n = len(rs)

    def plan(in_refs, out_refs, me):
        x, y, c = me
        remote = [(out_refs[k].at[c], out_refs[k].at[c], out_refs[k].at[1 - c], (x, y, 1 - c)) for k in range(n)]
        return [], remote

    shapes = [jax.ShapeDtypeStruct(a.shape, a.dtype) for a in rs]
    return _exchange(rs, shapes, plan, 0, n, name=name, aliased=True)


def _gather_all_devices(packed, *, name):
    def plan(in_refs, out_refs, me):
        x, y, c = me
        mine = 4 * x + 2 * y + c
        remote = []
        for fx, fy, fc in DEVICE_FLIPS:
            px, py, pc = x ^ fx, y ^ fy, c ^ fc
            remote.append((in_refs[0], out_refs[0].at[mine], out_refs[0].at[4 * px + 2 * py + pc], (px, py, pc)))
        return [(in_refs[0], out_refs[0].at[mine])], remote

    shape = jax.ShapeDtypeStruct((2 * N_SHARD,) + packed.shape, packed.dtype)
    return _exchange([packed], [shape], plan, 1, len(DEVICE_FLIPS), name=name)[0]


WEIGHT_NAMES = ("w_ada", "b_ada", "g_ffn1", "w1_ffn1", "w3_ffn1", "w2_ffn1", "g_mix", "w_in", "conv_qkv", "a_log",
                "dt_bias", "g_onorm", "lam_re", "lam_im", "log_step", "b_re", "b_im", "c_re", "c_im", "d_skip", "w_glu",
                "b_glu", "w_proj_a", "w_proj_b", "w_out", "g_ffn2", "w1_ffn2", "w3_ffn2", "w2_ffn2", "g_final")
LARGE = tuple(n for n in SHARDED if n != "conv_qkv")
SMALL = tuple(n for n in WEIGHT_NAMES if n not in LARGE)
REDUCED_LARGE = tuple(n for n in LARGE if n != "w_ada")
REDUCED_SMALL = tuple(n for n in SMALL if n != "b_ada")
PACK_ROW = SUBLANES * LANES


def _pack(arrays):
    flat = jnp.concatenate([a.reshape(-1) for a in arrays])
    n_pad = -flat.shape[0] % PACK_ROW
    return jnp.pad(flat, (0, n_pad)).reshape(-1, LANES)


def _unpack(packed, shapes):
    flat = packed.reshape(-1)
    out, start = [], 0
    for s in shapes:
        size = math.prod(s)
        out.append(flat[start:start + size].reshape(s))
        start += size
    return out


def _unpack_slots(gathered, shapes):
    flat = gathered.reshape(gathered.shape[0], -1)
    out, start = [], 0
    for s in shapes:
        size = math.prod(s)
        out.append(flat[:, start:start + size].reshape((gathered.shape[0],) + tuple(s)))
        start += size
    return out


TRANSPOSED = ("w1_ffn1", "w3_ffn1", "w1_ffn2", "w3_ffn2", "w_in")


def _to_internal(name, a):
    return jnp.swapaxes(a[0], 0, 1) if name in TRANSPOSED else a[0]


def _from_internal(name, a):
    return (jnp.swapaxes(a, 0, 1) if name in TRANSPOSED else a)[None]


def _step(x, c, target, weights, m_in, v_in):
    xi, yi, ci = lax.axis_index("x"), lax.axis_index("y"), lax.axis_index("c")
    my_chip = 2 * xi + yi

    others = [k + (k >= my_chip).astype(jnp.int32) for k in range(N_SHARD - 1)]
    place = jnp.stack([ci, my_chip] + others).astype(jnp.int32)

    slots = [_cast_into_slot(_to_internal(n, weights[n]), place, F32 if n == "conv_qkv" else MXU_DTYPE, name=f"cast_{n}")
             for n in SHARDED]
    stacks = dict(zip(SHARDED, _gather_shards(slots, name="gather_weights")))
    rep = {n: weights[n] for n in WEIGHT_NAMES if n not in SHARDED}
    loss, grad_x, g, dmod = _local_step(x, c, target, _gathered_weights(stacks, rep))
    g_stacks, g_small = _grads_to_problem_layout(g)

    g4s = [g_stacks[n].reshape(N_SHARD, 2, g_stacks[n].shape[1] // 2, g_stacks[n].shape[2]) for n in REDUCED_LARGE]
    from_sibling = _swap_sibling_halves(g4s, name="swap_sibling_halves")
    chip_sums = [_add_sibling_half(a, r, place, name=f"chip_sum_{n}") for n, a, r in zip(REDUCED_LARGE, g4s, from_sibling)]
    from_chips = _scatter_to_chips(chip_sums, name="scatter_to_chips")
    reduced = [_sum_chips(h, p, place, name=f"sum_chips_{n}") for n, h, p in zip(REDUCED_LARGE, chip_sums, from_chips)]
    joined = _join_sibling_halves(reduced, name="join_sibling_halves")
    grads_2d = {n: j.reshape(2 * j.shape[1], j.shape[2]) for n, j in zip(REDUCED_LARGE, joined)}
    grads = {n: _from_internal(n, a) for n, a in grads_2d.items()}

    summed_shapes = [g_small[n].shape for n in REDUCED_SMALL] + [(1, 1)]
    packed = _pack([g_small[n] for n in REDUCED_SMALL] + [loss, c, dmod])
    gathered = _gather_all_devices(packed, name="gather_small")
    *small_grads, loss_sum = _unpack(_sum_slots(gathered, name="sum_small"), summed_shapes)
    grads.update(zip(REDUCED_SMALL, small_grads))
    n_conv = weights["conv_qkv"].shape[-1]
    grads["conv_qkv"] = lax.dynamic_slice_in_dim(grads["conv_qkv"], my_chip * n_conv, n_conv, axis=2)
    n_dev = gathered.shape[0]
    rows_of = lambda t: t.reshape(n_dev * t.shape[1], t.shape[2])
    _, c_all, dmod_all = _unpack_slots(gathered, [(sum(math.prod(s) for s in summed_shapes),), c.shape, dmod.shape])
    dw_ada, grads["b_ada"] = _ada_grads(rows_of(c_all), rows_of(dmod_all))
    grads_2d["w_ada"] = lax.dynamic_index_in_dim(dw_ada, my_chip, axis=0, keepdims=False)
    grads["w_ada"] = grads_2d["w_ada"][None]

    delta, new_m, new_v = {}, {}, {}
    grads_2d["conv_qkv"] = grads["conv_qkv"][0]
    for n in LARGE + ("conv_qkv",):
        outs = _adamw(_to_internal(n, weights[n]), grads_2d[n], _to_internal(n, m_in[n]), _to_internal(n, v_in[n]),
                      name=f"adamw_{n}")
        delta[n], new_m[n], new_v[n] = [_from_internal(n, o) for o in outs]
    packed_names = tuple(n for n in SMALL if n != "conv_qkv")
    shapes = [weights[n].shape for n in packed_names]
    outs = _adamw(*[_pack([d[n] for n in packed_names]) for d in (weights, grads, m_in, v_in)], name="adamw_small")
    for d, o in zip((delta, new_m, new_v), outs):
        d.update(zip(packed_names, _unpack(o, shapes)))
    return (loss_sum.reshape(()), grad_x, *[grads[n] for n in WEIGHT_NAMES], *[delta[n] for n in WEIGHT_NAMES],
            *[new_m[n] for n in WEIGHT_NAMES], *[new_v[n] for n in WEIGHT_NAMES])


def kernel(x, c, w_ada, b_ada, g_ffn1, w1_ffn1, w3_ffn1, w2_ffn1, g_mix, w_in, conv_qkv, a_log, dt_bias, g_onorm, lam_re, lam_im, log_step, b_re, b_im, c_re, c_im, d_skip, w_glu, b_glu, w_proj_a, w_proj_b, w_out, g_ffn2, w1_ffn2, w3_ffn2, w2_ffn2, g_final, loss_target, m_w_ada, m_b_ada, m_g_ffn1, m_w1_ffn1, m_w3_ffn1, m_w2_ffn1, m_g_mix, m_w_in, m_conv_qkv, m_a_log, m_dt_bias, m_g_onorm, m_lam_re, m_lam_im, m_log_step, m_b_re, m_b_im, m_c_re, m_c_im, m_d_skip, m_w_glu, m_b_glu, m_w_proj_a, m_w_proj_b, m_w_out, m_g_ffn2, m_w1_ffn2, m_w3_ffn2, m_w2_ffn2, m_g_final, v_w_ada, v_b_ada, v_g_ffn1, v_w1_ffn1, v_w3_ffn1, v_w2_ffn1, v_g_mix, v_w_in, v_conv_qkv, v_a_log, v_dt_bias, v_g_onorm, v_lam_re, v_lam_im, v_log_step, v_b_re, v_b_im, v_c_re, v_c_im, v_d_skip, v_w_glu, v_b_glu, v_w_proj_a, v_w_proj_b, v_w_out, v_g_ffn2, v_w1_ffn2, v_w3_ffn2, v_w2_ffn2, v_g_final):
    w_vals = (w_ada, b_ada, g_ffn1, w1_ffn1, w3_ffn1, w2_ffn1, g_mix, w_in, conv_qkv, a_log, dt_bias, g_onorm, lam_re, lam_im, log_step, b_re, b_im, c_re, c_im, d_skip, w_glu, b_glu, w_proj_a, w_proj_b, w_out, g_ffn2, w1_ffn2, w3_ffn2, w2_ffn2, g_final)
    m_vals = (m_w_ada, m_b_ada, m_g_ffn1, m_w1_ffn1, m_w3_ffn1, m_w2_ffn1, m_g_mix, m_w_in, m_conv_qkv, m_a_log, m_dt_bias, m_g_onorm, m_lam_re, m_lam_im, m_log_step, m_b_re, m_b_im, m_c_re, m_c_im, m_d_skip, m_w_glu, m_b_glu, m_w_proj_a, m_w_proj_b, m_w_out, m_g_ffn2, m_w1_ffn2, m_w3_ffn2, m_w2_ffn2, m_g_final)
    v_vals = (v_w_ada, v_b_ada, v_g_ffn1, v_w1_ffn1, v_w3_ffn1, v_w2_ffn1, v_g_mix, v_w_in, v_conv_qkv, v_a_log, v_dt_bias, v_g_onorm, v_lam_re, v_lam_im, v_log_step, v_b_re, v_b_im, v_c_re, v_c_im, v_d_skip, v_w_glu, v_b_glu, v_w_proj_a, v_w_proj_b, v_w_out, v_g_ffn2, v_w1_ffn2, v_w3_ffn2, v_w2_ffn2, v_g_final)
    return _step(x, c, loss_target, dict(zip(WEIGHT_NAMES, w_vals)), dict(zip(WEIGHT_NAMES, m_vals)),
                 dict(zip(WEIGHT_NAMES, v_vals)))
```

```python
import functools
import math

import jax
import jax.numpy as jnp
from jax import lax
from jax.experimental import pallas as pl
from jax.experimental.pallas import tpu as pltpu

F32 = jnp.float32
BF16 = jnp.bfloat16
MXU_DTYPE = BF16

D_MODEL = 1024
D_FF = 2816
DN_HEADS = 8
DN_HEAD_DIM = 64
DN_WIDTH = DN_HEADS * DN_HEAD_DIM
CONV_WIDTH = 4
CHUNK = 64
S5_GROUP_CH = 16
S5_GROUPS = 32
S5_WIDTH = S5_GROUPS * S5_GROUP_CH
S5_STATE = 64
S5_LANES = S5_GROUPS * S5_STATE
N_MOD = 9
EPS = 1e-6
N_SHARD = 4
FF_SHARD = D_FF // N_SHARD
BA_PAD = 128

ADAM_LR = 0.001
ADAM_B1 = 0.9
ADAM_B2 = 0.999
ADAM_EPS = 1e-08
ADAM_WD = 0.01
ADAM_STEP = 10

VMEM_BYTES_V7X = 64 * 1024 * 1024
SUBLANES = 8
LANES = 128


def _params(block_bytes, extra_bytes=0):
    need = 2 * block_bytes + extra_bytes + (4 << 20)
    return pltpu.CompilerParams(vmem_limit_bytes=int(min(max(need, 16 << 20), VMEM_BYTES_V7X - (8 << 20))))


def _nbytes(shape, dtype):
    return math.prod(shape) * jnp.dtype(dtype).itemsize


HBM_OPERAND_BYTES = 1 << 20


def _hbm(*args):
    return [pltpu.with_memory_space_constraint(a, pltpu.HBM) if _nbytes(a.shape, a.dtype) >= HBM_OPERAND_BYTES else a
            for a in args]


_NN = (((1,), (0,)), ((), ()))
_NT = (((1,), (1,)), ((), ()))
_TN = (((0,), (0,)), ((), ()))


LHS_ROW_BYTES = 4096


def _mm_act(pairs, mode, *, name, reduce_shards=False, out_dtype=F32, tm=None):
    n_tok = pairs[0][0].shape[1]
    n_out = pairs[0][1].shape[2] if mode == "nn" else pairs[0][1].shape[1]
    if tm is None:
        row_bytes = sum(a.shape[2] * jnp.dtype(a.dtype).itemsize for a, _ in pairs)
        tm = 1024 if row_bytes <= LHS_ROW_BYTES else 512
    tm = min(tm, n_tok)
    tn = n_out if n_out <= 1536 else 1024
    assert n_tok % tm == 0 and n_out % tn == 0
    n_red = N_SHARD if reduce_shards else 1
    grid = (n_tok // tm, n_out // tn, n_red)
    dims = _NN if mode == "nn" else _NT
    shard_of = lambda n_sh: (lambda r: 0) if n_sh == 1 else (lambda r: r)

    in_specs, args, blk = [], [], 0
    for a, b in pairs:
        k_dim = a.shape[2]
        sa, sb = shard_of(a.shape[0]), shard_of(b.shape[0])
        in_specs.append(pl.BlockSpec((1, tm, k_dim), lambda i, j, r, sa=sa: (sa(r), i, 0)))
        if mode == "nn":
            assert b.shape[1] == k_dim
            in_specs.append(pl.BlockSpec((1, k_dim, tn), lambda i, j, r, sb=sb: (sb(r), 0, j)))
        else:
            assert b.shape[2] == k_dim
            in_specs.append(pl.BlockSpec((1, tn, k_dim), lambda i, j, r, sb=sb: (sb(r), j, 0)))
        args += [a, b]
        blk += _nbytes((tm, k_dim), a.dtype) + _nbytes((k_dim, tn), b.dtype)
    blk += _nbytes((tm, tn), out_dtype)
    n_pairs = len(pairs)

    def body(*refs):
        out_ref = refs[2 * n_pairs]
        acc = None
        for k in range(n_pairs):
            a = refs[2 * k][0].astype(MXU_DTYPE)
            b = refs[2 * k + 1][0].astype(MXU_DTYPE)
            d = lax.dot_general(a, b, dims, preferred_element_type=F32)
            acc = d if acc is None else acc + d

        if n_red == 1:
            out_ref[0] = acc.astype(out_dtype)
        else:
            acc_ref = refs[-1]
            r = pl.program_id(2)

            @pl.when(r == 0)
            def _():
                acc_ref[...] = acc

            @pl.when(r > 0)
            def _():
                acc_ref[...] += acc

            @pl.when(r == n_red - 1)
            def _():
                out_ref[0] = acc_ref[...].astype(out_dtype)

    return pl.pallas_call(
        body,
        name=name,
        grid=grid,
        in_specs=in_specs,
        out_specs=pl.BlockSpec((1, tm, tn), lambda i, j, r: (0, i, j)),
        out_shape=jax.ShapeDtypeStruct((1, n_tok, n_out), out_dtype),
        scratch_shapes=[pltpu.VMEM((tm, tn), F32)] if n_red > 1 else [],
        compiler_params=_params(blk, 3 * _nbytes((tm, tn), F32)),
    )(*_hbm(*args))


def _mm_tn(a, b, *, name, tt=1024):
    n_tok, k_dim = a.shape[1], a.shape[2]
    n_out = b.shape[2]
    tt = min(tt, n_tok)
    tk = k_dim if k_dim <= 1536 else 1024
    tn = n_out if n_out <= 1536 else 1024
    assert n_tok % tt == 0 and k_dim % tk == 0 and n_out % tn == 0
    n_so = max(a.shape[0], b.shape[0])
    sa = (lambda s: s) if a.shape[0] > 1 else (lambda s: 0)
    sb = (lambda s: s) if b.shape[0] > 1 else (lambda s: 0)
    grid = (n_so, k_dim // tk, n_out // tn, n_tok // tt)

    def body(a_ref, b_ref, out_ref):
        d = lax.dot_general(a_ref[0].astype(MXU_DTYPE), b_ref[0].astype(MXU_DTYPE), _TN, preferred_element_type=F32)
        t = pl.program_id(3)

        @pl.when(t == 0)
        def _():
            out_ref[0] = d

        @pl.when(t > 0)
        def _():
            out_ref[0] += d

    blk = _nbytes((tt, tk), a.dtype) + _nbytes((tt, tn), b.dtype) + _nbytes((tk, tn), F32)
    return pl.pallas_call(
        body,
        name=name,
        grid=grid,
        in_specs=[
            pl.BlockSpec((1, tt, tk), lambda s, ki, nj, t: (sa(s), t, ki)),
            pl.BlockSpec((1, tt, tn), lambda s, ki, nj, t: (sb(s), t, nj)),
        ],
        out_specs=pl.BlockSpec((1, tk, tn), lambda s, ki, nj, t: (s, ki, nj)),
        out_shape=jax.ShapeDtypeStruct((n_so, k_dim, n_out), F32),
        compiler_params=_params(blk, 2 * _nbytes((tk, tn), F32) + _nbytes((tt, tk), F32)),
    )(*_hbm(a, b))


@functools.partial(jax.custom_vjp, nondiff_argnums=(2,))
def _mdot(a, b, dims):
    return lax.dot_general(a.astype(MXU_DTYPE), b.astype(MXU_DTYPE), dims, preferred_element_type=F32)


def _mdot_fwd(a, b, dims):
    return _mdot(a, b, dims), (a, b)


def _mdot_bwd(dims, res, g):
    a, b = res
    (ca, cb), (ba, bb) = dims
    nb = len(ba)
    assert tuple(ba) == tuple(range(nb)) and tuple(bb) == tuple(range(nb)) and len(ca) == 1 and a.ndim == nb + 2
    batch = (tuple(range(nb)), tuple(range(nb)))
    ra, rb = nb, nb + 1
    a_free = (set(range(nb, nb + 2)) - set(ca)).pop()
    b_free = (set(range(nb, nb + 2)) - set(cb)).pop()
    if a_free < ca[0]:
        da = _mdot(g, b, (((rb,), (b_free,)), batch))
    else:
        da = _mdot(b, g, (((b_free,), (rb,)), batch))
    if b_free > cb[0]:
        db = _mdot(a, g, (((a_free,), (ra,)), batch))
    else:
        db = _mdot(g, a, (((ra,), (a_free,)), batch))
    return da.astype(a.dtype), db.astype(b.dtype)


_mdot.defvjp(_mdot_fwd, _mdot_bwd)


def _rms(x, gain):
    return x * lax.rsqrt(jnp.mean(x * x, axis=-1, keepdims=True) + EPS) * gain


def _pre_fn(coef, x_in, f, gate, gain, shift, scale):
    x_new = x_in if f is None else x_in + coef * gate * f
    return x_new, _rms(x_new, gain) * (1.0 + scale) + shift


def _row_spec(ts):
    return pl.BlockSpec((1, ts, D_MODEL), lambda b, j: (b, j, 0))


_BATCH_VEC = pl.BlockSpec((1, 1, D_MODEL), lambda b, j: (b, 0, 0))
_ONE_VEC = pl.BlockSpec((1, D_MODEL), lambda b, j: (0, 0))


def _pre(x_in, f, gate, gain, shift, scale, coef, *, name, ts=512):
    n_b, n_s, _ = x_in.shape
    ts = min(ts, n_s)
    has_res = f is not None

    def body(*refs):
        if has_res:
            x_ref, f_ref, gate_ref, gain_ref, sh_ref, sc_ref, xn_ref, a_ref = refs
            x_new, a = _pre_fn(coef, x_ref[0], f_ref[0], gate_ref[0], gain_ref[...], sh_ref[0], sc_ref[0])
            xn_ref[0] = x_new
        else:
            x_ref, gain_ref, sh_ref, sc_ref, a_ref = refs
            _, a = _pre_fn(coef, x_ref[0], None, None, gain_ref[...], sh_ref[0], sc_ref[0])
        a_ref[0] = a.astype(a_ref.dtype)

    row = _row_spec(ts)
    if has_res:
        args = (x_in, f, gate, gain, shift, scale)
        in_specs = [row, row, _BATCH_VEC, _ONE_VEC, _BATCH_VEC, _BATCH_VEC]
        out_specs = (row, row)
        out_shape = (jax.ShapeDtypeStruct(x_in.shape, F32), jax.ShapeDtypeStruct(x_in.shape, MXU_DTYPE))
    else:
        args = (x_in, gain, shift, scale)
        in_specs = [row, _ONE_VEC, _BATCH_VEC, _BATCH_VEC]
        out_specs = row
        out_shape = jax.ShapeDtypeStruct(x_in.shape, MXU_DTYPE)
    return pl.pallas_call(
        body, name=name, grid=(n_b, n_s // ts), in_specs=in_specs, out_specs=out_specs, out_shape=out_shape,
        compiler_params=_params(5 * _nbytes((ts, D_MODEL), F32), 4 * _nbytes((ts, D_MODEL), F32)),
    )(*_hbm(*args))


def _accumulate(ref, value, first):
    @pl.when(first)
    def _():
        ref[...] = value

    @pl.when(jnp.logical_not(first))
    def _():
        ref[...] += value


def _pre_bwd(x_in, f, gate, gain, shift, scale, coef, da, dx_up, *, name, ts=512):
    n_b, n_s, _ = x_in.shape
    ts = min(ts, n_s)
    has_res = f is not None
    has_up = dx_up is not None

    def body(*refs):
        refs = list(refs)
        x_ref = refs.pop(0)
        f_ref, gate_ref = (refs.pop(0), refs.pop(0)) if has_res else (None, None)
        gain_ref, sh_ref, sc_ref, da_ref = refs.pop(0), refs.pop(0), refs.pop(0), refs.pop(0)
        up_ref = refs.pop(0) if has_up else None
        dx_ref = refs.pop(0)
        df_ref, dgate_ref = (refs.pop(0), refs.pop(0)) if has_res else (None, None)
        dgain_ref, dsh_ref, dsc_ref = refs
        b, j = pl.program_id(0), pl.program_id(1)
        da_v = da_ref[0].astype(F32)
        up_v = up_ref[0] if has_up else jnp.zeros((ts, D_MODEL), F32)
        if has_res:
            fn = functools.partial(_pre_fn, coef)
            _, pull = jax.vjp(fn, x_ref[0], f_ref[0], gate_ref[0], gain_ref[...], sh_ref[0], sc_ref[0])
            dx, df, dgate, dgain, dsh, dsc = pull((up_v, da_v))
            df_ref[0] = df.astype(df_ref.dtype)
            _accumulate(dgate_ref, dgate[None], j == 0)
        else:
            fn = lambda x, g, sh, sc: _pre_fn(coef, x, None, None, g, sh, sc)
            _, pull = jax.vjp(fn, x_ref[0], gain_ref[...], sh_ref[0], sc_ref[0])
            dx, dgain, dsh, dsc = pull((up_v, da_v))
        dx_ref[0] = dx
        _accumulate(dgain_ref, dgain, jnp.logical_and(b == 0, j == 0))
        _accumulate(dsh_ref, dsh[None], j == 0)
        _accumulate(dsc_ref, dsc[None], j == 0)

    row = _row_spec(ts)
    args, in_specs = [x_in], [row]
    if has_res:
        args += [f, gate]
        in_specs += [row, _BATCH_VEC]
    args += [gain, shift, scale, da]
    in_specs += [_ONE_VEC, _BATCH_VEC, _BATCH_VEC, row]
    if has_up:
        args.append(dx_up)
        in_specs.append(row)
    vec = jax.ShapeDtypeStruct((n_b, 1, D_MODEL), F32)
    out_shape, out_specs = [jax.ShapeDtypeStruct(x_in.shape, F32)], [row]
    if has_res:
        out_shape += [jax.ShapeDtypeStruct(x_in.shape, MXU_DTYPE), vec]
        out_specs += [row, _BATCH_VEC]
    out_shape += [jax.ShapeDtypeStruct((1, D_MODEL), F32), vec, vec]
    out_specs += [_ONE_VEC, _BATCH_VEC, _BATCH_VEC]
    return pl.pallas_call(
        body, name=name, grid=(n_b, n_s // ts), in_specs=in_specs, out_specs=tuple(out_specs), out_shape=tuple(out_shape),
        compiler_params=_params(6 * _nbytes((ts, D_MODEL), F32), 8 * _nbytes((ts, D_MODEL), F32)),
    )(*_hbm(*args))


def _final_fn(x_in, f, gate, gain, target):
    x_new = x_in + 0.5 * gate * f
    err = jnp.square(_rms(x_new, gain) - target)
    return 0.5 * jnp.sum(jnp.mean(err, axis=-1))


def _final(x_in, f, gate, gain, target, *, name, ts=512):
    n_b, n_s, _ = x_in.shape
    ts = min(ts, n_s)

    def body(x_ref, f_ref, gate_ref, gain_ref, t_ref, loss_ref, dx_ref, df_ref, dgate_ref, dgain_ref):
        b, j = pl.program_id(0), pl.program_id(1)
        loss, (dx, df, dgate, dgain) = jax.value_and_grad(_final_fn, argnums=(0, 1, 2, 3))(
            x_ref[0], f_ref[0], gate_ref[0], gain_ref[...], t_ref[0])
        first = jnp.logical_and(b == 0, j == 0)
        _accumulate(loss_ref, jnp.reshape(loss, (1, 1)), first)
        dx_ref[0] = dx
        df_ref[0] = df.astype(df_ref.dtype)
        _accumulate(dgate_ref, dgate[None], j == 0)
        _accumulate(dgain_ref, dgain, first)

    row = _row_spec(ts)
    return pl.pallas_call(
        body, name=name, grid=(n_b, n_s // ts),
        in_specs=[row, row, _BATCH_VEC, _ONE_VEC, row],
        out_specs=(pl.BlockSpec((1, 1), lambda b, j: (0, 0)), row, row, _BATCH_VEC, _ONE_VEC),
        out_shape=(jax.ShapeDtypeStruct((1, 1), F32), jax.ShapeDtypeStruct(x_in.shape, F32),
                   jax.ShapeDtypeStruct(x_in.shape, MXU_DTYPE), jax.ShapeDtypeStruct((n_b, 1, D_MODEL), F32),
                   jax.ShapeDtypeStruct((1, D_MODEL), F32)),
        compiler_params=_params(5 * _nbytes((ts, D_MODEL), F32), 8 * _nbytes((ts, D_MODEL), F32)),
    )(*_hbm(x_in, f, gate, gain, target))


FFN_TOKENS = 1024


def _ffn_up(a, w1s, w3s, *, name, tm=FFN_TOKENS):
    n_tok = a.shape[0]
    tm = min(tm, n_tok)

    def body(a_ref, w1_ref, w3_ref, h1_ref, h3_ref, g_ref):
        av = a_ref[...].astype(MXU_DTYPE)
        h1 = lax.dot_general(av, w1_ref[0].astype(MXU_DTYPE), _NT, preferred_element_type=F32)
        h3 = lax.dot_general(av, w3_ref[0].astype(MXU_DTYPE), _NT, preferred_element_type=F32)
        h1_ref[0] = h1.astype(h1_ref.dtype)
        h3_ref[0] = h3.astype(h3_ref.dtype)
        g_ref[0] = (jax.nn.silu(h1) * h3).astype(g_ref.dtype)

    w_spec = pl.BlockSpec((1, FF_SHARD, D_MODEL), lambda s, i: (s, 0, 0))
    h_spec = pl.BlockSpec((1, tm, FF_SHARD), lambda s, i: (s, i, 0))
    h_shape = jax.ShapeDtypeStruct((N_SHARD, n_tok, FF_SHARD), MXU_DTYPE)
    blk = _nbytes((tm, D_MODEL), a.dtype) + 2 * _nbytes((D_MODEL, FF_SHARD), w1s.dtype) + 3 * _nbytes((tm, FF_SHARD), MXU_DTYPE)
    return pl.pallas_call(
        body, name=name, grid=(N_SHARD, n_tok // tm),
        in_specs=[pl.BlockSpec((tm, D_MODEL), lambda s, i: (i, 0)), w_spec, w_spec],
        out_specs=(h_spec, h_spec, h_spec), out_shape=(h_shape, h_shape, h_shape),
        compiler_params=_params(blk, 6 * _nbytes((tm, FF_SHARD), F32)),
    )(*_hbm(a, w1s, w3s))


def _ffn_down_bwd(df, w2s, h1, h3, *, name, tm=FFN_TOKENS):
    n_tok = df.shape[0]
    tm = min(tm, n_tok)

    def body(df_ref, w2_ref, h1_ref, h3_ref, dh1_ref, dh3_ref):
        dg = lax.dot_general(df_ref[...].astype(MXU_DTYPE), w2_ref[0].astype(MXU_DTYPE), _NT, preferred_element_type=F32)
        h1v = h1_ref[0].astype(F32)
        h3v = h3_ref[0].astype(F32)
        sig = jax.nn.sigmoid(h1v)
        dh3_ref[0] = (dg * (h1v * sig)).astype(dh3_ref.dtype)
        dh1_ref[0] = (dg * h3v * (sig * (1.0 + h1v * (1.0 - sig)))).astype(dh1_ref.dtype)

    h_spec = pl.BlockSpec((1, tm, FF_SHARD), lambda s, i: (s, i, 0))
    h_shape = jax.ShapeDtypeStruct((N_SHARD, n_tok, FF_SHARD), MXU_DTYPE)
    blk = _nbytes((tm, D_MODEL), df.dtype) + _nbytes((FF_SHARD, D_MODEL), w2s.dtype) + 4 * _nbytes((tm, FF_SHARD), MXU_DTYPE)
    return pl.pallas_call(
        body, name=name, grid=(N_SHARD, n_tok // tm),
        in_specs=[pl.BlockSpec((tm, D_MODEL), lambda s, i: (i, 0)),
                  pl.BlockSpec((1, FF_SHARD, D_MODEL), lambda s, i: (s, 0, 0)), h_spec, h_spec],
        out_specs=(h_spec, h_spec), out_shape=(h_shape, h_shape),
        compiler_params=_params(blk, 8 * _nbytes((tm, FF_SHARD), F32)),
    )(*_hbm(df, w2s, h1, h3))


def _ffn_fwd(a, w1s, w3s, w2s, tag):
    h1, h3, g = _ffn_up(a, w1s, w3s, name=f"{tag}_up")
    f = _mm_act([(g, w2s)], "nn", reduce_shards=True, tm=FFN_TOKENS, name=f"{tag}_down")[0]
    return f, (h1, h3, g)


def _ffn_bwd(a, w1s, w3s, w2s, saved, df, tag):
    h1, h3, g = saved
    dh1, dh3 = _ffn_down_bwd(df, w2s, h1, h3, name=f"{tag}_down_bwd")
    da = _mm_act([(dh1, w1s), (dh3, w3s)], "nn", reduce_shards=True, tm=FFN_TOKENS, name=f"{tag}_up_bwd")[0]
    a3 = a[None]
    dw1 = _mm_tn(dh1, a3, tt=FFN_TOKENS, name=f"{tag}_dw1")
    dw3 = _mm_tn(dh3, a3, tt=FFN_TOKENS, name=f"{tag}_dw3")
    dw2 = _mm_tn(g, df[None], tt=FFN_TOKENS, name=f"{tag}_dw2")
    return da, dw1, dw3, dw2


CONV_LANES = 256


def _shift_down(x, d):
    if d == 0:
        return x
    row = lax.broadcasted_iota(jnp.int32, x.shape, 0)
    return jnp.where(row >= d, pltpu.roll(x, d, 0), 0.0)


def _shift_up(x, d):
    if d == 0:
        return x
    n = x.shape[0]
    row = lax.broadcasted_iota(jnp.int32, x.shape, 0)
    return jnp.where(row < n - d, pltpu.roll(x, n - d, 0), 0.0)


def _conv_pre(x, w):
    acc = None
    for j in range(CONV_WIDTH):
        term = w[j:j + 1, :] * _shift_down(x, CONV_WIDTH - 1 - j)
        acc = term if acc is None else acc + term
    return acc


def _conv_fwd(x, w, *, name):
    n_b, n_s, n_c = x.shape
    spec = pl.BlockSpec((1, n_s, CONV_LANES), lambda b, cj: (b, 0, cj))

    def body(x_ref, w_ref, o_ref):
        o_ref[0] = jax.nn.silu(_conv_pre(x_ref[0], w_ref[...]))

    return pl.pallas_call(
        body, name=name, grid=(n_b, n_c // CONV_LANES),
        in_specs=[spec, pl.BlockSpec((CONV_WIDTH, CONV_LANES), lambda b, cj: (0, cj))],
        out_specs=spec, out_shape=jax.ShapeDtypeStruct(x.shape, F32),
        compiler_params=_params(2 * _nbytes((n_s, CONV_LANES), F32), 6 * _nbytes((n_s, CONV_LANES), F32)),
    )(*_hbm(x, w))


def _conv_bwd(x, w, dout, *, name):
    n_b, n_s, n_c = x.shape
    per_part = DN_WIDTH // CONV_LANES
    spec = pl.BlockSpec((1, n_s, CONV_LANES), lambda cj, b: (b, 0, cj))
    do_spec = pl.BlockSpec((1, 1, n_s, CONV_LANES), lambda cj, b: (cj // per_part, b, 0, cj % per_part))
    w_spec = pl.BlockSpec((CONV_WIDTH, CONV_LANES), lambda cj, b: (0, cj))

    def body(x_ref, w_ref, do_ref, dx_ref, dw_ref):
        xv, wv = x_ref[0], w_ref[...]
        pre = _conv_pre(xv, wv)
        sig = jax.nn.sigmoid(pre)
        dpre = do_ref[0, 0] * (sig * (1.0 + pre * (1.0 - sig)))
        dx = None
        first = pl.program_id(1) == 0
        for j in range(CONV_WIDTH):
            d = CONV_WIDTH - 1 - j
            ahead = _shift_up(dpre, d)
            term = wv[j:j + 1, :] * ahead
            dx = term if dx is None else dx + term
            dwj = jnp.sum(ahead * xv, axis=0, keepdims=True)
            _accumulate(dw_ref.at[j:j + 1, :], dwj, first)
        dx_ref[0] = dx.astype(dx_ref.dtype)

    return pl.pallas_call(
        body, name=name, grid=(n_c // CONV_LANES, n_b),
        in_specs=[spec, w_spec, do_spec], out_specs=(spec, w_spec),
        out_shape=(jax.ShapeDtypeStruct(x.shape, MXU_DTYPE), jax.ShapeDtypeStruct((CONV_WIDTH, n_c), F32)),
        compiler_params=_params(3 * _nbytes((n_s, CONV_LANES), F32), 8 * _nbytes((n_s, CONV_LANES), F32)),
    )(*_hbm(x, w, dout))


_BNT = (((2,), (2,)), ((0,), (0,)))
_BNN = (((2,), (1,)), ((0,), (0,)))
_BTN = (((1,), (1,)), ((0,), (0,)))
DN_PREP_CHUNKS = 8
DN_SCAN_HEADS = 16
DN_SCAN_CHUNKS = 4
N_DOUBLINGS = 5


def _fdot(a, b, dims):
    return lax.dot_general(a, b, dims, precision=lax.Precision.HIGHEST, preferred_element_type=F32)


def _hdot(a, b, dims):
    return lax.dot_general(a, b, dims, precision=lax.Precision.HIGH, preferred_element_type=F32)


def _solve_by_doubling(a, rhs_u, rhs_w):
    row = lax.broadcasted_iota(jnp.int32, (CHUNK, CHUNK), 0)
    col = lax.broadcasted_iota(jnp.int32, (CHUNK, CHUNK), 1)
    inv = jnp.where(row == col, 1.0, 0.0) - a
    power = a
    for _ in range(N_DOUBLINGS):
        power = _hdot(power, power, _BNN)
        inv = inv + _hdot(inv, power, _BNN)
    return _hdot(inv, rhs_u, _BNN), _hdot(inv, rhs_w, _BNN), inv


@jax.custom_vjp
def _solve_saved(a, rhs_u, rhs_w, inv, u, w):
    return u, w


def _solve_saved_fwd(a, rhs_u, rhs_w, inv, u, w):
    return (u, w), (inv, u, w)


def _solve_saved_bwd(res, cts):
    inv, u, w = res
    gu = _hdot(inv, cts[0], _BTN)
    gw = _hdot(inv, cts[1], _BTN)
    da = -(_hdot(gu, u, _BNT) + _hdot(gw, w, _BNT))
    return da, gu, gw, jnp.zeros_like(inv), jnp.zeros_like(u), jnp.zeros_like(w)


_solve_saved.defvjp(_solve_saved_fwd, _solve_saved_bwd)


def _dn_prep_fn(solve, qc, kc, vc, bl, lac, lar, a_log, dt_bias):
    q = qc * lax.rsqrt(jnp.sum(qc * qc, axis=-1, keepdims=True) + EPS) * (DN_HEAD_DIM ** -0.5)
    k = kc * lax.rsqrt(jnp.sum(kc * kc, axis=-1, keepdims=True) + EPS)
    beta = jax.nn.sigmoid(bl)
    neg_a = -jnp.exp(a_log)
    lgc = neg_a * jax.nn.softplus(lac + dt_bias)
    lgr = neg_a * jax.nn.softplus(lar + dt_bias)
    row = lax.broadcasted_iota(jnp.int32, (CHUNK, CHUNK), 0)
    col = lax.broadcasted_iota(jnp.int32, (CHUNK, CHUNK), 1)
    causal, strict = row >= col, row > col
    g_c = jnp.sum(jnp.where(causal, lgr, 0.0), axis=-1, keepdims=True)
    g_r = jnp.sum(jnp.where(row <= col, lgc, 0.0), axis=-2, keepdims=True)
    decay = jnp.exp(jnp.where(causal, g_c - g_r, -jnp.inf))
    kb = k * beta
    a = jnp.where(strict, _mdot(kb, k, _BNT) * decay, 0.0)
    u, w, extra = solve(a, vc * beta, kb * jnp.exp(g_c))
    attn = _mdot(q, k, _BNT) * decay
    g_last = jnp.sum(lgc, axis=-2, keepdims=True)
    return q * jnp.exp(g_c), k * jnp.exp(g_last - g_c), u, w, attn, g_last, extra


PAIR = 2
PAIR_LANES = PAIR * DN_HEAD_DIM


def _dn_prep_specs(n_cb):
    tok = n_cb * CHUNK
    wide = pl.BlockSpec((1, 1, tok, PAIR_LANES), lambda p, b, j: (b, p, j, 0))
    rowv = pl.BlockSpec((1, PAIR, n_cb, 1, CHUNK), lambda p, b, j: (b, p, j, 0, 0))
    one = pl.BlockSpec((1, PAIR, n_cb, 1, 1), lambda p, b, j: (b, p, j, 0, 0))
    head = pl.BlockSpec((PAIR, 1, 1), lambda p, b, j: (p, 0, 0))
    lanes = lambda part: pl.BlockSpec((1, tok, PAIR_LANES), lambda p, b, j: (b, j, part * (DN_HEADS // PAIR) + p))
    return wide, rowv, one, head, lanes


def _split_pair(x, n_cb):
    halves = [x[:, h * DN_HEAD_DIM:(h + 1) * DN_HEAD_DIM].reshape(n_cb, CHUNK, DN_HEAD_DIM) for h in range(PAIR)]
    return jnp.concatenate(halves, axis=0)


def _join_pair(chunks, tok):
    per_head = chunks.reshape(PAIR, tok, DN_HEAD_DIM)
    return jnp.concatenate([per_head[h] for h in range(PAIR)], axis=-1)


def _dn_prep_load(n_cb, q_ref, k_ref, v_ref, blr_ref, lar_ref, al_ref, dt_ref):
    rowf = lambda r: r[0].reshape(PAIR * n_cb, 1, CHUNK)
    return (_split_pair(q_ref[0], n_cb), _split_pair(k_ref[0], n_cb), _split_pair(v_ref[0], n_cb), rowf(blr_ref),
            rowf(lar_ref), al_ref[...], dt_ref[...])


def _dn_prep_pair_fn(n_cb, solve, qc, kc, vc, blr, lar, a_log, dt_bias):
    per_chunk = lambda t: jnp.broadcast_to(t[:, None], (PAIR, n_cb, 1, 1)).reshape(PAIR * n_cb, 1, 1)
    eye = lax.broadcasted_iota(jnp.int32, (CHUNK, CHUNK), 0) == lax.broadcasted_iota(jnp.int32, (CHUNK, CHUNK), 1)
    to_col = lambda r: jnp.sum(jnp.where(eye, r, 0.0), axis=-1, keepdims=True)
    return _dn_prep_fn(solve, qc, kc, vc, to_col(blr), to_col(lar), lar, per_chunk(a_log), per_chunk(dt_bias))


def _dn_prep(qkv, blr, lar, a_log, dt_bias, *, name):
    n_b, n_s, _ = qkv.shape
    n_cb = min(DN_PREP_CHUNKS, n_s // CHUNK)
    tok = n_cb * CHUNK
    wide, rowv, one, head, lanes = _dn_prep_specs(n_cb)

    def body(*refs):
        outs = _dn_prep_pair_fn(n_cb, _solve_by_doubling, *_dn_prep_load(n_cb, *refs[:7]))
        for ref, val in zip(refs[7:12], outs[:5]):
            ref[0, 0] = _join_pair(val, tok)
        refs[12][0] = outs[5].reshape(PAIR, n_cb, 1, 1)
        refs[13][0, 0] = _join_pair(outs[6], tok)

    big = jax.ShapeDtypeStruct((n_b, DN_HEADS // PAIR, n_s, PAIR_LANES), F32)
    return pl.pallas_call(
        body, name=name, grid=(DN_HEADS // PAIR, n_b, n_s // tok),
        in_specs=[lanes(0), lanes(1), lanes(2), rowv, rowv, head, head],
        out_specs=(wide, wide, wide, wide, wide, one, wide),
        out_shape=(big, big, big, big, big, jax.ShapeDtypeStruct((n_b, DN_HEADS, n_s // CHUNK, 1, 1), F32), big),
        compiler_params=_params(11 * PAIR * _nbytes((tok, LANES), F32), 48 * PAIR * _nbytes((tok, LANES), F32)),
    )(*_hbm(qkv, qkv, qkv, blr, lar, a_log, dt_bias))


def _dn_prep_bwd(qkv, blr, lar, a_log, dt_bias, inv, u, w, cts, *, name):
    n_b, n_s, _ = qkv.shape
    n_cb = min(DN_PREP_CHUNKS, n_s // CHUNK)
    tok = n_cb * CHUNK
    wide, rowv, one, head, lanes = _dn_prep_specs(n_cb)

    def body(*refs):
        prim = _dn_prep_load(n_cb, *refs[:7])
        chunks = lambda r: _split_pair(r[0, 0], n_cb)
        inv_v, u_v, w_v = chunks(refs[7]), chunks(refs[8]), chunks(refs[9])
        ct = tuple(chunks(r) for r in refs[10:15]) + (refs[15][0].reshape(PAIR * n_cb, 1, 1),)

        def fn(*args):
            solve = lambda a, ru, rw: _solve_saved(a, ru, rw, inv_v, u_v, w_v) + (None,)
            return _dn_prep_pair_fn(n_cb, solve, *args)[:6]

        _, pull = jax.vjp(fn, *prim)
        dq, dk, dv, dblr, dlar, dal, ddt = pull(ct)
        outs = refs[16:]
        for part, val in enumerate((dq, dk, dv)):
            outs[0][part, 0] = _join_pair(val, tok)
        outs[1][0] = dblr.reshape(PAIR, n_cb, 1, CHUNK)
        outs[2][0] = dlar.reshape(PAIR, n_cb, 1, CHUNK)
        first = jnp.logical_and(pl.program_id(1) == 0, pl.program_id(2) == 0)
        _accumulate(outs[3], dal, first)
        _accumulate(outs[4], ddt, first)

    dqkv_spec = pl.BlockSpec((3, 1, tok, PAIR_LANES), lambda p, b, j: (0, b, j, p))
    return pl.pallas_call(
        body, name=name, grid=(DN_HEADS // PAIR, n_b, n_s // tok),
        in_specs=[lanes(0), lanes(1), lanes(2), rowv, rowv, head, head, wide, wide, wide, wide, wide, wide, wide, wide, one],
        out_specs=(dqkv_spec, rowv, rowv, head, head),
        out_shape=(jax.ShapeDtypeStruct((3, n_b, n_s, DN_WIDTH), F32), jax.ShapeDtypeStruct(blr.shape, F32),
                   jax.ShapeDtypeStruct(lar.shape, F32), jax.ShapeDtypeStruct(a_log.shape, F32),
                   jax.ShapeDtypeStruct(dt_bias.shape, F32)),
        compiler_params=_params(21 * PAIR * _nbytes((tok, LANES), F32), 64 * PAIR * _nbytes((tok, LANES), F32)),
    )(*_hbm(qkv, qkv, qkv, blr, lar, a_log, dt_bias, inv, u, w, *cts))


def _dn_step(state, q, k, u, w, a, gl):
    v_new = u - _mdot(w, state, _BNN)
    o = _mdot(q, state, _BNN) + _mdot(a, v_new, _BNN)
    return state * jnp.exp(gl) + _mdot(k, v_new, _BTN), o


def _dn_scan_specs(n_cb, n_blocks, reverse, n_seq):
    tok = n_cb * CHUNK
    jj = (lambda j: n_blocks - 1 - j) if reverse else (lambda j: j)
    wide = pl.BlockSpec((1, n_seq // PAIR, tok, PAIR_LANES), lambda b, j: (b, 0, jj(j), 0))
    one = pl.BlockSpec((1, n_seq, n_cb, 1, 1), lambda b, j: (b, 0, jj(j), 0, 0))
    st = pl.BlockSpec((1, n_seq, n_cb, DN_HEAD_DIM, DN_HEAD_DIM), lambda b, j: (b, 0, jj(j), 0, 0))
    return wide, one, st


def _scan_fold(n_b):
    fold = max(1, DN_SCAN_HEADS // DN_HEADS)
    return fold if n_b % fold == 0 else 1


def _fold_rows(arrays, fold):
    return [a.reshape((a.shape[0] // fold, fold * a.shape[1]) + a.shape[2:]) for a in arrays]


def _to_scan_order(per_head, fold):
    n_b, rest = per_head.shape[0], per_head.shape[2:]
    t = per_head.reshape((n_b // fold, fold, DN_HEADS // PAIR, PAIR) + rest)
    t = jnp.moveaxis(t, 3, 1)
    return t.reshape((n_b // fold, fold * DN_HEADS) + rest)


def _from_scan_order(t, fold):
    n_bf, rest = t.shape[0], t.shape[2:]
    t = t.reshape((n_bf, PAIR, fold, DN_HEADS // PAIR) + rest)
    t = jnp.moveaxis(t, 1, 3)
    return t.reshape((n_bf * fold, DN_HEADS) + rest)


def _unpack_seqs(x):
    return jnp.concatenate([x[:, :, :DN_HEAD_DIM], x[:, :, DN_HEAD_DIM:]], axis=0)


def _pack_seqs(x):
    n_p = x.shape[0] // PAIR
    return jnp.concatenate([x[:n_p], x[n_p:]], axis=-1)


def _dn_scan(qd, kd, u, w, attn, g_last, *, name):
    shape = qd.shape
    fold = _scan_fold(shape[0])
    qd, kd, u, w, attn = _fold_rows([qd, kd, u, w, attn], fold)
    g_last = _to_scan_order(g_last, fold)
    n_b, n_pk, n_s, _ = qd.shape
    n_seq = PAIR * n_pk
    n_cb = min(DN_SCAN_CHUNKS, n_s // CHUNK)
    n_blocks = n_s // (n_cb * CHUNK)
    wide, one, st = _dn_scan_specs(n_cb, n_blocks, False, n_seq)

    def body(qd_ref, kd_ref, u_ref, w_ref, a_ref, gl_ref, o_ref, st_ref, state_ref):
        @pl.when(pl.program_id(1) == 0)
        def _():
            state_ref[...] = jnp.zeros(state_ref.shape, F32)

        def step(n, state):
            rows = pl.ds(pl.multiple_of(n * CHUNK, CHUNK), CHUNK)
            seqs = lambda r: _unpack_seqs(r[0, :, rows, :])
            st_ref[0, :, n] = state
            state, o = _dn_step(state, seqs(qd_ref), seqs(kd_ref), seqs(u_ref), seqs(w_ref), seqs(a_ref), gl_ref[0, :, n])
            o_ref[0, :, rows, :] = _pack_seqs(o)
            return state

        state_ref[...] = lax.fori_loop(0, n_cb, step, state_ref[...])

    o, states = pl.pallas_call(
        body, name=name, grid=(n_b, n_blocks),
        in_specs=[wide, wide, wide, wide, wide, one], out_specs=(wide, st),
        out_shape=(jax.ShapeDtypeStruct(qd.shape, F32),
                   jax.ShapeDtypeStruct((n_b, n_seq, n_s // CHUNK, DN_HEAD_DIM, DN_HEAD_DIM), F32)),
        scratch_shapes=[pltpu.VMEM((n_seq, DN_HEAD_DIM, DN_HEAD_DIM), F32)],
        compiler_params=_params(6 * _nbytes((n_pk, n_cb * CHUNK, PAIR_LANES), F32)
                                + _nbytes((n_seq, n_cb * CHUNK, LANES), F32), 8 << 20),
    )(*_hbm(qd, kd, u, w, attn, g_last))
    return o.reshape(shape), states


def _dn_scan_bwd(qd, kd, u, w, attn, g_last, states, do, *, name):
    shape = qd.shape
    fold = _scan_fold(shape[0])
    qd, kd, u, w, attn, do = _fold_rows([qd, kd, u, w, attn, do], fold)
    g_last = _to_scan_order(g_last, fold)
    n_b, n_pk, n_s, _ = qd.shape
    n_seq = PAIR * n_pk
    n_cb = min(DN_SCAN_CHUNKS, n_s // CHUNK)
    n_blocks = n_s // (n_cb * CHUNK)
    wide, one, st = _dn_scan_specs(n_cb, n_blocks, True, n_seq)

    def body(qd_ref, kd_ref, u_ref, w_ref, a_ref, gl_ref, st_ref, do_ref,
             dq_ref, dk_ref, du_ref, dw_ref, da_ref, dgl_ref, dstate_ref):
        @pl.when(pl.program_id(1) == 0)
        def _():
            dstate_ref[...] = jnp.zeros(dstate_ref.shape, F32)

        def step(i, dstate):
            n = n_cb - 1 - i
            rows = pl.ds(pl.multiple_of(n * CHUNK, CHUNK), CHUNK)
            seqs = lambda r: _unpack_seqs(r[0, :, rows, :])
            _, pull = jax.vjp(_dn_step, st_ref[0, :, n], seqs(qd_ref), seqs(kd_ref), seqs(u_ref), seqs(w_ref),
                              seqs(a_ref), gl_ref[0, :, n])
            dstate, dq, dk, du, dw, da, dgl = pull((dstate, seqs(do_ref)))
            for ref, val in zip((dq_ref, dk_ref, du_ref, dw_ref, da_ref), (dq, dk, du, dw, da)):
                ref[0, :, rows, :] = _pack_seqs(val)
            dgl_ref[0, :, n] = dgl
            return dstate

        dstate_ref[...] = lax.fori_loop(0, n_cb, step, dstate_ref[...])

    big = jax.ShapeDtypeStruct(qd.shape, F32)
    outs = pl.pallas_call(
        body, name=name, grid=(n_b, n_blocks),
        in_specs=[wide, wide, wide, wide, wide, one, st, wide],
        out_specs=(wide, wide, wide, wide, wide, one),
        out_shape=(big, big, big, big, big, jax.ShapeDtypeStruct(g_last.shape, F32)),
        scratch_shapes=[pltpu.VMEM((n_seq, DN_HEAD_DIM, DN_HEAD_DIM), F32)],
        compiler_params=_params(11 * _nbytes((n_pk, n_cb * CHUNK, PAIR_LANES), F32)
                                + _nbytes((n_seq, n_cb * CHUNK, LANES), F32), 8 << 20),
    )(*_hbm(qd, kd, u, w, attn, g_last, states, do))
    return tuple(o.reshape(shape) for o in outs[:5]) + (_from_scan_order(outs[5], fold),)


def _dn_post_fn(o, z, gain):
    return o * lax.rsqrt(jnp.mean(o * o, axis=-1, keepdims=True) + EPS) * gain * jax.nn.silu(z)


_HEAD_ROWS = lambda n_s: pl.BlockSpec((1, 1, n_s, PAIR_LANES), lambda b, p: (b, p, 0, 0))
_PAIR_LANES = lambda n_s: pl.BlockSpec((1, n_s, PAIR_LANES), lambda b, p: (b, 0, p))
_HEAD_GAIN = pl.BlockSpec((1, DN_HEAD_DIM), lambda b, p: (0, 0))


def _pair_heads(x):
    return jnp.stack([x[:, h * DN_HEAD_DIM:(h + 1) * DN_HEAD_DIM] for h in range(PAIR)])


def _pair_lanes(x):
    return jnp.concatenate([x[h] for h in range(PAIR)], axis=-1)


def _dn_post(o, z, gain, *, name):
    n_b, _, n_s, _ = o.shape

    def body(o_ref, z_ref, g_ref, out_ref):
        out = _dn_post_fn(_pair_heads(o_ref[0, 0]), _pair_heads(z_ref[0]), g_ref[...])
        out_ref[0] = _pair_lanes(out).astype(out_ref.dtype)

    lanes = _PAIR_LANES(n_s)
    return pl.pallas_call(
        body, name=name, grid=(n_b, DN_HEADS // PAIR), in_specs=[_HEAD_ROWS(n_s), lanes, _HEAD_GAIN], out_specs=lanes,
        out_shape=jax.ShapeDtypeStruct(z.shape, MXU_DTYPE),
        compiler_params=_params(3 * PAIR * _nbytes((n_s, LANES), F32), 6 * PAIR * _nbytes((n_s, LANES), F32)),
    )(*_hbm(o, z, gain))


def _dn_post_bwd(o, z, gain, dout, *, name):
    n_b, _, n_s, _ = o.shape

    def body(o_ref, z_ref, g_ref, dout_ref, do_ref, dz_ref, dg_ref):
        _, pull = jax.vjp(_dn_post_fn, _pair_heads(o_ref[0, 0]), _pair_heads(z_ref[0]), g_ref[...])
        do, dz, dg = pull(_pair_heads(dout_ref[0].astype(F32)))
        do_ref[0, 0] = _pair_lanes(do)
        dz_ref[0] = _pair_lanes(dz).astype(dz_ref.dtype)
        _accumulate(dg_ref, dg, jnp.logical_and(pl.program_id(0) == 0, pl.program_id(1) == 0))

    rows, lanes = _HEAD_ROWS(n_s), _PAIR_LANES(n_s)
    return pl.pallas_call(
        body, name=name, grid=(n_b, DN_HEADS // PAIR), in_specs=[rows, lanes, _HEAD_GAIN, lanes],
        out_specs=(rows, lanes, _HEAD_GAIN),
        out_shape=(jax.ShapeDtypeStruct(o.shape, F32), jax.ShapeDtypeStruct(z.shape, MXU_DTYPE),
                   jax.ShapeDtypeStruct((1, DN_HEAD_DIM), F32)),
        compiler_params=_params(5 * PAIR * _nbytes((n_s, LANES), F32), 10 * PAIR * _nbytes((n_s, LANES), F32)),
    )(*_hbm(o, z, gain, dout))


TILE_ROWS = SUBLANES


def _s5_prep_fn(lam_re, lam_im, log_step, bt_re, bt_im, c_im):
    lr = jnp.minimum(lam_re, -1e-4)
    step = jnp.exp(log_step)
    mag = jnp.exp(lr * step)
    ang = lam_im * step
    lb_re = mag * jnp.cos(ang)
    lb_im = mag * jnp.sin(ang)
    den = lr * lr + lam_im * lam_im
    coef_re = ((lb_re - 1.0) * lr + lb_im * lam_im) / den
    coef_im = (lb_im * lr - (lb_re - 1.0) * lam_im) / den
    return (lb_re, lb_im, coef_re * bt_re - coef_im * bt_im, coef_re * bt_im + coef_im * bt_re, -c_im)


def _s5_prep(lam_re, lam_im, log_step, bt_re, bt_im, c_im, *, name):
    def body(*refs):
        outs = _s5_prep_fn(*(r[...] for r in refs[:6]))
        for ref, val in zip(refs[6:], outs):
            ref[...] = val

    vec = jax.ShapeDtypeStruct(lam_re.shape, F32)
    mat = jax.ShapeDtypeStruct(bt_re.shape, F32)
    return pl.pallas_call(body, name=name, out_shape=(vec, vec, mat, mat, mat))(lam_re, lam_im, log_step, bt_re, bt_im, c_im)


def _s5_prep_bwd(lam_re, lam_im, log_step, bt_re, bt_im, c_im, cts, *, name):
    def body(*refs):
        _, pull = jax.vjp(_s5_prep_fn, *(r[...] for r in refs[:6]))
        grads = pull(tuple(r[...] for r in refs[6:11]))
        for ref, val in zip(refs[11:], grads):
            ref[...] = val

    shapes = tuple(jax.ShapeDtypeStruct(a.shape, F32) for a in (lam_re, lam_im, log_step, bt_re, bt_im, c_im))
    return pl.pallas_call(body, name=name, out_shape=shapes)(lam_re, lam_im, log_step, bt_re, bt_im, c_im, *cts)


def _cmul(ar, ai, br, bi):
    return ar * br - ai * bi, ar * bi + ai * br


def _s5_powers(lr, li):
    pows = [(lr, li)]
    for _ in range(TILE_ROWS - 1):
        pows.append(_cmul(pows[-1][0], pows[-1][1], lr, li))
    return pows


def _s5_carry_table(pows, n_lanes, reverse):
    row = lax.broadcasted_iota(jnp.int32, (TILE_ROWS, n_lanes), 0)
    t_re = jnp.zeros((TILE_ROWS, n_lanes), F32)
    t_im = jnp.zeros((TILE_ROWS, n_lanes), F32)
    for r in range(TILE_ROWS):
        p_re, p_im = pows[TILE_ROWS - 1 - r] if reverse else pows[r]
        t_re = jnp.where(row == r, p_re, t_re)
        t_im = jnp.where(row == r, p_im, t_im)
    return t_re, t_im


def _s5_step_tables(pows, n_lanes, reverse):
    row = lax.broadcasted_iota(jnp.int32, (TILE_ROWS, n_lanes), 0)
    tables, d = [], 1
    while d < TILE_ROWS:
        inside = (row < TILE_ROWS - d) if reverse else (row >= d)
        tables.append((d, jnp.where(inside, pows[d - 1][0], 0.0), jnp.where(inside, pows[d - 1][1], 0.0)))
        d *= 2
    return tables


def _s5_tile(y_re, y_im, tables, reverse):
    for d, p_re, p_im in tables:
        shift = TILE_ROWS - d if reverse else d
        m_re, m_im = _cmul(p_re, p_im, pltpu.roll(y_re, shift, 0), pltpu.roll(y_im, shift, 0))
        y_re, y_im = y_re + m_re, y_im + m_im
    return y_re, y_im


S5_BLOCKS = N_SHARD
S5_BLOCK_CH = S5_WIDTH // S5_BLOCKS
S5_BLOCK_LANES = S5_LANES // S5_BLOCKS


def _scan_rows(i):
    return pl.ds(pl.multiple_of(i * TILE_ROWS, TILE_ROWS), TILE_ROWS)


def _s5_mix_specs(n_s, order):
    jb = lambda *g: order(*g)[0]
    bb = lambda *g: order(*g)[1]
    act = pl.BlockSpec((1, 1, n_s, S5_BLOCK_CH), lambda *g: (bb(*g), 0, 0, jb(*g)))
    state = pl.BlockSpec((1, 1, n_s, S5_BLOCK_LANES), lambda *g: (jb(*g), bb(*g), 0, 0))
    lam = pl.BlockSpec((1, S5_BLOCK_LANES), lambda *g: (0, jb(*g)))
    w_in = pl.BlockSpec((1, S5_BLOCK_CH, S5_BLOCK_LANES), lambda *g: (jb(*g), 0, 0))
    w_out = pl.BlockSpec((1, S5_BLOCK_LANES, S5_BLOCK_CH), lambda *g: (jb(*g), 0, 0))
    return act, state, lam, w_in, w_out


def _s5_mix(u, wb_re, wb_im, lb_re, lb_im, wc_re, wc_im, *, name):
    n_b, n_s, _ = u.shape
    n_blk = S5_BLOCKS
    lanes = lambda t: t[:, None]
    n_tiles = n_s // TILE_ROWS
    L = S5_BLOCK_LANES

    def body(u_ref, wbr_ref, wbi_ref, lr_ref, li_ref, wcr_ref, wci_ref, y_ref, xr_ref, xi_ref):
        uv = u_ref[0, 0].astype(MXU_DTYPE)
        xr_ref[0, 0] = lax.dot_general(uv, wbr_ref[0].astype(MXU_DTYPE), _NN, preferred_element_type=F32)
        xi_ref[0, 0] = lax.dot_general(uv, wbi_ref[0].astype(MXU_DTYPE), _NN, preferred_element_type=F32)
        pows = _s5_powers(lr_ref[...], li_ref[...])
        t_re, t_im = _s5_carry_table(pows, L, False)
        steps = _s5_step_tables(pows, L, False)

        def step(i, carry):
            rows = _scan_rows(i)
            y_re, y_im = _s5_tile(xr_ref[0, 0, rows, :], xi_ref[0, 0, rows, :], steps, False)
            c_re, c_im = _cmul(t_re, t_im, carry[0], carry[1])
            y_re, y_im = y_re + c_re, y_im + c_im
            xr_ref[0, 0, rows, :] = y_re
            xi_ref[0, 0, rows, :] = y_im
            return y_re[TILE_ROWS - 1:, :], y_im[TILE_ROWS - 1:, :]

        zero = jnp.zeros((1, L), F32)
        lax.fori_loop(0, n_tiles, step, (zero, zero), unroll=2)
        y_ref[0, 0] = (
            lax.dot_general(xr_ref[0, 0].astype(MXU_DTYPE), wcr_ref[0].astype(MXU_DTYPE), _NN, preferred_element_type=F32)
            + lax.dot_general(xi_ref[0, 0].astype(MXU_DTYPE), wci_ref[0].astype(MXU_DTYPE), _NN, preferred_element_type=F32))

    act, state, lam, w_in, w_out = _s5_mix_specs(n_s, lambda b, j: (j, b))
    x_shape = jax.ShapeDtypeStruct((n_blk, n_b, n_s, L), F32)
    return pl.pallas_call(
        body, name=name, grid=(n_b, n_blk),
        in_specs=[act, w_in, w_in, lam, lam, w_out, w_out], out_specs=(act, state, state),
        out_shape=(jax.ShapeDtypeStruct((n_b, 1, n_s, S5_WIDTH), F32), x_shape, x_shape),
        compiler_params=_params(2 * _nbytes((n_s, L), F32) + 2 * _nbytes((n_s, S5_BLOCK_CH), F32), 3 * _nbytes((n_s, L), F32)),
    )(*_hbm(lanes(u), wb_re, wb_im, lb_re, lb_im, wc_re, wc_im))


def _s5_mix_bwd(dy, du_skip, u, x_re, x_im, wb_re, wb_im, lb_re, lb_im, wc_re, wc_im, *, name):
    n_b, n_s, _ = u.shape
    n_blk = S5_BLOCKS
    lanes = lambda t: t[:, None]
    n_tiles = n_s // TILE_ROWS
    L = S5_BLOCK_LANES

    def body(dy_ref, ds_ref, u_ref, xr_ref, xi_ref, wbr_ref, wbi_ref, lr_ref, li_ref, wcr_ref, wci_ref,
             du_ref, dwbr_ref, dwbi_ref, dlr_ref, dli_ref, dwcr_ref, dwci_ref, ar_ref, ai_ref):
        dyv = dy_ref[0, 0].astype(MXU_DTYPE)
        ar_ref[...] = lax.dot_general(dyv, wcr_ref[0].astype(MXU_DTYPE), _NT, preferred_element_type=F32)
        ai_ref[...] = lax.dot_general(dyv, wci_ref[0].astype(MXU_DTYPE), _NT, preferred_element_type=F32)
        pows = _s5_powers(lr_ref[...], -li_ref[...])
        t_re, t_im = _s5_carry_table(pows, L, True)
        steps = _s5_step_tables(pows, L, True)
        row = lax.broadcasted_iota(jnp.int32, (TILE_ROWS, L), 0)

        def step(k, carry):
            c_re, c_im, s_re, s_im = carry
            i = n_tiles - 1 - k
            rows = _scan_rows(i)
            a_re, a_im = _s5_tile(ar_ref[rows, :], ai_ref[rows, :], steps, True)
            m_re, m_im = _cmul(t_re, t_im, c_re, c_im)
            a_re, a_im = a_re + m_re, a_im + m_im
            ar_ref[rows, :] = a_re
            ai_ref[rows, :] = a_im
            prev = _scan_rows(jnp.maximum(i - 1, 0))
            keep = jnp.where(i > 0, 1.0, 0.0)
            last_re = xr_ref[0, 0, prev, :][TILE_ROWS - 1:, :] * keep
            last_im = xi_ref[0, 0, prev, :][TILE_ROWS - 1:, :] * keep
            xp_re = jnp.where(row == 0, last_re, pltpu.roll(xr_ref[0, 0, rows, :], 1, 0))
            xp_im = jnp.where(row == 0, last_im, pltpu.roll(xi_ref[0, 0, rows, :], 1, 0))
            s_re = s_re + a_re * xp_re + a_im * xp_im
            s_im = s_im + a_im * xp_re - a_re * xp_im
            return a_re[:1, :], a_im[:1, :], s_re, s_im

        zero = jnp.zeros((1, L), F32)
        zt = jnp.zeros((TILE_ROWS, L), F32)
        _, _, s_re, s_im = lax.fori_loop(0, n_tiles, step, (zero, zero, zt, zt), unroll=2)
        first = pl.program_id(1) == 0
        _accumulate(dlr_ref, jnp.sum(s_re, axis=0, keepdims=True), first)
        _accumulate(dli_ref, jnp.sum(s_im, axis=0, keepdims=True), first)
        a_re, a_im = ar_ref[...].astype(MXU_DTYPE), ai_ref[...].astype(MXU_DTYPE)
        du = (lax.dot_general(a_re, wbr_ref[0].astype(MXU_DTYPE), _NT, preferred_element_type=F32)
              + lax.dot_general(a_im, wbi_ref[0].astype(MXU_DTYPE), _NT, preferred_element_type=F32))
        du_ref[0, 0] = (du + ds_ref[0, 0]).astype(du_ref.dtype)
        uv = u_ref[0, 0].astype(MXU_DTYPE)
        _accumulate(dwbr_ref, lax.dot_general(uv, a_re, _TN, preferred_element_type=F32)[None], first)
        _accumulate(dwbi_ref, lax.dot_general(uv, a_im, _TN, preferred_element_type=F32)[None], first)
        _accumulate(dwcr_ref, lax.dot_general(xr_ref[0, 0].astype(MXU_DTYPE), dyv, _TN, preferred_element_type=F32)[None], first)
        _accumulate(dwci_ref, lax.dot_general(xi_ref[0, 0].astype(MXU_DTYPE), dyv, _TN, preferred_element_type=F32)[None], first)

    act, state, lam, w_in, w_out = _s5_mix_specs(n_s, lambda j, b: (j, b))
    lam_shape = jax.ShapeDtypeStruct((1, S5_LANES), F32)
    return pl.pallas_call(
        body, name=name, grid=(n_blk, n_b),
        in_specs=[act, act, act, state, state, w_in, w_in, lam, lam, w_out, w_out],
        out_specs=(act, w_in, w_in, lam, lam, w_out, w_out),
        out_shape=(jax.ShapeDtypeStruct((n_b, 1, n_s, S5_WIDTH), MXU_DTYPE), jax.ShapeDtypeStruct(wb_re.shape, F32),
                   jax.ShapeDtypeStruct(wb_im.shape, F32), lam_shape, lam_shape,
                   jax.ShapeDtypeStruct(wc_re.shape, F32), jax.ShapeDtypeStruct(wc_im.shape, F32)),
        scratch_shapes=[pltpu.VMEM((n_s, L), F32), pltpu.VMEM((n_s, L), F32)],
        compiler_params=_params(2 * _nbytes((n_s, L), F32) + 4 * _nbytes((n_s, S5_BLOCK_CH), F32), 5 * _nbytes((n_s, L), F32)),
    )(*_hbm(lanes(dy), lanes(du_skip), lanes(u), x_re, x_im, wb_re, wb_im, lb_re, lb_im, wc_re, wc_im))


def _s5_out_fn(ymm, u, d_skip, w_glu, b_glu):
    y = jax.nn.gelu(ymm + d_skip * u)
    return y * jax.nn.sigmoid(_mdot(y, w_glu, _NN) + b_glu)


def _s5_out_specs(tm):
    rows = pl.BlockSpec((tm, S5_WIDTH), lambda i: (i, 0))
    vec = pl.BlockSpec((1, S5_WIDTH), lambda i: (0, 0))
    mat = pl.BlockSpec((S5_WIDTH, S5_WIDTH), lambda i: (0, 0))
    return rows, vec, mat


def _s5_out(ymm, u, d_skip, w_glu, b_glu, *, name, tm=512):
    n_tok = ymm.shape[0]
    tm = min(tm, n_tok)
    rows, vec, mat = _s5_out_specs(tm)

    def body(y_ref, u_ref, d_ref, w_ref, b_ref, o_ref):
        o_ref[...] = _s5_out_fn(y_ref[...], u_ref[...], d_ref[...], w_ref[...], b_ref[...]).astype(o_ref.dtype)

    return pl.pallas_call(
        body, name=name, grid=(n_tok // tm,), in_specs=[rows, rows, vec, mat, vec], out_specs=rows,
        out_shape=jax.ShapeDtypeStruct((n_tok, S5_WIDTH), MXU_DTYPE),
        compiler_params=_params(4 * _nbytes((tm, S5_WIDTH), F32), 8 * _nbytes((tm, S5_WIDTH), F32)),
    )(*_hbm(ymm, u, d_skip, w_glu, b_glu))


def _s5_out_bwd(ymm, u, d_skip, w_glu, b_glu, dout, *, name, tm=512):
    n_tok = ymm.shape[0]
    tm = min(tm, n_tok)
    rows, vec, mat = _s5_out_specs(tm)

    def body(y_ref, u_ref, d_ref, w_ref, b_ref, do_ref, dy_ref, du_ref, dd_ref, dw_ref, db_ref):
        _, pull = jax.vjp(_s5_out_fn, y_ref[...], u_ref[...], d_ref[...], w_ref[...].astype(F32), b_ref[...])
        dy, du, dd, dw, db = pull(do_ref[...])
        dy_ref[...] = dy.astype(dy_ref.dtype)
        du_ref[...] = du
        first = pl.program_id(0) == 0
        _accumulate(dd_ref, dd, first)
        _accumulate(dw_ref, dw, first)
        _accumulate(db_ref, db, first)

    return pl.pallas_call(
        body, name=name, grid=(n_tok // tm,), in_specs=[rows, rows, vec, mat, vec, rows],
        out_specs=(rows, rows, vec, mat, vec),
        out_shape=(jax.ShapeDtypeStruct(ymm.shape, MXU_DTYPE), jax.ShapeDtypeStruct(ymm.shape, F32),
                   jax.ShapeDtypeStruct((1, S5_WIDTH), F32), jax.ShapeDtypeStruct((S5_WIDTH, S5_WIDTH), F32),
                   jax.ShapeDtypeStruct((1, S5_WIDTH), F32)),
        compiler_params=_params(6 * _nbytes((tm, S5_WIDTH), F32), 12 * _nbytes((tm, S5_WIDTH), F32)),
    )(*_hbm(ymm, u, d_skip, w_glu, b_glu, dout))


def _merge_fn(ga, gb, ya, yb):
    return jax.nn.sigmoid(ga) * ya + jax.nn.sigmoid(gb) * yb


def _merge(gab, ya, yb, *, name, tm=512):
    n_tok = ya.shape[0]
    tm = min(tm, n_tok)
    rows = pl.BlockSpec((tm, D_MODEL), lambda i: (i, 0))

    def body(ga_ref, gb_ref, ya_ref, yb_ref, o_ref):
        o_ref[...] = _merge_fn(ga_ref[...], gb_ref[...], ya_ref[...], yb_ref[...]).astype(o_ref.dtype)

    return pl.pallas_call(
        body, name=name, grid=(n_tok // tm,),
        in_specs=[rows, pl.BlockSpec((tm, D_MODEL), lambda i: (i, 1)), rows, rows], out_specs=rows,
        out_shape=jax.ShapeDtypeStruct(ya.shape, MXU_DTYPE),
        compiler_params=_params(5 * _nbytes((tm, D_MODEL), F32), 4 * _nbytes((tm, D_MODEL), F32)),
    )(*_hbm(gab, gab, ya, yb))


def _merge_bwd(gab, ya, yb, dout, *, name, tm=512):
    n_tok = ya.shape[0]
    tm = min(tm, n_tok)
    rows = pl.BlockSpec((tm, D_MODEL), lambda i: (i, 0))

    def body(ga_ref, gb_ref, ya_ref, yb_ref, do_ref, *out_refs):
        _, pull = jax.vjp(_merge_fn, ga_ref[...], gb_ref[...], ya_ref[...], yb_ref[...])
        for ref, val in zip(out_refs, pull(do_ref[...])):
            ref[...] = val.astype(ref.dtype)

    shape = jax.ShapeDtypeStruct(ya.shape, MXU_DTYPE)
    return pl.pallas_call(
        body, name=name, grid=(n_tok // tm,),
        in_specs=[rows, pl.BlockSpec((tm, D_MODEL), lambda i: (i, 1)), rows, rows, rows],
        out_specs=(rows, rows, rows, rows), out_shape=(shape, shape, shape, shape),
        compiler_params=_params(7 * _nbytes((tm, D_MODEL), F32), 6 * _nbytes((tm, D_MODEL), F32)),
    )(*_hbm(gab, gab, ya, yb, dout))


ADA_SHARD = N_MOD * D_MODEL // N_SHARD


def _ada_fwd(c_pad, w_s, b_s, *, name):
    n_r = c_pad.shape[0]

    def body(c_ref, w_ref, b_ref, o_ref):
        sc = jax.nn.silu(c_ref[...]).astype(MXU_DTYPE)
        o_ref[0] = lax.dot_general(sc, w_ref[0].astype(MXU_DTYPE), _NN, preferred_element_type=F32) + b_ref[0]

    return pl.pallas_call(
        body, name=name, grid=(N_SHARD,),
        in_specs=[pl.BlockSpec((n_r, D_MODEL), lambda s: (0, 0)),
                  pl.BlockSpec((1, D_MODEL, ADA_SHARD), lambda s: (s, 0, 0)),
                  pl.BlockSpec((1, 1, ADA_SHARD), lambda s: (s, 0, 0))],
        out_specs=pl.BlockSpec((1, n_r, ADA_SHARD), lambda s: (s, 0, 0)),
        out_shape=jax.ShapeDtypeStruct((N_SHARD, n_r, ADA_SHARD), F32),
        compiler_params=_params(_nbytes((D_MODEL, ADA_SHARD), w_s.dtype), 1 << 20),
    )(*_hbm(c_pad, w_s, b_s))


def _ada_bwd(c_pad, dmod_s, *, name):
    n_r = c_pad.shape[0]

    def body(c_ref, d_ref, dw_ref, db_ref):
        sc = jax.nn.silu(c_ref[...])
        dm = d_ref[0]
        dw_ref[0] = _fdot(sc, dm, _TN)
        db_ref[0] = jnp.sum(dm, axis=0, keepdims=True)

    return pl.pallas_call(
        body, name=name, grid=(N_SHARD,),
        in_specs=[pl.BlockSpec((n_r, D_MODEL), lambda s: (0, 0)), pl.BlockSpec((1, n_r, ADA_SHARD), lambda s: (s, 0, 0))],
        out_specs=(pl.BlockSpec((1, D_MODEL, ADA_SHARD), lambda s: (s, 0, 0)),
                   pl.BlockSpec((1, 1, ADA_SHARD), lambda s: (s, 0, 0))),
        out_shape=(jax.ShapeDtypeStruct((N_SHARD, D_MODEL, ADA_SHARD), F32),
                   jax.ShapeDtypeStruct((N_SHARD, 1, ADA_SHARD), F32)),
        compiler_params=_params(_nbytes((D_MODEL, ADA_SHARD), F32), 2 * _nbytes((D_MODEL, ADA_SHARD), F32)),
    )(*_hbm(c_pad, dmod_s))


def _block_diag(blocks):
    n_per = S5_GROUPS // S5_BLOCKS
    _, n_r, n_c = blocks.shape
    b4 = blocks.reshape(S5_BLOCKS, n_per, n_r, n_c)
    eye = jnp.eye(n_per, dtype=blocks.dtype)
    return (b4[:, :, :, None, :] * eye[None, :, None, :, None]).reshape(S5_BLOCKS, n_per * n_r, n_per * n_c)


def _diag_blocks(mat, n_r, n_c):
    n_per = S5_GROUPS // S5_BLOCKS
    m5 = mat.reshape(S5_BLOCKS, n_per, n_r, n_per, n_c)
    eye = jnp.eye(n_per, dtype=mat.dtype)
    return jnp.sum(m5 * eye[None, :, None, :, None], axis=3).reshape(S5_GROUPS, n_r, n_c)


def _local_step(x, c, target, wts):
    n_b, n_s, _ = x.shape
    n_tok = n_b * n_s
    flat = lambda t: t.reshape(n_tok, t.shape[-1])
    unflat = lambda t: t.reshape(n_b, n_s, t.shape[-1])
    n_chunks = n_s // CHUNK

    c_pad = jnp.zeros((SUBLANES, D_MODEL), F32).at[:n_b].set(c)
    mod_s = _ada_fwd(c_pad, wts["w_ada"], wts["b_ada"], name="ada_fwd")
    mod = mod_s.transpose(1, 0, 2).reshape(SUBLANES, N_MOD * D_MODEL)[:n_b]
    sh1, sc1, gt1, sh2, sc2, gt2, sh3, sc3, gt3 = [m[:, None, :] for m in jnp.split(mod, N_MOD, axis=-1)]

    a1 = _pre(x, None, None, wts["g_ffn1"], sh1, sc1, 0.0, name="pre1")
    f1, ffn1_saved = _ffn_fwd(flat(a1), wts["w1_ffn1"], wts["w3_ffn1"], wts["w2_ffn1"], "ffn1")
    x1, a2 = _pre(x, unflat(f1), gt1, wts["g_mix"], sh2, sc2, 0.5, name="pre2")
    u = flat(a2)[None]
    p_qkv = _mm_act([(u, wts["w_qkv"])], "nt", name="in_qkv")[0]
    p_z = _mm_act([(u, wts["w_z"])], "nt", name="in_z")[0]
    p_gab = _mm_act([(u, wts["w_gab"])], "nt", name="in_gab")[0]
    p_s5 = _mm_act([(u, wts["w_s5"])], "nt", name="in_s5")[0]
    p_ba = _mm_act([(u, wts["w_ba"])], "nt", name="in_ba")[0]

    qkv_c = _conv_fwd(unflat(p_qkv), wts["conv_qkv"], name="conv_fwd")
    z_tok = unflat(p_z)
    ba = p_ba.reshape(n_b, n_s, BA_PAD)
    head_rows = lambda t: t.transpose(0, 2, 1).reshape(n_b, DN_HEADS, n_chunks, 1, CHUNK)
    blr = head_rows(ba[:, :, :DN_HEADS])
    lar = head_rows(ba[:, :, DN_HEADS:2 * DN_HEADS])
    a_log, dt_bias = wts["a_log"], wts["dt_bias"]
    dn_in = (qkv_c, blr, lar, a_log, dt_bias)
    qd, kd, uc, wc, attn, g_last, dn_inv = _dn_prep(*dn_in, name="dn_prep")
    o, states = _dn_scan(qd, kd, uc, wc, attn, g_last, name="dn_scan")
    og = _dn_post(o, z_tok, wts["g_onorm"], name="dn_post")
    og_t = og.reshape(1, n_tok, DN_WIDTH)
    ya = _mm_act([(og_t, wts["w_proj_a"])], "nn", name="proj_a")[0]

    s5p_in = (wts["lam_re"], wts["lam_im"], wts["log_step"], wts["bt_re"], wts["bt_im"], wts["c_im"])
    lb_re, lb_im, bb_re, bb_im, c_neg = _s5_prep(*s5p_in, name="s5_prep")
    wb_re, wb_im = _block_diag(bb_re), _block_diag(bb_im)
    wc_re = _block_diag(wts["c_re"].transpose(0, 2, 1))
    wc_im = _block_diag(c_neg.transpose(0, 2, 1))
    lbr, lbi = lb_re.reshape(1, S5_LANES), lb_im.reshape(1, S5_LANES)
    s5_w = (wb_re, wb_im, lbr, lbi, wc_re, wc_im)
    ymm, x_re, x_im = _s5_mix(unflat(p_s5), *s5_w, name="s5_mix")
    ymm = ymm.reshape(n_tok, S5_WIDTH)
    y2 = _s5_out(ymm, p_s5, wts["d_skip"], wts["w_glu"], wts["b_glu"], name="s5_out")
    yb = _mm_act([(y2[None], wts["w_proj_b"])], "nn", name="proj_b")[0]

    merged = _merge(p_gab, ya, yb, name="merge")
    m_out = _mm_act([(merged[None], wts["w_out"])], "nn", name="mix_out")[0]
    x2, a3 = _pre(x1, unflat(m_out), gt2, wts["g_ffn2"], sh3, sc3, 1.0, name="pre3")
    f3, ffn2_saved = _ffn_fwd(flat(a3), wts["w1_ffn2"], wts["w3_ffn2"], wts["w2_ffn2"], "ffn2")

    g = {}
    loss, dx2_res, df3, dgt3, g["g_final"] = _final(x2, unflat(f3), gt3, wts["g_final"], target, name="final")
    da3, g["w1_ffn2"], g["w3_ffn2"], g["w2_ffn2"] = _ffn_bwd(
        flat(a3), wts["w1_ffn2"], wts["w3_ffn2"], wts["w2_ffn2"], ffn2_saved, flat(df3), "ffn2")
    dx1_res, dm_out, dgt2, g["g_ffn2"], dsh3, dsc3 = _pre_bwd(
        x1, unflat(m_out), gt2, wts["g_ffn2"], sh3, sc3, 1.0, unflat(da3), dx2_res, name="pre3_bwd")
    dm_out = flat(dm_out)[None]
    dmerged = _mm_act([(dm_out, wts["w_out"])], "nt", name="mix_out_bwd")[0]
    g["w_out"] = _mm_tn(merged[None], dm_out, name="dw_out")[0]
    dga, dgb, dya, dyb = _merge_bwd(p_gab, ya, yb, dmerged, name="merge_bwd")

    dy2 = _mm_act([(dyb[None], wts["w_proj_b"])], "nt", name="proj_b_bwd")[0]
    g["w_proj_b"] = _mm_tn(y2[None], dyb[None], name="dw_proj_b")[0]
    dymm, du_skip, g["d_skip"], g["w_glu"], g["b_glu"] = _s5_out_bwd(
        ymm, p_s5, wts["d_skip"], wts["w_glu"], wts["b_glu"], dy2, name="s5_out_bwd")
    dp_s5, dwb_re, dwb_im, dlb_re, dlb_im, dwc_re, dwc_im = _s5_mix_bwd(
        unflat(dymm), unflat(du_skip), unflat(p_s5), x_re, x_im, *s5_w, name="s5_mix_bwd")
    dp_s5 = dp_s5.reshape(n_tok, S5_WIDTH)
    g["c_re"] = _diag_blocks(dwc_re, S5_STATE, S5_GROUP_CH).transpose(0, 2, 1)
    s5_cts = (dlb_re.reshape(lb_re.shape), dlb_im.reshape(lb_im.shape),
              _diag_blocks(dwb_re, S5_GROUP_CH, S5_STATE), _diag_blocks(dwb_im, S5_GROUP_CH, S5_STATE),
              _diag_blocks(dwc_im, S5_STATE, S5_GROUP_CH).transpose(0, 2, 1))
    g["lam_re"], g["lam_im"], g["log_step"], g["bt_re"], g["bt_im"], g["c_im"] = _s5_prep_bwd(
        *s5p_in, s5_cts, name="s5_prep_bwd")

    dog = _mm_act([(dya[None], wts["w_proj_a"])], "nt", name="proj_a_bwd")[0]
    g["w_proj_a"] = _mm_tn(og_t, dya[None], name="dw_proj_a")[0]
    do, dz, g["g_onorm"] = _dn_post_bwd(o, z_tok, wts["g_onorm"], unflat(dog), name="dn_post_bwd")
    scan_cts = _dn_scan_bwd(qd, kd, uc, wc, attn, g_last, states, do, name="dn_scan_bwd")
    dqkv_c, dblr, dlar, g["a_log"], g["dt_bias"] = _dn_prep_bwd(*dn_in, dn_inv, uc, wc, scan_cts, name="dn_prep_bwd")
    dqkv, g["conv_qkv"] = _conv_bwd(unflat(p_qkv), wts["conv_qkv"], dqkv_c, name="conv_bwd")
    token_cols = lambda t: t.reshape(n_b, DN_HEADS, n_s).transpose(0, 2, 1)
    dba = jnp.concatenate([token_cols(dblr), token_cols(dlar),
                           jnp.zeros((n_b, n_s, BA_PAD - 2 * DN_HEADS), F32)], axis=-1).astype(MXU_DTYPE)

    dps = {"w_qkv": flat(dqkv)[None], "w_z": flat(dz)[None], "w_ga": dga[None], "w_gb": dgb[None],
           "w_s5": dp_s5[None], "w_ba": flat(dba)[None]}
    w_ga, w_gb = wts["w_gab"][:, :D_MODEL], wts["w_gab"][:, D_MODEL:]
    w_of = dict(wts, w_ga=w_ga, w_gb=w_gb)
    du = _mm_act([(dps[k], w_of[k]) for k in dps], "nn", name="in_bwd")[0]
    for k in dps:
        g[k] = _mm_tn(dps[k], u, name=f"d{k}")[0]
    dx0_res, df1, dgt1, g["g_mix"], dsh2, dsc2 = _pre_bwd(
        x, unflat(f1), gt1, wts["g_mix"], sh2, sc2, 0.5, unflat(du), dx1_res, name="pre2_bwd")
    da1, g["w1_ffn1"], g["w3_ffn1"], g["w2_ffn1"] = _ffn_bwd(
        flat(a1), wts["w1_ffn1"], wts["w3_ffn1"], wts["w2_ffn1"], ffn1_saved, flat(df1), "ffn1")
    grad_x, g["g_ffn1"], dsh1, dsc1 = _pre_bwd(
        x, None, None, wts["g_ffn1"], sh1, sc1, 0.0, unflat(da1), dx0_res, name="pre1_bwd")

    dmod = jnp.concatenate([t[:, 0, :] for t in (dsh1, dsc1, dgt1, dsh2, dsc2, dgt2, dsh3, dsc3, dgt3)], axis=-1)
    return loss, grad_x, g, dmod


def _ada_grads(c_rows, dmod_rows):
    n_r = c_rows.shape[0]
    n_pad = -n_r % SUBLANES
    c_pad = jnp.pad(c_rows, ((0, n_pad), (0, 0)))
    dmod_s = jnp.pad(dmod_rows, ((0, n_pad), (0, 0))).reshape(n_r + n_pad, N_SHARD, ADA_SHARD).transpose(1, 0, 2)
    dw, db = _ada_bwd(c_pad, dmod_s, name="ada_bwd")
    return dw, db.reshape(1, N_MOD * D_MODEL)


IN_SPLITS = (("w_qkv", 3 * DN_WIDTH), ("w_z", DN_WIDTH), ("w_ba", 2 * DN_HEADS), ("w_s5", S5_WIDTH),
             ("w_ga", D_MODEL), ("w_gb", D_MODEL))
SHARDED = ("w_ada", "w1_ffn1", "w3_ffn1", "w2_ffn1", "w_in", "conv_qkv", "w_glu", "w_proj_a", "w_proj_b", "w_out",
           "w1_ffn2", "w3_ffn2", "w2_ffn2")


def _cat_columns(stack):
    return stack.transpose(1, 0, 2).reshape(stack.shape[1], N_SHARD * stack.shape[2])


def _split_columns(full):
    n_r, n_c = full.shape
    return full.reshape(n_r, N_SHARD, n_c // N_SHARD).transpose(1, 0, 2)


def _gathered_weights(st, rep):
    w = {k: st[k] for k in ("w_ada", "w1_ffn1", "w3_ffn1", "w2_ffn1", "w1_ffn2", "w3_ffn2", "w2_ffn2")}
    w["b_ada"] = rep["b_ada"].reshape(N_SHARD, 1, ADA_SHARD)
    for k in ("g_ffn1", "g_mix", "g_ffn2", "g_final"):
        w[k] = rep[k].reshape(1, D_MODEL)
    w_in_t = st["w_in"].reshape(N_SHARD * st["w_in"].shape[1], D_MODEL)
    start = 0
    for k, size in IN_SPLITS:
        w[k] = w_in_t[None, start:start + size]
        start += size
    w["w_gab"] = jnp.concatenate([w.pop("w_ga"), w.pop("w_gb")], axis=1)
    w["w_ba"] = jnp.pad(w["w_ba"], ((0, 0), (0, BA_PAD - 2 * DN_HEADS), (0, 0)))
    w["conv_qkv"] = _cat_columns(st["conv_qkv"])
    w["a_log"] = rep["a_log"].reshape(DN_HEADS, 1, 1)
    w["dt_bias"] = rep["dt_bias"].reshape(DN_HEADS, 1, 1)
    w["g_onorm"] = rep["g_onorm"].reshape(1, DN_HEAD_DIM)
    w["lam_re"] = rep["lam_re"].reshape(S5_GROUPS, 1, S5_STATE)
    w["lam_im"] = rep["lam_im"].reshape(S5_GROUPS, 1, S5_STATE)
    w["log_step"] = rep["log_step"].reshape(S5_GROUPS, 1, 1)
    w["bt_re"] = rep["b_re"][0].transpose(0, 2, 1)
    w["bt_im"] = rep["b_im"][0].transpose(0, 2, 1)
    w["c_re"] = rep["c_re"][0]
    w["c_im"] = rep["c_im"][0]
    w["d_skip"] = rep["d_skip"].reshape(1, S5_WIDTH)
    w["b_glu"] = rep["b_glu"].reshape(1, S5_WIDTH)
    w["w_glu"] = st["w_glu"].reshape(S5_WIDTH, S5_WIDTH)
    w["w_proj_a"] = _cat_columns(st["w_proj_a"])[None]
    w["w_proj_b"] = _cat_columns(st["w_proj_b"])[None]
    w["w_out"] = st["w_out"].reshape(1, D_MODEL, D_MODEL)
    return w


def _grads_to_problem_layout(g):
    st = {k: g[k] for k in ("w1_ffn1", "w3_ffn1", "w2_ffn1", "w1_ffn2", "w3_ffn2", "w2_ffn2")}
    w_in_t = jnp.concatenate([g[k][:size] for k, size in IN_SPLITS], axis=0)
    st["w_in"] = w_in_t.reshape(N_SHARD, w_in_t.shape[0] // N_SHARD, D_MODEL)
    st["w_glu"] = g["w_glu"].reshape(N_SHARD, S5_WIDTH // N_SHARD, S5_WIDTH)
    st["w_proj_a"] = _split_columns(g["w_proj_a"])
    st["w_proj_b"] = _split_columns(g["w_proj_b"])
    st["w_out"] = g["w_out"].reshape(N_SHARD, D_MODEL // N_SHARD, D_MODEL)
    small = {
        "g_ffn1": g["g_ffn1"], "g_mix": g["g_mix"], "g_ffn2": g["g_ffn2"], "g_final": g["g_final"].reshape(D_MODEL),
        "conv_qkv": g["conv_qkv"][None],
        "a_log": g["a_log"].reshape(1, DN_HEADS), "dt_bias": g["dt_bias"].reshape(1, DN_HEADS),
        "g_onorm": g["g_onorm"],
        "lam_re": g["lam_re"].reshape(1, S5_GROUPS, S5_STATE), "lam_im": g["lam_im"].reshape(1, S5_GROUPS, S5_STATE),
        "log_step": g["log_step"].reshape(1, S5_GROUPS),
        "b_re": g["bt_re"].transpose(0, 2, 1)[None], "b_im": g["bt_im"].transpose(0, 2, 1)[None],
        "c_re": g["c_re"][None], "c_im": g["c_im"][None],
        "d_skip": g["d_skip"], "b_glu": g["b_glu"],
    }
    return st, small


ELEMENTWISE_BLOCK_BYTES = 1 << 20


def _row_tile(n_rows, n_cols, n_lead=1, multiple=SUBLANES):
    best = None
    for t in range(multiple, n_rows + 1, multiple):
        if n_rows % t == 0 and n_lead * t * n_cols * 4 <= ELEMENTWISE_BLOCK_BYTES:
            best = t
    return best if best is not None else n_rows


def _add_sibling_half(g4, recv, my_c, *, name):
    n_sh, _, n_h, n_c = g4.shape
    th = _row_tile(n_h, n_c, multiple=2 * SUBLANES)

    def body(c_ref, g_ref, r_ref, o_ref):
        o_ref[0] = (g_ref[0, 0] + r_ref[0]).astype(o_ref.dtype)

    grid_spec = pltpu.PrefetchScalarGridSpec(
        num_scalar_prefetch=1, grid=(n_sh, n_h // th),
        in_specs=[pl.BlockSpec((1, 1, th, n_c), lambda s, i, c_ref: (s, c_ref[0], i, 0)),
                  pl.BlockSpec((1, th, n_c), lambda s, i, c_ref: (s, i, 0))],
        out_specs=pl.BlockSpec((1, th, n_c), lambda s, i, c_ref: (s, i, 0)))
    return pl.pallas_call(
        body, name=name, grid_spec=grid_spec, out_shape=jax.ShapeDtypeStruct((n_sh, n_h, n_c), MXU_DTYPE),
        compiler_params=_params(3 * _nbytes((th, n_c), F32)),
    )(*_hbm(my_c, g4, recv))


def _sum_slots(parts, *, name):
    n_p, n_r, n_c = parts.shape
    th = _row_tile(n_r, n_c, n_p)

    def body(p_ref, o_ref):
        total = p_ref[0].astype(F32)
        for k in range(1, n_p):
            total = total + p_ref[k].astype(F32)
        o_ref[...] = total

    return pl.pallas_call(
        body, name=name, grid=(n_r // th,),
        in_specs=[pl.BlockSpec((n_p, th, n_c), lambda i: (0, i, 0))],
        out_specs=pl.BlockSpec((th, n_c), lambda i: (i, 0)),
        out_shape=jax.ShapeDtypeStruct((n_r, n_c), F32),
        compiler_params=_params((n_p + 1) * _nbytes((th, n_c), F32)),
    )(*_hbm(parts))


def _cast_into_slot(w, place, dtype, *, name):
    n_r, n_c = w.shape
    th = _row_tile(n_r, n_c, multiple=2 * SUBLANES)

    def body(p_ref, w_ref, o_ref):
        o_ref[0] = w_ref[...].astype(o_ref.dtype)

    grid_spec = pltpu.PrefetchScalarGridSpec(
        num_scalar_prefetch=1, grid=(n_r // th,),
        in_specs=[pl.BlockSpec((th, n_c), lambda i, p: (i, 0))],
        out_specs=pl.BlockSpec((1, th, n_c), lambda i, p: (p[1], i, 0)))
    return pl.pallas_call(
        body, name=name, grid_spec=grid_spec, out_shape=jax.ShapeDtypeStruct((N_SHARD, n_r, n_c), dtype),
        compiler_params=_params(2 * _nbytes((th, n_c), F32)),
    )(*_hbm(place, w))


def _sum_chips(own, parts, place, *, name):
    n_sh, n_h, n_c = own.shape
    th = _row_tile(n_h, n_c, n_sh, multiple=2 * SUBLANES)

    def body(p_ref, own_ref, a_ref, b_ref, c_ref, o_ref):
        o_ref[0] = ((own_ref[0].astype(F32) + a_ref[0].astype(F32)) + b_ref[0].astype(F32)) + c_ref[0].astype(F32)

    slab = lambda k: pl.BlockSpec((1, th, n_c), lambda i, p, k=k: (p[k], i, 0))
    grid_spec = pltpu.PrefetchScalarGridSpec(
        num_scalar_prefetch=1, grid=(n_h // th,),
        in_specs=[slab(1), slab(2), slab(3), slab(4)], out_specs=slab(0))
    return pl.pallas_call(
        body, name=name, grid_spec=grid_spec, out_shape=jax.ShapeDtypeStruct((2, n_h, n_c), F32),
        compiler_params=_params(5 * _nbytes((th, n_c), F32)),
    )(*_hbm(place, own, parts, parts, parts))


def _adamw(w, g, m, v, *, name):
    n_r, n_c = w.shape
    th = _row_tile(n_r, n_c)
    tc = n_c
    if th == n_r and n_c % LANES == 0:
        tc = max(t for t in range(LANES, n_c + 1, LANES) if n_c % t == 0 and (n_r * t * 4 <= ELEMENTWISE_BLOCK_BYTES or t == LANES))
    bias1 = 1.0 - ADAM_B1 ** ADAM_STEP
    bias2 = 1.0 - ADAM_B2 ** ADAM_STEP

    def body(w_ref, g_ref, m_ref, v_ref, d_ref, mo_ref, vo_ref):
        gv = g_ref[...]
        m_new = ADAM_B1 * m_ref[...] + (1.0 - ADAM_B1) * gv
        v_new = ADAM_B2 * v_ref[...] + (1.0 - ADAM_B2) * jnp.square(gv)
        d_ref[...] = -ADAM_LR * ((m_new / bias1) / (jnp.sqrt(v_new / bias2) + ADAM_EPS) + ADAM_WD * w_ref[...])
        mo_ref[...] = m_new
        vo_ref[...] = v_new

    spec = pl.BlockSpec((th, tc), lambda i, j: (i, j))
    shape = jax.ShapeDtypeStruct((n_r, n_c), F32)
    return pl.pallas_call(
        body, name=name, grid=(n_r // th, n_c // tc), in_specs=[spec] * 4, out_specs=(spec,) * 3, out_shape=(shape,) * 3,
        compiler_params=_params(7 * _nbytes((th, tc), F32)),
    )(*_hbm(w, g, m, v))


CHIP_FLIPS = ((1, 0), (0, 1), (1, 1))
DEVICE_FLIPS = tuple((fx, fy, fc) for fx in (0, 1) for fy in (0, 1) for fc in (0, 1))[1:]


def _exchange(ins, out_shapes, plan, n_local, n_remote, *, name, aliased=False):
    n_in, n_out = len(ins), len(out_shapes)

    def body(*refs):
        in_refs, out_refs = refs[:n_in], refs[n_in:n_in + n_out]
        send_sems, recv_sems, local_sems = refs[n_in + n_out:]
        me = (lax.axis_index("x"), lax.axis_index("y"), lax.axis_index("c"))
        local, remote = plan(in_refs, out_refs, me)
        assert len(local) == n_local and len(remote) == n_remote
        here = [pltpu.make_async_copy(src, dst, local_sems.at[i]) for i, (src, dst) in enumerate(local)]
        for cp in here:
            cp.start()
        sends = [pltpu.make_async_remote_copy(src_ref=src, dst_ref=dst, send_sem=send_sems.at[i], recv_sem=recv_sems.at[i],
                                              device_id=peer, device_id_type=pl.DeviceIdType.MESH)
                 for i, (src, dst, _, peer) in enumerate(remote)]
        for cp in sends:
            cp.start()
        for i, (src, _, landing, peer) in enumerate(remote):
            pltpu.make_async_remote_copy(src_ref=src, dst_ref=landing, send_sem=send_sems.at[i], recv_sem=recv_sems.at[i],
                                         device_id=peer, device_id_type=pl.DeviceIdType.MESH).wait_recv()
        for cp in sends:
            cp.wait_send()
        for cp in here:
            cp.wait()

    any_spec = pl.BlockSpec(memory_space=pl.ANY)
    return pl.pallas_call(
        body, name=name, in_specs=[any_spec] * n_in, out_specs=tuple([any_spec] * n_out), out_shape=tuple(out_shapes),
        scratch_shapes=[pltpu.SemaphoreType.DMA((n_remote,)), pltpu.SemaphoreType.DMA((n_remote,)),
                        pltpu.SemaphoreType.DMA((max(n_local, 1),))],
        input_output_aliases={k: k for k in range(n_in)} if aliased else {},
    )(*ins)


def _gather_shards(stacks, *, name):
    n = len(stacks)
    halved = [a.shape[1] % 64 == 0 for a in stacks]
    unit_rows = [a.shape[1] // 2 if h else a.shape[1] for a, h in zip(stacks, halved)]
    part1_rows = [(r // 32) * 16 if r >= 32 else r for r in unit_rows]
    has_part2 = [p < r for p, r in zip(part1_rows, unit_rows)]
    n_sem = sum(2 + 1 + int(h2) + 3 * int(h) for h2, h in zip(has_part2, halved))

    def body(*refs):
        outs = refs[n:2 * n]
        send_sems, recv_sems = refs[2 * n:]
        x, y, c = lax.axis_index("x"), lax.axis_index("y"), lax.axis_index("c")
        mine, chip_x, chip_y, chip_d = 2 * x + y, 2 * (1 - x) + y, 2 * x + (1 - y), 2 * (1 - x) + (1 - y)
        to_x, to_y, sibling = (1 - x, y, c), (x, 1 - y, c), (x, y, 1 - c)

        def region(k, slot, half, part=None):
            start = half * unit_rows[k] if halved[k] else 0
            size = unit_rows[k]
            if part == 1:
                size = part1_rows[k]
            elif part == 2:
                start, size = start + part1_rows[k], unit_rows[k] - part1_rows[k]
            if not halved[k] and part is None:
                return outs[k].at[slot]
            if halved[k]:
                start = pl.multiple_of(start, 16)
            return outs[k].at[slot, pl.ds(start, size)]

        counter = [0]
        started, pending = [], []

        def send(region_of, peer, landing_of):
            i = counter[0]
            counter[0] += 1
            src = region_of
            cp = pltpu.make_async_remote_copy(src_ref=src, dst_ref=src, send_sem=send_sems.at[i], recv_sem=recv_sems.at[i],
                                              device_id=peer, device_id_type=pl.DeviceIdType.MESH)
            cp.start()
            started.append(cp)
            return pltpu.make_async_remote_copy(src_ref=landing_of, dst_ref=landing_of, send_sem=send_sems.at[i],
                                                recv_sem=recv_sems.at[i], device_id=peer, device_id_type=pl.DeviceIdType.MESH)

        from_x = [send(region(k, mine, c), to_x, region(k, chip_x, c)) for k in range(n)]
        from_y = [send(region(k, mine, c), to_y, region(k, chip_y, c)) for k in range(n)]
        diag = []
        for k in range(n):
            from_x[k].wait_recv()
            fwd = [send(region(k, chip_x, c, 1), to_y, region(k, chip_d, c, 1))]
            if halved[k]:
                pending.append(send(region(k, chip_x, c), sibling, region(k, chip_x, 1 - c)))
            from_y[k].wait_recv()
            if has_part2[k]:
                fwd.append(send(region(k, chip_y, c, 2), to_x, region(k, chip_d, c, 2)))
            if halved[k]:
                pending.append(send(region(k, chip_y, c), sibling, region(k, chip_y, 1 - c)))
            diag.append(fwd)
        for k in range(n):
            for landed in diag[k]:
                landed.wait_recv()
            if halved[k]:
                pending.append(send(region(k, chip_d, c), sibling, region(k, chip_d, 1 - c)))
        for landed in pending:
            landed.wait_recv()
        for cp in started:
            cp.wait_send()
        assert counter[0] == n_sem

    any_spec = pl.BlockSpec(memory_space=pl.ANY)
    return pl.pallas_call(
        body, name=name, in_specs=[any_spec] * n, out_specs=tuple([any_spec] * n),
        out_shape=tuple(jax.ShapeDtypeStruct(a.shape, a.dtype) for a in stacks),
        scratch_shapes=[pltpu.SemaphoreType.DMA((n_sem,)), pltpu.SemaphoreType.DMA((n_sem,))],
        input_output_aliases={k: k for k in range(n)},
    )(*stacks)


def _swap_sibling_halves(g4s, *, name):
    n = len(g4s)

    def plan(in_refs, out_refs, me):
        x, y, c = me
        remote = [(in_refs[k].at[:, 1 - c], out_refs[k], out_refs[k], (x, y, 1 - c)) for k in range(n)]
        return [], remote

    shapes = [jax.ShapeDtypeStruct((a.shape[0],) + a.shape[2:], a.dtype) for a in g4s]
    return _exchange(g4s, shapes, plan, 0, n, name=name)


def _scatter_to_chips(hs, *, name):
    n = len(hs)

    def plan(in_refs, out_refs, me):
        x, y, c = me
        mine = 2 * x + y
        remote = []
        for fx, fy in CHIP_FLIPS:
            px, py = x ^ fx, y ^ fy
            peer = 2 * px + py
            for k in range(n):
                remote.append((in_refs[k].at[peer], out_refs[k].at[mine], out_refs[k].at[peer], (px, py, c)))
        return [], remote

    shapes = [jax.ShapeDtypeStruct(a.shape, a.dtype) for a in hs]
    return _exchange(hs, shapes, plan, 0, len(CHIP_FLIPS) * n, name=name)


def _join_sibling_halves(rs, *, name):
    n = len(rs)

    def plan(in_refs, out_refs, me):
        x, y, c = me
        remote = [(out_refs[k].at[c], out_refs[k].at[c], out_refs[k].at[1 - c], (x, y, 1 - c)) for k in range(n)]
        return [], remote

    shapes = [jax.ShapeDtypeStruct(a.shape, a.dtype) for a in rs]
    return _exchange(rs, shapes, plan, 0, n, name=name, aliased=True)


def _gather_all_devices(packed, *, name):
    def plan(in_refs, out_refs, me):
        x, y, c = me
        mine = 4 * x + 2 * y + c
        remote = []
        for fx, fy, fc in DEVICE_FLIPS:
            px, py, pc = x ^ fx, y ^ fy, c ^ fc
            remote.append((in_refs[0], out_refs[0].at[mine], out_refs[0].at[4 * px + 2 * py + pc], (px, py, pc)))
        return [(in_refs[0], out_refs[0].at[mine])], remote

    shape = jax.ShapeDtypeStruct((2 * N_SHARD,) + packed.shape, packed.dtype)
    return _exchange([packed], [shape], plan, 1, len(DEVICE_FLIPS), name=name)[0]


WEIGHT_NAMES = ("w_ada", "b_ada", "g_ffn1", "w1_ffn1", "w3_ffn1", "w2_ffn1", "g_mix", "w_in", "conv_qkv", "a_log",
                "dt_bias", "g_onorm", "lam_re", "lam_im", "log_step", "b_re", "b_im", "c_re", "c_im", "d_skip", "w_glu",
                "b_glu", "w_proj_a", "w_proj_b", "w_out", "g_ffn2", "w1_ffn2", "w3_ffn2", "w2_ffn2", "g_final")
LARGE = tuple(n for n in SHARDED if n != "conv_qkv")
SMALL = tuple(n for n in WEIGHT_NAMES if n not in LARGE)
REDUCED_LARGE = tuple(n for n in LARGE if n != "w_ada")
REDUCED_SMALL = tuple(n for n in SMALL if n != "b_ada")
PACK_ROW = SUBLANES * LANES


def _pack(arrays):
    flat = jnp.concatenate([a.reshape(-1) for a in arrays])
    n_pad = -flat.shape[0] % PACK_ROW
    return jnp.pad(flat, (0, n_pad)).reshape(-1, LANES)


def _unpack(packed, shapes):
    flat = packed.reshape(-1)
    out, start = [], 0
    for s in shapes:
        size = math.prod(s)
        out.append(flat[start:start + size].reshape(s))
        start += size
    return out


def _unpack_slots(gathered, shapes):
    flat = gathered.reshape(gathered.shape[0], -1)
    out, start = [], 0
    for s in shapes:
        size = math.prod(s)
        out.append(flat[:, start:start + size].reshape((gathered.shape[0],) + tuple(s)))
        start += size
    return out


TRANSPOSED = ("w1_ffn1", "w3_ffn1", "w1_ffn2", "w3_ffn2", "w_in")


def _to_internal(name, a):
    return jnp.swapaxes(a[0], 0, 1) if name in TRANSPOSED else a[0]


def _from_internal(name, a):
    return (jnp.swapaxes(a, 0, 1) if name in TRANSPOSED else a)[None]


def _step(x, c, target, weights, m_in, v_in):
    xi, yi, ci = lax.axis_index("x"), lax.axis_index("y"), lax.axis_index("c")
    my_chip = 2 * xi + yi

    others = [k + (k >= my_chip).astype(jnp.int32) for k in range(N_SHARD - 1)]
    place = jnp.stack([ci, my_chip] + others).astype(jnp.int32)

    slots = [_cast_into_slot(_to_internal(n, weights[n]), place, F32 if n == "conv_qkv" else MXU_DTYPE, name=f"cast_{n}")
             for n in SHARDED]
    stacks = dict(zip(SHARDED, _gather_shards(slots, name="gather_weights")))
    rep = {n: weights[n] for n in WEIGHT_NAMES if n not in SHARDED}
    loss, grad_x, g, dmod = _local_step(x, c, target, _gathered_weights(stacks, rep))
    g_stacks, g_small = _grads_to_problem_layout(g)

    g4s = [g_stacks[n].reshape(N_SHARD, 2, g_stacks[n].shape[1] // 2, g_stacks[n].shape[2]) for n in REDUCED_LARGE]
    from_sibling = _swap_sibling_halves(g4s, name="swap_sibling_halves")
    chip_sums = [_add_sibling_half(a, r, place, name=f"chip_sum_{n}") for n, a, r in zip(REDUCED_LARGE, g4s, from_sibling)]
    from_chips = _scatter_to_chips(chip_sums, name="scatter_to_chips")
    reduced = [_sum_chips(h, p, place, name=f"sum_chips_{n}") for n, h, p in zip(REDUCED_LARGE, chip_sums, from_chips)]
    joined = _join_sibling_halves(reduced, name="join_sibling_halves")
    grads_2d = {n: j.reshape(2 * j.shape[1], j.shape[2]) for n, j in zip(REDUCED_LARGE, joined)}
    grads = {n: _from_internal(n, a) for n, a in grads_2d.items()}

    summed_shapes = [g_small[n].shape for n in REDUCED_SMALL] + [(1, 1)]
    packed = _pack([g_small[n] for n in REDUCED_SMALL] + [loss, c, dmod])
    gathered = _gather_all_devices(packed, name="gather_small")
    *small_grads, loss_sum = _unpack(_sum_slots(gathered, name="sum_small"), summed_shapes)
    grads.update(zip(REDUCED_SMALL, small_grads))
    n_conv = weights["conv_qkv"].shape[-1]
    grads["conv_qkv"] = lax.dynamic_slice_in_dim(grads["conv_qkv"], my_chip * n_conv, n_conv, axis=2)
    n_dev = gathered.shape[0]
    rows_of = lambda t: t.reshape(n_dev * t.shape[1], t.shape[2])
    _, c_all, dmod_all = _unpack_slots(gathered, [(sum(math.prod(s) for s in summed_shapes),), c.shape, dmod.shape])
    dw_ada, grads["b_ada"] = _ada_grads(rows_of(c_all), rows_of(dmod_all))
    grads_2d["w_ada"] = lax.dynamic_index_in_dim(dw_ada, my_chip, axis=0, keepdims=False)
    grads["w_ada"] = grads_2d["w_ada"][None]

    delta, new_m, new_v = {}, {}, {}
    grads_2d["conv_qkv"] = grads["conv_qkv"][0]
    for n in LARGE + ("conv_qkv",):
        outs = _adamw(_to_internal(n, weights[n]), grads_2d[n], _to_internal(n, m_in[n]), _to_internal(n, v_in[n]),
                      name=f"adamw_{n}")
        delta[n], new_m[n], new_v[n] = [_from_internal(n, o) for o in outs]
    packed_names = tuple(n for n in SMALL if n != "conv_qkv")
    shapes = [weights[n].shape for n in packed_names]
    outs = _adamw(*[_pack([d[n] for n in packed_names]) for d in (weights, grads, m_in, v_in)], name="adamw_small")
    for d, o in zip((delta, new_m, new_v), outs):
        d.update(zip(packed_names, _unpack(o, shapes)))
    return (loss_sum.reshape(()), grad_x, *[grads[n] for n in WEIGHT_NAMES], *[delta[n] for n in WEIGHT_NAMES],
            *[new_m[n] for n in WEIGHT_NAMES], *[new_v[n] for n in WEIGHT_NAMES])


def kernel(x, c, w_ada, b_ada, g_ffn1, w1_ffn1, w3_ffn1, w2_ffn1, g_mix, w_in, conv_qkv, a_log, dt_bias, g_onorm, lam_re, lam_im, log_step, b_re, b_im, c_re, c_im, d_skip, w_glu, b_glu, w_proj_a, w_proj_b, w_out, g_ffn2, w1_ffn2, w3_ffn2, w2_ffn2, g_final, loss_target, m_w_ada, m_b_ada, m_g_ffn1, m_w1_ffn1, m_w3_ffn1, m_w2_ffn1, m_g_mix, m_w_in, m_conv_qkv, m_a_log, m_dt_bias, m_g_onorm, m_lam_re, m_lam_im, m_log_step, m_b_re, m_b_im, m_c_re, m_c_im, m_d_skip, m_w_glu, m_b_glu, m_w_proj_a, m_w_proj_b, m_w_out, m_g_ffn2, m_w1_ffn2, m_w3_ffn2, m_w2_ffn2, m_g_final, v_w_ada, v_b_ada, v_g_ffn1, v_w1_ffn1, v_w3_ffn1, v_w2_ffn1, v_g_mix, v_w_in, v_conv_qkv, v_a_log, v_dt_bias, v_g_onorm, v_lam_re, v_lam_im, v_log_step, v_b_re, v_b_im, v_c_re, v_c_im, v_d_skip, v_w_glu, v_b_glu, v_w_proj_a, v_w_proj_b, v_w_out, v_g_ffn2, v_w1_ffn2, v_w3_ffn2, v_w2_ffn2, v_g_final):
    w_vals = (w_ada, b_ada, g_ffn1, w1_ffn1, w3_ffn1, w2_ffn1, g_mix, w_in, conv_qkv, a_log, dt_bias, g_onorm, lam_re, lam_im, log_step, b_re, b_im, c_re, c_im, d_skip, w_glu, b_glu, w_proj_a, w_proj_b, w_out, g_ffn2, w1_ffn2, w3_ffn2, w2_ffn2, g_final)
    m_vals = (m_w_ada, m_b_ada, m_g_ffn1, m_w1_ffn1, m_w3_ffn1, m_w2_ffn1, m_g_mix, m_w_in, m_conv_qkv, m_a_log, m_dt_bias, m_g_onorm, m_lam_re, m_lam_im, m_log_step, m_b_re, m_b_im, m_c_re, m_c_im, m_d_skip, m_w_glu, m_b_glu, m_w_proj_a, m_w_proj_b, m_w_out, m_g_ffn2, m_w1_ffn2, m_w3_ffn2, m_w2_ffn2, m_g_final)
    v_vals = (v_w_ada, v_b_ada, v_g_ffn1, v_w1_ffn1, v_w3_ffn1, v_w2_ffn1, v_g_mix, v_w_in, v_conv_qkv, v_a_log, v_dt_bias, v_g_onorm, v_lam_re, v_lam_im, v_log_step, v_b_re, v_b_im, v_c_re, v_c_im, v_d_skip, v_w_glu, v_b_glu, v_w_proj_a, v_w_proj_b, v_w_out, v_g_ffn2, v_w1_ffn2, v_w3_ffn2, v_w2_ffn2, v_g_final)
    return _step(x, c, loss_target, dict(zip(WEIGHT_NAMES, w_vals)), dict(zip(WEIGHT_NAMES, m_vals)),
                 dict(zip(WEIGHT_NAMES, v_vals)))
```

```python
import functools
import math

import jax
import jax.numpy as jnp
from jax import lax
from jax.experimental import pallas as pl
from jax.experimental.pallas import tpu as pltpu

F32 = jnp.float32
BF16 = jnp.bfloat16
MXU_DTYPE = BF16

D_MODEL = 1024
D_FF = 2816
DN_HEADS = 8
DN_HEAD_DIM = 64
DN_WIDTH = DN_HEADS * DN_HEAD_DIM
CONV_WIDTH = 4
CHUNK = 64
S5_GROUP_CH = 16
S5_GROUPS = 32
S5_WIDTH = S5_GROUPS * S5_GROUP_CH
S5_STATE = 64
S5_LANES = S5_GROUPS * S5_STATE
N_MOD = 9
EPS = 1e-6
N_SHARD = 4
FF_SHARD = D_FF // N_SHARD
BA_PAD = 128

ADAM_LR = 0.001
ADAM_B1 = 0.9
ADAM_B2 = 0.999
ADAM_EPS = 1e-08
ADAM_WD = 0.01
ADAM_STEP = 10

VMEM_BYTES_V7X = 64 * 1024 * 1024
SUBLANES = 8
LANES = 128


def _params(block_bytes, extra_bytes=0):
    need = 2 * block_bytes + extra_bytes + (4 << 20)
    return pltpu.CompilerParams(vmem_limit_bytes=int(min(max(need, 16 << 20), VMEM_BYTES_V7X - (8 << 20))))


def _nbytes(shape, dtype):
    return math.prod(shape) * jnp.dtype(dtype).itemsize


HBM_OPERAND_BYTES = 1 << 20


def _hbm(*args):
    return [pltpu.with_memory_space_constraint(a, pltpu.HBM) if _nbytes(a.shape, a.dtype) >= HBM_OPERAND_BYTES else a
            for a in args]


_NN = (((1,), (0,)), ((), ()))
_NT = (((1,), (1,)), ((), ()))
_TN = (((0,), (0,)), ((), ()))


LHS_ROW_BYTES = 4096


def _mm_act(pairs, mode, *, name, reduce_shards=False, out_dtype=F32, tm=None):
    n_tok = pairs[0][0].shape[1]
    n_out = pairs[0][1].shape[2] if mode == "nn" else pairs[0][1].shape[1]
    if tm is None:
        row_bytes = sum(a.shape[2] * jnp.dtype(a.dtype).itemsize for a, _ in pairs)
        tm = 1024 if row_bytes <= LHS_ROW_BYTES else 512
    tm = min(tm, n_tok)
    tn = n_out if n_out <= 1536 else 1024
    assert n_tok % tm == 0 and n_out % tn == 0
    n_red = N_SHARD if reduce_shards else 1
    grid = (n_tok // tm, n_out // tn, n_red)
    dims = _NN if mode == "nn" else _NT
    shard_of = lambda n_sh: (lambda r: 0) if n_sh == 1 else (lambda r: r)

    in_specs, args, blk = [], [], 0
    for a, b in pairs:
        k_dim = a.shape[2]
        sa, sb = shard_of(a.shape[0]), shard_of(b.shape[0])
        in_specs.append(pl.BlockSpec((1, tm, k_dim), lambda i, j, r, sa=sa: (sa(r), i, 0)))
        if mode == "nn":
            assert b.shape[1] == k_dim
            in_specs.append(pl.BlockSpec((1, k_dim, tn), lambda i, j, r, sb=sb: (sb(r), 0, j)))
        else:
            assert b.shape[2] == k_dim
            in_specs.append(pl.BlockSpec((1, tn, k_dim), lambda i, j, r, sb=sb: (sb(r), j, 0)))
        args += [a, b]
        blk += _nbytes((tm, k_dim), a.dtype) + _nbytes((k_dim, tn), b.dtype)
    blk += _nbytes((tm, tn), out_dtype)
    n_pairs = len(pairs)

    def body(*refs):
        out_ref = refs[2 * n_pairs]
        acc = None
        for k in range(n_pairs):
            a = refs[2 * k][0].astype(MXU_DTYPE)
            b = refs[2 * k + 1][0].astype(MXU_DTYPE)
            d = lax.dot_general(a, b, dims, preferred_element_type=F32)
            acc = d if acc is None else acc + d

        if n_red == 1:
            out_ref[0] = acc.astype(out_dtype)
        else:
            acc_ref = refs[-1]
            r = pl.program_id(2)

            @pl.when(r == 0)
            def _():
                acc_ref[...] = acc

            @pl.when(r > 0)
            def _():
                acc_ref[...] += acc

            @pl.when(r == n_red - 1)
            def _():
                out_ref[0] = acc_ref[...].astype(out_dtype)

    return pl.pallas_call(
        body,
        name=name,
        grid=grid,
        in_specs=in_specs,
        out_specs=pl.BlockSpec((1, tm, tn), lambda i, j, r: (0, i, j)),
        out_shape=jax.ShapeDtypeStruct((1, n_tok, n_out), out_dtype),
        scratch_shapes=[pltpu.VMEM((tm, tn), F32)] if n_red > 1 else [],
        compiler_params=_params(blk, 3 * _nbytes((tm, tn), F32)),
    )(*_hbm(*args))


def _mm_tn(a, b, *, name, tt=1024):
    n_tok, k_dim = a.shape[1], a.shape[2]
    n_out = b.shape[2]
    tt = min(tt, n_tok)
    tk = k_dim if k_dim <= 1536 else 1024
    tn = n_out if n_out <= 1536 else 1024
    assert n_tok % tt == 0 and k_dim % tk == 0 and n_out % tn == 0
    n_so = max(a.shape[0], b.shape[0])
    sa = (lambda s: s) if a.shape[0] > 1 else (lambda s: 0)
    sb = (lambda s: s) if b.shape[0] > 1 else (lambda s: 0)
    grid = (n_so, k_dim // tk, n_out // tn, n_tok // tt)

    def body(a_ref, b_ref, out_ref):
        d = lax.dot_general(a_ref[0].astype(MXU_DTYPE), b_ref[0].astype(MXU_DTYPE), _TN, preferred_element_type=F32)
        t = pl.program_id(3)

        @pl.when(t == 0)
        def _():
            out_ref[0] = d

        @pl.when(t > 0)
        def _():
            out_ref[0] += d

    blk = _nbytes((tt, tk), a.dtype) + _nbytes((tt, tn), b.dtype) + _nbytes((tk, tn), F32)
    return pl.pallas_call(
        body,
        name=name,
        grid=grid,
        in_specs=[
            pl.BlockSpec((1, tt, tk), lambda s, ki, nj, t: (sa(s), t, ki)),
            pl.BlockSpec((1, tt, tn), lambda s, ki, nj, t: (sb(s), t, nj)),
        ],
        out_specs=pl.BlockSpec((1, tk, tn), lambda s, ki, nj, t: (s, ki, nj)),
        out_shape=jax.ShapeDtypeStruct((n_so, k_dim, n_out), F32),
        compiler_params=_params(blk, 2 * _nbytes((tk, tn), F32) + _nbytes((tt, tk), F32)),
    )(*_hbm(a, b))


@functools.partial(jax.custom_vjp, nondiff_argnums=(2,))
def _mdot(a, b, dims):
    return lax.dot_general(a.astype(MXU_DTYPE), b.astype(MXU_DTYPE), dims, preferred_element_type=F32)


def _mdot_fwd(a, b, dims):
    return _mdot(a, b, dims), (a, b)


def _mdot_bwd(dims, res, g):
    a, b = res
    (ca, cb), (ba, bb) = dims
    nb = len(ba)
    assert tuple(ba) == tuple(range(nb)) and tuple(bb) == tuple(range(nb)) and len(ca) == 1 and a.ndim == nb + 2
    batch = (tuple(range(nb)), tuple(range(nb)))
    ra, rb = nb, nb + 1
    a_free = (set(range(nb, nb + 2)) - set(ca)).pop()
    b_free = (set(range(nb, nb + 2)) - set(cb)).pop()
    if a_free < ca[0]:
        da = _mdot(g, b, (((rb,), (b_free,)), batch))
    else:
        da = _mdot(b, g, (((b_free,), (rb,)), batch))
    if b_free > cb[0]:
        db = _mdot(a, g, (((a_free,), (ra,)), batch))
    else:
        db = _mdot(g, a, (((ra,), (a_free,)), batch))
    return da.astype(a.dtype), db.astype(b.dtype)


_mdot.defvjp(_mdot_fwd, _mdot_bwd)


def _rms(x, gain):
    return x * lax.rsqrt(jnp.mean(x * x, axis=-1, keepdims=True) + EPS) * gain


def _pre_fn(coef, x_in, f, gate, gain, shift, scale):
    x_new = x_in if f is None else x_in + coef * gate * f
    return x_new, _rms(x_new, gain) * (1.0 + scale) + shift


def _row_spec(ts):
    return pl.BlockSpec((1, ts, D_MODEL), lambda b, j: (b, j, 0))


_BATCH_VEC = pl.BlockSpec((1, 1, D_MODEL), lambda b, j: (b, 0, 0))
_ONE_VEC = pl.BlockSpec((1, D_MODEL), lambda b, j: (0, 0))


def _pre(x_in, f, gate, gain, shift, scale, coef, *, name, ts=512):
    n_b, n_s, _ = x_in.shape
    ts = min(ts, n_s)
    has_res = f is not None

    def body(*refs):
        if has_res:
            x_ref, f_ref, gate_ref, gain_ref, sh_ref, sc_ref, xn_ref, a_ref = refs
            x_new, a = _pre_fn(coef, x_ref[0], f_ref[0], gate_ref[0], gain_ref[...], sh_ref[0], sc_ref[0])
            xn_ref[0] = x_new
        else:
            x_ref, gain_ref, sh_ref, sc_ref, a_ref = refs
            _, a = _pre_fn(coef, x_ref[0], None, None, gain_ref[...], sh_ref[0], sc_ref[0])
        a_ref[0] = a.astype(a_ref.dtype)

    row = _row_spec(ts)
    if has_res:
        args = (x_in, f, gate, gain, shift, scale)
        in_specs = [row, row, _BATCH_VEC, _ONE_VEC, _BATCH_VEC, _BATCH_VEC]
        out_specs = (row, row)
        out_shape = (jax.ShapeDtypeStruct(x_in.shape, F32), jax.ShapeDtypeStruct(x_in.shape, MXU_DTYPE))
    else:
        args = (x_in, gain, shift, scale)
        in_specs = [row, _ONE_VEC, _BATCH_VEC, _BATCH_VEC]
        out_specs = row
        out_shape = jax.ShapeDtypeStruct(x_in.shape, MXU_DTYPE)
    return pl.pallas_call(
        body, name=name, grid=(n_b, n_s // ts), in_specs=in_specs, out_specs=out_specs, out_shape=out_shape,
        compiler_params=_params(5 * _nbytes((ts, D_MODEL), F32), 4 * _nbytes((ts, D_MODEL), F32)),
    )(*_hbm(*args))


def _accumulate(ref, value, first):
    @pl.when(first)
    def _():
        ref[...] = value

    @pl.when(jnp.logical_not(first))
    def _():
        ref[...] += value


def _pre_bwd(x_in, f, gate, gain, shift, scale, coef, da, dx_up, *, name, ts=512):
    n_b, n_s, _ = x_in.shape
    ts = min(ts, n_s)
    has_res = f is not None
    has_up = dx_up is not None

    def body(*refs):
        refs = list(refs)
        x_ref = refs.pop(0)
        f_ref, gate_ref = (refs.pop(0), refs.pop(0)) if has_res else (None, None)
        gain_ref, sh_ref, sc_ref, da_ref = refs.pop(0), refs.pop(0), refs.pop(0), refs.pop(0)
        up_ref = refs.pop(0) if has_up else None
        dx_ref = refs.pop(0)
        df_ref, dgate_ref = (refs.pop(0), refs.pop(0)) if has_res else (None, None)
        dgain_ref, dsh_ref, dsc_ref = refs
        b, j = pl.program_id(0), pl.program_id(1)
        da_v = da_ref[0].astype(F32)
        up_v = up_ref[0] if has_up else jnp.zeros((ts, D_MODEL), F32)
        if has_res:
            fn = functools.partial(_pre_fn, coef)
            _, pull = jax.vjp(fn, x_ref[0], f_ref[0], gate_ref[0], gain_ref[...], sh_ref[0], sc_ref[0])
            dx, df, dgate, dgain, dsh, dsc = pull((up_v, da_v))
            df_ref[0] = df.astype(df_ref.dtype)
            _accumulate(dgate_ref, dgate[None], j == 0)
        else:
            fn = lambda x, g, sh, sc: _pre_fn(coef, x, None, None, g, sh, sc)
            _, pull = jax.vjp(fn, x_ref[0], gain_ref[...], sh_ref[0], sc_ref[0])
            dx, dgain, dsh, dsc = pull((up_v, da_v))
        dx_ref[0] = dx
        _accumulate(dgain_ref, dgain, jnp.logical_and(b == 0, j == 0))
        _accumulate(dsh_ref, dsh[None], j == 0)
        _accumulate(dsc_ref, dsc[None], j == 0)

    row = _row_spec(ts)
    args, in_specs = [x_in], [row]
    if has_res:
        args += [f, gate]
        in_specs += [row, _BATCH_VEC]
    args += [gain, shift, scale, da]
    in_specs += [_ONE_VEC, _BATCH_VEC, _BATCH_VEC, row]
    if has_up:
        args.append(dx_up)
        in_specs.append(row)
    vec = jax.ShapeDtypeStruct((n_b, 1, D_MODEL), F32)
    out_shape, out_specs = [jax.ShapeDtypeStruct(x_in.shape, F32)], [row]
    if has_res:
        out_shape += [jax.ShapeDtypeStruct(x_in.shape, MXU_DTYPE), vec]
        out_specs += [row, _BATCH_VEC]
    out_shape += [jax.ShapeDtypeStruct((1, D_MODEL), F32), vec, vec]
    out_specs += [_ONE_VEC, _BATCH_VEC, _BATCH_VEC]
    return pl.pallas_call(
        body, name=name, grid=(n_b, n_s // ts), in_specs=in_specs, out_specs=tuple(out_specs), out_shape=tuple(out_shape),
        compiler_params=_params(6 * _nbytes((ts, D_MODEL), F32), 8 * _nbytes((ts, D_MODEL), F32)),
    )(*_hbm(*args))


def _final_fn(x_in, f, gate, gain, target):
    x_new = x_in + 0.5 * gate * f
    err = jnp.square(_rms(x_new, gain) - target)
    return 0.5 * jnp.sum(jnp.mean(err, axis=-1))


def _final(x_in, f, gate, gain, target, *, name, ts=512):
    n_b, n_s, _ = x_in.shape
    ts = min(ts, n_s)

    def body(x_ref, f_ref, gate_ref, gain_ref, t_ref, loss_ref, dx_ref, df_ref, dgate_ref, dgain_ref):
        b, j = pl.program_id(0), pl.program_id(1)
        loss, (dx, df, dgate, dgain) = jax.value_and_grad(_final_fn, argnums=(0, 1, 2, 3))(
            x_ref[0], f_ref[0], gate_ref[0], gain_ref[...], t_ref[0])
        first = jnp.logical_and(b == 0, j == 0)
        _accumulate(loss_ref, jnp.reshape(loss, (1, 1)), first)
        dx_ref[0] = dx
        df_ref[0] = df.astype(df_ref.dtype)
        _accumulate(dgate_ref, dgate[None], j == 0)
        _accumulate(dgain_ref, dgain, first)

    row = _row_spec(ts)
    return pl.pallas_call(
        body, name=name, grid=(n_b, n_s // ts),
        in_specs=[row, row, _BATCH_VEC, _ONE_VEC, row],
        out_specs=(pl.BlockSpec((1, 1), lambda b, j: (0, 0)), row, row, _BATCH_VEC, _ONE_VEC),
        out_shape=(jax.ShapeDtypeStruct((1, 1), F32), jax.ShapeDtypeStruct(x_in.shape, F32),
                   jax.ShapeDtypeStruct(x_in.shape, MXU_DTYPE), jax.ShapeDtypeStruct((n_b, 1, D_MODEL), F32),
                   jax.ShapeDtypeStruct((1, D_MODEL), F32)),
        compiler_params=_params(5 * _nbytes((ts, D_MODEL), F32), 8 * _nbytes((ts, D_MODEL), F32)),
    )(*_hbm(x_in, f, gate, gain, target))


FFN_TOKENS = 1024


def _ffn_up(a, w1s, w3s, *, name, tm=FFN_TOKENS):
    n_tok = a.shape[0]
    tm = min(tm, n_tok)

    def body(a_ref, w1_ref, w3_ref, h1_ref, h3_ref, g_ref):
        av = a_ref[...].astype(MXU_DTYPE)
        h1 = lax.dot_general(av, w1_ref[0].astype(MXU_DTYPE), _NT, preferred_element_type=F32)
        h3 = lax.dot_general(av, w3_ref[0].astype(MXU_DTYPE), _NT, preferred_element_type=F32)
        h1_ref[0] = h1.astype(h1_ref.dtype)
        h3_ref[0] = h3.astype(h3_ref.dtype)
        g_ref[0] = (jax.nn.silu(h1) * h3).astype(g_ref.dtype)

    w_spec = pl.BlockSpec((1, FF_SHARD, D_MODEL), lambda s, i: (s, 0, 0))
    h_spec = pl.BlockSpec((1, tm, FF_SHARD), lambda s, i: (s, i, 0))
    h_shape = jax.ShapeDtypeStruct((N_SHARD, n_tok, FF_SHARD), MXU_DTYPE)
    blk = _nbytes((tm, D_MODEL), a.dtype) + 2 * _nbytes((D_MODEL, FF_SHARD), w1s.dtype) + 3 * _nbytes((tm, FF_SHARD), MXU_DTYPE)
    return pl.pallas_call(
        body, name=name, grid=(N_SHARD, n_tok // tm),
        in_specs=[pl.BlockSpec((tm, D_MODEL), lambda s, i: (i, 0)), w_spec, w_spec],
        out_specs=(h_spec, h_spec, h_spec), out_shape=(h_shape, h_shape, h_shape),
        compiler_params=_params(blk, 6 * _nbytes((tm, FF_SHARD), F32)),
    )(*_hbm(a, w1s, w3s))


def _ffn_down_bwd(df, w2s, h1, h3, *, name, tm=FFN_TOKENS):
    n_tok = df.shape[0]
    tm = min(tm, n_tok)

    def body(df_ref, w2_ref, h1_ref, h3_ref, dh1_ref, dh3_ref):
        dg = lax.dot_general(df_ref[...].astype(MXU_DTYPE), w2_ref[0].astype(MXU_DTYPE), _NT, preferred_element_type=F32)
        h1v = h1_ref[0].astype(F32)
        h3v = h3_ref[0].astype(F32)
        sig = jax.nn.sigmoid(h1v)
        dh3_ref[0] = (dg * (h1v * sig)).astype(dh3_ref.dtype)
        dh1_ref[0] = (dg * h3v * (sig * (1.0 + h1v * (1.0 - sig)))).astype(dh1_ref.dtype)

    h_spec = pl.BlockSpec((1, tm, FF_SHARD), lambda s, i: (s, i, 0))
    h_shape = jax.ShapeDtypeStruct((N_SHARD, n_tok, FF_SHARD), MXU_DTYPE)
    blk = _nbytes((tm, D_MODEL), df.dtype) + _nbytes((FF_SHARD, D_MODEL), w2s.dtype) + 4 * _nbytes((tm, FF_SHARD), MXU_DTYPE)
    return pl.pallas_call(
        body, name=name, grid=(N_SHARD, n_tok // tm),
        in_specs=[pl.BlockSpec((tm, D_MODEL), lambda s, i: (i, 0)),
                  pl.BlockSpec((1, FF_SHARD, D_MODEL), lambda s, i: (s, 0, 0)), h_spec, h_spec],
        out_specs=(h_spec, h_spec), out_shape=(h_shape, h_shape),
        compiler_params=_params(blk, 8 * _nbytes((tm, FF_SHARD), F32)),
    )(*_hbm(df, w2s, h1, h3))


def _ffn_fwd(a, w1s, w3s, w2s, tag):
    h1, h3, g = _ffn_up(a, w1s, w3s, name=f"{tag}_up")
    f = _mm_act([(g, w2s)], "nn", reduce_shards=True, tm=FFN_TOKENS, name=f"{tag}_down")[0]
    return f, (h1, h3, g)


def _ffn_bwd(a, w1s, w3s, w2s, saved, df, tag):
    h1, h3, g = saved
    dh1, dh3 = _ffn_down_bwd(df, w2s, h1, h3, name=f"{tag}_down_bwd")
    da = _mm_act([(dh1, w1s), (dh3, w3s)], "nn", reduce_shards=True, tm=FFN_TOKENS, name=f"{tag}_up_bwd")[0]
    a3 = a[None]
    dw1 = _mm_tn(dh1, a3, tt=FFN_TOKENS, name=f"{tag}_dw1")
    dw3 = _mm_tn(dh3, a3, tt=FFN_TOKENS, name=f"{tag}_dw3")
    dw2 = _mm_tn(g, df[None], tt=FFN_TOKENS, name=f"{tag}_dw2")
    return da, dw1, dw3, dw2


CONV_LANES = 256


def _shift_down(x, d):
    if d == 0:
        return x
    row = lax.broadcasted_iota(jnp.int32, x.shape, 0)
    return jnp.where(row >= d, pltpu.roll(x, d, 0), 0.0)


def _shift_up(x, d):
    if d == 0:
        return x
    n = x.shape[0]
    row = lax.broadcasted_iota(jnp.int32, x.shape, 0)
    return jnp.where(row < n - d, pltpu.roll(x, n - d, 0), 0.0)


def _conv_pre(x, w):
    acc = None
    for j in range(CONV_WIDTH):
        term = w[j:j + 1, :] * _shift_down(x, CONV_WIDTH - 1 - j)
        acc = term if acc is None else acc + term
    return acc


def _conv_fwd(x, w, *, name):
    n_b, n_s, n_c = x.shape
    spec = pl.BlockSpec((1, n_s, CONV_LANES), lambda b, cj: (b, 0, cj))

    def body(x_ref, w_ref, o_ref):
        o_ref[0] = jax.nn.silu(_conv_pre(x_ref[0], w_ref[...]))

    return pl.pallas_call(
        body, name=name, grid=(n_b, n_c // CONV_LANES),
        in_specs=[spec, pl.BlockSpec((CONV_WIDTH, CONV_LANES), lambda b, cj: (0, cj))],
        out_specs=spec, out_shape=jax.ShapeDtypeStruct(x.shape, F32),
        compiler_params=_params(2 * _nbytes((n_s, CONV_LANES), F32), 6 * _nbytes((n_s, CONV_LANES), F32)),
    )(*_hbm(x, w))


def _conv_bwd(x, w, dout, *, name):
    n_b, n_s, n_c = x.shape
    per_part = DN_WIDTH // CONV_LANES
    spec = pl.BlockSpec((1, n_s, CONV_LANES), lambda cj, b: (b, 0, cj))
    do_spec = pl.BlockSpec((1, 1, n_s, CONV_LANES), lambda cj, b: (cj // per_part, b, 0, cj % per_part))
    w_spec = pl.BlockSpec((CONV_WIDTH, CONV_LANES), lambda cj, b: (0, cj))

    def body(x_ref, w_ref, do_ref, dx_ref, dw_ref):
        xv, wv = x_ref[0], w_ref[...]
        pre = _conv_pre(xv, wv)
        sig = jax.nn.sigmoid(pre)
        dpre = do_ref[0, 0] * (sig * (1.0 + pre * (1.0 - sig)))
        dx = None
        first = pl.program_id(1) == 0
        for j in range(CONV_WIDTH):
            d = CONV_WIDTH - 1 - j
            ahead = _shift_up(dpre, d)
            term = wv[j:j + 1, :] * ahead
            dx = term if dx is None else dx + term
            dwj = jnp.sum(ahead * xv, axis=0, keepdims=True)
            _accumulate(dw_ref.at[j:j + 1, :], dwj, first)
        dx_ref[0] = dx.astype(dx_ref.dtype)

    return pl.pallas_call(
        body, name=name, grid=(n_c // CONV_LANES, n_b),
        in_specs=[spec, w_spec, do_spec], out_specs=(spec, w_spec),
        out_shape=(jax.ShapeDtypeStruct(x.shape, MXU_DTYPE), jax.ShapeDtypeStruct((CONV_WIDTH, n_c), F32)),
        compiler_params=_params(3 * _nbytes((n_s, CONV_LANES), F32), 8 * _nbytes((n_s, CONV_LANES), F32)),
    )(*_hbm(x, w, dout))


_BNT = (((2,), (2,)), ((0,), (0,)))
_BNN = (((2,), (1,)), ((0,), (0,)))
_BTN = (((1,), (1,)), ((0,), (0,)))
DN_PREP_CHUNKS = 8
DN_SCAN_HEADS = 16
DN_SCAN_CHUNKS = 4
N_DOUBLINGS = 5


def _fdot(a, b, dims):
    return lax.dot_general(a, b, dims, precision=lax.Precision.HIGHEST, preferred_element_type=F32)


def _hdot(a, b, dims):
    return lax.dot_general(a, b, dims, precision=lax.Precision.HIGH, preferred_element_type=F32)


def _solve_by_doubling(a, rhs_u, rhs_w):
    row = lax.broadcasted_iota(jnp.int32, (CHUNK, CHUNK), 0)
    col = lax.broadcasted_iota(jnp.int32, (CHUNK, CHUNK), 1)
    inv = jnp.where(row == col, 1.0, 0.0) - a
    power = a
    for _ in range(N_DOUBLINGS):
        power = _hdot(power, power, _BNN)
        inv = inv + _hdot(inv, power, _BNN)
    return _hdot(inv, rhs_u, _BNN), _hdot(inv, rhs_w, _BNN), inv


@jax.custom_vjp
def _solve_saved(a, rhs_u, rhs_w, inv, u, w):
    return u, w


def _solve_saved_fwd(a, rhs_u, rhs_w, inv, u, w):
    return (u, w), (inv, u, w)


def _solve_saved_bwd(res, cts):
    inv, u, w = res
    gu = _hdot(inv, cts[0], _BTN)
    gw = _hdot(inv, cts[1], _BTN)
    da = -(_hdot(gu, u, _BNT) + _hdot(gw, w, _BNT))
    return da, gu, gw, jnp.zeros_like(inv), jnp.zeros_like(u), jnp.zeros_like(w)


_solve_saved.defvjp(_solve_saved_fwd, _solve_saved_bwd)


def _dn_prep_fn(solve, qc, kc, vc, bl, lac, lar, a_log, dt_bias):
    q = qc * lax.rsqrt(jnp.sum(qc * qc, axis=-1, keepdims=True) + EPS) * (DN_HEAD_DIM ** -0.5)
    k = kc * lax.rsqrt(jnp.sum(kc * kc, axis=-1, keepdims=True) + EPS)
    beta = jax.nn.sigmoid(bl)
    neg_a = -jnp.exp(a_log)
    lgc = neg_a * jax.nn.softplus(lac + dt_bias)
    lgr = neg_a * jax.nn.softplus(lar + dt_bias)
    row = lax.broadcasted_iota(jnp.int32, (CHUNK, CHUNK), 0)
    col = lax.broadcasted_iota(jnp.int32, (CHUNK, CHUNK), 1)
    causal, strict = row >= col, row > col
    g_c = jnp.sum(jnp.where(causal, lgr, 0.0), axis=-1, keepdims=True)
    g_r = jnp.sum(jnp.where(row <= col, lgc, 0.0), axis=-2, keepdims=True)
    decay = jnp.exp(jnp.where(causal, g_c - g_r, -jnp.inf))
    kb = k * beta
    a = jnp.where(strict, _mdot(kb, k, _BNT) * decay, 0.0)
    u, w, extra = solve(a, vc * beta, kb * jnp.exp(g_c))
    attn = _mdot(q, k, _BNT) * decay
    g_last = jnp.sum(lgc, axis=-2, keepdims=True)
    return q * jnp.exp(g_c), k * jnp.exp(g_last - g_c), u, w, attn, g_last, extra


PAIR = 2
PAIR_LANES = PAIR * DN_HEAD_DIM


def _dn_prep_specs(n_cb):
    tok = n_cb * CHUNK
    wide = pl.BlockSpec((1, 1, tok, PAIR_LANES), lambda p, b, j: (b, p, j, 0))
    rowv = pl.BlockSpec((1, PAIR, n_cb, 1, CHUNK), lambda p, b, j: (b, p, j, 0, 0))
    one = pl.BlockSpec((1, PAIR, n_cb, 1, 1), lambda p, b, j: (b, p, j, 0, 0))
    head = pl.BlockSpec((PAIR, 1, 1), lambda p, b, j: (p, 0, 0))
    lanes = lambda part: pl.BlockSpec((1, tok, PAIR_LANES), lambda p, b, j: (b, j, part * (DN_HEADS // PAIR) + p))
    return wide, rowv, one, head, lanes


def _split_pair(x, n_cb):
    halves = [x[:, h * DN_HEAD_DIM:(h + 1) * DN_HEAD_DIM].reshape(n_cb, CHUNK, DN_HEAD_DIM) for h in range(PAIR)]
    return jnp.concatenate(halves, axis=0)


def _join_pair(chunks, tok):
    per_head = chunks.reshape(PAIR, tok, DN_HEAD_DIM)
    return jnp.concatenate([per_head[h] for h in range(PAIR)], axis=-1)


def _dn_prep_load(n_cb, q_ref, k_ref, v_ref, blr_ref, lar_ref, al_ref, dt_ref):
    rowf = lambda r: r[0].reshape(PAIR * n_cb, 1, CHUNK)
    return (_split_pair(q_ref[0], n_cb), _split_pair(k_ref[0], n_cb), _split_pair(v_ref[0], n_cb), rowf(blr_ref),
            rowf(lar_ref), al_ref[...], dt_ref[...])


def _dn_prep_pair_fn(n_cb, solve, qc, kc, vc, blr, lar, a_log, dt_bias):
    per_chunk = lambda t: jnp.broadcast_to(t[:, None], (PAIR, n_cb, 1, 1)).reshape(PAIR * n_cb, 1, 1)
    eye = lax.broadcasted_iota(jnp.int32, (CHUNK, CHUNK), 0) == lax.broadcasted_iota(jnp.int32, (CHUNK, CHUNK), 1)
    to_col = lambda r: jnp.sum(jnp.where(eye, r, 0.0), axis=-1, keepdims=True)
    return _dn_prep_fn(solve, qc, kc, vc, to_col(blr), to_col(lar), lar, per_chunk(a_log), per_chunk(dt_bias))


def _dn_prep(qkv, blr, lar, a_log, dt_bias, *, name):
    n_b, n_s, _ = qkv.shape
    n_cb = min(DN_PREP_CHUNKS, n_s // CHUNK)
    tok = n_cb * CHUNK
    wide, rowv, one, head, lanes = _dn_prep_specs(n_cb)

    def body(*refs):
        outs = _dn_prep_pair_fn(n_cb, _solve_by_doubling, *_dn_prep_load(n_cb, *refs[:7]))
        for ref, val in zip(refs[7:12], outs[:5]):
            ref[0, 0] = _join_pair(val, tok)
        refs[12][0] = outs[5].reshape(PAIR, n_cb, 1, 1)
        refs[13][0, 0] = _join_pair(outs[6], tok)

    big = jax.ShapeDtypeStruct((n_b, DN_HEADS // PAIR, n_s, PAIR_LANES), F32)
    return pl.pallas_call(
        body, name=name, grid=(DN_HEADS // PAIR, n_b, n_s // tok),
        in_specs=[lanes(0), lanes(1), lanes(2), rowv, rowv, head, head],
        out_specs=(wide, wide, wide, wide, wide, one, wide),
        out_shape=(big, big, big, big, big, jax.ShapeDtypeStruct((n_b, DN_HEADS, n_s // CHUNK, 1, 1), F32), big),
        compiler_params=_params(11 * PAIR * _nbytes((tok, LANES), F32), 48 * PAIR * _nbytes((tok, LANES), F32)),
    )(*_hbm(qkv, qkv, qkv, blr, lar, a_log, dt_bias))


def _dn_prep_bwd(qkv, blr, lar, a_log, dt_bias, inv, u, w, cts, *, name):
    n_b, n_s, _ = qkv.shape
    n_cb = min(DN_PREP_CHUNKS, n_s // CHUNK)
    tok = n_cb * CHUNK
    wide, rowv, one, head, lanes = _dn_prep_specs(n_cb)

    def body(*refs):
        prim = _dn_prep_load(n_cb, *refs[:7])
        chunks = lambda r: _split_pair(r[0, 0], n_cb)
        inv_v, u_v, w_v = chunks(refs[7]), chunks(refs[8]), chunks(refs[9])
        ct = tuple(chunks(r) for r in refs[10:15]) + (refs[15][0].reshape(PAIR * n_cb, 1, 1),)

        def fn(*args):
            solve = lambda a, ru, rw: _solve_saved(a, ru, rw, inv_v, u_v, w_v) + (None,)
            return _dn_prep_pair_fn(n_cb, solve, *args)[:6]

        _, pull = jax.vjp(fn, *prim)
        dq, dk, dv, dblr, dlar, dal, ddt = pull(ct)
        outs = refs[16:]
        for part, val in enumerate((dq, dk, dv)):
            outs[0][part, 0] = _join_pair(val, tok)
        outs[1][0] = dblr.reshape(PAIR, n_cb, 1, CHUNK)
        outs[2][0] = dlar.reshape(PAIR, n_cb, 1, CHUNK)
        first = jnp.logical_and(pl.program_id(1) == 0, pl.program_id(2) == 0)
        _accumulate(outs[3], dal, first)
        _accumulate(outs[4], ddt, first)

    dqkv_spec = pl.BlockSpec((3, 1, tok, PAIR_LANES), lambda p, b, j: (0, b, j, p))
    return pl.pallas_call(
        body, name=name, grid=(DN_HEADS // PAIR, n_b, n_s // tok),
        in_specs=[lanes(0), lanes(1), lanes(2), rowv, rowv, head, head, wide, wide, wide, wide, wide, wide, wide, wide, one],
        out_specs=(dqkv_spec, rowv, rowv, head, head),
        out_shape=(jax.ShapeDtypeStruct((3, n_b, n_s, DN_WIDTH), F32), jax.ShapeDtypeStruct(blr.shape, F32),
                   jax.ShapeDtypeStruct(lar.shape, F32), jax.ShapeDtypeStruct(a_log.shape, F32),
                   jax.ShapeDtypeStruct(dt_bias.shape, F32)),
        compiler_params=_params(21 * PAIR * _nbytes((tok, LANES), F32), 64 * PAIR * _nbytes((tok, LANES), F32)),
    )(*_hbm(qkv, qkv, qkv, blr, lar, a_log, dt_bias, inv, u, w, *cts))


def _dn_step(state, q, k, u, w, a, gl):
    v_new = u - _mdot(w, state, _BNN)
    o = _mdot(q, state, _BNN) + _mdot(a, v_new, _BNN)
    return state * jnp.exp(gl) + _mdot(k, v_new, _BTN), o


def _dn_scan_specs(n_cb, n_blocks, reverse, n_seq):
    tok = n_cb * CHUNK
    jj = (lambda j: n_blocks - 1 - j) if reverse else (lambda j: j)
    wide = pl.BlockSpec((1, n_seq // PAIR, tok, PAIR_LANES), lambda b, j: (b, 0, jj(j), 0))
    one = pl.BlockSpec((1, n_seq, n_cb, 1, 1), lambda b, j: (b, 0, jj(j), 0, 0))
    st = pl.BlockSpec((1, n_seq, n_cb, DN_HEAD_DIM, DN_HEAD_DIM), lambda b, j: (b, 0, jj(j), 0, 0))
    return wide, one, st


def _scan_fold(n_b):
    fold = max(1, DN_SCAN_HEADS // DN_HEADS)
    return fold if n_b % fold == 0 else 1


def _fold_rows(arrays, fold):
    return [a.reshape((a.shape[0] // fold, fold * a.shape[1]) + a.shape[2:]) for a in arrays]


def _to_scan_order(per_head, fold):
    n_b, rest = per_head.shape[0], per_head.shape[2:]
    t = per_head.reshape((n_b // fold, fold, DN_HEADS // PAIR, PAIR) + rest)
    t = jnp.moveaxis(t, 3, 1)
    return t.reshape((n_b // fold, fold * DN_HEADS) + rest)


def _from_scan_order(t, fold):
    n_bf, rest = t.shape[0], t.shape[2:]
    t = t.reshape((n_bf, PAIR, fold, DN_HEADS // PAIR) + rest)
    t = jnp.moveaxis(t, 1, 3)
    return t.reshape((n_bf * fold, DN_HEADS) + rest)


def _unpack_seqs(x):
    return jnp.concatenate([x[:, :, :DN_HEAD_DIM], x[:, :, DN_HEAD_DIM:]], axis=0)


def _pack_seqs(x):
    n_p = x.shape[0] // PAIR
    return jnp.concatenate([x[:n_p], x[n_p:]], axis=-1)


def _dn_scan(qd, kd, u, w, attn, g_last, *, name):
    shape = qd.shape
    fold = _scan_fold(shape[0])
    qd, kd, u, w, attn = _fold_rows([qd, kd, u, w, attn], fold)
    g_last = _to_scan_order(g_last, fold)
    n_b, n_pk, n_s, _ = qd.shape
    n_seq = PAIR * n_pk
    n_cb = min(DN_SCAN_CHUNKS, n_s // CHUNK)
    n_blocks = n_s // (n_cb * CHUNK)
    wide, one, st = _dn_scan_specs(n_cb, n_blocks, False, n_seq)

    def body(qd_ref, kd_ref, u_ref, w_ref, a_ref, gl_ref, o_ref, st_ref, state_ref):
        @pl.when(pl.program_id(1) == 0)
        def _():
            state_ref[...] = jnp.zeros(state_ref.shape, F32)

        def step(n, state):
            rows = pl.ds(pl.multiple_of(n * CHUNK, CHUNK), CHUNK)
            seqs = lambda r: _unpack_seqs(r[0, :, rows, :])
            st_ref[0, :, n] = state
            state, o = _dn_step(state, seqs(qd_ref), seqs(kd_ref), seqs(u_ref), seqs(w_ref), seqs(a_ref), gl_ref[0, :, n])
            o_ref[0, :, rows, :] = _pack_seqs(o)
            return state

        state_ref[...] = lax.fori_loop(0, n_cb, step, state_ref[...], unroll=True)

    o, states = pl.pallas_call(
        body, name=name, grid=(n_b, n_blocks),
        in_specs=[wide, wide, wide, wide, wide, one], out_specs=(wide, st),
        out_shape=(jax.ShapeDtypeStruct(qd.shape, F32),
                   jax.ShapeDtypeStruct((n_b, n_seq, n_s // CHUNK, DN_HEAD_DIM, DN_HEAD_DIM), F32)),
        scratch_shapes=[pltpu.VMEM((n_seq, DN_HEAD_DIM, DN_HEAD_DIM), F32)],
        compiler_params=_params(6 * _nbytes((n_pk, n_cb * CHUNK, PAIR_LANES), F32)
                                + _nbytes((n_seq, n_cb * CHUNK, LANES), F32), 8 << 20),
    )(*_hbm(qd, kd, u, w, attn, g_last))
    return o.reshape(shape), states


def _dn_scan_bwd(qd, kd, u, w, attn, g_last, states, do, *, name):
    shape = qd.shape
    fold = _scan_fold(shape[0])
    qd, kd, u, w, attn, do = _fold_rows([qd, kd, u, w, attn, do], fold)
    g_last = _to_scan_order(g_last, fold)
    n_b, n_pk, n_s, _ = qd.shape
    n_seq = PAIR * n_pk
    n_cb = min(DN_SCAN_CHUNKS, n_s // CHUNK)
    n_blocks = n_s // (n_cb * CHUNK)
    wide, one, st = _dn_scan_specs(n_cb, n_blocks, True, n_seq)

    def body(qd_ref, kd_ref, u_ref, w_ref, a_ref, gl_ref, st_ref, do_ref,
             dq_ref, dk_ref, du_ref, dw_ref, da_ref, dgl_ref, dstate_ref):
        @pl.when(pl.program_id(1) == 0)
        def _():
            dstate_ref[...] = jnp.zeros(dstate_ref.shape, F32)

        def step(i, dstate):
            n = n_cb - 1 - i
            rows = pl.ds(pl.multiple_of(n * CHUNK, CHUNK), CHUNK)
            seqs = lambda r: _unpack_seqs(r[0, :, rows, :])
            _, pull = jax.vjp(_dn_step, st_ref[0, :, n], seqs(qd_ref), seqs(kd_ref), seqs(u_ref), seqs(w_ref),
                              seqs(a_ref), gl_ref[0, :, n])
            dstate, dq, dk, du, dw, da, dgl = pull((dstate, seqs(do_ref)))
            for ref, val in zip((dq_ref, dk_ref, du_ref, dw_ref, da_ref), (dq, dk, du, dw, da)):
                ref[0, :, rows, :] = _pack_seqs(val)
            dgl_ref[0, :, n] = dgl
            return dstate

        dstate_ref[...] = lax.fori_loop(0, n_cb, step, dstate_ref[...], unroll=True)

    big = jax.ShapeDtypeStruct(qd.shape, F32)
    outs = pl.pallas_call(
        body, name=name, grid=(n_b, n_blocks),
        in_specs=[wide, wide, wide, wide, wide, one, st, wide],
        out_specs=(wide, wide, wide, wide, wide, one),
        out_shape=(big, big, big, big, big, jax.ShapeDtypeStruct(g_last.shape, F32)),
        scratch_shapes=[pltpu.VMEM((n_seq, DN_HEAD_DIM, DN_HEAD_DIM), F32)],
        compiler_params=_params(11 * _nbytes((n_pk, n_cb * CHUNK, PAIR_LANES), F32)
                                + _nbytes((n_seq, n_cb * CHUNK, LANES), F32), 8 << 20),
    )(*_hbm(qd, kd, u, w, attn, g_last, states, do))
    return tuple(o.reshape(shape) for o in outs[:5]) + (_from_scan_order(outs[5], fold),)


def _dn_post_fn(o, z, gain):
    return o * lax.rsqrt(jnp.mean(o * o, axis=-1, keepdims=True) + EPS) * gain * jax.nn.silu(z)


_HEAD_ROWS = lambda n_s: pl.BlockSpec((1, 1, n_s, PAIR_LANES), lambda b, p: (b, p, 0, 0))
_PAIR_LANES = lambda n_s: pl.BlockSpec((1, n_s, PAIR_LANES), lambda b, p: (b, 0, p))
_HEAD_GAIN = pl.BlockSpec((1, DN_HEAD_DIM), lambda b, p: (0, 0))


def _pair_heads(x):
    return jnp.stack([x[:, h * DN_HEAD_DIM:(h + 1) * DN_HEAD_DIM] for h in range(PAIR)])


def _pair_lanes(x):
    return jnp.concatenate([x[h] for h in range(PAIR)], axis=-1)


def _dn_post(o, z, gain, *, name):
    n_b, _, n_s, _ = o.shape

    def body(o_ref, z_ref, g_ref, out_ref):
        out = _dn_post_fn(_pair_heads(o_ref[0, 0]), _pair_heads(z_ref[0]), g_ref[...])
        out_ref[0] = _pair_lanes(out).astype(out_ref.dtype)

    lanes = _PAIR_LANES(n_s)
    return pl.pallas_call(
        body, name=name, grid=(n_b, DN_HEADS // PAIR), in_specs=[_HEAD_ROWS(n_s), lanes, _HEAD_GAIN], out_specs=lanes,
        out_shape=jax.ShapeDtypeStruct(z.shape, MXU_DTYPE),
        compiler_params=_params(3 * PAIR * _nbytes((n_s, LANES), F32), 6 * PAIR * _nbytes((n_s, LANES), F32)),
    )(*_hbm(o, z, gain))


def _dn_post_bwd(o, z, gain, dout, *, name):
    n_b, _, n_s, _ = o.shape

    def body(o_ref, z_ref, g_ref, dout_ref, do_ref, dz_ref, dg_ref):
        _, pull = jax.vjp(_dn_post_fn, _pair_heads(o_ref[0, 0]), _pair_heads(z_ref[0]), g_ref[...])
        do, dz, dg = pull(_pair_heads(dout_ref[0].astype(F32)))
        do_ref[0, 0] = _pair_lanes(do)
        dz_ref[0] = _pair_lanes(dz).astype(dz_ref.dtype)
        _accumulate(dg_ref, dg, jnp.logical_and(pl.program_id(0) == 0, pl.program_id(1) == 0))

    rows, lanes = _HEAD_ROWS(n_s), _PAIR_LANES(n_s)
    return pl.pallas_call(
        body, name=name, grid=(n_b, DN_HEADS // PAIR), in_specs=[rows, lanes, _HEAD_GAIN, lanes],
        out_specs=(rows, lanes, _HEAD_GAIN),
        out_shape=(jax.ShapeDtypeStruct(o.shape, F32), jax.ShapeDtypeStruct(z.shape, MXU_DTYPE),
                   jax.ShapeDtypeStruct((1, DN_HEAD_DIM), F32)),
        compiler_params=_params(5 * PAIR * _nbytes((n_s, LANES), F32), 10 * PAIR * _nbytes((n_s, LANES), F32)),
    )(*_hbm(o, z, gain, dout))


TILE_ROWS = SUBLANES


def _s5_prep_fn(lam_re, lam_im, log_step, bt_re, bt_im, c_im):
    lr = jnp.minimum(lam_re, -1e-4)
    step = jnp.exp(log_step)
    mag = jnp.exp(lr * step)
    ang = lam_im * step
    lb_re = mag * jnp.cos(ang)
    lb_im = mag * jnp.sin(ang)
    den = lr * lr + lam_im * lam_im
    coef_re = ((lb_re - 1.0) * lr + lb_im * lam_im) / den
    coef_im = (lb_im * lr - (lb_re - 1.0) * lam_im) / den
    return (lb_re, lb_im, coef_re * bt_re - coef_im * bt_im, coef_re * bt_im + coef_im * bt_re, -c_im)


def _s5_prep(lam_re, lam_im, log_step, bt_re, bt_im, c_im, *, name):
    def body(*refs):
        outs = _s5_prep_fn(*(r[...] for r in refs[:6]))
        for ref, val in zip(refs[6:], outs):
            ref[...] = val

    vec = jax.ShapeDtypeStruct(lam_re.shape, F32)
    mat = jax.ShapeDtypeStruct(bt_re.shape, F32)
    return pl.pallas_call(body, name=name, out_shape=(vec, vec, mat, mat, mat))(lam_re, lam_im, log_step, bt_re, bt_im, c_im)


def _s5_prep_bwd(lam_re, lam_im, log_step, bt_re, bt_im, c_im, cts, *, name):
    def body(*refs):
        _, pull = jax.vjp(_s5_prep_fn, *(r[...] for r in refs[:6]))
        grads = pull(tuple(r[...] for r in refs[6:11]))
        for ref, val in zip(refs[11:], grads):
            ref[...] = val

    shapes = tuple(jax.ShapeDtypeStruct(a.shape, F32) for a in (lam_re, lam_im, log_step, bt_re, bt_im, c_im))
    return pl.pallas_call(body, name=name, out_shape=shapes)(lam_re, lam_im, log_step, bt_re, bt_im, c_im, *cts)


def _cmul(ar, ai, br, bi):
    return ar * br - ai * bi, ar * bi + ai * br


def _s5_powers(lr, li):
    pows = [(lr, li)]
    for _ in range(TILE_ROWS - 1):
        pows.append(_cmul(pows[-1][0], pows[-1][1], lr, li))
    return pows


def _s5_carry_table(pows, n_lanes, reverse):
    row = lax.broadcasted_iota(jnp.int32, (TILE_ROWS, n_lanes), 0)
    t_re = jnp.zeros((TILE_ROWS, n_lanes), F32)
    t_im = jnp.zeros((TILE_ROWS, n_lanes), F32)
    for r in range(TILE_ROWS):
        p_re, p_im = pows[TILE_ROWS - 1 - r] if reverse else pows[r]
        t_re = jnp.where(row == r, p_re, t_re)
        t_im = jnp.where(row == r, p_im, t_im)
    return t_re, t_im


def _s5_step_tables(pows, n_lanes, reverse):
    row = lax.broadcasted_iota(jnp.int32, (TILE_ROWS, n_lanes), 0)
    tables, d = [], 1
    while d < TILE_ROWS:
        inside = (row < TILE_ROWS - d) if reverse else (row >= d)
        tables.append((d, jnp.where(inside, pows[d - 1][0], 0.0), jnp.where(inside, pows[d - 1][1], 0.0)))
        d *= 2
    return tables


def _s5_tile(y_re, y_im, tables, reverse):
    for d, p_re, p_im in tables:
        shift = TILE_ROWS - d if reverse else d
        m_re, m_im = _cmul(p_re, p_im, pltpu.roll(y_re, shift, 0), pltpu.roll(y_im, shift, 0))
        y_re, y_im = y_re + m_re, y_im + m_im
    return y_re, y_im


S5_BLOCKS = N_SHARD
S5_BLOCK_CH = S5_WIDTH // S5_BLOCKS
S5_BLOCK_LANES = S5_LANES // S5_BLOCKS


def _scan_rows(i):
    return pl.ds(pl.multiple_of(i * TILE_ROWS, TILE_ROWS), TILE_ROWS)


def _s5_mix_specs(n_s, order):
    jb = lambda *g: order(*g)[0]
    bb = lambda *g: order(*g)[1]
    act = pl.BlockSpec((1, 1, n_s, S5_BLOCK_CH), lambda *g: (bb(*g), 0, 0, jb(*g)))
    state = pl.BlockSpec((1, 1, n_s, S5_BLOCK_LANES), lambda *g: (jb(*g), bb(*g), 0, 0))
    lam = pl.BlockSpec((1, S5_BLOCK_LANES), lambda *g: (0, jb(*g)))
    w_in = pl.BlockSpec((1, S5_BLOCK_CH, S5_BLOCK_LANES), lambda *g: (jb(*g), 0, 0))
    w_out = pl.BlockSpec((1, S5_BLOCK_LANES, S5_BLOCK_CH), lambda *g: (jb(*g), 0, 0))
    return act, state, lam, w_in, w_out


def _s5_mix(u, wb_re, wb_im, lb_re, lb_im, wc_re, wc_im, *, name):
    n_b, n_s, _ = u.shape
    n_blk = S5_BLOCKS
    lanes = lambda t: t[:, None]
    n_tiles = n_s // TILE_ROWS
    L = S5_BLOCK_LANES

    def body(u_ref, wbr_ref, wbi_ref, lr_ref, li_ref, wcr_ref, wci_ref, y_ref, xr_ref, xi_ref):
        uv = u_ref[0, 0].astype(MXU_DTYPE)
        xr_ref[0, 0] = lax.dot_general(uv, wbr_ref[0].astype(MXU_DTYPE), _NN, preferred_element_type=F32)
        xi_ref[0, 0] = lax.dot_general(uv, wbi_ref[0].astype(MXU_DTYPE), _NN, preferred_element_type=F32)
        pows = _s5_powers(lr_ref[...], li_ref[...])
        t_re, t_im = _s5_carry_table(pows, L, False)
        steps = _s5_step_tables(pows, L, False)

        def step(i, carry):
            rows = _scan_rows(i)
            y_re, y_im = _s5_tile(xr_ref[0, 0, rows, :], xi_ref[0, 0, rows, :], steps, False)
            c_re, c_im = _cmul(t_re, t_im, carry[0], carry[1])
            y_re, y_im = y_re + c_re, y_im + c_im
            xr_ref[0, 0, rows, :] = y_re
            xi_ref[0, 0, rows, :] = y_im
            return y_re[TILE_ROWS - 1:, :], y_im[TILE_ROWS - 1:, :]

        zero = jnp.zeros((1, L), F32)
        lax.fori_loop(0, n_tiles, step, (zero, zero), unroll=2)
        y_ref[0, 0] = (
            lax.dot_general(xr_ref[0, 0].astype(MXU_DTYPE), wcr_ref[0].astype(MXU_DTYPE), _NN, preferred_element_type=F32)
            + lax.dot_general(xi_ref[0, 0].astype(MXU_DTYPE), wci_ref[0].astype(MXU_DTYPE), _NN, preferred_element_type=F32))

    act, state, lam, w_in, w_out = _s5_mix_specs(n_s, lambda b, j: (j, b))
    x_shape = jax.ShapeDtypeStruct((n_blk, n_b, n_s, L), F32)
    return pl.pallas_call(
        body, name=name, grid=(n_b, n_blk),
        in_specs=[act, w_in, w_in, lam, lam, w_out, w_out], out_specs=(act, state, state),
        out_shape=(jax.ShapeDtypeStruct((n_b, 1, n_s, S5_WIDTH), F32), x_shape, x_shape),
        compiler_params=_params(2 * _nbytes((n_s, L), F32) + 2 * _nbytes((n_s, S5_BLOCK_CH), F32), 3 * _nbytes((n_s, L), F32)),
    )(*_hbm(lanes(u), wb_re, wb_im, lb_re, lb_im, wc_re, wc_im))


def _s5_mix_bwd(dy, du_skip, u, x_re, x_im, wb_re, wb_im, lb_re, lb_im, wc_re, wc_im, *, name):
    n_b, n_s, _ = u.shape
    n_blk = S5_BLOCKS
    lanes = lambda t: t[:, None]
    n_tiles = n_s // TILE_ROWS
    L = S5_BLOCK_LANES

    def body(dy_ref, ds_ref, u_ref, xr_ref, xi_ref, wbr_ref, wbi_ref, lr_ref, li_ref, wcr_ref, wci_ref,
             du_ref, dwbr_ref, dwbi_ref, dlr_ref, dli_ref, dwcr_ref, dwci_ref, ar_ref, ai_ref):
        dyv = dy_ref[0, 0].astype(MXU_DTYPE)
        ar_ref[...] = lax.dot_general(dyv, wcr_ref[0].astype(MXU_DTYPE), _NT, preferred_element_type=F32)
        ai_ref[...] = lax.dot_general(dyv, wci_ref[0].astype(MXU_DTYPE), _NT, preferred_element_type=F32)
        pows = _s5_powers(lr_ref[...], -li_ref[...])
        t_re, t_im = _s5_carry_table(pows, L, True)
        steps = _s5_step_tables(pows, L, True)
        row = lax.broadcasted_iota(jnp.int32, (TILE_ROWS, L), 0)

        def step(k, carry):
            c_re, c_im, s_re, s_im = carry
            i = n_tiles - 1 - k
            rows = _scan_rows(i)
            a_re, a_im = _s5_tile(ar_ref[rows, :], ai_ref[rows, :], steps, True)
            m_re, m_im = _cmul(t_re, t_im, c_re, c_im)
            a_re, a_im = a_re + m_re, a_im + m_im
            ar_ref[rows, :] = a_re
            ai_ref[rows, :] = a_im
            prev = _scan_rows(jnp.maximum(i - 1, 0))
            keep = jnp.where(i > 0, 1.0, 0.0)
            last_re = xr_ref[0, 0, prev, :][TILE_ROWS - 1:, :] * keep
            last_im = xi_ref[0, 0, prev, :][TILE_ROWS - 1:, :] * keep
            xp_re = jnp.where(row == 0, last_re, pltpu.roll(xr_ref[0, 0, rows, :], 1, 0))
            xp_im = jnp.where(row == 0, last_im, pltpu.roll(xi_ref[0, 0, rows, :], 1, 0))
            s_re = s_re + a_re * xp_re + a_im * xp_im
            s_im = s_im + a_im * xp_re - a_re * xp_im
            return a_re[:1, :], a_im[:1, :], s_re, s_im

        zero = jnp.zeros((1, L), F32)
        zt = jnp.zeros((TILE_ROWS, L), F32)
        _, _, s_re, s_im = lax.fori_loop(0, n_tiles, step, (zero, zero, zt, zt), unroll=2)
        first = pl.program_id(1) == 0
        _accumulate(dlr_ref, jnp.sum(s_re, axis=0, keepdims=True), first)
        _accumulate(dli_ref, jnp.sum(s_im, axis=0, keepdims=True), first)
        a_re, a_im = ar_ref[...].astype(MXU_DTYPE), ai_ref[...].astype(MXU_DTYPE)
        du = (lax.dot_general(a_re, wbr_ref[0].astype(MXU_DTYPE), _NT, preferred_element_type=F32)
              + lax.dot_general(a_im, wbi_ref[0].astype(MXU_DTYPE), _NT, preferred_element_type=F32))
        du_ref[0, 0] = (du + ds_ref[0, 0]).astype(du_ref.dtype)
        uv = u_ref[0, 0].astype(MXU_DTYPE)
        _accumulate(dwbr_ref, lax.dot_general(uv, a_re, _TN, preferred_element_type=F32)[None], first)
        _accumulate(dwbi_ref, lax.dot_general(uv, a_im, _TN, preferred_element_type=F32)[None], first)
        _accumulate(dwcr_ref, lax.dot_general(xr_ref[0, 0].astype(MXU_DTYPE), dyv, _TN, preferred_element_type=F32)[None], first)
        _accumulate(dwci_ref, lax.dot_general(xi_ref[0, 0].astype(MXU_DTYPE), dyv, _TN, preferred_element_type=F32)[None], first)

    act, state, lam, w_in, w_out = _s5_mix_specs(n_s, lambda j, b: (j, b))
    lam_shape = jax.ShapeDtypeStruct((1, S5_LANES), F32)
    return pl.pallas_call(
        body, name=name, grid=(n_blk, n_b),
        in_specs=[act, act, act, state, state, w_in, w_in, lam, lam, w_out, w_out],
        out_specs=(act, w_in, w_in, lam, lam, w_out, w_out),
        out_shape=(jax.ShapeDtypeStruct((n_b, 1, n_s, S5_WIDTH), MXU_DTYPE), jax.ShapeDtypeStruct(wb_re.shape, F32),
                   jax.ShapeDtypeStruct(wb_im.shape, F32), lam_shape, lam_shape,
                   jax.ShapeDtypeStruct(wc_re.shape, F32), jax.ShapeDtypeStruct(wc_im.shape, F32)),
        scratch_shapes=[pltpu.VMEM((n_s, L), F32), pltpu.VMEM((n_s, L), F32)],
        compiler_params=_params(2 * _nbytes((n_s, L), F32) + 4 * _nbytes((n_s, S5_BLOCK_CH), F32), 5 * _nbytes((n_s, L), F32)),
    )(*_hbm(lanes(dy), lanes(du_skip), lanes(u), x_re, x_im, wb_re, wb_im, lb_re, lb_im, wc_re, wc_im))


def _s5_out_fn(ymm, u, d_skip, w_glu, b_glu):
    y = jax.nn.gelu(ymm + d_skip * u)
    return y * jax.nn.sigmoid(_mdot(y, w_glu, _NN) + b_glu)


def _s5_out_specs(tm):
    rows = pl.BlockSpec((tm, S5_WIDTH), lambda i: (i, 0))
    vec = pl.BlockSpec((1, S5_WIDTH), lambda i: (0, 0))
    mat = pl.BlockSpec((S5_WIDTH, S5_WIDTH), lambda i: (0, 0))
    return rows, vec, mat


def _s5_out(ymm, u, d_skip, w_glu, b_glu, *, name, tm=512):
    n_tok = ymm.shape[0]
    tm = min(tm, n_tok)
    rows, vec, mat = _s5_out_specs(tm)

    def body(y_ref, u_ref, d_ref, w_ref, b_ref, o_ref):
        o_ref[...] = _s5_out_fn(y_ref[...], u_ref[...], d_ref[...], w_ref[...], b_ref[...]).astype(o_ref.dtype)

    return pl.pallas_call(
        body, name=name, grid=(n_tok // tm,), in_specs=[rows, rows, vec, mat, vec], out_specs=rows,
        out_shape=jax.ShapeDtypeStruct((n_tok, S5_WIDTH), MXU_DTYPE),
        compiler_params=_params(4 * _nbytes((tm, S5_WIDTH), F32), 8 * _nbytes((tm, S5_WIDTH), F32)),
    )(*_hbm(ymm, u, d_skip, w_glu, b_glu))


def _s5_out_bwd(ymm, u, d_skip, w_glu, b_glu, dout, *, name, tm=512):
    n_tok = ymm.shape[0]
    tm = min(tm, n_tok)
    rows, vec, mat = _s5_out_specs(tm)

    def body(y_ref, u_ref, d_ref, w_ref, b_ref, do_ref, dy_ref, du_ref, dd_ref, dw_ref, db_ref):
        _, pull = jax.vjp(_s5_out_fn, y_ref[...], u_ref[...], d_ref[...], w_ref[...].astype(F32), b_ref[...])
        dy, du, dd, dw, db = pull(do_ref[...])
        dy_ref[...] = dy.astype(dy_ref.dtype)
        du_ref[...] = du
        first = pl.program_id(0) == 0
        _accumulate(dd_ref, dd, first)
        _accumulate(dw_ref, dw, first)
        _accumulate(db_ref, db, first)

    return pl.pallas_call(
        body, name=name, grid=(n_tok // tm,), in_specs=[rows, rows, vec, mat, vec, rows],
        out_specs=(rows, rows, vec, mat, vec),
        out_shape=(jax.ShapeDtypeStruct(ymm.shape, MXU_DTYPE), jax.ShapeDtypeStruct(ymm.shape, F32),
                   jax.ShapeDtypeStruct((1, S5_WIDTH), F32), jax.ShapeDtypeStruct((S5_WIDTH, S5_WIDTH), F32),
                   jax.ShapeDtypeStruct((1, S5_WIDTH), F32)),
        compiler_params=_params(6 * _nbytes((tm, S5_WIDTH), F32), 12 * _nbytes((tm, S5_WIDTH), F32)),
    )(*_hbm(ymm, u, d_skip, w_glu, b_glu, dout))


def _merge_fn(ga, gb, ya, yb):
    return jax.nn.sigmoid(ga) * ya + jax.nn.sigmoid(gb) * yb


def _merge(gab, ya, yb, *, name, tm=512):
    n_tok = ya.shape[0]
    tm = min(tm, n_tok)
    rows = pl.BlockSpec((tm, D_MODEL), lambda i: (i, 0))

    def body(ga_ref, gb_ref, ya_ref, yb_ref, o_ref):
        o_ref[...] = _merge_fn(ga_ref[...], gb_ref[...], ya_ref[...], yb_ref[...]).astype(o_ref.dtype)

    return pl.pallas_call(
        body, name=name, grid=(n_tok // tm,),
        in_specs=[rows, pl.BlockSpec((tm, D_MODEL), lambda i: (i, 1)), rows, rows], out_specs=rows,
        out_shape=jax.ShapeDtypeStruct(ya.shape, MXU_DTYPE),
        compiler_params=_params(5 * _nbytes((tm, D_MODEL), F32), 4 * _nbytes((tm, D_MODEL), F32)),
    )(*_hbm(gab, gab, ya, yb))


def _merge_bwd(gab, ya, yb, dout, *, name, tm=512):
    n_tok = ya.shape[0]
    tm = min(tm, n_tok)
    rows = pl.BlockSpec((tm, D_MODEL), lambda i: (i, 0))

    def body(ga_ref, gb_ref, ya_ref, yb_ref, do_ref, *out_refs):
        _, pull = jax.vjp(_merge_fn, ga_ref[...], gb_ref[...], ya_ref[...], yb_ref[...])
        for ref, val in zip(out_refs, pull(do_ref[...])):
            ref[...] = val.astype(ref.dtype)

    shape = jax.ShapeDtypeStruct(ya.shape, MXU_DTYPE)
    return pl.pallas_call(
        body, name=name, grid=(n_tok // tm,),
        in_specs=[rows, pl.BlockSpec((tm, D_MODEL), lambda i: (i, 1)), rows, rows, rows],
        out_specs=(rows, rows, rows, rows), out_shape=(shape, shape, shape, shape),
        compiler_params=_params(7 * _nbytes((tm, D_MODEL), F32), 6 * _nbytes((tm, D_MODEL), F32)),
    )(*_hbm(gab, gab, ya, yb, dout))


ADA_SHARD = N_MOD * D_MODEL // N_SHARD


def _ada_fwd(c_pad, w_s, b_s, *, name):
    n_r = c_pad.shape[0]

    def body(c_ref, w_ref, b_ref, o_ref):
        sc = jax.nn.silu(c_ref[...]).astype(MXU_DTYPE)
        o_ref[0] = lax.dot_general(sc, w_ref[0].astype(MXU_DTYPE), _NN, preferred_element_type=F32) + b_ref[0]

    return pl.pallas_call(
        body, name=name, grid=(N_SHARD,),
        in_specs=[pl.BlockSpec((n_r, D_MODEL), lambda s: (0, 0)),
                  pl.BlockSpec((1, D_MODEL, ADA_SHARD), lambda s: (s, 0, 0)),
                  pl.BlockSpec((1, 1, ADA_SHARD), lambda s: (s, 0, 0))],
        out_specs=pl.BlockSpec((1, n_r, ADA_SHARD), lambda s: (s, 0, 0)),
        out_shape=jax.ShapeDtypeStruct((N_SHARD, n_r, ADA_SHARD), F32),
        compiler_params=_params(_nbytes((D_MODEL, ADA_SHARD), w_s.dtype), 1 << 20),
    )(*_hbm(c_pad, w_s, b_s))


def _ada_bwd(c_pad, dmod_s, *, name):
    n_r = c_pad.shape[0]

    def body(c_ref, d_ref, dw_ref, db_ref):
        sc = jax.nn.silu(c_ref[...])
        dm = d_ref[0]
        dw_ref[0] = _fdot(sc, dm, _TN)
        db_ref[0] = jnp.sum(dm, axis=0, keepdims=True)

    return pl.pallas_call(
        body, name=name, grid=(N_SHARD,),
        in_specs=[pl.BlockSpec((n_r, D_MODEL), lambda s: (0, 0)), pl.BlockSpec((1, n_r, ADA_SHARD), lambda s: (s, 0, 0))],
        out_specs=(pl.BlockSpec((1, D_MODEL, ADA_SHARD), lambda s: (s, 0, 0)),
                   pl.BlockSpec((1, 1, ADA_SHARD), lambda s: (s, 0, 0))),
        out_shape=(jax.ShapeDtypeStruct((N_SHARD, D_MODEL, ADA_SHARD), F32),
                   jax.ShapeDtypeStruct((N_SHARD, 1, ADA_SHARD), F32)),
        compiler_params=_params(_nbytes((D_MODEL, ADA_SHARD), F32), 2 * _nbytes((D_MODEL, ADA_SHARD), F32)),
    )(*_hbm(c_pad, dmod_s))


def _block_diag(blocks):
    n_per = S5_GROUPS // S5_BLOCKS
    _, n_r, n_c = blocks.shape
    b4 = blocks.reshape(S5_BLOCKS, n_per, n_r, n_c)
    eye = jnp.eye(n_per, dtype=blocks.dtype)
    return (b4[:, :, :, None, :] * eye[None, :, None, :, None]).reshape(S5_BLOCKS, n_per * n_r, n_per * n_c)


def _diag_blocks(mat, n_r, n_c):
    n_per = S5_GROUPS // S5_BLOCKS
    m5 = mat.reshape(S5_BLOCKS, n_per, n_r, n_per, n_c)
    eye = jnp.eye(n_per, dtype=mat.dtype)
    return jnp.sum(m5 * eye[None, :, None, :, None], axis=3).reshape(S5_GROUPS, n_r, n_c)


def _local_step(x, c, target, wts):
    n_b, n_s, _ = x.shape
    n_tok = n_b * n_s
    flat = lambda t: t.reshape(n_tok, t.shape[-1])
    unflat = lambda t: t.reshape(n_b, n_s, t.shape[-1])
    n_chunks = n_s // CHUNK

    c_pad = jnp.zeros((SUBLANES, D_MODEL), F32).at[:n_b].set(c)
    mod_s = _ada_fwd(c_pad, wts["w_ada"], wts["b_ada"], name="ada_fwd")
    mod = mod_s.transpose(1, 0, 2).reshape(SUBLANES, N_MOD * D_MODEL)[:n_b]
    sh1, sc1, gt1, sh2, sc2, gt2, sh3, sc3, gt3 = [m[:, None, :] for m in jnp.split(mod, N_MOD, axis=-1)]

    a1 = _pre(x, None, None, wts["g_ffn1"], sh1, sc1, 0.0, name="pre1")
    f1, ffn1_saved = _ffn_fwd(flat(a1), wts["w1_ffn1"], wts["w3_ffn1"], wts["w2_ffn1"], "ffn1")
    x1, a2 = _pre(x, unflat(f1), gt1, wts["g_mix"], sh2, sc2, 0.5, name="pre2")
    u = flat(a2)[None]
    p_qkv = _mm_act([(u, wts["w_qkv"])], "nt", name="in_qkv")[0]
    p_z = _mm_act([(u, wts["w_z"])], "nt", name="in_z")[0]
    p_gab = _mm_act([(u, wts["w_gab"])], "nt", name="in_gab")[0]
    p_s5 = _mm_act([(u, wts["w_s5"])], "nt", name="in_s5")[0]
    p_ba = _mm_act([(u, wts["w_ba"])], "nt", name="in_ba")[0]

    qkv_c = _conv_fwd(unflat(p_qkv), wts["conv_qkv"], name="conv_fwd")
    z_tok = unflat(p_z)
    ba = p_ba.reshape(n_b, n_s, BA_PAD)
    head_rows = lambda t: t.transpose(0, 2, 1).reshape(n_b, DN_HEADS, n_chunks, 1, CHUNK)
    blr = head_rows(ba[:, :, :DN_HEADS])
    lar = head_rows(ba[:, :, DN_HEADS:2 * DN_HEADS])
    a_log, dt_bias = wts["a_log"], wts["dt_bias"]
    dn_in = (qkv_c, blr, lar, a_log, dt_bias)
    qd, kd, uc, wc, attn, g_last, dn_inv = _dn_prep(*dn_in, name="dn_prep")
    o, states = _dn_scan(qd, kd, uc, wc, attn, g_last, name="dn_scan")
    og = _dn_post(o, z_tok, wts["g_onorm"], name="dn_post")
    og_t = og.reshape(1, n_tok, DN_WIDTH)
    ya = _mm_act([(og_t, wts["w_proj_a"])], "nn", name="proj_a")[0]

    s5p_in = (wts["lam_re"], wts["lam_im"], wts["log_step"], wts["bt_re"], wts["bt_im"], wts["c_im"])
    lb_re, lb_im, bb_re, bb_im, c_neg = _s5_prep(*s5p_in, name="s5_prep")
    wb_re, wb_im = _block_diag(bb_re), _block_diag(bb_im)
    wc_re = _block_diag(wts["c_re"].transpose(0, 2, 1))
    wc_im = _block_diag(c_neg.transpose(0, 2, 1))
    lbr, lbi = lb_re.reshape(1, S5_LANES), lb_im.reshape(1, S5_LANES)
    s5_w = (wb_re, wb_im, lbr, lbi, wc_re, wc_im)
    ymm, x_re, x_im = _s5_mix(unflat(p_s5), *s5_w, name="s5_mix")
    ymm = ymm.reshape(n_tok, S5_WIDTH)
    y2 = _s5_out(ymm, p_s5, wts["d_skip"], wts["w_glu"], wts["b_glu"], name="s5_out")
    yb = _mm_act([(y2[None], wts["w_proj_b"])], "nn", name="proj_b")[0]

    merged = _merge(p_gab, ya, yb, name="merge")
    m_out = _mm_act([(merged[None], wts["w_out"])], "nn", name="mix_out")[0]
    x2, a3 = _pre(x1, unflat(m_out), gt2, wts["g_ffn2"], sh3, sc3, 1.0, name="pre3")
    f3, ffn2_saved = _ffn_fwd(flat(a3), wts["w1_ffn2"], wts["w3_ffn2"], wts["w2_ffn2"], "ffn2")

    g = {}
    loss, dx2_res, df3, dgt3, g["g_final"] = _final(x2, unflat(f3), gt3, wts["g_final"], target, name="final")
    da3, g["w1_ffn2"], g["w3_ffn2"], g["w2_ffn2"] = _ffn_bwd(
        flat(a3), wts["w1_ffn2"], wts["w3_ffn2"], wts["w2_ffn2"], ffn2_saved, flat(df3), "ffn2")
    dx1_res, dm_out, dgt2, g["g_ffn2"], dsh3, dsc3 = _pre_bwd(
        x1, unflat(m_out), gt2, wts["g_ffn2"], sh3, sc3, 1.0, unflat(da3), dx2_res, name="pre3_bwd")
    dm_out = flat(dm_out)[None]
    dmerged = _mm_act([(dm_out, wts["w_out"])], "nt", name="mix_out_bwd")[0]
    g["w_out"] = _mm_tn(merged[None], dm_out, name="dw_out")[0]
    dga, dgb, dya, dyb = _merge_bwd(p_gab, ya, yb, dmerged, name="merge_bwd")

    dy2 = _mm_act([(dyb[None], wts["w_proj_b"])], "nt", name="proj_b_bwd")[0]
    g["w_proj_b"] = _mm_tn(y2[None], dyb[None], name="dw_proj_b")[0]
    dymm, du_skip, g["d_skip"], g["w_glu"], g["b_glu"] = _s5_out_bwd(
        ymm, p_s5, wts["d_skip"], wts["w_glu"], wts["b_glu"], dy2, name="s5_out_bwd")
    dp_s5, dwb_re, dwb_im, dlb_re, dlb_im, dwc_re, dwc_im = _s5_mix_bwd(
        unflat(dymm), unflat(du_skip), unflat(p_s5), x_re, x_im, *s5_w, name="s5_mix_bwd")
    dp_s5 = dp_s5.reshape(n_tok, S5_WIDTH)
    g["c_re"] = _diag_blocks(dwc_re, S5_STATE, S5_GROUP_CH).transpose(0, 2, 1)
    s5_cts = (dlb_re.reshape(lb_re.shape), dlb_im.reshape(lb_im.shape),
              _diag_blocks(dwb_re, S5_GROUP_CH, S5_STATE), _diag_blocks(dwb_im, S5_GROUP_CH, S5_STATE),
              _diag_blocks(dwc_im, S5_STATE, S5_GROUP_CH).transpose(0, 2, 1))
    g["lam_re"], g["lam_im"], g["log_step"], g["bt_re"], g["bt_im"], g["c_im"] = _s5_prep_bwd(
        *s5p_in, s5_cts, name="s5_prep_bwd")

    dog = _mm_act([(dya[None], wts["w_proj_a"])], "nt", name="proj_a_bwd")[0]
    g["w_proj_a"] = _mm_tn(og_t, dya[None], name="dw_proj_a")[0]
    do, dz, g["g_onorm"] = _dn_post_bwd(o, z_tok, wts["g_onorm"], unflat(dog), name="dn_post_bwd")
    scan_cts = _dn_scan_bwd(qd, kd, uc, wc, attn, g_last, states, do, name="dn_scan_bwd")
    dqkv_c, dblr, dlar, g["a_log"], g["dt_bias"] = _dn_prep_bwd(*dn_in, dn_inv, uc, wc, scan_cts, name="dn_prep_bwd")
    dqkv, g["conv_qkv"] = _conv_bwd(unflat(p_qkv), wts["conv_qkv"], dqkv_c, name="conv_bwd")
    token_cols = lambda t: t.reshape(n_b, DN_HEADS, n_s).transpose(0, 2, 1)
    dba = jnp.concatenate([token_cols(dblr), token_cols(dlar),
                           jnp.zeros((n_b, n_s, BA_PAD - 2 * DN_HEADS), F32)], axis=-1).astype(MXU_DTYPE)

    dps = {"w_qkv": flat(dqkv)[None], "w_z": flat(dz)[None], "w_ga": dga[None], "w_gb": dgb[None],
           "w_s5": dp_s5[None], "w_ba": flat(dba)[None]}
    w_ga, w_gb = wts["w_gab"][:, :D_MODEL], wts["w_gab"][:, D_MODEL:]
    w_of = dict(wts, w_ga=w_ga, w_gb=w_gb)
    du = _mm_act([(dps[k], w_of[k]) for k in dps], "nn", name="in_bwd")[0]
    for k in dps:
        g[k] = _mm_tn(dps[k], u, name=f"d{k}")[0]
    dx0_res, df1, dgt1, g["g_mix"], dsh2, dsc2 = _pre_bwd(
        x, unflat(f1), gt1, wts["g_mix"], sh2, sc2, 0.5, unflat(du), dx1_res, name="pre2_bwd")
    da1, g["w1_ffn1"], g["w3_ffn1"], g["w2_ffn1"] = _ffn_bwd(
        flat(a1), wts["w1_ffn1"], wts["w3_ffn1"], wts["w2_ffn1"], ffn1_saved, flat(df1), "ffn1")
    grad_x, g["g_ffn1"], dsh1, dsc1 = _pre_bwd(
        x, None, None, wts["g_ffn1"], sh1, sc1, 0.0, unflat(da1), dx0_res, name="pre1_bwd")

    dmod = jnp.concatenate([t[:, 0, :] for t in (dsh1, dsc1, dgt1, dsh2, dsc2, dgt2, dsh3, dsc3, dgt3)], axis=-1)
    return loss, grad_x, g, dmod


def _ada_grads(c_rows, dmod_rows):
    n_r = c_rows.shape[0]
    n_pad = -n_r % SUBLANES
    c_pad = jnp.pad(c_rows, ((0, n_pad), (0, 0)))
    dmod_s = jnp.pad(dmod_rows, ((0, n_pad), (0, 0))).reshape(n_r + n_pad, N_SHARD, ADA_SHARD).transpose(1, 0, 2)
    dw, db = _ada_bwd(c_pad, dmod_s, name="ada_bwd")
    return dw, db.reshape(1, N_MOD * D_MODEL)


IN_SPLITS = (("w_qkv", 3 * DN_WIDTH), ("w_z", DN_WIDTH), ("w_ba", 2 * DN_HEADS), ("w_s5", S5_WIDTH),
             ("w_ga", D_MODEL), ("w_gb", D_MODEL))
SHARDED = ("w_ada", "w1_ffn1", "w3_ffn1", "w2_ffn1", "w_in", "conv_qkv", "w_glu", "w_proj_a", "w_proj_b", "w_out",
           "w1_ffn2", "w3_ffn2", "w2_ffn2")


def _cat_columns(stack):
    return stack.transpose(1, 0, 2).reshape(stack.shape[1], N_SHARD * stack.shape[2])


def _split_columns(full):
    n_r, n_c = full.shape
    return full.reshape(n_r, N_SHARD, n_c // N_SHARD).transpose(1, 0, 2)


def _gathered_weights(st, rep):
    w = {k: st[k] for k in ("w_ada", "w1_ffn1", "w3_ffn1", "w2_ffn1", "w1_ffn2", "w3_ffn2", "w2_ffn2")}
    w["b_ada"] = rep["b_ada"].reshape(N_SHARD, 1, ADA_SHARD)
    for k in ("g_ffn1", "g_mix", "g_ffn2", "g_final"):
        w[k] = rep[k].reshape(1, D_MODEL)
    w_in_t = st["w_in"].reshape(N_SHARD * st["w_in"].shape[1], D_MODEL)
    start = 0
    for k, size in IN_SPLITS:
        w[k] = w_in_t[None, start:start + size]
        start += size
    w["w_gab"] = jnp.concatenate([w.pop("w_ga"), w.pop("w_gb")], axis=1)
    w["w_ba"] = jnp.pad(w["w_ba"], ((0, 0), (0, BA_PAD - 2 * DN_HEADS), (0, 0)))
    w["conv_qkv"] = _cat_columns(st["conv_qkv"])
    w["a_log"] = rep["a_log"].reshape(DN_HEADS, 1, 1)
    w["dt_bias"] = rep["dt_bias"].reshape(DN_HEADS, 1, 1)
    w["g_onorm"] = rep["g_onorm"].reshape(1, DN_HEAD_DIM)
    w["lam_re"] = rep["lam_re"].reshape(S5_GROUPS, 1, S5_STATE)
    w["lam_im"] = rep["lam_im"].reshape(S5_GROUPS, 1, S5_STATE)
    w["log_step"] = rep["log_step"].reshape(S5_GROUPS, 1, 1)
    w["bt_re"] = rep["b_re"][0].transpose(0, 2, 1)
    w["bt_im"] = rep["b_im"][0].transpose(0, 2, 1)
    w["c_re"] = rep["c_re"][0]
    w["c_im"] = rep["c_im"][0]
    w["d_skip"] = rep["d_skip"].reshape(1, S5_WIDTH)
    w["b_glu"] = rep["b_glu"].reshape(1, S5_WIDTH)
    w["w_glu"] = st["w_glu"].reshape(S5_WIDTH, S5_WIDTH)
    w["w_proj_a"] = _cat_columns(st["w_proj_a"])[None]
    w["w_proj_b"] = _cat_columns(st["w_proj_b"])[None]
    w["w_out"] = st["w_out"].reshape(1, D_MODEL, D_MODEL)
    return w


def _grads_to_problem_layout(g):
    st = {k: g[k] for k in ("w1_ffn1", "w3_ffn1", "w2_ffn1", "w1_ffn2", "w3_ffn2", "w2_ffn2")}
    w_in_t = jnp.concatenate([g[k][:size] for k, size in IN_SPLITS], axis=0)
    st["w_in"] = w_in_t.reshape(N_SHARD, w_in_t.shape[0] // N_SHARD, D_MODEL)
    st["w_glu"] = g["w_glu"].reshape(N_SHARD, S5_WIDTH // N_SHARD, S5_WIDTH)
    st["w_proj_a"] = _split_columns(g["w_proj_a"])
    st["w_proj_b"] = _split_columns(g["w_proj_b"])
    st["w_out"] = g["w_out"].reshape(N_SHARD, D_MODEL // N_SHARD, D_MODEL)
    small = {
        "g_ffn1": g["g_ffn1"], "g_mix": g["g_mix"], "g_ffn2": g["g_ffn2"], "g_final": g["g_final"].reshape(D_MODEL),
        "conv_qkv": g["conv_qkv"][None],
        "a_log": g["a_log"].reshape(1, DN_HEADS), "dt_bias": g["dt_bias"].reshape(1, DN_HEADS),
        "g_onorm": g["g_onorm"],
        "lam_re": g["lam_re"].reshape(1, S5_GROUPS, S5_STATE), "lam_im": g["lam_im"].reshape(1, S5_GROUPS, S5_STATE),
        "log_step": g["log_step"].reshape(1, S5_GROUPS),
        "b_re": g["bt_re"].transpose(0, 2, 1)[None], "b_im": g["bt_im"].transpose(0, 2, 1)[None],
        "c_re": g["c_re"][None], "c_im": g["c_im"][None],
        "d_skip": g["d_skip"], "b_glu": g["b_glu"],
    }
    return st, small


ELEMENTWISE_BLOCK_BYTES = 1 << 20


def _row_tile(n_rows, n_cols, n_lead=1, multiple=SUBLANES):
    best = None
    for t in range(multiple, n_rows + 1, multiple):
        if n_rows % t == 0 and n_lead * t * n_cols * 4 <= ELEMENTWISE_BLOCK_BYTES:
            best = t
    return best if best is not None else n_rows


def _add_sibling_half(g4, recv, my_c, *, name):
    n_sh, _, n_h, n_c = g4.shape
    th = _row_tile(n_h, n_c, multiple=2 * SUBLANES)

    def body(c_ref, g_ref, r_ref, o_ref):
        o_ref[0] = (g_ref[0, 0] + r_ref[0]).astype(o_ref.dtype)

    grid_spec = pltpu.PrefetchScalarGridSpec(
        num_scalar_prefetch=1, grid=(n_sh, n_h // th),
        in_specs=[pl.BlockSpec((1, 1, th, n_c), lambda s, i, c_ref: (s, c_ref[0], i, 0)),
                  pl.BlockSpec((1, th, n_c), lambda s, i, c_ref: (s, i, 0))],
        out_specs=pl.BlockSpec((1, th, n_c), lambda s, i, c_ref: (s, i, 0)))
    return pl.pallas_call(
        body, name=name, grid_spec=grid_spec, out_shape=jax.ShapeDtypeStruct((n_sh, n_h, n_c), MXU_DTYPE),
        compiler_params=_params(3 * _nbytes((th, n_c), F32)),
    )(*_hbm(my_c, g4, recv))


def _sum_slots(parts, *, name):
    n_p, n_r, n_c = parts.shape
    th = _row_tile(n_r, n_c, n_p)

    def body(p_ref, o_ref):
        total = p_ref[0].astype(F32)
        for k in range(1, n_p):
            total = total + p_ref[k].astype(F32)
        o_ref[...] = total

    return pl.pallas_call(
        body, name=name, grid=(n_r // th,),
        in_specs=[pl.BlockSpec((n_p, th, n_c), lambda i: (0, i, 0))],
        out_specs=pl.BlockSpec((th, n_c), lambda i: (i, 0)),
        out_shape=jax.ShapeDtypeStruct((n_r, n_c), F32),
        compiler_params=_params((n_p + 1) * _nbytes((th, n_c), F32)),
    )(*_hbm(parts))


def _cast_into_slot(w, place, dtype, *, name):
    n_r, n_c = w.shape
    th = _row_tile(n_r, n_c, multiple=2 * SUBLANES)

    def body(p_ref, w_ref, o_ref):
        o_ref[0] = w_ref[...].astype(o_ref.dtype)

    grid_spec = pltpu.PrefetchScalarGridSpec(
        num_scalar_prefetch=1, grid=(n_r // th,),
        in_specs=[pl.BlockSpec((th, n_c), lambda i, p: (i, 0))],
        out_specs=pl.BlockSpec((1, th, n_c), lambda i, p: (p[1], i, 0)))
    return pl.pallas_call(
        body, name=name, grid_spec=grid_spec, out_shape=jax.ShapeDtypeStruct((N_SHARD, n_r, n_c), dtype),
        compiler_params=_params(2 * _nbytes((th, n_c), F32)),
    )(*_hbm(place, w))


def _sum_chips(own, parts, place, *, name):
    n_sh, n_h, n_c = own.shape
    th = _row_tile(n_h, n_c, n_sh, multiple=2 * SUBLANES)

    def body(p_ref, own_ref, a_ref, b_ref, c_ref, o_ref):
        o_ref[0] = ((own_ref[0].astype(F32) + a_ref[0].astype(F32)) + b_ref[0].astype(F32)) + c_ref[0].astype(F32)

    slab = lambda k: pl.BlockSpec((1, th, n_c), lambda i, p, k=k: (p[k], i, 0))
    grid_spec = pltpu.PrefetchScalarGridSpec(
        num_scalar_prefetch=1, grid=(n_h // th,),
        in_specs=[slab(1), slab(2), slab(3), slab(4)], out_specs=slab(0))
    return pl.pallas_call(
        body, name=name, grid_spec=grid_spec, out_shape=jax.ShapeDtypeStruct((2, n_h, n_c), F32),
        compiler_params=_params(5 * _nbytes((th, n_c), F32)),
    )(*_hbm(place, own, parts, parts, parts))


def _adamw(w, g, m, v, *, name):
    n_r, n_c = w.shape
    th = _row_tile(n_r, n_c)
    tc = n_c
    if th == n_r and n_c % LANES == 0:
        tc = max(t for t in range(LANES, n_c + 1, LANES) if n_c % t == 0 and (n_r * t * 4 <= ELEMENTWISE_BLOCK_BYTES or t == LANES))
    bias1 = 1.0 - ADAM_B1 ** ADAM_STEP
    bias2 = 1.0 - ADAM_B2 ** ADAM_STEP

    def body(w_ref, g_ref, m_ref, v_ref, d_ref, mo_ref, vo_ref):
        gv = g_ref[...]
        m_new = ADAM_B1 * m_ref[...] + (1.0 - ADAM_B1) * gv
        v_new = ADAM_B2 * v_ref[...] + (1.0 - ADAM_B2) * jnp.square(gv)
        d_ref[...] = -ADAM_LR * ((m_new / bias1) / (jnp.sqrt(v_new / bias2) + ADAM_EPS) + ADAM_WD * w_ref[...])
        mo_ref[...] = m_new
        vo_ref[...] = v_new

    spec = pl.BlockSpec((th, tc), lambda i, j: (i, j))
    shape = jax.ShapeDtypeStruct((n_r, n_c), F32)
    return pl.pallas_call(
        body, name=name, grid=(n_r // th, n_c // tc), in_specs=[spec] * 4, out_specs=(spec,) * 3, out_shape=(shape,) * 3,
        compiler_params=_params(7 * _nbytes((th, tc), F32)),
    )(*_hbm(w, g, m, v))


CHIP_FLIPS = ((1, 0), (0, 1), (1, 1))
DEVICE_FLIPS = tuple((fx, fy, fc) for fx in (0, 1) for fy in (0, 1) for fc in (0, 1))[1:]


def _exchange(ins, out_shapes, plan, n_local, n_remote, *, name, aliased=False):
    n_in, n_out = len(ins), len(out_shapes)

    def body(*refs):
        in_refs, out_refs = refs[:n_in], refs[n_in:n_in + n_out]
        send_sems, recv_sems, local_sems = refs[n_in + n_out:]
        me = (lax.axis_index("x"), lax.axis_index("y"), lax.axis_index("c"))
        local, remote = plan(in_refs, out_refs, me)
        assert len(local) == n_local and len(remote) == n_remote
        here = [pltpu.make_async_copy(src, dst, local_sems.at[i]) for i, (src, dst) in enumerate(local)]
        for cp in here:
            cp.start()
        sends = [pltpu.make_async_remote_copy(src_ref=src, dst_ref=dst, send_sem=send_sems.at[i], recv_sem=recv_sems.at[i],
                                              device_id=peer, device_id_type=pl.DeviceIdType.MESH)
                 for i, (src, dst, _, peer) in enumerate(remote)]
        for cp in sends:
            cp.start()
        for i, (src, _, landing, peer) in enumerate(remote):
            pltpu.make_async_remote_copy(src_ref=src, dst_ref=landing, send_sem=send_sems.at[i], recv_sem=recv_sems.at[i],
                                         device_id=peer, device_id_type=pl.DeviceIdType.MESH).wait_recv()
        for cp in sends:
            cp.wait_send()
        for cp in here:
            cp.wait()

    any_spec = pl.BlockSpec(memory_space=pl.ANY)
    return pl.pallas_call(
        body, name=name, in_specs=[any_spec] * n_in, out_specs=tuple([any_spec] * n_out), out_shape=tuple(out_shapes),
        scratch_shapes=[pltpu.SemaphoreType.DMA((n_remote,)), pltpu.SemaphoreType.DMA((n_remote,)),
                        pltpu.SemaphoreType.DMA((max(n_local, 1),))],
        input_output_aliases={k: k for k in range(n_in)} if aliased else {},
    )(*ins)


def _gather_shards(stacks, *, name):
    n = len(stacks)
    halved = [a.shape[1] % 64 == 0 for a in stacks]
    unit_rows = [a.shape[1] // 2 if h else a.shape[1] for a, h in zip(stacks, halved)]
    part1_rows = [(r // 32) * 16 if r >= 32 else r for r in unit_rows]
    has_part2 = [p < r for p, r in zip(part1_rows, unit_rows)]
    n_sem = sum(2 + 1 + int(h2) + 3 * int(h) for h2, h in zip(has_part2, halved))

    def body(*refs):
        outs = refs[n:2 * n]
        send_sems, recv_sems = refs[2 * n:]
        x, y, c = lax.axis_index("x"), lax.axis_index("y"), lax.axis_index("c")
        mine, chip_x, chip_y, chip_d = 2 * x + y, 2 * (1 - x) + y, 2 * x + (1 - y), 2 * (1 - x) + (1 - y)
        to_x, to_y, sibling = (1 - x, y, c), (x, 1 - y, c), (x, y, 1 - c)

        def region(k, slot, half, part=None):
            start = half * unit_rows[k] if halved[k] else 0
            size = unit_rows[k]
            if part == 1:
                size = part1_rows[k]
            elif part == 2:
                start, size = start + part1_rows[k], unit_rows[k] - part1_rows[k]
            if not halved[k] and part is None:
                return outs[k].at[slot]
            if halved[k]:
                start = pl.multiple_of(start, 16)
            return outs[k].at[slot, pl.ds(start, size)]

        counter = [0]
        started, pending = [], []

        def send(region_of, peer, landing_of):
            i = counter[0]
            counter[0] += 1
            src = region_of
            cp = pltpu.make_async_remote_copy(src_ref=src, dst_ref=src, send_sem=send_sems.at[i], recv_sem=recv_sems.at[i],
                                              device_id=peer, device_id_type=pl.DeviceIdType.MESH)
            cp.start()
            started.append(cp)
            return pltpu.make_async_remote_copy(src_ref=landing_of, dst_ref=landing_of, send_sem=send_sems.at[i],
                                                recv_sem=recv_sems.at[i], device_id=peer, device_id_type=pl.DeviceIdType.MESH)

        from_x = [send(region(k, mine, c), to_x, region(k, chip_x, c)) for k in range(n)]
        from_y = [send(region(k, mine, c), to_y, region(k, chip_y, c)) for k in range(n)]
        diag = []
        for k in range(n):
            from_x[k].wait_recv()
            fwd = [send(region(k, chip_x, c, 1), to_y, region(k, chip_d, c, 1))]
            if halved[k]:
                pending.append(send(region(k, chip_x, c), sibling, region(k, chip_x, 1 - c)))
            from_y[k].wait_recv()
            if has_part2[k]:
                fwd.append(send(region(k, chip_y, c, 2), to_x, region(k, chip_d, c, 2)))
            if halved[k]:
                pending.append(send(region(k, chip_y, c), sibling, region(k, chip_y, 1 - c)))
            diag.append(fwd)
        for k in range(n):
            for landed in diag[k]:
                landed.wait_recv()
            if halved[k]:
                pending.append(send(region(k, chip_d, c), sibling, region(k, chip_d, 1 - c)))
        for landed in pending:
            landed.wait_recv()
        for cp in started:
            cp.wait_send()
        assert counter[0] == n_sem

    any_spec = pl.BlockSpec(memory_space=pl.ANY)
    return pl.pallas_call(
        body, name=name, in_specs=[any_spec] * n, out_specs=tuple([any_spec] * n),
        out_shape=tuple(jax.ShapeDtypeStruct(a.shape, a.dtype) for a in stacks),
        scratch_shapes=[pltpu.SemaphoreType.DMA((n_sem,)), pltpu.SemaphoreType.DMA((n_sem,))],
        input_output_aliases={k: k for k in range(n)},
    )(*stacks)


def _swap_sibling_halves(g4s, *, name):
    n = len(g4s)

    def plan(in_refs, out_refs, me):
        x, y, c = me
        remote = [(in_refs[k].at[:, 1 - c], out_refs[k], out_refs[k], (x, y, 1 - c)) for k in range(n)]
        return [], remote

    shapes = [jax.ShapeDtypeStruct((a.shape[0],) + a.shape[2:], a.dtype) for a in g4s]
    return _exchange(g4s, shapes, plan, 0, n, name=name)


def _scatter_to_chips(hs, *, name):
    n = len(hs)

    def plan(in_refs, out_refs, me):
        x, y, c = me
        mine = 2 * x + y
        remote = []
        for fx, fy in CHIP_FLIPS:
            px, py = x ^ fx, y ^ fy
            peer = 2 * px + py
            for k in range(n):
                remote.append((in_refs[k].at[peer], out_refs[k].at[mine], out_refs[k].at[peer], (px, py, c)))
        return [], remote

    shapes = [jax.ShapeDtypeStruct(a.shape, a.dtype) for a in hs]
    return _exchange(hs, shapes, plan, 0, len(CHIP_FLIPS) * n, name=name)


def _join_sibling_halves(rs, *, name):
    n = len(rs)

    def plan(in_refs, out_refs, me):
        x, y, c = me
        remote = [(out_refs[k].at[c], out_refs[k].at[c], out_refs[k].at[1 - c], (x, y, 1 - c)) for k in range(n)]
        return [], remote

    shapes = [jax.ShapeDtypeStruct(a.shape, a.dtype) for a in rs]
    return _exchange(rs, shapes, plan, 0, n, name=name, aliased=True)


def _gather_all_devices(packed, *, name):
    def plan(in_refs, out_refs, me):
        x, y, c = me
        mine = 4 * x + 2 * y + c
        remote = []
        for fx, fy, fc in DEVICE_FLIPS:
            px, py, pc = x ^ fx, y ^ fy, c ^ fc
            remote.append((in_refs[0], out_refs[0].at[mine], out_refs[0].at[4 * px + 2 * py + pc], (px, py, pc)))
        return [(in_refs[0], out_refs[0].at[mine])], remote

    shape = jax.ShapeDtypeStruct((2 * N_SHARD,) + packed.shape, packed.dtype)
    return _exchange([packed], [shape], plan, 1, len(DEVICE_FLIPS), name=name)[0]


WEIGHT_NAMES = ("w_ada", "b_ada", "g_ffn1", "w1_ffn1", "w3_ffn1", "w2_ffn1", "g_mix", "w_in", "conv_qkv", "a_log",
                "dt_bias", "g_onorm", "lam_re", "lam_im", "log_step", "b_re", "b_im", "c_re", "c_im", "d_skip", "w_glu",
                "b_glu", "w_proj_a", "w_proj_b", "w_out", "g_ffn2", "w1_ffn2", "w3_ffn2", "w2_ffn2", "g_final")
LARGE = tuple(n for n in SHARDED if n != "conv_qkv")
SMALL = tuple(n for n in WEIGHT_NAMES if n not in LARGE)
REDUCED_LARGE = tuple(n for n in LARGE if n != "w_ada")
REDUCED_SMALL = tuple(n for n in SMALL if n != "b_ada")
PACK_ROW = SUBLANES * LANES


def _pack(arrays):
    flat = jnp.concatenate([a.reshape(-1) for a in arrays])
    n_pad = -flat.shape[0] % PACK_ROW
    return jnp.pad(flat, (0, n_pad)).reshape(-1, LANES)


def _unpack(packed, shapes):
    flat = packed.reshape(-1)
    out, start = [], 0
    for s in shapes:
        size = math.prod(s)
        out.append(flat[start:start + size].reshape(s))
        start += size
    return out


def _unpack_slots(gathered, shapes):
    flat = gathered.reshape(gathered.shape[0], -1)
    out, start = [], 0
    for s in shapes:
        size = math.prod(s)
        out.append(flat[:, start:start + size].reshape((gathered.shape[0],) + tuple(s)))
        start += size
    return out


TRANSPOSED = ("w1_ffn1", "w3_ffn1", "w1_ffn2", "w3_ffn2", "w_in")


def _to_internal(name, a):
    return jnp.swapaxes(a[0], 0, 1) if name in TRANSPOSED else a[0]


def _from_internal(name, a):
    return (jnp.swapaxes(a, 0, 1) if name in TRANSPOSED else a)[None]


def _step(x, c, target, weights, m_in, v_in):
    xi, yi, ci = lax.axis_index("x"), lax.axis_index("y"), lax.axis_index("c")
    my_chip = 2 * xi + yi

    others = [k + (k >= my_chip).astype(jnp.int32) for k in range(N_SHARD - 1)]
    place = jnp.stack([ci, my_chip] + others).astype(jnp.int32)

    slots = [_cast_into_slot(_to_internal(n, weights[n]), place, F32 if n == "conv_qkv" else MXU_DTYPE, name=f"cast_{n}")
             for n in SHARDED]
    stacks = dict(zip(SHARDED, _gather_shards(slots, name="gather_weights")))
    rep = {n: weights[n] for n in WEIGHT_NAMES if n not in SHARDED}
    loss, grad_x, g, dmod = _local_step(x, c, target, _gathered_weights(stacks, rep))
    g_stacks, g_small = _grads_to_problem_layout(g)

    g4s = [g_stacks[n].reshape(N_SHARD, 2, g_stacks[n].shape[1] // 2, g_stacks[n].shape[2]) for n in REDUCED_LARGE]
    from_sibling = _swap_sibling_halves(g4s, name="swap_sibling_halves")
    chip_sums = [_add_sibling_half(a, r, place, name=f"chip_sum_{n}") for n, a, r in zip(REDUCED_LARGE, g4s, from_sibling)]
    from_chips = _scatter_to_chips(chip_sums, name="scatter_to_chips")
    reduced = [_sum_chips(h, p, place, name=f"sum_chips_{n}") for n, h, p in zip(REDUCED_LARGE, chip_sums, from_chips)]
    joined = _join_sibling_halves(reduced, name="join_sibling_halves")
    grads_2d = {n: j.reshape(2 * j.shape[1], j.shape[2]) for n, j in zip(REDUCED_LARGE, joined)}
    grads = {n: _from_internal(n, a) for n, a in grads_2d.items()}

    summed_shapes = [g_small[n].shape for n in REDUCED_SMALL] + [(1, 1)]
    packed = _pack([g_small[n] for n in REDUCED_SMALL] + [loss, c, dmod])
    gathered = _gather_all_devices(packed, name="gather_small")
    *small_grads, loss_sum = _unpack(_sum_slots(gathered, name="sum_small"), summed_shapes)
    grads.update(zip(REDUCED_SMALL, small_grads))
    n_conv = weights["conv_qkv"].shape[-1]
    grads["conv_qkv"] = lax.dynamic_slice_in_dim(grads["conv_qkv"], my_chip * n_conv, n_conv, axis=2)
    n_dev = gathered.shape[0]
    rows_of = lambda t: t.reshape(n_dev * t.shape[1], t.shape[2])
    _, c_all, dmod_all = _unpack_slots(gathered, [(sum(math.prod(s) for s in summed_shapes),), c.shape, dmod.shape])
    dw_ada, grads["b_ada"] = _ada_grads(rows_of(c_all), rows_of(dmod_all))
    grads_2d["w_ada"] = lax.dynamic_index_in_dim(dw_ada, my_chip, axis=0, keepdims=False)
    grads["w_ada"] = grads_2d["w_ada"][None]

    delta, new_m, new_v = {}, {}, {}
    grads_2d["conv_qkv"] = grads["conv_qkv"][0]
    for n in LARGE + ("conv_qkv",):
        outs = _adamw(_to_internal(n, weights[n]), grads_2d[n], _to_internal(n, m_in[n]), _to_internal(n, v_in[n]),
                      name=f"adamw_{n}")
        delta[n], new_m[n], new_v[n] = [_from_internal(n, o) for o in outs]
    packed_names = tuple(n for n in SMALL if n != "conv_qkv")
    shapes = [weights[n].shape for n in packed_names]
    outs = _adamw(*[_pack([d[n] for n in packed_names]) for d in (weights, grads, m_in, v_in)], name="adamw_small")
    for d, o in zip((delta, new_m, new_v), outs):
        d.update(zip(packed_names, _unpack(o, shapes)))
    return (loss_sum.reshape(()), grad_x, *[grads[n] for n in WEIGHT_NAMES], *[delta[n] for n in WEIGHT_NAMES],
            *[new_m[n] for n in WEIGHT_NAMES], *[new_v[n] for n in WEIGHT_NAMES])


def kernel(x, c, w_ada, b_ada, g_ffn1, w1_ffn1, w3_ffn1, w2_ffn1, g_mix, w_in, conv_qkv, a_log, dt_bias, g_onorm, lam_re, lam_im, log_step, b_re, b_im, c_re, c_im, d_skip, w_glu, b_glu, w_proj_a, w_proj_b, w_out, g_ffn2, w1_ffn2, w3_ffn2, w2_ffn2, g_final, loss_target, m_w_ada, m_b_ada, m_g_ffn1, m_w1_ffn1, m_w3_ffn1, m_w2_ffn1, m_g_mix, m_w_in, m_conv_qkv, m_a_log, m_dt_bias, m_g_onorm, m_lam_re, m_lam_im, m_log_step, m_b_re, m_b_im, m_c_re, m_c_im, m_d_skip, m_w_glu, m_b_glu, m_w_proj_a, m_w_proj_b, m_w_out, m_g_ffn2, m_w1_ffn2, m_w3_ffn2, m_w2_ffn2, m_g_final, v_w_ada, v_b_ada, v_g_ffn1, v_w1_ffn1, v_w3_ffn1, v_w2_ffn1, v_g_mix, v_w_in, v_conv_qkv, v_a_log, v_dt_bias, v_g_onorm, v_lam_re, v_lam_im, v_log_step, v_b_re, v_b_im, v_c_re, v_c_im, v_d_skip, v_w_glu, v_b_glu, v_w_proj_a, v_w_proj_b, v_w_out, v_g_ffn2, v_w1_ffn2, v_w3_ffn2, v_w2_ffn2, v_g_final):
    w_vals = (w_ada, b_ada, g_ffn1, w1_ffn1, w3_ffn1, w2_ffn1, g_mix, w_in, conv_qkv, a_log, dt_bias, g_onorm, lam_re, lam_im, log_step, b_re, b_im, c_re, c_im, d_skip, w_glu, b_glu, w_proj_a, w_proj_b, w_out, g_ffn2, w1_ffn2, w3_ffn2, w2_ffn2, g_final)
    m_vals = (m_w_ada, m_b_ada, m_g_ffn1, m_w1_ffn1, m_w3_ffn1, m_w2_ffn1, m_g_mix, m_w_in, m_conv_qkv, m_a_log, m_dt_bias, m_g_onorm, m_lam_re, m_lam_im, m_log_step, m_b_re, m_b_im, m_c_re, m_c_im, m_d_skip, m_w_glu, m_b_glu, m_w_proj_a, m_w_proj_b, m_w_out, m_g_ffn2, m_w1_ffn2, m_w3_ffn2, m_w2_ffn2, m_g_final)
    v_vals = (v_w_ada, v_b_ada, v_g_ffn1, v_w1_ffn1, v_w3_ffn1, v_w2_ffn1, v_g_mix, v_w_in, v_conv_qkv, v_a_log, v_dt_bias, v_g_onorm, v_lam_re, v_lam_im, v_log_step, v_b_re, v_b_im, v_c_re, v_c_im, v_d_skip, v_w_glu, v_b_glu, v_w_proj_a, v_w_proj_b, v_w_out, v_g_ffn2, v_w1_ffn2, v_w3_ffn2, v_w2_ffn2, v_g_final)
    return _step(x, c, loss_target, dict(zip(WEIGHT_NAMES, w_vals)), dict(zip(WEIGHT_NAMES, m_vals)),
                 dict(zip(WEIGHT_NAMES, v_vals)))
```
